```python
import math
import jax, jax.numpy as jnp
from jax import lax
import numpy as np


D_MODEL = 1024
BATCH = 16
SEQ = 4096
DEPTH = 4

CHUNK = 64
N_META = 16
Q_BLOCK = 2 * CHUNK
SSD_CHUNK = 2 * CHUNK
NORM_EPS = 1e-6

ATTN_HEADS = 16
ATTN_HEAD_DIM = 64
D_ATTN = ATTN_HEADS * ATTN_HEAD_DIM
FORGET_BIAS_INIT = 3.0

SSD_HEAD_DIM = 64
D_SSD = D_MODEL
SSD_HEADS = D_SSD // SSD_HEAD_DIM
SSD_GROUPS = 2
SSD_STATE = 128
SSD_CONV = 4
D_XBC = D_SSD + 2 * SSD_GROUPS * SSD_STATE

D_LRU = D_MODEL
LRU_BLOCKS = 16
LRU_BLOCK_DIM = D_LRU // LRU_BLOCKS
LRU_CONV = 4
LRU_C = 8.0

D_FF = 2816

N_BRANCH = 3
IN_SIZES = (D_ATTN, D_ATTN, D_ATTN, ATTN_HEADS, D_SSD, D_XBC, SSD_HEADS, D_LRU, D_LRU, N_BRANCH * D_MODEL)
N_IN = 3 * D_ATTN + ATTN_HEADS + D_SSD + D_XBC + SSD_HEADS + 2 * D_LRU + N_BRANCH * D_MODEL

kernel_name = 'hybrid_fox_ssd_rglru_macaron'


def _in_split_points():
    return [int(v) for v in np.cumsum(IN_SIZES)[:-1]]


def rms_norm(x, g):
    xf = x.astype(jnp.float32)
    y = xf * lax.rsqrt(jnp.mean(xf * xf, axis=-1, keepdims=True) + NORM_EPS)
    return (y * g.astype(jnp.float32)).astype(x.dtype)


def swiglu_ffn(h, w_gate_up, w_down):
    g, u = jnp.split(h @ w_gate_up, 2, axis=-1)
    return (jax.nn.silu(g) * u) @ w_down


def causal_depthwise_conv(x, w, b):
    k_width, c = w.shape
    y = lax.conv_general_dilated(x, w[:, None, :].astype(x.dtype), window_strides=(1,),
                                 padding=[(k_width - 1, 0)],
                                 dimension_numbers=('NWC', 'WIO', 'NWC'),
                                 feature_group_count=c)
    return y + b.astype(y.dtype)


def forgetting_attention(q, k, v, log_f):
    b, L, h, dh = q.shape
    c = jnp.cumsum(log_f, axis=1).transpose(0, 2, 1)
    scale = dh ** -0.5
    outs = []
    for start in range(0, L, Q_BLOCK):
        end = start + Q_BLOCK
        s = jnp.einsum('bqhd,bkhd->bhqk', q[:, start:end], k[:, :end],
                       preferred_element_type=jnp.float32) * scale
        bias = c[:, :, start:end, None] - c[:, :, None, :end]
        visible = jnp.arange(end)[None, :] <= jnp.arange(start, end)[:, None]
        p = jax.nn.softmax(jnp.where(visible, s + bias, -jnp.inf), axis=-1)
        outs.append(jnp.einsum('bhqk,bkhd->bqhd', p.astype(v.dtype), v[:, :end]))
    return jnp.concatenate(outs, axis=1)


def segsum(a):
    t = a.shape[-1]
    a_rep = jnp.broadcast_to(a[..., :, None], a.shape + (t,))
    a_rep = jnp.where(jnp.tril(jnp.ones((t, t), bool), -1), a_rep, 0.0)
    cs = jnp.cumsum(a_rep, axis=-2)
    return jnp.where(jnp.tril(jnp.ones((t, t), bool)), cs, -jnp.inf)


def ssd_chunked_scan(x, dt, a, bm, cm):
    b, L, h, p = x.shape
    g, n = bm.shape[2], bm.shape[3]
    e = h // g
    nc = L // SSD_CHUNK
    q = SSD_CHUNK
    xdt = (x * dt[..., None]).reshape(b, nc, q, g, e, p)
    da = (dt * a).reshape(b, nc, q, g, e).transpose(0, 3, 4, 1, 2)
    bc = bm.reshape(b, nc, q, g, n)
    cc = cm.reshape(b, nc, q, g, n)
    a_cum = jnp.cumsum(da, axis=-1)
    decay_in = jnp.exp(segsum(da))
    cb = jnp.einsum('bclgn,bcsgn->bgcls', cc, bc)
    y_diag = jnp.einsum('bgecls,bcsgep->bclgep', cb[:, :, None] * decay_in, xdt)
    decay_to_end = jnp.exp(a_cum[..., -1:] - a_cum).transpose(0, 3, 4, 1, 2)
    states = jnp.einsum('bclgn,bclgep->bcgepn', bc, xdt * decay_to_end[..., None])
    states = jnp.concatenate([jnp.zeros_like(states[:, :1]), states], axis=1)
    chunk_a = jnp.pad(a_cum[..., -1], [(0, 0), (0, 0), (0, 0), (1, 0)])
    chunk_decay = jnp.exp(segsum(chunk_a))
    prev_states = jnp.einsum('bgezc,bcgepn->bzgepn', chunk_decay, states)[:, :-1]
    decay_from_start = jnp.exp(a_cum).transpose(0, 3, 4, 1, 2)
    y_off = jnp.einsum('bclgn,bcgepn->bclgep', cc, prev_states) * decay_from_start[..., None]
    return (y_diag + y_off).reshape(b, L, h, p)


def mamba2_branch(z, xbc, dt, conv_w, conv_b, dt_bias, a_log, d_skip, norm_w):
    b, L, _ = z.shape
    f32 = jnp.float32
    xbc = jax.nn.silu(causal_depthwise_conv(xbc, conv_w, conv_b)).astype(f32)
    xs, bm, cm = jnp.split(xbc, [D_SSD, D_SSD + SSD_GROUPS * SSD_STATE], axis=-1)
    dt = jax.nn.softplus(dt.astype(f32) + dt_bias.astype(f32))
    a = -jnp.exp(a_log.astype(f32))
    xh = xs.reshape(b, L, SSD_HEADS, SSD_HEAD_DIM)
    y = ssd_chunked_scan(xh, dt, a,
                         bm.reshape(b, L, SSD_GROUPS, SSD_STATE),
                         cm.reshape(b, L, SSD_GROUPS, SSD_STATE))
    y = y + d_skip.astype(f32)[:, None] * xh
    gsz = D_SSD // SSD_GROUPS
    y = y.reshape(b, L, SSD_GROUPS, gsz) * jax.nn.silu(z.astype(f32)).reshape(b, L, SSD_GROUPS, gsz)
    y = y * lax.rsqrt(jnp.mean(y * y, axis=-1, keepdims=True) + NORM_EPS)
    return (y.reshape(b, L, D_SSD) * norm_w.astype(f32)).astype(z.dtype)


def rglru_branch(xr, gate, conv_w, conv_b, w_a, b_a, w_x, b_x, lam):
    b, L, _ = xr.shape
    f32 = jnp.float32
    xc = causal_depthwise_conv(xr, conv_w, conv_b).astype(f32)
    xb = xc.reshape(b, L, LRU_BLOCKS, LRU_BLOCK_DIM)
    r = jax.nn.sigmoid(jnp.einsum('blhi,hij->blhj', xb, w_a.astype(f32)).reshape(b, L, D_LRU) + b_a.astype(f32))
    i = jax.nn.sigmoid(jnp.einsum('blhi,hij->blhj', xb, w_x.astype(f32)).reshape(b, L, D_LRU) + b_x.astype(f32))
    log_a = LRU_C * r * jax.nn.log_sigmoid(lam.astype(f32))
    a = jnp.exp(log_a)
    mult = jnp.sqrt(-jnp.expm1(2.0 * log_a))
    mult = jnp.where(jnp.arange(L)[None, :, None] == 0, 1.0, mult)
    u = mult * (i * xc)

    def combine(left, right):
        a_l, u_l = left
        a_r, u_r = right
        return a_l * a_r, a_r * u_l + u_r

    _, hs = lax.associative_scan(combine, (a, u), axis=1)
    return (hs * jax.nn.gelu(gate.astype(f32))).astype(xr.dtype)


def mixer_block(h, w_in, fox_forget_bias, ssd_conv_w, ssd_conv_b, ssd_dt_bias, ssd_a_log, ssd_d, ssd_norm,
                lru_conv_w, lru_conv_b, lru_w_a, lru_b_a, lru_w_x, lru_b_x, lru_lambda,
                w_branch_attn, w_branch_ssd, w_branch_lru, w_out):
    b, L, _ = h.shape
    proj = h @ w_in
    q, k, v, f_logit, z, xbc, dt, xr, gate_r, merge = jnp.split(proj, _in_split_points(), axis=-1)
    log_f = jax.nn.log_sigmoid(f_logit.astype(jnp.float32) + fox_forget_bias.astype(jnp.float32))
    split_heads = lambda t: t.reshape(b, L, ATTN_HEADS, ATTN_HEAD_DIM)
    y_a = forgetting_attention(split_heads(q), split_heads(k), split_heads(v), log_f).reshape(b, L, D_ATTN)
    y_b = mamba2_branch(z, xbc, dt, ssd_conv_w, ssd_conv_b, ssd_dt_bias, ssd_a_log, ssd_d, ssd_norm)
    y_c = rglru_branch(xr, gate_r, lru_conv_w, lru_conv_b, lru_w_a, lru_b_a, lru_w_x, lru_b_x, lru_lambda)
    g_a, g_b, g_c = jnp.split(jax.nn.sigmoid(merge), N_BRANCH, axis=-1)
    mixed = g_a * (y_a @ w_branch_attn) + g_b * (y_b @ w_branch_ssd) + g_c * (y_c @ w_branch_lru)
    return mixed @ w_out


def _fwd_setup_inputs(seed: int = 0) -> dict:
    key = jax.random.key(seed)
    keys = iter(jax.random.split(key, 40))

    def normal(shape, scale):
        return jax.random.normal(next(keys), shape, jnp.float32) * scale

    def uniform(shape, lo, hi):
        return jax.random.uniform(next(keys), shape, jnp.float32, lo, hi)

    def gain(shape):
        return 1.0 + normal(shape, 0.02)

    x = normal((BATCH, SEQ, D_MODEL), 1.0)
    meta_tokens = normal((N_META, D_MODEL), 1.0)
    ffn1_norm = gain((DEPTH, D_MODEL))
    ffn1_w_gate_up = normal((DEPTH, D_MODEL, 2 * D_FF), D_MODEL ** -0.5)
    ffn1_w_down = normal((DEPTH, D_FF, D_MODEL), D_FF ** -0.5)
    mix_norm = gain((DEPTH, D_MODEL))
    w_in = normal((DEPTH, D_MODEL, N_IN), D_MODEL ** -0.5)
    fox_forget_bias = FORGET_BIAS_INIT + normal((DEPTH, ATTN_HEADS), 0.5)
    ssd_conv_w = normal((DEPTH, SSD_CONV, D_XBC), SSD_CONV ** -0.5)
    ssd_conv_b = normal((DEPTH, D_XBC), 0.02)
    dt0 = jnp.exp(uniform((DEPTH, SSD_HEADS), math.log(1e-3), math.log(1e-1)))
    ssd_dt_bias = dt0 + jnp.log(-jnp.expm1(-dt0))
    ssd_a_log = jnp.log(uniform((DEPTH, SSD_HEADS), 1.0, 16.0))
    ssd_d = gain((DEPTH, SSD_HEADS))
    ssd_norm = gain((DEPTH, D_SSD))
    lru_conv_w = normal((DEPTH, LRU_CONV, D_LRU), LRU_CONV ** -0.5)
    lru_conv_b = normal((DEPTH, D_LRU), 0.02)
    lru_w_a = normal((DEPTH, LRU_BLOCKS, LRU_BLOCK_DIM, LRU_BLOCK_DIM), LRU_BLOCK_DIM ** -0.5)
    lru_b_a = normal((DEPTH, D_LRU), 0.02)
    lru_w_x = normal((DEPTH, LRU_BLOCKS, LRU_BLOCK_DIM, LRU_BLOCK_DIM), LRU_BLOCK_DIM ** -0.5)
    lru_b_x = normal((DEPTH, D_LRU), 0.02)
    a_pow_c = uniform((DEPTH, D_LRU), 0.9, 0.999)
    a0 = a_pow_c ** (1.0 / LRU_C)
    lru_lambda = jnp.log(a0) - jnp.log1p(-a0)
    w_branch_attn = normal((DEPTH, D_ATTN, D_MODEL), D_ATTN ** -0.5)
    w_branch_ssd = normal((DEPTH, D_SSD, D_MODEL), D_SSD ** -0.5)
    w_branch_lru = normal((DEPTH, D_LRU, D_MODEL), D_LRU ** -0.5)
    w_out = normal((DEPTH, D_MODEL, D_MODEL), D_MODEL ** -0.5)
    ffn2_norm = gain((DEPTH, D_MODEL))
    ffn2_w_gate_up = normal((DEPTH, D_MODEL, 2 * D_FF), D_MODEL ** -0.5)
    ffn2_w_down = normal((DEPTH, D_FF, D_MODEL), D_FF ** -0.5)
    final_norm = gain((D_MODEL,))
    return {'x': x, 'meta_tokens': meta_tokens,
            'ffn1_norm': ffn1_norm, 'ffn1_w_gate_up': ffn1_w_gate_up, 'ffn1_w_down': ffn1_w_down,
            'mix_norm': mix_norm, 'w_in': w_in, 'fox_forget_bias': fox_forget_bias,
            'ssd_conv_w': ssd_conv_w, 'ssd_conv_b': ssd_conv_b, 'ssd_dt_bias': ssd_dt_bias,
            'ssd_a_log': ssd_a_log, 'ssd_d': ssd_d, 'ssd_norm': ssd_norm,
            'lru_conv_w': lru_conv_w, 'lru_conv_b': lru_conv_b, 'lru_w_a': lru_w_a, 'lru_b_a': lru_b_a,
            'lru_w_x': lru_w_x, 'lru_b_x': lru_b_x, 'lru_lambda': lru_lambda,
            'w_branch_attn': w_branch_attn, 'w_branch_ssd': w_branch_ssd, 'w_branch_lru': w_branch_lru,
            'w_out': w_out,
            'ffn2_norm': ffn2_norm, 'ffn2_w_gate_up': ffn2_w_gate_up, 'ffn2_w_down': ffn2_w_down,
            'final_norm': final_norm}


def _fwd_reference(x, meta_tokens, ffn1_norm, ffn1_w_gate_up, ffn1_w_down, mix_norm, w_in, fox_forget_bias,
              ssd_conv_w, ssd_conv_b, ssd_dt_bias, ssd_a_log, ssd_d, ssd_norm,
              lru_conv_w, lru_conv_b, lru_w_a, lru_b_a, lru_w_x, lru_b_x, lru_lambda,
              w_branch_attn, w_branch_ssd, w_branch_lru, w_out,
              ffn2_norm, ffn2_w_gate_up, ffn2_w_down, final_norm):
    b, s, d = x.shape
    length = N_META + s
    padded = -(-length // Q_BLOCK) * Q_BLOCK
    meta = jnp.broadcast_to(meta_tokens.astype(x.dtype)[None], (b, N_META, d))
    h = jnp.concatenate([meta, x, jnp.zeros((b, padded - length, d), x.dtype)], axis=1)
    for l in range(DEPTH):
        h = h + 0.5 * swiglu_ffn(rms_norm(h, ffn1_norm[l]), ffn1_w_gate_up[l], ffn1_w_down[l])
        h = h + mixer_block(rms_norm(h, mix_norm[l]), w_in[l], fox_forget_bias[l],
                            ssd_conv_w[l], ssd_conv_b[l], ssd_dt_bias[l], ssd_a_log[l], ssd_d[l], ssd_norm[l],
                            lru_conv_w[l], lru_conv_b[l], lru_w_a[l], lru_b_a[l], lru_w_x[l], lru_b_x[l],
                            lru_lambda[l], w_branch_attn[l], w_branch_ssd[l], w_branch_lru[l], w_out[l])
        h = h + 0.5 * swiglu_ffn(rms_norm(h, ffn2_norm[l]), ffn2_w_gate_up[l], ffn2_w_down[l])
    return rms_norm(h, final_norm)[:, N_META:N_META + s]


import jax as _jax
import jax.numpy as _jnp

TWIN_FORMAT = 'train_step'
FWD_PARAMS = ['x', 'meta_tokens', 'ffn1_norm', 'ffn1_w_gate_up', 'ffn1_w_down', 'mix_norm', 'w_in', 'fox_forget_bias', 'ssd_conv_w', 'ssd_conv_b', 'ssd_dt_bias', 'ssd_a_log', 'ssd_d', 'ssd_norm', 'lru_conv_w', 'lru_conv_b', 'lru_w_a', 'lru_b_a', 'lru_w_x', 'lru_b_x', 'lru_lambda', 'w_branch_attn', 'w_branch_ssd', 'w_branch_lru', 'w_out', 'ffn2_norm', 'ffn2_w_gate_up', 'ffn2_w_down', 'final_norm']
TWIN_WEIGHTS = ['meta_tokens', 'ffn1_norm', 'ffn1_w_gate_up', 'ffn1_w_down', 'mix_norm', 'w_in', 'fox_forget_bias', 'ssd_conv_w', 'ssd_conv_b', 'ssd_dt_bias', 'ssd_a_log', 'ssd_d', 'ssd_norm', 'lru_conv_w', 'lru_conv_b', 'lru_w_a', 'lru_b_a', 'lru_w_x', 'lru_b_x', 'lru_lambda', 'w_branch_attn', 'w_branch_ssd', 'w_branch_lru', 'w_out', 'ffn2_norm', 'ffn2_w_gate_up', 'ffn2_w_down', 'final_norm']
TWIN_DIFF_INPUT = 'x'
TWIN_INPUTS = ['x', 'meta_tokens', 'ffn1_norm', 'ffn1_w_gate_up', 'ffn1_w_down', 'mix_norm', 'w_in', 'fox_forget_bias', 'ssd_conv_w', 'ssd_conv_b', 'ssd_dt_bias', 'ssd_a_log', 'ssd_d', 'ssd_norm', 'lru_conv_w', 'lru_conv_b', 'lru_w_a', 'lru_b_a', 'lru_w_x', 'lru_b_x', 'lru_lambda', 'w_branch_attn', 'w_branch_ssd', 'w_branch_lru', 'w_out', 'ffn2_norm', 'ffn2_w_gate_up', 'ffn2_w_down', 'final_norm', 'loss_target', 'm_meta_tokens', 'm_ffn1_norm', 'm_ffn1_w_gate_up', 'm_ffn1_w_down', 'm_mix_norm', 'm_w_in', 'm_fox_forget_bias', 'm_ssd_conv_w', 'm_ssd_conv_b', 'm_ssd_dt_bias', 'm_ssd_a_log', 'm_ssd_d', 'm_ssd_norm', 'm_lru_conv_w', 'm_lru_conv_b', 'm_lru_w_a', 'm_lru_b_a', 'm_lru_w_x', 'm_lru_b_x', 'm_lru_lambda', 'm_w_branch_attn', 'm_w_branch_ssd', 'm_w_branch_lru', 'm_w_out', 'm_ffn2_norm', 'm_ffn2_w_gate_up', 'm_ffn2_w_down', 'm_final_norm', 'v_meta_tokens', 'v_ffn1_norm', 'v_ffn1_w_gate_up', 'v_ffn1_w_down', 'v_mix_norm', 'v_w_in', 'v_fox_forget_bias', 'v_ssd_conv_w', 'v_ssd_conv_b', 'v_ssd_dt_bias', 'v_ssd_a_log', 'v_ssd_d', 'v_ssd_norm', 'v_lru_conv_w', 'v_lru_conv_b', 'v_lru_w_a', 'v_lru_b_a', 'v_lru_w_x', 'v_lru_b_x', 'v_lru_lambda', 'v_w_branch_attn', 'v_w_branch_ssd', 'v_w_branch_lru', 'v_w_out', 'v_ffn2_norm', 'v_ffn2_w_gate_up', 'v_ffn2_w_down', 'v_final_norm']
TWIN_OUTPUTS = ['loss', 'grad_x', 'grad_meta_tokens', 'grad_ffn1_norm', 'grad_ffn1_w_gate_up', 'grad_ffn1_w_down', 'grad_mix_norm', 'grad_w_in', 'grad_fox_forget_bias', 'grad_ssd_conv_w', 'grad_ssd_conv_b', 'grad_ssd_dt_bias', 'grad_ssd_a_log', 'grad_ssd_d', 'grad_ssd_norm', 'grad_lru_conv_w', 'grad_lru_conv_b', 'grad_lru_w_a', 'grad_lru_b_a', 'grad_lru_w_x', 'grad_lru_b_x', 'grad_lru_lambda', 'grad_w_branch_attn', 'grad_w_branch_ssd', 'grad_w_branch_lru', 'grad_w_out', 'grad_ffn2_norm', 'grad_ffn2_w_gate_up', 'grad_ffn2_w_down', 'grad_final_norm', 'delta_meta_tokens', 'delta_ffn1_norm', 'delta_ffn1_w_gate_up', 'delta_ffn1_w_down', 'delta_mix_norm', 'delta_w_in', 'delta_fox_forget_bias', 'delta_ssd_conv_w', 'delta_ssd_conv_b', 'delta_ssd_dt_bias', 'delta_ssd_a_log', 'delta_ssd_d', 'delta_ssd_norm', 'delta_lru_conv_w', 'delta_lru_conv_b', 'delta_lru_w_a', 'delta_lru_b_a', 'delta_lru_w_x', 'delta_lru_b_x', 'delta_lru_lambda', 'delta_w_branch_attn', 'delta_w_branch_ssd', 'delta_w_branch_lru', 'delta_w_out', 'delta_ffn2_norm', 'delta_ffn2_w_gate_up', 'delta_ffn2_w_down', 'delta_final_norm', 'new_m_meta_tokens', 'new_m_ffn1_norm', 'new_m_ffn1_w_gate_up', 'new_m_ffn1_w_down', 'new_m_mix_norm', 'new_m_w_in', 'new_m_fox_forget_bias', 'new_m_ssd_conv_w', 'new_m_ssd_conv_b', 'new_m_ssd_dt_bias', 'new_m_ssd_a_log', 'new_m_ssd_d', 'new_m_ssd_norm', 'new_m_lru_conv_w', 'new_m_lru_conv_b', 'new_m_lru_w_a', 'new_m_lru_b_a', 'new_m_lru_w_x', 'new_m_lru_b_x', 'new_m_lru_lambda', 'new_m_w_branch_attn', 'new_m_w_branch_ssd', 'new_m_w_branch_lru', 'new_m_w_out', 'new_m_ffn2_norm', 'new_m_ffn2_w_gate_up', 'new_m_ffn2_w_down', 'new_m_final_norm', 'new_v_meta_tokens', 'new_v_ffn1_norm', 'new_v_ffn1_w_gate_up', 'new_v_ffn1_w_down', 'new_v_mix_norm', 'new_v_w_in', 'new_v_fox_forget_bias', 'new_v_ssd_conv_w', 'new_v_ssd_conv_b', 'new_v_ssd_dt_bias', 'new_v_ssd_a_log', 'new_v_ssd_d', 'new_v_ssd_norm', 'new_v_lru_conv_w', 'new_v_lru_conv_b', 'new_v_lru_w_a', 'new_v_lru_b_a', 'new_v_lru_w_x', 'new_v_lru_b_x', 'new_v_lru_lambda', 'new_v_w_branch_attn', 'new_v_w_branch_ssd', 'new_v_w_branch_lru', 'new_v_w_out', 'new_v_ffn2_norm', 'new_v_ffn2_w_gate_up', 'new_v_ffn2_w_down', 'new_v_final_norm']
TWIN_LEAF_KINDS = {'loss': 'loss', 'grad_x': 'grad_x', 'grad_meta_tokens': 'grad_w', 'grad_ffn1_norm': 'grad_w', 'grad_ffn1_w_gate_up': 'grad_w', 'grad_ffn1_w_down': 'grad_w', 'grad_mix_norm': 'grad_w', 'grad_w_in': 'grad_w', 'grad_fox_forget_bias': 'grad_w', 'grad_ssd_conv_w': 'grad_w', 'grad_ssd_conv_b': 'grad_w', 'grad_ssd_dt_bias': 'grad_w', 'grad_ssd_a_log': 'grad_w', 'grad_ssd_d': 'grad_w', 'grad_ssd_norm': 'grad_w', 'grad_lru_conv_w': 'grad_w', 'grad_lru_conv_b': 'grad_w', 'grad_lru_w_a': 'grad_w', 'grad_lru_b_a': 'grad_w', 'grad_lru_w_x': 'grad_w', 'grad_lru_b_x': 'grad_w', 'grad_lru_lambda': 'grad_w', 'grad_w_branch_attn': 'grad_w', 'grad_w_branch_ssd': 'grad_w', 'grad_w_branch_lru': 'grad_w', 'grad_w_out': 'grad_w', 'grad_ffn2_norm': 'grad_w', 'grad_ffn2_w_gate_up': 'grad_w', 'grad_ffn2_w_down': 'grad_w', 'grad_final_norm': 'grad_w', 'delta_meta_tokens': 'delta_w', 'delta_ffn1_norm': 'delta_w', 'delta_ffn1_w_gate_up': 'delta_w', 'delta_ffn1_w_down': 'delta_w', 'delta_mix_norm': 'delta_w', 'delta_w_in': 'delta_w', 'delta_fox_forget_bias': 'delta_w', 'delta_ssd_conv_w': 'delta_w', 'delta_ssd_conv_b': 'delta_w', 'delta_ssd_dt_bias': 'delta_w', 'delta_ssd_a_log': 'delta_w', 'delta_ssd_d': 'delta_w', 'delta_ssd_norm': 'delta_w', 'delta_lru_conv_w': 'delta_w', 'delta_lru_conv_b': 'delta_w', 'delta_lru_w_a': 'delta_w', 'delta_lru_b_a': 'delta_w', 'delta_lru_w_x': 'delta_w', 'delta_lru_b_x': 'delta_w', 'delta_lru_lambda': 'delta_w', 'delta_w_branch_attn': 'delta_w', 'delta_w_branch_ssd': 'delta_w', 'delta_w_branch_lru': 'delta_w', 'delta_w_out': 'delta_w', 'delta_ffn2_norm': 'delta_w', 'delta_ffn2_w_gate_up': 'delta_w', 'delta_ffn2_w_down': 'delta_w', 'delta_final_norm': 'delta_w', 'new_m_meta_tokens': 'new_m', 'new_m_ffn1_norm': 'new_m', 'new_m_ffn1_w_gate_up': 'new_m', 'new_m_ffn1_w_down': 'new_m', 'new_m_mix_norm': 'new_m', 'new_m_w_in': 'new_m', 'new_m_fox_forget_bias': 'new_m', 'new_m_ssd_conv_w': 'new_m', 'new_m_ssd_conv_b': 'new_m', 'new_m_ssd_dt_bias': 'new_m', 'new_m_ssd_a_log': 'new_m', 'new_m_ssd_d': 'new_m', 'new_m_ssd_norm': 'new_m', 'new_m_lru_conv_w': 'new_m', 'new_m_lru_conv_b': 'new_m', 'new_m_lru_w_a': 'new_m', 'new_m_lru_b_a': 'new_m', 'new_m_lru_w_x': 'new_m', 'new_m_lru_b_x': 'new_m', 'new_m_lru_lambda': 'new_m', 'new_m_w_branch_attn': 'new_m', 'new_m_w_branch_ssd': 'new_m', 'new_m_w_branch_lru': 'new_m', 'new_m_w_out': 'new_m', 'new_m_ffn2_norm': 'new_m', 'new_m_ffn2_w_gate_up': 'new_m', 'new_m_ffn2_w_down': 'new_m', 'new_m_final_norm': 'new_m', 'new_v_meta_tokens': 'new_v', 'new_v_ffn1_norm': 'new_v', 'new_v_ffn1_w_gate_up': 'new_v', 'new_v_ffn1_w_down': 'new_v', 'new_v_mix_norm': 'new_v', 'new_v_w_in': 'new_v', 'new_v_fox_forget_bias': 'new_v', 'new_v_ssd_conv_w': 'new_v', 'new_v_ssd_conv_b': 'new_v', 'new_v_ssd_dt_bias': 'new_v', 'new_v_ssd_a_log': 'new_v', 'new_v_ssd_d': 'new_v', 'new_v_ssd_norm': 'new_v', 'new_v_lru_conv_w': 'new_v', 'new_v_lru_conv_b': 'new_v', 'new_v_lru_w_a': 'new_v', 'new_v_lru_b_a': 'new_v', 'new_v_lru_w_x': 'new_v', 'new_v_lru_b_x': 'new_v', 'new_v_lru_lambda': 'new_v', 'new_v_w_branch_attn': 'new_v', 'new_v_w_branch_ssd': 'new_v', 'new_v_w_branch_lru': 'new_v', 'new_v_w_out': 'new_v', 'new_v_ffn2_norm': 'new_v', 'new_v_ffn2_w_gate_up': 'new_v', 'new_v_ffn2_w_down': 'new_v', 'new_v_final_norm': 'new_v'}


def _forward(args):
    return _fwd_reference(*[args[k] for k in FWD_PARAMS])


def _output_shape():
    out = _jax.eval_shape(lambda: _forward(_fwd_setup_inputs(0)))
    return out.shape, out.dtype

N_MICROBATCH = 1
ADAM_LR = 0.001
ADAM_B1 = 0.9
ADAM_B2 = 0.999
ADAM_EPS = 1e-08
ADAM_WD = 0.01
ADAM_STEP = 10
PER_EXAMPLE_BATCH_AXIS = {'x': 0, 'loss_target': 0}
SHARED_INPUTS = []
_WEIGHT_DTYPES = {'meta_tokens': _jnp.float32, 'ffn1_norm': _jnp.float32, 'ffn1_w_gate_up': _jnp.float32, 'ffn1_w_down': _jnp.float32, 'mix_norm': _jnp.float32, 'w_in': _jnp.float32, 'fox_forget_bias': _jnp.float32, 'ssd_conv_w': _jnp.float32, 'ssd_conv_b': _jnp.float32, 'ssd_dt_bias': _jnp.float32, 'ssd_a_log': _jnp.float32, 'ssd_d': _jnp.float32, 'ssd_norm': _jnp.float32, 'lru_conv_w': _jnp.float32, 'lru_conv_b': _jnp.float32, 'lru_w_a': _jnp.float32, 'lru_b_a': _jnp.float32, 'lru_w_x': _jnp.float32, 'lru_b_x': _jnp.float32, 'lru_lambda': _jnp.float32, 'w_branch_attn': _jnp.float32, 'w_branch_ssd': _jnp.float32, 'w_branch_lru': _jnp.float32, 'w_out': _jnp.float32, 'ffn2_norm': _jnp.float32, 'ffn2_w_gate_up': _jnp.float32, 'ffn2_w_down': _jnp.float32, 'final_norm': _jnp.float32}
MOMENT_SCALE = {'meta_tokens': 1.288134e-02, 'ffn1_norm': 1.250508e-01, 'ffn1_w_gate_up': 5.285320e-02, 'ffn1_w_down': 8.629552e-02, 'mix_norm': 2.425087e-01, 'w_in': 6.849619e-02, 'fox_forget_bias': 1.593798e-01, 'ssd_conv_w': 1.088175e-01, 'ssd_conv_b': 1.657415e-01, 'ssd_dt_bias': 2.273967e-01, 'ssd_a_log': 3.327467e-01, 'ssd_d': 6.233178e-01, 'ssd_norm': 1.234731e-01, 'lru_conv_w': 9.917806e-02, 'lru_conv_b': 9.124020e-01, 'lru_w_a': 3.042389e-02, 'lru_b_a': 2.818840e-02, 'lru_w_x': 5.416853e-02, 'lru_b_x': 3.333785e-02, 'lru_lambda': 5.605786e-02, 'w_branch_attn': 4.188188e-02, 'w_branch_ssd': 1.246151e-01, 'w_branch_lru': 1.039335e-01, 'w_out': 1.605465e-01, 'ffn2_norm': 9.119166e-02, 'ffn2_w_gate_up': 3.933573e-02, 'ffn2_w_down': 6.413003e-02, 'final_norm': 6.387303e+01}


def _to_microbatches(a, axis):
    t = _jnp.moveaxis(a, axis, 0)
    t = t.reshape((N_MICROBATCH, t.shape[0] // N_MICROBATCH) + t.shape[1:])
    return _jnp.moveaxis(t, 1, axis + 1)


def setup_inputs(seed: int = 0) -> dict:
    inp = _fwd_setup_inputs(seed)
    key = _jax.random.fold_in(_jax.random.key(seed), 7919)
    shape, _ = _output_shape()
    out = dict(inp)
    out["loss_target"] = _jax.random.normal(_jax.random.fold_in(key, 0), shape, _jnp.float32)
    for i, name in enumerate(TWIN_WEIGHTS):
        w = inp[name].astype(_jnp.float32)
        if MOMENT_SCALE is None:
            s = _jnp.sqrt(_jnp.mean(_jnp.square(w)) + 1e-30)
        else:
            s = MOMENT_SCALE[name]
        km, kv = _jax.random.split(_jax.random.fold_in(key, i + 1))
        out[name] = w
        out["m_" + name] = s * _jax.random.normal(km, w.shape, _jnp.float32)
        out["v_" + name] = (s * s) * _jax.random.uniform(kv, w.shape, _jnp.float32, 0.5, 1.5)
    if N_MICROBATCH > 1:
        for name, axis in PER_EXAMPLE_BATCH_AXIS.items():
            out[name] = _to_microbatches(out[name], axis)
    return {'x': out['x'], 'meta_tokens': out['meta_tokens'], 'ffn1_norm': out['ffn1_norm'], 'ffn1_w_gate_up': out['ffn1_w_gate_up'], 'ffn1_w_down': out['ffn1_w_down'], 'mix_norm': out['mix_norm'], 'w_in': out['w_in'], 'fox_forget_bias': out['fox_forget_bias'], 'ssd_conv_w': out['ssd_conv_w'], 'ssd_conv_b': out['ssd_conv_b'], 'ssd_dt_bias': out['ssd_dt_bias'], 'ssd_a_log': out['ssd_a_log'], 'ssd_d': out['ssd_d'], 'ssd_norm': out['ssd_norm'], 'lru_conv_w': out['lru_conv_w'], 'lru_conv_b': out['lru_conv_b'], 'lru_w_a': out['lru_w_a'], 'lru_b_a': out['lru_b_a'], 'lru_w_x': out['lru_w_x'], 'lru_b_x': out['lru_b_x'], 'lru_lambda': out['lru_lambda'], 'w_branch_attn': out['w_branch_attn'], 'w_branch_ssd': out['w_branch_ssd'], 'w_branch_lru': out['w_branch_lru'], 'w_out': out['w_out'], 'ffn2_norm': out['ffn2_norm'], 'ffn2_w_gate_up': out['ffn2_w_gate_up'], 'ffn2_w_down': out['ffn2_w_down'], 'final_norm': out['final_norm'], 'loss_target': out['loss_target'], 'm_meta_tokens': out['m_meta_tokens'], 'm_ffn1_norm': out['m_ffn1_norm'], 'm_ffn1_w_gate_up': out['m_ffn1_w_gate_up'], 'm_ffn1_w_down': out['m_ffn1_w_down'], 'm_mix_norm': out['m_mix_norm'], 'm_w_in': out['m_w_in'], 'm_fox_forget_bias': out['m_fox_forget_bias'], 'm_ssd_conv_w': out['m_ssd_conv_w'], 'm_ssd_conv_b': out['m_ssd_conv_b'], 'm_ssd_dt_bias': out['m_ssd_dt_bias'], 'm_ssd_a_log': out['m_ssd_a_log'], 'm_ssd_d': out['m_ssd_d'], 'm_ssd_norm': out['m_ssd_norm'], 'm_lru_conv_w': out['m_lru_conv_w'], 'm_lru_conv_b': out['m_lru_conv_b'], 'm_lru_w_a': out['m_lru_w_a'], 'm_lru_b_a': out['m_lru_b_a'], 'm_lru_w_x': out['m_lru_w_x'], 'm_lru_b_x': out['m_lru_b_x'], 'm_lru_lambda': out['m_lru_lambda'], 'm_w_branch_attn': out['m_w_branch_attn'], 'm_w_branch_ssd': out['m_w_branch_ssd'], 'm_w_branch_lru': out['m_w_branch_lru'], 'm_w_out': out['m_w_out'], 'm_ffn2_norm': out['m_ffn2_norm'], 'm_ffn2_w_gate_up': out['m_ffn2_w_gate_up'], 'm_ffn2_w_down': out['m_ffn2_w_down'], 'm_final_norm': out['m_final_norm'], 'v_meta_tokens': out['v_meta_tokens'], 'v_ffn1_norm': out['v_ffn1_norm'], 'v_ffn1_w_gate_up': out['v_ffn1_w_gate_up'], 'v_ffn1_w_down': out['v_ffn1_w_down'], 'v_mix_norm': out['v_mix_norm'], 'v_w_in': out['v_w_in'], 'v_fox_forget_bias': out['v_fox_forget_bias'], 'v_ssd_conv_w': out['v_ssd_conv_w'], 'v_ssd_conv_b': out['v_ssd_conv_b'], 'v_ssd_dt_bias': out['v_ssd_dt_bias'], 'v_ssd_a_log': out['v_ssd_a_log'], 'v_ssd_d': out['v_ssd_d'], 'v_ssd_norm': out['v_ssd_norm'], 'v_lru_conv_w': out['v_lru_conv_w'], 'v_lru_conv_b': out['v_lru_conv_b'], 'v_lru_w_a': out['v_lru_w_a'], 'v_lru_b_a': out['v_lru_b_a'], 'v_lru_w_x': out['v_lru_w_x'], 'v_lru_b_x': out['v_lru_b_x'], 'v_lru_lambda': out['v_lru_lambda'], 'v_w_branch_attn': out['v_w_branch_attn'], 'v_w_branch_ssd': out['v_w_branch_ssd'], 'v_w_branch_lru': out['v_w_branch_lru'], 'v_w_out': out['v_w_out'], 'v_ffn2_norm': out['v_ffn2_norm'], 'v_ffn2_w_gate_up': out['v_ffn2_w_gate_up'], 'v_ffn2_w_down': out['v_ffn2_w_down'], 'v_final_norm': out['v_final_norm']}


def _loss(weights, diff, rest, loss_target):
    with _jax.named_scope("forward"):
        args = {**rest, TWIN_DIFF_INPUT: diff, **{k: w.astype(_WEIGHT_DTYPES[k]) for k, w in weights.items()}}
        y = _forward(args)
    with _jax.named_scope("loss_head"):
        err = _jnp.square(y.astype(_jnp.float32) - loss_target)
        return 0.5 * _jnp.sum(_jnp.mean(err, axis=-1)) if err.ndim else 0.5 * err


def _adamw(w, g, m, v):
    m = ADAM_B1 * m + (1.0 - ADAM_B1) * g
    v = ADAM_B2 * v + (1.0 - ADAM_B2) * _jnp.square(g)
    m_hat = m / (1.0 - ADAM_B1 ** ADAM_STEP)
    v_hat = v / (1.0 - ADAM_B2 ** ADAM_STEP)
    delta = -ADAM_LR * (m_hat / (_jnp.sqrt(v_hat) + ADAM_EPS) + ADAM_WD * w)
    return delta, m, v


def reference(x, meta_tokens, ffn1_norm, ffn1_w_gate_up, ffn1_w_down, mix_norm, w_in, fox_forget_bias, ssd_conv_w, ssd_conv_b, ssd_dt_bias, ssd_a_log, ssd_d, ssd_norm, lru_conv_w, lru_conv_b, lru_w_a, lru_b_a, lru_w_x, lru_b_x, lru_lambda, w_branch_attn, w_branch_ssd, w_branch_lru, w_out, ffn2_norm, ffn2_w_gate_up, ffn2_w_down, final_norm, loss_target, m_meta_tokens, m_ffn1_norm, m_ffn1_w_gate_up, m_ffn1_w_down, m_mix_norm, m_w_in, m_fox_forget_bias, m_ssd_conv_w, m_ssd_conv_b, m_ssd_dt_bias, m_ssd_a_log, m_ssd_d, m_ssd_norm, m_lru_conv_w, m_lru_conv_b, m_lru_w_a, m_lru_b_a, m_lru_w_x, m_lru_b_x, m_lru_lambda, m_w_branch_attn, m_w_branch_ssd, m_w_branch_lru, m_w_out, m_ffn2_norm, m_ffn2_w_gate_up, m_ffn2_w_down, m_final_norm, v_meta_tokens, v_ffn1_norm, v_ffn1_w_gate_up, v_ffn1_w_down, v_mix_norm, v_w_in, v_fox_forget_bias, v_ssd_conv_w, v_ssd_conv_b, v_ssd_dt_bias, v_ssd_a_log, v_ssd_d, v_ssd_norm, v_lru_conv_w, v_lru_conv_b, v_lru_w_a, v_lru_b_a, v_lru_w_x, v_lru_b_x, v_lru_lambda, v_w_branch_attn, v_w_branch_ssd, v_w_branch_lru, v_w_out, v_ffn2_norm, v_ffn2_w_gate_up, v_ffn2_w_down, v_final_norm):
    given = dict(x=x, meta_tokens=meta_tokens, ffn1_norm=ffn1_norm, ffn1_w_gate_up=ffn1_w_gate_up, ffn1_w_down=ffn1_w_down, mix_norm=mix_norm, w_in=w_in, fox_forget_bias=fox_forget_bias, ssd_conv_w=ssd_conv_w, ssd_conv_b=ssd_conv_b, ssd_dt_bias=ssd_dt_bias, ssd_a_log=ssd_a_log, ssd_d=ssd_d, ssd_norm=ssd_norm, lru_conv_w=lru_conv_w, lru_conv_b=lru_conv_b, lru_w_a=lru_w_a, lru_b_a=lru_b_a, lru_w_x=lru_w_x, lru_b_x=lru_b_x, lru_lambda=lru_lambda, w_branch_attn=w_branch_attn, w_branch_ssd=w_branch_ssd, w_branch_lru=w_branch_lru, w_out=w_out, ffn2_norm=ffn2_norm, ffn2_w_gate_up=ffn2_w_gate_up, ffn2_w_down=ffn2_w_down, final_norm=final_norm, loss_target=loss_target, m_meta_tokens=m_meta_tokens, m_ffn1_norm=m_ffn1_norm, m_ffn1_w_gate_up=m_ffn1_w_gate_up, m_ffn1_w_down=m_ffn1_w_down, m_mix_norm=m_mix_norm, m_w_in=m_w_in, m_fox_forget_bias=m_fox_forget_bias, m_ssd_conv_w=m_ssd_conv_w, m_ssd_conv_b=m_ssd_conv_b, m_ssd_dt_bias=m_ssd_dt_bias, m_ssd_a_log=m_ssd_a_log, m_ssd_d=m_ssd_d, m_ssd_norm=m_ssd_norm, m_lru_conv_w=m_lru_conv_w, m_lru_conv_b=m_lru_conv_b, m_lru_w_a=m_lru_w_a, m_lru_b_a=m_lru_b_a, m_lru_w_x=m_lru_w_x, m_lru_b_x=m_lru_b_x, m_lru_lambda=m_lru_lambda, m_w_branch_attn=m_w_branch_attn, m_w_branch_ssd=m_w_branch_ssd, m_w_branch_lru=m_w_branch_lru, m_w_out=m_w_out, m_ffn2_norm=m_ffn2_norm, m_ffn2_w_gate_up=m_ffn2_w_gate_up, m_ffn2_w_down=m_ffn2_w_down, m_final_norm=m_final_norm, v_meta_tokens=v_meta_tokens, v_ffn1_norm=v_ffn1_norm, v_ffn1_w_gate_up=v_ffn1_w_gate_up, v_ffn1_w_down=v_ffn1_w_down, v_mix_norm=v_mix_norm, v_w_in=v_w_in, v_fox_forget_bias=v_fox_forget_bias, v_ssd_conv_w=v_ssd_conv_w, v_ssd_conv_b=v_ssd_conv_b, v_ssd_dt_bias=v_ssd_dt_bias, v_ssd_a_log=v_ssd_a_log, v_ssd_d=v_ssd_d, v_ssd_norm=v_ssd_norm, v_lru_conv_w=v_lru_conv_w, v_lru_conv_b=v_lru_conv_b, v_lru_w_a=v_lru_w_a, v_lru_b_a=v_lru_b_a, v_lru_w_x=v_lru_w_x, v_lru_b_x=v_lru_b_x, v_lru_lambda=v_lru_lambda, v_w_branch_attn=v_w_branch_attn, v_w_branch_ssd=v_w_branch_ssd, v_w_branch_lru=v_w_branch_lru, v_w_out=v_w_out, v_ffn2_norm=v_ffn2_norm, v_ffn2_w_gate_up=v_ffn2_w_gate_up, v_ffn2_w_down=v_ffn2_w_down, v_final_norm=v_final_norm)
    weights = {n: given[n] for n in TWIN_WEIGHTS}
    shared = {n: given[n] for n in SHARED_INPUTS}
    per_example = {n: given[n] for n in ['x']}
    grad_fn = _jax.value_and_grad(_loss, argnums=(0, 1))

    def one_microbatch(ex, loss_target):
        ex = dict(ex)
        diff = ex.pop(TWIN_DIFF_INPUT)
        return grad_fn(weights, diff, {**shared, **ex}, loss_target)

    if N_MICROBATCH == 1:
        loss, (grad_w, grad_x) = one_microbatch(per_example, given["loss_target"])
    else:
        def body(carry, xs):
            loss_sum, grad_sum = carry
            l_k, (gw_k, gx_k) = one_microbatch(xs[0], xs[1])
            with _jax.named_scope("update"):
                return (loss_sum + l_k, _jax.tree.map(_jnp.add, grad_sum, gw_k)), gx_k

        init = (_jnp.zeros((), _jnp.float32), _jax.tree.map(_jnp.zeros_like, weights))
        (loss, grad_w), grad_x = _jax.lax.scan(body, init, (per_example, given["loss_target"]))
    with _jax.named_scope("update"):
        delta_w, new_m, new_v = {}, {}, {}
        for n in TWIN_WEIGHTS:
            delta_w[n], new_m[n], new_v[n] = _adamw(weights[n], grad_w[n], given["m_" + n], given["v_" + n])
    return (loss, grad_x, *[grad_w[n] for n in TWIN_WEIGHTS], *[delta_w[n] for n in TWIN_WEIGHTS],
            *[new_m[n] for n in TWIN_WEIGHTS], *[new_v[n] for n in TWIN_WEIGHTS])
```

```python
import functools
import math

import jax
import jax.numpy as jnp
from jax import lax
from jax.experimental import pallas as pl
from jax.experimental.pallas import tpu as pltpu

F32 = jnp.float32
BF16 = jnp.bfloat16

N_DEV = 8
N_META = 16
Q_BLOCK = 128
NORM_EPS = 1e-6
HEADS = 16
HEAD_DIM = 64
SSD_GROUPS = 2
SSD_STATE = 128
CONV_K = 4
LRU_C = 8.0
ADAM_LR, ADAM_B1, ADAM_B2, ADAM_EPS, ADAM_WD, ADAM_STEP = 0.001, 0.9, 0.999, 1e-08, 0.01, 10

LANES = 128
SUBLANES = 8
VMEM_LIMIT = 56 * 1024 * 1024
NEG = -1e30


def _cparams(sem=None):
    return pltpu.CompilerParams(dimension_semantics=sem, vmem_limit_bytes=VMEM_LIMIT)


def _tile(dim, target, mult=LANES):
    if dim <= target:
        return dim
    best = None
    for t in range(mult, target + 1, mult):
        if dim % t == 0:
            best = t
    assert best is not None, (dim, target)
    return best


def _sigmoid(x):
    return 1.0 / (1.0 + jnp.exp(-x))


def _log1p_exp_neg_abs(x):
    e = jnp.exp(-jnp.abs(x))
    u = 1.0 + e
    return jnp.where(u == 1.0, e, jnp.log(u) * (e / jnp.where(u == 1.0, 1.0, u - 1.0)))


def _log_sigmoid(x):
    return jnp.minimum(x, 0.0) - _log1p_exp_neg_abs(x)


def _softplus(x):
    return jnp.maximum(x, 0.0) + _log1p_exp_neg_abs(x)


def _one_minus_exp(y):
    u = jnp.exp(y)
    safe = jnp.where(u == 1.0, 0.5, u)
    return jnp.where(u == 1.0, -y, (1.0 - u) * y / jnp.log(safe))


def _silu(x):
    return x * _sigmoid(x)


def _dsilu(x):
    s = _sigmoid(x)
    return s * (1.0 + x * (1.0 - s))


_GELU_C = math.sqrt(2.0 / math.pi)


def _gelu(x):
    return 0.5 * x * (1.0 + jnp.tanh(_GELU_C * (x + 0.044715 * x * x * x)))


def _dgelu(x):
    t = jnp.tanh(_GELU_C * (x + 0.044715 * x * x * x))
    return 0.5 * (1.0 + t) + 0.5 * x * (1.0 - t * t) * _GELU_C * (1.0 + 3.0 * 0.044715 * x * x)


def _split3_dot(tri, x):
    hi = x.astype(BF16)
    r1 = x - hi.astype(F32)
    mid = r1.astype(BF16)
    lo = (r1 - mid.astype(F32)).astype(BF16)
    t = tri.astype(BF16)
    d = lambda p: jnp.dot(t, p, preferred_element_type=F32)
    return d(hi) + d(mid) + d(lo)


def _lower_tri(n, strict=False):
    r = lax.broadcasted_iota(jnp.int32, (n, n), 0)
    c = lax.broadcasted_iota(jnp.int32, (n, n), 1)
    return (c < r) if strict else (c <= r)


def _mm(a, b, *, ta=False, tb=False, out_dtype=F32, res=None, scale=None, tm=None, tn=None, tk=None,
        a_off=(0, 0), b_off=(0, 0), dims=None, name):
    if dims is None:
        M, K = (a.shape[1], a.shape[0]) if ta else a.shape
        N = b.shape[0] if tb else b.shape[1]
    else:
        M, N, K = dims
    tm = tm or _tile(M, 768)
    tn = tn or _tile(N, 768)
    tk = tk or (K if K <= 2816 else _tile(K, 1408))
    assert M % tm == 0 and N % tn == 0 and K % tk == 0, (name, M, N, K, tm, tn, tk)
    nk = K // tk
    ca = 0 if ta else 1
    cb = 1 if tb else 0

    def blk(rows, cols, off):
        assert off[0] % rows == 0 and off[1] % cols == 0, (name, off, rows, cols)
        return off[0] // rows, off[1] // cols

    if ta:
        ao = blk(tk, tm, a_off)
        a_spec = pl.BlockSpec((tk, tm), lambda i, j, k: (k + ao[0], i + ao[1]))
    else:
        ao = blk(tm, tk, a_off)
        a_spec = pl.BlockSpec((tm, tk), lambda i, j, k: (i + ao[0], k + ao[1]))
    if tb:
        bo = blk(tn, tk, b_off)
        b_spec = pl.BlockSpec((tn, tk), lambda i, j, k: (j + bo[0], k + bo[1]))
    else:
        bo = blk(tk, tn, b_off)
        b_spec = pl.BlockSpec((tk, tn), lambda i, j, k: (k + bo[0], j + bo[1]))
    o_spec = pl.BlockSpec((tm, tn), lambda i, j, k: (i, j))
    in_specs = [a_spec, b_spec] + ([o_spec] if res is not None else [])
    has_res = res is not None

    def kern(*refs):
        if has_res:
            a_ref, b_ref, r_ref, o_ref = refs[:4]
            scr = refs[4:]
        else:
            a_ref, b_ref, o_ref = refs[:3]
            r_ref = None
            scr = refs[3:]
        p = lax.dot_general(a_ref[...].astype(BF16), b_ref[...].astype(BF16), (((ca,), (cb,)), ((), ())),
                            preferred_element_type=F32)

        def fin(val):
            if scale is not None:
                val = val * scale
            if has_res:
                val = r_ref[...] + val
            o_ref[...] = val.astype(out_dtype)

        if nk == 1:
            fin(p)
        else:
            acc = scr[0]
            k = pl.program_id(2)

            @pl.when(k == 0)
            def _():
                acc[...] = p

            @pl.when(k > 0)
            def _():
                acc[...] += p

            @pl.when(k == nk - 1)
            def _():
                fin(acc[...])

    args = (a, b) + ((res,) if has_res else ())
    return pl.pallas_call(
        kern, name=name, grid=(M // tm, N // tn, nk), in_specs=in_specs, out_specs=o_spec,
        out_shape=jax.ShapeDtypeStruct((M, N), out_dtype),
        scratch_shapes=[pltpu.VMEM((tm, tn), F32)] if nk > 1 else [],
        compiler_params=_cparams(("parallel", "parallel", "arbitrary")),
    )(*args)


def _rows(body, tiled, full, outs, accs, *, tr, name, T):
    assert T % tr == 0
    in_specs = []
    for arr, width, off in tiled:
        assert off % width == 0, (name, off, width)
        in_specs.append(pl.BlockSpec((tr, width), functools.partial(lambda i, o: (i, o), o=off // width)))
    for arr in full:
        in_specs.append(pl.BlockSpec(arr.shape, lambda i: (0, 0)))
    out_specs = [pl.BlockSpec((tr, w), lambda i: (i, 0)) for w, _ in outs]
    out_specs += [pl.BlockSpec(s, lambda i: (0, 0)) for s, _ in accs]
    out_shape = [jax.ShapeDtypeStruct((T, w), d) for w, d in outs] + [jax.ShapeDtypeStruct(s, d) for s, d in accs]
    nt, nf, no = len(tiled), len(full), len(outs)

    def kern(*refs):
        i = pl.program_id(0)
        acc_refs = refs[nt + nf + no:]

        @pl.when(i == 0)
        def _():
            for r in acc_refs:
                r[...] = jnp.zeros(r.shape, r.dtype)

        body(i, refs[:nt], refs[nt:nt + nf], refs[nt + nf:nt + nf + no], acc_refs)

    res = pl.pallas_call(
        kern, name=name, grid=(T // tr,), in_specs=in_specs, out_specs=out_specs, out_shape=out_shape,
        compiler_params=_cparams(("arbitrary",)),
    )(*[t[0] for t in tiled], *full)
    return res


def _colsum(x):
    return jnp.sum(x, axis=0, keepdims=True)


def _norm_fwd(h, g, name):
    T, D = h.shape

    def body(i, t, f, o, a):
        x = t[0][...]
        r = lax.rsqrt(jnp.mean(x * x, axis=-1, keepdims=True) + NORM_EPS)
        o[0][...] = (x * r * f[0][...]).astype(BF16)

    return _rows(body, [(h, D, 0)], [g], [(D, BF16)], [], tr=_tile(T, 768, 8), name=name, T=T)[0]


def _norm_bwd(h, dn, dh, g, name):
    T, D = h.shape

    def body(i, t, f, o, a):
        x, dnv, dhv = t[0][...], t[1][...], t[2][...]
        r = lax.rsqrt(jnp.mean(x * x, axis=-1, keepdims=True) + NORM_EPS)
        xh = x * r
        dng = dnv * f[0][...]
        o[0][...] = dhv + r * (dng - xh * jnp.mean(dng * xh, axis=-1, keepdims=True))
        a[0][...] += _colsum(dnv * xh)

    return _rows(body, [(h, D, 0), (dn, D, 0), (dh, D, 0)], [g], [(D, F32)], [((1, D), F32)],
                 tr=_tile(T, 384, 8), name=name, T=T)


def _swiglu_fwd(gu, name):
    T, F2 = gu.shape
    F = F2 // 2

    def body(i, t, f, o, a):
        o[0][...] = (_silu(t[0][...]) * t[1][...]).astype(BF16)

    return _rows(body, [(gu, F, 0), (gu, F, F)], [], [(F, BF16)], [], tr=_tile(T, 256, 8), name=name, T=T)[0]


def _swiglu_bwd(gu, da, name):
    T, F2 = gu.shape
    F = F2 // 2

    def body(i, t, f, o, a):
        gv, uv, dav = t[0][...], t[1][...], t[2][...]
        o[0][:, :F] = (dav * uv * _dsilu(gv)).astype(BF16)
        o[0][:, F:] = (dav * _silu(gv)).astype(BF16)

    return _rows(body, [(gu, F, 0), (gu, F, F), (da, F, 0)], [], [(F2, BF16)], [], tr=_tile(T, 256, 8),
                 name=name, T=T)[0]


def _merge_fwd(proj, off, pa, pb, pc, name):
    T, D = pa.shape

    def body(i, t, f, o, a):
        o[0][...] = (_sigmoid(t[0][...]) * t[3][...] + _sigmoid(t[1][...]) * t[4][...]
                     + _sigmoid(t[2][...]) * t[5][...]).astype(BF16)

    tiled = [(proj, D, off), (proj, D, off + D), (proj, D, off + 2 * D), (pa, D, 0), (pb, D, 0), (pc, D, 0)]
    return _rows(body, tiled, [], [(D, BF16)], [], tr=_tile(T, 384, 8), name=name, T=T)[0]


def _merge_bwd(dmixed, proj, off, pa, pb, pc, name):
    T, D = pa.shape

    def body(i, t, f, o, a):
        dm = t[0][...]
        for k in range(3):
            g = _sigmoid(t[1 + k][...])
            o[k][...] = (dm * g).astype(BF16)
            o[3][:, k * D:(k + 1) * D] = (dm * t[4 + k][...] * g * (1.0 - g)).astype(BF16)

    tiled = [(dmixed, D, 0), (proj, D, off), (proj, D, off + D), (proj, D, off + 2 * D), (pa, D, 0), (pb, D, 0),
             (pc, D, 0)]
    return _rows(body, tiled, [], [(D, BF16)] * 3 + [(3 * D, BF16)], [], tr=_tile(T, 384, 8), name=name, T=T)


def _gnorm_fwd(y, proj, zoff, nw, name):
    T, D = y.shape
    gs = D // SSD_GROUPS

    def body(i, t, f, o, a):
        s = t[0][...] * _silu(t[1][...])
        for g in range(SSD_GROUPS):
            sg = s[:, g * gs:(g + 1) * gs]
            r = lax.rsqrt(jnp.mean(sg * sg, axis=-1, keepdims=True) + NORM_EPS)
            o[0][:, g * gs:(g + 1) * gs] = (sg * r * f[0][:, g * gs:(g + 1) * gs]).astype(BF16)

    return _rows(body, [(y, D, 0), (proj, D, zoff)], [nw], [(D, BF16)], [], tr=_tile(T, 384, 8), name=name, T=T)[0]


def _gnorm_bwd(dout, y, proj, zoff, nw, name):
    T, D = y.shape
    gs = D // SSD_GROUPS

    def body(i, t, f, o, a):
        dov, yv, zv = t[0][...], t[1][...], t[2][...]
        sz = _silu(zv)
        s = yv * sz
        dsz = _dsilu(zv)
        for g in range(SSD_GROUPS):
            sl = slice(g * gs, (g + 1) * gs)
            sg = s[:, sl]
            r = lax.rsqrt(jnp.mean(sg * sg, axis=-1, keepdims=True) + NORM_EPS)
            sh = sg * r
            dog = dov[:, sl]
            dng = dog * f[0][:, sl]
            ds = r * (dng - sh * jnp.mean(dng * sh, axis=-1, keepdims=True))
            o[0][:, sl] = ds * sz[:, sl]
            o[1][:, sl] = (ds * yv[:, sl] * dsz[:, sl]).astype(BF16)
            a[0][:, sl] += _colsum(dog * sh)

    return _rows(body, [(dout, D, 0), (y, D, 0), (proj, D, zoff)], [nw], [(D, F32), (D, BF16)], [((1, D), F32)],
                 tr=_tile(T, 384, 8), name=name, T=T)


def _loss_bwd(h, tgt, g, seq_len, n_real, name):
    T, D = h.shape
    tr = _tile(seq_len, 384, 8)
    per_seq = seq_len // tr

    def body(i, t, f, o, a):
        x, tg = t[0][...], t[1][...]
        pos = (i % per_seq) * tr + lax.broadcasted_iota(jnp.int32, (tr, 1), 0)
        valid = (pos >= N_META) & (pos < N_META + n_real)
        r = lax.rsqrt(jnp.mean(x * x, axis=-1, keepdims=True) + NORM_EPS)
        xh = x * r
        e = jnp.where(valid, xh * f[0][...] - tg, 0.0)
        a[0][...] += jnp.zeros((1, LANES), F32) + 0.5 * jnp.sum(jnp.sum(e * e, axis=-1, keepdims=True) / D,
                                                              axis=0, keepdims=True)
        dy = e / D
        dng = dy * f[0][...]
        o[0][...] = r * (dng - xh * jnp.mean(dng * xh, axis=-1, keepdims=True))
        a[1][...] += _colsum(dy * xh)

    return _rows(body, [(h, D, 0), (tgt, D, 0)], [g], [(D, F32)], [((1, LANES), F32), ((1, D), F32)], tr=tr,
                 name=name, T=T)


def _lane_is_attn(shape):
    return lax.broadcasted_iota(jnp.int32, shape, len(shape) - 1) < HEADS


def _gate_prep(proj3, col_blk, bias, avec, name):
    B, L, _ = proj3.shape
    Q = Q_BLOCK
    nc = L // Q

    def kern(x_ref, b_ref, a_ref, v_ref, c_ref, carry):
        c = pl.program_id(1)

        @pl.when(c == 0)
        def _():
            carry[...] = jnp.zeros_like(carry)

        x = x_ref[0] + b_ref[...]
        attn = _lane_is_attn(x.shape)
        v = jnp.where(attn, _log_sigmoid(x), _softplus(x))
        w = jnp.where(attn, v, v * a_ref[...])
        cs = _split3_dot(_lower_tri(Q), w) + jnp.where(attn[:1], carry[...], 0.0)
        v_ref[0] = v
        c_ref[0] = cs
        rows = lax.broadcasted_iota(jnp.int32, (Q, 1), 0)
        carry[...] = jnp.sum(jnp.where(rows == Q - 1, cs, 0.0), axis=0, keepdims=True)

    blk = pl.BlockSpec((1, Q, LANES), lambda b, c: (b, c, 0))
    vec = pl.BlockSpec((1, LANES), lambda b, c: (0, 0))
    return pl.pallas_call(
        kern, name=name, grid=(B, nc),
        in_specs=[pl.BlockSpec((1, Q, LANES), lambda b, c: (b, c, col_blk)), vec, vec],
        out_specs=[blk, blk], out_shape=[jax.ShapeDtypeStruct((B, L, LANES), F32)] * 2,
        scratch_shapes=[pltpu.VMEM((1, LANES), F32)],
        compiler_params=_cparams(("parallel", "arbitrary")),
    )(proj3, bias, avec)


def _gate_post(drow, dcol, ddt, proj3, col_blk, vals, bias, avec, name):
    B, L, _ = proj3.shape
    Q = Q_BLOCK
    nc = L // Q

    def kern(dr_ref, dc_ref, dd_ref, x_ref, v_ref, b_ref, a_ref, o_ref, db_ref, da_ref, carry):
        b = pl.program_id(0)
        c = pl.program_id(1)

        @pl.when((b == 0) & (c == 0))
        def _():
            db_ref[...] = jnp.zeros_like(db_ref)
            da_ref[...] = jnp.zeros_like(da_ref)

        @pl.when(c == 0)
        def _():
            carry[...] = jnp.zeros_like(carry)

        x = x_ref[0] + b_ref[...]
        attn = _lane_is_attn(x.shape)
        dcs = dr_ref[0] + dc_ref[0]
        upper = jnp.logical_not(_lower_tri(Q, strict=True))
        rc = _split3_dot(upper, dcs) + jnp.where(attn[:1], carry[...], 0.0)
        rows = lax.broadcasted_iota(jnp.int32, (Q, 1), 0)
        carry[...] = jnp.sum(jnp.where(rows == 0, rc, 0.0), axis=0, keepdims=True)
        dv = jnp.where(attn, rc, dd_ref[0] + rc * a_ref[...])
        dpre = dv * jnp.where(attn, _sigmoid(-x), _sigmoid(x))
        o_ref[0] = dpre.astype(BF16)
        db_ref[...] += _colsum(dpre)
        da_ref[...] += _colsum(jnp.where(attn, 0.0, rc * v_ref[0]))

    rev = pl.BlockSpec((1, Q, LANES), lambda b, c: (b, nc - 1 - c, 0))
    vec = pl.BlockSpec((1, LANES), lambda b, c: (0, 0))
    return pl.pallas_call(
        kern, name=name, grid=(B, nc),
        in_specs=[rev, rev, rev, pl.BlockSpec((1, Q, LANES), lambda b, c: (b, nc - 1 - c, col_blk)), rev, vec, vec],
        out_specs=[rev, vec, vec],
        out_shape=[jax.ShapeDtypeStruct((B, L, LANES), BF16), jax.ShapeDtypeStruct((1, LANES), F32),
                   jax.ShapeDtypeStruct((1, LANES), F32)],
        scratch_shapes=[pltpu.VMEM((1, LANES), F32)],
        compiler_params=_cparams(("arbitrary", "arbitrary")),
    )(drow, dcol, ddt, proj3, vals, bias, avec)


def _lane_col(tile, lane):
    sel = lax.broadcasted_iota(jnp.int32, tile.shape, 1) == lane
    return jnp.sum(jnp.where(sel, tile, 0.0), axis=1, keepdims=True)


def _attn_fwd(q, k, v, cums, cums_t, name):
    B, H, L, Dh = q.shape
    tq = _tile(L, 384)
    tk = Q_BLOCK
    nq = L // tq
    scale = Dh ** -0.5

    def kern(q_ref, k_ref, v_ref, c_ref, ct_ref, o_ref, l_ref):
        h = pl.program_id(1)
        qi = pl.program_id(2)
        qv = q_ref[0, 0]
        ccol = _lane_col(c_ref[0, pl.ds(pl.multiple_of(qi * tq, tq), tq), :], h)
        t_idx = qi * tq + lax.broadcasted_iota(jnp.int32, (tq, tk), 0)
        s_loc = lax.broadcasted_iota(jnp.int32, (tq, tk), 1)

        def body(j, carry):
            m, l, acc = carry
            rows = pl.ds(pl.multiple_of(j * tk, tk), tk)
            kj = k_ref[0, 0, rows, :]
            vj = v_ref[0, 0, rows, :]
            s = lax.dot_general(qv, kj, (((1,), (1,)), ((), ())), preferred_element_type=F32) * scale
            s = s + (ccol - ct_ref[0, 0, pl.ds(j, 1), :])
            s = jnp.where(j * tk + s_loc <= t_idx, s, NEG)
            m_new = jnp.maximum(m, jnp.max(s, axis=1, keepdims=True))
            p = jnp.exp(s - m_new)
            alpha = jnp.exp(m - m_new)
            l = alpha * l + jnp.sum(p, axis=1, keepdims=True)
            acc = alpha * acc + jnp.dot(p.astype(BF16), vj, preferred_element_type=F32)
            return m_new, l, acc

        init = (jnp.full((tq, 1), NEG, F32), jnp.zeros((tq, 1), F32), jnp.zeros((tq, Dh), F32))
        m, l, acc = lax.fori_loop(0, (qi + 1) * (tq // tk), body, init)
        o_ref[0, 0] = acc / l
        l_ref[0, 0] = m + jnp.log(l)

    qspec = pl.BlockSpec((1, 1, tq, Dh), lambda b, h, i: (b, h, i, 0))
    kvspec = pl.BlockSpec((1, 1, L, Dh), lambda b, h, i: (b, h, 0, 0))
    return pl.pallas_call(
        kern, name=name, grid=(B, H, nq),
        in_specs=[qspec, kvspec, kvspec, pl.BlockSpec((1, L, LANES), lambda b, h, i: (b, 0, 0)),
                  pl.BlockSpec((1, 1, L // tk, tk), lambda b, h, i: (b, h, 0, 0))],
        out_specs=[qspec, pl.BlockSpec((1, 1, tq, 1), lambda b, h, i: (b, h, i, 0))],
        out_shape=[jax.ShapeDtypeStruct((B, H, L, Dh), F32), jax.ShapeDtypeStruct((B, H, L, 1), F32)],
        compiler_params=_cparams(("parallel", "parallel", "arbitrary")),
    )(q, k, v, cums, cums_t)


def _attn_bwd(q, k, v, o, do, lse, cums, cums_t, name):
    B, H, L, Dh = q.shape
    tq = _tile(L, 384)
    tk = Q_BLOCK
    nq = L // tq
    nc = L // tk
    scale = Dh ** -0.5

    def kern(q_ref, k_ref, v_ref, o_ref, do_ref, l_ref, c_ref, ct_ref, dq_ref, dk_ref, dv_ref, dc_ref, dcc_ref,
             dk_acc, dv_acc):
        h = pl.program_id(1)
        qi = pl.program_id(2)

        @pl.when(qi == 0)
        def _():
            dk_acc[...] = jnp.zeros_like(dk_acc)
            dv_acc[...] = jnp.zeros_like(dv_acc)
            dc_ref[...] = jnp.zeros_like(dc_ref)

        qv = q_ref[0, 0]
        dov = do_ref[0, 0]
        dob = dov.astype(BF16)
        lsev = l_ref[0, 0]
        dsum = jnp.sum(dov * o_ref[0, 0], axis=1, keepdims=True)
        ccol = _lane_col(c_ref[0, pl.ds(pl.multiple_of(qi * tq, tq), tq), :], h)
        t_idx = qi * tq + lax.broadcasted_iota(jnp.int32, (tq, tk), 0)
        s_loc = lax.broadcasted_iota(jnp.int32, (tq, tk), 1)

        def body(j, carry):
            dq, dcc = carry
            rows = pl.ds(pl.multiple_of(j * tk, tk), tk)
            kj = k_ref[0, 0, rows, :]
            vj = v_ref[0, 0, rows, :]
            s = lax.dot_general(qv, kj, (((1,), (1,)), ((), ())), preferred_element_type=F32) * scale
            s = s + (ccol - ct_ref[0, 0, pl.ds(j, 1), :])
            s = jnp.where(j * tk + s_loc <= t_idx, s, NEG)
            p = jnp.exp(s - lsev)
            dp = lax.dot_general(dob, vj, (((1,), (1,)), ((), ())), preferred_element_type=F32)
            ds = p * (dp - dsum)
            dsb = ds.astype(BF16)
            dv_acc[rows, :] += lax.dot_general(p.astype(BF16), dob, (((0,), (0,)), ((), ())),
                                               preferred_element_type=F32)
            dk_acc[rows, :] += lax.dot_general(dsb, qv, (((0,), (0,)), ((), ())), preferred_element_type=F32) * scale
            dc_ref[0, 0, pl.ds(j, 1), :] -= jnp.sum(ds, axis=0, keepdims=True)
            return (dq + jnp.dot(dsb, kj, preferred_element_type=F32) * scale,
                    dcc + jnp.sum(ds, axis=1, keepdims=True))

        dq, dcc = lax.fori_loop(0, (qi + 1) * (tq // tk), body, (jnp.zeros((tq, Dh), F32), jnp.zeros((tq, 1), F32)))
        dq_ref[0, 0] = dq.astype(BF16)
        dcc_ref[0, 0] = dcc

        @pl.when(qi == nq - 1)
        def _():
            dk_ref[0, 0] = dk_acc[...].astype(BF16)
            dv_ref[0, 0] = dv_acc[...].astype(BF16)

    qspec = pl.BlockSpec((1, 1, tq, Dh), lambda b, h, i: (b, h, i, 0))
    kvspec = pl.BlockSpec((1, 1, L, Dh), lambda b, h, i: (b, h, 0, 0))
    cspec = pl.BlockSpec((1, 1, nc, tk), lambda b, h, i: (b, h, 0, 0))
    colspec = pl.BlockSpec((1, 1, tq, 1), lambda b, h, i: (b, h, i, 0))
    return pl.pallas_call(
        kern, name=name, grid=(B, H, nq),
        in_specs=[qspec, kvspec, kvspec, qspec, qspec, colspec,
                  pl.BlockSpec((1, L, LANES), lambda b, h, i: (b, 0, 0)), cspec],
        out_specs=[qspec, kvspec, kvspec, cspec, colspec],
        out_shape=[jax.ShapeDtypeStruct((B, H, L, Dh), BF16)] * 3 + [jax.ShapeDtypeStruct((B, H, nc, tk), F32),
                                                                    jax.ShapeDtypeStruct((B, H, L, 1), F32)],
        scratch_shapes=[pltpu.VMEM((L, Dh), F32), pltpu.VMEM((L, Dh), F32)],
        compiler_params=_cparams(("parallel", "parallel", "arbitrary")),
    )(q, k, v, o, do, lse, cums, cums_t)


PAD = SUBLANES


def _conv_fwd(xp, w, b, n_silu, name):
    B, Lp, C = xp.shape
    L = Lp - PAD
    TR = _tile(L, 384, 8)

    def kern(x_ref, w_ref, b_ref, o_ref):
        cb = pl.program_id(1)

        def body(i, carry):
            r0 = pl.multiple_of(i * TR, TR)
            ext = x_ref[0, pl.ds(r0, TR + PAD), :]
            acc = jnp.zeros((TR, LANES), F32) + b_ref[...]
            for k in range(CONV_K):
                s = CONV_K - 1 - k
                sh = ext if s == 0 else pltpu.roll(ext, s, 0)
                acc = acc + w_ref[k:k + 1, :] * sh[PAD:PAD + TR]
            o_ref[0, pl.ds(r0, TR), :] = jnp.where(cb < n_silu, _silu(acc), acc)
            return carry

        lax.fori_loop(0, L // TR, body, 0)

    return pl.pallas_call(
        kern, name=name, grid=(B, C // LANES),
        in_specs=[pl.BlockSpec((1, Lp, LANES), lambda b_, c: (b_, 0, c)),
                  pl.BlockSpec((CONV_K, LANES), lambda b_, c: (0, c)), pl.BlockSpec((1, LANES), lambda b_, c: (0, c))],
        out_specs=pl.BlockSpec((1, L, LANES), lambda b_, c: (b_, 0, c)),
        out_shape=jax.ShapeDtypeStruct((B, L, C), F32),
        compiler_params=_cparams(("parallel", "parallel")),
    )(xp, w, b)


def _conv_bwd_pre(xp, du, w, b, n_silu, name):
    B, Lp, C = xp.shape
    L = Lp - PAD
    TR = _tile(L, 384, 8)

    def kern(x_ref, du_ref, w_ref, b_ref, dp_ref, dw_ref):
        cb = pl.program_id(0)

        @pl.when(pl.program_id(1) == 0)
        def _():
            dw_ref[...] = jnp.zeros_like(dw_ref)

        def body(i, carry):
            r0 = pl.multiple_of(i * TR, TR)
            ext = x_ref[0, pl.ds(r0, TR + PAD), :]
            taps = []
            acc = jnp.zeros((TR, LANES), F32) + b_ref[...]
            for k in range(CONV_K):
                s = CONV_K - 1 - k
                sh = ext if s == 0 else pltpu.roll(ext, s, 0)
                taps.append(sh[PAD:PAD + TR])
                acc = acc + w_ref[k:k + 1, :] * taps[-1]
            dv = du_ref[0, pl.ds(r0, TR), :]
            dpre = jnp.where(cb < n_silu, dv * _dsilu(acc), dv)
            dp_ref[0, pl.ds(r0, TR), :] = dpre
            return tuple(c + _colsum(dpre * t) for c, t in zip(carry[:CONV_K], taps)) + (carry[CONV_K] + _colsum(dpre),)

        z = jnp.zeros((1, LANES), F32)
        sums = lax.fori_loop(0, L // TR, body, (z,) * (CONV_K + 1))
        dp_ref[0, pl.ds(L, PAD), :] = jnp.zeros((PAD, LANES), F32)
        for k in range(CONV_K + 1):
            dw_ref[k:k + 1, :] += sums[k]

    return pl.pallas_call(
        kern, name=name, grid=(C // LANES, B),
        in_specs=[pl.BlockSpec((1, Lp, LANES), lambda c, b_: (b_, 0, c)),
                  pl.BlockSpec((1, L, LANES), lambda c, b_: (b_, 0, c)),
                  pl.BlockSpec((CONV_K, LANES), lambda c, b_: (0, c)), pl.BlockSpec((1, LANES), lambda c, b_: (0, c))],
        out_specs=[pl.BlockSpec((1, Lp, LANES), lambda c, b_: (b_, 0, c)),
                   pl.BlockSpec((SUBLANES, LANES), lambda c, b_: (0, c))],
        out_shape=[jax.ShapeDtypeStruct((B, Lp, C), F32), jax.ShapeDtypeStruct((SUBLANES, C), F32)],
        compiler_params=_cparams(("parallel", "arbitrary")),
    )(xp, du, w, b)


def _conv_bwd_in(dpp, w, name):
    B, Lp, C = dpp.shape
    L = Lp - PAD
    TR = _tile(L, 384, 16)

    def kern(d_ref, w_ref, o_ref):
        def body(i, carry):
            r0 = pl.multiple_of(i * TR, TR)
            ext = d_ref[0, pl.ds(r0, TR + PAD), :]
            acc = jnp.zeros((TR, LANES), F32)
            for k in range(CONV_K):
                s = CONV_K - 1 - k
                sh = ext if s == 0 else pltpu.roll(ext, TR + PAD - s, 0)
                acc = acc + w_ref[k:k + 1, :] * sh[0:TR]
            o_ref[0, pl.ds(r0, TR), :] = acc.astype(BF16)
            return carry

        lax.fori_loop(0, L // TR, body, 0)

    return pl.pallas_call(
        kern, name=name, grid=(B, C // LANES),
        in_specs=[pl.BlockSpec((1, Lp, LANES), lambda b_, c: (b_, 0, c)),
                  pl.BlockSpec((CONV_K, LANES), lambda b_, c: (0, c))],
        out_specs=pl.BlockSpec((1, L, LANES), lambda b_, c: (b_, 0, c)),
        out_shape=jax.ShapeDtypeStruct((B, L, C), BF16),
        compiler_params=_cparams(("parallel", "parallel")),
    )(dpp, w)


def _dot_nt(a, b):
    return lax.dot_general(a, b, (((1,), (1,)), ((), ())), preferred_element_type=F32)


def _dot_tn(a, b):
    return lax.dot_general(a, b, (((0,), (0,)), ((), ())), preferred_element_type=F32)


def _dot(a, b):
    return jnp.dot(a, b, preferred_element_type=F32)


def _ssd_specs(L, nc, b_blk, c_blk):
    E = HEADS // SSD_GROUPS
    return [
        pl.BlockSpec((1, 1, L, HEAD_DIM), lambda b, h: (b, h, 0, 0)),
        pl.BlockSpec((1, L, SSD_STATE), lambda b, h: (b, 0, b_blk + h // E)),
        pl.BlockSpec((1, L, SSD_STATE), lambda b, h: (b, 0, c_blk + h // E)),
        pl.BlockSpec((1, L, LANES), lambda b, h: (b, 0, 0)),
        pl.BlockSpec((1, L, LANES), lambda b, h: (b, 0, 0)),
        pl.BlockSpec((1, 1, nc, Q_BLOCK), lambda b, h: (b, HEADS + h, 0, 0)),
        pl.BlockSpec((1, LANES), lambda b, h: (0, 0)),
    ]


def _ssd_chunk(c, S, x_ref, b_ref, c_ref, v_ref, cu_ref, ct_ref, lane):
    Q = Q_BLOCK
    rows = pl.ds(pl.multiple_of(c * Q, Q), Q)
    xc = x_ref[0, 0, rows, :]
    Bb = b_ref[0, rows, :].astype(BF16)
    Cb = c_ref[0, rows, :].astype(BF16)
    dt = _lane_col(v_ref[0, rows, :], lane)
    A = _lane_col(cu_ref[0, rows, :], lane)
    Ar = ct_ref[0, 0, pl.ds(c, 1), :]
    Aend = _lane_col(Ar, Q - 1)
    xdt = xc * dt
    Lm = jnp.exp(jnp.where(_lower_tri(Q), A - Ar, NEG))
    CB = _dot_nt(Cb, Bb)
    e_end = jnp.exp(Aend - A)
    W = xdt * e_end
    S_new = S * jnp.exp(Aend) + _dot_tn(W.astype(BF16), Bb)
    return dict(rows=rows, xc=xc, Bb=Bb, Cb=Cb, dt=dt, A=A, Aend=Aend, xdt=xdt, Lm=Lm, CB=CB, e_end=e_end, W=W,
                S_new=S_new)


def _ssd_fwd(x4, u, b_blk, c_blk, vals, cums, cums_t, dvec, name):
    B, H, L, P = x4.shape
    nc = L // Q_BLOCK

    def kern(x_ref, b_ref, c_ref, v_ref, cu_ref, ct_ref, d_ref, y_ref):
        lane = HEADS + pl.program_id(1)
        dskip = _lane_col(d_ref[...], lane)

        def body(c, S):
            q = _ssd_chunk(c, S, x_ref, b_ref, c_ref, v_ref, cu_ref, ct_ref, lane)
            yd = _dot((q["CB"] * q["Lm"]).astype(BF16), q["xdt"].astype(BF16))
            z = _dot_nt(q["Cb"], S.astype(BF16))
            y_ref[0, 0, q["rows"], :] = yd + z * jnp.exp(q["A"]) + dskip * q["xc"]
            return q["S_new"]

        lax.fori_loop(0, nc, body, jnp.zeros((P, SSD_STATE), F32))

    return pl.pallas_call(
        kern, name=name, grid=(B, H), in_specs=_ssd_specs(L, nc, b_blk, c_blk),
        out_specs=pl.BlockSpec((1, 1, L, P), lambda b, h: (b, h, 0, 0)),
        out_shape=jax.ShapeDtypeStruct((B, H, L, P), F32),
        compiler_params=_cparams(("parallel", "arbitrary")),
    )(x4, u, u, vals, cums, cums_t, dvec)


def _ssd_bwd(x4, u, b_blk, c_blk, vals, cums, cums_t, dvec, dy4, name):
    B, H, L, P = x4.shape
    Q = Q_BLOCK
    nc = L // Q
    N = SSD_STATE
    E = HEADS // SSD_GROUPS

    def kern(x_ref, b_ref, c_ref, v_ref, cu_ref, ct_ref, d_ref, dy_ref,
             dx_ref, dB_ref, dC_ref, ddt_ref, dAc_ref, dAr_ref, dD_ref, s_all):
        b = pl.program_id(0)
        h = pl.program_id(1)
        lane = HEADS + h
        dskip = _lane_col(d_ref[...], lane)
        onehot = (lax.broadcasted_iota(jnp.int32, (1, LANES), 1) == lane).astype(F32)

        @pl.when(h % E == 0)
        def _():
            dB_ref[...] = jnp.zeros_like(dB_ref)
            dC_ref[...] = jnp.zeros_like(dC_ref)

        @pl.when(h == 0)
        def _():
            ddt_ref[...] = jnp.zeros_like(ddt_ref)
            dAc_ref[...] = jnp.zeros_like(dAc_ref)

        @pl.when((b == 0) & (h == 0))
        def _():
            dD_ref[...] = jnp.zeros_like(dD_ref)

        def fwd(c, S):
            s_all[c] = S
            return _ssd_chunk(c, S, x_ref, b_ref, c_ref, v_ref, cu_ref, ct_ref, lane)["S_new"]

        lax.fori_loop(0, nc, fwd, jnp.zeros((P, N), F32))
        last_row = lax.broadcasted_iota(jnp.int32, (Q, 1), 0) == Q - 1

        def bwd(i, carry):
            dS, dD = carry
            c = nc - 1 - i
            S = s_all[c]
            q = _ssd_chunk(c, S, x_ref, b_ref, c_ref, v_ref, cu_ref, ct_ref, lane)
            rows, xc, Bb, Cb, xdt, Lm, CB = q["rows"], q["xc"], q["Bb"], q["Cb"], q["xdt"], q["Lm"], q["CB"]
            eA = jnp.exp(q["A"])
            eAend = jnp.exp(q["Aend"])
            dy = dy_ref[0, 0, rows, :]
            dyb = dy.astype(BF16)
            Sb = S.astype(BF16)
            dD = dD + jnp.sum(jnp.sum(dy * xc, axis=1, keepdims=True), axis=0, keepdims=True)
            dM = _dot_nt(dyb, xdt.astype(BF16))
            dxdt = _dot_tn((CB * Lm).astype(BF16), dyb)
            dCBb = (dM * Lm).astype(BF16)
            G = dM * CB * Lm
            dAc = jnp.sum(G, axis=1, keepdims=True)
            dAr = -jnp.sum(G, axis=0, keepdims=True)
            dC = _dot(dCBb, Bb)
            dBm = _dot_tn(dCBb, Cb)
            z = _dot_nt(Cb, Sb)
            dAc = dAc + jnp.sum(dy * z, axis=1, keepdims=True) * eA
            dzb = (dy * eA).astype(BF16)
            dC = dC + _dot(dzb, Sb)
            dS_in = _dot_tn(dzb, Cb)
            dSb = dS.astype(BF16)
            dW = _dot_nt(Bb, dSb)
            dBm = dBm + _dot(q["W"].astype(BF16), dSb)
            dxdt = dxdt + dW * q["e_end"]
            de = jnp.sum(dW * xdt, axis=1, keepdims=True) * q["e_end"]
            dAend = (jnp.sum(jnp.sum(dS * S, axis=1, keepdims=True), axis=0, keepdims=True) * eAend
                     + jnp.sum(de, axis=0, keepdims=True))
            dAc = dAc - de + jnp.where(last_row, dAend, 0.0)
            dx_ref[0, 0, rows, :] = dskip * dy + dxdt * q["dt"]
            dB_ref[0, 0, rows, :] += dBm
            dC_ref[0, 0, rows, :] += dC
            ddt_ref[0, rows, :] += jnp.sum(dxdt * xc, axis=1, keepdims=True) * onehot
            dAc_ref[0, rows, :] += dAc * onehot
            dAr_ref[0, 0, pl.ds(c, 1), :] = dAr
            return dS * eAend + dS_in, dD

        _, dD = lax.fori_loop(0, nc, bwd, (jnp.zeros((P, N), F32), jnp.zeros((1, 1), F32)))
        dD_ref[...] += dD * onehot

    tm = pl.BlockSpec((1, L, LANES), lambda b, h: (b, 0, 0))
    grp = pl.BlockSpec((1, 1, L, N), lambda b, h: (b, h // E, 0, 0))
    xs = pl.BlockSpec((1, 1, L, P), lambda b, h: (b, h, 0, 0))
    return pl.pallas_call(
        kern, name=name, grid=(B, H), in_specs=_ssd_specs(L, nc, b_blk, c_blk) + [xs],
        out_specs=[xs, grp, grp, tm, tm, pl.BlockSpec((1, 1, nc, Q), lambda b, h: (b, h, 0, 0)),
                   pl.BlockSpec((1, LANES), lambda b, h: (0, 0))],
        out_shape=[jax.ShapeDtypeStruct((B, H, L, P), F32), jax.ShapeDtypeStruct((B, SSD_GROUPS, L, N), F32),
                   jax.ShapeDtypeStruct((B, SSD_GROUPS, L, N), F32), jax.ShapeDtypeStruct((B, L, LANES), F32),
                   jax.ShapeDtypeStruct((B, L, LANES), F32), jax.ShapeDtypeStruct((B, H, nc, Q), F32),
                   jax.ShapeDtypeStruct((1, LANES), F32)],
        scratch_shapes=[pltpu.VMEM((nc, P, N), F32)],
        compiler_params=_cparams(("arbitrary", "arbitrary")),
    )(x4, u, u, vals, cums, cums_t, dvec, dy4)


LRU_TR = 384
LRU_CB = 512


def _lru_gates(xc, ra, ix, p_ref, first):
    r = _sigmoid(ra + p_ref[0:1, :])
    i = _sigmoid(ix + p_ref[1:2, :])
    ls = _log_sigmoid(p_ref[2:3, :])
    log_a = LRU_C * r * ls
    a = jnp.exp(log_a)
    mult0 = jnp.sqrt(_one_minus_exp(2.0 * log_a))
    mult = jnp.where(first, 1.0, mult0)
    return r, i, ls, a, mult0, mult


def _lru_fwd(u, xc_off, ra, ix, proj3, gate_off, pvec, name):
    B, L, D = ra.shape
    TR, CB = _tile(L, LRU_TR, 8), LRU_CB
    nrt = L // TR

    def kern(xc_ref, ra_ref, ix_ref, g_ref, p_ref, y_ref, hs_ref, a_ref, pa_s, pu_s, carry):
        rt = pl.program_id(2)

        @pl.when(rt == 0)
        def _():
            carry[...] = jnp.zeros_like(carry)

        row = lax.broadcasted_iota(jnp.int32, (TR, 1), 0)
        first = (rt == 0) & (row == 0)
        xc = xc_ref[0]
        r, i, ls, a, mult0, mult = _lru_gates(xc, ra_ref[0], ix_ref[0], p_ref, first)
        a_ref[0] = a
        pa, pu = a, mult * (i * xc)
        sub = row % SUBLANES
        for s in (1, 2, 4):
            ok = sub >= s
            pu = jnp.where(ok, pa * pltpu.roll(pu, s, 0) + pu, pu)
            pa = jnp.where(ok, pa * pltpu.roll(pa, s, 0), pa)
        pa_s[...] = pa
        pu_s[...] = pu
        row8 = lax.broadcasted_iota(jnp.int32, (SUBLANES, 1), 0)

        def gbody(g, c):
            r8 = pl.ds(pl.multiple_of(g * SUBLANES, SUBLANES), SUBLANES)
            hg = pa_s[r8, :] * c + pu_s[r8, :]
            hs_ref[0, r8, :] = hg
            return jnp.sum(jnp.where(row8 == SUBLANES - 1, hg, 0.0), axis=0, keepdims=True)

        carry[...] = lax.fori_loop(0, TR // SUBLANES, gbody, carry[...])
        y_ref[0] = (hs_ref[0] * _gelu(g_ref[0])).astype(BF16)

    def win(off):
        assert off % CB == 0
        return pl.BlockSpec((1, TR, CB), functools.partial(lambda b, j, t, o: (b, t, j + o), o=off // CB))

    return pl.pallas_call(
        kern, name=name, grid=(B, D // CB, nrt),
        in_specs=[win(xc_off), win(0), win(0), win(gate_off), pl.BlockSpec((SUBLANES, CB), lambda b, j, t: (0, j))],
        out_specs=[win(0)] * 3,
        out_shape=[jax.ShapeDtypeStruct((B, L, D), BF16), jax.ShapeDtypeStruct((B, L, D), F32),
                   jax.ShapeDtypeStruct((B, L, D), F32)],
        scratch_shapes=[pltpu.VMEM((TR, CB), F32), pltpu.VMEM((TR, CB), F32), pltpu.VMEM((1, CB), F32)],
        compiler_params=_cparams(("parallel", "parallel", "arbitrary")),
    )(u, ra, ix, proj3, pvec)


def _lru_bwd(dy, proj3, gate_off, hs, a, u, xc_off, ra, ix, pvec, name):
    B, L, D = ra.shape
    TR, CB = _tile(L, LRU_TR, 8), LRU_CB
    nrt = L // TR

    def kern(dy_ref, g_ref, hs_ref, hsp_ref, a_ref, an_ref, xc_ref, ra_ref, ix_ref, p_ref,
             dg_ref, dra_ref, dix_ref, dxc_ref, dp_ref, pb_s, pd_s, g_s, carry):
        b = pl.program_id(1)
        rt = pl.program_id(2)
        t = nrt - 1 - rt

        @pl.when((b == 0) & (rt == 0))
        def _():
            dp_ref[...] = jnp.zeros_like(dp_ref)

        @pl.when(rt == 0)
        def _():
            carry[...] = jnp.zeros_like(carry)

        row = lax.broadcasted_iota(jnp.int32, (TR, 1), 0)
        gate, hsv, av, dyv = g_ref[0], hs_ref[0], a_ref[0], dy_ref[0]
        dg_ref[0] = (dyv * hsv * _dgelu(gate)).astype(BF16)
        a_next = jnp.where(t == nrt - 1, 0.0, an_ref[0, 0:1, :])
        pb = jnp.where(row == TR - 1, a_next, pltpu.roll(av, TR - 1, 0))
        pd = dyv * _gelu(gate)
        sub = row % SUBLANES
        for s in (1, 2, 4):
            ok = sub < SUBLANES - s
            pd = jnp.where(ok, pd + pb * pltpu.roll(pd, TR - s, 0), pd)
            pb = jnp.where(ok, pb * pltpu.roll(pb, TR - s, 0), pb)
        pb_s[...] = pb
        pd_s[...] = pd
        row8 = lax.broadcasted_iota(jnp.int32, (SUBLANES, 1), 0)

        def gbody(i, c):
            r8 = pl.ds(pl.multiple_of((TR // SUBLANES - 1 - i) * SUBLANES, SUBLANES), SUBLANES)
            gg = pd_s[r8, :] + pb_s[r8, :] * c
            g_s[r8, :] = gg
            return jnp.sum(jnp.where(row8 == 0, gg, 0.0), axis=0, keepdims=True)

        carry[...] = lax.fori_loop(0, TR // SUBLANES, gbody, carry[...])
        gv = g_s[...]
        h_first = jnp.where(t == 0, 0.0, hsp_ref[0, TR - 1:TR, :])
        hprev = jnp.where(row == 0, h_first, pltpu.roll(hsv, 1, 0))
        first = (t == 0) & (row == 0)
        xc = xc_ref[0]
        r, i, ls, a2, mult0, mult = _lru_gates(xc, ra_ref[0], ix_ref[0], p_ref, first)
        dxc_ref[0] = gv * mult * i
        dlog_a = gv * hprev * av + jnp.where(first, 0.0, gv * i * xc * (-(av * av) / mult0))
        dra = dlog_a * LRU_C * ls * r * (1.0 - r)
        dix = gv * mult * xc * i * (1.0 - i)
        dra_ref[0] = dra.astype(BF16)
        dix_ref[0] = dix.astype(BF16)
        dp_ref[0:1, :] += _colsum(dra)
        dp_ref[1:2, :] += _colsum(dix)
        dp_ref[2:3, :] += _colsum(dlog_a * LRU_C * r) * _sigmoid(-p_ref[2:3, :])

    def win(off, shift=0):
        assert off % CB == 0
        o = off // CB
        return pl.BlockSpec((1, TR, CB), lambda j, b, rt: (b, jnp.clip(nrt - 1 - rt + shift, 0, nrt - 1), j + o))

    return pl.pallas_call(
        kern, name=name, grid=(D // CB, B, nrt),
        in_specs=[win(0), win(gate_off), win(0), win(0, -1), win(0), win(0, 1), win(xc_off), win(0), win(0),
                  pl.BlockSpec((SUBLANES, CB), lambda j, b, rt: (0, j))],
        out_specs=[win(0)] * 4 + [pl.BlockSpec((SUBLANES, CB), lambda j, b, rt: (0, j))],
        out_shape=[jax.ShapeDtypeStruct((B, L, D), BF16)] * 3 + [jax.ShapeDtypeStruct((B, L, D), F32),
                                                                 jax.ShapeDtypeStruct((SUBLANES, D), F32)],
        scratch_shapes=[pltpu.VMEM((TR, CB), F32)] * 3 + [pltpu.VMEM((1, CB), F32)],
        compiler_params=_cparams(("parallel", "arbitrary", "arbitrary")),
    )(dy, proj3, hs, hs, a, a, u, ra, ix, pvec)


def _sum8(parts, name):
    _, R, C = parts.shape
    tr = _tile(R, 1024, 8)

    def kern(p_ref, o_ref):
        acc = p_ref[0]
        for d in range(1, N_DEV):
            acc = acc + p_ref[d]
        o_ref[...] = acc

    return pl.pallas_call(
        kern, name=name, grid=(R // tr,), in_specs=[pl.BlockSpec((N_DEV, tr, C), lambda i: (0, i, 0))],
        out_specs=pl.BlockSpec((tr, C), lambda i: (i, 0)), out_shape=jax.ShapeDtypeStruct((R, C), F32),
        compiler_params=_cparams(("parallel",)),
    )(parts)


def _adamw(w, g, m, v, name):
    shape = w.shape
    C = shape[-1] if w.ndim > 1 else shape[0]
    R = w.size // C
    w2, g2, m2, v2 = (t.reshape(R, C) for t in (w, g, m, v))
    tr = R
    for cand in range(8, min(R, 512) + 1, 8):
        if R % cand == 0:
            tr = cand

    def kern(w_ref, g_ref, m_ref, v_ref, d_ref, nm_ref, nv_ref):
        gv = g_ref[...]
        nm = ADAM_B1 * m_ref[...] + (1.0 - ADAM_B1) * gv
        nv = ADAM_B2 * v_ref[...] + (1.0 - ADAM_B2) * (gv * gv)
        m_hat = nm / (1.0 - ADAM_B1 ** ADAM_STEP)
        v_hat = nv / (1.0 - ADAM_B2 ** ADAM_STEP)
        d_ref[...] = -ADAM_LR * (m_hat / (jnp.sqrt(v_hat) + ADAM_EPS) + ADAM_WD * w_ref[...])
        nm_ref[...] = nm
        nv_ref[...] = nv

    spec = pl.BlockSpec((tr, C), lambda i: (i, 0))
    outs = pl.pallas_call(
        kern, name=name, grid=(R // tr,), in_specs=[spec] * 4, out_specs=[spec] * 3,
        out_shape=[jax.ShapeDtypeStruct((R, C), F32)] * 3, compiler_params=_cparams(("parallel",)),
    )(w2, g2, m2, v2)
    return tuple(o.reshape(shape) for o in outs)


MESH_ID = pl.DeviceIdType.MESH
ANY = pl.BlockSpec(memory_space=pl.ANY)


def _my_place():
    return lax.axis_index("x"), lax.axis_index("y"), lax.axis_index("c")


def _all_gather(xs, name):
    R, C = xs.shape

    def body(x_ref, out_ref, send_sems, recv_sems, local_sem):
        x, y, c = _my_place()
        me, sibling = (x, y, c), (x, y, 1 - c)
        chips = [(1 - x, y), (x, 1 - y), (1 - x, 1 - y)]

        def slab(px, py, pc):
            return out_ref.at[4 * px + 2 * py + pc]

        def copy(k, block, to, src=None):
            return pltpu.make_async_remote_copy(
                src_ref=slab(*block) if src is None else src, dst_ref=slab(*block),
                send_sem=send_sems.at[k], recv_sem=recv_sems.at[k], device_id=to, device_id_type=MESH_ID)

        mine = pltpu.make_async_copy(x_ref, slab(*me), local_sem)
        mine.start()
        first = [copy(0, me, sibling, src=x_ref)]
        first += [copy(1 + j, me, (*chip, c), src=x_ref) for j, chip in enumerate(chips)]
        for cp in first:
            cp.start()
        passed = [copy(4 + j, (*chip, c), sibling) for j, chip in enumerate(chips)]
        for j, chip in enumerate(chips):
            copy(1 + j, (*chip, c), me).wait_recv()
            passed[j].start()
        copy(0, sibling, me).wait_recv()
        for j, chip in enumerate(chips):
            copy(4 + j, (*chip, 1 - c), me).wait_recv()
        for cp in first + passed:
            cp.wait_send()
        mine.wait()

    return pl.pallas_call(
        body, name=name, out_shape=jax.ShapeDtypeStruct((N_DEV, R, C), xs.dtype), in_specs=[ANY], out_specs=ANY,
        scratch_shapes=[pltpu.SemaphoreType.DMA((7,)), pltpu.SemaphoreType.DMA((7,)), pltpu.SemaphoreType.DMA],
    )(xs)


def _exchange(parts, name):
    _, R, C = parts.shape

    def body(p_ref, out_ref, send_sems, recv_sems, local_sem):
        x, y, c = _my_place()
        my_idx = 4 * x + 2 * y + c
        mine = pltpu.make_async_copy(p_ref.at[my_idx], out_ref.at[my_idx], local_sem)
        mine.start()
        copies = []
        for k in range(1, N_DEV):
            px, py, pc = x ^ (k >> 2), y ^ ((k >> 1) & 1), c ^ (k & 1)
            copies.append(pltpu.make_async_remote_copy(
                src_ref=p_ref.at[4 * px + 2 * py + pc], dst_ref=out_ref.at[my_idx],
                send_sem=send_sems.at[k - 1], recv_sem=recv_sems.at[k - 1], device_id=(px, py, pc),
                device_id_type=MESH_ID))
        for cp in copies:
            cp.start()
        for cp in copies:
            cp.wait()
        mine.wait()

    return pl.pallas_call(
        body, name=name, out_shape=jax.ShapeDtypeStruct((N_DEV, R, C), parts.dtype), in_specs=[ANY], out_specs=ANY,
        scratch_shapes=[pltpu.SemaphoreType.DMA((7,)), pltpu.SemaphoreType.DMA((7,)), pltpu.SemaphoreType.DMA],
    )(parts)


D_XBC_EXTRA = 2 * SSD_GROUPS * SSD_STATE
SMALL_W = LANES


def _layout(D):
    d_xbc = D + D_XBC_EXTRA
    off = dict(qkv=0, z=3 * D, merge=4 * D, gate=7 * D, conv=8 * D, xr=8 * D + d_xbc, small=9 * D + d_xbc)
    off["n_all"] = off["small"] + SMALL_W
    off["d_xbc"] = d_xbc
    off["conv_c"] = d_xbc + D
    return off


def _w_in_segments(D):
    d_xbc = D + D_XBC_EXTRA
    names = [("q", D), ("k", D), ("v", D), ("f", HEADS), ("z", D), ("xbc", d_xbc), ("dt", HEADS), ("xr", D),
             ("gate", D), ("merge", 3 * D)]
    segs, o = {}, 0
    for n, w in names:
        segs[n] = (o, w)
        o += w
    return segs


def _reorder_w_in(w):
    D = w.shape[0]
    s = _w_in_segments(D)
    cut = lambda n: w[:, s[n][0]:s[n][0] + s[n][1]]
    pad = jnp.zeros((D, SMALL_W - 2 * HEADS), w.dtype)
    return jnp.concatenate([cut("q"), cut("k"), cut("v"), cut("z"), cut("merge"), cut("gate"), cut("xbc"), cut("xr"),
                            cut("f"), cut("dt"), pad], axis=1)


def _restore_w_in(dw):
    D = dw.shape[0]
    lo = _layout(D)
    c = lambda a, w: dw[:, a:a + w]
    return jnp.concatenate([c(0, 3 * D), c(lo["small"], HEADS), c(lo["z"], D), c(lo["conv"], lo["d_xbc"]),
                            c(lo["small"] + HEADS, HEADS), c(lo["xr"], D), c(lo["gate"], D), c(lo["merge"], 3 * D)],
                           axis=1)


def _block_diag(w):
    H, n, _ = w.shape
    eye = jnp.eye(H, dtype=w.dtype)
    return (eye[:, None, :, None] * w[:, :, None, :]).reshape(H * n, H * n)


def _diag_blocks(m, H):
    n = m.shape[0] // H
    m4 = m.reshape(H, n, H, n)
    idx = jnp.arange(H)
    return m4[idx, :, idx, :]


def _to_heads(t, B, L):
    return t.reshape(B, L, HEADS, HEAD_DIM).transpose(0, 2, 1, 3)


def _from_heads(t4):
    B, H, L, P = t4.shape
    return t4.transpose(0, 2, 1, 3).reshape(B * L, H * P)


def _rows_to_tm(rows):
    B, H, nc, Q = rows.shape
    return rows.reshape(B, H, nc * Q).transpose(0, 2, 1)


def _ffn_fwd(h, g, wgu, wd, tag):
    n = _norm_fwd(h, g, tag + "_norm")
    gu = _mm(n, wgu, name=tag + "_up")
    act = _swiglu_fwd(gu, tag + "_act")
    out = _mm(act, wd, res=h, scale=0.5, name=tag + "_down")
    return out, (h, n, gu, act)


def _ffn_bwd(dh, saved, g, wgu, wd, tag):
    h, n, gu, act = saved
    dact = _mm(dh, wd, tb=True, scale=0.5, name=tag + "_down_dx")
    dwd = _mm(act, dh, ta=True, scale=0.5, name=tag + "_down_dw")
    dgu = _swiglu_bwd(gu, dact, tag + "_act_bwd")
    dwgu = _mm(n, dgu, ta=True, tm=1024, name=tag + "_up_dw")
    dn = _mm(dgu, wgu, tb=True, name=tag + "_up_dx")
    dh_in, dg = _norm_bwd(h, dn, dh, g, tag + "_norm_bwd")
    return dh_in, dict(norm=dg, gu=dwgu, down=dwd)


def _mixer_fwd(h, p, B, L):
    T, D = h.shape
    lo = _layout(D)
    n = _norm_fwd(h, p["gm"], "mix_norm")
    proj = _mm(n, p["w_all"], name="mix_in")
    proj3 = proj.reshape(B, L, lo["n_all"])
    qkv = proj[:, :3 * D].astype(BF16).reshape(B, L, 3, HEADS, HEAD_DIM).transpose(2, 0, 3, 1, 4)
    vals, cums = _gate_prep(proj3, lo["small"] // LANES, p["small_bias"], p["avec"], "gate_prep")
    cums_t = cums[..., :2 * HEADS].transpose(0, 2, 1).reshape(B, 2 * HEADS, L // Q_BLOCK, Q_BLOCK)
    o4, lse = _attn_fwd(qkv[0], qkv[1], qkv[2], cums, cums_t, "attn_fwd")
    y_a = _from_heads(o4)
    xp = jnp.pad(proj3[:, :, lo["conv"]:lo["conv"] + lo["conv_c"]], ((0, 0), (PAD, 0), (0, 0)))
    u = _conv_fwd(xp, p["conv_w"], p["conv_b"], lo["d_xbc"] // LANES, "conv_fwd")
    x4 = _to_heads(u[..., :D], B, L)
    b_blk = D // LANES
    c_blk = b_blk + SSD_GROUPS * SSD_STATE // LANES
    y4 = _ssd_fwd(x4, u, b_blk, c_blk, vals, cums, cums_t, p["dvec"], "ssd_fwd")
    y_s = _from_heads(y4)
    yb = _gnorm_fwd(y_s, proj, lo["z"], p["ssd_norm"], "gnorm_fwd")
    u2 = u.reshape(T, lo["conv_c"])
    ra = _mm(u2, p["wa"], a_off=(0, lo["d_xbc"]), dims=(T, D, D), tk=512, name="lru_ra")
    ix = _mm(u2, p["wx"], a_off=(0, lo["d_xbc"]), dims=(T, D, D), tk=512, name="lru_ix")
    yc, hs, a = _lru_fwd(u, lo["d_xbc"], ra.reshape(B, L, D), ix.reshape(B, L, D), proj3, lo["gate"], p["pvec"],
                         "lru_fwd")
    yc = yc.reshape(T, D)
    pa = _mm(y_a, p["wba"], name="branch_attn")
    pb = _mm(yb, p["wbs"], name="branch_ssd")
    pc = _mm(yc, p["wbl"], name="branch_lru")
    mixed = _merge_fwd(proj, lo["merge"], pa, pb, pc, "merge_fwd")
    out = _mm(mixed, p["wout"], res=h, name="mix_out")
    saved = dict(h=h, n=n, proj=proj, qkv=qkv, vals=vals, cums=cums, cums_t=cums_t, o4=o4, lse=lse, y_a=y_a, xp=xp, u=u,
                 x4=x4, y_s=y_s, yb=yb, ra=ra, ix=ix, yc=yc, hs=hs, a=a, pa=pa, pb=pb, pc=pc, mixed=mixed)
    return out, saved


def _mixer_bwd(dh, s, p, B, L):
    T, D = dh.shape
    lo = _layout(D)
    proj, u = s["proj"], s["u"]
    proj3 = proj.reshape(B, L, lo["n_all"])
    g = {}
    dmixed = _mm(dh, p["wout"], tb=True, name="mix_out_dx")
    g["wout"] = _mm(s["mixed"], dh, ta=True, name="mix_out_dw")
    dpa, dpb, dpc, dmerge = _merge_bwd(dmixed, proj, lo["merge"], s["pa"], s["pb"], s["pc"], "merge_bwd")
    dy_a = _mm(dpa, p["wba"], tb=True, name="branch_attn_dx")
    g["wba"] = _mm(s["y_a"], dpa, ta=True, name="branch_attn_dw")
    dyb = _mm(dpb, p["wbs"], tb=True, name="branch_ssd_dx")
    g["wbs"] = _mm(s["yb"], dpb, ta=True, name="branch_ssd_dw")
    dyc = _mm(dpc, p["wbl"], tb=True, name="branch_lru_dx")
    g["wbl"] = _mm(s["yc"], dpc, ta=True, name="branch_lru_dw")
    dgate, dra, dix, dxc, g["pvec"] = _lru_bwd(dyc.reshape(B, L, D), proj3, lo["gate"], s["hs"], s["a"], u, lo["d_xbc"],
                                               s["ra"].reshape(B, L, D), s["ix"].reshape(B, L, D), p["pvec"], "lru_bwd")
    dra, dix = dra.reshape(T, D), dix.reshape(T, D)
    u2 = u.reshape(T, lo["conv_c"])
    g["wa"] = _mm(u2, dra, ta=True, a_off=(0, lo["d_xbc"]), dims=(D, D, T), tm=512, name="lru_ra_dw")
    g["wx"] = _mm(u2, dix, ta=True, a_off=(0, lo["d_xbc"]), dims=(D, D, T), tm=512, name="lru_ix_dw")
    dxc = _mm(dra, p["wa"], tb=True, res=dxc.reshape(T, D), name="lru_ra_dx")
    dxc = _mm(dix, p["wx"], tb=True, res=dxc, name="lru_ix_dx")
    dy_s, dz, g["ssd_norm"] = _gnorm_bwd(dyb, s["y_s"], proj, lo["z"], p["ssd_norm"], "gnorm_bwd")
    b_blk = D // LANES
    c_blk = b_blk + SSD_GROUPS * SSD_STATE // LANES
    dx4, dBg, dCg, ddt_tm, dAc_tm, dAr, g["dvec"] = _ssd_bwd(s["x4"], u, b_blk, c_blk, s["vals"], s["cums"], s["cums_t"],
                                                             p["dvec"], _to_heads(dy_s, B, L), "ssd_bwd")
    grp = lambda t: t.transpose(0, 2, 1, 3).reshape(B, L, SSD_GROUPS * SSD_STATE)
    du = jnp.concatenate([_from_heads(dx4).reshape(B, L, D), grp(dBg), grp(dCg), dxc.reshape(B, L, D)], axis=-1)
    dpp, g["conv_wb"] = _conv_bwd_pre(s["xp"], du, p["conv_w"], p["conv_b"], lo["d_xbc"] // LANES, "conv_bwd_pre")
    dconv = _conv_bwd_in(dpp, p["conv_w"], "conv_bwd_in")
    dq4, dk4, dv4, dc_rows, dc_col = _attn_bwd(s["qkv"][0], s["qkv"][1], s["qkv"][2], s["o4"], _to_heads(dy_a, B, L), s["lse"],
                                       s["cums"], s["cums_t"], "attn_bwd")
    dqkv = jnp.stack([dq4, dk4, dv4], 0).transpose(1, 3, 0, 2, 4).reshape(T, 3 * D)
    drow_tm = jnp.concatenate([_rows_to_tm(dc_rows), _rows_to_tm(dAr),
                               jnp.zeros((B, L, LANES - 2 * HEADS), F32)], axis=-1)
    dcol_tm = dAc_tm + jnp.pad(dc_col[..., 0].transpose(0, 2, 1), ((0, 0), (0, 0), (0, LANES - HEADS)))
    dsmall, g["small_bias"], g["avec"] = _gate_post(drow_tm, dcol_tm, ddt_tm, proj3, lo["small"] // LANES, s["vals"],
                                                    p["small_bias"], p["avec"], "gate_post")
    dproj = jnp.concatenate([dqkv, dz, dmerge, dgate.reshape(T, D), dconv.reshape(T, lo["conv_c"]),
                             dsmall.reshape(T, SMALL_W)], axis=1)
    g["w_all"] = _mm(s["n"], dproj, ta=True, tm=1024, name="mix_in_dw")
    dn = _mm(dproj, p["w_all"], tb=True, name="mix_in_dx")
    dh_in, g["gm"] = _norm_bwd(s["h"], dn, dh, p["gm"], "mix_norm_bwd")
    return dh_in, g


def _small_vec(a, b):
    return jnp.concatenate([a, b, jnp.zeros((LANES - 2 * HEADS,), F32)])[None, :]


def _layer_params(w):
    zeros16 = jnp.zeros((HEADS,), F32)
    pvec = jnp.concatenate([w["lru_b_a"][None], w["lru_b_x"][None], w["lru_lambda"][None],
                            jnp.zeros((SUBLANES - 3, w["lru_b_a"].shape[0]), F32)], axis=0)
    return dict(
        g1=w["ffn1_norm"][None], gu1=w["ffn1_w_gate_up"], d1=w["ffn1_w_down"],
        gm=w["mix_norm"][None], w_all=_reorder_w_in(w["w_in"]),
        small_bias=_small_vec(w["fox_forget_bias"], w["ssd_dt_bias"]),
        avec=_small_vec(zeros16, -jnp.exp(w["ssd_a_log"])), dvec=_small_vec(zeros16, w["ssd_d"]),
        conv_w=jnp.concatenate([w["ssd_conv_w"], w["lru_conv_w"]], axis=1),
        conv_b=jnp.concatenate([w["ssd_conv_b"], w["lru_conv_b"]])[None],
        ssd_norm=w["ssd_norm"][None],
        wa=_block_diag(w["lru_w_a"]).astype(BF16), wx=_block_diag(w["lru_w_x"]).astype(BF16), pvec=pvec,
        wba=w["w_branch_attn"], wbs=w["w_branch_ssd"], wbl=w["w_branch_lru"], wout=w["w_out"],
        g2=w["ffn2_norm"][None], gu2=w["ffn2_w_gate_up"], d2=w["ffn2_w_down"],
    )


def _layer_fwd(h, p, B, L):
    h, s1 = _ffn_fwd(h, p["g1"], p["gu1"], p["d1"], "ffn1")
    h, sm = _mixer_fwd(h, p, B, L)
    h, s2 = _ffn_fwd(h, p["g2"], p["gu2"], p["d2"], "ffn2")
    return h, (s1, sm, s2)


def _layer_bwd(dh, saved, p, w, B, L):
    s1, sm, s2 = saved
    D = dh.shape[1]
    d_xbc = D + D_XBC_EXTRA
    dh, f2 = _ffn_bwd(dh, s2, p["g2"], p["gu2"], p["d2"], "ffn2")
    dh, gm = _mixer_bwd(dh, sm, p, B, L)
    dh, f1 = _ffn_bwd(dh, s1, p["g1"], p["gu1"], p["d1"], "ffn1")
    sb, av = gm["small_bias"][0], gm["avec"][0]
    cw = gm["conv_wb"]
    grads = dict(
        ffn1_norm=f1["norm"][0], ffn1_w_gate_up=f1["gu"], ffn1_w_down=f1["down"],
        mix_norm=gm["gm"][0], w_in=_restore_w_in(gm["w_all"]),
        fox_forget_bias=sb[:HEADS], ssd_conv_w=cw[:CONV_K, :d_xbc], ssd_conv_b=cw[CONV_K, :d_xbc],
        ssd_dt_bias=sb[HEADS:2 * HEADS], ssd_a_log=av[HEADS:2 * HEADS] * (-jnp.exp(w["ssd_a_log"])),
        ssd_d=gm["dvec"][0, HEADS:2 * HEADS], ssd_norm=gm["ssd_norm"][0],
        lru_conv_w=cw[:CONV_K, d_xbc:], lru_conv_b=cw[CONV_K, d_xbc:],
        lru_w_a=_diag_blocks(gm["wa"], HEADS), lru_b_a=gm["pvec"][0], lru_w_x=_diag_blocks(gm["wx"], HEADS),
        lru_b_x=gm["pvec"][1], lru_lambda=gm["pvec"][2],
        w_branch_attn=gm["wba"], w_branch_ssd=gm["wbs"], w_branch_lru=gm["wbl"], w_out=gm["wout"],
        ffn2_norm=f2["norm"][0], ffn2_w_gate_up=f2["gu"], ffn2_w_down=f2["down"],
    )
    return dh, grads


LAYER_NAMES = ["ffn1_norm", "ffn1_w_gate_up", "ffn1_w_down", "mix_norm", "w_in", "fox_forget_bias", "ssd_conv_w",
               "ssd_conv_b", "ssd_dt_bias", "ssd_a_log", "ssd_d", "ssd_norm", "lru_conv_w", "lru_conv_b", "lru_w_a",
               "lru_b_a", "lru_w_x", "lru_b_x", "lru_lambda", "w_branch_attn", "w_branch_ssd", "w_branch_lru", "w_out",
               "ffn2_norm", "ffn2_w_gate_up", "ffn2_w_down"]
WEIGHT_NAMES = ["meta_tokens"] + LAYER_NAMES + ["final_norm"]


def _local_step(x, target, meta, layers, final_norm):
    B, S, D = x.shape
    L = -(-(N_META + S) // Q_BLOCK) * Q_BLOCK
    h = jnp.concatenate([jnp.broadcast_to(meta[None], (B, N_META, D)), x,
                         jnp.zeros((B, L - N_META - S, D), F32)], axis=1).reshape(B * L, D)
    params, saved = [], []
    for w in layers:
        p = _layer_params(w)
        h, s = _layer_fwd(h, p, B, L)
        params.append(p)
        saved.append(s)
    tgt = jnp.pad(target, ((0, 0), (N_META, L - N_META - S), (0, 0))).reshape(B * L, D)
    dh, loss, dfinal = _loss_bwd(h, tgt, final_norm[None], L, S, "loss")
    grads = [None] * len(layers)
    for l in reversed(range(len(layers))):
        dh, grads[l] = _layer_bwd(dh, saved[l], params[l], layers[l], B, L)
    dh3 = dh.reshape(B, L, D)
    return loss, dh3[:, N_META:N_META + S], jnp.sum(dh3[:, :N_META], axis=0), grads, dfinal[0]


BIG_NAMES = ["ffn1_w_gate_up", "ffn1_w_down", "w_in", "w_branch_attn", "w_branch_ssd", "w_branch_lru", "w_out",
             "ffn2_w_gate_up", "ffn2_w_down"]
COL_SHARDED = {"ffn1_w_gate_up", "w_in", "ffn2_w_gate_up"}
SMALL_SHARDED = ["meta_tokens", "ssd_conv_w", "lru_conv_w"]
SMALL_NAMES = [n for n in LAYER_NAMES if n not in BIG_NAMES]


def _as_rows(flat):
    n = flat.shape[0]
    unit = LANES * SUBLANES
    total = -(-n // unit) * unit
    return jnp.pad(flat, (0, total - n)).reshape(total // LANES, LANES)


def _pack_shards(shards):
    return _as_rows(jnp.concatenate([s.reshape(-1) for s in shards]))


def _unpack_gathered(gathered, shapes):
    flat = gathered.reshape(N_DEV, -1)
    out, o = {}, 0
    for n, (r, c) in zip(BIG_NAMES, shapes):
        seg = flat[:, o:o + r * c].reshape(N_DEV, r, c)
        o += r * c
        out[n] = seg.transpose(1, 0, 2).reshape(r, N_DEV * c) if n in COL_SHARDED else seg.reshape(N_DEV * r, c)
    return out


def _pack_full_grads(grads, shapes):
    segs = []
    for n, (r, c) in zip(BIG_NAMES, shapes):
        g = grads[n]
        if n in COL_SHARDED:
            g = g.reshape(r, N_DEV, c).transpose(1, 0, 2)
        segs.append(g.reshape(N_DEV, r * c))
    flat = jnp.concatenate(segs, axis=1)
    n = flat.shape[1]
    unit = LANES * SUBLANES
    total = -(-n // unit) * unit
    return jnp.pad(flat, ((0, 0), (0, total - n))).reshape(N_DEV, total // LANES, LANES)


def _unpack_local(rows, shapes):
    flat = rows.reshape(-1)
    out, o = {}, 0
    for n, (r, c) in zip(BIG_NAMES, shapes):
        out[n] = flat[o:o + r * c].reshape(r, c)
        o += r * c
    return out


def _flatten_list(arrs):
    return _as_rows(jnp.concatenate([a.reshape(-1) for a in arrs]))


def _split_like(rows, shapes):
    flat = rows.reshape(-1)
    out, o = [], 0
    for s in shapes:
        n = math.prod(s)
        out.append(flat[o:o + n].reshape(s))
        o += n
    return out


def _gather_last(rows8, shape):
    lead, c = shape[:-1], shape[-1]
    t = rows8.reshape((N_DEV,) + tuple(lead) + (c,))
    return jnp.moveaxis(t, 0, -2).reshape(tuple(lead) + (N_DEV * c,))


def kernel(x, meta_tokens, ffn1_norm, ffn1_w_gate_up, ffn1_w_down, mix_norm, w_in, fox_forget_bias, ssd_conv_w, ssd_conv_b, ssd_dt_bias, ssd_a_log, ssd_d, ssd_norm, lru_conv_w, lru_conv_b, lru_w_a, lru_b_a, lru_w_x, lru_b_x, lru_lambda, w_branch_attn, w_branch_ssd, w_branch_lru, w_out, ffn2_norm, ffn2_w_gate_up, ffn2_w_down, final_norm, loss_target, m_meta_tokens, m_ffn1_norm, m_ffn1_w_gate_up, m_ffn1_w_down, m_mix_norm, m_w_in, m_fox_forget_bias, m_ssd_conv_w, m_ssd_conv_b, m_ssd_dt_bias, m_ssd_a_log, m_ssd_d, m_ssd_norm, m_lru_conv_w, m_lru_conv_b, m_lru_w_a, m_lru_b_a, m_lru_w_x, m_lru_b_x, m_lru_lambda, m_w_branch_attn, m_w_branch_ssd, m_w_branch_lru, m_w_out, m_ffn2_norm, m_ffn2_w_gate_up, m_ffn2_w_down, m_final_norm, v_meta_tokens, v_ffn1_norm, v_ffn1_w_gate_up, v_ffn1_w_down, v_mix_norm, v_w_in, v_fox_forget_bias, v_ssd_conv_w, v_ssd_conv_b, v_ssd_dt_bias, v_ssd_a_log, v_ssd_d, v_ssd_norm, v_lru_conv_w, v_lru_conv_b, v_lru_w_a, v_lru_b_a, v_lru_w_x, v_lru_b_x, v_lru_lambda, v_w_branch_attn, v_w_branch_ssd, v_w_branch_lru, v_w_out, v_ffn2_norm, v_ffn2_w_gate_up, v_ffn2_w_down, v_final_norm):
    weights = dict(zip(WEIGHT_NAMES, (meta_tokens, ffn1_norm, ffn1_w_gate_up, ffn1_w_down, mix_norm, w_in, fox_forget_bias, ssd_conv_w, ssd_conv_b, ssd_dt_bias, ssd_a_log, ssd_d, ssd_norm, lru_conv_w, lru_conv_b, lru_w_a, lru_b_a, lru_w_x, lru_b_x, lru_lambda, w_branch_attn, w_branch_ssd, w_branch_lru, w_out, ffn2_norm, ffn2_w_gate_up, ffn2_w_down, final_norm,)))
    mom1 = dict(zip(WEIGHT_NAMES, (m_meta_tokens, m_ffn1_norm, m_ffn1_w_gate_up, m_ffn1_w_down, m_mix_norm, m_w_in, m_fox_forget_bias, m_ssd_conv_w, m_ssd_conv_b, m_ssd_dt_bias, m_ssd_a_log, m_ssd_d, m_ssd_norm, m_lru_conv_w, m_lru_conv_b, m_lru_w_a, m_lru_b_a, m_lru_w_x, m_lru_b_x, m_lru_lambda, m_w_branch_attn, m_w_branch_ssd, m_w_branch_lru, m_w_out, m_ffn2_norm, m_ffn2_w_gate_up, m_ffn2_w_down, m_final_norm,)))
    mom2 = dict(zip(WEIGHT_NAMES, (v_meta_tokens, v_ffn1_norm, v_ffn1_w_gate_up, v_ffn1_w_down, v_mix_norm, v_w_in, v_fox_forget_bias, v_ssd_conv_w, v_ssd_conv_b, v_ssd_dt_bias, v_ssd_a_log, v_ssd_d, v_ssd_norm, v_lru_conv_w, v_lru_conv_b, v_lru_w_a, v_lru_b_a, v_lru_w_x, v_lru_b_x, v_lru_lambda, v_w_branch_attn, v_w_branch_ssd, v_w_branch_lru, v_w_out, v_ffn2_norm, v_ffn2_w_gate_up, v_ffn2_w_down, v_final_norm,)))
    depth = ffn1_norm.shape[0]
    my_idx = 4 * lax.axis_index("x") + 2 * lax.axis_index("y") + lax.axis_index("c")

    small_shapes = [weights[n].shape for n in SMALL_SHARDED]
    gathered = _all_gather(_flatten_list([weights[n] for n in SMALL_SHARDED]), "gather_small").reshape(N_DEV, -1)
    small_full, o = {}, 0
    for n, s in zip(SMALL_SHARDED, small_shapes):
        k = math.prod(s)
        small_full[n] = _gather_last(gathered[:, o:o + k], s)
        o += k

    shard_shapes = [weights[n].shape[1:] for n in BIG_NAMES]
    layers = []
    for l in range(depth):
        packed = _pack_shards([weights[n][l].astype(BF16) for n in BIG_NAMES])
        w = _unpack_gathered(_all_gather(packed, "gather_weights"), shard_shapes)
        for n in SMALL_NAMES:
            w[n] = small_full[n][l] if n in SMALL_SHARDED else weights[n][l]
        layers.append(w)

    loss, dx, dmeta, grads, dfinal = _local_step(x, loss_target, small_full["meta_tokens"], layers, final_norm)
    loss = lax.psum(loss[0, 0], ("x", "y", "c"))

    summed = {n: [] for n in WEIGHT_NAMES}
    for l in range(depth):
        parts = _pack_full_grads(grads[l], shard_shapes)
        local = _unpack_local(_sum8(_exchange(parts, "exchange_grads"), "sum_grads"), shard_shapes)
        for n in BIG_NAMES:
            summed[n].append(local[n])

    small_list = [dmeta, dfinal] + [grads[l][n] for l in range(depth) for n in SMALL_NAMES]
    total = _sum8(_all_gather(_flatten_list(small_list), "gather_small_grads"), "sum_small_grads")
    parts = _split_like(total, [a.shape for a in small_list])
    full_small = {"meta_tokens": parts[0], "final_norm": parts[1]}
    for i, n in enumerate(SMALL_NAMES):
        full_small[n] = jnp.stack([parts[2 + l * len(SMALL_NAMES) + i] for l in range(depth)])
    grad = {}
    for n in WEIGHT_NAMES:
        if n in BIG_NAMES:
            grad[n] = jnp.stack(summed[n])
        elif n in SMALL_SHARDED:
            c = weights[n].shape[-1]
            grad[n] = lax.dynamic_slice_in_dim(full_small[n], my_idx * c, c, axis=full_small[n].ndim - 1)
        else:
            grad[n] = full_small[n]

    delta, new_m, new_v = {}, {}, {}
    for n in WEIGHT_NAMES:
        delta[n], new_m[n], new_v[n] = _adamw(weights[n], grad[n], mom1[n], mom2[n], "adamw_" + n)
    return (loss, dx, *[grad[n] for n in WEIGHT_NAMES], *[delta[n] for n in WEIGHT_NAMES],
            *[new_m[n] for n in WEIGHT_NAMES], *[new_v[n] for n in WEIGHT_NAMES])
```

```python
import functools
import math

import jax
import jax.numpy as jnp
from jax import lax
from jax.experimental import pallas as pl
from jax.experimental.pallas import tpu as pltpu

F32 = jnp.float32
BF16 = jnp.bfloat16

N_DEV = 8
N_META = 16
Q_BLOCK = 128
NORM_EPS = 1e-6
HEADS = 16
HEAD_DIM = 64
SSD_GROUPS = 2
SSD_STATE = 128
CONV_K = 4
LRU_C = 8.0
ADAM_LR, ADAM_B1, ADAM_B2, ADAM_EPS, ADAM_WD, ADAM_STEP = 0.001, 0.9, 0.999, 1e-08, 0.01, 10

LANES = 128
SUBLANES = 8
VMEM_LIMIT = 56 * 1024 * 1024
NEG = -1e30


def _cparams(sem=None):
    return pltpu.CompilerParams(dimension_semantics=sem, vmem_limit_bytes=VMEM_LIMIT)


def _tile(dim, target, mult=LANES):
    if dim <= target:
        return dim
    best = None
    for t in range(mult, target + 1, mult):
        if dim % t == 0:
            best = t
    assert best is not None, (dim, target)
    return best


def _sigmoid(x):
    return 1.0 / (1.0 + jnp.exp(-x))


def _log1p_exp_neg_abs(x):
    e = jnp.exp(-jnp.abs(x))
    u = 1.0 + e
    return jnp.where(u == 1.0, e, jnp.log(u) * (e / jnp.where(u == 1.0, 1.0, u - 1.0)))


def _log_sigmoid(x):
    return jnp.minimum(x, 0.0) - _log1p_exp_neg_abs(x)


def _softplus(x):
    return jnp.maximum(x, 0.0) + _log1p_exp_neg_abs(x)


def _one_minus_exp(y):
    u = jnp.exp(y)
    safe = jnp.where(u == 1.0, 0.5, u)
    return jnp.where(u == 1.0, -y, (1.0 - u) * y / jnp.log(safe))


def _silu(x):
    return x * _sigmoid(x)


def _dsilu(x):
    s = _sigmoid(x)
    return s * (1.0 + x * (1.0 - s))


_GELU_C = math.sqrt(2.0 / math.pi)


def _gelu(x):
    return 0.5 * x * (1.0 + jnp.tanh(_GELU_C * (x + 0.044715 * x * x * x)))


def _dgelu(x):
    t = jnp.tanh(_GELU_C * (x + 0.044715 * x * x * x))
    return 0.5 * (1.0 + t) + 0.5 * x * (1.0 - t * t) * _GELU_C * (1.0 + 3.0 * 0.044715 * x * x)


def _split3_dot(tri, x):
    hi = x.astype(BF16)
    r1 = x - hi.astype(F32)
    mid = r1.astype(BF16)
    lo = (r1 - mid.astype(F32)).astype(BF16)
    t = tri.astype(BF16)
    d = lambda p: jnp.dot(t, p, preferred_element_type=F32)
    return d(hi) + d(mid) + d(lo)


def _lower_tri(n, strict=False):
    r = lax.broadcasted_iota(jnp.int32, (n, n), 0)
    c = lax.broadcasted_iota(jnp.int32, (n, n), 1)
    return (c < r) if strict else (c <= r)


def _mm(a, b, *, ta=False, tb=False, out_dtype=F32, res=None, scale=None, tm=None, tn=None, tk=None,
        a_off=(0, 0), b_off=(0, 0), dims=None, name):
    if dims is None:
        M, K = (a.shape[1], a.shape[0]) if ta else a.shape
        N = b.shape[0] if tb else b.shape[1]
    else:
        M, N, K = dims
    tm = tm or _tile(M, 768)
    tn = tn or _tile(N, 768)
    tk = tk or (K if K <= 2816 else _tile(K, 1408))
    assert M % tm == 0 and N % tn == 0 and K % tk == 0, (name, M, N, K, tm, tn, tk)
    nk = K // tk
    ca = 0 if ta else 1
    cb = 1 if tb else 0

    def blk(rows, cols, off):
        assert off[0] % rows == 0 and off[1] % cols == 0, (name, off, rows, cols)
        return off[0] // rows, off[1] // cols

    if ta:
        ao = blk(tk, tm, a_off)
        a_spec = pl.BlockSpec((tk, tm), lambda i, j, k: (k + ao[0], i + ao[1]))
    else:
        ao = blk(tm, tk, a_off)
        a_spec = pl.BlockSpec((tm, tk), lambda i, j, k: (i + ao[0], k + ao[1]))
    if tb:
        bo = blk(tn, tk, b_off)
        b_spec = pl.BlockSpec((tn, tk), lambda i, j, k: (j + bo[0], k + bo[1]))
    else:
        bo = blk(tk, tn, b_off)
        b_spec = pl.BlockSpec((tk, tn), lambda i, j, k: (k + bo[0], j + bo[1]))
    o_spec = pl.BlockSpec((tm, tn), lambda i, j, k: (i, j))
    in_specs = [a_spec, b_spec] + ([o_spec] if res is not None else [])
    has_res = res is not None

    def kern(*refs):
        if has_res:
            a_ref, b_ref, r_ref, o_ref = refs[:4]
            scr = refs[4:]
        else:
            a_ref, b_ref, o_ref = refs[:3]
            r_ref = None
            scr = refs[3:]
        p = lax.dot_general(a_ref[...].astype(BF16), b_ref[...].astype(BF16), (((ca,), (cb,)), ((), ())),
                            preferred_element_type=F32)

        def fin(val):
            if scale is not None:
                val = val * scale
            if has_res:
                val = r_ref[...] + val
            o_ref[...] = val.astype(out_dtype)

        if nk == 1:
            fin(p)
        else:
            acc = scr[0]
            k = pl.program_id(2)

            @pl.when(k == 0)
            def _():
                acc[...] = p

            @pl.when(k > 0)
            def _():
                acc[...] += p

            @pl.when(k == nk - 1)
            def _():
                fin(acc[...])

    args = (a, b) + ((res,) if has_res else ())
    return pl.pallas_call(
        kern, name=name, grid=(M // tm, N // tn, nk), in_specs=in_specs, out_specs=o_spec,
        out_shape=jax.ShapeDtypeStruct((M, N), out_dtype),
        scratch_shapes=[pltpu.VMEM((tm, tn), F32)] if nk > 1 else [],
        compiler_params=_cparams(("parallel", "parallel", "arbitrary")),
    )(*args)


def _rows(body, tiled, full, outs, accs, *, tr, name, T):
    assert T % tr == 0
    in_specs = []
    for arr, width, off in tiled:
        assert off % width == 0, (name, off, width)
        in_specs.append(pl.BlockSpec((tr, width), functools.partial(lambda i, o: (i, o), o=off // width)))
    for arr in full:
        in_specs.append(pl.BlockSpec(arr.shape, lambda i: (0, 0)))
    out_specs = [pl.BlockSpec((tr, w), lambda i: (i, 0)) for w, _ in outs]
    out_specs += [pl.BlockSpec(s, lambda i: (0, 0)) for s, _ in accs]
    out_shape = [jax.ShapeDtypeStruct((T, w), d) for w, d in outs] + [jax.ShapeDtypeStruct(s, d) for s, d in accs]
    nt, nf, no = len(tiled), len(full), len(outs)

    def kern(*refs):
        i = pl.program_id(0)
        acc_refs = refs[nt + nf + no:]

        @pl.when(i == 0)
        def _():
            for r in acc_refs:
                r[...] = jnp.zeros(r.shape, r.dtype)

        body(i, refs[:nt], refs[nt:nt + nf], refs[nt + nf:nt + nf + no], acc_refs)

    res = pl.pallas_call(
        kern, name=name, grid=(T // tr,), in_specs=in_specs, out_specs=out_specs, out_shape=out_shape,
        compiler_params=_cparams(("arbitrary",)),
    )(*[t[0] for t in tiled], *full)
    return res


def _colsum(x):
    return jnp.sum(x, axis=0, keepdims=True)


def _norm_fwd(h, g, name):
    T, D = h.shape

    def body(i, t, f, o, a):
        x = t[0][...]
        r = lax.rsqrt(jnp.mean(x * x, axis=-1, keepdims=True) + NORM_EPS)
        o[0][...] = (x * r * f[0][...]).astype(BF16)

    return _rows(body, [(h, D, 0)], [g], [(D, BF16)], [], tr=_tile(T, 768, 8), name=name, T=T)[0]


def _norm_bwd(h, dn, dh, g, name):
    T, D = h.shape

    def body(i, t, f, o, a):
        x, dnv, dhv = t[0][...], t[1][...], t[2][...]
        r = lax.rsqrt(jnp.mean(x * x, axis=-1, keepdims=True) + NORM_EPS)
        xh = x * r
        dng = dnv * f[0][...]
        o[0][...] = dhv + r * (dng - xh * jnp.mean(dng * xh, axis=-1, keepdims=True))
        a[0][...] += _colsum(dnv * xh)

    return _rows(body, [(h, D, 0), (dn, D, 0), (dh, D, 0)], [g], [(D, F32)], [((1, D), F32)],
                 tr=_tile(T, 384, 8), name=name, T=T)


def _swiglu_fwd(gu, name):
    T, F2 = gu.shape
    F = F2 // 2

    def body(i, t, f, o, a):
        o[0][...] = (_silu(t[0][...]) * t[1][...]).astype(BF16)

    return _rows(body, [(gu, F, 0), (gu, F, F)], [], [(F, BF16)], [], tr=_tile(T, 256, 8), name=name, T=T)[0]


def _swiglu_bwd(gu, da, name):
    T, F2 = gu.shape
    F = F2 // 2

    def body(i, t, f, o, a):
        gv, uv, dav = t[0][...], t[1][...], t[2][...]
        o[0][:, :F] = (dav * uv * _dsilu(gv)).astype(BF16)
        o[0][:, F:] = (dav * _silu(gv)).astype(BF16)

    return _rows(body, [(gu, F, 0), (gu, F, F), (da, F, 0)], [], [(F2, BF16)], [], tr=_tile(T, 256, 8),
                 name=name, T=T)[0]


def _merge_fwd(proj, off, pa, pb, pc, name):
    T, D = pa.shape

    def body(i, t, f, o, a):
        o[0][...] = (_sigmoid(t[0][...]) * t[3][...] + _sigmoid(t[1][...]) * t[4][...]
                     + _sigmoid(t[2][...]) * t[5][...]).astype(BF16)

    tiled = [(proj, D, off), (proj, D, off + D), (proj, D, off + 2 * D), (pa, D, 0), (pb, D, 0), (pc, D, 0)]
    return _rows(body, tiled, [], [(D, BF16)], [], tr=_tile(T, 384, 8), name=name, T=T)[0]


def _merge_bwd(dmixed, proj, off, pa, pb, pc, name):
    T, D = pa.shape

    def body(i, t, f, o, a):
        dm = t[0][...]
        for k in range(3):
            g = _sigmoid(t[1 + k][...])
            o[k][...] = (dm * g).astype(BF16)
            o[3][:, k * D:(k + 1) * D] = (dm * t[4 + k][...] * g * (1.0 - g)).astype(BF16)

    tiled = [(dmixed, D, 0), (proj, D, off), (proj, D, off + D), (proj, D, off + 2 * D), (pa, D, 0), (pb, D, 0),
             (pc, D, 0)]
    return _rows(body, tiled, [], [(D, BF16)] * 3 + [(3 * D, BF16)], [], tr=_tile(T, 384, 8), name=name, T=T)


def _gnorm_fwd(y, proj, zoff, nw, name):
    T, D = y.shape
    gs = D // SSD_GROUPS

    def body(i, t, f, o, a):
        s = t[0][...] * _silu(t[1][...])
        for g in range(SSD_GROUPS):
            sg = s[:, g * gs:(g + 1) * gs]
            r = lax.rsqrt(jnp.mean(sg * sg, axis=-1, keepdims=True) + NORM_EPS)
            o[0][:, g * gs:(g + 1) * gs] = (sg * r * f[0][:, g * gs:(g + 1) * gs]).astype(BF16)

    return _rows(body, [(y, D, 0), (proj, D, zoff)], [nw], [(D, BF16)], [], tr=_tile(T, 384, 8), name=name, T=T)[0]


def _gnorm_bwd(dout, y, proj, zoff, nw, name):
    T, D = y.shape
    gs = D // SSD_GROUPS

    def body(i, t, f, o, a):
        dov, yv, zv = t[0][...], t[1][...], t[2][...]
        sz = _silu(zv)
        s = yv * sz
        dsz = _dsilu(zv)
        for g in range(SSD_GROUPS):
            sl = slice(g * gs, (g + 1) * gs)
            sg = s[:, sl]
            r = lax.rsqrt(jnp.mean(sg * sg, axis=-1, keepdims=True) + NORM_EPS)
            sh = sg * r
            dog = dov[:, sl]
            dng = dog * f[0][:, sl]
            ds = r * (dng - sh * jnp.mean(dng * sh, axis=-1, keepdims=True))
            o[0][:, sl] = ds * sz[:, sl]
            o[1][:, sl] = (ds * yv[:, sl] * dsz[:, sl]).astype(BF16)
            a[0][:, sl] += _colsum(dog * sh)

    return _rows(body, [(dout, D, 0), (y, D, 0), (proj, D, zoff)], [nw], [(D, F32), (D, BF16)], [((1, D), F32)],
                 tr=_tile(T, 384, 8), name=name, T=T)


def _loss_bwd(h, tgt, g, seq_len, n_real, name):
    T, D = h.shape
    tr = _tile(seq_len, 384, 8)
    per_seq = seq_len // tr

    def body(i, t, f, o, a):
        x, tg = t[0][...], t[1][...]
        pos = (i % per_seq) * tr + lax.broadcasted_iota(jnp.int32, (tr, 1), 0)
        valid = (pos >= N_META) & (pos < N_META + n_real)
        r = lax.rsqrt(jnp.mean(x * x, axis=-1, keepdims=True) + NORM_EPS)
        xh = x * r
        e = jnp.where(valid, xh * f[0][...] - tg, 0.0)
        a[0][...] += jnp.zeros((1, LANES), F32) + 0.5 * jnp.sum(jnp.sum(e * e, axis=-1, keepdims=True) / D,
                                                              axis=0, keepdims=True)
        dy = e / D
        dng = dy * f[0][...]
        o[0][...] = r * (dng - xh * jnp.mean(dng * xh, axis=-1, keepdims=True))
        a[1][...] += _colsum(dy * xh)

    return _rows(body, [(h, D, 0), (tgt, D, 0)], [g], [(D, F32)], [((1, LANES), F32), ((1, D), F32)], tr=tr,
                 name=name, T=T)


def _lane_is_attn(shape):
    return lax.broadcasted_iota(jnp.int32, shape, len(shape) - 1) < HEADS


def _gate_prep(proj3, col_blk, bias, avec, name):
    B, L, _ = proj3.shape
    Q = Q_BLOCK
    nc = L // Q

    def kern(x_ref, b_ref, a_ref, v_ref, c_ref, carry):
        c = pl.program_id(1)

        @pl.when(c == 0)
        def _():
            carry[...] = jnp.zeros_like(carry)

        x = x_ref[0] + b_ref[...]
        attn = _lane_is_attn(x.shape)
        v = jnp.where(attn, _log_sigmoid(x), _softplus(x))
        w = jnp.where(attn, v, v * a_ref[...])
        cs = _split3_dot(_lower_tri(Q), w) + jnp.where(attn[:1], carry[...], 0.0)
        v_ref[0] = v
        c_ref[0] = cs
        rows = lax.broadcasted_iota(jnp.int32, (Q, 1), 0)
        carry[...] = jnp.sum(jnp.where(rows == Q - 1, cs, 0.0), axis=0, keepdims=True)

    blk = pl.BlockSpec((1, Q, LANES), lambda b, c: (b, c, 0))
    vec = pl.BlockSpec((1, LANES), lambda b, c: (0, 0))
    return pl.pallas_call(
        kern, name=name, grid=(B, nc),
        in_specs=[pl.BlockSpec((1, Q, LANES), lambda b, c: (b, c, col_blk)), vec, vec],
        out_specs=[blk, blk], out_shape=[jax.ShapeDtypeStruct((B, L, LANES), F32)] * 2,
        scratch_shapes=[pltpu.VMEM((1, LANES), F32)],
        compiler_params=_cparams(("parallel", "arbitrary")),
    )(proj3, bias, avec)


def _gate_post(drow, dcol, ddt, proj3, col_blk, vals, bias, avec, name):
    B, L, _ = proj3.shape
    Q = Q_BLOCK
    nc = L // Q

    def kern(dr_ref, dc_ref, dd_ref, x_ref, v_ref, b_ref, a_ref, o_ref, db_ref, da_ref, carry):
        b = pl.program_id(0)
        c = pl.program_id(1)

        @pl.when((b == 0) & (c == 0))
        def _():
            db_ref[...] = jnp.zeros_like(db_ref)
            da_ref[...] = jnp.zeros_like(da_ref)

        @pl.when(c == 0)
        def _():
            carry[...] = jnp.zeros_like(carry)

        x = x_ref[0] + b_ref[...]
        attn = _lane_is_attn(x.shape)
        dcs = dr_ref[0] + dc_ref[0]
        upper = jnp.logical_not(_lower_tri(Q, strict=True))
        rc = _split3_dot(upper, dcs) + jnp.where(attn[:1], carry[...], 0.0)
        rows = lax.broadcasted_iota(jnp.int32, (Q, 1), 0)
        carry[...] = jnp.sum(jnp.where(rows == 0, rc, 0.0), axis=0, keepdims=True)
        dv = jnp.where(attn, rc, dd_ref[0] + rc * a_ref[...])
        dpre = dv * jnp.where(attn, _sigmoid(-x), _sigmoid(x))
        o_ref[0] = dpre.astype(BF16)
        db_ref[...] += _colsum(dpre)
        da_ref[...] += _colsum(jnp.where(attn, 0.0, rc * v_ref[0]))

    rev = pl.BlockSpec((1, Q, LANES), lambda b, c: (b, nc - 1 - c, 0))
    vec = pl.BlockSpec((1, LANES), lambda b, c: (0, 0))
    return pl.pallas_call(
        kern, name=name, grid=(B, nc),
        in_specs=[rev, rev, rev, pl.BlockSpec((1, Q, LANES), lambda b, c: (b, nc - 1 - c, col_blk)), rev, vec, vec],
        out_specs=[rev, vec, vec],
        out_shape=[jax.ShapeDtypeStruct((B, L, LANES), BF16), jax.ShapeDtypeStruct((1, LANES), F32),
                   jax.ShapeDtypeStruct((1, LANES), F32)],
        scratch_shapes=[pltpu.VMEM((1, LANES), F32)],
        compiler_params=_cparams(("arbitrary", "arbitrary")),
    )(drow, dcol, ddt, proj3, vals, bias, avec)


def _lane_col(tile, lane):
    sel = lax.broadcasted_iota(jnp.int32, tile.shape, 1) == lane
    return jnp.sum(jnp.where(sel, tile, 0.0), axis=1, keepdims=True)


AUG = LANES
AUG_A = HEAD_DIM
AUG_B = HEAD_DIM + 3


def _split3(x):
    hi = x.astype(BF16).astype(F32)
    mid = (x - hi).astype(BF16).astype(F32)
    lo = (x - hi - mid).astype(BF16).astype(F32)
    return hi, mid, lo


def _put3(base, lane, first, x):
    hi, mid, lo = _split3(x)
    return jnp.where(lane == first, hi, jnp.where(lane == first + 1, mid, jnp.where(lane == first + 2, lo, base)))


def _attn_fwd(qa, ka, va, name):
    B, H, L, _ = qa.shape
    tq = _tile(L, 384)
    nq = L // tq

    def kern(q_ref, k_ref, v_ref, o_ref, l_ref):
        qi = pl.program_id(2)
        qv = q_ref[0, 0]
        causal = _lower_tri(tq)

        def step(j, carry, masked):
            m, acc = carry
            rows = pl.ds(pl.multiple_of(j * tq, tq), tq)
            s = _dot_nt(qv, k_ref[0, 0, rows, :])
            if masked:
                s = jnp.where(causal, s, NEG)
            m_new = jnp.maximum(m, jnp.max(s, axis=1, keepdims=True))
            p = jnp.exp(s - m_new)
            acc = jnp.exp(m - m_new) * acc + _dot(p.astype(BF16), v_ref[0, 0, rows, :])
            return m_new, acc

        init = (jnp.full((tq, 1), NEG, F32), jnp.zeros((tq, AUG), F32))
        carry = lax.fori_loop(0, qi, lambda j, c: step(j, c, False), init)
        m, acc = step(qi, carry, True)
        l = _lane_col(acc, AUG_A)
        o_ref[0, 0] = acc / l
        l_ref[0, 0] = m + jnp.log(l)

    qspec = pl.BlockSpec((1, 1, tq, AUG), lambda b, h, i: (b, h, i, 0))
    kvspec = pl.BlockSpec((1, 1, L, AUG), lambda b, h, i: (b, h, 0, 0))
    return pl.pallas_call(
        kern, name=name, grid=(B, H, nq), in_specs=[qspec, kvspec, kvspec],
        out_specs=[qspec, pl.BlockSpec((1, 1, tq, 1), lambda b, h, i: (b, h, i, 0))],
        out_shape=[jax.ShapeDtypeStruct((B, H, L, AUG), F32), jax.ShapeDtypeStruct((B, H, L, 1), F32)],
        compiler_params=_cparams(("parallel", "parallel", "arbitrary")),
    )(qa, ka, va)


def _attn_bwd(qa, ka, va, o, do, lse, name):
    B, H, L, _ = qa.shape
    tq = _tile(L, 384)
    nq = L // tq

    def kern(q_ref, k_ref, v_ref, o_ref, do_ref, l_ref, dq_ref, dk_ref, dv_ref, dk_acc, dv_acc):
        qi = pl.program_id(2)

        @pl.when(qi == 0)
        def _():
            dk_acc[...] = jnp.zeros_like(dk_acc)
            dv_acc[...] = jnp.zeros_like(dv_acc)

        lane = lax.broadcasted_iota(jnp.int32, (tq, AUG), 1)
        qf = q_ref[0, 0].astype(F32)
        dov = do_ref[0, 0]
        head = lane < HEAD_DIM
        dsum = jnp.sum(jnp.where(head, dov * o_ref[0, 0], 0.0), axis=1, keepdims=True)
        dob = _put3(dov, lane, AUG_A, -dsum).astype(BF16)
        c_t = jnp.sum(jnp.where((lane >= AUG_A) & (lane < AUG_A + 3), qf, 0.0), axis=1, keepdims=True)
        qb = _put3(qf, lane, AUG_A, c_t - l_ref[0, 0]).astype(BF16)
        causal = _lower_tri(tq)

        def step(j, dq, masked):
            rows = pl.ds(pl.multiple_of(j * tq, tq), tq)
            kj = k_ref[0, 0, rows, :]
            s = _dot_nt(qb, kj)
            if masked:
                s = jnp.where(causal, s, NEG)
            p = jnp.exp(s)
            ds = (p * _dot_nt(dob, v_ref[0, 0, rows, :])).astype(BF16)
            dv_acc[rows, :] += _dot_tn(p.astype(BF16), dob)
            dk_acc[rows, :] += _dot_tn(ds, qb)
            return dq + _dot(ds, kj)

        dq = lax.fori_loop(0, qi, lambda j, c: step(j, c, False), jnp.zeros((tq, AUG), F32))
        dq_ref[0, 0] = step(qi, dq, True)

        @pl.when(qi == nq - 1)
        def _():
            dk_ref[0, 0] = dk_acc[...]
            dv_ref[0, 0] = dv_acc[...]

    qspec = pl.BlockSpec((1, 1, tq, AUG), lambda b, h, i: (b, h, i, 0))
    kvspec = pl.BlockSpec((1, 1, L, AUG), lambda b, h, i: (b, h, 0, 0))
    return pl.pallas_call(
        kern, name=name, grid=(B, H, nq),
        in_specs=[qspec, kvspec, kvspec, qspec, qspec, pl.BlockSpec((1, 1, tq, 1), lambda b, h, i: (b, h, i, 0))],
        out_specs=[qspec, kvspec, kvspec], out_shape=[jax.ShapeDtypeStruct((B, H, L, AUG), F32)] * 3,
        scratch_shapes=[pltpu.VMEM((L, AUG), F32), pltpu.VMEM((L, AUG), F32)],
        compiler_params=_cparams(("parallel", "parallel", "arbitrary")),
    )(qa, ka, va, o, do, lse)


PAD = SUBLANES


def _conv_fwd(xp, w, b, n_silu, name):
    B, Lp, C = xp.shape
    L = Lp - PAD
    TR = _tile(L, 384, 8)

    def kern(x_ref, w_ref, b_ref, o_ref):
        cb = pl.program_id(1)

        def body(i, carry):
            r0 = pl.multiple_of(i * TR, TR)
            ext = x_ref[0, pl.ds(r0, TR + PAD), :]
            acc = jnp.zeros((TR, LANES), F32) + b_ref[...]
            for k in range(CONV_K):
                s = CONV_K - 1 - k
                sh = ext if s == 0 else pltpu.roll(ext, s, 0)
                acc = acc + w_ref[k:k + 1, :] * sh[PAD:PAD + TR]
            o_ref[0, pl.ds(r0, TR), :] = jnp.where(cb < n_silu, _silu(acc), acc)
            return carry

        lax.fori_loop(0, L // TR, body, 0)

    return pl.pallas_call(
        kern, name=name, grid=(B, C // LANES),
        in_specs=[pl.BlockSpec((1, Lp, LANES), lambda b_, c: (b_, 0, c)),
                  pl.BlockSpec((CONV_K, LANES), lambda b_, c: (0, c)), pl.BlockSpec((1, LANES), lambda b_, c: (0, c))],
        out_specs=pl.BlockSpec((1, L, LANES), lambda b_, c: (b_, 0, c)),
        out_shape=jax.ShapeDtypeStruct((B, L, C), F32),
        compiler_params=_cparams(("parallel", "parallel")),
    )(xp, w, b)


def _conv_bwd_pre(xp, du, w, b, n_silu, name):
    B, Lp, C = xp.shape
    L = Lp - PAD
    TR = _tile(L, 384, 8)

    def kern(x_ref, du_ref, w_ref, b_ref, dp_ref, dw_ref):
        cb = pl.program_id(0)

        @pl.when(pl.program_id(1) == 0)
        def _():
            dw_ref[...] = jnp.zeros_like(dw_ref)

        def body(i, carry):
            r0 = pl.multiple_of(i * TR, TR)
            ext = x_ref[0, pl.ds(r0, TR + PAD), :]
            taps = []
            acc = jnp.zeros((TR, LANES), F32) + b_ref[...]
            for k in range(CONV_K):
                s = CONV_K - 1 - k
                sh = ext if s == 0 else pltpu.roll(ext, s, 0)
                taps.append(sh[PAD:PAD + TR])
                acc = acc + w_ref[k:k + 1, :] * taps[-1]
            dv = du_ref[0, pl.ds(r0, TR), :]
            dpre = jnp.where(cb < n_silu, dv * _dsilu(acc), dv)
            dp_ref[0, pl.ds(r0, TR), :] = dpre
            return tuple(c + _colsum(dpre * t) for c, t in zip(carry[:CONV_K], taps)) + (carry[CONV_K] + _colsum(dpre),)

        z = jnp.zeros((1, LANES), F32)
        sums = lax.fori_loop(0, L // TR, body, (z,) * (CONV_K + 1))
        dp_ref[0, pl.ds(L, PAD), :] = jnp.zeros((PAD, LANES), F32)
        for k in range(CONV_K + 1):
            dw_ref[k:k + 1, :] += sums[k]

    return pl.pallas_call(
        kern, name=name, grid=(C // LANES, B),
        in_specs=[pl.BlockSpec((1, Lp, LANES), lambda c, b_: (b_, 0, c)),
                  pl.BlockSpec((1, L, LANES), lambda c, b_: (b_, 0, c)),
                  pl.BlockSpec((CONV_K, LANES), lambda c, b_: (0, c)), pl.BlockSpec((1, LANES), lambda c, b_: (0, c))],
        out_specs=[pl.BlockSpec((1, Lp, LANES), lambda c, b_: (b_, 0, c)),
                   pl.BlockSpec((SUBLANES, LANES), lambda c, b_: (0, c))],
        out_shape=[jax.ShapeDtypeStruct((B, Lp, C), F32), jax.ShapeDtypeStruct((SUBLANES, C), F32)],
        compiler_params=_cparams(("parallel", "arbitrary")),
    )(xp, du, w, b)


def _conv_bwd_in(dpp, w, name):
    B, Lp, C = dpp.shape
    L = Lp - PAD
    TR = _tile(L, 384, 16)

    def kern(d_ref, w_ref, o_ref):
        def body(i, carry):
            r0 = pl.multiple_of(i * TR, TR)
            ext = d_ref[0, pl.ds(r0, TR + PAD), :]
            acc = jnp.zeros((TR, LANES), F32)
            for k in range(CONV_K):
                s = CONV_K - 1 - k
                sh = ext if s == 0 else pltpu.roll(ext, TR + PAD - s, 0)
                acc = acc + w_ref[k:k + 1, :] * sh[0:TR]
            o_ref[0, pl.ds(r0, TR), :] = acc.astype(BF16)
            return carry

        lax.fori_loop(0, L // TR, body, 0)

    return pl.pallas_call(
        kern, name=name, grid=(B, C // LANES),
        in_specs=[pl.BlockSpec((1, Lp, LANES), lambda b_, c: (b_, 0, c)),
                  pl.BlockSpec((CONV_K, LANES), lambda b_, c: (0, c))],
        out_specs=pl.BlockSpec((1, L, LANES), lambda b_, c: (b_, 0, c)),
        out_shape=jax.ShapeDtypeStruct((B, L, C), BF16),
        compiler_params=_cparams(("parallel", "parallel")),
    )(dpp, w)


def _dot_nt(a, b):
    return lax.dot_general(a, b, (((1,), (1,)), ((), ())), preferred_element_type=F32)


def _dot_tn(a, b):
    return lax.dot_general(a, b, (((0,), (0,)), ((), ())), preferred_element_type=F32)


def _dot(a, b):
    return jnp.dot(a, b, preferred_element_type=F32)


def _ssd_specs(L, nc, b_blk, c_blk):
    E = HEADS // SSD_GROUPS
    return [
        pl.BlockSpec((1, 1, L, HEAD_DIM), lambda b, h: (b, h, 0, 0)),
        pl.BlockSpec((1, L, SSD_STATE), lambda b, h: (b, 0, b_blk + h // E)),
        pl.BlockSpec((1, L, SSD_STATE), lambda b, h: (b, 0, c_blk + h // E)),
        pl.BlockSpec((1, L, LANES), lambda b, h: (b, 0, 0)),
        pl.BlockSpec((1, L, LANES), lambda b, h: (b, 0, 0)),
        pl.BlockSpec((1, 1, nc, Q_BLOCK), lambda b, h: (b, HEADS + h, 0, 0)),
        pl.BlockSpec((1, LANES), lambda b, h: (0, 0)),
    ]


def _ssd_chunk(c, S, x_ref, b_ref, c_ref, v_ref, cu_ref, ct_ref, lane):
    Q = Q_BLOCK
    rows = pl.ds(pl.multiple_of(c * Q, Q), Q)
    xc = x_ref[0, 0, rows, :]
    Bb = b_ref[0, rows, :].astype(BF16)
    Cb = c_ref[0, rows, :].astype(BF16)
    dt = _lane_col(v_ref[0, rows, :], lane)
    A = _lane_col(cu_ref[0, rows, :], lane)
    Ar = ct_ref[0, 0, pl.ds(c, 1), :]
    Aend = _lane_col(Ar, Q - 1)
    xdt = xc * dt
    Lm = jnp.exp(jnp.where(_lower_tri(Q), A - Ar, NEG))
    CB = _dot_nt(Cb, Bb)
    e_end = jnp.exp(Aend - A)
    W = xdt * e_end
    S_new = S * jnp.exp(Aend) + _dot_tn(W.astype(BF16), Bb)
    return dict(rows=rows, xc=xc, Bb=Bb, Cb=Cb, dt=dt, A=A, Aend=Aend, xdt=xdt, Lm=Lm, CB=CB, e_end=e_end, W=W,
                S_new=S_new)


def _ssd_fwd(x4, u, b_blk, c_blk, vals, cums, cums_t, dvec, name):
    B, H, L, P = x4.shape
    nc = L // Q_BLOCK

    def kern(x_ref, b_ref, c_ref, v_ref, cu_ref, ct_ref, d_ref, y_ref):
        lane = HEADS + pl.program_id(1)
        dskip = _lane_col(d_ref[...], lane)

        def body(c, S):
            q = _ssd_chunk(c, S, x_ref, b_ref, c_ref, v_ref, cu_ref, ct_ref, lane)
            yd = _dot((q["CB"] * q["Lm"]).astype(BF16), q["xdt"].astype(BF16))
            z = _dot_nt(q["Cb"], S.astype(BF16))
            y_ref[0, 0, q["rows"], :] = yd + z * jnp.exp(q["A"]) + dskip * q["xc"]
            return q["S_new"]

        lax.fori_loop(0, nc, body, jnp.zeros((P, SSD_STATE), F32))

    return pl.pallas_call(
        kern, name=name, grid=(B, H), in_specs=_ssd_specs(L, nc, b_blk, c_blk),
        out_specs=pl.BlockSpec((1, 1, L, P), lambda b, h: (b, h, 0, 0)),
        out_shape=jax.ShapeDtypeStruct((B, H, L, P), F32),
        compiler_params=_cparams(("parallel", "arbitrary")),
    )(x4, u, u, vals, cums, cums_t, dvec)


def _ssd_bwd(x4, u, b_blk, c_blk, vals, cums, cums_t, dvec, dy4, name):
    B, H, L, P = x4.shape
    Q = Q_BLOCK
    nc = L // Q
    N = SSD_STATE
    E = HEADS // SSD_GROUPS

    def kern(x_ref, b_ref, c_ref, v_ref, cu_ref, ct_ref, d_ref, dy_ref,
             dx_ref, dB_ref, dC_ref, ddt_ref, dAc_ref, dAr_ref, dD_ref, s_all):
        b = pl.program_id(0)
        h = pl.program_id(1)
        lane = HEADS + h
        dskip = _lane_col(d_ref[...], lane)
        onehot = (lax.broadcasted_iota(jnp.int32, (1, LANES), 1) == lane).astype(F32)

        @pl.when(h % E == 0)
        def _():
            dB_ref[...] = jnp.zeros_like(dB_ref)
            dC_ref[...] = jnp.zeros_like(dC_ref)

        @pl.when(h == 0)
        def _():
            ddt_ref[...] = jnp.zeros_like(ddt_ref)
            dAc_ref[...] = jnp.zeros_like(dAc_ref)

        @pl.when((b == 0) & (h == 0))
        def _():
            dD_ref[...] = jnp.zeros_like(dD_ref)

        def fwd(c, S):
            s_all[c] = S
            return _ssd_chunk(c, S, x_ref, b_ref, c_ref, v_ref, cu_ref, ct_ref, lane)["S_new"]

        lax.fori_loop(0, nc, fwd, jnp.zeros((P, N), F32))
        last_row = lax.broadcasted_iota(jnp.int32, (Q, 1), 0) == Q - 1

        def bwd(i, carry):
            dS, dD = carry
            c = nc - 1 - i
            S = s_all[c]
            q = _ssd_chunk(c, S, x_ref, b_ref, c_ref, v_ref, cu_ref, ct_ref, lane)
            rows, xc, Bb, Cb, xdt, Lm, CB = q["rows"], q["xc"], q["Bb"], q["Cb"], q["xdt"], q["Lm"], q["CB"]
            eA = jnp.exp(q["A"])
            eAend = jnp.exp(q["Aend"])
            dy = dy_ref[0, 0, rows, :]
            dyb = dy.astype(BF16)
            Sb = S.astype(BF16)
            dD = dD + jnp.sum(jnp.sum(dy * xc, axis=1, keepdims=True), axis=0, keepdims=True)
            dM = _dot_nt(dyb, xdt.astype(BF16))
            dxdt = _dot_tn((CB * Lm).astype(BF16), dyb)
            dCBb = (dM * Lm).astype(BF16)
            G = dM * CB * Lm
            dAc = jnp.sum(G, axis=1, keepdims=True)
            dAr = -jnp.sum(G, axis=0, keepdims=True)
            dC = _dot(dCBb, Bb)
            dBm = _dot_tn(dCBb, Cb)
            z = _dot_nt(Cb, Sb)
            dAc = dAc + jnp.sum(dy * z, axis=1, keepdims=True) * eA
            dzb = (dy * eA).astype(BF16)
            dC = dC + _dot(dzb, Sb)
            dS_in = _dot_tn(dzb, Cb)
            dSb = dS.astype(BF16)
            dW = _dot_nt(Bb, dSb)
            dBm = dBm + _dot(q["W"].astype(BF16), dSb)
            dxdt = dxdt + dW * q["e_end"]
            de = jnp.sum(dW * xdt, axis=1, keepdims=True) * q["e_end"]
            dAend = (jnp.sum(jnp.sum(dS * S, axis=1, keepdims=True), axis=0, keepdims=True) * eAend
                     + jnp.sum(de, axis=0, keepdims=True))
            dAc = dAc - de + jnp.where(last_row, dAend, 0.0)
            dx_ref[0, 0, rows, :] = dskip * dy + dxdt * q["dt"]
            dB_ref[0, 0, rows, :] += dBm
            dC_ref[0, 0, rows, :] += dC
            ddt_ref[0, rows, :] += jnp.sum(dxdt * xc, axis=1, keepdims=True) * onehot
            dAc_ref[0, rows, :] += dAc * onehot
            dAr_ref[0, 0, pl.ds(c, 1), :] = dAr
            return dS * eAend + dS_in, dD

        _, dD = lax.fori_loop(0, nc, bwd, (jnp.zeros((P, N), F32), jnp.zeros((1, 1), F32)))
        dD_ref[...] += dD * onehot

    tm = pl.BlockSpec((1, L, LANES), lambda b, h: (b, 0, 0))
    grp = pl.BlockSpec((1, 1, L, N), lambda b, h: (b, h // E, 0, 0))
    xs = pl.BlockSpec((1, 1, L, P), lambda b, h: (b, h, 0, 0))
    return pl.pallas_call(
        kern, name=name, grid=(B, H), in_specs=_ssd_specs(L, nc, b_blk, c_blk) + [xs],
        out_specs=[xs, grp, grp, tm, tm, pl.BlockSpec((1, 1, nc, Q), lambda b, h: (b, h, 0, 0)),
                   pl.BlockSpec((1, LANES), lambda b, h: (0, 0))],
        out_shape=[jax.ShapeDtypeStruct((B, H, L, P), F32), jax.ShapeDtypeStruct((B, SSD_GROUPS, L, N), F32),
                   jax.ShapeDtypeStruct((B, SSD_GROUPS, L, N), F32), jax.ShapeDtypeStruct((B, L, LANES), F32),
                   jax.ShapeDtypeStruct((B, L, LANES), F32), jax.ShapeDtypeStruct((B, H, nc, Q), F32),
                   jax.ShapeDtypeStruct((1, LANES), F32)],
        scratch_shapes=[pltpu.VMEM((nc, P, N), F32)],
        compiler_params=_cparams(("arbitrary", "arbitrary")),
    )(x4, u, u, vals, cums, cums_t, dvec, dy4)


LRU_TR = 384
LRU_CB = 512


def _lru_gates(xc, ra, ix, p_ref, first):
    r = _sigmoid(ra + p_ref[0:1, :])
    i = _sigmoid(ix + p_ref[1:2, :])
    ls = _log_sigmoid(p_ref[2:3, :])
    log_a = LRU_C * r * ls
    a = jnp.exp(log_a)
    mult0 = jnp.sqrt(_one_minus_exp(2.0 * log_a))
    mult = jnp.where(first, 1.0, mult0)
    return r, i, ls, a, mult0, mult


def _lru_fwd(u, xc_off, ra, ix, proj3, gate_off, pvec, name):
    B, L, D = ra.shape
    TR, CB = _tile(L, LRU_TR, 8), LRU_CB
    nrt = L // TR

    def kern(xc_ref, ra_ref, ix_ref, g_ref, p_ref, y_ref, hs_ref, a_ref, pa_s, pu_s, carry):
        rt = pl.program_id(2)

        @pl.when(rt == 0)
        def _():
            carry[...] = jnp.zeros_like(carry)

        row = lax.broadcasted_iota(jnp.int32, (TR, 1), 0)
        first = (rt == 0) & (row == 0)
        xc = xc_ref[0]
        r, i, ls, a, mult0, mult = _lru_gates(xc, ra_ref[0], ix_ref[0], p_ref, first)
        a_ref[0] = a
        pa, pu = a, mult * (i * xc)
        sub = row % SUBLANES
        for s in (1, 2, 4):
            ok = sub >= s
            pu = jnp.where(ok, pa * pltpu.roll(pu, s, 0) + pu, pu)
            pa = jnp.where(ok, pa * pltpu.roll(pa, s, 0), pa)
        pa_s[...] = pa
        pu_s[...] = pu
        row8 = lax.broadcasted_iota(jnp.int32, (SUBLANES, 1), 0)

        def gbody(g, c):
            r8 = pl.ds(pl.multiple_of(g * SUBLANES, SUBLANES), SUBLANES)
            hg = pa_s[r8, :] * c + pu_s[r8, :]
            hs_ref[0, r8, :] = hg
            return jnp.sum(jnp.where(row8 == SUBLANES - 1, hg, 0.0), axis=0, keepdims=True)

        carry[...] = lax.fori_loop(0, TR // SUBLANES, gbody, carry[...])
        y_ref[0] = (hs_ref[0] * _gelu(g_ref[0])).astype(BF16)

    def win(off):
        assert off % CB == 0
        return pl.BlockSpec((1, TR, CB), functools.partial(lambda b, j, t, o: (b, t, j + o), o=off // CB))

    return pl.pallas_call(
        kern, name=name, grid=(B, D // CB, nrt),
        in_specs=[win(xc_off), win(0), win(0), win(gate_off), pl.BlockSpec((SUBLANES, CB), lambda b, j, t: (0, j))],
        out_specs=[win(0)] * 3,
        out_shape=[jax.ShapeDtypeStruct((B, L, D), BF16), jax.ShapeDtypeStruct((B, L, D), F32),
                   jax.ShapeDtypeStruct((B, L, D), F32)],
        scratch_shapes=[pltpu.VMEM((TR, CB), F32), pltpu.VMEM((TR, CB), F32), pltpu.VMEM((1, CB), F32)],
        compiler_params=_cparams(("parallel", "parallel", "arbitrary")),
    )(u, ra, ix, proj3, pvec)


def _lru_bwd(dy, proj3, gate_off, hs, a, u, xc_off, ra, ix, pvec, name):
    B, L, D = ra.shape
    TR, CB = _tile(L, LRU_TR, 8), LRU_CB
    nrt = L // TR

    def kern(dy_ref, g_ref, hs_ref, hsp_ref, a_ref, an_ref, xc_ref, ra_ref, ix_ref, p_ref,
             dg_ref, dra_ref, dix_ref, dxc_ref, dp_ref, pb_s, pd_s, g_s, carry):
        b = pl.program_id(1)
        rt = pl.program_id(2)
        t = nrt - 1 - rt

        @pl.when((b == 0) & (rt == 0))
        def _():
            dp_ref[...] = jnp.zeros_like(dp_ref)

        @pl.when(rt == 0)
        def _():
            carry[...] = jnp.zeros_like(carry)

        row = lax.broadcasted_iota(jnp.int32, (TR, 1), 0)
        gate, hsv, av, dyv = g_ref[0], hs_ref[0], a_ref[0], dy_ref[0]
        dg_ref[0] = (dyv * hsv * _dgelu(gate)).astype(BF16)
        a_next = jnp.where(t == nrt - 1, 0.0, an_ref[0, 0:1, :])
        pb = jnp.where(row == TR - 1, a_next, pltpu.roll(av, TR - 1, 0))
        pd = dyv * _gelu(gate)
        sub = row % SUBLANES
        for s in (1, 2, 4):
            ok = sub < SUBLANES - s
            pd = jnp.where(ok, pd + pb * pltpu.roll(pd, TR - s, 0), pd)
            pb = jnp.where(ok, pb * pltpu.roll(pb, TR - s, 0), pb)
        pb_s[...] = pb
        pd_s[...] = pd
        row8 = lax.broadcasted_iota(jnp.int32, (SUBLANES, 1), 0)

        def gbody(i, c):
            r8 = pl.ds(pl.multiple_of((TR // SUBLANES - 1 - i) * SUBLANES, SUBLANES), SUBLANES)
            gg = pd_s[r8, :] + pb_s[r8, :] * c
            g_s[r8, :] = gg
            return jnp.sum(jnp.where(row8 == 0, gg, 0.0), axis=0, keepdims=True)

        carry[...] = lax.fori_loop(0, TR // SUBLANES, gbody, carry[...])
        gv = g_s[...]
        h_first = jnp.where(t == 0, 0.0, hsp_ref[0, TR - 1:TR, :])
        hprev = jnp.where(row == 0, h_first, pltpu.roll(hsv, 1, 0))
        first = (t == 0) & (row == 0)
        xc = xc_ref[0]
        r, i, ls, a2, mult0, mult = _lru_gates(xc, ra_ref[0], ix_ref[0], p_ref, first)
        dxc_ref[0] = gv * mult * i
        dlog_a = gv * hprev * av + jnp.where(first, 0.0, gv * i * xc * (-(av * av) / mult0))
        dra = dlog_a * LRU_C * ls * r * (1.0 - r)
        dix = gv * mult * xc * i * (1.0 - i)
        dra_ref[0] = dra.astype(BF16)
        dix_ref[0] = dix.astype(BF16)
        dp_ref[0:1, :] += _colsum(dra)
        dp_ref[1:2, :] += _colsum(dix)
        dp_ref[2:3, :] += _colsum(dlog_a * LRU_C * r) * _sigmoid(-p_ref[2:3, :])

    def win(off, shift=0):
        assert off % CB == 0
        o = off // CB
        return pl.BlockSpec((1, TR, CB), lambda j, b, rt: (b, jnp.clip(nrt - 1 - rt + shift, 0, nrt - 1), j + o))

    return pl.pallas_call(
        kern, name=name, grid=(D // CB, B, nrt),
        in_specs=[win(0), win(gate_off), win(0), win(0, -1), win(0), win(0, 1), win(xc_off), win(0), win(0),
                  pl.BlockSpec((SUBLANES, CB), lambda j, b, rt: (0, j))],
        out_specs=[win(0)] * 4 + [pl.BlockSpec((SUBLANES, CB), lambda j, b, rt: (0, j))],
        out_shape=[jax.ShapeDtypeStruct((B, L, D), BF16)] * 3 + [jax.ShapeDtypeStruct((B, L, D), F32),
                                                                 jax.ShapeDtypeStruct((SUBLANES, D), F32)],
        scratch_shapes=[pltpu.VMEM((TR, CB), F32)] * 3 + [pltpu.VMEM((1, CB), F32)],
        compiler_params=_cparams(("parallel", "arbitrary", "arbitrary")),
    )(dy, proj3, hs, hs, a, a, u, ra, ix, pvec)


def _sum8(parts, name):
    _, R, C = parts.shape
    tr = _tile(R, 1024, 8)

    def kern(p_ref, o_ref):
        acc = p_ref[0]
        for d in range(1, N_DEV):
            acc = acc + p_ref[d]
        o_ref[...] = acc

    return pl.pallas_call(
        kern, name=name, grid=(R // tr,), in_specs=[pl.BlockSpec((N_DEV, tr, C), lambda i: (0, i, 0))],
        out_specs=pl.BlockSpec((tr, C), lambda i: (i, 0)), out_shape=jax.ShapeDtypeStruct((R, C), F32),
        compiler_params=_cparams(("parallel",)),
    )(parts)


def _adamw(w, g, m, v, name):
    shape = w.shape
    C = shape[-1] if w.ndim > 1 else shape[0]
    R = w.size // C
    w2, g2, m2, v2 = (t.reshape(R, C) for t in (w, g, m, v))
    tr = R
    for cand in range(8, min(R, 512) + 1, 8):
        if R % cand == 0:
            tr = cand

    def kern(w_ref, g_ref, m_ref, v_ref, d_ref, nm_ref, nv_ref):
        gv = g_ref[...]
        nm = ADAM_B1 * m_ref[...] + (1.0 - ADAM_B1) * gv
        nv = ADAM_B2 * v_ref[...] + (1.0 - ADAM_B2) * (gv * gv)
        m_hat = nm / (1.0 - ADAM_B1 ** ADAM_STEP)
        v_hat = nv / (1.0 - ADAM_B2 ** ADAM_STEP)
        d_ref[...] = -ADAM_LR * (m_hat / (jnp.sqrt(v_hat) + ADAM_EPS) + ADAM_WD * w_ref[...])
        nm_ref[...] = nm
        nv_ref[...] = nv

    spec = pl.BlockSpec((tr, C), lambda i: (i, 0))
    outs = pl.pallas_call(
        kern, name=name, grid=(R // tr,), in_specs=[spec] * 4, out_specs=[spec] * 3,
        out_shape=[jax.ShapeDtypeStruct((R, C), F32)] * 3, compiler_params=_cparams(("parallel",)),
    )(w2, g2, m2, v2)
    return tuple(o.reshape(shape) for o in outs)


MESH_ID = pl.DeviceIdType.MESH
ANY = pl.BlockSpec(memory_space=pl.ANY)


def _my_place():
    return lax.axis_index("x"), lax.axis_index("y"), lax.axis_index("c")


def _all_gather(xs, name):
    R, C = xs.shape

    def body(x_ref, out_ref, send_sems, recv_sems, local_sem):
        x, y, c = _my_place()
        me, sibling = (x, y, c), (x, y, 1 - c)
        chips = [(1 - x, y), (x, 1 - y), (1 - x, 1 - y)]

        def slab(px, py, pc):
            return out_ref.at[4 * px + 2 * py + pc]

        def copy(k, block, to, src=None):
            return pltpu.make_async_remote_copy(
                src_ref=slab(*block) if src is None else src, dst_ref=slab(*block),
                send_sem=send_sems.at[k], recv_sem=recv_sems.at[k], device_id=to, device_id_type=MESH_ID)

        mine = pltpu.make_async_copy(x_ref, slab(*me), local_sem)
        mine.start()
        first = [copy(0, me, sibling, src=x_ref)]
        first += [copy(1 + j, me, (*chip, c), src=x_ref) for j, chip in enumerate(chips)]
        for cp in first:
            cp.start()
        passed = [copy(4 + j, (*chip, c), sibling) for j, chip in enumerate(chips)]
        for j, chip in enumerate(chips):
            copy(1 + j, (*chip, c), me).wait_recv()
            passed[j].start()
        copy(0, sibling, me).wait_recv()
        for j, chip in enumerate(chips):
            copy(4 + j, (*chip, 1 - c), me).wait_recv()
        for cp in first + passed:
            cp.wait_send()
        mine.wait()

    return pl.pallas_call(
        body, name=name, out_shape=jax.ShapeDtypeStruct((N_DEV, R, C), xs.dtype), in_specs=[ANY], out_specs=ANY,
        scratch_shapes=[pltpu.SemaphoreType.DMA((7,)), pltpu.SemaphoreType.DMA((7,)), pltpu.SemaphoreType.DMA],
    )(xs)


def _exchange(parts, name):
    _, R, C = parts.shape

    def body(p_ref, out_ref, send_sems, recv_sems, local_sem):
        x, y, c = _my_place()
        my_idx = 4 * x + 2 * y + c
        mine = pltpu.make_async_copy(p_ref.at[my_idx], out_ref.at[my_idx], local_sem)
        mine.start()
        copies = []
        for k in range(1, N_DEV):
            px, py, pc = x ^ (k >> 2), y ^ ((k >> 1) & 1), c ^ (k & 1)
            copies.append(pltpu.make_async_remote_copy(
                src_ref=p_ref.at[4 * px + 2 * py + pc], dst_ref=out_ref.at[my_idx],
                send_sem=send_sems.at[k - 1], recv_sem=recv_sems.at[k - 1], device_id=(px, py, pc),
                device_id_type=MESH_ID))
        for cp in copies:
            cp.start()
        for cp in copies:
            cp.wait()
        mine.wait()

    return pl.pallas_call(
        body, name=name, out_shape=jax.ShapeDtypeStruct((N_DEV, R, C), parts.dtype), in_specs=[ANY], out_specs=ANY,
        scratch_shapes=[pltpu.SemaphoreType.DMA((7,)), pltpu.SemaphoreType.DMA((7,)), pltpu.SemaphoreType.DMA],
    )(parts)


D_XBC_EXTRA = 2 * SSD_GROUPS * SSD_STATE
SMALL_W = LANES
ROW_ALIGN = 16


def _layout(D):
    d_xbc = D + D_XBC_EXTRA
    off = dict(qkv=0, z=3 * D, merge=4 * D, gate=7 * D, conv=8 * D, xr=8 * D + d_xbc, small=9 * D + d_xbc)
    off["n_all"] = off["small"] + SMALL_W
    off["d_xbc"] = d_xbc
    off["conv_c"] = d_xbc + D
    return off


def _w_in_map(D):
    lo = _layout(D)
    widths = [("q", D, 0), ("k", D, D), ("v", D, 2 * D), ("f", HEADS, lo["small"]), ("z", D, lo["z"]),
              ("xbc", lo["d_xbc"], lo["conv"]), ("dt", HEADS, lo["small"] + HEADS), ("xr", D, lo["xr"]),
              ("gate", D, lo["gate"]), ("merge", 3 * D, lo["merge"])]
    out, o = [], 0
    for _, w, mine in widths:
        out.append((o, w, mine))
        o += w
    return out


def _padded(c):
    return -(-c // ROW_ALIGN) * ROW_ALIGN


def _permute_rows(src, pieces, name):
    R, C = src.shape
    n_out = sum(n for _, n in pieces)

    def kern(x_ref, o_ref):
        o = 0
        for start, n in pieces:
            if start is None:
                o_ref[o:o + n, :] = jnp.zeros((n, LANES), src.dtype)
            else:
                o_ref[o:o + n, :] = x_ref[start:start + n, :]
            o += n

    return pl.pallas_call(
        kern, name=name, grid=(C // LANES,), in_specs=[pl.BlockSpec((R, LANES), lambda i: (0, i))],
        out_specs=pl.BlockSpec((n_out, LANES), lambda i: (0, i)), out_shape=jax.ShapeDtypeStruct((n_out, C), src.dtype),
        compiler_params=_cparams(("parallel",)),
    )(src)


def _reorder_rows(wt, D, c, name="reorder_w_in"):
    cp = _padded(c)
    lo = _layout(D)
    pieces = []
    for a, w, mine in sorted(_w_in_map(D), key=lambda t: t[2]):
        b = a + w
        while a < b:
            j = a // c
            e = min(b, (j + 1) * c)
            pieces.append((j * cp + a - j * c, e - a))
            a = e
    pieces.append((None, lo["n_all"] - lo["small"] - 2 * HEADS))
    return _permute_rows(wt, pieces, name)


def _restore_rows(dwt, D, c, name="restore_w_in"):
    cp = _padded(c)
    segs = _w_in_map(D)
    pieces = []
    for j in range(N_DEV):
        a, b = j * c, (j + 1) * c
        for s0, w, mine in segs:
            lo_, hi_ = max(a, s0), min(b, s0 + w)
            if lo_ < hi_:
                pieces.append((mine + lo_ - s0, hi_ - lo_))
        if cp > c:
            pieces.append((None, cp - c))
    return _permute_rows(dwt, pieces, name)


def _block_diag(w):
    H, n, _ = w.shape
    tiled = jnp.tile(w.reshape(H * n, n), (1, H))
    r = lax.broadcasted_iota(jnp.int32, (H * n, H * n), 0) // n
    c = lax.broadcasted_iota(jnp.int32, (H * n, H * n), 1) // n
    return jnp.where(r == c, tiled, jnp.zeros_like(tiled))


def _diag_blocks(m, H):
    n = m.shape[0] // H
    return jnp.stack([m[h * n:(h + 1) * n, h * n:(h + 1) * n] for h in range(H)])


def _to_heads(t, B, L):
    return t.reshape(B, L, HEADS, HEAD_DIM).transpose(0, 2, 1, 3)


def _from_heads(t4):
    B, H, L, P = t4.shape
    return t4.transpose(0, 2, 1, 3).reshape(B * L, H * P)


def _rows_to_tm(rows):
    B, H, nc, Q = rows.shape
    return rows.reshape(B, H, nc * Q).transpose(0, 2, 1)


def _ffn_fwd(h, g, wgu_t, wd, tag):
    n = _norm_fwd(h, g, tag + "_norm")
    gu = _mm(n, wgu_t, tb=True, name=tag + "_up")
    act = _swiglu_fwd(gu, tag + "_act")
    out = _mm(act, wd, res=h, scale=0.5, name=tag + "_down")
    return out, (h, n, gu, act)


def _ffn_bwd(dh, saved, g, wgu_t, wd, tag):
    h, n, gu, act = saved
    dact = _mm(dh, wd, tb=True, scale=0.5, name=tag + "_down_dx")
    dwd = _mm(act, dh, ta=True, scale=0.5, name=tag + "_down_dw")
    dgu = _swiglu_bwd(gu, dact, tag + "_act_bwd")
    dwgu_t = _mm(dgu, n, ta=True, tn=1024, name=tag + "_up_dw")
    dn = _mm(dgu, wgu_t, name=tag + "_up_dx")
    dh_in, dg = _norm_bwd(h, dn, dh, g, tag + "_norm_bwd")
    return dh_in, dict(norm=dg, gu=dwgu_t, down=dwd)


def _augment(proj, cums, B, L, D):
    scale = HEAD_DIM ** -0.5
    q4 = _to_heads(proj[:, :D] * scale, B, L).astype(BF16)
    k4 = _to_heads(proj[:, D:2 * D], B, L).astype(BF16)
    v4 = _to_heads(proj[:, 2 * D:3 * D], B, L).astype(BF16)
    c = cums[..., :HEADS].transpose(0, 2, 1)
    rnd = lambda t: lax.reduce_precision(t, 8, 7)
    c_hi = rnd(c)
    c_mid = rnd(c - c_hi)
    c3 = jnp.stack([c_hi, c_mid, rnd(c - c_hi - c_mid)], axis=-1).astype(BF16)
    ones = jnp.ones((B, HEADS, L, 3), BF16)
    fill = lambda n: jnp.zeros((B, HEADS, L, n), BF16)
    qa = jnp.concatenate([q4, c3, ones, fill(AUG - AUG_B - 3)], axis=-1)
    ka = jnp.concatenate([k4, ones, -c3, fill(AUG - AUG_B - 3)], axis=-1)
    va = jnp.concatenate([v4, ones, fill(AUG - AUG_B)], axis=-1)
    return qa, ka, va


def _mixer_fwd(h, p, B, L):
    T, D = h.shape
    lo = _layout(D)
    n = _norm_fwd(h, p["gm"], "mix_norm")
    proj = _mm(n, p["w_all_t"], tb=True, name="mix_in")
    proj3 = proj.reshape(B, L, lo["n_all"])
    vals, cums = _gate_prep(proj3, lo["small"] // LANES, p["small_bias"], p["avec"], "gate_prep")
    cums_t = cums[..., :2 * HEADS].transpose(0, 2, 1).reshape(B, 2 * HEADS, L // Q_BLOCK, Q_BLOCK)
    qa, ka, va = _augment(proj, cums, B, L, D)
    o4, lse = _attn_fwd(qa, ka, va, "attn_fwd")
    y_a = _from_heads(o4[..., :HEAD_DIM])
    xp = jnp.pad(proj3[:, :, lo["conv"]:lo["conv"] + lo["conv_c"]], ((0, 0), (PAD, 0), (0, 0)))
    u = _conv_fwd(xp, p["conv_w"], p["conv_b"], lo["d_xbc"] // LANES, "conv_fwd")
    x4 = _to_heads(u[..., :D], B, L)
    b_blk = D // LANES
    c_blk = b_blk + SSD_GROUPS * SSD_STATE // LANES
    y4 = _ssd_fwd(x4, u, b_blk, c_blk, vals, cums, cums_t, p["dvec"], "ssd_fwd")
    y_s = _from_heads(y4)
    yb = _gnorm_fwd(y_s, proj, lo["z"], p["ssd_norm"], "gnorm_fwd")
    u2 = u.reshape(T, lo["conv_c"])
    ra = _mm(u2, p["wa"], a_off=(0, lo["d_xbc"]), dims=(T, D, D), tk=512, name="lru_ra")
    ix = _mm(u2, p["wx"], a_off=(0, lo["d_xbc"]), dims=(T, D, D), tk=512, name="lru_ix")
    yc, hs, a = _lru_fwd(u, lo["d_xbc"], ra.reshape(B, L, D), ix.reshape(B, L, D), proj3, lo["gate"], p["pvec"],
                         "lru_fwd")
    yc = yc.reshape(T, D)
    pa = _mm(y_a, p["wba"], name="branch_attn")
    pb = _mm(yb, p["wbs"], name="branch_ssd")
    pc = _mm(yc, p["wbl"], name="branch_lru")
    mixed = _merge_fwd(proj, lo["merge"], pa, pb, pc, "merge_fwd")
    out = _mm(mixed, p["wout"], res=h, name="mix_out")
    saved = dict(h=h, n=n, proj=proj, qa=qa, ka=ka, va=va, vals=vals, cums=cums, cums_t=cums_t, o4=o4, lse=lse, y_a=y_a,
                 xp=xp, u=u, x4=x4, y_s=y_s, yb=yb, ra=ra, ix=ix, yc=yc, hs=hs, a=a, pa=pa, pb=pb, pc=pc, mixed=mixed)
    return out, saved


def _mixer_bwd(dh, s, p, B, L):
    T, D = dh.shape
    lo = _layout(D)
    proj, u = s["proj"], s["u"]
    proj3 = proj.reshape(B, L, lo["n_all"])
    g = {}
    dmixed = _mm(dh, p["wout"], tb=True, name="mix_out_dx")
    g["wout"] = _mm(s["mixed"], dh, ta=True, name="mix_out_dw")
    dpa, dpb, dpc, dmerge = _merge_bwd(dmixed, proj, lo["merge"], s["pa"], s["pb"], s["pc"], "merge_bwd")
    dy_a = _mm(dpa, p["wba"], tb=True, name="branch_attn_dx")
    g["wba"] = _mm(s["y_a"], dpa, ta=True, name="branch_attn_dw")
    dyb = _mm(dpb, p["wbs"], tb=True, name="branch_ssd_dx")
    g["wbs"] = _mm(s["yb"], dpb, ta=True, name="branch_ssd_dw")
    dyc = _mm(dpc, p["wbl"], tb=True, name="branch_lru_dx")
    g["wbl"] = _mm(s["yc"], dpc, ta=True, name="branch_lru_dw")
    dgate, dra, dix, dxc, g["pvec"] = _lru_bwd(dyc.reshape(B, L, D), proj3, lo["gate"], s["hs"], s["a"], u, lo["d_xbc"],
                                               s["ra"].reshape(B, L, D), s["ix"].reshape(B, L, D), p["pvec"], "lru_bwd")
    dra, dix = dra.reshape(T, D), dix.reshape(T, D)
    u2 = u.reshape(T, lo["conv_c"])
    g["wa"] = _mm(u2, dra, ta=True, a_off=(0, lo["d_xbc"]), dims=(D, D, T), tm=512, name="lru_ra_dw")
    g["wx"] = _mm(u2, dix, ta=True, a_off=(0, lo["d_xbc"]), dims=(D, D, T), tm=512, name="lru_ix_dw")
    dxc = _mm(dra, p["wa"], tb=True, res=dxc.reshape(T, D), name="lru_ra_dx")
    dxc = _mm(dix, p["wx"], tb=True, res=dxc, name="lru_ix_dx")
    dy_s, dz, g["ssd_norm"] = _gnorm_bwd(dyb, s["y_s"], proj, lo["z"], p["ssd_norm"], "gnorm_bwd")
    b_blk = D // LANES
    c_blk = b_blk + SSD_GROUPS * SSD_STATE // LANES
    dx4, dBg, dCg, ddt_tm, dAc_tm, dAr, g["dvec"] = _ssd_bwd(s["x4"], u, b_blk, c_blk, s["vals"], s["cums"], s["cums_t"],
                                                             p["dvec"], _to_heads(dy_s, B, L), "ssd_bwd")
    grp = lambda t: t.transpose(0, 2, 1, 3).reshape(B, L, SSD_GROUPS * SSD_STATE)
    du = jnp.concatenate([_from_heads(dx4).reshape(B, L, D), grp(dBg), grp(dCg), dxc.reshape(B, L, D)], axis=-1)
    dpp, g["conv_wb"] = _conv_bwd_pre(s["xp"], du, p["conv_w"], p["conv_b"], lo["d_xbc"] // LANES, "conv_bwd_pre")
    dconv = _conv_bwd_in(dpp, p["conv_w"], "conv_bwd_in")
    do4 = jnp.pad(_to_heads(dy_a, B, L), ((0, 0), (0, 0), (0, 0), (0, AUG - HEAD_DIM)))
    dqa, dka, dva = _attn_bwd(s["qa"], s["ka"], s["va"], s["o4"], do4, s["lse"], "attn_bwd")
    scale = HEAD_DIM ** -0.5
    dqkv = jnp.stack([(dqa[..., :HEAD_DIM] * scale).astype(BF16), dka[..., :HEAD_DIM].astype(BF16),
                      dva[..., :HEAD_DIM].astype(BF16)], 0).transpose(1, 3, 0, 2, 4).reshape(T, 3 * D)
    dc_tm = (dqa[..., AUG_A] - dka[..., AUG_B]).transpose(0, 2, 1)
    drow_tm = jnp.concatenate([dc_tm, _rows_to_tm(dAr), jnp.zeros((B, L, LANES - 2 * HEADS), F32)], axis=-1)
    dsmall, g["small_bias"], g["avec"] = _gate_post(drow_tm, dAc_tm, ddt_tm, proj3, lo["small"] // LANES, s["vals"],
                                                    p["small_bias"], p["avec"], "gate_post")
    dproj = jnp.concatenate([dqkv, dz, dmerge, dgate.reshape(T, D), dconv.reshape(T, lo["conv_c"]),
                             dsmall.reshape(T, SMALL_W)], axis=1)
    g["w_all_t"] = _mm(dproj, s["n"], ta=True, tn=1024, name="mix_in_dw")
    dn = _mm(dproj, p["w_all_t"], name="mix_in_dx")
    dh_in, g["gm"] = _norm_bwd(s["h"], dn, dh, p["gm"], "mix_norm_bwd")
    return dh_in, g


def _small_vec(a, b):
    return jnp.concatenate([a, b, jnp.zeros((LANES - 2 * HEADS,), F32)])[None, :]


def _layer_params(w):
    zeros16 = jnp.zeros((HEADS,), F32)
    pvec = jnp.concatenate([w["lru_b_a"][None], w["lru_b_x"][None], w["lru_lambda"][None],
                            jnp.zeros((SUBLANES - 3, w["lru_b_a"].shape[0]), F32)], axis=0)
    return dict(
        g1=w["ffn1_norm"][None], gu1=w["ffn1_w_gate_up"], d1=w["ffn1_w_down"],
        gm=w["mix_norm"][None], w_all_t=w["w_in"],
        small_bias=_small_vec(w["fox_forget_bias"], w["ssd_dt_bias"]),
        avec=_small_vec(zeros16, -jnp.exp(w["ssd_a_log"])), dvec=_small_vec(zeros16, w["ssd_d"]),
        conv_w=jnp.concatenate([w["ssd_conv_w"], w["lru_conv_w"]], axis=1),
        conv_b=jnp.concatenate([w["ssd_conv_b"], w["lru_conv_b"]])[None],
        ssd_norm=w["ssd_norm"][None],
        wa=_block_diag(w["lru_w_a"]).astype(BF16), wx=_block_diag(w["lru_w_x"]).astype(BF16), pvec=pvec,
        wba=w["w_branch_attn"], wbs=w["w_branch_ssd"], wbl=w["w_branch_lru"], wout=w["w_out"],
        g2=w["ffn2_norm"][None], gu2=w["ffn2_w_gate_up"], d2=w["ffn2_w_down"],
    )


def _layer_fwd(h, p, B, L):
    h, s1 = _ffn_fwd(h, p["g1"], p["gu1"], p["d1"], "ffn1")
    h, sm = _mixer_fwd(h, p, B, L)
    h, s2 = _ffn_fwd(h, p["g2"], p["gu2"], p["d2"], "ffn2")
    return h, (s1, sm, s2)


def _layer_bwd(dh, saved, p, w, B, L):
    s1, sm, s2 = saved
    D = dh.shape[1]
    d_xbc = D + D_XBC_EXTRA
    dh, f2 = _ffn_bwd(dh, s2, p["g2"], p["gu2"], p["d2"], "ffn2")
    dh, gm = _mixer_bwd(dh, sm, p, B, L)
    dh, f1 = _ffn_bwd(dh, s1, p["g1"], p["gu1"], p["d1"], "ffn1")
    sb, av = gm["small_bias"][0], gm["avec"][0]
    cw = gm["conv_wb"]
    grads = dict(
        ffn1_norm=f1["norm"][0], ffn1_w_gate_up=f1["gu"], ffn1_w_down=f1["down"],
        mix_norm=gm["gm"][0], w_in=gm["w_all_t"],
        fox_forget_bias=sb[:HEADS], ssd_conv_w=cw[:CONV_K, :d_xbc], ssd_conv_b=cw[CONV_K, :d_xbc],
        ssd_dt_bias=sb[HEADS:2 * HEADS], ssd_a_log=av[HEADS:2 * HEADS] * (-jnp.exp(w["ssd_a_log"])),
        ssd_d=gm["dvec"][0, HEADS:2 * HEADS], ssd_norm=gm["ssd_norm"][0],
        lru_conv_w=cw[:CONV_K, d_xbc:], lru_conv_b=cw[CONV_K, d_xbc:],
        lru_w_a=_diag_blocks(gm["wa"], HEADS), lru_b_a=gm["pvec"][0], lru_w_x=_diag_blocks(gm["wx"], HEADS),
        lru_b_x=gm["pvec"][1], lru_lambda=gm["pvec"][2],
        w_branch_attn=gm["wba"], w_branch_ssd=gm["wbs"], w_branch_lru=gm["wbl"], w_out=gm["wout"],
        ffn2_norm=f2["norm"][0], ffn2_w_gate_up=f2["gu"], ffn2_w_down=f2["down"],
    )
    return dh, grads


LAYER_NAMES = ["ffn1_norm", "ffn1_w_gate_up", "ffn1_w_down", "mix_norm", "w_in", "fox_forget_bias", "ssd_conv_w",
               "ssd_conv_b", "ssd_dt_bias", "ssd_a_log", "ssd_d", "ssd_norm", "lru_conv_w", "lru_conv_b", "lru_w_a",
               "lru_b_a", "lru_w_x", "lru_b_x", "lru_lambda", "w_branch_attn", "w_branch_ssd", "w_branch_lru", "w_out",
               "ffn2_norm", "ffn2_w_gate_up", "ffn2_w_down"]
WEIGHT_NAMES = ["meta_tokens"] + LAYER_NAMES + ["final_norm"]


def _local_step(x, target, meta, layers, final_norm):
    B, S, D = x.shape
    L = -(-(N_META + S) // Q_BLOCK) * Q_BLOCK
    h = jnp.concatenate([jnp.broadcast_to(meta[None], (B, N_META, D)), x,
                         jnp.zeros((B, L - N_META - S, D), F32)], axis=1).reshape(B * L, D)
    params, saved = [], []
    for w in layers:
        p = _layer_params(w)
        h, s = _layer_fwd(h, p, B, L)
        params.append(p)
        saved.append(s)
    tgt = jnp.pad(target, ((0, 0), (N_META, L - N_META - S), (0, 0))).reshape(B * L, D)
    dh, loss, dfinal = _loss_bwd(h, tgt, final_norm[None], L, S, "loss")
    grads = [None] * len(layers)
    for l in reversed(range(len(layers))):
        dh, grads[l] = _layer_bwd(dh, saved[l], params[l], layers[l], B, L)
    dh3 = dh.reshape(B, L, D)
    return loss, dh3[:, N_META:N_META + S], jnp.sum(dh3[:, :N_META], axis=0), grads, dfinal[0]


BIG_NAMES = ["ffn1_w_gate_up", "ffn1_w_down", "w_in", "w_branch_attn", "w_branch_ssd", "w_branch_lru", "w_out",
             "ffn2_w_gate_up", "ffn2_w_down"]
COL_SHARDED = {"ffn1_w_gate_up", "w_in", "ffn2_w_gate_up"}
SMALL_SHARDED = ["meta_tokens", "ssd_conv_w", "lru_conv_w"]
SMALL_NAMES = [n for n in LAYER_NAMES if n not in BIG_NAMES]


def _shard_rows(name, shape):
    return _padded(shape[1]) if name in COL_SHARDED else shape[0]


def _pack_shards(shards):
    rows = []
    for n in BIG_NAMES:
        s = shards[n]
        if n in COL_SHARDED:
            s = jnp.pad(s.T, ((0, _padded(s.shape[1]) - s.shape[1]), (0, 0)))
        rows.append(s)
    return jnp.concatenate(rows, axis=0)


def _unpack_gathered(gathered, shapes, D):
    out, o = {}, 0
    for n in BIG_NAMES:
        r = _shard_rows(n, shapes[n])
        out[n] = gathered[:, o:o + r].reshape(N_DEV * r, D)
        o += r
    out["w_in"] = _reorder_rows(out["w_in"], D, shapes["w_in"][1])
    return out


def _pack_full_grads(grads, shapes, D):
    slabs = []
    for n in BIG_NAMES:
        g = grads[n]
        if n == "w_in":
            g = _restore_rows(g, D, shapes[n][1])
        slabs.append(g.reshape(N_DEV, _shard_rows(n, shapes[n]), D))
    return jnp.concatenate(slabs, axis=1)


def _unpack_local(rows, shapes):
    out, o = {}, 0
    for n in BIG_NAMES:
        r = _shard_rows(n, shapes[n])
        blk = rows[o:o + r]
        out[n] = blk[:shapes[n][1]].T if n in COL_SHARDED else blk
        o += r
    return out


def _as_rows(flat):
    n = flat.shape[0]
    unit = LANES * SUBLANES
    total = -(-n // unit) * unit
    return jnp.pad(flat, (0, total - n)).reshape(total // LANES, LANES)


def _flatten_list(arrs):
    return _as_rows(jnp.concatenate([a.reshape(-1) for a in arrs]))


def _split_like(rows, shapes):
    flat = rows.reshape(-1)
    out, o = [], 0
    for s in shapes:
        n = math.prod(s)
        out.append(flat[o:o + n].reshape(s))
        o += n
    return out


def _gather_last(rows8, shape):
    lead, c = shape[:-1], shape[-1]
    t = rows8.reshape((N_DEV,) + tuple(lead) + (c,))
    return jnp.moveaxis(t, 0, -2).reshape(tuple(lead) + (N_DEV * c,))


def kernel(x, meta_tokens, ffn1_norm, ffn1_w_gate_up, ffn1_w_down, mix_norm, w_in, fox_forget_bias, ssd_conv_w, ssd_conv_b, ssd_dt_bias, ssd_a_log, ssd_d, ssd_norm, lru_conv_w, lru_conv_b, lru_w_a, lru_b_a, lru_w_x, lru_b_x, lru_lambda, w_branch_attn, w_branch_ssd, w_branch_lru, w_out, ffn2_norm, ffn2_w_gate_up, ffn2_w_down, final_norm, loss_target, m_meta_tokens, m_ffn1_norm, m_ffn1_w_gate_up, m_ffn1_w_down, m_mix_norm, m_w_in, m_fox_forget_bias, m_ssd_conv_w, m_ssd_conv_b, m_ssd_dt_bias, m_ssd_a_log, m_ssd_d, m_ssd_norm, m_lru_conv_w, m_lru_conv_b, m_lru_w_a, m_lru_b_a, m_lru_w_x, m_lru_b_x, m_lru_lambda, m_w_branch_attn, m_w_branch_ssd, m_w_branch_lru, m_w_out, m_ffn2_norm, m_ffn2_w_gate_up, m_ffn2_w_down, m_final_norm, v_meta_tokens, v_ffn1_norm, v_ffn1_w_gate_up, v_ffn1_w_down, v_mix_norm, v_w_in, v_fox_forget_bias, v_ssd_conv_w, v_ssd_conv_b, v_ssd_dt_bias, v_ssd_a_log, v_ssd_d, v_ssd_norm, v_lru_conv_w, v_lru_conv_b, v_lru_w_a, v_lru_b_a, v_lru_w_x, v_lru_b_x, v_lru_lambda, v_w_branch_attn, v_w_branch_ssd, v_w_branch_lru, v_w_out, v_ffn2_norm, v_ffn2_w_gate_up, v_ffn2_w_down, v_final_norm):
    weights = dict(zip(WEIGHT_NAMES, (meta_tokens, ffn1_norm, ffn1_w_gate_up, ffn1_w_down, mix_norm, w_in, fox_forget_bias, ssd_conv_w, ssd_conv_b, ssd_dt_bias, ssd_a_log, ssd_d, ssd_norm, lru_conv_w, lru_conv_b, lru_w_a, lru_b_a, lru_w_x, lru_b_x, lru_lambda, w_branch_attn, w_branch_ssd, w_branch_lru, w_out, ffn2_norm, ffn2_w_gate_up, ffn2_w_down, final_norm,)))
    mom1 = dict(zip(WEIGHT_NAMES, (m_meta_tokens, m_ffn1_norm, m_ffn1_w_gate_up, m_ffn1_w_down, m_mix_norm, m_w_in, m_fox_forget_bias, m_ssd_conv_w, m_ssd_conv_b, m_ssd_dt_bias, m_ssd_a_log, m_ssd_d, m_ssd_norm, m_lru_conv_w, m_lru_conv_b, m_lru_w_a, m_lru_b_a, m_lru_w_x, m_lru_b_x, m_lru_lambda, m_w_branch_attn, m_w_branch_ssd, m_w_branch_lru, m_w_out, m_ffn2_norm, m_ffn2_w_gate_up, m_ffn2_w_down, m_final_norm,)))
    mom2 = dict(zip(WEIGHT_NAMES, (v_meta_tokens, v_ffn1_norm, v_ffn1_w_gate_up, v_ffn1_w_down, v_mix_norm, v_w_in, v_fox_forget_bias, v_ssd_conv_w, v_ssd_conv_b, v_ssd_dt_bias, v_ssd_a_log, v_ssd_d, v_ssd_norm, v_lru_conv_w, v_lru_conv_b, v_lru_w_a, v_lru_b_a, v_lru_w_x, v_lru_b_x, v_lru_lambda, v_w_branch_attn, v_w_branch_ssd, v_w_branch_lru, v_w_out, v_ffn2_norm, v_ffn2_w_gate_up, v_ffn2_w_down, v_final_norm,)))
    depth = ffn1_norm.shape[0]
    D = x.shape[-1]
    my_idx = 4 * lax.axis_index("x") + 2 * lax.axis_index("y") + lax.axis_index("c")

    small_shapes = [weights[n].shape for n in SMALL_SHARDED]
    gathered = _all_gather(_flatten_list([weights[n] for n in SMALL_SHARDED]), "gather_small").reshape(N_DEV, -1)
    small_full, o = {}, 0
    for n, s in zip(SMALL_SHARDED, small_shapes):
        k = math.prod(s)
        small_full[n] = _gather_last(gathered[:, o:o + k], s)
        o += k

    shard_shapes = {n: weights[n].shape[1:] for n in BIG_NAMES}
    layers = []
    for l in range(depth):
        packed = _pack_shards({n: weights[n][l].astype(BF16) for n in BIG_NAMES})
        w = _unpack_gathered(_all_gather(packed, "gather_weights"), shard_shapes, D)
        for n in SMALL_NAMES:
            w[n] = small_full[n][l] if n in SMALL_SHARDED else weights[n][l]
        layers.append(w)

    loss, dx, dmeta, grads, dfinal = _local_step(x, loss_target, small_full["meta_tokens"], layers, final_norm)
    loss = lax.psum(loss[0, 0], ("x", "y", "c"))

    summed = {n: [] for n in WEIGHT_NAMES}
    for l in range(depth):
        parts = _pack_full_grads(grads[l], shard_shapes, D)
        local = _unpack_local(_sum8(_exchange(parts, "exchange_grads"), "sum_grads"), shard_shapes)
        for n in BIG_NAMES:
            summed[n].append(local[n])

    small_list = [dmeta, dfinal] + [grads[l][n] for l in range(depth) for n in SMALL_NAMES]
    total = _sum8(_all_gather(_flatten_list(small_list), "gather_small_grads"), "sum_small_grads")
    parts = _split_like(total, [a.shape for a in small_list])
    full_small = {"meta_tokens": parts[0], "final_norm": parts[1]}
    for i, n in enumerate(SMALL_NAMES):
        full_small[n] = jnp.stack([parts[2 + l * len(SMALL_NAMES) + i] for l in range(depth)])
    grad = {}
    for n in WEIGHT_NAMES:
        if n in BIG_NAMES:
            grad[n] = jnp.stack(summed[n])
        elif n in SMALL_SHARDED:
            c = weights[n].shape[-1]
            grad[n] = lax.dynamic_slice_in_dim(full_small[n], my_idx * c, c, axis=full_small[n].ndim - 1)
        else:
            grad[n] = full_small[n]

    delta, new_m, new_v = {}, {}, {}
    for n in WEIGHT_NAMES:
        delta[n], new_m[n], new_v[n] = _adamw(weights[n], grad[n], mom1[n], mom2[n], "adamw_" + n)
    return (loss, dx, *[grad[n] for n in WEIGHT_NAMES], *[delta[n] for n in WEIGHT_NAMES],
            *[new_m[n] for n in WEIGHT_NAMES], *[new_v[n] for n in WEIGHT_NAMES])
```

```python
import functools
import math

import jax
import jax.numpy as jnp
from jax import lax
from jax.experimental import pallas as pl
from jax.experimental.pallas import tpu as pltpu

F32 = jnp.float32
BF16 = jnp.bfloat16

N_DEV = 8
N_META = 16
Q_BLOCK = 128
NORM_EPS = 1e-6
HEADS = 16
HEAD_DIM = 64
SSD_GROUPS = 2
SSD_STATE = 128
CONV_K = 4
LRU_C = 8.0
ADAM_LR, ADAM_B1, ADAM_B2, ADAM_EPS, ADAM_WD, ADAM_STEP = 0.001, 0.9, 0.999, 1e-08, 0.01, 10

LANES = 128
SUBLANES = 8
VMEM_LIMIT = 56 * 1024 * 1024
NEG = -1e30
MM_TILE = 1408
MM_VMEM = 40 * 1024 * 1024


def _cparams(sem=None):
    return pltpu.CompilerParams(dimension_semantics=sem, vmem_limit_bytes=VMEM_LIMIT)


def _tile(dim, target, mult=LANES):
    if dim <= target:
        return dim
    best = None
    for t in range(mult, target + 1, mult):
        if dim % t == 0:
            best = t
    assert best is not None, (dim, target)
    return best


def _sigmoid(x):
    return 1.0 / (1.0 + jnp.exp(-x))


def _log1p_exp_neg_abs(x):
    e = jnp.exp(-jnp.abs(x))
    u = 1.0 + e
    return jnp.where(u == 1.0, e, jnp.log(u) * (e / jnp.where(u == 1.0, 1.0, u - 1.0)))


def _log_sigmoid(x):
    return jnp.minimum(x, 0.0) - _log1p_exp_neg_abs(x)


def _softplus(x):
    return jnp.maximum(x, 0.0) + _log1p_exp_neg_abs(x)


def _one_minus_exp(y):
    u = jnp.exp(y)
    safe = jnp.where(u == 1.0, 0.5, u)
    return jnp.where(u == 1.0, -y, (1.0 - u) * y / jnp.log(safe))


def _silu(x):
    return x * _sigmoid(x)


def _dsilu(x):
    s = _sigmoid(x)
    return s * (1.0 + x * (1.0 - s))


_GELU_C = math.sqrt(2.0 / math.pi)


def _gelu(x):
    return 0.5 * x * (1.0 + jnp.tanh(_GELU_C * (x + 0.044715 * x * x * x)))


def _dgelu(x):
    t = jnp.tanh(_GELU_C * (x + 0.044715 * x * x * x))
    return 0.5 * (1.0 + t) + 0.5 * x * (1.0 - t * t) * _GELU_C * (1.0 + 3.0 * 0.044715 * x * x)


def _split3_dot(tri, x):
    hi = x.astype(BF16)
    r1 = x - hi.astype(F32)
    mid = r1.astype(BF16)
    lo = (r1 - mid.astype(F32)).astype(BF16)
    t = tri.astype(BF16)
    d = lambda p: jnp.dot(t, p, preferred_element_type=F32)
    return d(hi) + d(mid) + d(lo)


def _lower_tri(n, strict=False):
    r = lax.broadcasted_iota(jnp.int32, (n, n), 0)
    c = lax.broadcasted_iota(jnp.int32, (n, n), 1)
    return (c < r) if strict else (c <= r)


def _mm(a, b, *, ta=False, tb=False, out_dtype=F32, res=None, scale=None, tm=None, tn=None, tk=None,
        a_off=(0, 0), b_off=(0, 0), dims=None, name):
    if dims is None:
        M, K = (a.shape[1], a.shape[0]) if ta else a.shape
        N = b.shape[0] if tb else b.shape[1]
    else:
        M, N, K = dims
    tk = tk or (K if K <= 2816 else _tile(K, 1408))
    nk_ = K // tk
    pick_m, pick_n = tm is None, tn is None
    tm = tm or _tile(M, MM_TILE)
    tn = tn or _tile(N, MM_TILE)

    def vmem(tm_, tn_):
        a_b = tm_ * tk * a.dtype.itemsize + (tm_ * tk * 2 if a.dtype != BF16 else 0)
        b_b = tn_ * tk * b.dtype.itemsize + (tn_ * tk * 2 if b.dtype != BF16 else 0)
        o_b = tm_ * tn_ * (jnp.dtype(out_dtype).itemsize + (4 if res is not None else 0))
        return 2 * (a_b + b_b + o_b) + (tm_ * tn_ * 4 if nk_ > 1 else 0) + tm_ * tn_ * 4

    while vmem(tm, tn) > MM_VMEM and (pick_m or pick_n):
        if pick_m and (tm >= tn or not pick_n) and tm > LANES:
            tm = _tile(M, tm - LANES)
        elif pick_n and tn > LANES:
            tn = _tile(N, tn - LANES)
        else:
            break
    assert M % tm == 0 and N % tn == 0 and K % tk == 0, (name, M, N, K, tm, tn, tk)
    nk = K // tk
    ca = 0 if ta else 1
    cb = 1 if tb else 0

    def blk(rows, cols, off):
        assert off[0] % rows == 0 and off[1] % cols == 0, (name, off, rows, cols)
        return off[0] // rows, off[1] // cols

    if ta:
        ao = blk(tk, tm, a_off)
        a_spec = pl.BlockSpec((tk, tm), lambda i, j, k: (k + ao[0], i + ao[1]))
    else:
        ao = blk(tm, tk, a_off)
        a_spec = pl.BlockSpec((tm, tk), lambda i, j, k: (i + ao[0], k + ao[1]))
    if tb:
        bo = blk(tn, tk, b_off)
        b_spec = pl.BlockSpec((tn, tk), lambda i, j, k: (j + bo[0], k + bo[1]))
    else:
        bo = blk(tk, tn, b_off)
        b_spec = pl.BlockSpec((tk, tn), lambda i, j, k: (k + bo[0], j + bo[1]))
    o_spec = pl.BlockSpec((tm, tn), lambda i, j, k: (i, j))
    in_specs = [a_spec, b_spec] + ([o_spec] if res is not None else [])
    has_res = res is not None

    def kern(*refs):
        if has_res:
            a_ref, b_ref, r_ref, o_ref = refs[:4]
            scr = refs[4:]
        else:
            a_ref, b_ref, o_ref = refs[:3]
            r_ref = None
            scr = refs[3:]
        p = lax.dot_general(a_ref[...].astype(BF16), b_ref[...].astype(BF16), (((ca,), (cb,)), ((), ())),
                            preferred_element_type=F32)

        def fin(val):
            if scale is not None:
                val = val * scale
            if has_res:
                val = r_ref[...] + val
            o_ref[...] = val.astype(out_dtype)

        if nk == 1:
            fin(p)
        else:
            acc = scr[0]
            k = pl.program_id(2)

            @pl.when(k == 0)
            def _():
                acc[...] = p

            @pl.when(k > 0)
            def _():
                acc[...] += p

            @pl.when(k == nk - 1)
            def _():
                fin(acc[...])

    args = (a, b) + ((res,) if has_res else ())
    return pl.pallas_call(
        kern, name=name, grid=(M // tm, N // tn, nk), in_specs=in_specs, out_specs=o_spec,
        out_shape=jax.ShapeDtypeStruct((M, N), out_dtype),
        scratch_shapes=[pltpu.VMEM((tm, tn), F32)] if nk > 1 else [],
        compiler_params=_cparams(("parallel", "parallel", "arbitrary")),
    )(*args)


def _rows(body, tiled, full, outs, accs, *, tr, name, T):
    assert T % tr == 0
    in_specs = []
    for arr, width, off in tiled:
        assert off % width == 0, (name, off, width)
        in_specs.append(pl.BlockSpec((tr, width), functools.partial(lambda i, o: (i, o), o=off // width)))
    for arr in full:
        in_specs.append(pl.BlockSpec(arr.shape, lambda i: (0, 0)))
    out_specs = [pl.BlockSpec((tr, w), lambda i: (i, 0)) for w, _ in outs]
    out_specs += [pl.BlockSpec(s, lambda i: (0, 0)) for s, _ in accs]
    out_shape = [jax.ShapeDtypeStruct((T, w), d) for w, d in outs] + [jax.ShapeDtypeStruct(s, d) for s, d in accs]
    nt, nf, no = len(tiled), len(full), len(outs)

    def kern(*refs):
        i = pl.program_id(0)
        acc_refs = refs[nt + nf + no:]

        @pl.when(i == 0)
        def _():
            for r in acc_refs:
                r[...] = jnp.zeros(r.shape, r.dtype)

        body(i, refs[:nt], refs[nt:nt + nf], refs[nt + nf:nt + nf + no], acc_refs)

    res = pl.pallas_call(
        kern, name=name, grid=(T // tr,), in_specs=in_specs, out_specs=out_specs, out_shape=out_shape,
        compiler_params=_cparams(("arbitrary",)),
    )(*[t[0] for t in tiled], *full)
    return res


def _colsum(x):
    return jnp.sum(x, axis=0, keepdims=True)


def _norm_fwd(h, g, name):
    T, D = h.shape

    def body(i, t, f, o, a):
        x = t[0][...]
        r = lax.rsqrt(jnp.mean(x * x, axis=-1, keepdims=True) + NORM_EPS)
        o[0][...] = (x * r * f[0][...]).astype(BF16)

    return _rows(body, [(h, D, 0)], [g], [(D, BF16)], [], tr=_tile(T, 768, 8), name=name, T=T)[0]


def _norm_bwd(h, dn, dh, g, name):
    T, D = h.shape

    def body(i, t, f, o, a):
        x, dnv, dhv = t[0][...], t[1][...], t[2][...]
        r = lax.rsqrt(jnp.mean(x * x, axis=-1, keepdims=True) + NORM_EPS)
        xh = x * r
        dng = dnv * f[0][...]
        o[0][...] = dhv + r * (dng - xh * jnp.mean(dng * xh, axis=-1, keepdims=True))
        a[0][...] += _colsum(dnv * xh)

    return _rows(body, [(h, D, 0), (dn, D, 0), (dh, D, 0)], [g], [(D, F32)], [((1, D), F32)],
                 tr=_tile(T, 384, 8), name=name, T=T)


def _swiglu_fwd(gu, name):
    T, F2 = gu.shape
    F = F2 // 2

    def body(i, t, f, o, a):
        o[0][...] = (_silu(t[0][...]) * t[1][...]).astype(BF16)

    return _rows(body, [(gu, F, 0), (gu, F, F)], [], [(F, BF16)], [], tr=_tile(T, 256, 8), name=name, T=T)[0]


def _swiglu_bwd(gu, da, name):
    T, F2 = gu.shape
    F = F2 // 2

    def body(i, t, f, o, a):
        gv, uv, dav = t[0][...], t[1][...], t[2][...]
        o[0][:, :F] = (dav * uv * _dsilu(gv)).astype(BF16)
        o[0][:, F:] = (dav * _silu(gv)).astype(BF16)

    return _rows(body, [(gu, F, 0), (gu, F, F), (da, F, 0)], [], [(F2, BF16)], [], tr=_tile(T, 256, 8),
                 name=name, T=T)[0]


def _merge_fwd(proj, off, pa, pb, pc, name):
    T, D = pa.shape

    def body(i, t, f, o, a):
        o[0][...] = (_sigmoid(t[0][...]) * t[3][...] + _sigmoid(t[1][...]) * t[4][...]
                     + _sigmoid(t[2][...]) * t[5][...]).astype(BF16)

    tiled = [(proj, D, off), (proj, D, off + D), (proj, D, off + 2 * D), (pa, D, 0), (pb, D, 0), (pc, D, 0)]
    return _rows(body, tiled, [], [(D, BF16)], [], tr=_tile(T, 384, 8), name=name, T=T)[0]


def _merge_bwd(dmixed, proj, off, pa, pb, pc, name):
    T, D = pa.shape

    def body(i, t, f, o, a):
        dm = t[0][...]
        for k in range(3):
            g = _sigmoid(t[1 + k][...])
            o[k][...] = (dm * g).astype(BF16)
            o[3][:, k * D:(k + 1) * D] = (dm * t[4 + k][...] * g * (1.0 - g)).astype(BF16)

    tiled = [(dmixed, D, 0), (proj, D, off), (proj, D, off + D), (proj, D, off + 2 * D), (pa, D, 0), (pb, D, 0),
             (pc, D, 0)]
    return _rows(body, tiled, [], [(D, BF16)] * 3 + [(3 * D, BF16)], [], tr=_tile(T, 384, 8), name=name, T=T)


def _gnorm_fwd(y, proj, zoff, nw, name):
    T, D = y.shape
    gs = D // SSD_GROUPS

    def body(i, t, f, o, a):
        s = t[0][...] * _silu(t[1][...])
        for g in range(SSD_GROUPS):
            sg = s[:, g * gs:(g + 1) * gs]
            r = lax.rsqrt(jnp.mean(sg * sg, axis=-1, keepdims=True) + NORM_EPS)
            o[0][:, g * gs:(g + 1) * gs] = (sg * r * f[0][:, g * gs:(g + 1) * gs]).astype(BF16)

    return _rows(body, [(y, D, 0), (proj, D, zoff)], [nw], [(D, BF16)], [], tr=_tile(T, 384, 8), name=name, T=T)[0]


def _gnorm_bwd(dout, y, proj, zoff, nw, name):
    T, D = y.shape
    gs = D // SSD_GROUPS

    def body(i, t, f, o, a):
        dov, yv, zv = t[0][...], t[1][...], t[2][...]
        sz = _silu(zv)
        s = yv * sz
        dsz = _dsilu(zv)
        for g in range(SSD_GROUPS):
            sl = slice(g * gs, (g + 1) * gs)
            sg = s[:, sl]
            r = lax.rsqrt(jnp.mean(sg * sg, axis=-1, keepdims=True) + NORM_EPS)
            sh = sg * r
            dog = dov[:, sl]
            dng = dog * f[0][:, sl]
            ds = r * (dng - sh * jnp.mean(dng * sh, axis=-1, keepdims=True))
            o[0][:, sl] = ds * sz[:, sl]
            o[1][:, sl] = (ds * yv[:, sl] * dsz[:, sl]).astype(BF16)
            a[0][:, sl] += _colsum(dog * sh)

    return _rows(body, [(dout, D, 0), (y, D, 0), (proj, D, zoff)], [nw], [(D, F32), (D, BF16)], [((1, D), F32)],
                 tr=_tile(T, 384, 8), name=name, T=T)


def _loss_bwd(h, tgt, g, seq_len, n_real, name):
    T, D = h.shape
    tr = _tile(seq_len, 384, 8)
    per_seq = seq_len // tr

    def body(i, t, f, o, a):
        x, tg = t[0][...], t[1][...]
        pos = (i % per_seq) * tr + lax.broadcasted_iota(jnp.int32, (tr, 1), 0)
        valid = (pos >= N_META) & (pos < N_META + n_real)
        r = lax.rsqrt(jnp.mean(x * x, axis=-1, keepdims=True) + NORM_EPS)
        xh = x * r
        e = jnp.where(valid, xh * f[0][...] - tg, 0.0)
        a[0][...] += jnp.zeros((1, LANES), F32) + 0.5 * jnp.sum(jnp.sum(e * e, axis=-1, keepdims=True) / D,
                                                              axis=0, keepdims=True)
        dy = e / D
        dng = dy * f[0][...]
        o[0][...] = r * (dng - xh * jnp.mean(dng * xh, axis=-1, keepdims=True))
        a[1][...] += _colsum(dy * xh)

    return _rows(body, [(h, D, 0), (tgt, D, 0)], [g], [(D, F32)], [((1, LANES), F32), ((1, D), F32)], tr=tr,
                 name=name, T=T)


def _lane_is_attn(shape):
    return lax.broadcasted_iota(jnp.int32, shape, len(shape) - 1) < HEADS


def _gate_prep(proj3, col_blk, bias, avec, name):
    B, L, _ = proj3.shape
    Q = Q_BLOCK
    nc = L // Q

    def kern(x_ref, b_ref, a_ref, v_ref, c_ref, carry):
        c = pl.program_id(1)

        @pl.when(c == 0)
        def _():
            carry[...] = jnp.zeros_like(carry)

        x = x_ref[0] + b_ref[...]
        attn = _lane_is_attn(x.shape)
        v = jnp.where(attn, _log_sigmoid(x), _softplus(x))
        w = jnp.where(attn, v, v * a_ref[...])
        cs = _split3_dot(_lower_tri(Q), w) + jnp.where(attn[:1], carry[...], 0.0)
        v_ref[0] = v
        c_ref[0] = cs
        rows = lax.broadcasted_iota(jnp.int32, (Q, 1), 0)
        carry[...] = jnp.sum(jnp.where(rows == Q - 1, cs, 0.0), axis=0, keepdims=True)

    blk = pl.BlockSpec((1, Q, LANES), lambda b, c: (b, c, 0))
    vec = pl.BlockSpec((1, LANES), lambda b, c: (0, 0))
    return pl.pallas_call(
        kern, name=name, grid=(B, nc),
        in_specs=[pl.BlockSpec((1, Q, LANES), lambda b, c: (b, c, col_blk)), vec, vec],
        out_specs=[blk, blk], out_shape=[jax.ShapeDtypeStruct((B, L, LANES), F32)] * 2,
        scratch_shapes=[pltpu.VMEM((1, LANES), F32)],
        compiler_params=_cparams(("parallel", "arbitrary")),
    )(proj3, bias, avec)


def _gate_post(drow, dcol, ddt, proj3, col_blk, vals, bias, avec, name):
    B, L, _ = proj3.shape
    Q = Q_BLOCK
    nc = L // Q

    def kern(dr_ref, dc_ref, dd_ref, x_ref, v_ref, b_ref, a_ref, o_ref, db_ref, da_ref, carry):
        b = pl.program_id(0)
        c = pl.program_id(1)

        @pl.when((b == 0) & (c == 0))
        def _():
            db_ref[...] = jnp.zeros_like(db_ref)
            da_ref[...] = jnp.zeros_like(da_ref)

        @pl.when(c == 0)
        def _():
            carry[...] = jnp.zeros_like(carry)

        x = x_ref[0] + b_ref[...]
        attn = _lane_is_attn(x.shape)
        dcs = dr_ref[0] + dc_ref[0]
        upper = jnp.logical_not(_lower_tri(Q, strict=True))
        rc = _split3_dot(upper, dcs) + jnp.where(attn[:1], carry[...], 0.0)
        rows = lax.broadcasted_iota(jnp.int32, (Q, 1), 0)
        carry[...] = jnp.sum(jnp.where(rows == 0, rc, 0.0), axis=0, keepdims=True)
        dv = jnp.where(attn, rc, dd_ref[0] + rc * a_ref[...])
        dpre = dv * jnp.where(attn, _sigmoid(-x), _sigmoid(x))
        o_ref[0] = dpre.astype(BF16)
        db_ref[...] += _colsum(dpre)
        da_ref[...] += _colsum(jnp.where(attn, 0.0, rc * v_ref[0]))

    rev = pl.BlockSpec((1, Q, LANES), lambda b, c: (b, nc - 1 - c, 0))
    vec = pl.BlockSpec((1, LANES), lambda b, c: (0, 0))
    return pl.pallas_call(
        kern, name=name, grid=(B, nc),
        in_specs=[rev, rev, rev, pl.BlockSpec((1, Q, LANES), lambda b, c: (b, nc - 1 - c, col_blk)), rev, vec, vec],
        out_specs=[rev, vec, vec],
        out_shape=[jax.ShapeDtypeStruct((B, L, LANES), BF16), jax.ShapeDtypeStruct((1, LANES), F32),
                   jax.ShapeDtypeStruct((1, LANES), F32)],
        scratch_shapes=[pltpu.VMEM((1, LANES), F32)],
        compiler_params=_cparams(("arbitrary", "arbitrary")),
    )(drow, dcol, ddt, proj3, vals, bias, avec)


def _lane_col(tile, lane):
    sel = lax.broadcasted_iota(jnp.int32, tile.shape, 1) == lane
    return jnp.sum(jnp.where(sel, tile, 0.0), axis=1, keepdims=True)


AUG = LANES
AUG_A = HEAD_DIM
AUG_B = HEAD_DIM + 3


def _split3(x):
    hi = x.astype(BF16).astype(F32)
    mid = (x - hi).astype(BF16).astype(F32)
    lo = (x - hi - mid).astype(BF16).astype(F32)
    return hi, mid, lo


def _put3(base, lane, first, x):
    hi, mid, lo = _split3(x)
    return jnp.where(lane == first, hi, jnp.where(lane == first + 1, mid, jnp.where(lane == first + 2, lo, base)))


HP = 2


def _attn_fwd(qa, ka, va, name, gather=None):
    B, H, L, _ = qa.shape
    tq = _tile(L, 384)
    nq = L // tq
    nh = H // HP
    comm = gather is not None

    def kern(*refs):
        if comm:
            q_ref, k_ref, v_ref, x_ref, o_ref, l_ref, g_ref, send_sems, recv_sems, local_sem = refs
        else:
            q_ref, k_ref, v_ref, o_ref, l_ref = refs
        qi = pl.program_id(2)
        if comm:
            _ride((pl.program_id(0) * nh + pl.program_id(1)) * nq + qi, B * nh * nq,
                  _gather_phases(x_ref, g_ref, send_sems, recv_sems, local_sem))
        qs = [q_ref[0, hh] for hh in range(HP)]
        causal = _lower_tri(tq)

        def step(j, carry, masked):
            rows = pl.ds(pl.multiple_of(j * tq, tq), tq)
            out = []
            for hh in range(HP):
                m, acc = carry[hh]
                s = _dot_nt(qs[hh], k_ref[0, hh, rows, :])
                if masked:
                    s = jnp.where(causal, s, NEG)
                m_new = jnp.maximum(m, jnp.max(s, axis=1, keepdims=True))
                p = jnp.exp(s - m_new)
                out.append((m_new, jnp.exp(m - m_new) * acc + _dot(p.astype(BF16), v_ref[0, hh, rows, :])))
            return tuple(out)

        init = tuple((jnp.full((tq, 1), NEG, F32), jnp.zeros((tq, AUG), F32)) for _ in range(HP))
        carry = lax.fori_loop(0, qi, lambda j, c: step(j, c, False), init)
        for hh, (m, acc) in enumerate(step(qi, carry, True)):
            l = _lane_col(acc, AUG_A)
            o_ref[0, hh] = acc / l
            l_ref[0, hh] = m + jnp.log(l)

    qspec = pl.BlockSpec((1, HP, tq, AUG), lambda b, h, i: (b, h, i, 0))
    kvspec = pl.BlockSpec((1, HP, L, AUG), lambda b, h, i: (b, h, 0, 0))
    lspec = pl.BlockSpec((1, HP, tq, 1), lambda b, h, i: (b, h, i, 0))
    out_shape = [jax.ShapeDtypeStruct((B, H, L, AUG), F32), jax.ShapeDtypeStruct((B, H, L, 1), F32)]
    if comm:
        out_shape.append(jax.ShapeDtypeStruct((N_DEV,) + gather.shape, gather.dtype))
    return pl.pallas_call(
        kern, name=name, grid=(B, nh, nq), in_specs=[qspec, kvspec, kvspec] + ([ANY] if comm else []),
        out_specs=[qspec, lspec] + ([ANY] if comm else []), out_shape=out_shape,
        scratch_shapes=COMM_SCRATCH if comm else [],
        compiler_params=_cparams(("arbitrary",) * 3 if comm else ("parallel", "parallel", "arbitrary")),
    )(qa, ka, va, *([gather] if comm else []))


def _attn_bwd(qa, ka, va, o, do, lse, name, parts=None):
    B, H, L, _ = qa.shape
    tq = _tile(L, 384)
    nq = L // tq
    nh = H // HP
    comm = parts is not None

    def kern(*refs):
        if comm:
            (q_ref, k_ref, v_ref, o_ref, do_ref, l_ref, p_ref, dq_ref, dk_ref, dv_ref, r_ref, dk_acc, dv_acc,
             send_sems, recv_sems, local_sem) = refs
        else:
            q_ref, k_ref, v_ref, o_ref, do_ref, l_ref, dq_ref, dk_ref, dv_ref, dk_acc, dv_acc = refs
        qi = pl.program_id(2)
        if comm:
            _ride((pl.program_id(0) * nh + pl.program_id(1)) * nq + qi, B * nh * nq,
                  _exchange_phases(p_ref, r_ref, send_sems, recv_sems, local_sem))

        @pl.when(qi == 0)
        def _():
            dk_acc[...] = jnp.zeros_like(dk_acc)
            dv_acc[...] = jnp.zeros_like(dv_acc)

        lane = lax.broadcasted_iota(jnp.int32, (tq, AUG), 1)
        head = lane < HEAD_DIM
        qbs, dobs = [], []
        for hh in range(HP):
            qf = q_ref[0, hh].astype(F32)
            dov = do_ref[0, hh]
            dsum = jnp.sum(jnp.where(head, dov * o_ref[0, hh], 0.0), axis=1, keepdims=True)
            dobs.append(_put3(dov, lane, AUG_A, -dsum).astype(BF16))
            c_t = jnp.sum(jnp.where((lane >= AUG_A) & (lane < AUG_A + 3), qf, 0.0), axis=1, keepdims=True)
            qbs.append(_put3(qf, lane, AUG_A, c_t - l_ref[0, hh]).astype(BF16))
        causal = _lower_tri(tq)

        def step(j, dqs, masked):
            rows = pl.ds(pl.multiple_of(j * tq, tq), tq)
            out = []
            for hh in range(HP):
                kj = k_ref[0, hh, rows, :]
                s = _dot_nt(qbs[hh], kj)
                if masked:
                    s = jnp.where(causal, s, NEG)
                p = jnp.exp(s)
                ds = (p * _dot_nt(dobs[hh], v_ref[0, hh, rows, :])).astype(BF16)
                dv_acc[hh, rows, :] += _dot_tn(p.astype(BF16), dobs[hh])
                dk_acc[hh, rows, :] += _dot_tn(ds, qbs[hh])
                out.append(dqs[hh] + _dot(ds, kj))
            return tuple(out)

        dqs = lax.fori_loop(0, qi, lambda j, c: step(j, c, False), tuple(jnp.zeros((tq, AUG), F32) for _ in range(HP)))
        for hh, dq in enumerate(step(qi, dqs, True)):
            dq_ref[0, hh] = dq

        @pl.when(qi == nq - 1)
        def _():
            dk_ref[0] = dk_acc[...]
            dv_ref[0] = dv_acc[...]

    qspec = pl.BlockSpec((1, HP, tq, AUG), lambda b, h, i: (b, h, i, 0))
    kvspec = pl.BlockSpec((1, HP, L, AUG), lambda b, h, i: (b, h, 0, 0))
    lspec = pl.BlockSpec((1, HP, tq, 1), lambda b, h, i: (b, h, i, 0))
    out_shape = [jax.ShapeDtypeStruct((B, H, L, AUG), F32)] * 3
    if comm:
        out_shape.append(jax.ShapeDtypeStruct(parts.shape, parts.dtype))
    return pl.pallas_call(
        kern, name=name, grid=(B, nh, nq),
        in_specs=[qspec, kvspec, kvspec, qspec, qspec, lspec] + ([ANY] if comm else []),
        out_specs=[qspec, kvspec, kvspec] + ([ANY] if comm else []), out_shape=out_shape,
        scratch_shapes=[pltpu.VMEM((HP, L, AUG), F32), pltpu.VMEM((HP, L, AUG), F32)] + (COMM_SCRATCH if comm else []),
        compiler_params=_cparams(("arbitrary",) * 3 if comm else ("parallel", "parallel", "arbitrary")),
    )(qa, ka, va, o, do, lse, *([parts] if comm else []))


PAD = SUBLANES


def _conv_fwd(xp, w, b, n_silu, name):
    B, Lp, C = xp.shape
    L = Lp - PAD
    TR = _tile(L, 384, 8)

    def kern(x_ref, w_ref, b_ref, o_ref):
        cb = pl.program_id(1)

        def body(i, carry):
            r0 = pl.multiple_of(i * TR, TR)
            ext = x_ref[0, pl.ds(r0, TR + PAD), :]
            acc = jnp.zeros((TR, LANES), F32) + b_ref[...]
            for k in range(CONV_K):
                s = CONV_K - 1 - k
                sh = ext if s == 0 else pltpu.roll(ext, s, 0)
                acc = acc + w_ref[k:k + 1, :] * sh[PAD:PAD + TR]
            o_ref[0, pl.ds(r0, TR), :] = jnp.where(cb < n_silu, _silu(acc), acc)
            return carry

        lax.fori_loop(0, L // TR, body, 0)

    return pl.pallas_call(
        kern, name=name, grid=(B, C // LANES),
        in_specs=[pl.BlockSpec((1, Lp, LANES), lambda b_, c: (b_, 0, c)),
                  pl.BlockSpec((CONV_K, LANES), lambda b_, c: (0, c)), pl.BlockSpec((1, LANES), lambda b_, c: (0, c))],
        out_specs=pl.BlockSpec((1, L, LANES), lambda b_, c: (b_, 0, c)),
        out_shape=jax.ShapeDtypeStruct((B, L, C), F32),
        compiler_params=_cparams(("parallel", "parallel")),
    )(xp, w, b)


def _conv_bwd_pre(xp, du, w, b, n_silu, name):
    B, Lp, C = xp.shape
    L = Lp - PAD
    TR = _tile(L, 384, 8)

    def kern(x_ref, du_ref, w_ref, b_ref, dp_ref, dw_ref):
        cb = pl.program_id(0)

        @pl.when(pl.program_id(1) == 0)
        def _():
            dw_ref[...] = jnp.zeros_like(dw_ref)

        def body(i, carry):
            r0 = pl.multiple_of(i * TR, TR)
            ext = x_ref[0, pl.ds(r0, TR + PAD), :]
            taps = []
            acc = jnp.zeros((TR, LANES), F32) + b_ref[...]
            for k in range(CONV_K):
                s = CONV_K - 1 - k
                sh = ext if s == 0 else pltpu.roll(ext, s, 0)
                taps.append(sh[PAD:PAD + TR])
                acc = acc + w_ref[k:k + 1, :] * taps[-1]
            dv = du_ref[0, pl.ds(r0, TR), :]
            dpre = jnp.where(cb < n_silu, dv * _dsilu(acc), dv)
            dp_ref[0, pl.ds(r0, TR), :] = dpre
            return tuple(c + _colsum(dpre * t) for c, t in zip(carry[:CONV_K], taps)) + (carry[CONV_K] + _colsum(dpre),)

        z = jnp.zeros((1, LANES), F32)
        sums = lax.fori_loop(0, L // TR, body, (z,) * (CONV_K + 1))
        dp_ref[0, pl.ds(L, PAD), :] = jnp.zeros((PAD, LANES), F32)
        for k in range(CONV_K + 1):
            dw_ref[k:k + 1, :] += sums[k]

    return pl.pallas_call(
        kern, name=name, grid=(C // LANES, B),
        in_specs=[pl.BlockSpec((1, Lp, LANES), lambda c, b_: (b_, 0, c)),
                  pl.BlockSpec((1, L, LANES), lambda c, b_: (b_, 0, c)),
                  pl.BlockSpec((CONV_K, LANES), lambda c, b_: (0, c)), pl.BlockSpec((1, LANES), lambda c, b_: (0, c))],
        out_specs=[pl.BlockSpec((1, Lp, LANES), lambda c, b_: (b_, 0, c)),
                   pl.BlockSpec((SUBLANES, LANES), lambda c, b_: (0, c))],
        out_shape=[jax.ShapeDtypeStruct((B, Lp, C), F32), jax.ShapeDtypeStruct((SUBLANES, C), F32)],
        compiler_params=_cparams(("parallel", "arbitrary")),
    )(xp, du, w, b)


def _conv_bwd_in(dpp, w, name):
    B, Lp, C = dpp.shape
    L = Lp - PAD
    TR = _tile(L, 384, 16)

    def kern(d_ref, w_ref, o_ref):
        def body(i, carry):
            r0 = pl.multiple_of(i * TR, TR)
            ext = d_ref[0, pl.ds(r0, TR + PAD), :]
            acc = jnp.zeros((TR, LANES), F32)
            for k in range(CONV_K):
                s = CONV_K - 1 - k
                sh = ext if s == 0 else pltpu.roll(ext, TR + PAD - s, 0)
                acc = acc + w_ref[k:k + 1, :] * sh[0:TR]
            o_ref[0, pl.ds(r0, TR), :] = acc.astype(BF16)
            return carry

        lax.fori_loop(0, L // TR, body, 0)

    return pl.pallas_call(
        kern, name=name, grid=(B, C // LANES),
        in_specs=[pl.BlockSpec((1, Lp, LANES), lambda b_, c: (b_, 0, c)),
                  pl.BlockSpec((CONV_K, LANES), lambda b_, c: (0, c))],
        out_specs=pl.BlockSpec((1, L, LANES), lambda b_, c: (b_, 0, c)),
        out_shape=jax.ShapeDtypeStruct((B, L, C), BF16),
        compiler_params=_cparams(("parallel", "parallel")),
    )(dpp, w)


def _dot_nt(a, b):
    return lax.dot_general(a, b, (((1,), (1,)), ((), ())), preferred_element_type=F32)


def _dot_tn(a, b):
    return lax.dot_general(a, b, (((0,), (0,)), ((), ())), preferred_element_type=F32)


def _dot(a, b):
    return jnp.dot(a, b, preferred_element_type=F32)


def _ssd_specs(L, nc, b_blk, c_blk):
    E = HEADS // SSD_GROUPS
    return [
        pl.BlockSpec((1, 1, L, HEAD_DIM), lambda b, h: (b, h, 0, 0)),
        pl.BlockSpec((1, L, SSD_STATE), lambda b, h: (b, 0, b_blk + h // E)),
        pl.BlockSpec((1, L, SSD_STATE), lambda b, h: (b, 0, c_blk + h // E)),
        pl.BlockSpec((1, L, LANES), lambda b, h: (b, 0, 0)),
        pl.BlockSpec((1, L, LANES), lambda b, h: (b, 0, 0)),
        pl.BlockSpec((1, 1, nc, Q_BLOCK), lambda b, h: (b, HEADS + h, 0, 0)),
        pl.BlockSpec((1, LANES), lambda b, h: (0, 0)),
    ]


def _ssd_chunk(c, S, x_ref, b_ref, c_ref, v_ref, cu_ref, ct_ref, lane):
    Q = Q_BLOCK
    rows = pl.ds(pl.multiple_of(c * Q, Q), Q)
    xc = x_ref[0, 0, rows, :]
    Bb = b_ref[0, rows, :].astype(BF16)
    Cb = c_ref[0, rows, :].astype(BF16)
    dt = _lane_col(v_ref[0, rows, :], lane)
    A = _lane_col(cu_ref[0, rows, :], lane)
    Ar = ct_ref[0, 0, pl.ds(c, 1), :]
    Aend = _lane_col(Ar, Q - 1)
    xdt = xc * dt
    Lm = jnp.exp(jnp.where(_lower_tri(Q), A - Ar, NEG))
    CB = _dot_nt(Cb, Bb)
    e_end = jnp.exp(Aend - A)
    W = xdt * e_end
    S_new = S * jnp.exp(Aend) + _dot_tn(W.astype(BF16), Bb)
    return dict(rows=rows, xc=xc, Bb=Bb, Cb=Cb, dt=dt, A=A, Aend=Aend, xdt=xdt, Lm=Lm, CB=CB, e_end=e_end, W=W,
                S_new=S_new)


def _ssd_fwd(x4, u, b_blk, c_blk, vals, cums, cums_t, dvec, name):
    B, H, L, P = x4.shape
    nc = L // Q_BLOCK

    def kern(x_ref, b_ref, c_ref, v_ref, cu_ref, ct_ref, d_ref, y_ref):
        lane = HEADS + pl.program_id(1)
        dskip = _lane_col(d_ref[...], lane)

        def body(c, S):
            q = _ssd_chunk(c, S, x_ref, b_ref, c_ref, v_ref, cu_ref, ct_ref, lane)
            yd = _dot((q["CB"] * q["Lm"]).astype(BF16), q["xdt"].astype(BF16))
            z = _dot_nt(q["Cb"], S.astype(BF16))
            y_ref[0, 0, q["rows"], :] = yd + z * jnp.exp(q["A"]) + dskip * q["xc"]
            return q["S_new"]

        lax.fori_loop(0, nc, body, jnp.zeros((P, SSD_STATE), F32))

    return pl.pallas_call(
        kern, name=name, grid=(B, H), in_specs=_ssd_specs(L, nc, b_blk, c_blk),
        out_specs=pl.BlockSpec((1, 1, L, P), lambda b, h: (b, h, 0, 0)),
        out_shape=jax.ShapeDtypeStruct((B, H, L, P), F32),
        compiler_params=_cparams(("parallel", "arbitrary")),
    )(x4, u, u, vals, cums, cums_t, dvec)


def _ssd_bwd(x4, u, b_blk, c_blk, vals, cums, cums_t, dvec, dy4, name):
    B, H, L, P = x4.shape
    Q = Q_BLOCK
    nc = L // Q
    N = SSD_STATE
    E = HEADS // SSD_GROUPS

    def kern(x_ref, b_ref, c_ref, v_ref, cu_ref, ct_ref, d_ref, dy_ref,
             dx_ref, dB_ref, dC_ref, ddt_ref, dAc_ref, dAr_ref, dD_ref, s_all):
        b = pl.program_id(0)
        h = pl.program_id(1)
        lane = HEADS + h
        dskip = _lane_col(d_ref[...], lane)
        onehot = (lax.broadcasted_iota(jnp.int32, (1, LANES), 1) == lane).astype(F32)

        @pl.when(h % E == 0)
        def _():
            dB_ref[...] = jnp.zeros_like(dB_ref)
            dC_ref[...] = jnp.zeros_like(dC_ref)

        @pl.when(h == 0)
        def _():
            ddt_ref[...] = jnp.zeros_like(ddt_ref)
            dAc_ref[...] = jnp.zeros_like(dAc_ref)

        @pl.when((b == 0) & (h == 0))
        def _():
            dD_ref[...] = jnp.zeros_like(dD_ref)

        def fwd(c, S):
            s_all[c] = S
            return _ssd_chunk(c, S, x_ref, b_ref, c_ref, v_ref, cu_ref, ct_ref, lane)["S_new"]

        lax.fori_loop(0, nc, fwd, jnp.zeros((P, N), F32))
        last_row = lax.broadcasted_iota(jnp.int32, (Q, 1), 0) == Q - 1

        def bwd(i, carry):
            dS, dD = carry
            c = nc - 1 - i
            S = s_all[c]
            q = _ssd_chunk(c, S, x_ref, b_ref, c_ref, v_ref, cu_ref, ct_ref, lane)
            rows, xc, Bb, Cb, xdt, Lm, CB = q["rows"], q["xc"], q["Bb"], q["Cb"], q["xdt"], q["Lm"], q["CB"]
            eA = jnp.exp(q["A"])
            eAend = jnp.exp(q["Aend"])
            dy = dy_ref[0, 0, rows, :]
            dyb = dy.astype(BF16)
            Sb = S.astype(BF16)
            dD = dD + jnp.sum(jnp.sum(dy * xc, axis=1, keepdims=True), axis=0, keepdims=True)
            dM = _dot_nt(dyb, xdt.astype(BF16))
            dxdt = _dot_tn((CB * Lm).astype(BF16), dyb)
            dCBb = (dM * Lm).astype(BF16)
            G = dM * CB * Lm
            dAc = jnp.sum(G, axis=1, keepdims=True)
            dAr = -jnp.sum(G, axis=0, keepdims=True)
            dC = _dot(dCBb, Bb)
            dBm = _dot_tn(dCBb, Cb)
            z = _dot_nt(Cb, Sb)
            dAc = dAc + jnp.sum(dy * z, axis=1, keepdims=True) * eA
            dzb = (dy * eA).astype(BF16)
            dC = dC + _dot(dzb, Sb)
            dS_in = _dot_tn(dzb, Cb)
            dSb = dS.astype(BF16)
            dW = _dot_nt(Bb, dSb)
            dBm = dBm + _dot(q["W"].astype(BF16), dSb)
            dxdt = dxdt + dW * q["e_end"]
            de = jnp.sum(dW * xdt, axis=1, keepdims=True) * q["e_end"]
            dAend = (jnp.sum(jnp.sum(dS * S, axis=1, keepdims=True), axis=0, keepdims=True) * eAend
                     + jnp.sum(de, axis=0, keepdims=True))
            dAc = dAc - de + jnp.where(last_row, dAend, 0.0)
            dx_ref[0, 0, rows, :] = dskip * dy + dxdt * q["dt"]
            dB_ref[0, 0, rows, :] += dBm
            dC_ref[0, 0, rows, :] += dC
            ddt_ref[0, rows, :] += jnp.sum(dxdt * xc, axis=1, keepdims=True) * onehot
            dAc_ref[0, rows, :] += dAc * onehot
            dAr_ref[0, 0, pl.ds(c, 1), :] = dAr
            return dS * eAend + dS_in, dD

        _, dD = lax.fori_loop(0, nc, bwd, (jnp.zeros((P, N), F32), jnp.zeros((1, 1), F32)))
        dD_ref[...] += dD * onehot

    tm = pl.BlockSpec((1, L, LANES), lambda b, h: (b, 0, 0))
    grp = pl.BlockSpec((1, 1, L, N), lambda b, h: (b, h // E, 0, 0))
    xs = pl.BlockSpec((1, 1, L, P), lambda b, h: (b, h, 0, 0))
    return pl.pallas_call(
        kern, name=name, grid=(B, H), in_specs=_ssd_specs(L, nc, b_blk, c_blk) + [xs],
        out_specs=[xs, grp, grp, tm, tm, pl.BlockSpec((1, 1, nc, Q), lambda b, h: (b, h, 0, 0)),
                   pl.BlockSpec((1, LANES), lambda b, h: (0, 0))],
        out_shape=[jax.ShapeDtypeStruct((B, H, L, P), F32), jax.ShapeDtypeStruct((B, SSD_GROUPS, L, N), F32),
                   jax.ShapeDtypeStruct((B, SSD_GROUPS, L, N), F32), jax.ShapeDtypeStruct((B, L, LANES), F32),
                   jax.ShapeDtypeStruct((B, L, LANES), F32), jax.ShapeDtypeStruct((B, H, nc, Q), F32),
                   jax.ShapeDtypeStruct((1, LANES), F32)],
        scratch_shapes=[pltpu.VMEM((nc, P, N), F32)],
        compiler_params=_cparams(("arbitrary", "arbitrary")),
    )(x4, u, u, vals, cums, cums_t, dvec, dy4)


LRU_TR = 384
LRU_CB = 512


def _lru_gates(xc, ra, ix, p_ref, first):
    r = _sigmoid(ra + p_ref[0:1, :])
    i = _sigmoid(ix + p_ref[1:2, :])
    ls = _log_sigmoid(p_ref[2:3, :])
    log_a = LRU_C * r * ls
    a = jnp.exp(log_a)
    mult0 = jnp.sqrt(_one_minus_exp(2.0 * log_a))
    mult = jnp.where(first, 1.0, mult0)
    return r, i, ls, a, mult0, mult


def _lru_fwd(u, xc_off, ra, ix, proj3, gate_off, pvec, name):
    B, L, D = ra.shape
    TR, CB = _tile(L, LRU_TR, 8), LRU_CB
    nrt = L // TR

    def kern(xc_ref, ra_ref, ix_ref, g_ref, p_ref, y_ref, hs_ref, a_ref, pa_s, pu_s, carry):
        rt = pl.program_id(2)

        @pl.when(rt == 0)
        def _():
            carry[...] = jnp.zeros_like(carry)

        row = lax.broadcasted_iota(jnp.int32, (TR, 1), 0)
        first = (rt == 0) & (row == 0)
        xc = xc_ref[0]
        r, i, ls, a, mult0, mult = _lru_gates(xc, ra_ref[0], ix_ref[0], p_ref, first)
        a_ref[0] = a
        pa, pu = a, mult * (i * xc)
        sub = row % SUBLANES
        for s in (1, 2, 4):
            ok = sub >= s
            pu = jnp.where(ok, pa * pltpu.roll(pu, s, 0) + pu, pu)
            pa = jnp.where(ok, pa * pltpu.roll(pa, s, 0), pa)
        pa_s[...] = pa
        pu_s[...] = pu
        row8 = lax.broadcasted_iota(jnp.int32, (SUBLANES, 1), 0)

        def gbody(g, c):
            r8 = pl.ds(pl.multiple_of(g * SUBLANES, SUBLANES), SUBLANES)
            hg = pa_s[r8, :] * c + pu_s[r8, :]
            hs_ref[0, r8, :] = hg
            return jnp.sum(jnp.where(row8 == SUBLANES - 1, hg, 0.0), axis=0, keepdims=True)

        carry[...] = lax.fori_loop(0, TR // SUBLANES, gbody, carry[...])
        y_ref[0] = (hs_ref[0] * _gelu(g_ref[0])).astype(BF16)

    def win(off):
        assert off % CB == 0
        return pl.BlockSpec((1, TR, CB), functools.partial(lambda b, j, t, o: (b, t, j + o), o=off // CB))

    return pl.pallas_call(
        kern, name=name, grid=(B, D // CB, nrt),
        in_specs=[win(xc_off), win(0), win(0), win(gate_off), pl.BlockSpec((SUBLANES, CB), lambda b, j, t: (0, j))],
        out_specs=[win(0)] * 3,
        out_shape=[jax.ShapeDtypeStruct((B, L, D), BF16), jax.ShapeDtypeStruct((B, L, D), F32),
                   jax.ShapeDtypeStruct((B, L, D), F32)],
        scratch_shapes=[pltpu.VMEM((TR, CB), F32), pltpu.VMEM((TR, CB), F32), pltpu.VMEM((1, CB), F32)],
        compiler_params=_cparams(("parallel", "parallel", "arbitrary")),
    )(u, ra, ix, proj3, pvec)


def _lru_bwd(dy, proj3, gate_off, hs, a, u, xc_off, ra, ix, pvec, name):
    B, L, D = ra.shape
    TR, CB = _tile(L, LRU_TR, 8), LRU_CB
    nrt = L // TR

    def kern(dy_ref, g_ref, hs_ref, hsp_ref, a_ref, an_ref, xc_ref, ra_ref, ix_ref, p_ref,
             dg_ref, dra_ref, dix_ref, dxc_ref, dp_ref, pb_s, pd_s, g_s, carry):
        b = pl.program_id(1)
        rt = pl.program_id(2)
        t = nrt - 1 - rt

        @pl.when((b == 0) & (rt == 0))
        def _():
            dp_ref[...] = jnp.zeros_like(dp_ref)

        @pl.when(rt == 0)
        def _():
            carry[...] = jnp.zeros_like(carry)

        row = lax.broadcasted_iota(jnp.int32, (TR, 1), 0)
        gate, hsv, av, dyv = g_ref[0], hs_ref[0], a_ref[0], dy_ref[0]
        dg_ref[0] = (dyv * hsv * _dgelu(gate)).astype(BF16)
        a_next = jnp.where(t == nrt - 1, 0.0, an_ref[0, 0:1, :])
        pb = jnp.where(row == TR - 1, a_next, pltpu.roll(av, TR - 1, 0))
        pd = dyv * _gelu(gate)
        sub = row % SUBLANES
        for s in (1, 2, 4):
            ok = sub < SUBLANES - s
            pd = jnp.where(ok, pd + pb * pltpu.roll(pd, TR - s, 0), pd)
            pb = jnp.where(ok, pb * pltpu.roll(pb, TR - s, 0), pb)
        pb_s[...] = pb
        pd_s[...] = pd
        row8 = lax.broadcasted_iota(jnp.int32, (SUBLANES, 1), 0)

        def gbody(i, c):
            r8 = pl.ds(pl.multiple_of((TR // SUBLANES - 1 - i) * SUBLANES, SUBLANES), SUBLANES)
            gg = pd_s[r8, :] + pb_s[r8, :] * c
            g_s[r8, :] = gg
            return jnp.sum(jnp.where(row8 == 0, gg, 0.0), axis=0, keepdims=True)

        carry[...] = lax.fori_loop(0, TR // SUBLANES, gbody, carry[...])
        gv = g_s[...]
        h_first = jnp.where(t == 0, 0.0, hsp_ref[0, TR - 1:TR, :])
        hprev = jnp.where(row == 0, h_first, pltpu.roll(hsv, 1, 0))
        first = (t == 0) & (row == 0)
        xc = xc_ref[0]
        r, i, ls, a2, mult0, mult = _lru_gates(xc, ra_ref[0], ix_ref[0], p_ref, first)
        dxc_ref[0] = gv * mult * i
        dlog_a = gv * hprev * av + jnp.where(first, 0.0, gv * i * xc * (-(av * av) / mult0))
        dra = dlog_a * LRU_C * ls * r * (1.0 - r)
        dix = gv * mult * xc * i * (1.0 - i)
        dra_ref[0] = dra.astype(BF16)
        dix_ref[0] = dix.astype(BF16)
        dp_ref[0:1, :] += _colsum(dra)
        dp_ref[1:2, :] += _colsum(dix)
        dp_ref[2:3, :] += _colsum(dlog_a * LRU_C * r) * _sigmoid(-p_ref[2:3, :])

    def win(off, shift=0):
        assert off % CB == 0
        o = off // CB
        return pl.BlockSpec((1, TR, CB), lambda j, b, rt: (b, jnp.clip(nrt - 1 - rt + shift, 0, nrt - 1), j + o))

    return pl.pallas_call(
        kern, name=name, grid=(D // CB, B, nrt),
        in_specs=[win(0), win(gate_off), win(0), win(0, -1), win(0), win(0, 1), win(xc_off), win(0), win(0),
                  pl.BlockSpec((SUBLANES, CB), lambda j, b, rt: (0, j))],
        out_specs=[win(0)] * 4 + [pl.BlockSpec((SUBLANES, CB), lambda j, b, rt: (0, j))],
        out_shape=[jax.ShapeDtypeStruct((B, L, D), BF16)] * 3 + [jax.ShapeDtypeStruct((B, L, D), F32),
                                                                 jax.ShapeDtypeStruct((SUBLANES, D), F32)],
        scratch_shapes=[pltpu.VMEM((TR, CB), F32)] * 3 + [pltpu.VMEM((1, CB), F32)],
        compiler_params=_cparams(("parallel", "arbitrary", "arbitrary")),
    )(dy, proj3, hs, hs, a, a, u, ra, ix, pvec)


def _sum8(parts, name):
    _, R, C = parts.shape
    tr = _tile(R, 1024, ROW_ALIGN if parts.dtype.itemsize == 2 else SUBLANES)

    def kern(p_ref, o_ref):
        acc = p_ref[0].astype(F32)
        for d in range(1, N_DEV):
            acc = acc + p_ref[d].astype(F32)
        o_ref[...] = acc

    return pl.pallas_call(
        kern, name=name, grid=(R // tr,), in_specs=[pl.BlockSpec((N_DEV, tr, C), lambda i: (0, i, 0))],
        out_specs=pl.BlockSpec((tr, C), lambda i: (i, 0)), out_shape=jax.ShapeDtypeStruct((R, C), F32),
        compiler_params=_cparams(("parallel",)),
    )(parts)


def _adamw(w, g, m, v, name):
    shape = w.shape
    C = shape[-1] if w.ndim > 1 else shape[0]
    R = w.size // C
    w2, g2, m2, v2 = (t.reshape(R, C) for t in (w, g, m, v))
    tr = R
    for cand in range(8, min(R, 512) + 1, 8):
        if R % cand == 0:
            tr = cand

    def kern(w_ref, g_ref, m_ref, v_ref, d_ref, nm_ref, nv_ref):
        gv = g_ref[...]
        nm = ADAM_B1 * m_ref[...] + (1.0 - ADAM_B1) * gv
        nv = ADAM_B2 * v_ref[...] + (1.0 - ADAM_B2) * (gv * gv)
        m_hat = nm / (1.0 - ADAM_B1 ** ADAM_STEP)
        v_hat = nv / (1.0 - ADAM_B2 ** ADAM_STEP)
        d_ref[...] = -ADAM_LR * (m_hat / (jnp.sqrt(v_hat) + ADAM_EPS) + ADAM_WD * w_ref[...])
        nm_ref[...] = nm
        nv_ref[...] = nv

    spec = pl.BlockSpec((tr, C), lambda i: (i, 0))
    outs = pl.pallas_call(
        kern, name=name, grid=(R // tr,), in_specs=[spec] * 4, out_specs=[spec] * 3,
        out_shape=[jax.ShapeDtypeStruct((R, C), F32)] * 3, compiler_params=_cparams(("parallel",)),
    )(w2, g2, m2, v2)
    return tuple(o.reshape(shape) for o in outs)


MESH_ID = pl.DeviceIdType.MESH
ANY = pl.BlockSpec(memory_space=pl.ANY)
N_COPIES = N_DEV - 1
COMM_SCRATCH = [pltpu.SemaphoreType.DMA((N_COPIES,)), pltpu.SemaphoreType.DMA((N_COPIES,)), pltpu.SemaphoreType.DMA]


def _my_place():
    return lax.axis_index("x"), lax.axis_index("y"), lax.axis_index("c")


def _gather_phases(x_ref, out_ref, send_sems, recv_sems, local_sem):
    x, y, c = _my_place()
    me, sibling = (x, y, c), (x, y, 1 - c)
    chips = [(1 - x, y), (x, 1 - y), (1 - x, 1 - y)]

    def slab(px, py, pc):
        return out_ref.at[4 * px + 2 * py + pc]

    def copy(k, block, to, src=None):
        return pltpu.make_async_remote_copy(
            src_ref=slab(*block) if src is None else src, dst_ref=slab(*block),
            send_sem=send_sems.at[k], recv_sem=recv_sems.at[k], device_id=to, device_id_type=MESH_ID)

    mine = pltpu.make_async_copy(x_ref, slab(*me), local_sem)
    first = [copy(0, me, sibling, src=x_ref)] + [copy(1 + j, me, (*chip, c), src=x_ref) for j, chip in enumerate(chips)]
    passed = [copy(4 + j, (*chip, c), sibling) for j, chip in enumerate(chips)]

    def start():
        mine.start()
        for cp in first:
            cp.start()

    def forward():
        for j, chip in enumerate(chips):
            copy(1 + j, (*chip, c), me).wait_recv()
            passed[j].start()

    def finish():
        copy(0, sibling, me).wait_recv()
        for j, chip in enumerate(chips):
            copy(4 + j, (*chip, 1 - c), me).wait_recv()
        for cp in first + passed:
            cp.wait_send()
        mine.wait()

    return start, forward, finish


def _exchange_phases(p_ref, out_ref, send_sems, recv_sems, local_sem):
    x, y, c = _my_place()
    my_idx = 4 * x + 2 * y + c
    mine = pltpu.make_async_copy(p_ref.at[my_idx], out_ref.at[my_idx], local_sem)
    copies = []
    for k in range(1, N_DEV):
        px, py, pc = x ^ (k >> 2), y ^ ((k >> 1) & 1), c ^ (k & 1)
        copies.append(pltpu.make_async_remote_copy(
            src_ref=p_ref.at[4 * px + 2 * py + pc], dst_ref=out_ref.at[my_idx],
            send_sem=send_sems.at[k - 1], recv_sem=recv_sems.at[k - 1], device_id=(px, py, pc),
            device_id_type=MESH_ID))

    def start():
        mine.start()
        for cp in copies:
            cp.start()

    def finish():
        for cp in copies:
            cp.wait()
        mine.wait()

    return start, finish


def _ride(lin, total, phases):
    assert total >= 3
    marks = [0, total - 1] if len(phases) == 2 else [0, total // 2, total - 1]
    for mark, phase in zip(marks, phases):
        pl.when(lin == mark)(phase)


def _all_gather(xs, name):
    R, C = xs.shape

    def body(x_ref, out_ref, send_sems, recv_sems, local_sem):
        for phase in _gather_phases(x_ref, out_ref, send_sems, recv_sems, local_sem):
            phase()

    return pl.pallas_call(
        body, name=name, out_shape=jax.ShapeDtypeStruct((N_DEV, R, C), xs.dtype), in_specs=[ANY], out_specs=ANY,
        scratch_shapes=COMM_SCRATCH,
    )(xs)


def _exchange(parts, name):
    def body(p_ref, out_ref, send_sems, recv_sems, local_sem):
        for phase in _exchange_phases(p_ref, out_ref, send_sems, recv_sems, local_sem):
            phase()

    return pl.pallas_call(
        body, name=name, out_shape=jax.ShapeDtypeStruct(parts.shape, parts.dtype), in_specs=[ANY], out_specs=ANY,
        scratch_shapes=COMM_SCRATCH,
    )(parts)


D_XBC_EXTRA = 2 * SSD_GROUPS * SSD_STATE
SMALL_W = LANES
ROW_ALIGN = 16


def _layout(D):
    d_xbc = D + D_XBC_EXTRA
    off = dict(qkv=0, z=3 * D, merge=4 * D, gate=7 * D, conv=8 * D, xr=8 * D + d_xbc, small=9 * D + d_xbc)
    off["n_all"] = off["small"] + SMALL_W
    off["d_xbc"] = d_xbc
    off["conv_c"] = d_xbc + D
    return off


def _w_in_map(D):
    lo = _layout(D)
    widths = [("q", D, 0), ("k", D, D), ("v", D, 2 * D), ("f", HEADS, lo["small"]), ("z", D, lo["z"]),
              ("xbc", lo["d_xbc"], lo["conv"]), ("dt", HEADS, lo["small"] + HEADS), ("xr", D, lo["xr"]),
              ("gate", D, lo["gate"]), ("merge", 3 * D, lo["merge"])]
    out, o = [], 0
    for _, w, mine in widths:
        out.append((o, w, mine))
        o += w
    return out


def _padded(c):
    return -(-c // ROW_ALIGN) * ROW_ALIGN


def _permute_rows(src, pieces, name):
    R, C = src.shape
    n_out = sum(n for _, n in pieces)

    def kern(x_ref, o_ref):
        o = 0
        for start, n in pieces:
            if start is None:
                o_ref[o:o + n, :] = jnp.zeros((n, LANES), src.dtype)
            else:
                o_ref[o:o + n, :] = x_ref[start:start + n, :]
            o += n

    return pl.pallas_call(
        kern, name=name, grid=(C // LANES,), in_specs=[pl.BlockSpec((R, LANES), lambda i: (0, i))],
        out_specs=pl.BlockSpec((n_out, LANES), lambda i: (0, i)), out_shape=jax.ShapeDtypeStruct((n_out, C), src.dtype),
        compiler_params=_cparams(("parallel",)),
    )(src)


def _reorder_rows(wt, D, c, name="reorder_w_in"):
    cp = _padded(c)
    lo = _layout(D)
    pieces = []
    for a, w, mine in sorted(_w_in_map(D), key=lambda t: t[2]):
        b = a + w
        while a < b:
            j = a // c
            e = min(b, (j + 1) * c)
            pieces.append((j * cp + a - j * c, e - a))
            a = e
    pieces.append((None, lo["n_all"] - lo["small"] - 2 * HEADS))
    return _permute_rows(wt, pieces, name)


def _restore_rows(dwt, D, c, name="restore_w_in"):
    cp = _padded(c)
    segs = _w_in_map(D)
    pieces = []
    for j in range(N_DEV):
        a, b = j * c, (j + 1) * c
        for s0, w, mine in segs:
            lo_, hi_ = max(a, s0), min(b, s0 + w)
            if lo_ < hi_:
                pieces.append((mine + lo_ - s0, hi_ - lo_))
        if cp > c:
            pieces.append((None, cp - c))
    return _permute_rows(dwt, pieces, name)


def _block_diag(w):
    H, n, _ = w.shape
    tiled = jnp.tile(w.reshape(H * n, n), (1, H))
    r = lax.broadcasted_iota(jnp.int32, (H * n, H * n), 0) // n
    c = lax.broadcasted_iota(jnp.int32, (H * n, H * n), 1) // n
    return jnp.where(r == c, tiled, jnp.zeros_like(tiled))


def _diag_blocks(m, H):
    n = m.shape[0] // H
    return jnp.stack([m[h * n:(h + 1) * n, h * n:(h + 1) * n] for h in range(H)])


def _to_heads(t, B, L):
    return t.reshape(B, L, HEADS, HEAD_DIM).transpose(0, 2, 1, 3)


def _from_heads(t4):
    B, H, L, P = t4.shape
    return t4.transpose(0, 2, 1, 3).reshape(B * L, H * P)


def _rows_to_tm(rows):
    B, H, nc, Q = rows.shape
    return rows.reshape(B, H, nc * Q).transpose(0, 2, 1)


def _ffn_fwd(h, g, wgu_t, wd, tag):
    n = _norm_fwd(h, g, tag + "_norm")
    gu = _mm(n, wgu_t, tb=True, name=tag + "_up")
    act = _swiglu_fwd(gu, tag + "_act")
    out = _mm(act, wd, res=h, scale=0.5, name=tag + "_down")
    return out, (h, n, gu, act)


def _ffn_bwd(dh, saved, g, wgu_t, wd, tag):
    h, n, gu, act = saved
    dact = _mm(dh, wd, tb=True, scale=0.5, name=tag + "_down_dx")
    dwd = _mm(act, dh, ta=True, scale=0.5, name=tag + "_down_dw")
    dgu = _swiglu_bwd(gu, dact, tag + "_act_bwd")
    dwgu_t = _mm(dgu, n, ta=True, tn=1024, name=tag + "_up_dw")
    dn = _mm(dgu, wgu_t, name=tag + "_up_dx")
    dh_in, dg = _norm_bwd(h, dn, dh, g, tag + "_norm_bwd")
    return dh_in, dict(norm=dg, gu=dwgu_t, down=dwd)


def _augment(proj, cums, B, L, D):
    scale = HEAD_DIM ** -0.5
    q4 = _to_heads(proj[:, :D] * scale, B, L).astype(BF16)
    k4 = _to_heads(proj[:, D:2 * D], B, L).astype(BF16)
    v4 = _to_heads(proj[:, 2 * D:3 * D], B, L).astype(BF16)
    c = cums[..., :HEADS].transpose(0, 2, 1)
    rnd = lambda t: lax.reduce_precision(t, 8, 7)
    c_hi = rnd(c)
    c_mid = rnd(c - c_hi)
    c3 = jnp.stack([c_hi, c_mid, rnd(c - c_hi - c_mid)], axis=-1).astype(BF16)
    ones = jnp.ones((B, HEADS, L, 3), BF16)
    fill = lambda n: jnp.zeros((B, HEADS, L, n), BF16)
    qa = jnp.concatenate([q4, c3, ones, fill(AUG - AUG_B - 3)], axis=-1)
    ka = jnp.concatenate([k4, ones, -c3, fill(AUG - AUG_B - 3)], axis=-1)
    va = jnp.concatenate([v4, ones, fill(AUG - AUG_B)], axis=-1)
    return qa, ka, va


def _mixer_fwd(h, p, B, L, gather=None):
    T, D = h.shape
    lo = _layout(D)
    n = _norm_fwd(h, p["gm"], "mix_norm")
    proj = _mm(n, p["w_all_t"], tb=True, name="mix_in")
    proj3 = proj.reshape(B, L, lo["n_all"])
    vals, cums = _gate_prep(proj3, lo["small"] // LANES, p["small_bias"], p["avec"], "gate_prep")
    cums_t = cums[..., :2 * HEADS].transpose(0, 2, 1).reshape(B, 2 * HEADS, L // Q_BLOCK, Q_BLOCK)
    qa, ka, va = _augment(proj, cums, B, L, D)
    o4, lse, *gathered = _attn_fwd(qa, ka, va, "attn_fwd", gather)
    y_a = _from_heads(o4[..., :HEAD_DIM])
    xp = jnp.pad(proj3[:, :, lo["conv"]:lo["conv"] + lo["conv_c"]], ((0, 0), (PAD, 0), (0, 0)))
    u = _conv_fwd(xp, p["conv_w"], p["conv_b"], lo["d_xbc"] // LANES, "conv_fwd")
    x4 = _to_heads(u[..., :D], B, L)
    b_blk = D // LANES
    c_blk = b_blk + SSD_GROUPS * SSD_STATE // LANES
    y4 = _ssd_fwd(x4, u, b_blk, c_blk, vals, cums, cums_t, p["dvec"], "ssd_fwd")
    y_s = _from_heads(y4)
    yb = _gnorm_fwd(y_s, proj, lo["z"], p["ssd_norm"], "gnorm_fwd")
    u2 = u.reshape(T, lo["conv_c"])
    ra = _mm(u2, p["wa"], a_off=(0, lo["d_xbc"]), dims=(T, D, D), tk=512, name="lru_ra")
    ix = _mm(u2, p["wx"], a_off=(0, lo["d_xbc"]), dims=(T, D, D), tk=512, name="lru_ix")
    yc, hs, a = _lru_fwd(u, lo["d_xbc"], ra.reshape(B, L, D), ix.reshape(B, L, D), proj3, lo["gate"], p["pvec"],
                         "lru_fwd")
    yc = yc.reshape(T, D)
    pa = _mm(y_a, p["wba"], name="branch_attn")
    pb = _mm(yb, p["wbs"], name="branch_ssd")
    pc = _mm(yc, p["wbl"], name="branch_lru")
    mixed = _merge_fwd(proj, lo["merge"], pa, pb, pc, "merge_fwd")
    out = _mm(mixed, p["wout"], res=h, name="mix_out")
    saved = dict(h=h, n=n, proj=proj, qa=qa, ka=ka, va=va, vals=vals, cums=cums, cums_t=cums_t, o4=o4, lse=lse, y_a=y_a,
                 xp=xp, u=u, x4=x4, y_s=y_s, yb=yb, ra=ra, ix=ix, yc=yc, hs=hs, a=a, pa=pa, pb=pb, pc=pc, mixed=mixed)
    return out, saved, (gathered[0] if gathered else None)


def _mixer_bwd(dh, s, p, B, L, parts=None):
    T, D = dh.shape
    lo = _layout(D)
    proj, u = s["proj"], s["u"]
    proj3 = proj.reshape(B, L, lo["n_all"])
    g = {}
    dmixed = _mm(dh, p["wout"], tb=True, name="mix_out_dx")
    g["wout"] = _mm(s["mixed"], dh, ta=True, name="mix_out_dw")
    dpa, dpb, dpc, dmerge = _merge_bwd(dmixed, proj, lo["merge"], s["pa"], s["pb"], s["pc"], "merge_bwd")
    dy_a = _mm(dpa, p["wba"], tb=True, name="branch_attn_dx")
    g["wba"] = _mm(s["y_a"], dpa, ta=True, name="branch_attn_dw")
    dyb = _mm(dpb, p["wbs"], tb=True, name="branch_ssd_dx")
    g["wbs"] = _mm(s["yb"], dpb, ta=True, name="branch_ssd_dw")
    dyc = _mm(dpc, p["wbl"], tb=True, name="branch_lru_dx")
    g["wbl"] = _mm(s["yc"], dpc, ta=True, name="branch_lru_dw")
    dgate, dra, dix, dxc, g["pvec"] = _lru_bwd(dyc.reshape(B, L, D), proj3, lo["gate"], s["hs"], s["a"], u, lo["d_xbc"],
                                               s["ra"].reshape(B, L, D), s["ix"].reshape(B, L, D), p["pvec"], "lru_bwd")
    dra, dix = dra.reshape(T, D), dix.reshape(T, D)
    u2 = u.reshape(T, lo["conv_c"])
    g["wa"] = _mm(u2, dra, ta=True, a_off=(0, lo["d_xbc"]), dims=(D, D, T), tm=512, name="lru_ra_dw")
    g["wx"] = _mm(u2, dix, ta=True, a_off=(0, lo["d_xbc"]), dims=(D, D, T), tm=512, name="lru_ix_dw")
    dxc = _mm(dra, p["wa"], tb=True, res=dxc.reshape(T, D), name="lru_ra_dx")
    dxc = _mm(dix, p["wx"], tb=True, res=dxc, name="lru_ix_dx")
    dy_s, dz, g["ssd_norm"] = _gnorm_bwd(dyb, s["y_s"], proj, lo["z"], p["ssd_norm"], "gnorm_bwd")
    b_blk = D // LANES
    c_blk = b_blk + SSD_GROUPS * SSD_STATE // LANES
    dx4, dBg, dCg, ddt_tm, dAc_tm, dAr, g["dvec"] = _ssd_bwd(s["x4"], u, b_blk, c_blk, s["vals"], s["cums"], s["cums_t"],
                                                             p["dvec"], _to_heads(dy_s, B, L), "ssd_bwd")
    grp = lambda t: t.transpose(0, 2, 1, 3).reshape(B, L, SSD_GROUPS * SSD_STATE)
    du = jnp.concatenate([_from_heads(dx4).reshape(B, L, D), grp(dBg), grp(dCg), dxc.reshape(B, L, D)], axis=-1)
    dpp, g["conv_wb"] = _conv_bwd_pre(s["xp"], du, p["conv_w"], p["conv_b"], lo["d_xbc"] // LANES, "conv_bwd_pre")
    dconv = _conv_bwd_in(dpp, p["conv_w"], "conv_bwd_in")
    do4 = jnp.pad(_to_heads(dy_a, B, L), ((0, 0), (0, 0), (0, 0), (0, AUG - HEAD_DIM)))
    dqa, dka, dva, *recv = _attn_bwd(s["qa"], s["ka"], s["va"], s["o4"], do4, s["lse"], "attn_bwd", parts)
    scale = HEAD_DIM ** -0.5
    dqkv = jnp.stack([(dqa[..., :HEAD_DIM] * scale).astype(BF16), dka[..., :HEAD_DIM].astype(BF16),
                      dva[..., :HEAD_DIM].astype(BF16)], 0).transpose(1, 3, 0, 2, 4).reshape(T, 3 * D)
    dc_tm = (dqa[..., AUG_A] - dka[..., AUG_B]).transpose(0, 2, 1)
    drow_tm = jnp.concatenate([dc_tm, _rows_to_tm(dAr), jnp.zeros((B, L, LANES - 2 * HEADS), F32)], axis=-1)
    dsmall, g["small_bias"], g["avec"] = _gate_post(drow_tm, dAc_tm, ddt_tm, proj3, lo["small"] // LANES, s["vals"],
                                                    p["small_bias"], p["avec"], "gate_post")
    dproj = jnp.concatenate([dqkv, dz, dmerge, dgate.reshape(T, D), dconv.reshape(T, lo["conv_c"]),
                             dsmall.reshape(T, SMALL_W)], axis=1)
    g["w_all_t"] = _mm(dproj, s["n"], ta=True, tn=1024, name="mix_in_dw")
    dn = _mm(dproj, p["w_all_t"], name="mix_in_dx")
    dh_in, g["gm"] = _norm_bwd(s["h"], dn, dh, p["gm"], "mix_norm_bwd")
    return dh_in, g, (recv[0] if recv else None)


def _small_vec(a, b):
    return jnp.concatenate([a, b, jnp.zeros((LANES - 2 * HEADS,), F32)])[None, :]


def _layer_params(w):
    zeros16 = jnp.zeros((HEADS,), F32)
    pvec = jnp.concatenate([w["lru_b_a"][None], w["lru_b_x"][None], w["lru_lambda"][None],
                            jnp.zeros((SUBLANES - 3, w["lru_b_a"].shape[0]), F32)], axis=0)
    return dict(
        g1=w["ffn1_norm"][None], gu1=w["ffn1_w_gate_up"], d1=w["ffn1_w_down"],
        gm=w["mix_norm"][None], w_all_t=w["w_in"],
        small_bias=_small_vec(w["fox_forget_bias"], w["ssd_dt_bias"]),
        avec=_small_vec(zeros16, -jnp.exp(w["ssd_a_log"])), dvec=_small_vec(zeros16, w["ssd_d"]),
        conv_w=jnp.concatenate([w["ssd_conv_w"], w["lru_conv_w"]], axis=1),
        conv_b=jnp.concatenate([w["ssd_conv_b"], w["lru_conv_b"]])[None],
        ssd_norm=w["ssd_norm"][None],
        wa=_block_diag(w["lru_w_a"]).astype(BF16), wx=_block_diag(w["lru_w_x"]).astype(BF16), pvec=pvec,
        wba=w["w_branch_attn"], wbs=w["w_branch_ssd"], wbl=w["w_branch_lru"], wout=w["w_out"],
        g2=w["ffn2_norm"][None], gu2=w["ffn2_w_gate_up"], d2=w["ffn2_w_down"],
    )


def _layer_fwd(h, p, B, L, gather=None):
    h, s1 = _ffn_fwd(h, p["g1"], p["gu1"], p["d1"], "ffn1")
    h, sm, gathered = _mixer_fwd(h, p, B, L, gather)
    h, s2 = _ffn_fwd(h, p["g2"], p["gu2"], p["d2"], "ffn2")
    return h, (s1, sm, s2), gathered


def _layer_bwd(dh, saved, p, w, B, L, parts=None):
    s1, sm, s2 = saved
    D = dh.shape[1]
    d_xbc = D + D_XBC_EXTRA
    dh, f2 = _ffn_bwd(dh, s2, p["g2"], p["gu2"], p["d2"], "ffn2")
    dh, gm, recv = _mixer_bwd(dh, sm, p, B, L, parts)
    dh, f1 = _ffn_bwd(dh, s1, p["g1"], p["gu1"], p["d1"], "ffn1")
    sb, av = gm["small_bias"][0], gm["avec"][0]
    cw = gm["conv_wb"]
    grads = dict(
        ffn1_norm=f1["norm"][0], ffn1_w_gate_up=f1["gu"], ffn1_w_down=f1["down"],
        mix_norm=gm["gm"][0], w_in=gm["w_all_t"],
        fox_forget_bias=sb[:HEADS], ssd_conv_w=cw[:CONV_K, :d_xbc], ssd_conv_b=cw[CONV_K, :d_xbc],
        ssd_dt_bias=sb[HEADS:2 * HEADS], ssd_a_log=av[HEADS:2 * HEADS] * (-jnp.exp(w["ssd_a_log"])),
        ssd_d=gm["dvec"][0, HEADS:2 * HEADS], ssd_norm=gm["ssd_norm"][0],
        lru_conv_w=cw[:CONV_K, d_xbc:], lru_conv_b=cw[CONV_K, d_xbc:],
        lru_w_a=_diag_blocks(gm["wa"], HEADS), lru_b_a=gm["pvec"][0], lru_w_x=_diag_blocks(gm["wx"], HEADS),
        lru_b_x=gm["pvec"][1], lru_lambda=gm["pvec"][2],
        w_branch_attn=gm["wba"], w_branch_ssd=gm["wbs"], w_branch_lru=gm["wbl"], w_out=gm["wout"],
        ffn2_norm=f2["norm"][0], ffn2_w_gate_up=f2["gu"], ffn2_w_down=f2["down"],
    )
    return dh, grads, recv


LAYER_NAMES = ["ffn1_norm", "ffn1_w_gate_up", "ffn1_w_down", "mix_norm", "w_in", "fox_forget_bias", "ssd_conv_w",
               "ssd_conv_b", "ssd_dt_bias", "ssd_a_log", "ssd_d", "ssd_norm", "lru_conv_w", "lru_conv_b", "lru_w_a",
               "lru_b_a", "lru_w_x", "lru_b_x", "lru_lambda", "w_branch_attn", "w_branch_ssd", "w_branch_lru", "w_out",
               "ffn2_norm", "ffn2_w_gate_up", "ffn2_w_down"]
WEIGHT_NAMES = ["meta_tokens"] + LAYER_NAMES + ["final_norm"]


def _local_step(x, target, meta, final_norm, depth, layer_weights, pack_next=None, pack_grads=None):
    B, S, D = x.shape
    L = -(-(N_META + S) // Q_BLOCK) * Q_BLOCK
    h = jnp.concatenate([jnp.broadcast_to(meta[None], (B, N_META, D)), x,
                         jnp.zeros((B, L - N_META - S, D), F32)], axis=1).reshape(B * L, D)
    weights, params, saved = [], [], []
    gathered = None
    for l in range(depth):
        w = layer_weights(l, gathered)
        p = _layer_params(w)
        nxt = pack_next(l + 1) if (pack_next is not None and l + 1 < depth) else None
        h, s, gathered = _layer_fwd(h, p, B, L, nxt)
        weights.append(w)
        params.append(p)
        saved.append(s)
    tgt = jnp.pad(target, ((0, 0), (N_META, L - N_META - S), (0, 0))).reshape(B * L, D)
    dh, loss, dfinal = _loss_bwd(h, tgt, final_norm[None], L, S, "loss")
    grads = [None] * depth
    received, parts = {}, None
    for l in reversed(range(depth)):
        dh, grads[l], recv = _layer_bwd(dh, saved[l], params[l], weights[l], B, L, parts)
        if recv is not None:
            received[l + 1] = recv
        parts = pack_grads(grads[l]) if pack_grads is not None else None
    dh3 = dh.reshape(B, L, D)
    return (loss, dh3[:, N_META:N_META + S], jnp.sum(dh3[:, :N_META], axis=0), grads, dfinal[0], received, parts)


BIG_NAMES = ["ffn1_w_gate_up", "ffn1_w_down", "w_in", "w_branch_attn", "w_branch_ssd", "w_branch_lru", "w_out",
             "ffn2_w_gate_up", "ffn2_w_down"]
COL_SHARDED = {"ffn1_w_gate_up", "w_in", "ffn2_w_gate_up"}
SMALL_SHARDED = ["meta_tokens", "ssd_conv_w", "lru_conv_w"]
SMALL_NAMES = [n for n in LAYER_NAMES if n not in BIG_NAMES]


def _shard_rows(name, shape):
    return _padded(shape[1]) if name in COL_SHARDED else shape[0]


def _pack_shards(shards):
    rows = []
    for n in BIG_NAMES:
        s = shards[n]
        if n in COL_SHARDED:
            s = jnp.pad(s.T, ((0, _padded(s.shape[1]) - s.shape[1]), (0, 0)))
        rows.append(s)
    return jnp.concatenate(rows, axis=0)


def _unpack_gathered(gathered, shapes, D):
    out, o = {}, 0
    for n in BIG_NAMES:
        r = _shard_rows(n, shapes[n])
        out[n] = gathered[:, o:o + r].reshape(N_DEV * r, D)
        o += r
    out["w_in"] = _reorder_rows(out["w_in"], D, shapes["w_in"][1])
    return out


def _pack_full_grads(grads, shapes, D):
    slabs = []
    for n in BIG_NAMES:
        g = grads[n]
        if n == "w_in":
            g = _restore_rows(g, D, shapes[n][1])
        slabs.append(g.reshape(N_DEV, _shard_rows(n, shapes[n]), D))
    return jnp.concatenate(slabs, axis=1)


def _unpack_local(rows, shapes):
    out, o = {}, 0
    for n in BIG_NAMES:
        r = _shard_rows(n, shapes[n])
        blk = rows[o:o + r]
        out[n] = blk[:shapes[n][1]].T if n in COL_SHARDED else blk
        o += r
    return out


def _as_rows(flat):
    n = flat.shape[0]
    unit = LANES * SUBLANES
    total = -(-n // unit) * unit
    return jnp.pad(flat, (0, total - n)).reshape(total // LANES, LANES)


def _flatten_list(arrs):
    return _as_rows(jnp.concatenate([a.reshape(-1) for a in arrs]))


def _split_like(rows, shapes):
    flat = rows.reshape(-1)
    out, o = [], 0
    for s in shapes:
        n = math.prod(s)
        out.append(flat[o:o + n].reshape(s))
        o += n
    return out


def _gather_last(rows8, shape):
    lead, c = shape[:-1], shape[-1]
    t = rows8.reshape((N_DEV,) + tuple(lead) + (c,))
    return jnp.moveaxis(t, 0, -2).reshape(tuple(lead) + (N_DEV * c,))


def kernel(x, meta_tokens, ffn1_norm, ffn1_w_gate_up, ffn1_w_down, mix_norm, w_in, fox_forget_bias, ssd_conv_w, ssd_conv_b, ssd_dt_bias, ssd_a_log, ssd_d, ssd_norm, lru_conv_w, lru_conv_b, lru_w_a, lru_b_a, lru_w_x, lru_b_x, lru_lambda, w_branch_attn, w_branch_ssd, w_branch_lru, w_out, ffn2_norm, ffn2_w_gate_up, ffn2_w_down, final_norm, loss_target, m_meta_tokens, m_ffn1_norm, m_ffn1_w_gate_up, m_ffn1_w_down, m_mix_norm, m_w_in, m_fox_forget_bias, m_ssd_conv_w, m_ssd_conv_b, m_ssd_dt_bias, m_ssd_a_log, m_ssd_d, m_ssd_norm, m_lru_conv_w, m_lru_conv_b, m_lru_w_a, m_lru_b_a, m_lru_w_x, m_lru_b_x, m_lru_lambda, m_w_branch_attn, m_w_branch_ssd, m_w_branch_lru, m_w_out, m_ffn2_norm, m_ffn2_w_gate_up, m_ffn2_w_down, m_final_norm, v_meta_tokens, v_ffn1_norm, v_ffn1_w_gate_up, v_ffn1_w_down, v_mix_norm, v_w_in, v_fox_forget_bias, v_ssd_conv_w, v_ssd_conv_b, v_ssd_dt_bias, v_ssd_a_log, v_ssd_d, v_ssd_norm, v_lru_conv_w, v_lru_conv_b, v_lru_w_a, v_lru_b_a, v_lru_w_x, v_lru_b_x, v_lru_lambda, v_w_branch_attn, v_w_branch_ssd, v_w_branch_lru, v_w_out, v_ffn2_norm, v_ffn2_w_gate_up, v_ffn2_w_down, v_final_norm):
    weights = dict(zip(WEIGHT_NAMES, (meta_tokens, ffn1_norm, ffn1_w_gate_up, ffn1_w_down, mix_norm, w_in, fox_forget_bias, ssd_conv_w, ssd_conv_b, ssd_dt_bias, ssd_a_log, ssd_d, ssd_norm, lru_conv_w, lru_conv_b, lru_w_a, lru_b_a, lru_w_x, lru_b_x, lru_lambda, w_branch_attn, w_branch_ssd, w_branch_lru, w_out, ffn2_norm, ffn2_w_gate_up, ffn2_w_down, final_norm,)))
    mom1 = dict(zip(WEIGHT_NAMES, (m_meta_tokens, m_ffn1_norm, m_ffn1_w_gate_up, m_ffn1_w_down, m_mix_norm, m_w_in, m_fox_forget_bias, m_ssd_conv_w, m_ssd_conv_b, m_ssd_dt_bias, m_ssd_a_log, m_ssd_d, m_ssd_norm, m_lru_conv_w, m_lru_conv_b, m_lru_w_a, m_lru_b_a, m_lru_w_x, m_lru_b_x, m_lru_lambda, m_w_branch_attn, m_w_branch_ssd, m_w_branch_lru, m_w_out, m_ffn2_norm, m_ffn2_w_gate_up, m_ffn2_w_down, m_final_norm,)))
    mom2 = dict(zip(WEIGHT_NAMES, (v_meta_tokens, v_ffn1_norm, v_ffn1_w_gate_up, v_ffn1_w_down, v_mix_norm, v_w_in, v_fox_forget_bias, v_ssd_conv_w, v_ssd_conv_b, v_ssd_dt_bias, v_ssd_a_log, v_ssd_d, v_ssd_norm, v_lru_conv_w, v_lru_conv_b, v_lru_w_a, v_lru_b_a, v_lru_w_x, v_lru_b_x, v_lru_lambda, v_w_branch_attn, v_w_branch_ssd, v_w_branch_lru, v_w_out, v_ffn2_norm, v_ffn2_w_gate_up, v_ffn2_w_down, v_final_norm,)))
    depth = ffn1_norm.shape[0]
    D = x.shape[-1]
    my_idx = 4 * lax.axis_index("x") + 2 * lax.axis_index("y") + lax.axis_index("c")

    small_shapes = [weights[n].shape for n in SMALL_SHARDED]
    gathered = _all_gather(_flatten_list([weights[n] for n in SMALL_SHARDED]), "gather_small").reshape(N_DEV, -1)
    small_full, o = {}, 0
    for n, s in zip(SMALL_SHARDED, small_shapes):
        k = math.prod(s)
        small_full[n] = _gather_last(gathered[:, o:o + k], s)
        o += k

    shard_shapes = {n: weights[n].shape[1:] for n in BIG_NAMES}
    pack_next = lambda l: _pack_shards({n: weights[n][l].astype(BF16) for n in BIG_NAMES})

    def layer_weights(l, gathered):
        if gathered is None:
            gathered = _all_gather(pack_next(l), "gather_weights")
        w = _unpack_gathered(gathered, shard_shapes, D)
        for n in SMALL_NAMES:
            w[n] = small_full[n][l] if n in SMALL_SHARDED else weights[n][l]
        return w

    pack_grads = lambda g: _pack_full_grads(g, shard_shapes, D).astype(BF16)
    loss, dx, dmeta, grads, dfinal, received, parts = _local_step(
        x, loss_target, small_full["meta_tokens"], final_norm, depth, layer_weights, pack_next, pack_grads)
    received[0] = _exchange(parts, "exchange_grads")
    loss = lax.psum(loss[0, 0], ("x", "y", "c"))
    summed = {n: [] for n in WEIGHT_NAMES}
    for l in range(depth):
        local = _unpack_local(_sum8(received[l], "sum_grads"), shard_shapes)
        for n in BIG_NAMES:
            summed[n].append(local[n])

    small_list = [dmeta, dfinal] + [grads[l][n] for l in range(depth) for n in SMALL_NAMES]
    total = _sum8(_all_gather(_flatten_list(small_list), "gather_small_grads"), "sum_small_grads")
    parts = _split_like(total, [a.shape for a in small_list])
    full_small = {"meta_tokens": parts[0], "final_norm": parts[1]}
    for i, n in enumerate(SMALL_NAMES):
        full_small[n] = jnp.stack([parts[2 + l * len(SMALL_NAMES) + i] for l in range(depth)])
    grad = {}
    for n in WEIGHT_NAMES:
        if n in BIG_NAMES:
            grad[n] = jnp.stack(summed[n])
        elif n in SMALL_SHARDED:
            c = weights[n].shape[-1]
            grad[n] = lax.dynamic_slice_in_dim(full_small[n], my_idx * c, c, axis=full_small[n].ndim - 1)
        else:
            grad[n] = full_small[n]

    delta, new_m, new_v = {}, {}, {}
    for n in WEIGHT_NAMES:
        delta[n], new_m[n], new_v[n] = _adamw(weights[n], grad[n], mom1[n], mom2[n], "adamw_" + n)
    return (loss, dx, *[grad[n] for n in WEIGHT_NAMES], *[delta[n] for n in WEIGHT_NAMES],
            *[new_m[n] for n in WEIGHT_NAMES], *[new_v[n] for n in WEIGHT_NAMES])
```

```python
import functools
import math

import jax
import jax.numpy as jnp
from jax import lax
from jax.experimental import pallas as pl
from jax.experimental.pallas import tpu as pltpu

F32 = jnp.float32
BF16 = jnp.bfloat16

N_DEV = 8
N_META = 16
Q_BLOCK = 128
NORM_EPS = 1e-6
HEADS = 16
HEAD_DIM = 64
SSD_GROUPS = 2
SSD_STATE = 128
CONV_K = 4
LRU_C = 8.0
ADAM_LR, ADAM_B1, ADAM_B2, ADAM_EPS, ADAM_WD, ADAM_STEP = 0.001, 0.9, 0.999, 1e-08, 0.01, 10

LANES = 128
SUBLANES = 8
VMEM_LIMIT = 56 * 1024 * 1024
NEG = -1e30
MM_TILE = 1408
MM_VMEM = 40 * 1024 * 1024


def _cparams(sem=None):
    return pltpu.CompilerParams(dimension_semantics=sem, vmem_limit_bytes=VMEM_LIMIT)


def _tile(dim, target, mult=LANES):
    if dim <= target:
        return dim
    best = None
    for t in range(mult, target + 1, mult):
        if dim % t == 0:
            best = t
    assert best is not None, (dim, target)
    return best


def _sigmoid(x):
    return 1.0 / (1.0 + jnp.exp(-x))


def _log1p_exp_neg_abs(x):
    e = jnp.exp(-jnp.abs(x))
    u = 1.0 + e
    return jnp.where(u == 1.0, e, jnp.log(u) * (e / jnp.where(u == 1.0, 1.0, u - 1.0)))


def _log_sigmoid(x):
    return jnp.minimum(x, 0.0) - _log1p_exp_neg_abs(x)


def _softplus(x):
    return jnp.maximum(x, 0.0) + _log1p_exp_neg_abs(x)


def _one_minus_exp(y):
    u = jnp.exp(y)
    safe = jnp.where(u == 1.0, 0.5, u)
    return jnp.where(u == 1.0, -y, (1.0 - u) * y / jnp.log(safe))


def _silu(x):
    return x * _sigmoid(x)


def _dsilu(x):
    s = _sigmoid(x)
    return s * (1.0 + x * (1.0 - s))


_GELU_C = math.sqrt(2.0 / math.pi)


def _gelu(x):
    return 0.5 * x * (1.0 + jnp.tanh(_GELU_C * (x + 0.044715 * x * x * x)))


def _dgelu(x):
    t = jnp.tanh(_GELU_C * (x + 0.044715 * x * x * x))
    return 0.5 * (1.0 + t) + 0.5 * x * (1.0 - t * t) * _GELU_C * (1.0 + 3.0 * 0.044715 * x * x)


def _split3_dot(tri, x):
    hi = x.astype(BF16)
    r1 = x - hi.astype(F32)
    mid = r1.astype(BF16)
    lo = (r1 - mid.astype(F32)).astype(BF16)
    t = tri.astype(BF16)
    d = lambda p: jnp.dot(t, p, preferred_element_type=F32)
    return d(hi) + d(mid) + d(lo)


def _lower_tri(n, strict=False):
    r = lax.broadcasted_iota(jnp.int32, (n, n), 0)
    c = lax.broadcasted_iota(jnp.int32, (n, n), 1)
    return (c < r) if strict else (c <= r)


def _mm(a, b, *, ta=False, tb=False, out_dtype=F32, res=None, scale=None, tm=None, tn=None, tk=None,
        a_off=(0, 0), b_off=(0, 0), dims=None, name):
    if dims is None:
        M, K = (a.shape[1], a.shape[0]) if ta else a.shape
        N = b.shape[0] if tb else b.shape[1]
    else:
        M, N, K = dims
    tk = tk or (K if K <= 2816 else _tile(K, 1408))
    nk_ = K // tk
    pick_m, pick_n = tm is None, tn is None
    tm = tm or _tile(M, MM_TILE)
    tn = tn or _tile(N, MM_TILE)

    def vmem(tm_, tn_):
        a_b = tm_ * tk * a.dtype.itemsize + (tm_ * tk * 2 if a.dtype != BF16 else 0)
        b_b = tn_ * tk * b.dtype.itemsize + (tn_ * tk * 2 if b.dtype != BF16 else 0)
        o_b = tm_ * tn_ * (jnp.dtype(out_dtype).itemsize + (4 if res is not None else 0))
        return 2 * (a_b + b_b + o_b) + (tm_ * tn_ * 4 if nk_ > 1 else 0) + tm_ * tn_ * 4

    while vmem(tm, tn) > MM_VMEM and (pick_m or pick_n):
        if pick_m and (tm >= tn or not pick_n) and tm > LANES:
            tm = _tile(M, tm - LANES)
        elif pick_n and tn > LANES:
            tn = _tile(N, tn - LANES)
        else:
            break
    assert M % tm == 0 and N % tn == 0 and K % tk == 0, (name, M, N, K, tm, tn, tk)
    nk = K // tk
    ca = 0 if ta else 1
    cb = 1 if tb else 0

    def blk(rows, cols, off):
        assert off[0] % rows == 0 and off[1] % cols == 0, (name, off, rows, cols)
        return off[0] // rows, off[1] // cols

    if ta:
        ao = blk(tk, tm, a_off)
        a_spec = pl.BlockSpec((tk, tm), lambda i, j, k: (k + ao[0], i + ao[1]))
    else:
        ao = blk(tm, tk, a_off)
        a_spec = pl.BlockSpec((tm, tk), lambda i, j, k: (i + ao[0], k + ao[1]))
    if tb:
        bo = blk(tn, tk, b_off)
        b_spec = pl.BlockSpec((tn, tk), lambda i, j, k: (j + bo[0], k + bo[1]))
    else:
        bo = blk(tk, tn, b_off)
        b_spec = pl.BlockSpec((tk, tn), lambda i, j, k: (k + bo[0], j + bo[1]))
    o_spec = pl.BlockSpec((tm, tn), lambda i, j, k: (i, j))
    in_specs = [a_spec, b_spec] + ([o_spec] if res is not None else [])
    has_res = res is not None

    def kern(*refs):
        if has_res:
            a_ref, b_ref, r_ref, o_ref = refs[:4]
            scr = refs[4:]
        else:
            a_ref, b_ref, o_ref = refs[:3]
            r_ref = None
            scr = refs[3:]
        p = lax.dot_general(a_ref[...].astype(BF16), b_ref[...].astype(BF16), (((ca,), (cb,)), ((), ())),
                            preferred_element_type=F32)

        def fin(val):
            if scale is not None:
                val = val * scale
            if has_res:
                val = r_ref[...] + val
            o_ref[...] = val.astype(out_dtype)

        if nk == 1:
            fin(p)
        else:
            acc = scr[0]
            k = pl.program_id(2)

            @pl.when(k == 0)
            def _():
                acc[...] = p

            @pl.when(k > 0)
            def _():
                acc[...] += p

            @pl.when(k == nk - 1)
            def _():
                fin(acc[...])

    args = (a, b) + ((res,) if has_res else ())
    return pl.pallas_call(
        kern, name=name, grid=(M // tm, N // tn, nk), in_specs=in_specs, out_specs=o_spec,
        out_shape=jax.ShapeDtypeStruct((M, N), out_dtype),
        scratch_shapes=[pltpu.VMEM((tm, tn), F32)] if nk > 1 else [],
        compiler_params=_cparams(("parallel", "parallel", "arbitrary")),
    )(*args)


def _rows(body, tiled, full, outs, accs, *, tr, name, T):
    assert T % tr == 0
    in_specs = []
    for arr, width, off in tiled:
        assert off % width == 0, (name, off, width)
        in_specs.append(pl.BlockSpec((tr, width), functools.partial(lambda i, o: (i, o), o=off // width)))
    for arr in full:
        in_specs.append(pl.BlockSpec(arr.shape, lambda i: (0, 0)))
    out_specs = [pl.BlockSpec((tr, w), lambda i: (i, 0)) for w, _ in outs]
    out_specs += [pl.BlockSpec(s, lambda i: (0, 0)) for s, _ in accs]
    out_shape = [jax.ShapeDtypeStruct((T, w), d) for w, d in outs] + [jax.ShapeDtypeStruct(s, d) for s, d in accs]
    nt, nf, no = len(tiled), len(full), len(outs)

    def kern(*refs):
        i = pl.program_id(0)
        acc_refs = refs[nt + nf + no:]

        @pl.when(i == 0)
        def _():
            for r in acc_refs:
                r[...] = jnp.zeros(r.shape, r.dtype)

        body(i, refs[:nt], refs[nt:nt + nf], refs[nt + nf:nt + nf + no], acc_refs)

    res = pl.pallas_call(
        kern, name=name, grid=(T // tr,), in_specs=in_specs, out_specs=out_specs, out_shape=out_shape,
        compiler_params=_cparams(("arbitrary",)),
    )(*[t[0] for t in tiled], *full)
    return res


def _colsum(x):
    return jnp.sum(x, axis=0, keepdims=True)


def _norm_fwd(h, g, name):
    T, D = h.shape

    def body(i, t, f, o, a):
        x = t[0][...]
        r = lax.rsqrt(jnp.mean(x * x, axis=-1, keepdims=True) + NORM_EPS)
        o[0][...] = (x * r * f[0][...]).astype(BF16)

    return _rows(body, [(h, D, 0)], [g], [(D, BF16)], [], tr=_tile(T, 768, 8), name=name, T=T)[0]


def _norm_bwd(h, dn, dh, g, name):
    T, D = h.shape

    def body(i, t, f, o, a):
        x, dnv, dhv = t[0][...], t[1][...], t[2][...]
        r = lax.rsqrt(jnp.mean(x * x, axis=-1, keepdims=True) + NORM_EPS)
        xh = x * r
        dng = dnv * f[0][...]
        o[0][...] = dhv + r * (dng - xh * jnp.mean(dng * xh, axis=-1, keepdims=True))
        a[0][...] += _colsum(dnv * xh)

    return _rows(body, [(h, D, 0), (dn, D, 0), (dh, D, 0)], [g], [(D, F32)], [((1, D), F32)],
                 tr=_tile(T, 384, 8), name=name, T=T)


def _swiglu_fwd(gu, name):
    T, F2 = gu.shape
    F = F2 // 2

    def body(i, t, f, o, a):
        o[0][...] = (_silu(t[0][...]) * t[1][...]).astype(BF16)

    return _rows(body, [(gu, F, 0), (gu, F, F)], [], [(F, BF16)], [], tr=_tile(T, 256, 8), name=name, T=T)[0]


def _swiglu_bwd(gu, da, name):
    T, F2 = gu.shape
    F = F2 // 2

    def body(i, t, f, o, a):
        gv, uv, dav = t[0][...], t[1][...], t[2][...]
        o[0][:, :F] = (dav * uv * _dsilu(gv)).astype(BF16)
        o[0][:, F:] = (dav * _silu(gv)).astype(BF16)

    return _rows(body, [(gu, F, 0), (gu, F, F), (da, F, 0)], [], [(F2, BF16)], [], tr=_tile(T, 256, 8),
                 name=name, T=T)[0]


def _merge_fwd(proj, off, pa, pb, pc, name):
    T, D = pa.shape

    def body(i, t, f, o, a):
        o[0][...] = (_sigmoid(t[0][...]) * t[3][...] + _sigmoid(t[1][...]) * t[4][...]
                     + _sigmoid(t[2][...]) * t[5][...]).astype(BF16)

    tiled = [(proj, D, off), (proj, D, off + D), (proj, D, off + 2 * D), (pa, D, 0), (pb, D, 0), (pc, D, 0)]
    return _rows(body, tiled, [], [(D, BF16)], [], tr=_tile(T, 384, 8), name=name, T=T)[0]


def _merge_bwd(dmixed, proj, off, pa, pb, pc, name):
    T, D = pa.shape

    def body(i, t, f, o, a):
        dm = t[0][...]
        for k in range(3):
            g = _sigmoid(t[1 + k][...])
            o[k][...] = (dm * g).astype(BF16)
            o[3][:, k * D:(k + 1) * D] = (dm * t[4 + k][...] * g * (1.0 - g)).astype(BF16)

    tiled = [(dmixed, D, 0), (proj, D, off), (proj, D, off + D), (proj, D, off + 2 * D), (pa, D, 0), (pb, D, 0),
             (pc, D, 0)]
    return _rows(body, tiled, [], [(D, BF16)] * 3 + [(3 * D, BF16)], [], tr=_tile(T, 384, 8), name=name, T=T)


def _gnorm_fwd(y, proj, zoff, nw, name):
    T, D = y.shape
    gs = D // SSD_GROUPS

    def body(i, t, f, o, a):
        s = t[0][...] * _silu(t[1][...])
        for g in range(SSD_GROUPS):
            sg = s[:, g * gs:(g + 1) * gs]
            r = lax.rsqrt(jnp.mean(sg * sg, axis=-1, keepdims=True) + NORM_EPS)
            o[0][:, g * gs:(g + 1) * gs] = (sg * r * f[0][:, g * gs:(g + 1) * gs]).astype(BF16)

    return _rows(body, [(y, D, 0), (proj, D, zoff)], [nw], [(D, BF16)], [], tr=_tile(T, 384, 8), name=name, T=T)[0]


def _gnorm_bwd(dout, y, proj, zoff, nw, name):
    T, D = y.shape
    gs = D // SSD_GROUPS

    def body(i, t, f, o, a):
        dov, yv, zv = t[0][...], t[1][...], t[2][...]
        sz = _silu(zv)
        s = yv * sz
        dsz = _dsilu(zv)
        for g in range(SSD_GROUPS):
            sl = slice(g * gs, (g + 1) * gs)
            sg = s[:, sl]
            r = lax.rsqrt(jnp.mean(sg * sg, axis=-1, keepdims=True) + NORM_EPS)
            sh = sg * r
            dog = dov[:, sl]
            dng = dog * f[0][:, sl]
            ds = r * (dng - sh * jnp.mean(dng * sh, axis=-1, keepdims=True))
            o[0][:, sl] = ds * sz[:, sl]
            o[1][:, sl] = (ds * yv[:, sl] * dsz[:, sl]).astype(BF16)
            a[0][:, sl] += _colsum(dog * sh)

    return _rows(body, [(dout, D, 0), (y, D, 0), (proj, D, zoff)], [nw], [(D, F32), (D, BF16)], [((1, D), F32)],
                 tr=_tile(T, 384, 8), name=name, T=T)


def _loss_bwd(h, tgt, g, seq_len, n_real, name):
    T, D = h.shape
    tr = _tile(seq_len, 384, 8)
    per_seq = seq_len // tr

    def body(i, t, f, o, a):
        x, tg = t[0][...], t[1][...]
        pos = (i % per_seq) * tr + lax.broadcasted_iota(jnp.int32, (tr, 1), 0)
        valid = (pos >= N_META) & (pos < N_META + n_real)
        r = lax.rsqrt(jnp.mean(x * x, axis=-1, keepdims=True) + NORM_EPS)
        xh = x * r
        e = jnp.where(valid, xh * f[0][...] - tg, 0.0)
        a[0][...] += jnp.zeros((1, LANES), F32) + 0.5 * jnp.sum(jnp.sum(e * e, axis=-1, keepdims=True) / D,
                                                              axis=0, keepdims=True)
        dy = e / D
        dng = dy * f[0][...]
        o[0][...] = r * (dng - xh * jnp.mean(dng * xh, axis=-1, keepdims=True))
        a[1][...] += _colsum(dy * xh)

    return _rows(body, [(h, D, 0), (tgt, D, 0)], [g], [(D, F32)], [((1, LANES), F32), ((1, D), F32)], tr=tr,
                 name=name, T=T)


def _lane_is_attn(shape):
    return lax.broadcasted_iota(jnp.int32, shape, len(shape) - 1) < HEADS


def _gate_prep(proj3, col_blk, bias, avec, name):
    B, L, _ = proj3.shape
    Q = Q_BLOCK
    nc = L // Q

    def kern(x_ref, b_ref, a_ref, v_ref, c_ref, carry):
        c = pl.program_id(1)

        @pl.when(c == 0)
        def _():
            carry[...] = jnp.zeros_like(carry)

        x = x_ref[0] + b_ref[...]
        attn = _lane_is_attn(x.shape)
        v = jnp.where(attn, _log_sigmoid(x), _softplus(x))
        w = jnp.where(attn, v, v * a_ref[...])
        cs = _split3_dot(_lower_tri(Q), w) + jnp.where(attn[:1], carry[...], 0.0)
        v_ref[0] = v
        c_ref[0] = cs
        rows = lax.broadcasted_iota(jnp.int32, (Q, 1), 0)
        carry[...] = jnp.sum(jnp.where(rows == Q - 1, cs, 0.0), axis=0, keepdims=True)

    blk = pl.BlockSpec((1, Q, LANES), lambda b, c: (b, c, 0))
    vec = pl.BlockSpec((1, LANES), lambda b, c: (0, 0))
    return pl.pallas_call(
        kern, name=name, grid=(B, nc),
        in_specs=[pl.BlockSpec((1, Q, LANES), lambda b, c: (b, c, col_blk)), vec, vec],
        out_specs=[blk, blk], out_shape=[jax.ShapeDtypeStruct((B, L, LANES), F32)] * 2,
        scratch_shapes=[pltpu.VMEM((1, LANES), F32)],
        compiler_params=_cparams(("parallel", "arbitrary")),
    )(proj3, bias, avec)


def _gate_post(drow, dcol, ddt, proj3, col_blk, vals, bias, avec, name):
    B, L, _ = proj3.shape
    Q = Q_BLOCK
    nc = L // Q

    def kern(dr_ref, dc_ref, dd_ref, x_ref, v_ref, b_ref, a_ref, o_ref, db_ref, da_ref, carry):
        b = pl.program_id(0)
        c = pl.program_id(1)

        @pl.when((b == 0) & (c == 0))
        def _():
            db_ref[...] = jnp.zeros_like(db_ref)
            da_ref[...] = jnp.zeros_like(da_ref)

        @pl.when(c == 0)
        def _():
            carry[...] = jnp.zeros_like(carry)

        x = x_ref[0] + b_ref[...]
        attn = _lane_is_attn(x.shape)
        dcs = dr_ref[0] + dc_ref[0]
        upper = jnp.logical_not(_lower_tri(Q, strict=True))
        rc = _split3_dot(upper, dcs) + jnp.where(attn[:1], carry[...], 0.0)
        rows = lax.broadcasted_iota(jnp.int32, (Q, 1), 0)
        carry[...] = jnp.sum(jnp.where(rows == 0, rc, 0.0), axis=0, keepdims=True)
        dv = jnp.where(attn, rc, dd_ref[0] + rc * a_ref[...])
        dpre = dv * jnp.where(attn, _sigmoid(-x), _sigmoid(x))
        o_ref[0] = dpre.astype(BF16)
        db_ref[...] += _colsum(dpre)
        da_ref[...] += _colsum(jnp.where(attn, 0.0, rc * v_ref[0]))

    rev = pl.BlockSpec((1, Q, LANES), lambda b, c: (b, nc - 1 - c, 0))
    vec = pl.BlockSpec((1, LANES), lambda b, c: (0, 0))
    return pl.pallas_call(
        kern, name=name, grid=(B, nc),
        in_specs=[rev, rev, rev, pl.BlockSpec((1, Q, LANES), lambda b, c: (b, nc - 1 - c, col_blk)), rev, vec, vec],
        out_specs=[rev, vec, vec],
        out_shape=[jax.ShapeDtypeStruct((B, L, LANES), BF16), jax.ShapeDtypeStruct((1, LANES), F32),
                   jax.ShapeDtypeStruct((1, LANES), F32)],
        scratch_shapes=[pltpu.VMEM((1, LANES), F32)],
        compiler_params=_cparams(("arbitrary", "arbitrary")),
    )(drow, dcol, ddt, proj3, vals, bias, avec)


def _lane_col(tile, lane):
    sel = lax.broadcasted_iota(jnp.int32, tile.shape, 1) == lane
    return jnp.sum(jnp.where(sel, tile, 0.0), axis=1, keepdims=True)


AUG = LANES
AUG_A = HEAD_DIM
AUG_B = HEAD_DIM + 3


def _split3(x):
    hi = x.astype(BF16).astype(F32)
    mid = (x - hi).astype(BF16).astype(F32)
    lo = (x - hi - mid).astype(BF16).astype(F32)
    return hi, mid, lo


def _put3(base, lane, first, x):
    hi, mid, lo = _split3(x)
    return jnp.where(lane == first, hi, jnp.where(lane == first + 1, mid, jnp.where(lane == first + 2, lo, base)))


HP = 2


def _other_half(x):
    return pltpu.roll(x, HEAD_DIM, 1)


def _attn_pack(proj3, cums, name):
    B, L, _ = proj3.shape
    D = HEADS * HEAD_DIM
    nh = HEADS // HP
    tr = _tile(L, 384)
    scale = HEAD_DIM ** -0.5

    def kern(q_ref, k_ref, v_ref, c_ref, qa_ref, ka_ref, va_ref):
        hp = pl.program_id(2)
        lane = lax.broadcasted_iota(jnp.int32, (tr, AUG), 1)
        head = lane < HEAD_DIM
        ones_a = jnp.where((lane >= AUG_A) & (lane < AUG_A + 3), 1.0, 0.0)
        ones_b = jnp.where((lane >= AUG_B) & (lane < AUG_B + 3), 1.0, 0.0)
        ct = c_ref[0]
        for hh in range(HP):
            c = _lane_col(ct, HP * hp + hh)
            sel = (lambda t: t) if hh == 0 else _other_half
            qa_ref[0, hh] = jnp.where(head, sel(q_ref[0]) * scale, _put3(ones_b, lane, AUG_A, c)).astype(BF16)
            ka_ref[0, hh] = jnp.where(head, sel(k_ref[0]), _put3(ones_a, lane, AUG_B, -c)).astype(BF16)
            va_ref[0, hh] = jnp.where(head, sel(v_ref[0]), ones_a).astype(BF16)

    def win(first):
        return pl.BlockSpec((1, tr, LANES), lambda b, i, h: (b, i, first + h))

    out = pl.BlockSpec((1, HP, tr, AUG), lambda b, i, h: (b, h, i, 0))
    return pl.pallas_call(
        kern, name=name, grid=(B, L // tr, nh),
        in_specs=[win(0), win(D // LANES), win(2 * D // LANES), pl.BlockSpec((1, tr, LANES), lambda b, i, h: (b, i, 0))],
        out_specs=[out] * 3, out_shape=[jax.ShapeDtypeStruct((B, HEADS, L, AUG), BF16)] * 3,
        compiler_params=_cparams(("parallel", "parallel", "arbitrary")),
    )(proj3, proj3, proj3, cums)


def _attn_fwd(qa, ka, va, name, gather=None):
    B, H, L, _ = qa.shape
    tq = _tile(L, 384)
    nq = L // tq
    nh = H // HP
    comm = gather is not None

    def kern(*refs):
        if comm:
            q_ref, k_ref, v_ref, x_ref, y_ref, l_ref, g_ref, send_sems, recv_sems, local_sem = refs
        else:
            q_ref, k_ref, v_ref, y_ref, l_ref = refs
        qi = pl.program_id(2)
        if comm:
            _ride((pl.program_id(0) * nh + pl.program_id(1)) * nq + qi, B * nh * nq,
                  _gather_phases(x_ref, g_ref, send_sems, recv_sems, local_sem))
        qs = [q_ref[0, hh] for hh in range(HP)]
        causal = _lower_tri(tq)

        def step(j, carry, masked):
            rows = pl.ds(pl.multiple_of(j * tq, tq), tq)
            out = []
            for hh in range(HP):
                m, acc = carry[hh]
                s = _dot_nt(qs[hh], k_ref[0, hh, rows, :])
                if masked:
                    s = jnp.where(causal, s, NEG)
                m_new = jnp.maximum(m, jnp.max(s, axis=1, keepdims=True))
                p = jnp.exp(s - m_new)
                out.append((m_new, jnp.exp(m - m_new) * acc + _dot(p.astype(BF16), v_ref[0, hh, rows, :])))
            return tuple(out)

        init = tuple((jnp.full((tq, 1), NEG, F32), jnp.zeros((tq, AUG), F32)) for _ in range(HP))
        carry = lax.fori_loop(0, qi, lambda j, c: step(j, c, False), init)
        outs = []
        for hh, (m, acc) in enumerate(step(qi, carry, True)):
            l = _lane_col(acc, AUG_A)
            outs.append(acc / l)
            l_ref[0, hh] = m + jnp.log(l)
        head = lax.broadcasted_iota(jnp.int32, (tq, AUG), 1) < HEAD_DIM
        y_ref[0] = jnp.where(head, outs[0], _other_half(outs[1]))

    qspec = pl.BlockSpec((1, HP, tq, AUG), lambda b, h, i: (b, h, i, 0))
    kvspec = pl.BlockSpec((1, HP, L, AUG), lambda b, h, i: (b, h, 0, 0))
    lspec = pl.BlockSpec((1, HP, tq, 1), lambda b, h, i: (b, h, i, 0))
    yspec = pl.BlockSpec((1, tq, LANES), lambda b, h, i: (b, i, h))
    out_shape = [jax.ShapeDtypeStruct((B, L, H * HEAD_DIM), F32), jax.ShapeDtypeStruct((B, H, L, 1), F32)]
    if comm:
        out_shape.append(jax.ShapeDtypeStruct((N_DEV,) + gather.shape, gather.dtype))
    return pl.pallas_call(
        kern, name=name, grid=(B, nh, nq), in_specs=[qspec, kvspec, kvspec] + ([ANY] if comm else []),
        out_specs=[yspec, lspec] + ([ANY] if comm else []), out_shape=out_shape,
        scratch_shapes=COMM_SCRATCH if comm else [],
        compiler_params=_cparams(("arbitrary",) * 3 if comm else ("parallel", "parallel", "arbitrary")),
    )(qa, ka, va, *([gather] if comm else []))


def _attn_bwd(qa, ka, va, y, dy, lse, name, parts=None):
    B, H, L, _ = qa.shape
    tq = _tile(L, 384)
    nq = L // tq
    nh = H // HP
    comm = parts is not None
    scale = HEAD_DIM ** -0.5

    def kern(*refs):
        if comm:
            (q_ref, k_ref, v_ref, y_ref, dy_ref, l_ref, p_ref, dq_ref, dk_ref, dv_ref, cq_ref, ck_ref, r_ref,
             dk_acc, dv_acc, send_sems, recv_sems, local_sem) = refs
        else:
            q_ref, k_ref, v_ref, y_ref, dy_ref, l_ref, dq_ref, dk_ref, dv_ref, cq_ref, ck_ref, dk_acc, dv_acc = refs
        qi = pl.program_id(2)
        if comm:
            _ride((pl.program_id(0) * nh + pl.program_id(1)) * nq + qi, B * nh * nq,
                  _exchange_phases(p_ref, r_ref, send_sems, recv_sems, local_sem))

        @pl.when(qi == 0)
        def _():
            dk_acc[...] = jnp.zeros_like(dk_acc)
            dv_acc[...] = jnp.zeros_like(dv_acc)

        lane = lax.broadcasted_iota(jnp.int32, (tq, AUG), 1)
        head = lane < HEAD_DIM
        qbs, dobs = [], []
        for hh in range(HP):
            sel = (lambda t: t) if hh == 0 else _other_half
            qf = q_ref[0, hh].astype(F32)
            dov = jnp.where(head, sel(dy_ref[0]), 0.0)
            dsum = jnp.sum(dov * sel(y_ref[0]), axis=1, keepdims=True)
            dobs.append(_put3(dov, lane, AUG_A, -dsum).astype(BF16))
            c_t = jnp.sum(jnp.where((lane >= AUG_A) & (lane < AUG_A + 3), qf, 0.0), axis=1, keepdims=True)
            qbs.append(_put3(qf, lane, AUG_A, c_t - l_ref[0, hh]).astype(BF16))
        causal = _lower_tri(tq)

        def step(j, dqs, masked):
            rows = pl.ds(pl.multiple_of(j * tq, tq), tq)
            out = []
            for hh in range(HP):
                kj = k_ref[0, hh, rows, :]
                s = _dot_nt(qbs[hh], kj)
                if masked:
                    s = jnp.where(causal, s, NEG)
                p = jnp.exp(s)
                ds = (p * _dot_nt(dobs[hh], v_ref[0, hh, rows, :])).astype(BF16)
                dv_acc[hh, rows, :] += _dot_tn(p.astype(BF16), dobs[hh])
                dk_acc[hh, rows, :] += _dot_tn(ds, qbs[hh])
                out.append(dqs[hh] + _dot(ds, kj))
            return tuple(out)

        dqs = lax.fori_loop(0, qi, lambda j, c: step(j, c, False), tuple(jnp.zeros((tq, AUG), F32) for _ in range(HP)))
        dqs = step(qi, dqs, True)
        for hh in range(HP):
            cq_ref[0, hh] = _lane_col(dqs[hh], AUG_A)
        dq_ref[0] = (jnp.where(head, dqs[0], _other_half(dqs[1])) * scale).astype(BF16)

        @pl.when(qi == nq - 1)
        def _():
            full = lax.broadcasted_iota(jnp.int32, (L, AUG), 1) < HEAD_DIM
            dk_ref[0] = jnp.where(full, dk_acc[0], _other_half(dk_acc[1])).astype(BF16)
            dv_ref[0] = jnp.where(full, dv_acc[0], _other_half(dv_acc[1])).astype(BF16)
            for hh in range(HP):
                ck_ref[0, hh] = _lane_col(dk_acc[hh], AUG_B)

    qspec = pl.BlockSpec((1, HP, tq, AUG), lambda b, h, i: (b, h, i, 0))
    kvspec = pl.BlockSpec((1, HP, L, AUG), lambda b, h, i: (b, h, 0, 0))
    lspec = pl.BlockSpec((1, HP, tq, 1), lambda b, h, i: (b, h, i, 0))
    lfull = pl.BlockSpec((1, HP, L, 1), lambda b, h, i: (b, h, 0, 0))
    yspec = pl.BlockSpec((1, tq, LANES), lambda b, h, i: (b, i, h))
    yfull = pl.BlockSpec((1, L, LANES), lambda b, h, i: (b, 0, h))
    nat = jax.ShapeDtypeStruct((B, L, H * HEAD_DIM), BF16)
    col = jax.ShapeDtypeStruct((B, H, L, 1), F32)
    out_shape = [nat, nat, nat, col, col]
    if comm:
        out_shape.append(jax.ShapeDtypeStruct(parts.shape, parts.dtype))
    return pl.pallas_call(
        kern, name=name, grid=(B, nh, nq),
        in_specs=[qspec, kvspec, kvspec, yspec, yspec, lspec] + ([ANY] if comm else []),
        out_specs=[yspec, yfull, yfull, lspec, lfull] + ([ANY] if comm else []), out_shape=out_shape,
        scratch_shapes=[pltpu.VMEM((HP, L, AUG), F32), pltpu.VMEM((HP, L, AUG), F32)] + (COMM_SCRATCH if comm else []),
        compiler_params=_cparams(("arbitrary",) * 3 if comm else ("parallel", "parallel", "arbitrary")),
    )(qa, ka, va, y, dy, lse, *([parts] if comm else []))


PAD = SUBLANES


def _halo_tile(x_ref, i, TR):
    r0 = pl.multiple_of(i * TR, TR)
    before = x_ref[0, pl.ds(pl.multiple_of(jnp.maximum(r0 - PAD, 0), PAD), PAD), :]
    return jnp.concatenate([jnp.where(i > 0, before, 0.0), x_ref[0, pl.ds(r0, TR), :]], axis=0)


def _conv_fwd(x3, x_blk, w, b, n_silu, name):
    B, L, _ = x3.shape
    C = w.shape[1]
    TR = _tile(L, 384, 8)

    def kern(x_ref, w_ref, b_ref, o_ref):
        cb = pl.program_id(1)

        def body(i, carry):
            r0 = pl.multiple_of(i * TR, TR)
            ext = _halo_tile(x_ref, i, TR)
            acc = jnp.zeros((TR, LANES), F32) + b_ref[...]
            for k in range(CONV_K):
                s = CONV_K - 1 - k
                sh = ext if s == 0 else pltpu.roll(ext, s, 0)
                acc = acc + w_ref[k:k + 1, :] * sh[PAD:PAD + TR]
            o_ref[0, pl.ds(r0, TR), :] = jnp.where(cb < n_silu, _silu(acc), acc)
            return carry

        lax.fori_loop(0, L // TR, body, 0)

    return pl.pallas_call(
        kern, name=name, grid=(B, C // LANES),
        in_specs=[pl.BlockSpec((1, L, LANES), lambda b_, c: (b_, 0, x_blk + c)),
                  pl.BlockSpec((CONV_K, LANES), lambda b_, c: (0, c)), pl.BlockSpec((1, LANES), lambda b_, c: (0, c))],
        out_specs=pl.BlockSpec((1, L, LANES), lambda b_, c: (b_, 0, c)),
        out_shape=jax.ShapeDtypeStruct((B, L, C), F32),
        compiler_params=_cparams(("parallel", "parallel")),
    )(x3, w, b)


def _conv_bwd_pre(x3, x_blk, du, w, b, n_silu, name):
    B, L, C = du.shape
    Lp = L + PAD
    TR = _tile(L, 384, 8)

    def kern(x_ref, du_ref, w_ref, b_ref, dp_ref, dw_ref):
        cb = pl.program_id(0)

        @pl.when(pl.program_id(1) == 0)
        def _():
            dw_ref[...] = jnp.zeros_like(dw_ref)

        def body(i, carry):
            r0 = pl.multiple_of(i * TR, TR)
            ext = _halo_tile(x_ref, i, TR)
            taps = []
            acc = jnp.zeros((TR, LANES), F32) + b_ref[...]
            for k in range(CONV_K):
                s = CONV_K - 1 - k
                sh = ext if s == 0 else pltpu.roll(ext, s, 0)
                taps.append(sh[PAD:PAD + TR])
                acc = acc + w_ref[k:k + 1, :] * taps[-1]
            dv = du_ref[0, pl.ds(r0, TR), :]
            dpre = jnp.where(cb < n_silu, dv * _dsilu(acc), dv)
            dp_ref[0, pl.ds(r0, TR), :] = dpre
            return tuple(c + _colsum(dpre * t) for c, t in zip(carry[:CONV_K], taps)) + (carry[CONV_K] + _colsum(dpre),)

        z = jnp.zeros((1, LANES), F32)
        sums = lax.fori_loop(0, L // TR, body, (z,) * (CONV_K + 1))
        dp_ref[0, pl.ds(L, PAD), :] = jnp.zeros((PAD, LANES), F32)
        for k in range(CONV_K + 1):
            dw_ref[k:k + 1, :] += sums[k]

    return pl.pallas_call(
        kern, name=name, grid=(C // LANES, B),
        in_specs=[pl.BlockSpec((1, L, LANES), lambda c, b_: (b_, 0, x_blk + c)),
                  pl.BlockSpec((1, L, LANES), lambda c, b_: (b_, 0, c)),
                  pl.BlockSpec((CONV_K, LANES), lambda c, b_: (0, c)), pl.BlockSpec((1, LANES), lambda c, b_: (0, c))],
        out_specs=[pl.BlockSpec((1, Lp, LANES), lambda c, b_: (b_, 0, c)),
                   pl.BlockSpec((SUBLANES, LANES), lambda c, b_: (0, c))],
        out_shape=[jax.ShapeDtypeStruct((B, Lp, C), F32), jax.ShapeDtypeStruct((SUBLANES, C), F32)],
        compiler_params=_cparams(("parallel", "arbitrary")),
    )(x3, du, w, b)


def _conv_bwd_in(dpp, w, name):
    B, Lp, C = dpp.shape
    L = Lp - PAD
    TR = _tile(L, 384, 16)

    def kern(d_ref, w_ref, o_ref):
        def body(i, carry):
            r0 = pl.multiple_of(i * TR, TR)
            ext = d_ref[0, pl.ds(r0, TR + PAD), :]
            acc = jnp.zeros((TR, LANES), F32)
            for k in range(CONV_K):
                s = CONV_K - 1 - k
                sh = ext if s == 0 else pltpu.roll(ext, TR + PAD - s, 0)
                acc = acc + w_ref[k:k + 1, :] * sh[0:TR]
            o_ref[0, pl.ds(r0, TR), :] = acc.astype(BF16)
            return carry

        lax.fori_loop(0, L // TR, body, 0)

    return pl.pallas_call(
        kern, name=name, grid=(B, C // LANES),
        in_specs=[pl.BlockSpec((1, Lp, LANES), lambda b_, c: (b_, 0, c)),
                  pl.BlockSpec((CONV_K, LANES), lambda b_, c: (0, c))],
        out_specs=pl.BlockSpec((1, L, LANES), lambda b_, c: (b_, 0, c)),
        out_shape=jax.ShapeDtypeStruct((B, L, C), BF16),
        compiler_params=_cparams(("parallel", "parallel")),
    )(dpp, w)


def _dot_nt(a, b):
    return lax.dot_general(a, b, (((1,), (1,)), ((), ())), preferred_element_type=F32)


def _dot_tn(a, b):
    return lax.dot_general(a, b, (((0,), (0,)), ((), ())), preferred_element_type=F32)


def _dot(a, b):
    return jnp.dot(a, b, preferred_element_type=F32)


def _ssd_specs(L, nc, b_blk, c_blk):
    E = HEADS // SSD_GROUPS
    return [
        pl.BlockSpec((1, 1, L, HEAD_DIM), lambda b, h: (b, h, 0, 0)),
        pl.BlockSpec((1, L, SSD_STATE), lambda b, h: (b, 0, b_blk + h // E)),
        pl.BlockSpec((1, L, SSD_STATE), lambda b, h: (b, 0, c_blk + h // E)),
        pl.BlockSpec((1, L, LANES), lambda b, h: (b, 0, 0)),
        pl.BlockSpec((1, L, LANES), lambda b, h: (b, 0, 0)),
        pl.BlockSpec((1, 1, nc, Q_BLOCK), lambda b, h: (b, HEADS + h, 0, 0)),
        pl.BlockSpec((1, LANES), lambda b, h: (0, 0)),
    ]


def _ssd_chunk(c, S, x_ref, b_ref, c_ref, v_ref, cu_ref, ct_ref, lane):
    Q = Q_BLOCK
    rows = pl.ds(pl.multiple_of(c * Q, Q), Q)
    xc = x_ref[0, 0, rows, :]
    Bb = b_ref[0, rows, :].astype(BF16)
    Cb = c_ref[0, rows, :].astype(BF16)
    dt = _lane_col(v_ref[0, rows, :], lane)
    A = _lane_col(cu_ref[0, rows, :], lane)
    Ar = ct_ref[0, 0, pl.ds(c, 1), :]
    Aend = _lane_col(Ar, Q - 1)
    xdt = xc * dt
    Lm = jnp.exp(jnp.where(_lower_tri(Q), A - Ar, NEG))
    CB = _dot_nt(Cb, Bb)
    e_end = jnp.exp(Aend - A)
    W = xdt * e_end
    S_new = S * jnp.exp(Aend) + _dot_tn(W.astype(BF16), Bb)
    return dict(rows=rows, xc=xc, Bb=Bb, Cb=Cb, dt=dt, A=A, Aend=Aend, xdt=xdt, Lm=Lm, CB=CB, e_end=e_end, W=W,
                S_new=S_new)


def _ssd_fwd(x4, u, b_blk, c_blk, vals, cums, cums_t, dvec, name):
    B, H, L, P = x4.shape
    nc = L // Q_BLOCK

    def kern(x_ref, b_ref, c_ref, v_ref, cu_ref, ct_ref, d_ref, y_ref):
        lane = HEADS + pl.program_id(1)
        dskip = _lane_col(d_ref[...], lane)

        def body(c, S):
            q = _ssd_chunk(c, S, x_ref, b_ref, c_ref, v_ref, cu_ref, ct_ref, lane)
            yd = _dot((q["CB"] * q["Lm"]).astype(BF16), q["xdt"].astype(BF16))
            z = _dot_nt(q["Cb"], S.astype(BF16))
            y_ref[0, 0, q["rows"], :] = yd + z * jnp.exp(q["A"]) + dskip * q["xc"]
            return q["S_new"]

        lax.fori_loop(0, nc, body, jnp.zeros((P, SSD_STATE), F32))

    return pl.pallas_call(
        kern, name=name, grid=(B, H), in_specs=_ssd_specs(L, nc, b_blk, c_blk),
        out_specs=pl.BlockSpec((1, 1, L, P), lambda b, h: (b, h, 0, 0)),
        out_shape=jax.ShapeDtypeStruct((B, H, L, P), F32),
        compiler_params=_cparams(("parallel", "arbitrary")),
    )(x4, u, u, vals, cums, cums_t, dvec)


def _ssd_bwd(x4, u, b_blk, c_blk, vals, cums, cums_t, dvec, dy4, name):
    B, H, L, P = x4.shape
    Q = Q_BLOCK
    nc = L // Q
    N = SSD_STATE
    E = HEADS // SSD_GROUPS

    def kern(x_ref, b_ref, c_ref, v_ref, cu_ref, ct_ref, d_ref, dy_ref,
             dx_ref, dB_ref, dC_ref, ddt_ref, dAc_ref, dAr_ref, dD_ref, s_all):
        b = pl.program_id(0)
        h = pl.program_id(1)
        lane = HEADS + h
        dskip = _lane_col(d_ref[...], lane)
        onehot = (lax.broadcasted_iota(jnp.int32, (1, LANES), 1) == lane).astype(F32)

        @pl.when(h % E == 0)
        def _():
            dB_ref[...] = jnp.zeros_like(dB_ref)
            dC_ref[...] = jnp.zeros_like(dC_ref)

        @pl.when(h == 0)
        def _():
            ddt_ref[...] = jnp.zeros_like(ddt_ref)
            dAc_ref[...] = jnp.zeros_like(dAc_ref)

        @pl.when((b == 0) & (h == 0))
        def _():
            dD_ref[...] = jnp.zeros_like(dD_ref)

        def fwd(c, S):
            s_all[c] = S
            return _ssd_chunk(c, S, x_ref, b_ref, c_ref, v_ref, cu_ref, ct_ref, lane)["S_new"]

        lax.fori_loop(0, nc, fwd, jnp.zeros((P, N), F32))
        last_row = lax.broadcasted_iota(jnp.int32, (Q, 1), 0) == Q - 1

        def bwd(i, carry):
            dS, dD = carry
            c = nc - 1 - i
            S = s_all[c]
            q = _ssd_chunk(c, S, x_ref, b_ref, c_ref, v_ref, cu_ref, ct_ref, lane)
            rows, xc, Bb, Cb, xdt, Lm, CB = q["rows"], q["xc"], q["Bb"], q["Cb"], q["xdt"], q["Lm"], q["CB"]
            eA = jnp.exp(q["A"])
            eAend = jnp.exp(q["Aend"])
            dy = dy_ref[0, 0, rows, :]
            dyb = dy.astype(BF16)
            Sb = S.astype(BF16)
            dD = dD + jnp.sum(jnp.sum(dy * xc, axis=1, keepdims=True), axis=0, keepdims=True)
            dM = _dot_nt(dyb, xdt.astype(BF16))
            dxdt = _dot_tn((CB * Lm).astype(BF16), dyb)
            dCBb = (dM * Lm).astype(BF16)
            G = dM * CB * Lm
            dAc = jnp.sum(G, axis=1, keepdims=True)
            dAr = -jnp.sum(G, axis=0, keepdims=True)
            dC = _dot(dCBb, Bb)
            dBm = _dot_tn(dCBb, Cb)
            z = _dot_nt(Cb, Sb)
            dAc = dAc + jnp.sum(dy * z, axis=1, keepdims=True) * eA
            dzb = (dy * eA).astype(BF16)
            dC = dC + _dot(dzb, Sb)
            dS_in = _dot_tn(dzb, Cb)
            dSb = dS.astype(BF16)
            dW = _dot_nt(Bb, dSb)
            dBm = dBm + _dot(q["W"].astype(BF16), dSb)
            dxdt = dxdt + dW * q["e_end"]
            de = jnp.sum(dW * xdt, axis=1, keepdims=True) * q["e_end"]
            dAend = (jnp.sum(jnp.sum(dS * S, axis=1, keepdims=True), axis=0, keepdims=True) * eAend
                     + jnp.sum(de, axis=0, keepdims=True))
            dAc = dAc - de + jnp.where(last_row, dAend, 0.0)
            dx_ref[0, 0, rows, :] = dskip * dy + dxdt * q["dt"]
            dB_ref[0, 0, rows, :] += dBm
            dC_ref[0, 0, rows, :] += dC
            ddt_ref[0, rows, :] += jnp.sum(dxdt * xc, axis=1, keepdims=True) * onehot
            dAc_ref[0, rows, :] += dAc * onehot
            dAr_ref[0, 0, pl.ds(c, 1), :] = dAr
            return dS * eAend + dS_in, dD

        _, dD = lax.fori_loop(0, nc, bwd, (jnp.zeros((P, N), F32), jnp.zeros((1, 1), F32)))
        dD_ref[...] += dD * onehot

    tm = pl.BlockSpec((1, L, LANES), lambda b, h: (b, 0, 0))
    grp = pl.BlockSpec((1, 1, L, N), lambda b, h: (b, h // E, 0, 0))
    xs = pl.BlockSpec((1, 1, L, P), lambda b, h: (b, h, 0, 0))
    return pl.pallas_call(
        kern, name=name, grid=(B, H), in_specs=_ssd_specs(L, nc, b_blk, c_blk) + [xs],
        out_specs=[xs, grp, grp, tm, tm, pl.BlockSpec((1, 1, nc, Q), lambda b, h: (b, h, 0, 0)),
                   pl.BlockSpec((1, LANES), lambda b, h: (0, 0))],
        out_shape=[jax.ShapeDtypeStruct((B, H, L, P), F32), jax.ShapeDtypeStruct((B, SSD_GROUPS, L, N), F32),
                   jax.ShapeDtypeStruct((B, SSD_GROUPS, L, N), F32), jax.ShapeDtypeStruct((B, L, LANES), F32),
                   jax.ShapeDtypeStruct((B, L, LANES), F32), jax.ShapeDtypeStruct((B, H, nc, Q), F32),
                   jax.ShapeDtypeStruct((1, LANES), F32)],
        scratch_shapes=[pltpu.VMEM((nc, P, N), F32)],
        compiler_params=_cparams(("arbitrary", "arbitrary")),
    )(x4, u, u, vals, cums, cums_t, dvec, dy4)


LRU_TR = 384
LRU_CB = 512


def _lru_gates(xc, ra, ix, p_ref, first):
    r = _sigmoid(ra + p_ref[0:1, :])
    i = _sigmoid(ix + p_ref[1:2, :])
    ls = _log_sigmoid(p_ref[2:3, :])
    log_a = LRU_C * r * ls
    a = jnp.exp(log_a)
    mult0 = jnp.sqrt(_one_minus_exp(2.0 * log_a))
    mult = jnp.where(first, 1.0, mult0)
    return r, i, ls, a, mult0, mult


def _lru_fwd(u, xc_off, ra, ix, proj3, gate_off, pvec, name):
    B, L, D = ra.shape
    TR, CB = _tile(L, LRU_TR, 8), LRU_CB
    nrt = L // TR

    def kern(xc_ref, ra_ref, ix_ref, g_ref, p_ref, y_ref, hs_ref, a_ref, pa_s, pu_s, carry):
        rt = pl.program_id(2)

        @pl.when(rt == 0)
        def _():
            carry[...] = jnp.zeros_like(carry)

        row = lax.broadcasted_iota(jnp.int32, (TR, 1), 0)
        first = (rt == 0) & (row == 0)
        xc = xc_ref[0]
        r, i, ls, a, mult0, mult = _lru_gates(xc, ra_ref[0], ix_ref[0], p_ref, first)
        a_ref[0] = a
        pa, pu = a, mult * (i * xc)
        sub = row % SUBLANES
        for s in (1, 2, 4):
            ok = sub >= s
            pu = jnp.where(ok, pa * pltpu.roll(pu, s, 0) + pu, pu)
            pa = jnp.where(ok, pa * pltpu.roll(pa, s, 0), pa)
        pa_s[...] = pa
        pu_s[...] = pu
        row8 = lax.broadcasted_iota(jnp.int32, (SUBLANES, 1), 0)

        def gbody(g, c):
            r8 = pl.ds(pl.multiple_of(g * SUBLANES, SUBLANES), SUBLANES)
            hg = pa_s[r8, :] * c + pu_s[r8, :]
            hs_ref[0, r8, :] = hg
            return jnp.sum(jnp.where(row8 == SUBLANES - 1, hg, 0.0), axis=0, keepdims=True)

        carry[...] = lax.fori_loop(0, TR // SUBLANES, gbody, carry[...])
        y_ref[0] = (hs_ref[0] * _gelu(g_ref[0])).astype(BF16)

    def win(off):
        assert off % CB == 0
        return pl.BlockSpec((1, TR, CB), functools.partial(lambda b, j, t, o: (b, t, j + o), o=off // CB))

    return pl.pallas_call(
        kern, name=name, grid=(B, D // CB, nrt),
        in_specs=[win(xc_off), win(0), win(0), win(gate_off), pl.BlockSpec((SUBLANES, CB), lambda b, j, t: (0, j))],
        out_specs=[win(0)] * 3,
        out_shape=[jax.ShapeDtypeStruct((B, L, D), BF16), jax.ShapeDtypeStruct((B, L, D), F32),
                   jax.ShapeDtypeStruct((B, L, D), F32)],
        scratch_shapes=[pltpu.VMEM((TR, CB), F32), pltpu.VMEM((TR, CB), F32), pltpu.VMEM((1, CB), F32)],
        compiler_params=_cparams(("parallel", "parallel", "arbitrary")),
    )(u, ra, ix, proj3, pvec)


def _lru_bwd(dy, proj3, gate_off, hs, a, u, xc_off, ra, ix, pvec, name):
    B, L, D = ra.shape
    TR, CB = _tile(L, LRU_TR, 8), LRU_CB
    nrt = L // TR

    def kern(dy_ref, g_ref, hs_ref, hsp_ref, a_ref, an_ref, xc_ref, ra_ref, ix_ref, p_ref,
             dg_ref, dra_ref, dix_ref, dxc_ref, dp_ref, pb_s, pd_s, g_s, carry):
        b = pl.program_id(1)
        rt = pl.program_id(2)
        t = nrt - 1 - rt

        @pl.when((b == 0) & (rt == 0))
        def _():
            dp_ref[...] = jnp.zeros_like(dp_ref)

        @pl.when(rt == 0)
        def _():
            carry[...] = jnp.zeros_like(carry)

        row = lax.broadcasted_iota(jnp.int32, (TR, 1), 0)
        gate, hsv, av, dyv = g_ref[0], hs_ref[0], a_ref[0], dy_ref[0]
        dg_ref[0] = (dyv * hsv * _dgelu(gate)).astype(BF16)
        a_next = jnp.where(t == nrt - 1, 0.0, an_ref[0, 0:1, :])
        pb = jnp.where(row == TR - 1, a_next, pltpu.roll(av, TR - 1, 0))
        pd = dyv * _gelu(gate)
        sub = row % SUBLANES
        for s in (1, 2, 4):
            ok = sub < SUBLANES - s
            pd = jnp.where(ok, pd + pb * pltpu.roll(pd, TR - s, 0), pd)
            pb = jnp.where(ok, pb * pltpu.roll(pb, TR - s, 0), pb)
        pb_s[...] = pb
        pd_s[...] = pd
        row8 = lax.broadcasted_iota(jnp.int32, (SUBLANES, 1), 0)

        def gbody(i, c):
            r8 = pl.ds(pl.multiple_of((TR // SUBLANES - 1 - i) * SUBLANES, SUBLANES), SUBLANES)
            gg = pd_s[r8, :] + pb_s[r8, :] * c
            g_s[r8, :] = gg
            return jnp.sum(jnp.where(row8 == 0, gg, 0.0), axis=0, keepdims=True)

        carry[...] = lax.fori_loop(0, TR // SUBLANES, gbody, carry[...])
        gv = g_s[...]
        h_first = jnp.where(t == 0, 0.0, hsp_ref[0, TR - 1:TR, :])
        hprev = jnp.where(row == 0, h_first, pltpu.roll(hsv, 1, 0))
        first = (t == 0) & (row == 0)
        xc = xc_ref[0]
        r, i, ls, a2, mult0, mult = _lru_gates(xc, ra_ref[0], ix_ref[0], p_ref, first)
        dxc_ref[0] = gv * mult * i
        dlog_a = gv * hprev * av + jnp.where(first, 0.0, gv * i * xc * (-(av * av) / mult0))
        dra = dlog_a * LRU_C * ls * r * (1.0 - r)
        dix = gv * mult * xc * i * (1.0 - i)
        dra_ref[0] = dra.astype(BF16)
        dix_ref[0] = dix.astype(BF16)
        dp_ref[0:1, :] += _colsum(dra)
        dp_ref[1:2, :] += _colsum(dix)
        dp_ref[2:3, :] += _colsum(dlog_a * LRU_C * r) * _sigmoid(-p_ref[2:3, :])

    def win(off, shift=0):
        assert off % CB == 0
        o = off // CB
        return pl.BlockSpec((1, TR, CB), lambda j, b, rt: (b, jnp.clip(nrt - 1 - rt + shift, 0, nrt - 1), j + o))

    return pl.pallas_call(
        kern, name=name, grid=(D // CB, B, nrt),
        in_specs=[win(0), win(gate_off), win(0), win(0, -1), win(0), win(0, 1), win(xc_off), win(0), win(0),
                  pl.BlockSpec((SUBLANES, CB), lambda j, b, rt: (0, j))],
        out_specs=[win(0)] * 4 + [pl.BlockSpec((SUBLANES, CB), lambda j, b, rt: (0, j))],
        out_shape=[jax.ShapeDtypeStruct((B, L, D), BF16)] * 3 + [jax.ShapeDtypeStruct((B, L, D), F32),
                                                                 jax.ShapeDtypeStruct((SUBLANES, D), F32)],
        scratch_shapes=[pltpu.VMEM((TR, CB), F32)] * 3 + [pltpu.VMEM((1, CB), F32)],
        compiler_params=_cparams(("parallel", "arbitrary", "arbitrary")),
    )(dy, proj3, hs, hs, a, a, u, ra, ix, pvec)


def _sum8(parts, name):
    _, R, C = parts.shape
    tr = _tile(R, 1024, ROW_ALIGN if parts.dtype.itemsize == 2 else SUBLANES)

    def kern(p_ref, o_ref):
        acc = p_ref[0].astype(F32)
        for d in range(1, N_DEV):
            acc = acc + p_ref[d].astype(F32)
        o_ref[...] = acc

    return pl.pallas_call(
        kern, name=name, grid=(R // tr,), in_specs=[pl.BlockSpec((N_DEV, tr, C), lambda i: (0, i, 0))],
        out_specs=pl.BlockSpec((tr, C), lambda i: (i, 0)), out_shape=jax.ShapeDtypeStruct((R, C), F32),
        compiler_params=_cparams(("parallel",)),
    )(parts)


def _adamw(w, g, m, v, name):
    shape = w.shape
    C = shape[-1] if w.ndim > 1 else shape[0]
    R = w.size // C
    w2, g2, m2, v2 = (t.reshape(R, C) for t in (w, g, m, v))
    tr = R
    for cand in range(8, min(R, 512) + 1, 8):
        if R % cand == 0:
            tr = cand

    def kern(w_ref, g_ref, m_ref, v_ref, d_ref, nm_ref, nv_ref):
        gv = g_ref[...]
        nm = ADAM_B1 * m_ref[...] + (1.0 - ADAM_B1) * gv
        nv = ADAM_B2 * v_ref[...] + (1.0 - ADAM_B2) * (gv * gv)
        m_hat = nm / (1.0 - ADAM_B1 ** ADAM_STEP)
        v_hat = nv / (1.0 - ADAM_B2 ** ADAM_STEP)
        d_ref[...] = -ADAM_LR * (m_hat / (jnp.sqrt(v_hat) + ADAM_EPS) + ADAM_WD * w_ref[...])
        nm_ref[...] = nm
        nv_ref[...] = nv

    spec = pl.BlockSpec((tr, C), lambda i: (i, 0))
    outs = pl.pallas_call(
        kern, name=name, grid=(R // tr,), in_specs=[spec] * 4, out_specs=[spec] * 3,
        out_shape=[jax.ShapeDtypeStruct((R, C), F32)] * 3, compiler_params=_cparams(("parallel",)),
    )(w2, g2, m2, v2)
    return tuple(o.reshape(shape) for o in outs)


MESH_ID = pl.DeviceIdType.MESH
ANY = pl.BlockSpec(memory_space=pl.ANY)
N_COPIES = N_DEV - 1
COMM_SCRATCH = [pltpu.SemaphoreType.DMA((N_COPIES,)), pltpu.SemaphoreType.DMA((N_COPIES,)), pltpu.SemaphoreType.DMA]


def _my_place():
    return lax.axis_index("x"), lax.axis_index("y"), lax.axis_index("c")


def _gather_phases(x_ref, out_ref, send_sems, recv_sems, local_sem):
    x, y, c = _my_place()
    me, sibling = (x, y, c), (x, y, 1 - c)
    chips = [(1 - x, y), (x, 1 - y), (1 - x, 1 - y)]

    def slab(px, py, pc):
        return out_ref.at[4 * px + 2 * py + pc]

    def copy(k, block, to, src=None):
        return pltpu.make_async_remote_copy(
            src_ref=slab(*block) if src is None else src, dst_ref=slab(*block),
            send_sem=send_sems.at[k], recv_sem=recv_sems.at[k], device_id=to, device_id_type=MESH_ID)

    mine = pltpu.make_async_copy(x_ref, slab(*me), local_sem)
    first = [copy(0, me, sibling, src=x_ref)] + [copy(1 + j, me, (*chip, c), src=x_ref) for j, chip in enumerate(chips)]
    passed = [copy(4 + j, (*chip, c), sibling) for j, chip in enumerate(chips)]

    def start():
        mine.start()
        for cp in first:
            cp.start()

    def forward():
        for j, chip in enumerate(chips):
            copy(1 + j, (*chip, c), me).wait_recv()
            passed[j].start()

    def finish():
        copy(0, sibling, me).wait_recv()
        for j, chip in enumerate(chips):
            copy(4 + j, (*chip, 1 - c), me).wait_recv()
        for cp in first + passed:
            cp.wait_send()
        mine.wait()

    return start, forward, finish


def _exchange_phases(p_ref, out_ref, send_sems, recv_sems, local_sem):
    x, y, c = _my_place()
    my_idx = 4 * x + 2 * y + c
    mine = pltpu.make_async_copy(p_ref.at[my_idx], out_ref.at[my_idx], local_sem)
    copies = []
    for k in range(1, N_DEV):
        px, py, pc = x ^ (k >> 2), y ^ ((k >> 1) & 1), c ^ (k & 1)
        copies.append(pltpu.make_async_remote_copy(
            src_ref=p_ref.at[4 * px + 2 * py + pc], dst_ref=out_ref.at[my_idx],
            send_sem=send_sems.at[k - 1], recv_sem=recv_sems.at[k - 1], device_id=(px, py, pc),
            device_id_type=MESH_ID))

    def start():
        mine.start()
        for cp in copies:
            cp.start()

    def finish():
        for cp in copies:
            cp.wait()
        mine.wait()

    return start, finish


def _ride(lin, total, phases):
    assert total >= 3
    marks = [0, total - 1] if len(phases) == 2 else [0, total // 2, total - 1]
    for mark, phase in zip(marks, phases):
        pl.when(lin == mark)(phase)


def _all_gather(xs, name):
    R, C = xs.shape

    def body(x_ref, out_ref, send_sems, recv_sems, local_sem):
        for phase in _gather_phases(x_ref, out_ref, send_sems, recv_sems, local_sem):
            phase()

    return pl.pallas_call(
        body, name=name, out_shape=jax.ShapeDtypeStruct((N_DEV, R, C), xs.dtype), in_specs=[ANY], out_specs=ANY,
        scratch_shapes=COMM_SCRATCH,
    )(xs)


def _exchange(parts, name):
    def body(p_ref, out_ref, send_sems, recv_sems, local_sem):
        for phase in _exchange_phases(p_ref, out_ref, send_sems, recv_sems, local_sem):
            phase()

    return pl.pallas_call(
        body, name=name, out_shape=jax.ShapeDtypeStruct(parts.shape, parts.dtype), in_specs=[ANY], out_specs=ANY,
        scratch_shapes=COMM_SCRATCH,
    )(parts)


D_XBC_EXTRA = 2 * SSD_GROUPS * SSD_STATE
SMALL_W = LANES
ROW_ALIGN = 16


def _layout(D):
    d_xbc = D + D_XBC_EXTRA
    off = dict(qkv=0, z=3 * D, merge=4 * D, gate=7 * D, conv=8 * D, xr=8 * D + d_xbc, small=9 * D + d_xbc)
    off["n_all"] = off["small"] + SMALL_W
    off["d_xbc"] = d_xbc
    off["conv_c"] = d_xbc + D
    return off


def _w_in_map(D):
    lo = _layout(D)
    widths = [("q", D, 0), ("k", D, D), ("v", D, 2 * D), ("f", HEADS, lo["small"]), ("z", D, lo["z"]),
              ("xbc", lo["d_xbc"], lo["conv"]), ("dt", HEADS, lo["small"] + HEADS), ("xr", D, lo["xr"]),
              ("gate", D, lo["gate"]), ("merge", 3 * D, lo["merge"])]
    out, o = [], 0
    for _, w, mine in widths:
        out.append((o, w, mine))
        o += w
    return out


def _padded(c):
    return -(-c // ROW_ALIGN) * ROW_ALIGN


def _permute_rows(src, pieces, name):
    R, C = src.shape
    n_out = sum(n for _, n in pieces)

    def kern(x_ref, o_ref):
        o = 0
        for start, n in pieces:
            if start is None:
                o_ref[o:o + n, :] = jnp.zeros((n, LANES), src.dtype)
            else:
                o_ref[o:o + n, :] = x_ref[start:start + n, :]
            o += n

    return pl.pallas_call(
        kern, name=name, grid=(C // LANES,), in_specs=[pl.BlockSpec((R, LANES), lambda i: (0, i))],
        out_specs=pl.BlockSpec((n_out, LANES), lambda i: (0, i)), out_shape=jax.ShapeDtypeStruct((n_out, C), src.dtype),
        compiler_params=_cparams(("parallel",)),
    )(src)


def _reorder_rows(wt, D, c, name="reorder_w_in"):
    cp = _padded(c)
    lo = _layout(D)
    pieces = []
    for a, w, mine in sorted(_w_in_map(D), key=lambda t: t[2]):
        b = a + w
        while a < b:
            j = a // c
            e = min(b, (j + 1) * c)
            pieces.append((j * cp + a - j * c, e - a))
            a = e
    pieces.append((None, lo["n_all"] - lo["small"] - 2 * HEADS))
    return _permute_rows(wt, pieces, name)


def _restore_rows(dwt, D, c, name="restore_w_in"):
    cp = _padded(c)
    segs = _w_in_map(D)
    pieces = []
    for j in range(N_DEV):
        a, b = j * c, (j + 1) * c
        for s0, w, mine in segs:
            lo_, hi_ = max(a, s0), min(b, s0 + w)
            if lo_ < hi_:
                pieces.append((mine + lo_ - s0, hi_ - lo_))
        if cp > c:
            pieces.append((None, cp - c))
    return _permute_rows(dwt, pieces, name)


def _block_diag(w):
    H, n, _ = w.shape
    tiled = jnp.tile(w.reshape(H * n, n), (1, H))
    r = lax.broadcasted_iota(jnp.int32, (H * n, H * n), 0) // n
    c = lax.broadcasted_iota(jnp.int32, (H * n, H * n), 1) // n
    return jnp.where(r == c, tiled, jnp.zeros_like(tiled))


def _diag_blocks(m, H):
    n = m.shape[0] // H
    return jnp.stack([m[h * n:(h + 1) * n, h * n:(h + 1) * n] for h in range(H)])


def _to_heads(t, B, L):
    return t.reshape(B, L, HEADS, HEAD_DIM).transpose(0, 2, 1, 3)


def _from_heads(t4):
    B, H, L, P = t4.shape
    return t4.transpose(0, 2, 1, 3).reshape(B * L, H * P)


def _rows_to_tm(rows):
    B, H, nc, Q = rows.shape
    return rows.reshape(B, H, nc * Q).transpose(0, 2, 1)


def _ffn_fwd(h, g, wgu_t, wd, tag):
    n = _norm_fwd(h, g, tag + "_norm")
    gu = _mm(n, wgu_t, tb=True, name=tag + "_up")
    act = _swiglu_fwd(gu, tag + "_act")
    out = _mm(act, wd, res=h, scale=0.5, name=tag + "_down")
    return out, (h, n, gu, act)


def _ffn_bwd(dh, saved, g, wgu_t, wd, tag):
    h, n, gu, act = saved
    dact = _mm(dh, wd, tb=True, scale=0.5, name=tag + "_down_dx")
    dwd = _mm(act, dh, ta=True, scale=0.5, name=tag + "_down_dw")
    dgu = _swiglu_bwd(gu, dact, tag + "_act_bwd")
    dwgu_t = _mm(dgu, n, ta=True, tn=1024, name=tag + "_up_dw")
    dn = _mm(dgu, wgu_t, name=tag + "_up_dx")
    dh_in, dg = _norm_bwd(h, dn, dh, g, tag + "_norm_bwd")
    return dh_in, dict(norm=dg, gu=dwgu_t, down=dwd)


def _mixer_fwd(h, p, B, L, gather=None):
    T, D = h.shape
    lo = _layout(D)
    n = _norm_fwd(h, p["gm"], "mix_norm")
    proj = _mm(n, p["w_all_t"], tb=True, name="mix_in")
    proj3 = proj.reshape(B, L, lo["n_all"])
    vals, cums = _gate_prep(proj3, lo["small"] // LANES, p["small_bias"], p["avec"], "gate_prep")
    cums_t = cums[..., :2 * HEADS].transpose(0, 2, 1).reshape(B, 2 * HEADS, L // Q_BLOCK, Q_BLOCK)
    qa, ka, va = _attn_pack(proj3, cums, "attn_pack")
    y_a3, lse, *gathered = _attn_fwd(qa, ka, va, "attn_fwd", gather)
    y_a = y_a3.reshape(T, D)
    u = _conv_fwd(proj3, lo["conv"] // LANES, p["conv_w"], p["conv_b"], lo["d_xbc"] // LANES, "conv_fwd")
    x4 = _to_heads(u[..., :D], B, L)
    b_blk = D // LANES
    c_blk = b_blk + SSD_GROUPS * SSD_STATE // LANES
    y4 = _ssd_fwd(x4, u, b_blk, c_blk, vals, cums, cums_t, p["dvec"], "ssd_fwd")
    y_s = _from_heads(y4)
    yb = _gnorm_fwd(y_s, proj, lo["z"], p["ssd_norm"], "gnorm_fwd")
    u2 = u.reshape(T, lo["conv_c"])
    ra = _mm(u2, p["wa"], a_off=(0, lo["d_xbc"]), dims=(T, D, D), tk=512, name="lru_ra")
    ix = _mm(u2, p["wx"], a_off=(0, lo["d_xbc"]), dims=(T, D, D), tk=512, name="lru_ix")
    yc, hs, a = _lru_fwd(u, lo["d_xbc"], ra.reshape(B, L, D), ix.reshape(B, L, D), proj3, lo["gate"], p["pvec"],
                         "lru_fwd")
    yc = yc.reshape(T, D)
    pa = _mm(y_a, p["wba"], name="branch_attn")
    pb = _mm(yb, p["wbs"], name="branch_ssd")
    pc = _mm(yc, p["wbl"], name="branch_lru")
    mixed = _merge_fwd(proj, lo["merge"], pa, pb, pc, "merge_fwd")
    out = _mm(mixed, p["wout"], res=h, name="mix_out")
    saved = dict(h=h, n=n, proj=proj, qa=qa, ka=ka, va=va, vals=vals, cums=cums, cums_t=cums_t, lse=lse, y_a=y_a,
                 u=u, x4=x4, y_s=y_s, yb=yb, ra=ra, ix=ix, yc=yc, hs=hs, a=a, pa=pa, pb=pb, pc=pc, mixed=mixed)
    return out, saved, (gathered[0] if gathered else None)


def _mixer_bwd(dh, s, p, B, L, parts=None):
    T, D = dh.shape
    lo = _layout(D)
    proj, u = s["proj"], s["u"]
    proj3 = proj.reshape(B, L, lo["n_all"])
    g = {}
    dmixed = _mm(dh, p["wout"], tb=True, name="mix_out_dx")
    g["wout"] = _mm(s["mixed"], dh, ta=True, name="mix_out_dw")
    dpa, dpb, dpc, dmerge = _merge_bwd(dmixed, proj, lo["merge"], s["pa"], s["pb"], s["pc"], "merge_bwd")
    dy_a = _mm(dpa, p["wba"], tb=True, name="branch_attn_dx")
    g["wba"] = _mm(s["y_a"], dpa, ta=True, name="branch_attn_dw")
    dyb = _mm(dpb, p["wbs"], tb=True, name="branch_ssd_dx")
    g["wbs"] = _mm(s["yb"], dpb, ta=True, name="branch_ssd_dw")
    dyc = _mm(dpc, p["wbl"], tb=True, name="branch_lru_dx")
    g["wbl"] = _mm(s["yc"], dpc, ta=True, name="branch_lru_dw")
    dgate, dra, dix, dxc, g["pvec"] = _lru_bwd(dyc.reshape(B, L, D), proj3, lo["gate"], s["hs"], s["a"], u, lo["d_xbc"],
                                               s["ra"].reshape(B, L, D), s["ix"].reshape(B, L, D), p["pvec"], "lru_bwd")
    dra, dix = dra.reshape(T, D), dix.reshape(T, D)
    u2 = u.reshape(T, lo["conv_c"])
    g["wa"] = _mm(u2, dra, ta=True, a_off=(0, lo["d_xbc"]), dims=(D, D, T), tm=512, name="lru_ra_dw")
    g["wx"] = _mm(u2, dix, ta=True, a_off=(0, lo["d_xbc"]), dims=(D, D, T), tm=512, name="lru_ix_dw")
    dxc = _mm(dra, p["wa"], tb=True, res=dxc.reshape(T, D), name="lru_ra_dx")
    dxc = _mm(dix, p["wx"], tb=True, res=dxc, name="lru_ix_dx")
    dy_s, dz, g["ssd_norm"] = _gnorm_bwd(dyb, s["y_s"], proj, lo["z"], p["ssd_norm"], "gnorm_bwd")
    b_blk = D // LANES
    c_blk = b_blk + SSD_GROUPS * SSD_STATE // LANES
    dx4, dBg, dCg, ddt_tm, dAc_tm, dAr, g["dvec"] = _ssd_bwd(s["x4"], u, b_blk, c_blk, s["vals"], s["cums"], s["cums_t"],
                                                             p["dvec"], _to_heads(dy_s, B, L), "ssd_bwd")
    grp = lambda t: t.transpose(0, 2, 1, 3).reshape(B, L, SSD_GROUPS * SSD_STATE)
    du = jnp.concatenate([_from_heads(dx4).reshape(B, L, D), grp(dBg), grp(dCg), dxc.reshape(B, L, D)], axis=-1)
    dpp, g["conv_wb"] = _conv_bwd_pre(proj3, lo["conv"] // LANES, du, p["conv_w"], p["conv_b"], lo["d_xbc"] // LANES,
                                      "conv_bwd_pre")
    dconv = _conv_bwd_in(dpp, p["conv_w"], "conv_bwd_in")
    dq3, dk3, dv3, dcq, dck, *recv = _attn_bwd(s["qa"], s["ka"], s["va"], s["y_a"].reshape(B, L, D),
                                               dy_a.reshape(B, L, D), s["lse"], "attn_bwd", parts)
    dc_tm = (dcq - dck)[..., 0].transpose(0, 2, 1)
    drow_tm = jnp.concatenate([dc_tm, _rows_to_tm(dAr), jnp.zeros((B, L, LANES - 2 * HEADS), F32)], axis=-1)
    dsmall, g["small_bias"], g["avec"] = _gate_post(drow_tm, dAc_tm, ddt_tm, proj3, lo["small"] // LANES, s["vals"],
                                                    p["small_bias"], p["avec"], "gate_post")
    dproj = jnp.concatenate([dq3.reshape(T, D), dk3.reshape(T, D), dv3.reshape(T, D), dz, dmerge, dgate.reshape(T, D), dconv.reshape(T, lo["conv_c"]),
                             dsmall.reshape(T, SMALL_W)], axis=1)
    g["w_all_t"] = _mm(dproj, s["n"], ta=True, tn=1024, name="mix_in_dw")
    dn = _mm(dproj, p["w_all_t"], name="mix_in_dx")
    dh_in, g["gm"] = _norm_bwd(s["h"], dn, dh, p["gm"], "mix_norm_bwd")
    return dh_in, g, (recv[0] if recv else None)


def _small_vec(a, b):
    return jnp.concatenate([a, b, jnp.zeros((LANES - 2 * HEADS,), F32)])[None, :]


def _layer_params(w):
    zeros16 = jnp.zeros((HEADS,), F32)
    pvec = jnp.concatenate([w["lru_b_a"][None], w["lru_b_x"][None], w["lru_lambda"][None],
                            jnp.zeros((SUBLANES - 3, w["lru_b_a"].shape[0]), F32)], axis=0)
    return dict(
        g1=w["ffn1_norm"][None], gu1=w["ffn1_w_gate_up"], d1=w["ffn1_w_down"],
        gm=w["mix_norm"][None], w_all_t=w["w_in"],
        small_bias=_small_vec(w["fox_forget_bias"], w["ssd_dt_bias"]),
        avec=_small_vec(zeros16, -jnp.exp(w["ssd_a_log"])), dvec=_small_vec(zeros16, w["ssd_d"]),
        conv_w=jnp.concatenate([w["ssd_conv_w"], w["lru_conv_w"]], axis=1),
        conv_b=jnp.concatenate([w["ssd_conv_b"], w["lru_conv_b"]])[None],
        ssd_norm=w["ssd_norm"][None],
        wa=_block_diag(w["lru_w_a"]).astype(BF16), wx=_block_diag(w["lru_w_x"]).astype(BF16), pvec=pvec,
        wba=w["w_branch_attn"], wbs=w["w_branch_ssd"], wbl=w["w_branch_lru"], wout=w["w_out"],
        g2=w["ffn2_norm"][None], gu2=w["ffn2_w_gate_up"], d2=w["ffn2_w_down"],
    )


def _layer_fwd(h, p, B, L, gather=None):
    h, s1 = _ffn_fwd(h, p["g1"], p["gu1"], p["d1"], "ffn1")
    h, sm, gathered = _mixer_fwd(h, p, B, L, gather)
    h, s2 = _ffn_fwd(h, p["g2"], p["gu2"], p["d2"], "ffn2")
    return h, (s1, sm, s2), gathered


def _layer_bwd(dh, saved, p, w, B, L, parts=None):
    s1, sm, s2 = saved
    D = dh.shape[1]
    d_xbc = D + D_XBC_EXTRA
    dh, f2 = _ffn_bwd(dh, s2, p["g2"], p["gu2"], p["d2"], "ffn2")
    dh, gm, recv = _mixer_bwd(dh, sm, p, B, L, parts)
    dh, f1 = _ffn_bwd(dh, s1, p["g1"], p["gu1"], p["d1"], "ffn1")
    sb, av = gm["small_bias"][0], gm["avec"][0]
    cw = gm["conv_wb"]
    grads = dict(
        ffn1_norm=f1["norm"][0], ffn1_w_gate_up=f1["gu"], ffn1_w_down=f1["down"],
        mix_norm=gm["gm"][0], w_in=gm["w_all_t"],
        fox_forget_bias=sb[:HEADS], ssd_conv_w=cw[:CONV_K, :d_xbc], ssd_conv_b=cw[CONV_K, :d_xbc],
        ssd_dt_bias=sb[HEADS:2 * HEADS], ssd_a_log=av[HEADS:2 * HEADS] * (-jnp.exp(w["ssd_a_log"])),
        ssd_d=gm["dvec"][0, HEADS:2 * HEADS], ssd_norm=gm["ssd_norm"][0],
        lru_conv_w=cw[:CONV_K, d_xbc:], lru_conv_b=cw[CONV_K, d_xbc:],
        lru_w_a=_diag_blocks(gm["wa"], HEADS), lru_b_a=gm["pvec"][0], lru_w_x=_diag_blocks(gm["wx"], HEADS),
        lru_b_x=gm["pvec"][1], lru_lambda=gm["pvec"][2],
        w_branch_attn=gm["wba"], w_branch_ssd=gm["wbs"], w_branch_lru=gm["wbl"], w_out=gm["wout"],
        ffn2_norm=f2["norm"][0], ffn2_w_gate_up=f2["gu"], ffn2_w_down=f2["down"],
    )
    return dh, grads, recv


LAYER_NAMES = ["ffn1_norm", "ffn1_w_gate_up", "ffn1_w_down", "mix_norm", "w_in", "fox_forget_bias", "ssd_conv_w",
               "ssd_conv_b", "ssd_dt_bias", "ssd_a_log", "ssd_d", "ssd_norm", "lru_conv_w", "lru_conv_b", "lru_w_a",
               "lru_b_a", "lru_w_x", "lru_b_x", "lru_lambda", "w_branch_attn", "w_branch_ssd", "w_branch_lru", "w_out",
               "ffn2_norm", "ffn2_w_gate_up", "ffn2_w_down"]
WEIGHT_NAMES = ["meta_tokens"] + LAYER_NAMES + ["final_norm"]


def _local_step(x, target, meta, final_norm, depth, layer_weights, pack_next=None, pack_grads=None):
    B, S, D = x.shape
    L = -(-(N_META + S) // Q_BLOCK) * Q_BLOCK
    h = jnp.concatenate([jnp.broadcast_to(meta[None], (B, N_META, D)), x,
                         jnp.zeros((B, L - N_META - S, D), F32)], axis=1).reshape(B * L, D)
    weights, params, saved = [], [], []
    gathered = None
    for l in range(depth):
        w = layer_weights(l, gathered)
        p = _layer_params(w)
        nxt = pack_next(l + 1) if (pack_next is not None and l + 1 < depth) else None
        h, s, gathered = _layer_fwd(h, p, B, L, nxt)
        weights.append(w)
        params.append(p)
        saved.append(s)
    tgt = jnp.pad(target, ((0, 0), (N_META, L - N_META - S), (0, 0))).reshape(B * L, D)
    dh, loss, dfinal = _loss_bwd(h, tgt, final_norm[None], L, S, "loss")
    grads = [None] * depth
    received, parts = {}, None
    for l in reversed(range(depth)):
        dh, grads[l], recv = _layer_bwd(dh, saved[l], params[l], weights[l], B, L, parts)
        if recv is not None:
            received[l + 1] = recv
        parts = pack_grads(grads[l]) if pack_grads is not None else None
    dh3 = dh.reshape(B, L, D)
    return (loss, dh3[:, N_META:N_META + S], jnp.sum(dh3[:, :N_META], axis=0), grads, dfinal[0], received, parts)


BIG_NAMES = ["ffn1_w_gate_up", "ffn1_w_down", "w_in", "w_branch_attn", "w_branch_ssd", "w_branch_lru", "w_out",
             "ffn2_w_gate_up", "ffn2_w_down"]
COL_SHARDED = {"ffn1_w_gate_up", "w_in", "ffn2_w_gate_up"}
SMALL_SHARDED = ["meta_tokens", "ssd_conv_w", "lru_conv_w"]
SMALL_NAMES = [n for n in LAYER_NAMES if n not in BIG_NAMES]


def _shard_rows(name, shape):
    return _padded(shape[1]) if name in COL_SHARDED else shape[0]


def _pack_shards(shards):
    rows = []
    for n in BIG_NAMES:
        s = shards[n]
        if n in COL_SHARDED:
            s = jnp.pad(s.T, ((0, _padded(s.shape[1]) - s.shape[1]), (0, 0)))
        rows.append(s)
    return jnp.concatenate(rows, axis=0)


def _unpack_gathered(gathered, shapes, D):
    out, o = {}, 0
    for n in BIG_NAMES:
        r = _shard_rows(n, shapes[n])
        out[n] = gathered[:, o:o + r].reshape(N_DEV * r, D)
        o += r
    out["w_in"] = _reorder_rows(out["w_in"], D, shapes["w_in"][1])
    return out


def _pack_full_grads(grads, shapes, D):
    slabs = []
    for n in BIG_NAMES:
        g = grads[n]
        if n == "w_in":
            g = _restore_rows(g, D, shapes[n][1])
        slabs.append(g.reshape(N_DEV, _shard_rows(n, shapes[n]), D))
    return jnp.concatenate(slabs, axis=1)


def _unpack_local(rows, shapes):
    out, o = {}, 0
    for n in BIG_NAMES:
        r = _shard_rows(n, shapes[n])
        blk = rows[o:o + r]
        out[n] = blk[:shapes[n][1]].T if n in COL_SHARDED else blk
        o += r
    return out


def _as_rows(flat):
    n = flat.shape[0]
    unit = LANES * SUBLANES
    total = -(-n // unit) * unit
    return jnp.pad(flat, (0, total - n)).reshape(total // LANES, LANES)


def _flatten_list(arrs):
    return _as_rows(jnp.concatenate([a.reshape(-1) for a in arrs]))


def _split_like(rows, shapes):
    flat = rows.reshape(-1)
    out, o = [], 0
    for s in shapes:
        n = math.prod(s)
        out.append(flat[o:o + n].reshape(s))
        o += n
    return out


def _gather_last(rows8, shape):
    lead, c = shape[:-1], shape[-1]
    t = rows8.reshape((N_DEV,) + tuple(lead) + (c,))
    return jnp.moveaxis(t, 0, -2).reshape(tuple(lead) + (N_DEV * c,))


def kernel(x, meta_tokens, ffn1_norm, ffn1_w_gate_up, ffn1_w_down, mix_norm, w_in, fox_forget_bias, ssd_conv_w, ssd_conv_b, ssd_dt_bias, ssd_a_log, ssd_d, ssd_norm, lru_conv_w, lru_conv_b, lru_w_a, lru_b_a, lru_w_x, lru_b_x, lru_lambda, w_branch_attn, w_branch_ssd, w_branch_lru, w_out, ffn2_norm, ffn2_w_gate_up, ffn2_w_down, final_norm, loss_target, m_meta_tokens, m_ffn1_norm, m_ffn1_w_gate_up, m_ffn1_w_down, m_mix_norm, m_w_in, m_fox_forget_bias, m_ssd_conv_w, m_ssd_conv_b, m_ssd_dt_bias, m_ssd_a_log, m_ssd_d, m_ssd_norm, m_lru_conv_w, m_lru_conv_b, m_lru_w_a, m_lru_b_a, m_lru_w_x, m_lru_b_x, m_lru_lambda, m_w_branch_attn, m_w_branch_ssd, m_w_branch_lru, m_w_out, m_ffn2_norm, m_ffn2_w_gate_up, m_ffn2_w_down, m_final_norm, v_meta_tokens, v_ffn1_norm, v_ffn1_w_gate_up, v_ffn1_w_down, v_mix_norm, v_w_in, v_fox_forget_bias, v_ssd_conv_w, v_ssd_conv_b, v_ssd_dt_bias, v_ssd_a_log, v_ssd_d, v_ssd_norm, v_lru_conv_w, v_lru_conv_b, v_lru_w_a, v_lru_b_a, v_lru_w_x, v_lru_b_x, v_lru_lambda, v_w_branch_attn, v_w_branch_ssd, v_w_branch_lru, v_w_out, v_ffn2_norm, v_ffn2_w_gate_up, v_ffn2_w_down, v_final_norm):
    weights = dict(zip(WEIGHT_NAMES, (meta_tokens, ffn1_norm, ffn1_w_gate_up, ffn1_w_down, mix_norm, w_in, fox_forget_bias, ssd_conv_w, ssd_conv_b, ssd_dt_bias, ssd_a_log, ssd_d, ssd_norm, lru_conv_w, lru_conv_b, lru_w_a, lru_b_a, lru_w_x, lru_b_x, lru_lambda, w_branch_attn, w_branch_ssd, w_branch_lru, w_out, ffn2_norm, ffn2_w_gate_up, ffn2_w_down, final_norm,)))
    mom1 = dict(zip(WEIGHT_NAMES, (m_meta_tokens, m_ffn1_norm, m_ffn1_w_gate_up, m_ffn1_w_down, m_mix_norm, m_w_in, m_fox_forget_bias, m_ssd_conv_w, m_ssd_conv_b, m_ssd_dt_bias, m_ssd_a_log, m_ssd_d, m_ssd_norm, m_lru_conv_w, m_lru_conv_b, m_lru_w_a, m_lru_b_a, m_lru_w_x, m_lru_b_x, m_lru_lambda, m_w_branch_attn, m_w_branch_ssd, m_w_branch_lru, m_w_out, m_ffn2_norm, m_ffn2_w_gate_up, m_ffn2_w_down, m_final_norm,)))
    mom2 = dict(zip(WEIGHT_NAMES, (v_meta_tokens, v_ffn1_norm, v_ffn1_w_gate_up, v_ffn1_w_down, v_mix_norm, v_w_in, v_fox_forget_bias, v_ssd_conv_w, v_ssd_conv_b, v_ssd_dt_bias, v_ssd_a_log, v_ssd_d, v_ssd_norm, v_lru_conv_w, v_lru_conv_b, v_lru_w_a, v_lru_b_a, v_lru_w_x, v_lru_b_x, v_lru_lambda, v_w_branch_attn, v_w_branch_ssd, v_w_branch_lru, v_w_out, v_ffn2_norm, v_ffn2_w_gate_up, v_ffn2_w_down, v_final_norm,)))
    depth = ffn1_norm.shape[0]
    D = x.shape[-1]
    my_idx = 4 * lax.axis_index("x") + 2 * lax.axis_index("y") + lax.axis_index("c")

    small_shapes = [weights[n].shape for n in SMALL_SHARDED]
    gathered = _all_gather(_flatten_list([weights[n] for n in SMALL_SHARDED]), "gather_small").reshape(N_DEV, -1)
    small_full, o = {}, 0
    for n, s in zip(SMALL_SHARDED, small_shapes):
        k = math.prod(s)
        small_full[n] = _gather_last(gathered[:, o:o + k], s)
        o += k

    shard_shapes = {n: weights[n].shape[1:] for n in BIG_NAMES}
    pack_next = lambda l: _pack_shards({n: weights[n][l].astype(BF16) for n in BIG_NAMES})

    def layer_weights(l, gathered):
        if gathered is None:
            gathered = _all_gather(pack_next(l), "gather_weights")
        w = _unpack_gathered(gathered, shard_shapes, D)
        for n in SMALL_NAMES:
            w[n] = small_full[n][l] if n in SMALL_SHARDED else weights[n][l]
        return w

    pack_grads = lambda g: _pack_full_grads(g, shard_shapes, D).astype(BF16)
    loss, dx, dmeta, grads, dfinal, received, parts = _local_step(
        x, loss_target, small_full["meta_tokens"], final_norm, depth, layer_weights, pack_next, pack_grads)
    received[0] = _exchange(parts, "exchange_grads")
    loss = lax.psum(loss[0, 0], ("x", "y", "c"))
    summed = {n: [] for n in WEIGHT_NAMES}
    for l in range(depth):
        local = _unpack_local(_sum8(received[l], "sum_grads"), shard_shapes)
        for n in BIG_NAMES:
            summed[n].append(local[n])

    small_list = [dmeta, dfinal] + [grads[l][n] for l in range(depth) for n in SMALL_NAMES]
    total = _sum8(_all_gather(_flatten_list(small_list), "gather_small_grads"), "sum_small_grads")
    parts = _split_like(total, [a.shape for a in small_list])
    full_small = {"meta_tokens": parts[0], "final_norm": parts[1]}
    for i, n in enumerate(SMALL_NAMES):
        full_small[n] = jnp.stack([parts[2 + l * len(SMALL_NAMES) + i] for l in range(depth)])
    grad = {}
    for n in WEIGHT_NAMES:
        if n in BIG_NAMES:
            grad[n] = jnp.stack(summed[n])
        elif n in SMALL_SHARDED:
            c = weights[n].shape[-1]
            grad[n] = lax.dynamic_slice_in_dim(full_small[n], my_idx * c, c, axis=full_small[n].ndim - 1)
        else:
            grad[n] = full_small[n]

    delta, new_m, new_v = {}, {}, {}
    for n in WEIGHT_NAMES:
        delta[n], new_m[n], new_v[n] = _adamw(weights[n], grad[n], mom1[n], mom2[n], "adamw_" + n)
    return (loss, dx, *[grad[n] for n in WEIGHT_NAMES], *[delta[n] for n in WEIGHT_NAMES],
            *[new_m[n] for n in WEIGHT_NAMES], *[new_v[n] for n in WEIGHT_NAMES])
```

```python
import functools
import math

import jax
import jax.numpy as jnp
from jax import lax
from jax.experimental import pallas as pl
from jax.experimental.pallas import tpu as pltpu

F32 = jnp.float32
BF16 = jnp.bfloat16

N_DEV = 8
N_META = 16
Q_BLOCK = 128
NORM_EPS = 1e-6
HEADS = 16
HEAD_DIM = 64
SSD_GROUPS = 2
SSD_STATE = 128
CONV_K = 4
LRU_C = 8.0
ADAM_LR, ADAM_B1, ADAM_B2, ADAM_EPS, ADAM_WD, ADAM_STEP = 0.001, 0.9, 0.999, 1e-08, 0.01, 10

LANES = 128
SUBLANES = 8
VMEM_LIMIT = 56 * 1024 * 1024
NEG = -1e30
MM_TILE = 1408
MM_VMEM = 40 * 1024 * 1024


def _cparams(sem=None):
    return pltpu.CompilerParams(dimension_semantics=sem, vmem_limit_bytes=VMEM_LIMIT)


def _tile(dim, target, mult=LANES):
    if dim <= target:
        return dim
    best = None
    for t in range(mult, target + 1, mult):
        if dim % t == 0:
            best = t
    assert best is not None, (dim, target)
    return best


def _sigmoid(x):
    return 1.0 / (1.0 + jnp.exp(-x))


def _log1p_exp_neg_abs(x):
    e = jnp.exp(-jnp.abs(x))
    u = 1.0 + e
    return jnp.where(u == 1.0, e, jnp.log(u) * (e / jnp.where(u == 1.0, 1.0, u - 1.0)))


def _log_sigmoid(x):
    return jnp.minimum(x, 0.0) - _log1p_exp_neg_abs(x)


def _softplus(x):
    return jnp.maximum(x, 0.0) + _log1p_exp_neg_abs(x)


def _one_minus_exp(y):
    u = jnp.exp(y)
    safe = jnp.where(u == 1.0, 0.5, u)
    return jnp.where(u == 1.0, -y, (1.0 - u) * y / jnp.log(safe))


def _silu(x):
    return x * _sigmoid(x)


def _dsilu(x):
    s = _sigmoid(x)
    return s * (1.0 + x * (1.0 - s))


_GELU_C = math.sqrt(2.0 / math.pi)


def _gelu(x):
    return 0.5 * x * (1.0 + jnp.tanh(_GELU_C * (x + 0.044715 * x * x * x)))


def _dgelu(x):
    t = jnp.tanh(_GELU_C * (x + 0.044715 * x * x * x))
    return 0.5 * (1.0 + t) + 0.5 * x * (1.0 - t * t) * _GELU_C * (1.0 + 3.0 * 0.044715 * x * x)


def _split3_dot(tri, x):
    hi = x.astype(BF16)
    r1 = x - hi.astype(F32)
    mid = r1.astype(BF16)
    lo = (r1 - mid.astype(F32)).astype(BF16)
    t = tri.astype(BF16)
    d = lambda p: jnp.dot(t, p, preferred_element_type=F32)
    return d(hi) + d(mid) + d(lo)


def _lower_tri(n, strict=False):
    r = lax.broadcasted_iota(jnp.int32, (n, n), 0)
    c = lax.broadcasted_iota(jnp.int32, (n, n), 1)
    return (c < r) if strict else (c <= r)


def _mm(a, b, *, ta=False, tb=False, out_dtype=F32, res=None, scale=None, tm=None, tn=None, tk=None,
        a_off=(0, 0), b_off=(0, 0), dims=None, name):
    if dims is None:
        M, K = (a.shape[1], a.shape[0]) if ta else a.shape
        N = b.shape[0] if tb else b.shape[1]
    else:
        M, N, K = dims
    tk = tk or (K if K <= 2816 else _tile(K, 1408))
    nk_ = K // tk
    pick_m, pick_n = tm is None, tn is None
    tm = tm or _tile(M, MM_TILE)
    tn = tn or _tile(N, MM_TILE)

    def vmem(tm_, tn_):
        a_b = tm_ * tk * a.dtype.itemsize + (tm_ * tk * 2 if a.dtype != BF16 else 0)
        b_b = tn_ * tk * b.dtype.itemsize + (tn_ * tk * 2 if b.dtype != BF16 else 0)
        o_b = tm_ * tn_ * (jnp.dtype(out_dtype).itemsize + (4 if res is not None else 0))
        return 2 * (a_b + b_b + o_b) + (tm_ * tn_ * 4 if nk_ > 1 else 0) + tm_ * tn_ * 4

    while vmem(tm, tn) > MM_VMEM and (pick_m or pick_n):
        if pick_m and (tm >= tn or not pick_n) and tm > LANES:
            tm = _tile(M, tm - LANES)
        elif pick_n and tn > LANES:
            tn = _tile(N, tn - LANES)
        else:
            break
    assert M % tm == 0 and N % tn == 0 and K % tk == 0, (name, M, N, K, tm, tn, tk)
    nk = K // tk
    ca = 0 if ta else 1
    cb = 1 if tb else 0

    def blk(rows, cols, off):
        assert off[0] % rows == 0 and off[1] % cols == 0, (name, off, rows, cols)
        return off[0] // rows, off[1] // cols

    if ta:
        ao = blk(tk, tm, a_off)
        a_spec = pl.BlockSpec((tk, tm), lambda i, j, k: (k + ao[0], i + ao[1]))
    else:
        ao = blk(tm, tk, a_off)
        a_spec = pl.BlockSpec((tm, tk), lambda i, j, k: (i + ao[0], k + ao[1]))
    if tb:
        bo = blk(tn, tk, b_off)
        b_spec = pl.BlockSpec((tn, tk), lambda i, j, k: (j + bo[0], k + bo[1]))
    else:
        bo = blk(tk, tn, b_off)
        b_spec = pl.BlockSpec((tk, tn), lambda i, j, k: (k + bo[0], j + bo[1]))
    o_spec = pl.BlockSpec((tm, tn), lambda i, j, k: (i, j))
    in_specs = [a_spec, b_spec] + ([o_spec] if res is not None else [])
    has_res = res is not None

    def kern(*refs):
        if has_res:
            a_ref, b_ref, r_ref, o_ref = refs[:4]
            scr = refs[4:]
        else:
            a_ref, b_ref, o_ref = refs[:3]
            r_ref = None
            scr = refs[3:]
        p = lax.dot_general(a_ref[...].astype(BF16), b_ref[...].astype(BF16), (((ca,), (cb,)), ((), ())),
                            preferred_element_type=F32)

        def fin(val):
            if scale is not None:
                val = val * scale
            if has_res:
                val = r_ref[...] + val
            o_ref[...] = val.astype(out_dtype)

        if nk == 1:
            fin(p)
        else:
            acc = scr[0]
            k = pl.program_id(2)

            @pl.when(k == 0)
            def _():
                acc[...] = p

            @pl.when(k > 0)
            def _():
                acc[...] += p

            @pl.when(k == nk - 1)
            def _():
                fin(acc[...])

    args = (a, b) + ((res,) if has_res else ())
    return pl.pallas_call(
        kern, name=name, grid=(M // tm, N // tn, nk), in_specs=in_specs, out_specs=o_spec,
        out_shape=jax.ShapeDtypeStruct((M, N), out_dtype),
        scratch_shapes=[pltpu.VMEM((tm, tn), F32)] if nk > 1 else [],
        compiler_params=_cparams(("parallel", "parallel", "arbitrary")),
    )(*args)


def _rows(body, tiled, full, outs, accs, *, tr, name, T):
    assert T % tr == 0
    in_specs = []
    for arr, width, off in tiled:
        assert off % width == 0, (name, off, width)
        in_specs.append(pl.BlockSpec((tr, width), functools.partial(lambda i, o: (i, o), o=off // width)))
    for arr in full:
        in_specs.append(pl.BlockSpec(arr.shape, lambda i: (0, 0)))
    out_specs = [pl.BlockSpec((tr, w), lambda i: (i, 0)) for w, _ in outs]
    out_specs += [pl.BlockSpec(s, lambda i: (0, 0)) for s, _ in accs]
    out_shape = [jax.ShapeDtypeStruct((T, w), d) for w, d in outs] + [jax.ShapeDtypeStruct(s, d) for s, d in accs]
    nt, nf, no = len(tiled), len(full), len(outs)

    def kern(*refs):
        i = pl.program_id(0)
        acc_refs = refs[nt + nf + no:]

        @pl.when(i == 0)
        def _():
            for r in acc_refs:
                r[...] = jnp.zeros(r.shape, r.dtype)

        body(i, refs[:nt], refs[nt:nt + nf], refs[nt + nf:nt + nf + no], acc_refs)

    res = pl.pallas_call(
        kern, name=name, grid=(T // tr,), in_specs=in_specs, out_specs=out_specs, out_shape=out_shape,
        compiler_params=_cparams(("arbitrary",)),
    )(*[t[0] for t in tiled], *full)
    return res


def _colsum(x):
    return jnp.sum(x, axis=0, keepdims=True)


def _norm_fwd(h, g, name):
    T, D = h.shape

    def body(i, t, f, o, a):
        x = t[0][...]
        r = lax.rsqrt(jnp.mean(x * x, axis=-1, keepdims=True) + NORM_EPS)
        o[0][...] = (x * r * f[0][...]).astype(BF16)

    return _rows(body, [(h, D, 0)], [g], [(D, BF16)], [], tr=_tile(T, 768, 8), name=name, T=T)[0]


def _norm_bwd(h, dn, dh, g, name):
    T, D = h.shape

    def body(i, t, f, o, a):
        x, dnv, dhv = t[0][...], t[1][...], t[2][...]
        r = lax.rsqrt(jnp.mean(x * x, axis=-1, keepdims=True) + NORM_EPS)
        xh = x * r
        dng = dnv * f[0][...]
        o[0][...] = dhv + r * (dng - xh * jnp.mean(dng * xh, axis=-1, keepdims=True))
        a[0][...] += _colsum(dnv * xh)

    return _rows(body, [(h, D, 0), (dn, D, 0), (dh, D, 0)], [g], [(D, F32)], [((1, D), F32)],
                 tr=_tile(T, 384, 8), name=name, T=T)


def _swiglu_fwd(gu, name):
    T, F2 = gu.shape
    F = F2 // 2

    def body(i, t, f, o, a):
        o[0][...] = (_silu(t[0][...]) * t[1][...]).astype(BF16)

    return _rows(body, [(gu, F, 0), (gu, F, F)], [], [(F, BF16)], [], tr=_tile(T, 256, 8), name=name, T=T)[0]


def _swiglu_bwd(gu, da, name):
    T, F2 = gu.shape
    F = F2 // 2

    def body(i, t, f, o, a):
        gv, uv, dav = t[0][...], t[1][...], t[2][...]
        o[0][:, :F] = (dav * uv * _dsilu(gv)).astype(BF16)
        o[0][:, F:] = (dav * _silu(gv)).astype(BF16)

    return _rows(body, [(gu, F, 0), (gu, F, F), (da, F, 0)], [], [(F2, BF16)], [], tr=_tile(T, 256, 8),
                 name=name, T=T)[0]


def _merge_fwd(proj, off, pa, pb, pc, name):
    T, D = pa.shape

    def body(i, t, f, o, a):
        o[0][...] = (_sigmoid(t[0][...]) * t[3][...] + _sigmoid(t[1][...]) * t[4][...]
                     + _sigmoid(t[2][...]) * t[5][...]).astype(BF16)

    tiled = [(proj, D, off), (proj, D, off + D), (proj, D, off + 2 * D), (pa, D, 0), (pb, D, 0), (pc, D, 0)]
    return _rows(body, tiled, [], [(D, BF16)], [], tr=_tile(T, 384, 8), name=name, T=T)[0]


def _merge_bwd(dmixed, proj, off, pa, pb, pc, name):
    T, D = pa.shape

    def body(i, t, f, o, a):
        dm = t[0][...]
        for k in range(3):
            g = _sigmoid(t[1 + k][...])
            o[k][...] = (dm * g).astype(BF16)
            o[3][:, k * D:(k + 1) * D] = (dm * t[4 + k][...] * g * (1.0 - g)).astype(BF16)

    tiled = [(dmixed, D, 0), (proj, D, off), (proj, D, off + D), (proj, D, off + 2 * D), (pa, D, 0), (pb, D, 0),
             (pc, D, 0)]
    return _rows(body, tiled, [], [(D, BF16)] * 3 + [(3 * D, BF16)], [], tr=_tile(T, 384, 8), name=name, T=T)


def _gnorm_fwd(y, proj, zoff, nw, name):
    T, D = y.shape
    gs = D // SSD_GROUPS

    def body(i, t, f, o, a):
        s = t[0][...] * _silu(t[1][...])
        for g in range(SSD_GROUPS):
            sg = s[:, g * gs:(g + 1) * gs]
            r = lax.rsqrt(jnp.mean(sg * sg, axis=-1, keepdims=True) + NORM_EPS)
            o[0][:, g * gs:(g + 1) * gs] = (sg * r * f[0][:, g * gs:(g + 1) * gs]).astype(BF16)

    return _rows(body, [(y, D, 0), (proj, D, zoff)], [nw], [(D, BF16)], [], tr=_tile(T, 384, 8), name=name, T=T)[0]


def _gnorm_bwd(dout, y, proj, zoff, nw, name):
    T, D = y.shape
    gs = D // SSD_GROUPS

    def body(i, t, f, o, a):
        dov, yv, zv = t[0][...], t[1][...], t[2][...]
        sz = _silu(zv)
        s = yv * sz
        dsz = _dsilu(zv)
        for g in range(SSD_GROUPS):
            sl = slice(g * gs, (g + 1) * gs)
            sg = s[:, sl]
            r = lax.rsqrt(jnp.mean(sg * sg, axis=-1, keepdims=True) + NORM_EPS)
            sh = sg * r
            dog = dov[:, sl]
            dng = dog * f[0][:, sl]
            ds = r * (dng - sh * jnp.mean(dng * sh, axis=-1, keepdims=True))
            o[0][:, sl] = ds * sz[:, sl]
            o[1][:, sl] = (ds * yv[:, sl] * dsz[:, sl]).astype(BF16)
            a[0][:, sl] += _colsum(dog * sh)

    return _rows(body, [(dout, D, 0), (y, D, 0), (proj, D, zoff)], [nw], [(D, F32), (D, BF16)], [((1, D), F32)],
                 tr=_tile(T, 384, 8), name=name, T=T)


def _loss_bwd(h, tgt, g, seq_len, n_real, name):
    T, D = h.shape
    tr = _tile(seq_len, 384, 8)
    per_seq = seq_len // tr

    def body(i, t, f, o, a):
        x, tg = t[0][...], t[1][...]
        pos = (i % per_seq) * tr + lax.broadcasted_iota(jnp.int32, (tr, 1), 0)
        valid = (pos >= N_META) & (pos < N_META + n_real)
        r = lax.rsqrt(jnp.mean(x * x, axis=-1, keepdims=True) + NORM_EPS)
        xh = x * r
        e = jnp.where(valid, xh * f[0][...] - tg, 0.0)
        a[0][...] += jnp.zeros((1, LANES), F32) + 0.5 * jnp.sum(jnp.sum(e * e, axis=-1, keepdims=True) / D,
                                                              axis=0, keepdims=True)
        dy = e / D
        dng = dy * f[0][...]
        o[0][...] = r * (dng - xh * jnp.mean(dng * xh, axis=-1, keepdims=True))
        a[1][...] += _colsum(dy * xh)

    return _rows(body, [(h, D, 0), (tgt, D, 0)], [g], [(D, F32)], [((1, LANES), F32), ((1, D), F32)], tr=tr,
                 name=name, T=T)


def _lane_is_attn(shape):
    return lax.broadcasted_iota(jnp.int32, shape, len(shape) - 1) < HEADS


def _gate_prep(proj3, col_blk, bias, avec, name):
    B, L, _ = proj3.shape
    Q = Q_BLOCK
    nc = L // Q

    def kern(x_ref, b_ref, a_ref, v_ref, c_ref, carry):
        c = pl.program_id(1)

        @pl.when(c == 0)
        def _():
            carry[...] = jnp.zeros_like(carry)

        x = x_ref[0] + b_ref[...]
        attn = _lane_is_attn(x.shape)
        v = jnp.where(attn, _log_sigmoid(x), _softplus(x))
        w = jnp.where(attn, v, v * a_ref[...])
        cs = _split3_dot(_lower_tri(Q), w) + jnp.where(attn[:1], carry[...], 0.0)
        v_ref[0] = v
        c_ref[0] = cs
        rows = lax.broadcasted_iota(jnp.int32, (Q, 1), 0)
        carry[...] = jnp.sum(jnp.where(rows == Q - 1, cs, 0.0), axis=0, keepdims=True)

    blk = pl.BlockSpec((1, Q, LANES), lambda b, c: (b, c, 0))
    vec = pl.BlockSpec((1, LANES), lambda b, c: (0, 0))
    return pl.pallas_call(
        kern, name=name, grid=(B, nc),
        in_specs=[pl.BlockSpec((1, Q, LANES), lambda b, c: (b, c, col_blk)), vec, vec],
        out_specs=[blk, blk], out_shape=[jax.ShapeDtypeStruct((B, L, LANES), F32)] * 2,
        scratch_shapes=[pltpu.VMEM((1, LANES), F32)],
        compiler_params=_cparams(("parallel", "arbitrary")),
    )(proj3, bias, avec)


def _gate_post(drow, dcol, ddt, proj3, col_blk, vals, bias, avec, name):
    B, L, _ = proj3.shape
    Q = Q_BLOCK
    nc = L // Q

    def kern(dr_ref, dc_ref, dd_ref, x_ref, v_ref, b_ref, a_ref, o_ref, db_ref, da_ref, carry):
        b = pl.program_id(0)
        c = pl.program_id(1)

        @pl.when((b == 0) & (c == 0))
        def _():
            db_ref[...] = jnp.zeros_like(db_ref)
            da_ref[...] = jnp.zeros_like(da_ref)

        @pl.when(c == 0)
        def _():
            carry[...] = jnp.zeros_like(carry)

        x = x_ref[0] + b_ref[...]
        attn = _lane_is_attn(x.shape)
        dcs = dr_ref[0] + dc_ref[0]
        upper = jnp.logical_not(_lower_tri(Q, strict=True))
        rc = _split3_dot(upper, dcs) + jnp.where(attn[:1], carry[...], 0.0)
        rows = lax.broadcasted_iota(jnp.int32, (Q, 1), 0)
        carry[...] = jnp.sum(jnp.where(rows == 0, rc, 0.0), axis=0, keepdims=True)
        dv = jnp.where(attn, rc, dd_ref[0] + rc * a_ref[...])
        dpre = dv * jnp.where(attn, _sigmoid(-x), _sigmoid(x))
        o_ref[0] = dpre.astype(BF16)
        db_ref[...] += _colsum(dpre)
        da_ref[...] += _colsum(jnp.where(attn, 0.0, rc * v_ref[0]))

    rev = pl.BlockSpec((1, Q, LANES), lambda b, c: (b, nc - 1 - c, 0))
    vec = pl.BlockSpec((1, LANES), lambda b, c: (0, 0))
    return pl.pallas_call(
        kern, name=name, grid=(B, nc),
        in_specs=[rev, rev, rev, pl.BlockSpec((1, Q, LANES), lambda b, c: (b, nc - 1 - c, col_blk)), rev, vec, vec],
        out_specs=[rev, vec, vec],
        out_shape=[jax.ShapeDtypeStruct((B, L, LANES), BF16), jax.ShapeDtypeStruct((1, LANES), F32),
                   jax.ShapeDtypeStruct((1, LANES), F32)],
        scratch_shapes=[pltpu.VMEM((1, LANES), F32)],
        compiler_params=_cparams(("arbitrary", "arbitrary")),
    )(drow, dcol, ddt, proj3, vals, bias, avec)


def _lane_col(tile, lane):
    sel = lax.broadcasted_iota(jnp.int32, tile.shape, 1) == lane
    return jnp.sum(jnp.where(sel, tile, 0.0), axis=1, keepdims=True)


AUG = LANES
AUG_A = HEAD_DIM
AUG_B = HEAD_DIM + 3


def _split3(x):
    hi = x.astype(BF16).astype(F32)
    mid = (x - hi).astype(BF16).astype(F32)
    lo = (x - hi - mid).astype(BF16).astype(F32)
    return hi, mid, lo


def _put3(base, lane, first, x):
    hi, mid, lo = _split3(x)
    return jnp.where(lane == first, hi, jnp.where(lane == first + 1, mid, jnp.where(lane == first + 2, lo, base)))


HP = 2


def _other_half(x):
    return pltpu.roll(x, HEAD_DIM, 1)


def _loop_by_twos(n, step, init):
    carry = lax.fori_loop(0, n // 2, lambda t, c: step(2 * t + 1, step(2 * t, c)), init)
    return lax.cond(n % 2 == 1, lambda c: step(n - 1, c), lambda c: c, carry)


def _attn_pack(proj3, cums, name):
    B, L, _ = proj3.shape
    D = HEADS * HEAD_DIM
    nh = HEADS // HP
    tr = _tile(L, 384)
    scale = HEAD_DIM ** -0.5

    def kern(q_ref, k_ref, v_ref, c_ref, qa_ref, ka_ref, va_ref):
        hp = pl.program_id(2)
        lane = lax.broadcasted_iota(jnp.int32, (tr, AUG), 1)
        head = lane < HEAD_DIM
        ones_a = jnp.where((lane >= AUG_A) & (lane < AUG_A + 3), 1.0, 0.0)
        ones_b = jnp.where((lane >= AUG_B) & (lane < AUG_B + 3), 1.0, 0.0)
        ct = c_ref[0]
        for hh in range(HP):
            c = _lane_col(ct, HP * hp + hh)
            sel = (lambda t: t) if hh == 0 else _other_half
            qa_ref[0, hh] = jnp.where(head, sel(q_ref[0]) * scale, _put3(ones_b, lane, AUG_A, c)).astype(BF16)
            ka_ref[0, hh] = jnp.where(head, sel(k_ref[0]), _put3(ones_a, lane, AUG_B, -c)).astype(BF16)
            va_ref[0, hh] = jnp.where(head, sel(v_ref[0]), ones_a).astype(BF16)

    def win(first):
        return pl.BlockSpec((1, tr, LANES), lambda b, i, h: (b, i, first + h))

    out = pl.BlockSpec((1, HP, tr, AUG), lambda b, i, h: (b, h, i, 0))
    return pl.pallas_call(
        kern, name=name, grid=(B, L // tr, nh),
        in_specs=[win(0), win(D // LANES), win(2 * D // LANES), pl.BlockSpec((1, tr, LANES), lambda b, i, h: (b, i, 0))],
        out_specs=[out] * 3, out_shape=[jax.ShapeDtypeStruct((B, HEADS, L, AUG), BF16)] * 3,
        compiler_params=_cparams(("parallel", "parallel", "arbitrary")),
    )(proj3, proj3, proj3, cums)


def _attn_fwd(qa, ka, va, name, gather=None):
    B, H, L, _ = qa.shape
    tq = _tile(L, 384)
    nq = L // tq
    nh = H // HP
    comm = gather is not None

    def kern(*refs):
        if comm:
            q_ref, k_ref, v_ref, x_ref, y_ref, l_ref, g_ref, send_sems, recv_sems, local_sem = refs
        else:
            q_ref, k_ref, v_ref, y_ref, l_ref = refs
        qi = pl.program_id(2)
        if comm:
            _ride((pl.program_id(0) * nh + pl.program_id(1)) * nq + qi, B * nh * nq,
                  _gather_phases(x_ref, g_ref, send_sems, recv_sems, local_sem))
        qs = [q_ref[0, hh] for hh in range(HP)]
        causal = _lower_tri(tq)

        def step(j, carry, masked):
            rows = pl.ds(pl.multiple_of(j * tq, tq), tq)
            out = []
            for hh in range(HP):
                m, acc = carry[hh]
                s = _dot_nt(qs[hh], k_ref[0, hh, rows, :])
                if masked:
                    s = jnp.where(causal, s, NEG)
                m_new = jnp.maximum(m, jnp.max(s, axis=1, keepdims=True))
                p = jnp.exp(s - m_new)
                out.append((m_new, jnp.exp(m - m_new) * acc + _dot(p.astype(BF16), v_ref[0, hh, rows, :])))
            return tuple(out)

        init = tuple((jnp.full((tq, 1), NEG, F32), jnp.zeros((tq, AUG), F32)) for _ in range(HP))
        carry = _loop_by_twos(qi, lambda j, c: step(j, c, False), init)
        outs = []
        for hh, (m, acc) in enumerate(step(qi, carry, True)):
            l = _lane_col(acc, AUG_A)
            outs.append(acc / l)
            l_ref[0, hh] = m + jnp.log(l)
        head = lax.broadcasted_iota(jnp.int32, (tq, AUG), 1) < HEAD_DIM
        y_ref[0] = jnp.where(head, outs[0], _other_half(outs[1]))

    qspec = pl.BlockSpec((1, HP, tq, AUG), lambda b, h, i: (b, h, i, 0))
    kvspec = pl.BlockSpec((1, HP, L, AUG), lambda b, h, i: (b, h, 0, 0))
    lspec = pl.BlockSpec((1, HP, tq, 1), lambda b, h, i: (b, h, i, 0))
    yspec = pl.BlockSpec((1, tq, LANES), lambda b, h, i: (b, i, h))
    out_shape = [jax.ShapeDtypeStruct((B, L, H * HEAD_DIM), F32), jax.ShapeDtypeStruct((B, H, L, 1), F32)]
    if comm:
        out_shape.append(jax.ShapeDtypeStruct((N_DEV,) + gather.shape, gather.dtype))
    return pl.pallas_call(
        kern, name=name, grid=(B, nh, nq), in_specs=[qspec, kvspec, kvspec] + ([ANY] if comm else []),
        out_specs=[yspec, lspec] + ([ANY] if comm else []), out_shape=out_shape,
        scratch_shapes=COMM_SCRATCH if comm else [],
        compiler_params=_cparams(("arbitrary",) * 3 if comm else ("parallel", "parallel", "arbitrary")),
    )(qa, ka, va, *([gather] if comm else []))


def _attn_bwd(qa, ka, va, y, dy, lse, name, parts=None):
    B, H, L, _ = qa.shape
    tq = _tile(L, 384)
    nq = L // tq
    nh = H // HP
    comm = parts is not None
    scale = HEAD_DIM ** -0.5

    def kern(*refs):
        if comm:
            (q_ref, k_ref, v_ref, y_ref, dy_ref, l_ref, p_ref, dq_ref, dk_ref, dv_ref, cq_ref, ck_ref, r_ref,
             dk_acc, dv_acc, send_sems, recv_sems, local_sem) = refs
        else:
            q_ref, k_ref, v_ref, y_ref, dy_ref, l_ref, dq_ref, dk_ref, dv_ref, cq_ref, ck_ref, dk_acc, dv_acc = refs
        qi = pl.program_id(2)
        if comm:
            _ride((pl.program_id(0) * nh + pl.program_id(1)) * nq + qi, B * nh * nq,
                  _exchange_phases(p_ref, r_ref, send_sems, recv_sems, local_sem))

        @pl.when(qi == 0)
        def _():
            dk_acc[...] = jnp.zeros_like(dk_acc)
            dv_acc[...] = jnp.zeros_like(dv_acc)

        lane = lax.broadcasted_iota(jnp.int32, (tq, AUG), 1)
        head = lane < HEAD_DIM
        qbs, dobs = [], []
        for hh in range(HP):
            sel = (lambda t: t) if hh == 0 else _other_half
            qf = q_ref[0, hh].astype(F32)
            dov = jnp.where(head, sel(dy_ref[0]), 0.0)
            dsum = jnp.sum(dov * sel(y_ref[0]), axis=1, keepdims=True)
            dobs.append(_put3(dov, lane, AUG_A, -dsum).astype(BF16))
            c_t = jnp.sum(jnp.where((lane >= AUG_A) & (lane < AUG_A + 3), qf, 0.0), axis=1, keepdims=True)
            qbs.append(_put3(qf, lane, AUG_A, c_t - l_ref[0, hh]).astype(BF16))
        causal = _lower_tri(tq)

        def step(j, dqs, masked):
            rows = pl.ds(pl.multiple_of(j * tq, tq), tq)
            out = []
            for hh in range(HP):
                kj = k_ref[0, hh, rows, :]
                s = _dot_nt(qbs[hh], kj)
                if masked:
                    s = jnp.where(causal, s, NEG)
                p = jnp.exp(s)
                ds = (p * _dot_nt(dobs[hh], v_ref[0, hh, rows, :])).astype(BF16)
                dv_acc[hh, rows, :] += _dot_tn(p.astype(BF16), dobs[hh])
                dk_acc[hh, rows, :] += _dot_tn(ds, qbs[hh])
                out.append(dqs[hh] + _dot(ds, kj))
            return tuple(out)

        dqs = _loop_by_twos(qi, lambda j, c: step(j, c, False), tuple(jnp.zeros((tq, AUG), F32) for _ in range(HP)))
        dqs = step(qi, dqs, True)
        for hh in range(HP):
            cq_ref[0, hh] = _lane_col(dqs[hh], AUG_A)
        dq_ref[0] = (jnp.where(head, dqs[0], _other_half(dqs[1])) * scale).astype(BF16)

        @pl.when(qi == nq - 1)
        def _():
            full = lax.broadcasted_iota(jnp.int32, (L, AUG), 1) < HEAD_DIM
            dk_ref[0] = jnp.where(full, dk_acc[0], _other_half(dk_acc[1])).astype(BF16)
            dv_ref[0] = jnp.where(full, dv_acc[0], _other_half(dv_acc[1])).astype(BF16)
            for hh in range(HP):
                ck_ref[0, hh] = _lane_col(dk_acc[hh], AUG_B)

    qspec = pl.BlockSpec((1, HP, tq, AUG), lambda b, h, i: (b, h, i, 0))
    kvspec = pl.BlockSpec((1, HP, L, AUG), lambda b, h, i: (b, h, 0, 0))
    lspec = pl.BlockSpec((1, HP, tq, 1), lambda b, h, i: (b, h, i, 0))
    lfull = pl.BlockSpec((1, HP, L, 1), lambda b, h, i: (b, h, 0, 0))
    yspec = pl.BlockSpec((1, tq, LANES), lambda b, h, i: (b, i, h))
    yfull = pl.BlockSpec((1, L, LANES), lambda b, h, i: (b, 0, h))
    nat = jax.ShapeDtypeStruct((B, L, H * HEAD_DIM), BF16)
    col = jax.ShapeDtypeStruct((B, H, L, 1), F32)
    out_shape = [nat, nat, nat, col, col]
    if comm:
        out_shape.append(jax.ShapeDtypeStruct(parts.shape, parts.dtype))
    return pl.pallas_call(
        kern, name=name, grid=(B, nh, nq),
        in_specs=[qspec, kvspec, kvspec, yspec, yspec, lspec] + ([ANY] if comm else []),
        out_specs=[yspec, yfull, yfull, lspec, lfull] + ([ANY] if comm else []), out_shape=out_shape,
        scratch_shapes=[pltpu.VMEM((HP, L, AUG), F32), pltpu.VMEM((HP, L, AUG), F32)] + (COMM_SCRATCH if comm else []),
        compiler_params=_cparams(("arbitrary",) * 3 if comm else ("parallel", "parallel", "arbitrary")),
    )(qa, ka, va, y, dy, lse, *([parts] if comm else []))


PAD = SUBLANES


def _halo_tile(x_ref, i, TR):
    r0 = pl.multiple_of(i * TR, TR)
    before = x_ref[0, pl.ds(pl.multiple_of(jnp.maximum(r0 - PAD, 0), PAD), PAD), :]
    return jnp.concatenate([jnp.where(i > 0, before, 0.0), x_ref[0, pl.ds(r0, TR), :]], axis=0)


def _conv_fwd(x3, x_blk, w, b, n_silu, name):
    B, L, _ = x3.shape
    C = w.shape[1]
    TR = _tile(L, 384, 8)

    def kern(x_ref, w_ref, b_ref, o_ref):
        cb = pl.program_id(1)

        def body(i, carry):
            r0 = pl.multiple_of(i * TR, TR)
            ext = _halo_tile(x_ref, i, TR)
            acc = jnp.zeros((TR, LANES), F32) + b_ref[...]
            for k in range(CONV_K):
                s = CONV_K - 1 - k
                sh = ext if s == 0 else pltpu.roll(ext, s, 0)
                acc = acc + w_ref[k:k + 1, :] * sh[PAD:PAD + TR]
            o_ref[0, pl.ds(r0, TR), :] = jnp.where(cb < n_silu, _silu(acc), acc)
            return carry

        lax.fori_loop(0, L // TR, body, 0)

    return pl.pallas_call(
        kern, name=name, grid=(B, C // LANES),
        in_specs=[pl.BlockSpec((1, L, LANES), lambda b_, c: (b_, 0, x_blk + c)),
                  pl.BlockSpec((CONV_K, LANES), lambda b_, c: (0, c)), pl.BlockSpec((1, LANES), lambda b_, c: (0, c))],
        out_specs=pl.BlockSpec((1, L, LANES), lambda b_, c: (b_, 0, c)),
        out_shape=jax.ShapeDtypeStruct((B, L, C), F32),
        compiler_params=_cparams(("parallel", "parallel")),
    )(x3, w, b)


def _conv_bwd_pre(x3, x_blk, du, w, b, n_silu, name):
    B, L, C = du.shape
    Lp = L + PAD
    TR = _tile(L, 384, 8)

    def kern(x_ref, du_ref, w_ref, b_ref, dp_ref, dw_ref):
        cb = pl.program_id(0)

        @pl.when(pl.program_id(1) == 0)
        def _():
            dw_ref[...] = jnp.zeros_like(dw_ref)

        def body(i, carry):
            r0 = pl.multiple_of(i * TR, TR)
            ext = _halo_tile(x_ref, i, TR)
            taps = []
            acc = jnp.zeros((TR, LANES), F32) + b_ref[...]
            for k in range(CONV_K):
                s = CONV_K - 1 - k
                sh = ext if s == 0 else pltpu.roll(ext, s, 0)
                taps.append(sh[PAD:PAD + TR])
                acc = acc + w_ref[k:k + 1, :] * taps[-1]
            dv = du_ref[0, pl.ds(r0, TR), :]
            dpre = jnp.where(cb < n_silu, dv * _dsilu(acc), dv)
            dp_ref[0, pl.ds(r0, TR), :] = dpre
            return tuple(c + _colsum(dpre * t) for c, t in zip(carry[:CONV_K], taps)) + (carry[CONV_K] + _colsum(dpre),)

        z = jnp.zeros((1, LANES), F32)
        sums = lax.fori_loop(0, L // TR, body, (z,) * (CONV_K + 1))
        dp_ref[0, pl.ds(L, PAD), :] = jnp.zeros((PAD, LANES), F32)
        for k in range(CONV_K + 1):
            dw_ref[k:k + 1, :] += sums[k]

    return pl.pallas_call(
        kern, name=name, grid=(C // LANES, B),
        in_specs=[pl.BlockSpec((1, L, LANES), lambda c, b_: (b_, 0, x_blk + c)),
                  pl.BlockSpec((1, L, LANES), lambda c, b_: (b_, 0, c)),
                  pl.BlockSpec((CONV_K, LANES), lambda c, b_: (0, c)), pl.BlockSpec((1, LANES), lambda c, b_: (0, c))],
        out_specs=[pl.BlockSpec((1, Lp, LANES), lambda c, b_: (b_, 0, c)),
                   pl.BlockSpec((SUBLANES, LANES), lambda c, b_: (0, c))],
        out_shape=[jax.ShapeDtypeStruct((B, Lp, C), F32), jax.ShapeDtypeStruct((SUBLANES, C), F32)],
        compiler_params=_cparams(("parallel", "arbitrary")),
    )(x3, du, w, b)


def _conv_bwd_in(dpp, w, name):
    B, Lp, C = dpp.shape
    L = Lp - PAD
    TR = _tile(L, 384, 16)

    def kern(d_ref, w_ref, o_ref):
        def body(i, carry):
            r0 = pl.multiple_of(i * TR, TR)
            ext = d_ref[0, pl.ds(r0, TR + PAD), :]
            acc = jnp.zeros((TR, LANES), F32)
            for k in range(CONV_K):
                s = CONV_K - 1 - k
                sh = ext if s == 0 else pltpu.roll(ext, TR + PAD - s, 0)
                acc = acc + w_ref[k:k + 1, :] * sh[0:TR]
            o_ref[0, pl.ds(r0, TR), :] = acc.astype(BF16)
            return carry

        lax.fori_loop(0, L // TR, body, 0)

    return pl.pallas_call(
        kern, name=name, grid=(B, C // LANES),
        in_specs=[pl.BlockSpec((1, Lp, LANES), lambda b_, c: (b_, 0, c)),
                  pl.BlockSpec((CONV_K, LANES), lambda b_, c: (0, c))],
        out_specs=pl.BlockSpec((1, L, LANES), lambda b_, c: (b_, 0, c)),
        out_shape=jax.ShapeDtypeStruct((B, L, C), BF16),
        compiler_params=_cparams(("parallel", "parallel")),
    )(dpp, w)


def _dot_nt(a, b):
    return lax.dot_general(a, b, (((1,), (1,)), ((), ())), preferred_element_type=F32)


def _dot_tn(a, b):
    return lax.dot_general(a, b, (((0,), (0,)), ((), ())), preferred_element_type=F32)


def _dot(a, b):
    return jnp.dot(a, b, preferred_element_type=F32)


def _ssd_specs(L, nc, b_blk, c_blk):
    pairs_per_group = HEADS // SSD_GROUPS // HP
    return [
        pl.BlockSpec((1, L, LANES), lambda b, h: (b, 0, h)),
        pl.BlockSpec((1, L, SSD_STATE), lambda b, h: (b, 0, b_blk + h // pairs_per_group)),
        pl.BlockSpec((1, L, SSD_STATE), lambda b, h: (b, 0, c_blk + h // pairs_per_group)),
        pl.BlockSpec((1, L, LANES), lambda b, h: (b, 0, 0)),
        pl.BlockSpec((1, L, LANES), lambda b, h: (b, 0, 0)),
        pl.BlockSpec((1, HP, nc, Q_BLOCK), lambda b, h: (b, HEADS // HP + h, 0, 0)),
        pl.BlockSpec((1, LANES), lambda b, h: (0, 0)),
    ]


def _halves(a, b, shape):
    return jnp.where(lax.broadcasted_iota(jnp.int32, shape, 1) < HEAD_DIM, a, b)


def _half_sums(t):
    first = lax.broadcasted_iota(jnp.int32, t.shape, 1) < HEAD_DIM
    lo = jnp.sum(jnp.where(first, t, 0.0), axis=1, keepdims=True)
    return lo, jnp.sum(t, axis=1, keepdims=True) - lo


def _ssd_chunk(c, S, x_ref, b_ref, c_ref, v_ref, cu_ref, ct_ref, lane0):
    Q = Q_BLOCK
    rows = pl.ds(pl.multiple_of(c * Q, Q), Q)
    x = x_ref[0, rows, :]
    Bb = b_ref[0, rows, :].astype(BF16)
    Cb = c_ref[0, rows, :].astype(BF16)
    vt, ct = v_ref[0, rows, :], cu_ref[0, rows, :]
    tri = _lower_tri(Q)
    A, Lm, e_end_h, eAend_h, dts = [], [], [], [], []
    for hh in range(HP):
        dts.append(_lane_col(vt, lane0 + hh))
        A.append(_lane_col(ct, lane0 + hh))
        Ar = ct_ref[0, hh, pl.ds(c, 1), :]
        Aend = _lane_col(Ar, Q - 1)
        Lm.append(jnp.exp(jnp.where(tri, A[hh] - Ar, NEG)))
        e_end_h.append(jnp.exp(Aend - A[hh]))
        eAend_h.append(jnp.exp(Aend))
    shape = (Q, LANES)
    dt = _halves(dts[0], dts[1], shape)
    eA = _halves(jnp.exp(A[0]), jnp.exp(A[1]), shape)
    e_end = _halves(e_end_h[0], e_end_h[1], shape)
    xdt = x * dt
    CB = _dot_nt(Cb, Bb)
    W = xdt * e_end
    srow = lax.broadcasted_iota(jnp.int32, (HP * HEAD_DIM, 1), 0) < HEAD_DIM
    eAend = jnp.where(srow, eAend_h[0], eAend_h[1])
    S_new = S * eAend + _dot_tn(W.astype(BF16), Bb)
    return dict(rows=rows, x=x, Bb=Bb, Cb=Cb, dt=dt, eA=eA, e_end=e_end, e_end_h=e_end_h, eAend=eAend,
                eAend_h=eAend_h, xdt=xdt, Lm=Lm, CB=CB, W=W, S_new=S_new)


def _ssd_fwd(u, b_blk, c_blk, vals, cums, cums_t, dvec, name):
    B, L, _ = u.shape
    nc = L // Q_BLOCK
    nh = HEADS // HP

    def kern(x_ref, b_ref, c_ref, v_ref, cu_ref, ct_ref, d_ref, y_ref):
        lane0 = HEADS + HP * pl.program_id(1)
        dskip = _halves(_lane_col(d_ref[...], lane0), _lane_col(d_ref[...], lane0 + 1), (1, LANES))
        first = lax.broadcasted_iota(jnp.int32, (Q_BLOCK, LANES), 1) < HEAD_DIM

        def body(c, S):
            q = _ssd_chunk(c, S, x_ref, b_ref, c_ref, v_ref, cu_ref, ct_ref, lane0)
            xb = q["xdt"].astype(BF16)
            yd = jnp.where(first, _dot((q["CB"] * q["Lm"][0]).astype(BF16), xb),
                           _dot((q["CB"] * q["Lm"][1]).astype(BF16), xb))
            z = _dot_nt(q["Cb"], S.astype(BF16))
            y_ref[0, q["rows"], :] = yd + z * q["eA"] + dskip * q["x"]
            return q["S_new"]

        lax.fori_loop(0, nc, body, jnp.zeros((HP * HEAD_DIM, SSD_STATE), F32))

    return pl.pallas_call(
        kern, name=name, grid=(B, nh), in_specs=_ssd_specs(L, nc, b_blk, c_blk),
        out_specs=pl.BlockSpec((1, L, LANES), lambda b, h: (b, 0, h)),
        out_shape=jax.ShapeDtypeStruct((B, L, HEADS * HEAD_DIM), F32),
        compiler_params=_cparams(("parallel", "arbitrary")),
    )(u, u, u, vals, cums, cums_t, dvec)


def _ssd_bwd(u, b_blk, c_blk, vals, cums, cums_t, dvec, dy, name):
    B, L, _ = u.shape
    Q = Q_BLOCK
    nc = L // Q
    N = SSD_STATE
    nh = HEADS // HP
    pairs_per_group = HEADS // SSD_GROUPS // HP
    PP = HP * HEAD_DIM

    def kern(x_ref, b_ref, c_ref, v_ref, cu_ref, ct_ref, d_ref, dy_ref,
             dx_ref, dB_ref, dC_ref, ddt_ref, dAc_ref, dAr_ref, dD_ref, s_all):
        b = pl.program_id(0)
        h = pl.program_id(1)
        lane0 = HEADS + HP * h
        dskip = _halves(_lane_col(d_ref[...], lane0), _lane_col(d_ref[...], lane0 + 1), (1, LANES))
        lane_row = lax.broadcasted_iota(jnp.int32, (1, LANES), 1)
        onehot = [(lane_row == lane0 + hh).astype(F32) for hh in range(HP)]

        @pl.when(h % pairs_per_group == 0)
        def _():
            dB_ref[...] = jnp.zeros_like(dB_ref)
            dC_ref[...] = jnp.zeros_like(dC_ref)

        @pl.when(h == 0)
        def _():
            ddt_ref[...] = jnp.zeros_like(ddt_ref)
            dAc_ref[...] = jnp.zeros_like(dAc_ref)

        @pl.when((b == 0) & (h == 0))
        def _():
            dD_ref[...] = jnp.zeros_like(dD_ref)

        def fwd(c, S):
            s_all[c] = S
            return _ssd_chunk(c, S, x_ref, b_ref, c_ref, v_ref, cu_ref, ct_ref, lane0)["S_new"]

        lax.fori_loop(0, nc, fwd, jnp.zeros((PP, N), F32))
        last_row = lax.broadcasted_iota(jnp.int32, (Q, 1), 0) == Q - 1
        first = lax.broadcasted_iota(jnp.int32, (Q, LANES), 1) < HEAD_DIM
        srow = lax.broadcasted_iota(jnp.int32, (PP, 1), 0) < HEAD_DIM

        def bwd(i, carry):
            dS, dD = carry
            c = nc - 1 - i
            S = s_all[c]
            q = _ssd_chunk(c, S, x_ref, b_ref, c_ref, v_ref, cu_ref, ct_ref, lane0)
            rows, x, Bb, Cb, xdt, Lm, CB = q["rows"], q["x"], q["Bb"], q["Cb"], q["xdt"], q["Lm"], q["CB"]
            dy = dy_ref[0, rows, :]
            dyb = dy.astype(BF16)
            xb = xdt.astype(BF16)
            Sb = S.astype(BF16)
            dD = dD + _colsum(dy * x)
            dyh = [jnp.where(first, dy, 0.0).astype(BF16), jnp.where(first, 0.0, dy).astype(BF16)]
            dM = [_dot_nt(dyh[hh], xb) for hh in range(HP)]
            dxdt = jnp.where(first, _dot_tn((CB * Lm[0]).astype(BF16), dyb), _dot_tn((CB * Lm[1]).astype(BF16), dyb))
            dCBb = (dM[0] * Lm[0] + dM[1] * Lm[1]).astype(BF16)
            dAc, dAr = [], []
            for hh in range(HP):
                G = dM[hh] * CB * Lm[hh]
                dAc.append(jnp.sum(G, axis=1, keepdims=True))
                dAr.append(-jnp.sum(G, axis=0, keepdims=True))
            dC = _dot(dCBb, Bb)
            dBm = _dot_tn(dCBb, Cb)
            z = _dot_nt(Cb, Sb)
            zs = _half_sums(dy * z)
            dzb = (dy * q["eA"]).astype(BF16)
            dC = dC + _dot(dzb, Sb)
            dS_in = _dot_tn(dzb, Cb)
            dSb = dS.astype(BF16)
            dW = _dot_nt(Bb, dSb)
            dBm = dBm + _dot(q["W"].astype(BF16), dSb)
            dxdt = dxdt + dW * q["e_end"]
            des = _half_sums(dW * xdt)
            ss = jnp.sum(dS * S, axis=1, keepdims=True)
            ss_lo = jnp.sum(jnp.where(srow, ss, 0.0), axis=0, keepdims=True)
            ss_h = [ss_lo, jnp.sum(ss, axis=0, keepdims=True) - ss_lo]
            ddts = _half_sums(dxdt * x)
            eA_h = [_lane_col(q["eA"], 0), _lane_col(q["eA"], HEAD_DIM)]
            dAc_tile = jnp.zeros((Q, LANES), F32)
            ddt_tile = jnp.zeros((Q, LANES), F32)
            for hh in range(HP):
                de = des[hh] * q["e_end_h"][hh]
                dAend = ss_h[hh] * q["eAend_h"][hh] + jnp.sum(de, axis=0, keepdims=True)
                col = dAc[hh] + zs[hh] * eA_h[hh] - de + jnp.where(last_row, dAend, 0.0)
                dAc_tile = dAc_tile + col * onehot[hh]
                ddt_tile = ddt_tile + ddts[hh] * onehot[hh]
                dAr_ref[0, hh, pl.ds(c, 1), :] = dAr[hh]
            dx_ref[0, rows, :] = dskip * dy + dxdt * q["dt"]
            dB_ref[0, 0, rows, :] += dBm
            dC_ref[0, 0, rows, :] += dC
            ddt_ref[0, rows, :] += ddt_tile
            dAc_ref[0, rows, :] += dAc_tile
            return dS * q["eAend"] + dS_in, dD

        _, dD = lax.fori_loop(0, nc, bwd, (jnp.zeros((PP, N), F32), jnp.zeros((1, LANES), F32)))
        dlo, dhi = _half_sums(dD)
        dD_ref[...] += dlo * onehot[0] + dhi * onehot[1]

    tm = pl.BlockSpec((1, L, LANES), lambda b, h: (b, 0, 0))
    grp = pl.BlockSpec((1, 1, L, N), lambda b, h: (b, h // pairs_per_group, 0, 0))
    xs = pl.BlockSpec((1, L, LANES), lambda b, h: (b, 0, h))
    return pl.pallas_call(
        kern, name=name, grid=(B, nh), in_specs=_ssd_specs(L, nc, b_blk, c_blk) + [xs],
        out_specs=[xs, grp, grp, tm, tm, pl.BlockSpec((1, HP, nc, Q), lambda b, h: (b, h, 0, 0)),
                   pl.BlockSpec((1, LANES), lambda b, h: (0, 0))],
        out_shape=[jax.ShapeDtypeStruct((B, L, HEADS * HEAD_DIM), F32), jax.ShapeDtypeStruct((B, SSD_GROUPS, L, N), F32),
                   jax.ShapeDtypeStruct((B, SSD_GROUPS, L, N), F32), jax.ShapeDtypeStruct((B, L, LANES), F32),
                   jax.ShapeDtypeStruct((B, L, LANES), F32), jax.ShapeDtypeStruct((B, HEADS, nc, Q), F32),
                   jax.ShapeDtypeStruct((1, LANES), F32)],
        scratch_shapes=[pltpu.VMEM((nc, PP, N), F32)],
        compiler_params=_cparams(("arbitrary", "arbitrary")),
    )(u, u, u, vals, cums, cums_t, dvec, dy)


LRU_TR = 384
LRU_CB = 512


def _lru_gates(xc, ra, ix, p_ref, first):
    r = _sigmoid(ra + p_ref[0:1, :])
    i = _sigmoid(ix + p_ref[1:2, :])
    ls = _log_sigmoid(p_ref[2:3, :])
    log_a = LRU_C * r * ls
    a = jnp.exp(log_a)
    mult0 = jnp.sqrt(_one_minus_exp(2.0 * log_a))
    mult = jnp.where(first, 1.0, mult0)
    return r, i, ls, a, mult0, mult


def _lru_fwd(u, xc_off, ra, ix, proj3, gate_off, pvec, name):
    B, L, D = ra.shape
    TR, CB = _tile(L, LRU_TR, 8), LRU_CB
    nrt = L // TR

    def kern(xc_ref, ra_ref, ix_ref, g_ref, p_ref, y_ref, hs_ref, a_ref, pa_s, pu_s, carry):
        rt = pl.program_id(2)

        @pl.when(rt == 0)
        def _():
            carry[...] = jnp.zeros_like(carry)

        row = lax.broadcasted_iota(jnp.int32, (TR, 1), 0)
        first = (rt == 0) & (row == 0)
        xc = xc_ref[0]
        r, i, ls, a, mult0, mult = _lru_gates(xc, ra_ref[0], ix_ref[0], p_ref, first)
        a_ref[0] = a
        pa, pu = a, mult * (i * xc)
        sub = row % SUBLANES
        for s in (1, 2, 4):
            ok = sub >= s
            pu = jnp.where(ok, pa * pltpu.roll(pu, s, 0) + pu, pu)
            pa = jnp.where(ok, pa * pltpu.roll(pa, s, 0), pa)
        pa_s[...] = pa
        pu_s[...] = pu
        row8 = lax.broadcasted_iota(jnp.int32, (SUBLANES, 1), 0)

        def gbody(g, c):
            r8 = pl.ds(pl.multiple_of(g * SUBLANES, SUBLANES), SUBLANES)
            hg = pa_s[r8, :] * c + pu_s[r8, :]
            hs_ref[0, r8, :] = hg
            return jnp.sum(jnp.where(row8 == SUBLANES - 1, hg, 0.0), axis=0, keepdims=True)

        carry[...] = lax.fori_loop(0, TR // SUBLANES, gbody, carry[...])
        y_ref[0] = (hs_ref[0] * _gelu(g_ref[0])).astype(BF16)

    def win(off):
        assert off % CB == 0
        return pl.BlockSpec((1, TR, CB), functools.partial(lambda b, j, t, o: (b, t, j + o), o=off // CB))

    return pl.pallas_call(
        kern, name=name, grid=(B, D // CB, nrt),
        in_specs=[win(xc_off), win(0), win(0), win(gate_off), pl.BlockSpec((SUBLANES, CB), lambda b, j, t: (0, j))],
        out_specs=[win(0)] * 3,
        out_shape=[jax.ShapeDtypeStruct((B, L, D), BF16), jax.ShapeDtypeStruct((B, L, D), F32),
                   jax.ShapeDtypeStruct((B, L, D), F32)],
        scratch_shapes=[pltpu.VMEM((TR, CB), F32), pltpu.VMEM((TR, CB), F32), pltpu.VMEM((1, CB), F32)],
        compiler_params=_cparams(("parallel", "parallel", "arbitrary")),
    )(u, ra, ix, proj3, pvec)


def _lru_bwd(dy, proj3, gate_off, hs, a, u, xc_off, ra, ix, pvec, name):
    B, L, D = ra.shape
    TR, CB = _tile(L, LRU_TR, 8), LRU_CB
    nrt = L // TR

    def kern(dy_ref, g_ref, hs_ref, hsp_ref, a_ref, an_ref, xc_ref, ra_ref, ix_ref, p_ref,
             dg_ref, dra_ref, dix_ref, dxc_ref, dp_ref, pb_s, pd_s, g_s, carry):
        b = pl.program_id(1)
        rt = pl.program_id(2)
        t = nrt - 1 - rt

        @pl.when((b == 0) & (rt == 0))
        def _():
            dp_ref[...] = jnp.zeros_like(dp_ref)

        @pl.when(rt == 0)
        def _():
            carry[...] = jnp.zeros_like(carry)

        row = lax.broadcasted_iota(jnp.int32, (TR, 1), 0)
        gate, hsv, av, dyv = g_ref[0], hs_ref[0], a_ref[0], dy_ref[0]
        dg_ref[0] = (dyv * hsv * _dgelu(gate)).astype(BF16)
        a_next = jnp.where(t == nrt - 1, 0.0, an_ref[0, 0:1, :])
        pb = jnp.where(row == TR - 1, a_next, pltpu.roll(av, TR - 1, 0))
        pd = dyv * _gelu(gate)
        sub = row % SUBLANES
        for s in (1, 2, 4):
            ok = sub < SUBLANES - s
            pd = jnp.where(ok, pd + pb * pltpu.roll(pd, TR - s, 0), pd)
            pb = jnp.where(ok, pb * pltpu.roll(pb, TR - s, 0), pb)
        pb_s[...] = pb
        pd_s[...] = pd
        row8 = lax.broadcasted_iota(jnp.int32, (SUBLANES, 1), 0)

        def gbody(i, c):
            r8 = pl.ds(pl.multiple_of((TR // SUBLANES - 1 - i) * SUBLANES, SUBLANES), SUBLANES)
            gg = pd_s[r8, :] + pb_s[r8, :] * c
            g_s[r8, :] = gg
            return jnp.sum(jnp.where(row8 == 0, gg, 0.0), axis=0, keepdims=True)

        carry[...] = lax.fori_loop(0, TR // SUBLANES, gbody, carry[...])
        gv = g_s[...]
        h_first = jnp.where(t == 0, 0.0, hsp_ref[0, TR - 1:TR, :])
        hprev = jnp.where(row == 0, h_first, pltpu.roll(hsv, 1, 0))
        first = (t == 0) & (row == 0)
        xc = xc_ref[0]
        r, i, ls, a2, mult0, mult = _lru_gates(xc, ra_ref[0], ix_ref[0], p_ref, first)
        dxc_ref[0] = gv * mult * i
        dlog_a = gv * hprev * av + jnp.where(first, 0.0, gv * i * xc * (-(av * av) / mult0))
        dra = dlog_a * LRU_C * ls * r * (1.0 - r)
        dix = gv * mult * xc * i * (1.0 - i)
        dra_ref[0] = dra.astype(BF16)
        dix_ref[0] = dix.astype(BF16)
        dp_ref[0:1, :] += _colsum(dra)
        dp_ref[1:2, :] += _colsum(dix)
        dp_ref[2:3, :] += _colsum(dlog_a * LRU_C * r) * _sigmoid(-p_ref[2:3, :])

    def win(off, shift=0):
        assert off % CB == 0
        o = off // CB
        return pl.BlockSpec((1, TR, CB), lambda j, b, rt: (b, jnp.clip(nrt - 1 - rt + shift, 0, nrt - 1), j + o))

    return pl.pallas_call(
        kern, name=name, grid=(D // CB, B, nrt),
        in_specs=[win(0), win(gate_off), win(0), win(0, -1), win(0), win(0, 1), win(xc_off), win(0), win(0),
                  pl.BlockSpec((SUBLANES, CB), lambda j, b, rt: (0, j))],
        out_specs=[win(0)] * 4 + [pl.BlockSpec((SUBLANES, CB), lambda j, b, rt: (0, j))],
        out_shape=[jax.ShapeDtypeStruct((B, L, D), BF16)] * 3 + [jax.ShapeDtypeStruct((B, L, D), F32),
                                                                 jax.ShapeDtypeStruct((SUBLANES, D), F32)],
        scratch_shapes=[pltpu.VMEM((TR, CB), F32)] * 3 + [pltpu.VMEM((1, CB), F32)],
        compiler_params=_cparams(("parallel", "arbitrary", "arbitrary")),
    )(dy, proj3, hs, hs, a, a, u, ra, ix, pvec)


def _sum8(parts, name):
    _, R, C = parts.shape
    tr = _tile(R, 1024, ROW_ALIGN if parts.dtype.itemsize == 2 else SUBLANES)

    def kern(p_ref, o_ref):
        acc = p_ref[0].astype(F32)
        for d in range(1, N_DEV):
            acc = acc + p_ref[d].astype(F32)
        o_ref[...] = acc

    return pl.pallas_call(
        kern, name=name, grid=(R // tr,), in_specs=[pl.BlockSpec((N_DEV, tr, C), lambda i: (0, i, 0))],
        out_specs=pl.BlockSpec((tr, C), lambda i: (i, 0)), out_shape=jax.ShapeDtypeStruct((R, C), F32),
        compiler_params=_cparams(("parallel",)),
    )(parts)


def _adamw(w, g, m, v, name):
    shape = w.shape
    C = shape[-1] if w.ndim > 1 else shape[0]
    R = w.size // C
    w2, g2, m2, v2 = (t.reshape(R, C) for t in (w, g, m, v))
    tr = R
    for cand in range(8, min(R, 512) + 1, 8):
        if R % cand == 0:
            tr = cand

    def kern(w_ref, g_ref, m_ref, v_ref, d_ref, nm_ref, nv_ref):
        gv = g_ref[...]
        nm = ADAM_B1 * m_ref[...] + (1.0 - ADAM_B1) * gv
        nv = ADAM_B2 * v_ref[...] + (1.0 - ADAM_B2) * (gv * gv)
        m_hat = nm / (1.0 - ADAM_B1 ** ADAM_STEP)
        v_hat = nv / (1.0 - ADAM_B2 ** ADAM_STEP)
        d_ref[...] = -ADAM_LR * (m_hat / (jnp.sqrt(v_hat) + ADAM_EPS) + ADAM_WD * w_ref[...])
        nm_ref[...] = nm
        nv_ref[...] = nv

    spec = pl.BlockSpec((tr, C), lambda i: (i, 0))
    outs = pl.pallas_call(
        kern, name=name, grid=(R // tr,), in_specs=[spec] * 4, out_specs=[spec] * 3,
        out_shape=[jax.ShapeDtypeStruct((R, C), F32)] * 3, compiler_params=_cparams(("parallel",)),
    )(w2, g2, m2, v2)
    return tuple(o.reshape(shape) for o in outs)


MESH_ID = pl.DeviceIdType.MESH
ANY = pl.BlockSpec(memory_space=pl.ANY)
N_COPIES = N_DEV - 1
COMM_SCRATCH = [pltpu.SemaphoreType.DMA((N_COPIES,)), pltpu.SemaphoreType.DMA((N_COPIES,)), pltpu.SemaphoreType.DMA]


def _my_place():
    return lax.axis_index("x"), lax.axis_index("y"), lax.axis_index("c")


def _gather_phases(x_ref, out_ref, send_sems, recv_sems, local_sem):
    x, y, c = _my_place()
    me, sibling = (x, y, c), (x, y, 1 - c)
    chips = [(1 - x, y), (x, 1 - y), (1 - x, 1 - y)]

    def slab(px, py, pc):
        return out_ref.at[4 * px + 2 * py + pc]

    def copy(k, block, to, src=None):
        return pltpu.make_async_remote_copy(
            src_ref=slab(*block) if src is None else src, dst_ref=slab(*block),
            send_sem=send_sems.at[k], recv_sem=recv_sems.at[k], device_id=to, device_id_type=MESH_ID)

    mine = pltpu.make_async_copy(x_ref, slab(*me), local_sem)
    first = [copy(0, me, sibling, src=x_ref)] + [copy(1 + j, me, (*chip, c), src=x_ref) for j, chip in enumerate(chips)]
    passed = [copy(4 + j, (*chip, c), sibling) for j, chip in enumerate(chips)]

    def start():
        mine.start()
        for cp in first:
            cp.start()

    def forward():
        for j, chip in enumerate(chips):
            copy(1 + j, (*chip, c), me).wait_recv()
            passed[j].start()

    def finish():
        copy(0, sibling, me).wait_recv()
        for j, chip in enumerate(chips):
            copy(4 + j, (*chip, 1 - c), me).wait_recv()
        for cp in first + passed:
            cp.wait_send()
        mine.wait()

    return start, forward, finish


def _exchange_phases(p_ref, out_ref, send_sems, recv_sems, local_sem):
    x, y, c = _my_place()
    my_idx = 4 * x + 2 * y + c
    mine = pltpu.make_async_copy(p_ref.at[my_idx], out_ref.at[my_idx], local_sem)
    copies = []
    for k in range(1, N_DEV):
        px, py, pc = x ^ (k >> 2), y ^ ((k >> 1) & 1), c ^ (k & 1)
        copies.append(pltpu.make_async_remote_copy(
            src_ref=p_ref.at[4 * px + 2 * py + pc], dst_ref=out_ref.at[my_idx],
            send_sem=send_sems.at[k - 1], recv_sem=recv_sems.at[k - 1], device_id=(px, py, pc),
            device_id_type=MESH_ID))

    def start():
        mine.start()
        for cp in copies:
            cp.start()

    def finish():
        for cp in copies:
            cp.wait()
        mine.wait()

    return start, finish


def _ride(lin, total, phases):
    assert total >= 3
    marks = [0, total - 1] if len(phases) == 2 else [0, total // 2, total - 1]
    for mark, phase in zip(marks, phases):
        pl.when(lin == mark)(phase)


def _all_gather(xs, name):
    R, C = xs.shape

    def body(x_ref, out_ref, send_sems, recv_sems, local_sem):
        for phase in _gather_phases(x_ref, out_ref, send_sems, recv_sems, local_sem):
            phase()

    return pl.pallas_call(
        body, name=name, out_shape=jax.ShapeDtypeStruct((N_DEV, R, C), xs.dtype), in_specs=[ANY], out_specs=ANY,
        scratch_shapes=COMM_SCRATCH,
    )(xs)


def _exchange(parts, name):
    def body(p_ref, out_ref, send_sems, recv_sems, local_sem):
        for phase in _exchange_phases(p_ref, out_ref, send_sems, recv_sems, local_sem):
            phase()

    return pl.pallas_call(
        body, name=name, out_shape=jax.ShapeDtypeStruct(parts.shape, parts.dtype), in_specs=[ANY], out_specs=ANY,
        scratch_shapes=COMM_SCRATCH,
    )(parts)


D_XBC_EXTRA = 2 * SSD_GROUPS * SSD_STATE
SMALL_W = LANES
ROW_ALIGN = 16


def _layout(D):
    d_xbc = D + D_XBC_EXTRA
    off = dict(qkv=0, z=3 * D, merge=4 * D, gate=7 * D, conv=8 * D, xr=8 * D + d_xbc, small=9 * D + d_xbc)
    off["n_all"] = off["small"] + SMALL_W
    off["d_xbc"] = d_xbc
    off["conv_c"] = d_xbc + D
    return off


def _w_in_map(D):
    lo = _layout(D)
    widths = [("q", D, 0), ("k", D, D), ("v", D, 2 * D), ("f", HEADS, lo["small"]), ("z", D, lo["z"]),
              ("xbc", lo["d_xbc"], lo["conv"]), ("dt", HEADS, lo["small"] + HEADS), ("xr", D, lo["xr"]),
              ("gate", D, lo["gate"]), ("merge", 3 * D, lo["merge"])]
    out, o = [], 0
    for _, w, mine in widths:
        out.append((o, w, mine))
        o += w
    return out


def _padded(c):
    return -(-c // ROW_ALIGN) * ROW_ALIGN


def _permute_rows(src, pieces, name):
    R, C = src.shape
    n_out = sum(n for _, n in pieces)

    def kern(x_ref, o_ref):
        o = 0
        for start, n in pieces:
            if start is None:
                o_ref[o:o + n, :] = jnp.zeros((n, LANES), src.dtype)
            else:
                o_ref[o:o + n, :] = x_ref[start:start + n, :]
            o += n

    return pl.pallas_call(
        kern, name=name, grid=(C // LANES,), in_specs=[pl.BlockSpec((R, LANES), lambda i: (0, i))],
        out_specs=pl.BlockSpec((n_out, LANES), lambda i: (0, i)), out_shape=jax.ShapeDtypeStruct((n_out, C), src.dtype),
        compiler_params=_cparams(("parallel",)),
    )(src)


def _reorder_rows(wt, D, c, name="reorder_w_in"):
    cp = _padded(c)
    lo = _layout(D)
    pieces = []
    for a, w, mine in sorted(_w_in_map(D), key=lambda t: t[2]):
        b = a + w
        while a < b:
            j = a // c
            e = min(b, (j + 1) * c)
            pieces.append((j * cp + a - j * c, e - a))
            a = e
    pieces.append((None, lo["n_all"] - lo["small"] - 2 * HEADS))
    return _permute_rows(wt, pieces, name)


def _restore_rows(dwt, D, c, name="restore_w_in"):
    cp = _padded(c)
    segs = _w_in_map(D)
    pieces = []
    for j in range(N_DEV):
        a, b = j * c, (j + 1) * c
        for s0, w, mine in segs:
            lo_, hi_ = max(a, s0), min(b, s0 + w)
            if lo_ < hi_:
                pieces.append((mine + lo_ - s0, hi_ - lo_))
        if cp > c:
            pieces.append((None, cp - c))
    return _permute_rows(dwt, pieces, name)


def _block_diag(w):
    H, n, _ = w.shape
    tiled = jnp.tile(w.reshape(H * n, n), (1, H))
    r = lax.broadcasted_iota(jnp.int32, (H * n, H * n), 0) // n
    c = lax.broadcasted_iota(jnp.int32, (H * n, H * n), 1) // n
    return jnp.where(r == c, tiled, jnp.zeros_like(tiled))


def _diag_blocks(m, H):
    n = m.shape[0] // H
    return jnp.stack([m[h * n:(h + 1) * n, h * n:(h + 1) * n] for h in range(H)])


def _to_heads(t, B, L):
    return t.reshape(B, L, HEADS, HEAD_DIM).transpose(0, 2, 1, 3)


def _from_heads(t4):
    B, H, L, P = t4.shape
    return t4.transpose(0, 2, 1, 3).reshape(B * L, H * P)


def _rows_to_tm(rows):
    B, H, nc, Q = rows.shape
    return rows.reshape(B, H, nc * Q).transpose(0, 2, 1)


def _ffn_fwd(h, g, wgu_t, wd, tag):
    n = _norm_fwd(h, g, tag + "_norm")
    gu = _mm(n, wgu_t, tb=True, name=tag + "_up")
    act = _swiglu_fwd(gu, tag + "_act")
    out = _mm(act, wd, res=h, scale=0.5, name=tag + "_down")
    return out, (h, n, gu, act)


def _ffn_bwd(dh, saved, g, wgu_t, wd, tag):
    h, n, gu, act = saved
    dact = _mm(dh, wd, tb=True, scale=0.5, name=tag + "_down_dx")
    dwd = _mm(act, dh, ta=True, scale=0.5, name=tag + "_down_dw")
    dgu = _swiglu_bwd(gu, dact, tag + "_act_bwd")
    dwgu_t = _mm(dgu, n, ta=True, tn=1024, name=tag + "_up_dw")
    dn = _mm(dgu, wgu_t, name=tag + "_up_dx")
    dh_in, dg = _norm_bwd(h, dn, dh, g, tag + "_norm_bwd")
    return dh_in, dict(norm=dg, gu=dwgu_t, down=dwd)


def _mixer_fwd(h, p, B, L, gather=None):
    T, D = h.shape
    lo = _layout(D)
    n = _norm_fwd(h, p["gm"], "mix_norm")
    proj = _mm(n, p["w_all_t"], tb=True, name="mix_in")
    proj3 = proj.reshape(B, L, lo["n_all"])
    vals, cums = _gate_prep(proj3, lo["small"] // LANES, p["small_bias"], p["avec"], "gate_prep")
    cums_t = cums[..., :2 * HEADS].transpose(0, 2, 1).reshape(B, 2 * HEADS, L // Q_BLOCK, Q_BLOCK)
    qa, ka, va = _attn_pack(proj3, cums, "attn_pack")
    y_a3, lse, *gathered = _attn_fwd(qa, ka, va, "attn_fwd", gather)
    y_a = y_a3.reshape(T, D)
    u = _conv_fwd(proj3, lo["conv"] // LANES, p["conv_w"], p["conv_b"], lo["d_xbc"] // LANES, "conv_fwd")
    b_blk = D // LANES
    c_blk = b_blk + SSD_GROUPS * SSD_STATE // LANES
    y_s = _ssd_fwd(u, b_blk, c_blk, vals, cums, cums_t, p["dvec"], "ssd_fwd").reshape(T, D)
    yb = _gnorm_fwd(y_s, proj, lo["z"], p["ssd_norm"], "gnorm_fwd")
    u2 = u.reshape(T, lo["conv_c"])
    ra = _mm(u2, p["wa"], a_off=(0, lo["d_xbc"]), dims=(T, D, D), tk=512, name="lru_ra")
    ix = _mm(u2, p["wx"], a_off=(0, lo["d_xbc"]), dims=(T, D, D), tk=512, name="lru_ix")
    yc, hs, a = _lru_fwd(u, lo["d_xbc"], ra.reshape(B, L, D), ix.reshape(B, L, D), proj3, lo["gate"], p["pvec"],
                         "lru_fwd")
    yc = yc.reshape(T, D)
    pa = _mm(y_a, p["wba"], name="branch_attn")
    pb = _mm(yb, p["wbs"], name="branch_ssd")
    pc = _mm(yc, p["wbl"], name="branch_lru")
    mixed = _merge_fwd(proj, lo["merge"], pa, pb, pc, "merge_fwd")
    out = _mm(mixed, p["wout"], res=h, name="mix_out")
    saved = dict(h=h, n=n, proj=proj, qa=qa, ka=ka, va=va, vals=vals, cums=cums, cums_t=cums_t, lse=lse, y_a=y_a,
                 u=u, y_s=y_s, yb=yb, ra=ra, ix=ix, yc=yc, hs=hs, a=a, pa=pa, pb=pb, pc=pc, mixed=mixed)
    return out, saved, (gathered[0] if gathered else None)


def _mixer_bwd(dh, s, p, B, L, parts=None):
    T, D = dh.shape
    lo = _layout(D)
    proj, u = s["proj"], s["u"]
    proj3 = proj.reshape(B, L, lo["n_all"])
    g = {}
    dmixed = _mm(dh, p["wout"], tb=True, name="mix_out_dx")
    g["wout"] = _mm(s["mixed"], dh, ta=True, name="mix_out_dw")
    dpa, dpb, dpc, dmerge = _merge_bwd(dmixed, proj, lo["merge"], s["pa"], s["pb"], s["pc"], "merge_bwd")
    dy_a = _mm(dpa, p["wba"], tb=True, name="branch_attn_dx")
    g["wba"] = _mm(s["y_a"], dpa, ta=True, name="branch_attn_dw")
    dyb = _mm(dpb, p["wbs"], tb=True, name="branch_ssd_dx")
    g["wbs"] = _mm(s["yb"], dpb, ta=True, name="branch_ssd_dw")
    dyc = _mm(dpc, p["wbl"], tb=True, name="branch_lru_dx")
    g["wbl"] = _mm(s["yc"], dpc, ta=True, name="branch_lru_dw")
    dgate, dra, dix, dxc, g["pvec"] = _lru_bwd(dyc.reshape(B, L, D), proj3, lo["gate"], s["hs"], s["a"], u, lo["d_xbc"],
                                               s["ra"].reshape(B, L, D), s["ix"].reshape(B, L, D), p["pvec"], "lru_bwd")
    dra, dix = dra.reshape(T, D), dix.reshape(T, D)
    u2 = u.reshape(T, lo["conv_c"])
    g["wa"] = _mm(u2, dra, ta=True, a_off=(0, lo["d_xbc"]), dims=(D, D, T), tm=512, name="lru_ra_dw")
    g["wx"] = _mm(u2, dix, ta=True, a_off=(0, lo["d_xbc"]), dims=(D, D, T), tm=512, name="lru_ix_dw")
    dxc = _mm(dra, p["wa"], tb=True, res=dxc.reshape(T, D), name="lru_ra_dx")
    dxc = _mm(dix, p["wx"], tb=True, res=dxc, name="lru_ix_dx")
    dy_s, dz, g["ssd_norm"] = _gnorm_bwd(dyb, s["y_s"], proj, lo["z"], p["ssd_norm"], "gnorm_bwd")
    b_blk = D // LANES
    c_blk = b_blk + SSD_GROUPS * SSD_STATE // LANES
    dxs, dBg, dCg, ddt_tm, dAc_tm, dAr, g["dvec"] = _ssd_bwd(u, b_blk, c_blk, s["vals"], s["cums"], s["cums_t"],
                                                             p["dvec"], dy_s.reshape(B, L, D), "ssd_bwd")
    grp = lambda t: t.transpose(0, 2, 1, 3).reshape(B, L, SSD_GROUPS * SSD_STATE)
    du = jnp.concatenate([dxs, grp(dBg), grp(dCg), dxc.reshape(B, L, D)], axis=-1)
    dpp, g["conv_wb"] = _conv_bwd_pre(proj3, lo["conv"] // LANES, du, p["conv_w"], p["conv_b"], lo["d_xbc"] // LANES,
                                      "conv_bwd_pre")
    dconv = _conv_bwd_in(dpp, p["conv_w"], "conv_bwd_in")
    dq3, dk3, dv3, dcq, dck, *recv = _attn_bwd(s["qa"], s["ka"], s["va"], s["y_a"].reshape(B, L, D),
                                               dy_a.reshape(B, L, D), s["lse"], "attn_bwd", parts)
    dc_tm = (dcq - dck)[..., 0].transpose(0, 2, 1)
    drow_tm = jnp.concatenate([dc_tm, _rows_to_tm(dAr), jnp.zeros((B, L, LANES - 2 * HEADS), F32)], axis=-1)
    dsmall, g["small_bias"], g["avec"] = _gate_post(drow_tm, dAc_tm, ddt_tm, proj3, lo["small"] // LANES, s["vals"],
                                                    p["small_bias"], p["avec"], "gate_post")
    dproj = jnp.concatenate([dq3.reshape(T, D), dk3.reshape(T, D), dv3.reshape(T, D), dz, dmerge, dgate.reshape(T, D), dconv.reshape(T, lo["conv_c"]),
                             dsmall.reshape(T, SMALL_W)], axis=1)
    g["w_all_t"] = _mm(dproj, s["n"], ta=True, tn=1024, name="mix_in_dw")
    dn = _mm(dproj, p["w_all_t"], name="mix_in_dx")
    dh_in, g["gm"] = _norm_bwd(s["h"], dn, dh, p["gm"], "mix_norm_bwd")
    return dh_in, g, (recv[0] if recv else None)


def _small_vec(a, b):
    return jnp.concatenate([a, b, jnp.zeros((LANES - 2 * HEADS,), F32)])[None, :]


def _layer_params(w):
    zeros16 = jnp.zeros((HEADS,), F32)
    pvec = jnp.concatenate([w["lru_b_a"][None], w["lru_b_x"][None], w["lru_lambda"][None],
                            jnp.zeros((SUBLANES - 3, w["lru_b_a"].shape[0]), F32)], axis=0)
    return dict(
        g1=w["ffn1_norm"][None], gu1=w["ffn1_w_gate_up"], d1=w["ffn1_w_down"],
        gm=w["mix_norm"][None], w_all_t=w["w_in"],
        small_bias=_small_vec(w["fox_forget_bias"], w["ssd_dt_bias"]),
        avec=_small_vec(zeros16, -jnp.exp(w["ssd_a_log"])), dvec=_small_vec(zeros16, w["ssd_d"]),
        conv_w=jnp.concatenate([w["ssd_conv_w"], w["lru_conv_w"]], axis=1),
        conv_b=jnp.concatenate([w["ssd_conv_b"], w["lru_conv_b"]])[None],
        ssd_norm=w["ssd_norm"][None],
        wa=_block_diag(w["lru_w_a"]).astype(BF16), wx=_block_diag(w["lru_w_x"]).astype(BF16), pvec=pvec,
        wba=w["w_branch_attn"], wbs=w["w_branch_ssd"], wbl=w["w_branch_lru"], wout=w["w_out"],
        g2=w["ffn2_norm"][None], gu2=w["ffn2_w_gate_up"], d2=w["ffn2_w_down"],
    )


def _layer_fwd(h, p, B, L, gather=None):
    h, s1 = _ffn_fwd(h, p["g1"], p["gu1"], p["d1"], "ffn1")
    h, sm, gathered = _mixer_fwd(h, p, B, L, gather)
    h, s2 = _ffn_fwd(h, p["g2"], p["gu2"], p["d2"], "ffn2")
    return h, (s1, sm, s2), gathered


def _layer_bwd(dh, saved, p, w, B, L, parts=None):
    s1, sm, s2 = saved
    D = dh.shape[1]
    d_xbc = D + D_XBC_EXTRA
    dh, f2 = _ffn_bwd(dh, s2, p["g2"], p["gu2"], p["d2"], "ffn2")
    dh, gm, recv = _mixer_bwd(dh, sm, p, B, L, parts)
    dh, f1 = _ffn_bwd(dh, s1, p["g1"], p["gu1"], p["d1"], "ffn1")
    sb, av = gm["small_bias"][0], gm["avec"][0]
    cw = gm["conv_wb"]
    grads = dict(
        ffn1_norm=f1["norm"][0], ffn1_w_gate_up=f1["gu"], ffn1_w_down=f1["down"],
        mix_norm=gm["gm"][0], w_in=gm["w_all_t"],
        fox_forget_bias=sb[:HEADS], ssd_conv_w=cw[:CONV_K, :d_xbc], ssd_conv_b=cw[CONV_K, :d_xbc],
        ssd_dt_bias=sb[HEADS:2 * HEADS], ssd_a_log=av[HEADS:2 * HEADS] * (-jnp.exp(w["ssd_a_log"])),
        ssd_d=gm["dvec"][0, HEADS:2 * HEADS], ssd_norm=gm["ssd_norm"][0],
        lru_conv_w=cw[:CONV_K, d_xbc:], lru_conv_b=cw[CONV_K, d_xbc:],
        lru_w_a=_diag_blocks(gm["wa"], HEADS), lru_b_a=gm["pvec"][0], lru_w_x=_diag_blocks(gm["wx"], HEADS),
        lru_b_x=gm["pvec"][1], lru_lambda=gm["pvec"][2],
        w_branch_attn=gm["wba"], w_branch_ssd=gm["wbs"], w_branch_lru=gm["wbl"], w_out=gm["wout"],
        ffn2_norm=f2["norm"][0], ffn2_w_gate_up=f2["gu"], ffn2_w_down=f2["down"],
    )
    return dh, grads, recv


LAYER_NAMES = ["ffn1_norm", "ffn1_w_gate_up", "ffn1_w_down", "mix_norm", "w_in", "fox_forget_bias", "ssd_conv_w",
               "ssd_conv_b", "ssd_dt_bias", "ssd_a_log", "ssd_d", "ssd_norm", "lru_conv_w", "lru_conv_b", "lru_w_a",
               "lru_b_a", "lru_w_x", "lru_b_x", "lru_lambda", "w_branch_attn", "w_branch_ssd", "w_branch_lru", "w_out",
               "ffn2_norm", "ffn2_w_gate_up", "ffn2_w_down"]
WEIGHT_NAMES = ["meta_tokens"] + LAYER_NAMES + ["final_norm"]


def _local_step(x, target, meta, final_norm, depth, layer_weights, pack_next=None, pack_grads=None):
    B, S, D = x.shape
    L = -(-(N_META + S) // Q_BLOCK) * Q_BLOCK
    h = jnp.concatenate([jnp.broadcast_to(meta[None], (B, N_META, D)), x,
                         jnp.zeros((B, L - N_META - S, D), F32)], axis=1).reshape(B * L, D)
    weights, params, saved = [], [], []
    gathered = None
    for l in range(depth):
        w = layer_weights(l, gathered)
        p = _layer_params(w)
        nxt = pack_next(l + 1) if (pack_next is not None and l + 1 < depth) else None
        h, s, gathered = _layer_fwd(h, p, B, L, nxt)
        weights.append(w)
        params.append(p)
        saved.append(s)
    tgt = jnp.pad(target, ((0, 0), (N_META, L - N_META - S), (0, 0))).reshape(B * L, D)
    dh, loss, dfinal = _loss_bwd(h, tgt, final_norm[None], L, S, "loss")
    grads = [None] * depth
    received, parts = {}, None
    for l in reversed(range(depth)):
        dh, grads[l], recv = _layer_bwd(dh, saved[l], params[l], weights[l], B, L, parts)
        if recv is not None:
            received[l + 1] = recv
        parts = pack_grads(grads[l]) if pack_grads is not None else None
    dh3 = dh.reshape(B, L, D)
    return (loss, dh3[:, N_META:N_META + S], jnp.sum(dh3[:, :N_META], axis=0), grads, dfinal[0], received, parts)


BIG_NAMES = ["ffn1_w_gate_up", "ffn1_w_down", "w_in", "w_branch_attn", "w_branch_ssd", "w_branch_lru", "w_out",
             "ffn2_w_gate_up", "ffn2_w_down"]
COL_SHARDED = {"ffn1_w_gate_up", "w_in", "ffn2_w_gate_up"}
SMALL_SHARDED = ["meta_tokens", "ssd_conv_w", "lru_conv_w"]
SMALL_NAMES = [n for n in LAYER_NAMES if n not in BIG_NAMES]


def _shard_rows(name, shape):
    return _padded(shape[1]) if name in COL_SHARDED else shape[0]


def _pack_shards(shards):
    rows = []
    for n in BIG_NAMES:
        s = shards[n]
        if n in COL_SHARDED:
            s = jnp.pad(s.T, ((0, _padded(s.shape[1]) - s.shape[1]), (0, 0)))
        rows.append(s)
    return jnp.concatenate(rows, axis=0)


def _unpack_gathered(gathered, shapes, D):
    out, o = {}, 0
    for n in BIG_NAMES:
        r = _shard_rows(n, shapes[n])
        out[n] = gathered[:, o:o + r].reshape(N_DEV * r, D)
        o += r
    out["w_in"] = _reorder_rows(out["w_in"], D, shapes["w_in"][1])
    return out


def _pack_full_grads(grads, shapes, D):
    slabs = []
    for n in BIG_NAMES:
        g = grads[n]
        if n == "w_in":
            g = _restore_rows(g, D, shapes[n][1])
        slabs.append(g.reshape(N_DEV, _shard_rows(n, shapes[n]), D))
    return jnp.concatenate(slabs, axis=1)


def _unpack_local(rows, shapes):
    out, o = {}, 0
    for n in BIG_NAMES:
        r = _shard_rows(n, shapes[n])
        blk = rows[o:o + r]
        out[n] = blk[:shapes[n][1]].T if n in COL_SHARDED else blk
        o += r
    return out


def _as_rows(flat):
    n = flat.shape[0]
    unit = LANES * SUBLANES
    total = -(-n // unit) * unit
    return jnp.pad(flat, (0, total - n)).reshape(total // LANES, LANES)


def _flatten_list(arrs):
    return _as_rows(jnp.concatenate([a.reshape(-1) for a in arrs]))


def _split_like(rows, shapes):
    flat = rows.reshape(-1)
    out, o = [], 0
    for s in shapes:
        n = math.prod(s)
        out.append(flat[o:o + n].reshape(s))
        o += n
    return out


def _gather_last(rows8, shape):
    lead, c = shape[:-1], shape[-1]
    t = rows8.reshape((N_DEV,) + tuple(lead) + (c,))
    return jnp.moveaxis(t, 0, -2).reshape(tuple(lead) + (N_DEV * c,))


def kernel(x, meta_tokens, ffn1_norm, ffn1_w_gate_up, ffn1_w_down, mix_norm, w_in, fox_forget_bias, ssd_conv_w, ssd_conv_b, ssd_dt_bias, ssd_a_log, ssd_d, ssd_norm, lru_conv_w, lru_conv_b, lru_w_a, lru_b_a, lru_w_x, lru_b_x, lru_lambda, w_branch_attn, w_branch_ssd, w_branch_lru, w_out, ffn2_norm, ffn2_w_gate_up, ffn2_w_down, final_norm, loss_target, m_meta_tokens, m_ffn1_norm, m_ffn1_w_gate_up, m_ffn1_w_down, m_mix_norm, m_w_in, m_fox_forget_bias, m_ssd_conv_w, m_ssd_conv_b, m_ssd_dt_bias, m_ssd_a_log, m_ssd_d, m_ssd_norm, m_lru_conv_w, m_lru_conv_b, m_lru_w_a, m_lru_b_a, m_lru_w_x, m_lru_b_x, m_lru_lambda, m_w_branch_attn, m_w_branch_ssd, m_w_branch_lru, m_w_out, m_ffn2_norm, m_ffn2_w_gate_up, m_ffn2_w_down, m_final_norm, v_meta_tokens, v_ffn1_norm, v_ffn1_w_gate_up, v_ffn1_w_down, v_mix_norm, v_w_in, v_fox_forget_bias, v_ssd_conv_w, v_ssd_conv_b, v_ssd_dt_bias, v_ssd_a_log, v_ssd_d, v_ssd_norm, v_lru_conv_w, v_lru_conv_b, v_lru_w_a, v_lru_b_a, v_lru_w_x, v_lru_b_x, v_lru_lambda, v_w_branch_attn, v_w_branch_ssd, v_w_branch_lru, v_w_out, v_ffn2_norm, v_ffn2_w_gate_up, v_ffn2_w_down, v_final_norm):
    weights = dict(zip(WEIGHT_NAMES, (meta_tokens, ffn1_norm, ffn1_w_gate_up, ffn1_w_down, mix_norm, w_in, fox_forget_bias, ssd_conv_w, ssd_conv_b, ssd_dt_bias, ssd_a_log, ssd_d, ssd_norm, lru_conv_w, lru_conv_b, lru_w_a, lru_b_a, lru_w_x, lru_b_x, lru_lambda, w_branch_attn, w_branch_ssd, w_branch_lru, w_out, ffn2_norm, ffn2_w_gate_up, ffn2_w_down, final_norm,)))
    mom1 = dict(zip(WEIGHT_NAMES, (m_meta_tokens, m_ffn1_norm, m_ffn1_w_gate_up, m_ffn1_w_down, m_mix_norm, m_w_in, m_fox_forget_bias, m_ssd_conv_w, m_ssd_conv_b, m_ssd_dt_bias, m_ssd_a_log, m_ssd_d, m_ssd_norm, m_lru_conv_w, m_lru_conv_b, m_lru_w_a, m_lru_b_a, m_lru_w_x, m_lru_b_x, m_lru_lambda, m_w_branch_attn, m_w_branch_ssd, m_w_branch_lru, m_w_out, m_ffn2_norm, m_ffn2_w_gate_up, m_ffn2_w_down, m_final_norm,)))
    mom2 = dict(zip(WEIGHT_NAMES, (v_meta_tokens, v_ffn1_norm, v_ffn1_w_gate_up, v_ffn1_w_down, v_mix_norm, v_w_in, v_fox_forget_bias, v_ssd_conv_w, v_ssd_conv_b, v_ssd_dt_bias, v_ssd_a_log, v_ssd_d, v_ssd_norm, v_lru_conv_w, v_lru_conv_b, v_lru_w_a, v_lru_b_a, v_lru_w_x, v_lru_b_x, v_lru_lambda, v_w_branch_attn, v_w_branch_ssd, v_w_branch_lru, v_w_out, v_ffn2_norm, v_ffn2_w_gate_up, v_ffn2_w_down, v_final_norm,)))
    depth = ffn1_norm.shape[0]
    D = x.shape[-1]
    my_idx = 4 * lax.axis_index("x") + 2 * lax.axis_index("y") + lax.axis_index("c")

    small_shapes = [weights[n].shape for n in SMALL_SHARDED]
    gathered = _all_gather(_flatten_list([weights[n] for n in SMALL_SHARDED]), "gather_small").reshape(N_DEV, -1)
    small_full, o = {}, 0
    for n, s in zip(SMALL_SHARDED, small_shapes):
        k = math.prod(s)
        small_full[n] = _gather_last(gathered[:, o:o + k], s)
        o += k

    shard_shapes = {n: weights[n].shape[1:] for n in BIG_NAMES}
    pack_next = lambda l: _pack_shards({n: weights[n][l].astype(BF16) for n in BIG_NAMES})

    def layer_weights(l, gathered):
        if gathered is None:
            gathered = _all_gather(pack_next(l), "gather_weights")
        w = _unpack_gathered(gathered, shard_shapes, D)
        for n in SMALL_NAMES:
            w[n] = small_full[n][l] if n in SMALL_SHARDED else weights[n][l]
        return w

    pack_grads = lambda g: _pack_full_grads(g, shard_shapes, D).astype(BF16)
    loss, dx, dmeta, grads, dfinal, received, parts = _local_step(
        x, loss_target, small_full["meta_tokens"], final_norm, depth, layer_weights, pack_next, pack_grads)
    received[0] = _exchange(parts, "exchange_grads")
    loss = lax.psum(loss[0, 0], ("x", "y", "c"))
    summed = {n: [] for n in WEIGHT_NAMES}
    for l in range(depth):
        local = _unpack_local(_sum8(received[l], "sum_grads"), shard_shapes)
        for n in BIG_NAMES:
            summed[n].append(local[n])

    small_list = [dmeta, dfinal] + [grads[l][n] for l in range(depth) for n in SMALL_NAMES]
    total = _sum8(_all_gather(_flatten_list(small_list), "gather_small_grads"), "sum_small_grads")
    parts = _split_like(total, [a.shape for a in small_list])
    full_small = {"meta_tokens": parts[0], "final_norm": parts[1]}
    for i, n in enumerate(SMALL_NAMES):
        full_small[n] = jnp.stack([parts[2 + l * len(SMALL_NAMES) + i] for l in range(depth)])
    grad = {}
    for n in WEIGHT_NAMES:
        if n in BIG_NAMES:
            grad[n] = jnp.stack(summed[n])
        elif n in SMALL_SHARDED:
            c = weights[n].shape[-1]
            grad[n] = lax.dynamic_slice_in_dim(full_small[n], my_idx * c, c, axis=full_small[n].ndim - 1)
        else:
            grad[n] = full_small[n]

    delta, new_m, new_v = {}, {}, {}
    for n in WEIGHT_NAMES:
        delta[n], new_m[n], new_v[n] = _adamw(weights[n], grad[n], mom1[n], mom2[n], "adamw_" + n)
    return (loss, dx, *[grad[n] for n in WEIGHT_NAMES], *[delta[n] for n in WEIGHT_NAMES],
            *[new_m[n] for n in WEIGHT_NAMES], *[new_v[n] for n in WEIGHT_NAMES])
```

```python
import functools
import math

import jax
import jax.numpy as jnp
from jax import lax
from jax.experimental import pallas as pl
from jax.experimental.pallas import tpu as pltpu

F32 = jnp.float32
BF16 = jnp.bfloat16

N_DEV = 8
N_META = 16
Q_BLOCK = 128
NORM_EPS = 1e-6
HEADS = 16
HEAD_DIM = 64
SSD_GROUPS = 2
SSD_STATE = 128
CONV_K = 4
LRU_C = 8.0
ADAM_LR, ADAM_B1, ADAM_B2, ADAM_EPS, ADAM_WD, ADAM_STEP = 0.001, 0.9, 0.999, 1e-08, 0.01, 10

LANES = 128
SUBLANES = 8
VMEM_LIMIT = 56 * 1024 * 1024
NEG = -1e30
MM_TILE = 1408
MM_VMEM = 40 * 1024 * 1024


def _cparams(sem=None):
    return pltpu.CompilerParams(dimension_semantics=sem, vmem_limit_bytes=VMEM_LIMIT)


def _tile(dim, target, mult=LANES):
    if dim <= target:
        return dim
    best = None
    for t in range(mult, target + 1, mult):
        if dim % t == 0:
            best = t
    assert best is not None, (dim, target)
    return best


def _sigmoid(x):
    return 1.0 / (1.0 + jnp.exp(-x))


def _log1p_exp_neg_abs(x):
    e = jnp.exp(-jnp.abs(x))
    u = 1.0 + e
    return jnp.where(u == 1.0, e, jnp.log(u) * (e / jnp.where(u == 1.0, 1.0, u - 1.0)))


def _log_sigmoid(x):
    return jnp.minimum(x, 0.0) - _log1p_exp_neg_abs(x)


def _softplus(x):
    return jnp.maximum(x, 0.0) + _log1p_exp_neg_abs(x)


def _one_minus_exp(y):
    u = jnp.exp(y)
    safe = jnp.where(u == 1.0, 0.5, u)
    return jnp.where(u == 1.0, -y, (1.0 - u) * y / jnp.log(safe))


def _silu(x):
    return x * _sigmoid(x)


def _dsilu(x):
    s = _sigmoid(x)
    return s * (1.0 + x * (1.0 - s))


_GELU_C = math.sqrt(2.0 / math.pi)


def _gelu(x):
    return 0.5 * x * (1.0 + jnp.tanh(_GELU_C * (x + 0.044715 * x * x * x)))


def _dgelu(x):
    t = jnp.tanh(_GELU_C * (x + 0.044715 * x * x * x))
    return 0.5 * (1.0 + t) + 0.5 * x * (1.0 - t * t) * _GELU_C * (1.0 + 3.0 * 0.044715 * x * x)


def _split3_dot(tri, x):
    hi = x.astype(BF16)
    r1 = x - hi.astype(F32)
    mid = r1.astype(BF16)
    lo = (r1 - mid.astype(F32)).astype(BF16)
    t = tri.astype(BF16)
    d = lambda p: jnp.dot(t, p, preferred_element_type=F32)
    return d(hi) + d(mid) + d(lo)


def _lower_tri(n, strict=False):
    r = lax.broadcasted_iota(jnp.int32, (n, n), 0)
    c = lax.broadcasted_iota(jnp.int32, (n, n), 1)
    return (c < r) if strict else (c <= r)


def _mm(a, b, *, ta=False, tb=False, out_dtype=F32, res=None, scale=None, tm=None, tn=None, tk=None,
        a_off=(0, 0), b_off=(0, 0), dims=None, name):
    if dims is None:
        M, K = (a.shape[1], a.shape[0]) if ta else a.shape
        N = b.shape[0] if tb else b.shape[1]
    else:
        M, N, K = dims
    tk = tk or _tile(K, 2816)
    nk_ = K // tk
    pick_m, pick_n = tm is None, tn is None
    tm = tm or _tile(M, MM_TILE)
    tn = tn or _tile(N, MM_TILE)

    def vmem(tm_, tn_):
        a_b = tm_ * tk * a.dtype.itemsize + (tm_ * tk * 2 if a.dtype != BF16 else 0)
        b_b = tn_ * tk * b.dtype.itemsize + (tn_ * tk * 2 if b.dtype != BF16 else 0)
        o_b = tm_ * tn_ * (jnp.dtype(out_dtype).itemsize + (4 if res is not None else 0))
        return 2 * (a_b + b_b + o_b) + (tm_ * tn_ * 4 if nk_ > 1 else 0) + tm_ * tn_ * 4

    while vmem(tm, tn) > MM_VMEM and (pick_m or pick_n):
        if pick_m and (tm >= tn or not pick_n) and tm > LANES:
            tm = _tile(M, tm - LANES)
        elif pick_n and tn > LANES:
            tn = _tile(N, tn - LANES)
        else:
            break
    assert M % tm == 0 and N % tn == 0 and K % tk == 0, (name, M, N, K, tm, tn, tk)
    nk = K // tk
    ca = 0 if ta else 1
    cb = 1 if tb else 0

    def blk(rows, cols, off):
        assert off[0] % rows == 0 and off[1] % cols == 0, (name, off, rows, cols)
        return off[0] // rows, off[1] // cols

    if ta:
        ao = blk(tk, tm, a_off)
        a_spec = pl.BlockSpec((tk, tm), lambda i, j, k: (k + ao[0], i + ao[1]))
    else:
        ao = blk(tm, tk, a_off)
        a_spec = pl.BlockSpec((tm, tk), lambda i, j, k: (i + ao[0], k + ao[1]))
    if tb:
        bo = blk(tn, tk, b_off)
        b_spec = pl.BlockSpec((tn, tk), lambda i, j, k: (j + bo[0], k + bo[1]))
    else:
        bo = blk(tk, tn, b_off)
        b_spec = pl.BlockSpec((tk, tn), lambda i, j, k: (k + bo[0], j + bo[1]))
    o_spec = pl.BlockSpec((tm, tn), lambda i, j, k: (i, j))
    in_specs = [a_spec, b_spec] + ([o_spec] if res is not None else [])
    has_res = res is not None

    def kern(*refs):
        if has_res:
            a_ref, b_ref, r_ref, o_ref = refs[:4]
            scr = refs[4:]
        else:
            a_ref, b_ref, o_ref = refs[:3]
            r_ref = None
            scr = refs[3:]
        p = lax.dot_general(a_ref[...].astype(BF16), b_ref[...].astype(BF16), (((ca,), (cb,)), ((), ())),
                            preferred_element_type=F32)

        def fin(val):
            if scale is not None:
                val = val * scale
            if has_res:
                val = r_ref[...] + val
            o_ref[...] = val.astype(out_dtype)

        if nk == 1:
            fin(p)
        else:
            acc = scr[0]
            k = pl.program_id(2)

            @pl.when(k == 0)
            def _():
                acc[...] = p

            @pl.when(k > 0)
            def _():
                acc[...] += p

            @pl.when(k == nk - 1)
            def _():
                fin(acc[...])

    args = (a, b) + ((res,) if has_res else ())
    return pl.pallas_call(
        kern, name=name, grid=(M // tm, N // tn, nk), in_specs=in_specs, out_specs=o_spec,
        out_shape=jax.ShapeDtypeStruct((M, N), out_dtype),
        scratch_shapes=[pltpu.VMEM((tm, tn), F32)] if nk > 1 else [],
        compiler_params=_cparams(("parallel", "parallel", "arbitrary")),
    )(*args)


def _rows(body, tiled, full, outs, accs, *, tr, name, T):
    assert T % tr == 0
    in_specs = []
    for arr, width, off in tiled:
        assert off % width == 0, (name, off, width)
        in_specs.append(pl.BlockSpec((tr, width), functools.partial(lambda i, o: (i, o), o=off // width)))
    for arr in full:
        in_specs.append(pl.BlockSpec(arr.shape, lambda i: (0, 0)))
    out_specs = [pl.BlockSpec((tr, w), lambda i: (i, 0)) for w, _ in outs]
    out_specs += [pl.BlockSpec(s, lambda i: (0, 0)) for s, _ in accs]
    out_shape = [jax.ShapeDtypeStruct((T, w), d) for w, d in outs] + [jax.ShapeDtypeStruct(s, d) for s, d in accs]
    nt, nf, no = len(tiled), len(full), len(outs)

    def kern(*refs):
        i = pl.program_id(0)
        acc_refs = refs[nt + nf + no:]

        @pl.when(i == 0)
        def _():
            for r in acc_refs:
                r[...] = jnp.zeros(r.shape, r.dtype)

        body(i, refs[:nt], refs[nt:nt + nf], refs[nt + nf:nt + nf + no], acc_refs)

    res = pl.pallas_call(
        kern, name=name, grid=(T // tr,), in_specs=in_specs, out_specs=out_specs, out_shape=out_shape,
        compiler_params=_cparams(("arbitrary",)),
    )(*[t[0] for t in tiled], *full)
    return res


def _colsum(x):
    return jnp.sum(x, axis=0, keepdims=True)


def _norm_fwd(h, g, name):
    T, D = h.shape

    def body(i, t, f, o, a):
        x = t[0][...]
        r = lax.rsqrt(jnp.mean(x * x, axis=-1, keepdims=True) + NORM_EPS)
        o[0][...] = (x * r * f[0][...]).astype(BF16)

    return _rows(body, [(h, D, 0)], [g], [(D, BF16)], [], tr=_tile(T, 768, 8), name=name, T=T)[0]


def _norm_bwd(h, dn, dh, g, name):
    T, D = h.shape

    def body(i, t, f, o, a):
        x, dnv, dhv = t[0][...], t[1][...], t[2][...]
        r = lax.rsqrt(jnp.mean(x * x, axis=-1, keepdims=True) + NORM_EPS)
        xh = x * r
        dng = dnv * f[0][...]
        out = dhv + r * (dng - xh * jnp.mean(dng * xh, axis=-1, keepdims=True))
        o[0][...] = out
        o[1][...] = out.astype(BF16)
        a[0][...] += _colsum(dnv * xh)

    return _rows(body, [(h, D, 0), (dn, D, 0), (dh, D, 0)], [g], [(D, F32), (D, BF16)], [((1, D), F32)],
                 tr=_tile(T, 384, 16), name=name, T=T)


def _swiglu_fwd(gu, name):
    T, F2 = gu.shape
    F = F2 // 2

    def body(i, t, f, o, a):
        o[0][...] = (_silu(t[0][...]) * t[1][...]).astype(BF16)

    return _rows(body, [(gu, F, 0), (gu, F, F)], [], [(F, BF16)], [], tr=_tile(T, 256, 8), name=name, T=T)[0]


def _swiglu_bwd(gu, da, name):
    T, F2 = gu.shape
    F = F2 // 2

    def body(i, t, f, o, a):
        gv, uv, dav = t[0][...], t[1][...], t[2][...]
        o[0][:, :F] = (dav * uv * _dsilu(gv)).astype(BF16)
        o[0][:, F:] = (dav * _silu(gv)).astype(BF16)

    return _rows(body, [(gu, F, 0), (gu, F, F), (da, F, 0)], [], [(F2, BF16)], [], tr=_tile(T, 256, 8),
                 name=name, T=T)[0]


def _merge_fwd(proj, off, pa, pb, pc, name):
    T, D = pa.shape

    def body(i, t, f, o, a):
        o[0][...] = (_sigmoid(t[0][...]) * t[3][...] + _sigmoid(t[1][...]) * t[4][...]
                     + _sigmoid(t[2][...]) * t[5][...]).astype(BF16)

    tiled = [(proj, D, off), (proj, D, off + D), (proj, D, off + 2 * D), (pa, D, 0), (pb, D, 0), (pc, D, 0)]
    return _rows(body, tiled, [], [(D, BF16)], [], tr=_tile(T, 384, 8), name=name, T=T)[0]


def _merge_bwd(dmixed, proj, off, pa, pb, pc, name):
    T, D = pa.shape

    def body(i, t, f, o, a):
        dm = t[0][...]
        for k in range(3):
            g = _sigmoid(t[1 + k][...])
            o[k][...] = (dm * g).astype(BF16)
            o[3][:, k * D:(k + 1) * D] = (dm * t[4 + k][...] * g * (1.0 - g)).astype(BF16)

    tiled = [(dmixed, D, 0), (proj, D, off), (proj, D, off + D), (proj, D, off + 2 * D), (pa, D, 0), (pb, D, 0),
             (pc, D, 0)]
    return _rows(body, tiled, [], [(D, BF16)] * 3 + [(3 * D, BF16)], [], tr=_tile(T, 384, 8), name=name, T=T)


def _gnorm_fwd(y, proj, zoff, nw, name):
    T, D = y.shape
    gs = D // SSD_GROUPS

    def body(i, t, f, o, a):
        s = t[0][...] * _silu(t[1][...])
        for g in range(SSD_GROUPS):
            sg = s[:, g * gs:(g + 1) * gs]
            r = lax.rsqrt(jnp.mean(sg * sg, axis=-1, keepdims=True) + NORM_EPS)
            o[0][:, g * gs:(g + 1) * gs] = (sg * r * f[0][:, g * gs:(g + 1) * gs]).astype(BF16)

    return _rows(body, [(y, D, 0), (proj, D, zoff)], [nw], [(D, BF16)], [], tr=_tile(T, 384, 8), name=name, T=T)[0]


def _gnorm_bwd(dout, y, proj, zoff, nw, name):
    T, D = y.shape
    gs = D // SSD_GROUPS

    def body(i, t, f, o, a):
        dov, yv, zv = t[0][...], t[1][...], t[2][...]
        sz = _silu(zv)
        s = yv * sz
        dsz = _dsilu(zv)
        for g in range(SSD_GROUPS):
            sl = slice(g * gs, (g + 1) * gs)
            sg = s[:, sl]
            r = lax.rsqrt(jnp.mean(sg * sg, axis=-1, keepdims=True) + NORM_EPS)
            sh = sg * r
            dog = dov[:, sl]
            dng = dog * f[0][:, sl]
            ds = r * (dng - sh * jnp.mean(dng * sh, axis=-1, keepdims=True))
            o[0][:, sl] = ds * sz[:, sl]
            o[1][:, sl] = (ds * yv[:, sl] * dsz[:, sl]).astype(BF16)
            a[0][:, sl] += _colsum(dog * sh)

    return _rows(body, [(dout, D, 0), (y, D, 0), (proj, D, zoff)], [nw], [(D, F32), (D, BF16)], [((1, D), F32)],
                 tr=_tile(T, 384, 8), name=name, T=T)


def _loss_bwd(h, tgt, g, seq_len, n_real, name):
    T, D = h.shape
    tr = _tile(seq_len, 384, 8)
    per_seq = seq_len // tr

    def body(i, t, f, o, a):
        x, tg = t[0][...], t[1][...]
        pos = (i % per_seq) * tr + lax.broadcasted_iota(jnp.int32, (tr, 1), 0)
        valid = (pos >= N_META) & (pos < N_META + n_real)
        r = lax.rsqrt(jnp.mean(x * x, axis=-1, keepdims=True) + NORM_EPS)
        xh = x * r
        e = jnp.where(valid, xh * f[0][...] - tg, 0.0)
        a[0][...] += jnp.zeros((1, LANES), F32) + 0.5 * jnp.sum(jnp.sum(e * e, axis=-1, keepdims=True) / D,
                                                              axis=0, keepdims=True)
        dy = e / D
        dng = dy * f[0][...]
        out = r * (dng - xh * jnp.mean(dng * xh, axis=-1, keepdims=True))
        o[0][...] = out
        o[1][...] = out.astype(BF16)
        a[1][...] += _colsum(dy * xh)

    return _rows(body, [(h, D, 0), (tgt, D, 0)], [g], [(D, F32), (D, BF16)], [((1, LANES), F32), ((1, D), F32)], tr=tr,
                 name=name, T=T)


def _lane_is_attn(shape):
    return lax.broadcasted_iota(jnp.int32, shape, len(shape) - 1) < HEADS


def _gate_prep(proj3, col_blk, bias, avec, name):
    B, L, _ = proj3.shape
    Q = Q_BLOCK
    nc = L // Q

    def kern(x_ref, b_ref, a_ref, v_ref, c_ref, carry):
        c = pl.program_id(1)

        @pl.when(c == 0)
        def _():
            carry[...] = jnp.zeros_like(carry)

        x = x_ref[0] + b_ref[...]
        attn = _lane_is_attn(x.shape)
        v = jnp.where(attn, _log_sigmoid(x), _softplus(x))
        w = jnp.where(attn, v, v * a_ref[...])
        cs = _split3_dot(_lower_tri(Q), w) + jnp.where(attn[:1], carry[...], 0.0)
        v_ref[0] = v
        c_ref[0] = cs
        rows = lax.broadcasted_iota(jnp.int32, (Q, 1), 0)
        carry[...] = jnp.sum(jnp.where(rows == Q - 1, cs, 0.0), axis=0, keepdims=True)

    blk = pl.BlockSpec((1, Q, LANES), lambda b, c: (b, c, 0))
    vec = pl.BlockSpec((1, LANES), lambda b, c: (0, 0))
    return pl.pallas_call(
        kern, name=name, grid=(B, nc),
        in_specs=[pl.BlockSpec((1, Q, LANES), lambda b, c: (b, c, col_blk)), vec, vec],
        out_specs=[blk, blk], out_shape=[jax.ShapeDtypeStruct((B, L, LANES), F32)] * 2,
        scratch_shapes=[pltpu.VMEM((1, LANES), F32)],
        compiler_params=_cparams(("parallel", "arbitrary")),
    )(proj3, bias, avec)


def _gate_post(drow, dcol, ddt, proj3, col_blk, vals, bias, avec, name):
    B, L, _ = proj3.shape
    Q = Q_BLOCK
    nc = L // Q

    def kern(dr_ref, dc_ref, dd_ref, x_ref, v_ref, b_ref, a_ref, o_ref, db_ref, da_ref, carry):
        b = pl.program_id(0)
        c = pl.program_id(1)

        @pl.when((b == 0) & (c == 0))
        def _():
            db_ref[...] = jnp.zeros_like(db_ref)
            da_ref[...] = jnp.zeros_like(da_ref)

        @pl.when(c == 0)
        def _():
            carry[...] = jnp.zeros_like(carry)

        x = x_ref[0] + b_ref[...]
        attn = _lane_is_attn(x.shape)
        dcs = dr_ref[0] + dc_ref[0]
        upper = jnp.logical_not(_lower_tri(Q, strict=True))
        rc = _split3_dot(upper, dcs) + jnp.where(attn[:1], carry[...], 0.0)
        rows = lax.broadcasted_iota(jnp.int32, (Q, 1), 0)
        carry[...] = jnp.sum(jnp.where(rows == 0, rc, 0.0), axis=0, keepdims=True)
        dv = jnp.where(attn, rc, dd_ref[0] + rc * a_ref[...])
        dpre = dv * jnp.where(attn, _sigmoid(-x), _sigmoid(x))
        o_ref[0] = dpre.astype(BF16)
        db_ref[...] += _colsum(dpre)
        da_ref[...] += _colsum(jnp.where(attn, 0.0, rc * v_ref[0]))

    rev = pl.BlockSpec((1, Q, LANES), lambda b, c: (b, nc - 1 - c, 0))
    vec = pl.BlockSpec((1, LANES), lambda b, c: (0, 0))
    return pl.pallas_call(
        kern, name=name, grid=(B, nc),
        in_specs=[rev, rev, rev, pl.BlockSpec((1, Q, LANES), lambda b, c: (b, nc - 1 - c, col_blk)), rev, vec, vec],
        out_specs=[rev, vec, vec],
        out_shape=[jax.ShapeDtypeStruct((B, L, LANES), BF16), jax.ShapeDtypeStruct((1, LANES), F32),
                   jax.ShapeDtypeStruct((1, LANES), F32)],
        scratch_shapes=[pltpu.VMEM((1, LANES), F32)],
        compiler_params=_cparams(("arbitrary", "arbitrary")),
    )(drow, dcol, ddt, proj3, vals, bias, avec)


def _lane_col(tile, lane):
    sel = lax.broadcasted_iota(jnp.int32, tile.shape, 1) == lane
    return jnp.sum(jnp.where(sel, tile, 0.0), axis=1, keepdims=True)


AUG = LANES
AUG_A = HEAD_DIM
AUG_B = HEAD_DIM + 3


def _split3(x):
    hi = x.astype(BF16).astype(F32)
    mid = (x - hi).astype(BF16).astype(F32)
    lo = (x - hi - mid).astype(BF16).astype(F32)
    return hi, mid, lo


def _put3(base, lane, first, x):
    hi, mid, lo = _split3(x)
    return jnp.where(lane == first, hi, jnp.where(lane == first + 1, mid, jnp.where(lane == first + 2, lo, base)))


HP = 2


def _other_half(x):
    return pltpu.roll(x, HEAD_DIM, 1)


def _loop_by_twos(n, step, init):
    carry = lax.fori_loop(0, n // 2, lambda t, c: step(2 * t + 1, step(2 * t, c)), init)
    return lax.cond(n % 2 == 1, lambda c: step(n - 1, c), lambda c: c, carry)


def _attn_pack(proj3, cums, name):
    B, L, _ = proj3.shape
    D = HEADS * HEAD_DIM
    nh = HEADS // HP
    tr = _tile(L, 384)
    scale = HEAD_DIM ** -0.5

    def kern(q_ref, k_ref, v_ref, c_ref, qa_ref, ka_ref, va_ref):
        lane = lax.broadcasted_iota(jnp.int32, (tr, AUG), 1)
        head = lane < HEAD_DIM
        ones_a = jnp.where((lane >= AUG_A) & (lane < AUG_A + 3), 1.0, 0.0)
        ones_b = jnp.where((lane >= AUG_B) & (lane < AUG_B + 3), 1.0, 0.0)
        ct = c_ref[0]
        for hp in range(nh):
            cols = slice(hp * LANES, (hp + 1) * LANES)
            for hh in range(HP):
                h = HP * hp + hh
                c = _lane_col(ct, h)
                sel = (lambda t: t) if hh == 0 else _other_half
                qa_ref[0, h] = jnp.where(head, sel(q_ref[0, :, cols]) * scale, _put3(ones_b, lane, AUG_A, c)).astype(BF16)
                ka_ref[0, h] = jnp.where(head, sel(k_ref[0, :, cols]), _put3(ones_a, lane, AUG_B, -c)).astype(BF16)
                va_ref[0, h] = jnp.where(head, sel(v_ref[0, :, cols]), ones_a).astype(BF16)

    def win(k):
        return pl.BlockSpec((1, tr, D), lambda b, i: (b, i, k))

    out = pl.BlockSpec((1, HEADS, tr, AUG), lambda b, i: (b, 0, i, 0))
    return pl.pallas_call(
        kern, name=name, grid=(B, L // tr),
        in_specs=[win(0), win(1), win(2), pl.BlockSpec((1, tr, LANES), lambda b, i: (b, i, 0))],
        out_specs=[out] * 3, out_shape=[jax.ShapeDtypeStruct((B, HEADS, L, AUG), BF16)] * 3,
        compiler_params=_cparams(("parallel", "parallel")),
    )(proj3, proj3, proj3, cums)


def _attn_fwd(qa, ka, va, name, gather=None):
    B, H, L, _ = qa.shape
    tq = _tile(L, 384)
    nq = L // tq
    nh = H // HP
    comm = gather is not None

    def kern(*refs):
        if comm:
            q_ref, k_ref, v_ref, x_ref, y_ref, yb_ref, l_ref, g_ref, send_sems, recv_sems, local_sem = refs
        else:
            q_ref, k_ref, v_ref, y_ref, yb_ref, l_ref = refs
        qi = pl.program_id(2)
        if comm:
            _ride((pl.program_id(0) * nh + pl.program_id(1)) * nq + qi, B * nh * nq,
                  _gather_phases(x_ref, g_ref, send_sems, recv_sems, local_sem))
        qs = [q_ref[0, hh] for hh in range(HP)]
        causal = _lower_tri(tq)

        def step(j, carry, masked):
            rows = pl.ds(pl.multiple_of(j * tq, tq), tq)
            out = []
            for hh in range(HP):
                m, acc = carry[hh]
                s = _dot_nt(qs[hh], k_ref[0, hh, rows, :])
                if masked:
                    s = jnp.where(causal, s, NEG)
                m_new = jnp.maximum(m, jnp.max(s, axis=1, keepdims=True))
                p = jnp.exp(s - m_new)
                out.append((m_new, jnp.exp(m - m_new) * acc + _dot(p.astype(BF16), v_ref[0, hh, rows, :])))
            return tuple(out)

        init = tuple((jnp.full((tq, 1), NEG, F32), jnp.zeros((tq, AUG), F32)) for _ in range(HP))
        carry = _loop_by_twos(qi, lambda j, c: step(j, c, False), init)
        outs = []
        for hh, (m, acc) in enumerate(step(qi, carry, True)):
            l = _lane_col(acc, AUG_A)
            outs.append(acc / l)
            l_ref[0, hh] = m + jnp.log(l)
        head = lax.broadcasted_iota(jnp.int32, (tq, AUG), 1) < HEAD_DIM
        y = jnp.where(head, outs[0], _other_half(outs[1]))
        y_ref[0] = y
        yb_ref[0] = y.astype(BF16)

    qspec = pl.BlockSpec((1, HP, tq, AUG), lambda b, h, i: (b, h, i, 0))
    kvspec = pl.BlockSpec((1, HP, L, AUG), lambda b, h, i: (b, h, 0, 0))
    lspec = pl.BlockSpec((1, HP, tq, 1), lambda b, h, i: (b, h, i, 0))
    yspec = pl.BlockSpec((1, tq, LANES), lambda b, h, i: (b, i, h))
    out_shape = [jax.ShapeDtypeStruct((B, L, H * HEAD_DIM), F32), jax.ShapeDtypeStruct((B, L, H * HEAD_DIM), BF16),
                 jax.ShapeDtypeStruct((B, H, L, 1), F32)]
    if comm:
        out_shape.append(jax.ShapeDtypeStruct((N_DEV,) + gather.shape, gather.dtype))
    return pl.pallas_call(
        kern, name=name, grid=(B, nh, nq), in_specs=[qspec, kvspec, kvspec] + ([ANY] if comm else []),
        out_specs=[yspec, yspec, lspec] + ([ANY] if comm else []), out_shape=out_shape,
        scratch_shapes=COMM_SCRATCH if comm else [],
        compiler_params=_cparams(("arbitrary",) * 3 if comm else ("parallel", "parallel", "arbitrary")),
    )(qa, ka, va, *([gather] if comm else []))


def _attn_bwd(qa, ka, va, y, dy, lse, name, parts=None):
    B, H, L, _ = qa.shape
    tq = _tile(L, 384)
    nq = L // tq
    nh = H // HP
    comm = parts is not None
    scale = HEAD_DIM ** -0.5

    def kern(*refs):
        if comm:
            (q_ref, k_ref, v_ref, y_ref, dy_ref, l_ref, p_ref, dq_ref, dk_ref, dv_ref, dc_ref, r_ref,
             dk_acc, dv_acc, send_sems, recv_sems, local_sem) = refs
        else:
            q_ref, k_ref, v_ref, y_ref, dy_ref, l_ref, dq_ref, dk_ref, dv_ref, dc_ref, dk_acc, dv_acc = refs
        qi = pl.program_id(2)
        hp = pl.program_id(1)
        lane_row = lax.broadcasted_iota(jnp.int32, (1, LANES), 1)
        onehot = [(lane_row == HP * hp + hh).astype(F32) for hh in range(HP)]

        @pl.when((hp == 0) & (qi == 0))
        def _():
            dc_ref[...] = jnp.zeros_like(dc_ref)

        if comm:
            _ride((pl.program_id(0) * nh + pl.program_id(1)) * nq + qi, B * nh * nq,
                  _exchange_phases(p_ref, r_ref, send_sems, recv_sems, local_sem))

        @pl.when(qi == 0)
        def _():
            dk_acc[...] = jnp.zeros_like(dk_acc)
            dv_acc[...] = jnp.zeros_like(dv_acc)

        lane = lax.broadcasted_iota(jnp.int32, (tq, AUG), 1)
        head = lane < HEAD_DIM
        qbs, dobs = [], []
        for hh in range(HP):
            sel = (lambda t: t) if hh == 0 else _other_half
            qf = q_ref[0, hh].astype(F32)
            dov = jnp.where(head, sel(dy_ref[0]), 0.0)
            dsum = jnp.sum(dov * sel(y_ref[0]), axis=1, keepdims=True)
            dobs.append(_put3(dov, lane, AUG_A, -dsum).astype(BF16))
            c_t = jnp.sum(jnp.where((lane >= AUG_A) & (lane < AUG_A + 3), qf, 0.0), axis=1, keepdims=True)
            qbs.append(_put3(qf, lane, AUG_A, c_t - l_ref[0, hh]).astype(BF16))
        causal = _lower_tri(tq)

        def step(j, dqs, masked):
            rows = pl.ds(pl.multiple_of(j * tq, tq), tq)
            out = []
            for hh in range(HP):
                kj = k_ref[0, hh, rows, :]
                s = _dot_nt(qbs[hh], kj)
                if masked:
                    s = jnp.where(causal, s, NEG)
                p = jnp.exp(s)
                ds = (p * _dot_nt(dobs[hh], v_ref[0, hh, rows, :])).astype(BF16)
                dv_acc[hh, rows, :] += _dot_tn(p.astype(BF16), dobs[hh])
                dk_acc[hh, rows, :] += _dot_tn(ds, qbs[hh])
                out.append(dqs[hh] + _dot(ds, kj))
            return tuple(out)

        dqs = _loop_by_twos(qi, lambda j, c: step(j, c, False), tuple(jnp.zeros((tq, AUG), F32) for _ in range(HP)))
        dqs = step(qi, dqs, True)
        dc_ref[0, pl.ds(pl.multiple_of(qi * tq, tq), tq), :] += (_lane_col(dqs[0], AUG_A) * onehot[0]
                                                                 + _lane_col(dqs[1], AUG_A) * onehot[1])
        dq_ref[0] = (jnp.where(head, dqs[0], _other_half(dqs[1])) * scale).astype(BF16)

        @pl.when(qi == nq - 1)
        def _():
            full = lax.broadcasted_iota(jnp.int32, (L, AUG), 1) < HEAD_DIM
            dk_ref[0] = jnp.where(full, dk_acc[0], _other_half(dk_acc[1])).astype(BF16)
            dv_ref[0] = jnp.where(full, dv_acc[0], _other_half(dv_acc[1])).astype(BF16)
            dc_ref[0] -= _lane_col(dk_acc[0], AUG_B) * onehot[0] + _lane_col(dk_acc[1], AUG_B) * onehot[1]

    qspec = pl.BlockSpec((1, HP, tq, AUG), lambda b, h, i: (b, h, i, 0))
    kvspec = pl.BlockSpec((1, HP, L, AUG), lambda b, h, i: (b, h, 0, 0))
    lspec = pl.BlockSpec((1, HP, tq, 1), lambda b, h, i: (b, h, i, 0))
    tmspec = pl.BlockSpec((1, L, LANES), lambda b, h, i: (b, 0, 0))
    yspec = pl.BlockSpec((1, tq, LANES), lambda b, h, i: (b, i, h))
    yfull = pl.BlockSpec((1, L, LANES), lambda b, h, i: (b, 0, h))
    nat = jax.ShapeDtypeStruct((B, L, H * HEAD_DIM), BF16)
    out_shape = [nat, nat, nat, jax.ShapeDtypeStruct((B, L, LANES), F32)]
    if comm:
        out_shape.append(jax.ShapeDtypeStruct(parts.shape, parts.dtype))
    return pl.pallas_call(
        kern, name=name, grid=(B, nh, nq),
        in_specs=[qspec, kvspec, kvspec, yspec, yspec, lspec] + ([ANY] if comm else []),
        out_specs=[yspec, yfull, yfull, tmspec] + ([ANY] if comm else []), out_shape=out_shape,
        scratch_shapes=[pltpu.VMEM((HP, L, AUG), F32), pltpu.VMEM((HP, L, AUG), F32)] + (COMM_SCRATCH if comm else []),
        compiler_params=_cparams(("parallel", "arbitrary", "arbitrary")),
    )(qa, ka, va, y, dy, lse, *([parts] if comm else []))


PAD = SUBLANES


def _halo_tile(x_ref, i, TR):
    r0 = pl.multiple_of(i * TR, TR)
    before = x_ref[0, pl.ds(pl.multiple_of(jnp.maximum(r0 - PAD, 0), PAD), PAD), :]
    return jnp.concatenate([jnp.where(i > 0, before, 0.0), x_ref[0, pl.ds(r0, TR), :]], axis=0)


def _conv_fwd(x3, x_blk, w, b, n_silu, name):
    B, L, _ = x3.shape
    C = w.shape[1]
    TR = _tile(L, 384, 8)

    def kern(x_ref, w_ref, b_ref, o_ref):
        cb = pl.program_id(1)

        def body(i, carry):
            r0 = pl.multiple_of(i * TR, TR)
            ext = _halo_tile(x_ref, i, TR)
            acc = jnp.zeros((TR, LANES), F32) + b_ref[...]
            for k in range(CONV_K):
                s = CONV_K - 1 - k
                sh = ext if s == 0 else pltpu.roll(ext, s, 0)
                acc = acc + w_ref[k:k + 1, :] * sh[PAD:PAD + TR]
            o_ref[0, pl.ds(r0, TR), :] = jnp.where(cb < n_silu, _silu(acc), acc)
            return carry

        lax.fori_loop(0, L // TR, body, 0)

    return pl.pallas_call(
        kern, name=name, grid=(B, C // LANES),
        in_specs=[pl.BlockSpec((1, L, LANES), lambda b_, c: (b_, 0, x_blk + c)),
                  pl.BlockSpec((CONV_K, LANES), lambda b_, c: (0, c)), pl.BlockSpec((1, LANES), lambda b_, c: (0, c))],
        out_specs=pl.BlockSpec((1, L, LANES), lambda b_, c: (b_, 0, c)),
        out_shape=jax.ShapeDtypeStruct((B, L, C), F32),
        compiler_params=_cparams(("parallel", "parallel")),
    )(x3, w, b)


def _conv_bwd_pre(x3, x_blk, du, w, b, n_silu, name):
    B, L, C = du.shape
    Lp = L + PAD
    TR = _tile(L, 384, 8)

    def kern(x_ref, du_ref, w_ref, b_ref, dp_ref, dw_ref):
        cb = pl.program_id(0)

        @pl.when(pl.program_id(1) == 0)
        def _():
            dw_ref[...] = jnp.zeros_like(dw_ref)

        def body(i, carry):
            r0 = pl.multiple_of(i * TR, TR)
            ext = _halo_tile(x_ref, i, TR)
            taps = []
            acc = jnp.zeros((TR, LANES), F32) + b_ref[...]
            for k in range(CONV_K):
                s = CONV_K - 1 - k
                sh = ext if s == 0 else pltpu.roll(ext, s, 0)
                taps.append(sh[PAD:PAD + TR])
                acc = acc + w_ref[k:k + 1, :] * taps[-1]
            dv = du_ref[0, pl.ds(r0, TR), :]
            dpre = jnp.where(cb < n_silu, dv * _dsilu(acc), dv)
            dp_ref[0, pl.ds(r0, TR), :] = dpre
            return tuple(c + _colsum(dpre * t) for c, t in zip(carry[:CONV_K], taps)) + (carry[CONV_K] + _colsum(dpre),)

        z = jnp.zeros((1, LANES), F32)
        sums = lax.fori_loop(0, L // TR, body, (z,) * (CONV_K + 1))
        dp_ref[0, pl.ds(L, PAD), :] = jnp.zeros((PAD, LANES), F32)
        for k in range(CONV_K + 1):
            dw_ref[k:k + 1, :] += sums[k]

    return pl.pallas_call(
        kern, name=name, grid=(C // LANES, B),
        in_specs=[pl.BlockSpec((1, L, LANES), lambda c, b_: (b_, 0, x_blk + c)),
                  pl.BlockSpec((1, L, LANES), lambda c, b_: (b_, 0, c)),
                  pl.BlockSpec((CONV_K, LANES), lambda c, b_: (0, c)), pl.BlockSpec((1, LANES), lambda c, b_: (0, c))],
        out_specs=[pl.BlockSpec((1, Lp, LANES), lambda c, b_: (b_, 0, c)),
                   pl.BlockSpec((SUBLANES, LANES), lambda c, b_: (0, c))],
        out_shape=[jax.ShapeDtypeStruct((B, Lp, C), F32), jax.ShapeDtypeStruct((SUBLANES, C), F32)],
        compiler_params=_cparams(("parallel", "arbitrary")),
    )(x3, du, w, b)


def _conv_bwd_in(dpp, w, name):
    B, Lp, C = dpp.shape
    L = Lp - PAD
    TR = _tile(L, 384, 16)

    def kern(d_ref, w_ref, o_ref):
        def body(i, carry):
            r0 = pl.multiple_of(i * TR, TR)
            ext = d_ref[0, pl.ds(r0, TR + PAD), :]
            acc = jnp.zeros((TR, LANES), F32)
            for k in range(CONV_K):
                s = CONV_K - 1 - k
                sh = ext if s == 0 else pltpu.roll(ext, TR + PAD - s, 0)
                acc = acc + w_ref[k:k + 1, :] * sh[0:TR]
            o_ref[0, pl.ds(r0, TR), :] = acc.astype(BF16)
            return carry

        lax.fori_loop(0, L // TR, body, 0)

    return pl.pallas_call(
        kern, name=name, grid=(B, C // LANES),
        in_specs=[pl.BlockSpec((1, Lp, LANES), lambda b_, c: (b_, 0, c)),
                  pl.BlockSpec((CONV_K, LANES), lambda b_, c: (0, c))],
        out_specs=pl.BlockSpec((1, L, LANES), lambda b_, c: (b_, 0, c)),
        out_shape=jax.ShapeDtypeStruct((B, L, C), BF16),
        compiler_params=_cparams(("parallel", "parallel")),
    )(dpp, w)


def _dot_nt(a, b):
    return lax.dot_general(a, b, (((1,), (1,)), ((), ())), preferred_element_type=F32)


def _dot_tn(a, b):
    return lax.dot_general(a, b, (((0,), (0,)), ((), ())), preferred_element_type=F32)


def _dot(a, b):
    return jnp.dot(a, b, preferred_element_type=F32)


def _ssd_specs(L, nc, b_blk, c_blk):
    pairs_per_group = HEADS // SSD_GROUPS // HP
    return [
        pl.BlockSpec((1, L, LANES), lambda b, h: (b, 0, h)),
        pl.BlockSpec((1, L, SSD_STATE), lambda b, h: (b, 0, b_blk + h // pairs_per_group)),
        pl.BlockSpec((1, L, SSD_STATE), lambda b, h: (b, 0, c_blk + h // pairs_per_group)),
        pl.BlockSpec((1, L, LANES), lambda b, h: (b, 0, 0)),
        pl.BlockSpec((1, L, LANES), lambda b, h: (b, 0, 0)),
        pl.BlockSpec((1, HP, nc, Q_BLOCK), lambda b, h: (b, HEADS // HP + h, 0, 0)),
        pl.BlockSpec((1, LANES), lambda b, h: (0, 0)),
    ]


def _halves(a, b, shape):
    return jnp.where(lax.broadcasted_iota(jnp.int32, shape, 1) < HEAD_DIM, a, b)


def _half_sums(t):
    first = lax.broadcasted_iota(jnp.int32, t.shape, 1) < HEAD_DIM
    lo = jnp.sum(jnp.where(first, t, 0.0), axis=1, keepdims=True)
    return lo, jnp.sum(t, axis=1, keepdims=True) - lo


def _ssd_chunk(c, S, x_ref, b_ref, c_ref, v_ref, cu_ref, ct_ref, lane0):
    Q = Q_BLOCK
    rows = pl.ds(pl.multiple_of(c * Q, Q), Q)
    x = x_ref[0, rows, :]
    Bb = b_ref[0, rows, :].astype(BF16)
    Cb = c_ref[0, rows, :].astype(BF16)
    vt, ct = v_ref[0, rows, :], cu_ref[0, rows, :]
    tri = _lower_tri(Q)
    A, Lm, e_end_h, eAend_h, dts = [], [], [], [], []
    for hh in range(HP):
        dts.append(_lane_col(vt, lane0 + hh))
        A.append(_lane_col(ct, lane0 + hh))
        Ar = ct_ref[0, hh, pl.ds(c, 1), :]
        Aend = _lane_col(Ar, Q - 1)
        Lm.append(jnp.exp(jnp.where(tri, A[hh] - Ar, NEG)))
        e_end_h.append(jnp.exp(Aend - A[hh]))
        eAend_h.append(jnp.exp(Aend))
    shape = (Q, LANES)
    dt = _halves(dts[0], dts[1], shape)
    eA = _halves(jnp.exp(A[0]), jnp.exp(A[1]), shape)
    e_end = _halves(e_end_h[0], e_end_h[1], shape)
    xdt = x * dt
    CB = _dot_nt(Cb, Bb)
    W = xdt * e_end
    srow = lax.broadcasted_iota(jnp.int32, (HP * HEAD_DIM, 1), 0) < HEAD_DIM
    eAend = jnp.where(srow, eAend_h[0], eAend_h[1])
    S_new = S * eAend + _dot_tn(W.astype(BF16), Bb)
    return dict(rows=rows, x=x, Bb=Bb, Cb=Cb, dt=dt, eA=eA, e_end=e_end, e_end_h=e_end_h, eAend=eAend,
                eAend_h=eAend_h, xdt=xdt, Lm=Lm, CB=CB, W=W, S_new=S_new)


def _ssd_fwd(u, b_blk, c_blk, vals, cums, cums_t, dvec, name):
    B, L, _ = u.shape
    nc = L // Q_BLOCK
    nh = HEADS // HP

    def kern(x_ref, b_ref, c_ref, v_ref, cu_ref, ct_ref, d_ref, y_ref):
        lane0 = HEADS + HP * pl.program_id(1)
        dskip = _halves(_lane_col(d_ref[...], lane0), _lane_col(d_ref[...], lane0 + 1), (1, LANES))
        first = lax.broadcasted_iota(jnp.int32, (Q_BLOCK, LANES), 1) < HEAD_DIM

        def body(c, S):
            q = _ssd_chunk(c, S, x_ref, b_ref, c_ref, v_ref, cu_ref, ct_ref, lane0)
            xb = q["xdt"].astype(BF16)
            yd = jnp.where(first, _dot((q["CB"] * q["Lm"][0]).astype(BF16), xb),
                           _dot((q["CB"] * q["Lm"][1]).astype(BF16), xb))
            z = _dot_nt(q["Cb"], S.astype(BF16))
            y_ref[0, q["rows"], :] = yd + z * q["eA"] + dskip * q["x"]
            return q["S_new"]

        lax.fori_loop(0, nc, body, jnp.zeros((HP * HEAD_DIM, SSD_STATE), F32))

    return pl.pallas_call(
        kern, name=name, grid=(B, nh), in_specs=_ssd_specs(L, nc, b_blk, c_blk),
        out_specs=pl.BlockSpec((1, L, LANES), lambda b, h: (b, 0, h)),
        out_shape=jax.ShapeDtypeStruct((B, L, HEADS * HEAD_DIM), F32),
        compiler_params=_cparams(("parallel", "arbitrary")),
    )(u, u, u, vals, cums, cums_t, dvec)


def _ssd_bwd(u, b_blk, c_blk, vals, cums, cums_t, dvec, dy, name):
    B, L, _ = u.shape
    Q = Q_BLOCK
    nc = L // Q
    N = SSD_STATE
    nh = HEADS // HP
    pairs_per_group = HEADS // SSD_GROUPS // HP
    PP = HP * HEAD_DIM

    def kern(x_ref, b_ref, c_ref, v_ref, cu_ref, ct_ref, d_ref, dy_ref,
             dx_ref, dB_ref, dC_ref, ddt_ref, dAc_ref, dAr_ref, dD_ref, s_all):
        b = pl.program_id(0)
        h = pl.program_id(1)
        lane0 = HEADS + HP * h
        dskip = _halves(_lane_col(d_ref[...], lane0), _lane_col(d_ref[...], lane0 + 1), (1, LANES))
        lane_row = lax.broadcasted_iota(jnp.int32, (1, LANES), 1)
        onehot = [(lane_row == lane0 + hh).astype(F32) for hh in range(HP)]

        @pl.when(h % pairs_per_group == 0)
        def _():
            dB_ref[...] = jnp.zeros_like(dB_ref)
            dC_ref[...] = jnp.zeros_like(dC_ref)

        @pl.when(h == 0)
        def _():
            ddt_ref[...] = jnp.zeros_like(ddt_ref)
            dAc_ref[...] = jnp.zeros_like(dAc_ref)

        @pl.when((b == 0) & (h == 0))
        def _():
            dD_ref[...] = jnp.zeros_like(dD_ref)

        def fwd(c, S):
            s_all[c] = S
            return _ssd_chunk(c, S, x_ref, b_ref, c_ref, v_ref, cu_ref, ct_ref, lane0)["S_new"]

        lax.fori_loop(0, nc, fwd, jnp.zeros((PP, N), F32))
        last_row = lax.broadcasted_iota(jnp.int32, (Q, 1), 0) == Q - 1
        first = lax.broadcasted_iota(jnp.int32, (Q, LANES), 1) < HEAD_DIM
        srow = lax.broadcasted_iota(jnp.int32, (PP, 1), 0) < HEAD_DIM

        def bwd(i, carry):
            dS, dD = carry
            c = nc - 1 - i
            S = s_all[c]
            q = _ssd_chunk(c, S, x_ref, b_ref, c_ref, v_ref, cu_ref, ct_ref, lane0)
            rows, x, Bb, Cb, xdt, Lm, CB = q["rows"], q["x"], q["Bb"], q["Cb"], q["xdt"], q["Lm"], q["CB"]
            dy = dy_ref[0, rows, :]
            dyb = dy.astype(BF16)
            xb = xdt.astype(BF16)
            Sb = S.astype(BF16)
            dD = dD + _colsum(dy * x)
            dyh = [jnp.where(first, dy, 0.0).astype(BF16), jnp.where(first, 0.0, dy).astype(BF16)]
            dM = [_dot_nt(dyh[hh], xb) for hh in range(HP)]
            dxdt = jnp.where(first, _dot_tn((CB * Lm[0]).astype(BF16), dyb), _dot_tn((CB * Lm[1]).astype(BF16), dyb))
            dCBb = (dM[0] * Lm[0] + dM[1] * Lm[1]).astype(BF16)
            dAc, dAr = [], []
            for hh in range(HP):
                G = dM[hh] * CB * Lm[hh]
                dAc.append(jnp.sum(G, axis=1, keepdims=True))
                dAr.append(-jnp.sum(G, axis=0, keepdims=True))
            dC = _dot(dCBb, Bb)
            dBm = _dot_tn(dCBb, Cb)
            z = _dot_nt(Cb, Sb)
            zs = _half_sums(dy * z)
            dzb = (dy * q["eA"]).astype(BF16)
            dC = dC + _dot(dzb, Sb)
            dS_in = _dot_tn(dzb, Cb)
            dSb = dS.astype(BF16)
            dW = _dot_nt(Bb, dSb)
            dBm = dBm + _dot(q["W"].astype(BF16), dSb)
            dxdt = dxdt + dW * q["e_end"]
            des = _half_sums(dW * xdt)
            ss = jnp.sum(dS * S, axis=1, keepdims=True)
            ss_lo = jnp.sum(jnp.where(srow, ss, 0.0), axis=0, keepdims=True)
            ss_h = [ss_lo, jnp.sum(ss, axis=0, keepdims=True) - ss_lo]
            ddts = _half_sums(dxdt * x)
            eA_h = [_lane_col(q["eA"], 0), _lane_col(q["eA"], HEAD_DIM)]
            dAc_tile = jnp.zeros((Q, LANES), F32)
            ddt_tile = jnp.zeros((Q, LANES), F32)
            for hh in range(HP):
                de = des[hh] * q["e_end_h"][hh]
                dAend = ss_h[hh] * q["eAend_h"][hh] + jnp.sum(de, axis=0, keepdims=True)
                col = dAc[hh] + zs[hh] * eA_h[hh] - de + jnp.where(last_row, dAend, 0.0)
                dAc_tile = dAc_tile + col * onehot[hh]
                ddt_tile = ddt_tile + ddts[hh] * onehot[hh]
                dAr_ref[0, hh, pl.ds(c, 1), :] = dAr[hh]
            dx_ref[0, rows, :] = dskip * dy + dxdt * q["dt"]
            dB_ref[0, 0, rows, :] += dBm
            dC_ref[0, 0, rows, :] += dC
            ddt_ref[0, rows, :] += ddt_tile
            dAc_ref[0, rows, :] += dAc_tile
            return dS * q["eAend"] + dS_in, dD

        _, dD = lax.fori_loop(0, nc, bwd, (jnp.zeros((PP, N), F32), jnp.zeros((1, LANES), F32)))
        dlo, dhi = _half_sums(dD)
        dD_ref[...] += dlo * onehot[0] + dhi * onehot[1]

    tm = pl.BlockSpec((1, L, LANES), lambda b, h: (b, 0, 0))
    grp = pl.BlockSpec((1, 1, L, N), lambda b, h: (b, h // pairs_per_group, 0, 0))
    xs = pl.BlockSpec((1, L, LANES), lambda b, h: (b, 0, h))
    return pl.pallas_call(
        kern, name=name, grid=(B, nh), in_specs=_ssd_specs(L, nc, b_blk, c_blk) + [xs],
        out_specs=[xs, grp, grp, tm, tm, pl.BlockSpec((1, HP, nc, Q), lambda b, h: (b, h, 0, 0)),
                   pl.BlockSpec((1, LANES), lambda b, h: (0, 0))],
        out_shape=[jax.ShapeDtypeStruct((B, L, HEADS * HEAD_DIM), F32), jax.ShapeDtypeStruct((B, SSD_GROUPS, L, N), F32),
                   jax.ShapeDtypeStruct((B, SSD_GROUPS, L, N), F32), jax.ShapeDtypeStruct((B, L, LANES), F32),
                   jax.ShapeDtypeStruct((B, L, LANES), F32), jax.ShapeDtypeStruct((B, HEADS, nc, Q), F32),
                   jax.ShapeDtypeStruct((1, LANES), F32)],
        scratch_shapes=[pltpu.VMEM((nc, PP, N), F32)],
        compiler_params=_cparams(("arbitrary", "arbitrary")),
    )(u, u, u, vals, cums, cums_t, dvec, dy)


LRU_TR = 384
LRU_CB = 512


def _lru_gates(xc, ra, ix, p_ref, first):
    r = _sigmoid(ra + p_ref[0:1, :])
    i = _sigmoid(ix + p_ref[1:2, :])
    ls = _log_sigmoid(p_ref[2:3, :])
    log_a = LRU_C * r * ls
    a = jnp.exp(log_a)
    mult0 = jnp.sqrt(_one_minus_exp(2.0 * log_a))
    mult = jnp.where(first, 1.0, mult0)
    return r, i, ls, a, mult0, mult


def _lru_fwd(u, xc_off, ra, ix, proj3, gate_off, pvec, name):
    B, L, D = ra.shape
    TR, CB = _tile(L, LRU_TR, 8), LRU_CB
    nrt = L // TR

    def kern(xc_ref, ra_ref, ix_ref, g_ref, p_ref, y_ref, hs_ref, a_ref, pa_s, pu_s, carry):
        rt = pl.program_id(2)

        @pl.when(rt == 0)
        def _():
            carry[...] = jnp.zeros_like(carry)

        row = lax.broadcasted_iota(jnp.int32, (TR, 1), 0)
        first = (rt == 0) & (row == 0)
        xc = xc_ref[0]
        r, i, ls, a, mult0, mult = _lru_gates(xc, ra_ref[0], ix_ref[0], p_ref, first)
        a_ref[0] = a
        pa, pu = a, mult * (i * xc)
        sub = row % SUBLANES
        for s in (1, 2, 4):
            ok = sub >= s
            pu = jnp.where(ok, pa * pltpu.roll(pu, s, 0) + pu, pu)
            pa = jnp.where(ok, pa * pltpu.roll(pa, s, 0), pa)
        pa_s[...] = pa
        pu_s[...] = pu
        row8 = lax.broadcasted_iota(jnp.int32, (SUBLANES, 1), 0)

        def gbody(g, c):
            r8 = pl.ds(pl.multiple_of(g * SUBLANES, SUBLANES), SUBLANES)
            hg = pa_s[r8, :] * c + pu_s[r8, :]
            hs_ref[0, r8, :] = hg
            return jnp.sum(jnp.where(row8 == SUBLANES - 1, hg, 0.0), axis=0, keepdims=True)

        carry[...] = lax.fori_loop(0, TR // SUBLANES, gbody, carry[...])
        y_ref[0] = (hs_ref[0] * _gelu(g_ref[0])).astype(BF16)

    def win(off):
        assert off % CB == 0
        return pl.BlockSpec((1, TR, CB), functools.partial(lambda b, j, t, o: (b, t, j + o), o=off // CB))

    return pl.pallas_call(
        kern, name=name, grid=(B, D // CB, nrt),
        in_specs=[win(xc_off), win(0), win(0), win(gate_off), pl.BlockSpec((SUBLANES, CB), lambda b, j, t: (0, j))],
        out_specs=[win(0)] * 3,
        out_shape=[jax.ShapeDtypeStruct((B, L, D), BF16), jax.ShapeDtypeStruct((B, L, D), F32),
                   jax.ShapeDtypeStruct((B, L, D), F32)],
        scratch_shapes=[pltpu.VMEM((TR, CB), F32), pltpu.VMEM((TR, CB), F32), pltpu.VMEM((1, CB), F32)],
        compiler_params=_cparams(("parallel", "parallel", "arbitrary")),
    )(u, ra, ix, proj3, pvec)


def _lru_bwd(dy, proj3, gate_off, hs, a, u, xc_off, ra, ix, pvec, name):
    B, L, D = ra.shape
    TR, CB = _tile(L, LRU_TR, 8), LRU_CB
    nrt = L // TR

    def kern(dy_ref, g_ref, hs_ref, hsp_ref, a_ref, an_ref, xc_ref, ra_ref, ix_ref, p_ref,
             dg_ref, dra_ref, dix_ref, dxc_ref, dp_ref, pb_s, pd_s, g_s, carry):
        b = pl.program_id(1)
        rt = pl.program_id(2)
        t = nrt - 1 - rt

        @pl.when((b == 0) & (rt == 0))
        def _():
            dp_ref[...] = jnp.zeros_like(dp_ref)

        @pl.when(rt == 0)
        def _():
            carry[...] = jnp.zeros_like(carry)

        row = lax.broadcasted_iota(jnp.int32, (TR, 1), 0)
        gate, hsv, av, dyv = g_ref[0], hs_ref[0], a_ref[0], dy_ref[0]
        dg_ref[0] = (dyv * hsv * _dgelu(gate)).astype(BF16)
        a_next = jnp.where(t == nrt - 1, 0.0, an_ref[0, 0:1, :])
        pb = jnp.where(row == TR - 1, a_next, pltpu.roll(av, TR - 1, 0))
        pd = dyv * _gelu(gate)
        sub = row % SUBLANES
        for s in (1, 2, 4):
            ok = sub < SUBLANES - s
            pd = jnp.where(ok, pd + pb * pltpu.roll(pd, TR - s, 0), pd)
            pb = jnp.where(ok, pb * pltpu.roll(pb, TR - s, 0), pb)
        pb_s[...] = pb
        pd_s[...] = pd
        row8 = lax.broadcasted_iota(jnp.int32, (SUBLANES, 1), 0)

        def gbody(i, c):
            r8 = pl.ds(pl.multiple_of((TR // SUBLANES - 1 - i) * SUBLANES, SUBLANES), SUBLANES)
            gg = pd_s[r8, :] + pb_s[r8, :] * c
            g_s[r8, :] = gg
            return jnp.sum(jnp.where(row8 == 0, gg, 0.0), axis=0, keepdims=True)

        carry[...] = lax.fori_loop(0, TR // SUBLANES, gbody, carry[...])
        gv = g_s[...]
        h_first = jnp.where(t == 0, 0.0, hsp_ref[0, TR - 1:TR, :])
        hprev = jnp.where(row == 0, h_first, pltpu.roll(hsv, 1, 0))
        first = (t == 0) & (row == 0)
        xc = xc_ref[0]
        r, i, ls, a2, mult0, mult = _lru_gates(xc, ra_ref[0], ix_ref[0], p_ref, first)
        dxc_ref[0] = gv * mult * i
        dlog_a = gv * hprev * av + jnp.where(first, 0.0, gv * i * xc * (-(av * av) / mult0))
        dra = dlog_a * LRU_C * ls * r * (1.0 - r)
        dix = gv * mult * xc * i * (1.0 - i)
        dra_ref[0] = dra.astype(BF16)
        dix_ref[0] = dix.astype(BF16)
        dp_ref[0:1, :] += _colsum(dra)
        dp_ref[1:2, :] += _colsum(dix)
        dp_ref[2:3, :] += _colsum(dlog_a * LRU_C * r) * _sigmoid(-p_ref[2:3, :])

    def win(off, shift=0):
        assert off % CB == 0
        o = off // CB
        return pl.BlockSpec((1, TR, CB), lambda j, b, rt: (b, jnp.clip(nrt - 1 - rt + shift, 0, nrt - 1), j + o))

    return pl.pallas_call(
        kern, name=name, grid=(D // CB, B, nrt),
        in_specs=[win(0), win(gate_off), win(0), win(0, -1), win(0), win(0, 1), win(xc_off), win(0), win(0),
                  pl.BlockSpec((SUBLANES, CB), lambda j, b, rt: (0, j))],
        out_specs=[win(0)] * 4 + [pl.BlockSpec((SUBLANES, CB), lambda j, b, rt: (0, j))],
        out_shape=[jax.ShapeDtypeStruct((B, L, D), BF16)] * 3 + [jax.ShapeDtypeStruct((B, L, D), F32),
                                                                 jax.ShapeDtypeStruct((SUBLANES, D), F32)],
        scratch_shapes=[pltpu.VMEM((TR, CB), F32)] * 3 + [pltpu.VMEM((1, CB), F32)],
        compiler_params=_cparams(("parallel", "arbitrary", "arbitrary")),
    )(dy, proj3, hs, hs, a, a, u, ra, ix, pvec)


def _sum8(parts, name):
    _, R, C = parts.shape
    tr = _tile(R, 1024, ROW_ALIGN if parts.dtype.itemsize == 2 else SUBLANES)

    def kern(p_ref, o_ref):
        acc = p_ref[0].astype(F32)
        for d in range(1, N_DEV):
            acc = acc + p_ref[d].astype(F32)
        o_ref[...] = acc

    return pl.pallas_call(
        kern, name=name, grid=(R // tr,), in_specs=[pl.BlockSpec((N_DEV, tr, C), lambda i: (0, i, 0))],
        out_specs=pl.BlockSpec((tr, C), lambda i: (i, 0)), out_shape=jax.ShapeDtypeStruct((R, C), F32),
        compiler_params=_cparams(("parallel",)),
    )(parts)


def _adamw(w, g, m, v, name):
    shape = w.shape
    C = shape[-1] if w.ndim > 1 else shape[0]
    R = w.size // C
    w2, g2, m2, v2 = (t.reshape(R, C) for t in (w, g, m, v))
    tr = R
    for cand in range(8, min(R, 512) + 1, 8):
        if R % cand == 0:
            tr = cand

    def kern(w_ref, g_ref, m_ref, v_ref, d_ref, nm_ref, nv_ref):
        gv = g_ref[...]
        nm = ADAM_B1 * m_ref[...] + (1.0 - ADAM_B1) * gv
        nv = ADAM_B2 * v_ref[...] + (1.0 - ADAM_B2) * (gv * gv)
        m_hat = nm / (1.0 - ADAM_B1 ** ADAM_STEP)
        v_hat = nv / (1.0 - ADAM_B2 ** ADAM_STEP)
        d_ref[...] = -ADAM_LR * (m_hat / (jnp.sqrt(v_hat) + ADAM_EPS) + ADAM_WD * w_ref[...])
        nm_ref[...] = nm
        nv_ref[...] = nv

    spec = pl.BlockSpec((tr, C), lambda i: (i, 0))
    outs = pl.pallas_call(
        kern, name=name, grid=(R // tr,), in_specs=[spec] * 4, out_specs=[spec] * 3,
        out_shape=[jax.ShapeDtypeStruct((R, C), F32)] * 3, compiler_params=_cparams(("parallel",)),
    )(w2, g2, m2, v2)
    return tuple(o.reshape(shape) for o in outs)


MESH_ID = pl.DeviceIdType.MESH
ANY = pl.BlockSpec(memory_space=pl.ANY)
N_COPIES = N_DEV - 1
COMM_SCRATCH = [pltpu.SemaphoreType.DMA((N_COPIES,)), pltpu.SemaphoreType.DMA((N_COPIES,)), pltpu.SemaphoreType.DMA]


def _my_place():
    return lax.axis_index("x"), lax.axis_index("y"), lax.axis_index("c")


def _gather_phases(x_ref, out_ref, send_sems, recv_sems, local_sem):
    x, y, c = _my_place()
    me, sibling = (x, y, c), (x, y, 1 - c)
    chips = [(1 - x, y), (x, 1 - y), (1 - x, 1 - y)]

    def slab(px, py, pc):
        return out_ref.at[4 * px + 2 * py + pc]

    def copy(k, block, to, src=None):
        return pltpu.make_async_remote_copy(
            src_ref=slab(*block) if src is None else src, dst_ref=slab(*block),
            send_sem=send_sems.at[k], recv_sem=recv_sems.at[k], device_id=to, device_id_type=MESH_ID)

    mine = pltpu.make_async_copy(x_ref, slab(*me), local_sem)
    first = [copy(0, me, sibling, src=x_ref)] + [copy(1 + j, me, (*chip, c), src=x_ref) for j, chip in enumerate(chips)]
    passed = [copy(4 + j, (*chip, c), sibling) for j, chip in enumerate(chips)]

    def start():
        mine.start()
        for cp in first:
            cp.start()

    def forward():
        for j, chip in enumerate(chips):
            copy(1 + j, (*chip, c), me).wait_recv()
            passed[j].start()

    def finish():
        copy(0, sibling, me).wait_recv()
        for j, chip in enumerate(chips):
            copy(4 + j, (*chip, 1 - c), me).wait_recv()
        for cp in first + passed:
            cp.wait_send()
        mine.wait()

    return start, forward, finish


def _exchange_phases(p_ref, out_ref, send_sems, recv_sems, local_sem):
    x, y, c = _my_place()
    my_idx = 4 * x + 2 * y + c
    mine = pltpu.make_async_copy(p_ref.at[my_idx], out_ref.at[my_idx], local_sem)
    copies = []
    for k in range(1, N_DEV):
        px, py, pc = x ^ (k >> 2), y ^ ((k >> 1) & 1), c ^ (k & 1)
        copies.append(pltpu.make_async_remote_copy(
            src_ref=p_ref.at[4 * px + 2 * py + pc], dst_ref=out_ref.at[my_idx],
            send_sem=send_sems.at[k - 1], recv_sem=recv_sems.at[k - 1], device_id=(px, py, pc),
            device_id_type=MESH_ID))

    def start():
        mine.start()
        for cp in copies:
            cp.start()

    def finish():
        for cp in copies:
            cp.wait()
        mine.wait()

    return start, finish


def _ride(lin, total, phases):
    assert total >= 3
    marks = [0, total - 1] if len(phases) == 2 else [0, total // 2, total - 1]
    for mark, phase in zip(marks, phases):
        pl.when(lin == mark)(phase)


def _all_gather(xs, name):
    R, C = xs.shape

    def body(x_ref, out_ref, send_sems, recv_sems, local_sem):
        for phase in _gather_phases(x_ref, out_ref, send_sems, recv_sems, local_sem):
            phase()

    return pl.pallas_call(
        body, name=name, out_shape=jax.ShapeDtypeStruct((N_DEV, R, C), xs.dtype), in_specs=[ANY], out_specs=ANY,
        scratch_shapes=COMM_SCRATCH,
    )(xs)


def _exchange(parts, name):
    def body(p_ref, out_ref, send_sems, recv_sems, local_sem):
        for phase in _exchange_phases(p_ref, out_ref, send_sems, recv_sems, local_sem):
            phase()

    return pl.pallas_call(
        body, name=name, out_shape=jax.ShapeDtypeStruct(parts.shape, parts.dtype), in_specs=[ANY], out_specs=ANY,
        scratch_shapes=COMM_SCRATCH,
    )(parts)


D_XBC_EXTRA = 2 * SSD_GROUPS * SSD_STATE
SMALL_W = LANES
ROW_ALIGN = 16


def _layout(D):
    d_xbc = D + D_XBC_EXTRA
    off = dict(qkv=0, z=3 * D, merge=4 * D, gate=7 * D, conv=8 * D, xr=8 * D + d_xbc, small=9 * D + d_xbc)
    off["n_all"] = off["small"] + SMALL_W
    off["d_xbc"] = d_xbc
    off["conv_c"] = d_xbc + D
    return off


def _w_in_map(D):
    lo = _layout(D)
    widths = [("q", D, 0), ("k", D, D), ("v", D, 2 * D), ("f", HEADS, lo["small"]), ("z", D, lo["z"]),
              ("xbc", lo["d_xbc"], lo["conv"]), ("dt", HEADS, lo["small"] + HEADS), ("xr", D, lo["xr"]),
              ("gate", D, lo["gate"]), ("merge", 3 * D, lo["merge"])]
    out, o = [], 0
    for _, w, mine in widths:
        out.append((o, w, mine))
        o += w
    return out


def _padded(c):
    return -(-c // ROW_ALIGN) * ROW_ALIGN


def _permute_rows(src, pieces, name):
    R, C = src.shape
    n_out = sum(n for _, n in pieces)

    def kern(x_ref, o_ref):
        o = 0
        for start, n in pieces:
            if start is None:
                o_ref[o:o + n, :] = jnp.zeros((n, LANES), src.dtype)
            else:
                o_ref[o:o + n, :] = x_ref[start:start + n, :]
            o += n

    return pl.pallas_call(
        kern, name=name, grid=(C // LANES,), in_specs=[pl.BlockSpec((R, LANES), lambda i: (0, i))],
        out_specs=pl.BlockSpec((n_out, LANES), lambda i: (0, i)), out_shape=jax.ShapeDtypeStruct((n_out, C), src.dtype),
        compiler_params=_cparams(("parallel",)),
    )(src)


def _reorder_rows(wt, D, c, name="reorder_w_in"):
    cp = _padded(c)
    lo = _layout(D)
    pieces = []
    for a, w, mine in sorted(_w_in_map(D), key=lambda t: t[2]):
        b = a + w
        while a < b:
            j = a // c
            e = min(b, (j + 1) * c)
            pieces.append((j * cp + a - j * c, e - a))
            a = e
    pieces.append((None, lo["n_all"] - lo["small"] - 2 * HEADS))
    return _permute_rows(wt, pieces, name)


def _restore_rows(dwt, D, c, name="restore_w_in"):
    cp = _padded(c)
    segs = _w_in_map(D)
    pieces = []
    for j in range(N_DEV):
        a, b = j * c, (j + 1) * c
        for s0, w, mine in segs:
            lo_, hi_ = max(a, s0), min(b, s0 + w)
            if lo_ < hi_:
                pieces.append((mine + lo_ - s0, hi_ - lo_))
        if cp > c:
            pieces.append((None, cp - c))
    return _permute_rows(dwt, pieces, name)


def _block_diag(w):
    H, n, _ = w.shape
    tiled = jnp.tile(w.reshape(H * n, n), (1, H))
    r = lax.broadcasted_iota(jnp.int32, (H * n, H * n), 0) // n
    c = lax.broadcasted_iota(jnp.int32, (H * n, H * n), 1) // n
    return jnp.where(r == c, tiled, jnp.zeros_like(tiled))


def _diag_blocks(m, H):
    n = m.shape[0] // H
    keep = jnp.eye(H, dtype=m.dtype)[:, None, :, None]
    return jnp.sum(m.reshape(H, n, H, n) * keep, axis=2)


def _to_heads(t, B, L):
    return t.reshape(B, L, HEADS, HEAD_DIM).transpose(0, 2, 1, 3)


def _from_heads(t4):
    B, H, L, P = t4.shape
    return t4.transpose(0, 2, 1, 3).reshape(B * L, H * P)


def _rows_to_tm(rows):
    B, H, nc, Q = rows.shape
    return rows.reshape(B, H, nc * Q).transpose(0, 2, 1)


def _ffn_fwd(h, g, wgu_t, wd, tag):
    n = _norm_fwd(h, g, tag + "_norm")
    gu = _mm(n, wgu_t, tb=True, name=tag + "_up")
    act = _swiglu_fwd(gu, tag + "_act")
    out = _mm(act, wd, res=h, scale=0.5, name=tag + "_down")
    return out, (h, n, gu, act)


def _ffn_bwd(dh, dhb, saved, g, wgu_t, wd, tag):
    h, n, gu, act = saved
    dact = _mm(dhb, wd, tb=True, scale=0.5, name=tag + "_down_dx")
    dwd = _mm(act, dhb, ta=True, scale=0.5, name=tag + "_down_dw")
    dgu = _swiglu_bwd(gu, dact, tag + "_act_bwd")
    dwgu_t = _mm(dgu, n, ta=True, tn=1024, name=tag + "_up_dw")
    dn = _mm(dgu, wgu_t, name=tag + "_up_dx")
    dh_in, dhb_in, dg = _norm_bwd(h, dn, dh, g, tag + "_norm_bwd")
    return dh_in, dhb_in, dict(norm=dg, gu=dwgu_t, down=dwd)


def _mixer_fwd(h, p, B, L, gather=None):
    T, D = h.shape
    lo = _layout(D)
    n = _norm_fwd(h, p["gm"], "mix_norm")
    proj = _mm(n, p["w_all_t"], tb=True, name="mix_in")
    proj3 = proj.reshape(B, L, lo["n_all"])
    vals, cums = _gate_prep(proj3, lo["small"] // LANES, p["small_bias"], p["avec"], "gate_prep")
    cums_t = cums[..., :2 * HEADS].transpose(0, 2, 1).reshape(B, 2 * HEADS, L // Q_BLOCK, Q_BLOCK)
    qa, ka, va = _attn_pack(proj3, cums, "attn_pack")
    y_a3, y_ab3, lse, *gathered = _attn_fwd(qa, ka, va, "attn_fwd", gather)
    y_a, y_ab = y_a3.reshape(T, D), y_ab3.reshape(T, D)
    u = _conv_fwd(proj3, lo["conv"] // LANES, p["conv_w"], p["conv_b"], lo["d_xbc"] // LANES, "conv_fwd")
    b_blk = D // LANES
    c_blk = b_blk + SSD_GROUPS * SSD_STATE // LANES
    y_s = _ssd_fwd(u, b_blk, c_blk, vals, cums, cums_t, p["dvec"], "ssd_fwd").reshape(T, D)
    yb = _gnorm_fwd(y_s, proj, lo["z"], p["ssd_norm"], "gnorm_fwd")
    u2 = u.reshape(T, lo["conv_c"])
    ra = _mm(u2, p["wa"], a_off=(0, lo["d_xbc"]), dims=(T, D, D), tk=512, name="lru_ra")
    ix = _mm(u2, p["wx"], a_off=(0, lo["d_xbc"]), dims=(T, D, D), tk=512, name="lru_ix")
    yc, hs, a = _lru_fwd(u, lo["d_xbc"], ra.reshape(B, L, D), ix.reshape(B, L, D), proj3, lo["gate"], p["pvec"],
                         "lru_fwd")
    yc = yc.reshape(T, D)
    pa = _mm(y_ab, p["wba"], name="branch_attn")
    pb = _mm(yb, p["wbs"], name="branch_ssd")
    pc = _mm(yc, p["wbl"], name="branch_lru")
    mixed = _merge_fwd(proj, lo["merge"], pa, pb, pc, "merge_fwd")
    out = _mm(mixed, p["wout"], res=h, name="mix_out")
    saved = dict(h=h, n=n, proj=proj, qa=qa, ka=ka, va=va, vals=vals, cums=cums, cums_t=cums_t, lse=lse, y_a=y_a, y_ab=y_ab,
                 u=u, y_s=y_s, yb=yb, ra=ra, ix=ix, yc=yc, hs=hs, a=a, pa=pa, pb=pb, pc=pc, mixed=mixed)
    return out, saved, (gathered[0] if gathered else None)


def _mixer_bwd(dh, dhb, s, p, B, L, parts=None):
    T, D = dh.shape
    lo = _layout(D)
    proj, u = s["proj"], s["u"]
    proj3 = proj.reshape(B, L, lo["n_all"])
    g = {}
    dmixed = _mm(dhb, p["wout"], tb=True, name="mix_out_dx")
    g["wout"] = _mm(s["mixed"], dhb, ta=True, name="mix_out_dw")
    dpa, dpb, dpc, dmerge = _merge_bwd(dmixed, proj, lo["merge"], s["pa"], s["pb"], s["pc"], "merge_bwd")
    dy_a = _mm(dpa, p["wba"], tb=True, name="branch_attn_dx")
    g["wba"] = _mm(s["y_ab"], dpa, ta=True, name="branch_attn_dw")
    dyb = _mm(dpb, p["wbs"], tb=True, name="branch_ssd_dx")
    g["wbs"] = _mm(s["yb"], dpb, ta=True, name="branch_ssd_dw")
    dyc = _mm(dpc, p["wbl"], tb=True, name="branch_lru_dx")
    g["wbl"] = _mm(s["yc"], dpc, ta=True, name="branch_lru_dw")
    dgate, dra, dix, dxc, g["pvec"] = _lru_bwd(dyc.reshape(B, L, D), proj3, lo["gate"], s["hs"], s["a"], u, lo["d_xbc"],
                                               s["ra"].reshape(B, L, D), s["ix"].reshape(B, L, D), p["pvec"], "lru_bwd")
    dra, dix = dra.reshape(T, D), dix.reshape(T, D)
    u2 = u.reshape(T, lo["conv_c"])
    g["wa"] = _mm(u2, dra, ta=True, a_off=(0, lo["d_xbc"]), dims=(D, D, T), tm=512, name="lru_ra_dw")
    g["wx"] = _mm(u2, dix, ta=True, a_off=(0, lo["d_xbc"]), dims=(D, D, T), tm=512, name="lru_ix_dw")
    dxc = _mm(dra, p["wa"], tb=True, res=dxc.reshape(T, D), name="lru_ra_dx")
    dxc = _mm(dix, p["wx"], tb=True, res=dxc, name="lru_ix_dx")
    dy_s, dz, g["ssd_norm"] = _gnorm_bwd(dyb, s["y_s"], proj, lo["z"], p["ssd_norm"], "gnorm_bwd")
    b_blk = D // LANES
    c_blk = b_blk + SSD_GROUPS * SSD_STATE // LANES
    dxs, dBg, dCg, ddt_tm, dAc_tm, dAr, g["dvec"] = _ssd_bwd(u, b_blk, c_blk, s["vals"], s["cums"], s["cums_t"],
                                                             p["dvec"], dy_s.reshape(B, L, D), "ssd_bwd")
    grp = lambda t: t.transpose(0, 2, 1, 3).reshape(B, L, SSD_GROUPS * SSD_STATE)
    du = jnp.concatenate([dxs, grp(dBg), grp(dCg), dxc.reshape(B, L, D)], axis=-1)
    dpp, g["conv_wb"] = _conv_bwd_pre(proj3, lo["conv"] // LANES, du, p["conv_w"], p["conv_b"], lo["d_xbc"] // LANES,
                                      "conv_bwd_pre")
    dconv = _conv_bwd_in(dpp, p["conv_w"], "conv_bwd_in")
    dq3, dk3, dv3, dc_tm, *recv = _attn_bwd(s["qa"], s["ka"], s["va"], s["y_a"].reshape(B, L, D),
                                            dy_a.reshape(B, L, D), s["lse"], "attn_bwd", parts)
    drow_tm = dc_tm + jnp.pad(_rows_to_tm(dAr), ((0, 0), (0, 0), (HEADS, LANES - 2 * HEADS)))
    dsmall, g["small_bias"], g["avec"] = _gate_post(drow_tm, dAc_tm, ddt_tm, proj3, lo["small"] // LANES, s["vals"],
                                                    p["small_bias"], p["avec"], "gate_post")
    dproj = jnp.concatenate([dq3.reshape(T, D), dk3.reshape(T, D), dv3.reshape(T, D), dz, dmerge, dgate.reshape(T, D), dconv.reshape(T, lo["conv_c"]),
                             dsmall.reshape(T, SMALL_W)], axis=1)
    g["w_all_t"] = _mm(dproj, s["n"], ta=True, tn=1024, name="mix_in_dw")
    dn = _mm(dproj, p["w_all_t"], name="mix_in_dx")
    dh_in, dhb_in, g["gm"] = _norm_bwd(s["h"], dn, dh, p["gm"], "mix_norm_bwd")
    return dh_in, dhb_in, g, (recv[0] if recv else None)


def _small_vec(a, b):
    return jnp.concatenate([a, b, jnp.zeros((LANES - 2 * HEADS,), F32)])[None, :]


def _layer_params(w):
    zeros16 = jnp.zeros((HEADS,), F32)
    pvec = jnp.concatenate([w["lru_b_a"][None], w["lru_b_x"][None], w["lru_lambda"][None],
                            jnp.zeros((SUBLANES - 3, w["lru_b_a"].shape[0]), F32)], axis=0)
    return dict(
        g1=w["ffn1_norm"][None], gu1=w["ffn1_w_gate_up"], d1=w["ffn1_w_down"],
        gm=w["mix_norm"][None], w_all_t=w["w_in"],
        small_bias=_small_vec(w["fox_forget_bias"], w["ssd_dt_bias"]),
        avec=_small_vec(zeros16, -jnp.exp(w["ssd_a_log"])), dvec=_small_vec(zeros16, w["ssd_d"]),
        conv_w=jnp.concatenate([w["ssd_conv_w"], w["lru_conv_w"]], axis=1),
        conv_b=jnp.concatenate([w["ssd_conv_b"], w["lru_conv_b"]])[None],
        ssd_norm=w["ssd_norm"][None],
        wa=_block_diag(w["lru_w_a"]).astype(BF16), wx=_block_diag(w["lru_w_x"]).astype(BF16), pvec=pvec,
        wba=w["w_branch_attn"], wbs=w["w_branch_ssd"], wbl=w["w_branch_lru"], wout=w["w_out"],
        g2=w["ffn2_norm"][None], gu2=w["ffn2_w_gate_up"], d2=w["ffn2_w_down"],
    )


def _layer_fwd(h, p, B, L, gather=None):
    h, s1 = _ffn_fwd(h, p["g1"], p["gu1"], p["d1"], "ffn1")
    h, sm, gathered = _mixer_fwd(h, p, B, L, gather)
    h, s2 = _ffn_fwd(h, p["g2"], p["gu2"], p["d2"], "ffn2")
    return h, (s1, sm, s2), gathered


def _layer_bwd(dh, dhb, saved, p, w, B, L, parts=None):
    s1, sm, s2 = saved
    D = dh.shape[1]
    d_xbc = D + D_XBC_EXTRA
    dh, dhb, f2 = _ffn_bwd(dh, dhb, s2, p["g2"], p["gu2"], p["d2"], "ffn2")
    dh, dhb, gm, recv = _mixer_bwd(dh, dhb, sm, p, B, L, parts)
    dh, dhb, f1 = _ffn_bwd(dh, dhb, s1, p["g1"], p["gu1"], p["d1"], "ffn1")
    sb, av = gm["small_bias"][0], gm["avec"][0]
    cw = gm["conv_wb"]
    grads = dict(
        ffn1_norm=f1["norm"][0], ffn1_w_gate_up=f1["gu"], ffn1_w_down=f1["down"],
        mix_norm=gm["gm"][0], w_in=gm["w_all_t"],
        fox_forget_bias=sb[:HEADS], ssd_conv_w=cw[:CONV_K, :d_xbc], ssd_conv_b=cw[CONV_K, :d_xbc],
        ssd_dt_bias=sb[HEADS:2 * HEADS], ssd_a_log=av[HEADS:2 * HEADS] * (-jnp.exp(w["ssd_a_log"])),
        ssd_d=gm["dvec"][0, HEADS:2 * HEADS], ssd_norm=gm["ssd_norm"][0],
        lru_conv_w=cw[:CONV_K, d_xbc:], lru_conv_b=cw[CONV_K, d_xbc:],
        lru_w_a=_diag_blocks(gm["wa"], HEADS), lru_b_a=gm["pvec"][0], lru_w_x=_diag_blocks(gm["wx"], HEADS),
        lru_b_x=gm["pvec"][1], lru_lambda=gm["pvec"][2],
        w_branch_attn=gm["wba"], w_branch_ssd=gm["wbs"], w_branch_lru=gm["wbl"], w_out=gm["wout"],
        ffn2_norm=f2["norm"][0], ffn2_w_gate_up=f2["gu"], ffn2_w_down=f2["down"],
    )
    return dh, dhb, grads, recv


LAYER_NAMES = ["ffn1_norm", "ffn1_w_gate_up", "ffn1_w_down", "mix_norm", "w_in", "fox_forget_bias", "ssd_conv_w",
               "ssd_conv_b", "ssd_dt_bias", "ssd_a_log", "ssd_d", "ssd_norm", "lru_conv_w", "lru_conv_b", "lru_w_a",
               "lru_b_a", "lru_w_x", "lru_b_x", "lru_lambda", "w_branch_attn", "w_branch_ssd", "w_branch_lru", "w_out",
               "ffn2_norm", "ffn2_w_gate_up", "ffn2_w_down"]
WEIGHT_NAMES = ["meta_tokens"] + LAYER_NAMES + ["final_norm"]


def _local_step(x, target, meta, final_norm, depth, layer_weights, pack_next=None, pack_grads=None):
    B, S, D = x.shape
    L = -(-(N_META + S) // Q_BLOCK) * Q_BLOCK
    h = jnp.concatenate([jnp.broadcast_to(meta[None], (B, N_META, D)), x,
                         jnp.zeros((B, L - N_META - S, D), F32)], axis=1).reshape(B * L, D)
    weights, params, saved = [], [], []
    gathered = None
    for l in range(depth):
        w = layer_weights(l, gathered)
        p = _layer_params(w)
        nxt = pack_next(l + 1) if (pack_next is not None and l + 1 < depth) else None
        h, s, gathered = _layer_fwd(h, p, B, L, nxt)
        weights.append(w)
        params.append(p)
        saved.append(s)
    tgt = jnp.pad(target, ((0, 0), (N_META, L - N_META - S), (0, 0))).reshape(B * L, D)
    dh, dhb, loss, dfinal = _loss_bwd(h, tgt, final_norm[None], L, S, "loss")
    grads = [None] * depth
    received, parts = {}, None
    for l in reversed(range(depth)):
        dh, dhb, grads[l], recv = _layer_bwd(dh, dhb, saved[l], params[l], weights[l], B, L, parts)
        if recv is not None:
            received[l + 1] = recv
        parts = pack_grads(grads[l]) if pack_grads is not None else None
    dh3 = dh.reshape(B, L, D)
    return (loss, dh3[:, N_META:N_META + S], jnp.sum(dh3[:, :N_META], axis=0), grads, dfinal[0], received, parts)


BIG_NAMES = ["ffn1_w_gate_up", "ffn1_w_down", "w_in", "w_branch_attn", "w_branch_ssd", "w_branch_lru", "w_out",
             "ffn2_w_gate_up", "ffn2_w_down"]
COL_SHARDED = {"ffn1_w_gate_up", "w_in", "ffn2_w_gate_up"}
SMALL_SHARDED = ["meta_tokens", "ssd_conv_w", "lru_conv_w"]
SMALL_NAMES = [n for n in LAYER_NAMES if n not in BIG_NAMES]


def _shard_rows(name, shape):
    return _padded(shape[1]) if name in COL_SHARDED else shape[0]


def _pack_shards(shards):
    rows = []
    for n in BIG_NAMES:
        s = shards[n]
        if n in COL_SHARDED:
            s = jnp.pad(s.T, ((0, _padded(s.shape[1]) - s.shape[1]), (0, 0)))
        rows.append(s)
    return jnp.concatenate(rows, axis=0)


def _unpack_gathered(gathered, shapes, D):
    out, o = {}, 0
    for n in BIG_NAMES:
        r = _shard_rows(n, shapes[n])
        out[n] = gathered[:, o:o + r].reshape(N_DEV * r, D)
        o += r
    out["w_in"] = _reorder_rows(out["w_in"], D, shapes["w_in"][1])
    return out


def _pack_full_grads(grads, shapes, D):
    slabs = []
    for n in BIG_NAMES:
        g = grads[n]
        if n == "w_in":
            g = _restore_rows(g, D, shapes[n][1])
        slabs.append(g.reshape(N_DEV, _shard_rows(n, shapes[n]), D))
    return jnp.concatenate(slabs, axis=1)


def _unpack_local(rows, shapes):
    out, o = {}, 0
    for n in BIG_NAMES:
        r = _shard_rows(n, shapes[n])
        blk = rows[o:o + r]
        out[n] = blk[:shapes[n][1]].T if n in COL_SHARDED else blk
        o += r
    return out


def _as_rows(flat):
    n = flat.shape[0]
    unit = LANES * SUBLANES
    total = -(-n // unit) * unit
    return jnp.pad(flat, (0, total - n)).reshape(total // LANES, LANES)


def _flatten_list(arrs):
    return _as_rows(jnp.concatenate([a.reshape(-1) for a in arrs]))


def _split_like(rows, shapes):
    flat = rows.reshape(-1)
    out, o = [], 0
    for s in shapes:
        n = math.prod(s)
        out.append(flat[o:o + n].reshape(s))
        o += n
    return out


def _gather_last(rows8, shape):
    lead, c = shape[:-1], shape[-1]
    t = rows8.reshape((N_DEV,) + tuple(lead) + (c,))
    return jnp.moveaxis(t, 0, -2).reshape(tuple(lead) + (N_DEV * c,))


def kernel(x, meta_tokens, ffn1_norm, ffn1_w_gate_up, ffn1_w_down, mix_norm, w_in, fox_forget_bias, ssd_conv_w, ssd_conv_b, ssd_dt_bias, ssd_a_log, ssd_d, ssd_norm, lru_conv_w, lru_conv_b, lru_w_a, lru_b_a, lru_w_x, lru_b_x, lru_lambda, w_branch_attn, w_branch_ssd, w_branch_lru, w_out, ffn2_norm, ffn2_w_gate_up, ffn2_w_down, final_norm, loss_target, m_meta_tokens, m_ffn1_norm, m_ffn1_w_gate_up, m_ffn1_w_down, m_mix_norm, m_w_in, m_fox_forget_bias, m_ssd_conv_w, m_ssd_conv_b, m_ssd_dt_bias, m_ssd_a_log, m_ssd_d, m_ssd_norm, m_lru_conv_w, m_lru_conv_b, m_lru_w_a, m_lru_b_a, m_lru_w_x, m_lru_b_x, m_lru_lambda, m_w_branch_attn, m_w_branch_ssd, m_w_branch_lru, m_w_out, m_ffn2_norm, m_ffn2_w_gate_up, m_ffn2_w_down, m_final_norm, v_meta_tokens, v_ffn1_norm, v_ffn1_w_gate_up, v_ffn1_w_down, v_mix_norm, v_w_in, v_fox_forget_bias, v_ssd_conv_w, v_ssd_conv_b, v_ssd_dt_bias, v_ssd_a_log, v_ssd_d, v_ssd_norm, v_lru_conv_w, v_lru_conv_b, v_lru_w_a, v_lru_b_a, v_lru_w_x, v_lru_b_x, v_lru_lambda, v_w_branch_attn, v_w_branch_ssd, v_w_branch_lru, v_w_out, v_ffn2_norm, v_ffn2_w_gate_up, v_ffn2_w_down, v_final_norm):
    weights = dict(zip(WEIGHT_NAMES, (meta_tokens, ffn1_norm, ffn1_w_gate_up, ffn1_w_down, mix_norm, w_in, fox_forget_bias, ssd_conv_w, ssd_conv_b, ssd_dt_bias, ssd_a_log, ssd_d, ssd_norm, lru_conv_w, lru_conv_b, lru_w_a, lru_b_a, lru_w_x, lru_b_x, lru_lambda, w_branch_attn, w_branch_ssd, w_branch_lru, w_out, ffn2_norm, ffn2_w_gate_up, ffn2_w_down, final_norm,)))
    mom1 = dict(zip(WEIGHT_NAMES, (m_meta_tokens, m_ffn1_norm, m_ffn1_w_gate_up, m_ffn1_w_down, m_mix_norm, m_w_in, m_fox_forget_bias, m_ssd_conv_w, m_ssd_conv_b, m_ssd_dt_bias, m_ssd_a_log, m_ssd_d, m_ssd_norm, m_lru_conv_w, m_lru_conv_b, m_lru_w_a, m_lru_b_a, m_lru_w_x, m_lru_b_x, m_lru_lambda, m_w_branch_attn, m_w_branch_ssd, m_w_branch_lru, m_w_out, m_ffn2_norm, m_ffn2_w_gate_up, m_ffn2_w_down, m_final_norm,)))
    mom2 = dict(zip(WEIGHT_NAMES, (v_meta_tokens, v_ffn1_norm, v_ffn1_w_gate_up, v_ffn1_w_down, v_mix_norm, v_w_in, v_fox_forget_bias, v_ssd_conv_w, v_ssd_conv_b, v_ssd_dt_bias, v_ssd_a_log, v_ssd_d, v_ssd_norm, v_lru_conv_w, v_lru_conv_b, v_lru_w_a, v_lru_b_a, v_lru_w_x, v_lru_b_x, v_lru_lambda, v_w_branch_attn, v_w_branch_ssd, v_w_branch_lru, v_w_out, v_ffn2_norm, v_ffn2_w_gate_up, v_ffn2_w_down, v_final_norm,)))
    depth = ffn1_norm.shape[0]
    D = x.shape[-1]
    my_idx = 4 * lax.axis_index("x") + 2 * lax.axis_index("y") + lax.axis_index("c")

    small_shapes = [weights[n].shape for n in SMALL_SHARDED]
    gathered = _all_gather(_flatten_list([weights[n] for n in SMALL_SHARDED]), "gather_small").reshape(N_DEV, -1)
    small_full, o = {}, 0
    for n, s in zip(SMALL_SHARDED, small_shapes):
        k = math.prod(s)
        small_full[n] = _gather_last(gathered[:, o:o + k], s)
        o += k

    shard_shapes = {n: weights[n].shape[1:] for n in BIG_NAMES}
    pack_next = lambda l: _pack_shards({n: weights[n][l].astype(BF16) for n in BIG_NAMES})

    def layer_weights(l, gathered):
        if gathered is None:
            gathered = _all_gather(pack_next(l), "gather_weights")
        w = _unpack_gathered(gathered, shard_shapes, D)
        for n in SMALL_NAMES:
            w[n] = small_full[n][l] if n in SMALL_SHARDED else weights[n][l]
        return w

    pack_grads = lambda g: _pack_full_grads(g, shard_shapes, D).astype(BF16)
    loss, dx, dmeta, grads, dfinal, received, parts = _local_step(
        x, loss_target, small_full["meta_tokens"], final_norm, depth, layer_weights, pack_next, pack_grads)
    received[0] = _exchange(parts, "exchange_grads")
    loss = lax.psum(loss[0, 0], ("x", "y", "c"))
    summed = {n: [] for n in WEIGHT_NAMES}
    for l in range(depth):
        local = _unpack_local(_sum8(received[l], "sum_grads"), shard_shapes)
        for n in BIG_NAMES:
            summed[n].append(local[n])

    small_list = [dmeta, dfinal] + [grads[l][n] for l in range(depth) for n in SMALL_NAMES]
    total = _sum8(_all_gather(_flatten_list(small_list), "gather_small_grads"), "sum_small_grads")
    parts = _split_like(total, [a.shape for a in small_list])
    full_small = {"meta_tokens": parts[0], "final_norm": parts[1]}
    for i, n in enumerate(SMALL_NAMES):
        full_small[n] = jnp.stack([parts[2 + l * len(SMALL_NAMES) + i] for l in range(depth)])
    grad = {}
    for n in WEIGHT_NAMES:
        if n in BIG_NAMES:
            grad[n] = jnp.stack(summed[n])
        elif n in SMALL_SHARDED:
            c = weights[n].shape[-1]
            grad[n] = lax.dynamic_slice_in_dim(full_small[n], my_idx * c, c, axis=full_small[n].ndim - 1)
        else:
            grad[n] = full_small[n]

    delta, new_m, new_v = {}, {}, {}
    for n in WEIGHT_NAMES:
        delta[n], new_m[n], new_v[n] = _adamw(weights[n], grad[n], mom1[n], mom2[n], "adamw_" + n)
    return (loss, dx, *[grad[n] for n in WEIGHT_NAMES], *[delta[n] for n in WEIGHT_NAMES],
            *[new_m[n] for n in WEIGHT_NAMES], *[new_v[n] for n in WEIGHT_NAMES])
```

```python
import functools
import math

import jax
import jax.numpy as jnp
from jax import lax
from jax.experimental import pallas as pl
from jax.experimental.pallas import tpu as pltpu

F32 = jnp.float32
BF16 = jnp.bfloat16

N_DEV = 8
N_META = 16
Q_BLOCK = 128
NORM_EPS = 1e-6
HEADS = 16
HEAD_DIM = 64
SSD_GROUPS = 2
SSD_STATE = 128
CONV_K = 4
LRU_C = 8.0
ADAM_LR, ADAM_B1, ADAM_B2, ADAM_EPS, ADAM_WD, ADAM_STEP = 0.001, 0.9, 0.999, 1e-08, 0.01, 10

LANES = 128
SUBLANES = 8
VMEM_LIMIT = 56 * 1024 * 1024
NEG = -1e30
MM_TILE = 1408
MM_VMEM = 40 * 1024 * 1024


def _cparams(sem=None):
    return pltpu.CompilerParams(dimension_semantics=sem, vmem_limit_bytes=VMEM_LIMIT)


def _tile(dim, target, mult=LANES):
    if dim <= target:
        return dim
    best = None
    for t in range(mult, target + 1, mult):
        if dim % t == 0:
            best = t
    assert best is not None, (dim, target)
    return best


def _sigmoid(x):
    return 1.0 / (1.0 + jnp.exp(-x))


def _log1p_exp_neg_abs(x):
    e = jnp.exp(-jnp.abs(x))
    u = 1.0 + e
    return jnp.where(u == 1.0, e, jnp.log(u) * (e / jnp.where(u == 1.0, 1.0, u - 1.0)))


def _log_sigmoid(x):
    return jnp.minimum(x, 0.0) - _log1p_exp_neg_abs(x)


def _softplus(x):
    return jnp.maximum(x, 0.0) + _log1p_exp_neg_abs(x)


def _one_minus_exp(y):
    u = jnp.exp(y)
    safe = jnp.where(u == 1.0, 0.5, u)
    return jnp.where(u == 1.0, -y, (1.0 - u) * y / jnp.log(safe))


def _silu(x):
    return x * _sigmoid(x)


def _dsilu(x):
    s = _sigmoid(x)
    return s * (1.0 + x * (1.0 - s))


_GELU_C = math.sqrt(2.0 / math.pi)


def _gelu(x):
    return 0.5 * x * (1.0 + jnp.tanh(_GELU_C * (x + 0.044715 * x * x * x)))


def _dgelu(x):
    t = jnp.tanh(_GELU_C * (x + 0.044715 * x * x * x))
    return 0.5 * (1.0 + t) + 0.5 * x * (1.0 - t * t) * _GELU_C * (1.0 + 3.0 * 0.044715 * x * x)


def _split3_dot(tri, x):
    hi = x.astype(BF16)
    r1 = x - hi.astype(F32)
    mid = r1.astype(BF16)
    lo = (r1 - mid.astype(F32)).astype(BF16)
    t = tri.astype(BF16)
    d = lambda p: jnp.dot(t, p, preferred_element_type=F32)
    return d(hi) + d(mid) + d(lo)


def _lower_tri(n, strict=False):
    r = lax.broadcasted_iota(jnp.int32, (n, n), 0)
    c = lax.broadcasted_iota(jnp.int32, (n, n), 1)
    return (c < r) if strict else (c <= r)


def _mm(a, b, *, ta=False, tb=False, out_dtype=F32, res=None, scale=None, tm=None, tn=None, tk=None,
        a_off=(0, 0), b_off=(0, 0), dims=None, name):
    if dims is None:
        M, K = (a.shape[1], a.shape[0]) if ta else a.shape
        N = b.shape[0] if tb else b.shape[1]
    else:
        M, N, K = dims
    tk = tk or _tile(K, 2816)
    nk_ = K // tk
    pick_m, pick_n = tm is None, tn is None
    tm = tm or _tile(M, MM_TILE)
    tn = tn or _tile(N, MM_TILE)

    def vmem(tm_, tn_):
        a_b = tm_ * tk * a.dtype.itemsize + (tm_ * tk * 2 if a.dtype != BF16 else 0)
        b_b = tn_ * tk * b.dtype.itemsize + (tn_ * tk * 2 if b.dtype != BF16 else 0)
        o_b = tm_ * tn_ * (jnp.dtype(out_dtype).itemsize + (4 if res is not None else 0))
        return 2 * (a_b + b_b + o_b) + (tm_ * tn_ * 4 if nk_ > 1 else 0) + tm_ * tn_ * 4

    while vmem(tm, tn) > MM_VMEM and (pick_m or pick_n):
        if pick_m and (tm >= tn or not pick_n) and tm > LANES:
            tm = _tile(M, tm - LANES)
        elif pick_n and tn > LANES:
            tn = _tile(N, tn - LANES)
        else:
            break
    assert M % tm == 0 and N % tn == 0 and K % tk == 0, (name, M, N, K, tm, tn, tk)
    nk = K // tk
    ca = 0 if ta else 1
    cb = 1 if tb else 0

    def blk(rows, cols, off):
        assert off[0] % rows == 0 and off[1] % cols == 0, (name, off, rows, cols)
        return off[0] // rows, off[1] // cols

    if ta:
        ao = blk(tk, tm, a_off)
        a_spec = pl.BlockSpec((tk, tm), lambda i, j, k: (k + ao[0], i + ao[1]))
    else:
        ao = blk(tm, tk, a_off)
        a_spec = pl.BlockSpec((tm, tk), lambda i, j, k: (i + ao[0], k + ao[1]))
    if tb:
        bo = blk(tn, tk, b_off)
        b_spec = pl.BlockSpec((tn, tk), lambda i, j, k: (j + bo[0], k + bo[1]))
    else:
        bo = blk(tk, tn, b_off)
        b_spec = pl.BlockSpec((tk, tn), lambda i, j, k: (k + bo[0], j + bo[1]))
    o_spec = pl.BlockSpec((tm, tn), lambda i, j, k: (i, j))
    in_specs = [a_spec, b_spec] + ([o_spec] if res is not None else [])
    has_res = res is not None

    def kern(*refs):
        if has_res:
            a_ref, b_ref, r_ref, o_ref = refs[:4]
            scr = refs[4:]
        else:
            a_ref, b_ref, o_ref = refs[:3]
            r_ref = None
            scr = refs[3:]
        p = lax.dot_general(a_ref[...].astype(BF16), b_ref[...].astype(BF16), (((ca,), (cb,)), ((), ())),
                            preferred_element_type=F32)

        def fin(val):
            if scale is not None:
                val = val * scale
            if has_res:
                val = r_ref[...] + val
            o_ref[...] = val.astype(out_dtype)

        if nk == 1:
            fin(p)
        else:
            acc = scr[0]
            k = pl.program_id(2)

            @pl.when(k == 0)
            def _():
                acc[...] = p

            @pl.when(k > 0)
            def _():
                acc[...] += p

            @pl.when(k == nk - 1)
            def _():
                fin(acc[...])

    args = (a, b) + ((res,) if has_res else ())
    return pl.pallas_call(
        kern, name=name, grid=(M // tm, N // tn, nk), in_specs=in_specs, out_specs=o_spec,
        out_shape=jax.ShapeDtypeStruct((M, N), out_dtype),
        scratch_shapes=[pltpu.VMEM((tm, tn), F32)] if nk > 1 else [],
        compiler_params=_cparams(("parallel", "parallel", "arbitrary")),
    )(*args)


def _rows(body, tiled, full, outs, accs, *, tr, name, T):
    assert T % tr == 0
    in_specs = []
    for arr, width, off in tiled:
        assert off % width == 0, (name, off, width)
        in_specs.append(pl.BlockSpec((tr, width), functools.partial(lambda i, o: (i, o), o=off // width)))
    for arr in full:
        in_specs.append(pl.BlockSpec(arr.shape, lambda i: (0, 0)))
    out_specs = [pl.BlockSpec((tr, w), lambda i: (i, 0)) for w, _ in outs]
    out_specs += [pl.BlockSpec(s, lambda i: (0, 0)) for s, _ in accs]
    out_shape = [jax.ShapeDtypeStruct((T, w), d) for w, d in outs] + [jax.ShapeDtypeStruct(s, d) for s, d in accs]
    nt, nf, no = len(tiled), len(full), len(outs)

    def kern(*refs):
        i = pl.program_id(0)
        acc_refs = refs[nt + nf + no:]

        @pl.when(i == 0)
        def _():
            for r in acc_refs:
                r[...] = jnp.zeros(r.shape, r.dtype)

        body(i, refs[:nt], refs[nt:nt + nf], refs[nt + nf:nt + nf + no], acc_refs)

    res = pl.pallas_call(
        kern, name=name, grid=(T // tr,), in_specs=in_specs, out_specs=out_specs, out_shape=out_shape,
        compiler_params=_cparams(("arbitrary",)),
    )(*[t[0] for t in tiled], *full)
    return res


def _colsum(x):
    return jnp.sum(x, axis=0, keepdims=True)


def _norm_fwd(h, g, name):
    T, D = h.shape

    def body(i, t, f, o, a):
        x = t[0][...]
        r = lax.rsqrt(jnp.mean(x * x, axis=-1, keepdims=True) + NORM_EPS)
        o[0][...] = (x * r * f[0][...]).astype(BF16)

    return _rows(body, [(h, D, 0)], [g], [(D, BF16)], [], tr=_tile(T, 768, 8), name=name, T=T)[0]


def _norm_bwd(h, dn, dh, g, name):
    T, D = h.shape

    def body(i, t, f, o, a):
        x, dnv, dhv = t[0][...], t[1][...], t[2][...]
        r = lax.rsqrt(jnp.mean(x * x, axis=-1, keepdims=True) + NORM_EPS)
        xh = x * r
        dng = dnv * f[0][...]
        out = dhv + r * (dng - xh * jnp.mean(dng * xh, axis=-1, keepdims=True))
        o[0][...] = out
        o[1][...] = out.astype(BF16)
        a[0][...] += _colsum(dnv * xh)

    return _rows(body, [(h, D, 0), (dn, D, 0), (dh, D, 0)], [g], [(D, F32), (D, BF16)], [((1, D), F32)],
                 tr=_tile(T, 384, 16), name=name, T=T)


FFN_TM = 768


def _ffn_up_act(n, wgu_t, name):
    T, D = n.shape
    F = wgu_t.shape[0] // 2
    tm, tn = _tile(T, FFN_TM), _tile(F, MM_TILE)
    nj = F // tn

    def kern(n_ref, wg_ref, wu_ref, g_ref, u_ref, a_ref):
        nv = n_ref[...]
        g = _dot_nt(nv, wg_ref[...])
        u = _dot_nt(nv, wu_ref[...])
        g_ref[...] = g
        u_ref[...] = u
        a_ref[...] = (_silu(g) * u).astype(BF16)

    out = pl.BlockSpec((tm, tn), lambda i, j: (i, j))
    return pl.pallas_call(
        kern, name=name, grid=(T // tm, nj),
        in_specs=[pl.BlockSpec((tm, D), lambda i, j: (i, 0)), pl.BlockSpec((tn, D), lambda i, j: (j, 0)),
                  pl.BlockSpec((tn, D), lambda i, j: (nj + j, 0))],
        out_specs=[out] * 3,
        out_shape=[jax.ShapeDtypeStruct((T, F), F32), jax.ShapeDtypeStruct((T, F), F32), jax.ShapeDtypeStruct((T, F), BF16)],
        compiler_params=_cparams(("parallel", "parallel")),
    )(n, wgu_t, wgu_t)


def _ffn_down_dx_act(dhb, wd, g, u, name):
    T, D = dhb.shape
    F = wd.shape[0]
    tm, tn = _tile(T, FFN_TM), _tile(F, MM_TILE)

    def kern(d_ref, w_ref, g_ref, u_ref, dg_ref, du_ref):
        da = _dot_nt(d_ref[...], w_ref[...]) * 0.5
        gv, uv = g_ref[...], u_ref[...]
        dg_ref[...] = (da * uv * _dsilu(gv)).astype(BF16)
        du_ref[...] = (da * _silu(gv)).astype(BF16)

    tile = pl.BlockSpec((tm, tn), lambda i, j: (i, j))
    return pl.pallas_call(
        kern, name=name, grid=(T // tm, F // tn),
        in_specs=[pl.BlockSpec((tm, D), lambda i, j: (i, 0)), pl.BlockSpec((tn, D), lambda i, j: (j, 0)), tile, tile],
        out_specs=[tile] * 2, out_shape=[jax.ShapeDtypeStruct((T, F), BF16)] * 2,
        compiler_params=_cparams(("parallel", "parallel")),
    )(dhb, wd, g, u)


def _merge_fwd(proj, off, pa, pb, pc, name):
    T, D = pa.shape

    def body(i, t, f, o, a):
        o[0][...] = (_sigmoid(t[0][...]) * t[3][...] + _sigmoid(t[1][...]) * t[4][...]
                     + _sigmoid(t[2][...]) * t[5][...]).astype(BF16)

    tiled = [(proj, D, off), (proj, D, off + D), (proj, D, off + 2 * D), (pa, D, 0), (pb, D, 0), (pc, D, 0)]
    return _rows(body, tiled, [], [(D, BF16)], [], tr=_tile(T, 384, 8), name=name, T=T)[0]


def _merge_bwd(dmixed, proj, off, pa, pb, pc, name):
    T, D = pa.shape

    def body(i, t, f, o, a):
        dm = t[0][...]
        for k in range(3):
            g = _sigmoid(t[1 + k][...])
            o[k][...] = (dm * g).astype(BF16)
            o[3][:, k * D:(k + 1) * D] = (dm * t[4 + k][...] * g * (1.0 - g)).astype(BF16)

    tiled = [(dmixed, D, 0), (proj, D, off), (proj, D, off + D), (proj, D, off + 2 * D), (pa, D, 0), (pb, D, 0),
             (pc, D, 0)]
    return _rows(body, tiled, [], [(D, BF16)] * 3 + [(3 * D, BF16)], [], tr=_tile(T, 384, 8), name=name, T=T)


def _gnorm_fwd(y, proj, zoff, nw, name):
    T, D = y.shape
    gs = D // SSD_GROUPS

    def body(i, t, f, o, a):
        s = t[0][...] * _silu(t[1][...])
        for g in range(SSD_GROUPS):
            sg = s[:, g * gs:(g + 1) * gs]
            r = lax.rsqrt(jnp.mean(sg * sg, axis=-1, keepdims=True) + NORM_EPS)
            o[0][:, g * gs:(g + 1) * gs] = (sg * r * f[0][:, g * gs:(g + 1) * gs]).astype(BF16)

    return _rows(body, [(y, D, 0), (proj, D, zoff)], [nw], [(D, BF16)], [], tr=_tile(T, 384, 8), name=name, T=T)[0]


def _gnorm_bwd(dout, y, proj, zoff, nw, name):
    T, D = y.shape
    gs = D // SSD_GROUPS

    def body(i, t, f, o, a):
        dov, yv, zv = t[0][...], t[1][...], t[2][...]
        sz = _silu(zv)
        s = yv * sz
        dsz = _dsilu(zv)
        for g in range(SSD_GROUPS):
            sl = slice(g * gs, (g + 1) * gs)
            sg = s[:, sl]
            r = lax.rsqrt(jnp.mean(sg * sg, axis=-1, keepdims=True) + NORM_EPS)
            sh = sg * r
            dog = dov[:, sl]
            dng = dog * f[0][:, sl]
            ds = r * (dng - sh * jnp.mean(dng * sh, axis=-1, keepdims=True))
            o[0][:, sl] = ds * sz[:, sl]
            o[1][:, sl] = (ds * yv[:, sl] * dsz[:, sl]).astype(BF16)
            a[0][:, sl] += _colsum(dog * sh)

    return _rows(body, [(dout, D, 0), (y, D, 0), (proj, D, zoff)], [nw], [(D, F32), (D, BF16)], [((1, D), F32)],
                 tr=_tile(T, 384, 8), name=name, T=T)


def _loss_bwd(h, tgt, g, seq_len, n_real, name):
    T, D = h.shape
    tr = _tile(seq_len, 384, 8)
    per_seq = seq_len // tr

    def body(i, t, f, o, a):
        x, tg = t[0][...], t[1][...]
        pos = (i % per_seq) * tr + lax.broadcasted_iota(jnp.int32, (tr, 1), 0)
        valid = (pos >= N_META) & (pos < N_META + n_real)
        r = lax.rsqrt(jnp.mean(x * x, axis=-1, keepdims=True) + NORM_EPS)
        xh = x * r
        e = jnp.where(valid, xh * f[0][...] - tg, 0.0)
        a[0][...] += jnp.zeros((1, LANES), F32) + 0.5 * jnp.sum(jnp.sum(e * e, axis=-1, keepdims=True) / D,
                                                              axis=0, keepdims=True)
        dy = e / D
        dng = dy * f[0][...]
        out = r * (dng - xh * jnp.mean(dng * xh, axis=-1, keepdims=True))
        o[0][...] = out
        o[1][...] = out.astype(BF16)
        a[1][...] += _colsum(dy * xh)

    return _rows(body, [(h, D, 0), (tgt, D, 0)], [g], [(D, F32), (D, BF16)], [((1, LANES), F32), ((1, D), F32)], tr=tr,
                 name=name, T=T)


def _lane_is_attn(shape):
    return lax.broadcasted_iota(jnp.int32, shape, len(shape) - 1) < HEADS


def _gate_prep(proj3, col_blk, bias, avec, name):
    B, L, _ = proj3.shape
    Q = Q_BLOCK
    nc = L // Q

    def kern(x_ref, b_ref, a_ref, v_ref, c_ref, carry):
        c = pl.program_id(1)

        @pl.when(c == 0)
        def _():
            carry[...] = jnp.zeros_like(carry)

        x = x_ref[0] + b_ref[...]
        attn = _lane_is_attn(x.shape)
        v = jnp.where(attn, _log_sigmoid(x), _softplus(x))
        w = jnp.where(attn, v, v * a_ref[...])
        cs = _split3_dot(_lower_tri(Q), w) + jnp.where(attn[:1], carry[...], 0.0)
        v_ref[0] = v
        c_ref[0] = cs
        rows = lax.broadcasted_iota(jnp.int32, (Q, 1), 0)
        carry[...] = jnp.sum(jnp.where(rows == Q - 1, cs, 0.0), axis=0, keepdims=True)

    blk = pl.BlockSpec((1, Q, LANES), lambda b, c: (b, c, 0))
    vec = pl.BlockSpec((1, LANES), lambda b, c: (0, 0))
    return pl.pallas_call(
        kern, name=name, grid=(B, nc),
        in_specs=[pl.BlockSpec((1, Q, LANES), lambda b, c: (b, c, col_blk)), vec, vec],
        out_specs=[blk, blk], out_shape=[jax.ShapeDtypeStruct((B, L, LANES), F32)] * 2,
        scratch_shapes=[pltpu.VMEM((1, LANES), F32)],
        compiler_params=_cparams(("parallel", "arbitrary")),
    )(proj3, bias, avec)


def _gate_post(drow, dcol, ddt, proj3, col_blk, vals, bias, avec, name):
    B, L, _ = proj3.shape
    Q = Q_BLOCK
    nc = L // Q

    def kern(dr_ref, dc_ref, dd_ref, x_ref, v_ref, b_ref, a_ref, o_ref, db_ref, da_ref, carry):
        b = pl.program_id(0)
        c = pl.program_id(1)

        @pl.when((b == 0) & (c == 0))
        def _():
            db_ref[...] = jnp.zeros_like(db_ref)
            da_ref[...] = jnp.zeros_like(da_ref)

        @pl.when(c == 0)
        def _():
            carry[...] = jnp.zeros_like(carry)

        x = x_ref[0] + b_ref[...]
        attn = _lane_is_attn(x.shape)
        dcs = dr_ref[0] + dc_ref[0]
        upper = jnp.logical_not(_lower_tri(Q, strict=True))
        rc = _split3_dot(upper, dcs) + jnp.where(attn[:1], carry[...], 0.0)
        rows = lax.broadcasted_iota(jnp.int32, (Q, 1), 0)
        carry[...] = jnp.sum(jnp.where(rows == 0, rc, 0.0), axis=0, keepdims=True)
        dv = jnp.where(attn, rc, dd_ref[0] + rc * a_ref[...])
        dpre = dv * jnp.where(attn, _sigmoid(-x), _sigmoid(x))
        o_ref[0] = dpre.astype(BF16)
        db_ref[...] += _colsum(dpre)
        da_ref[...] += _colsum(jnp.where(attn, 0.0, rc * v_ref[0]))

    rev = pl.BlockSpec((1, Q, LANES), lambda b, c: (b, nc - 1 - c, 0))
    vec = pl.BlockSpec((1, LANES), lambda b, c: (0, 0))
    return pl.pallas_call(
        kern, name=name, grid=(B, nc),
        in_specs=[rev, rev, rev, pl.BlockSpec((1, Q, LANES), lambda b, c: (b, nc - 1 - c, col_blk)), rev, vec, vec],
        out_specs=[rev, vec, vec],
        out_shape=[jax.ShapeDtypeStruct((B, L, LANES), BF16), jax.ShapeDtypeStruct((1, LANES), F32),
                   jax.ShapeDtypeStruct((1, LANES), F32)],
        scratch_shapes=[pltpu.VMEM((1, LANES), F32)],
        compiler_params=_cparams(("arbitrary", "arbitrary")),
    )(drow, dcol, ddt, proj3, vals, bias, avec)


def _lane_col(tile, lane):
    sel = lax.broadcasted_iota(jnp.int32, tile.shape, 1) == lane
    return jnp.sum(jnp.where(sel, tile, 0.0), axis=1, keepdims=True)


AUG = LANES
AUG_A = HEAD_DIM
AUG_B = HEAD_DIM + 3


def _split3(x):
    hi = x.astype(BF16).astype(F32)
    mid = (x - hi).astype(BF16).astype(F32)
    lo = (x - hi - mid).astype(BF16).astype(F32)
    return hi, mid, lo


def _put3(base, lane, first, x):
    hi, mid, lo = _split3(x)
    return jnp.where(lane == first, hi, jnp.where(lane == first + 1, mid, jnp.where(lane == first + 2, lo, base)))


HP = 2


def _other_half(x):
    return pltpu.roll(x, HEAD_DIM, 1)


def _loop_by_twos(n, step, init):
    carry = lax.fori_loop(0, n // 2, lambda t, c: step(2 * t + 1, step(2 * t, c)), init)
    return lax.cond(n % 2 == 1, lambda c: step(n - 1, c), lambda c: c, carry)


def _attn_pack(proj3, cums, name):
    B, L, _ = proj3.shape
    D = HEADS * HEAD_DIM
    nh = HEADS // HP
    tr = _tile(L, 384)
    scale = HEAD_DIM ** -0.5

    def kern(q_ref, k_ref, v_ref, c_ref, qa_ref, ka_ref, va_ref):
        lane = lax.broadcasted_iota(jnp.int32, (tr, AUG), 1)
        head = lane < HEAD_DIM
        ones_a = jnp.where((lane >= AUG_A) & (lane < AUG_A + 3), 1.0, 0.0)
        ones_b = jnp.where((lane >= AUG_B) & (lane < AUG_B + 3), 1.0, 0.0)
        ct = c_ref[0]
        for hp in range(nh):
            cols = slice(hp * LANES, (hp + 1) * LANES)
            for hh in range(HP):
                h = HP * hp + hh
                c = _lane_col(ct, h)
                sel = (lambda t: t) if hh == 0 else _other_half
                qa_ref[0, h] = jnp.where(head, sel(q_ref[0, :, cols]) * scale, _put3(ones_b, lane, AUG_A, c)).astype(BF16)
                ka_ref[0, h] = jnp.where(head, sel(k_ref[0, :, cols]), _put3(ones_a, lane, AUG_B, -c)).astype(BF16)
                va_ref[0, h] = jnp.where(head, sel(v_ref[0, :, cols]), ones_a).astype(BF16)

    def win(k):
        return pl.BlockSpec((1, tr, D), lambda b, i: (b, i, k))

    out = pl.BlockSpec((1, HEADS, tr, AUG), lambda b, i: (b, 0, i, 0))
    return pl.pallas_call(
        kern, name=name, grid=(B, L // tr),
        in_specs=[win(0), win(1), win(2), pl.BlockSpec((1, tr, LANES), lambda b, i: (b, i, 0))],
        out_specs=[out] * 3, out_shape=[jax.ShapeDtypeStruct((B, HEADS, L, AUG), BF16)] * 3,
        compiler_params=_cparams(("parallel", "parallel")),
    )(proj3, proj3, proj3, cums)


def _attn_fwd(qa, ka, va, name, gather=None):
    B, H, L, _ = qa.shape
    tq = _tile(L, 384)
    nq = L // tq
    nh = H // HP
    comm = gather is not None

    def kern(*refs):
        if comm:
            q_ref, k_ref, v_ref, x_ref, y_ref, yb_ref, l_ref, g_ref, send_sems, recv_sems, local_sem = refs
        else:
            q_ref, k_ref, v_ref, y_ref, yb_ref, l_ref = refs
        qi = pl.program_id(2)
        if comm:
            _ride((pl.program_id(0) * nh + pl.program_id(1)) * nq + qi, B * nh * nq,
                  _gather_phases(x_ref, g_ref, send_sems, recv_sems, local_sem))
        qs = [q_ref[0, hh] for hh in range(HP)]
        causal = _lower_tri(tq)

        def step(j, carry, masked):
            rows = pl.ds(pl.multiple_of(j * tq, tq), tq)
            out = []
            for hh in range(HP):
                m, acc = carry[hh]
                s = _dot_nt(qs[hh], k_ref[0, hh, rows, :])
                if masked:
                    s = jnp.where(causal, s, NEG)
                m_new = jnp.maximum(m, jnp.max(s, axis=1, keepdims=True))
                p = jnp.exp(s - m_new)
                out.append((m_new, jnp.exp(m - m_new) * acc + _dot(p.astype(BF16), v_ref[0, hh, rows, :])))
            return tuple(out)

        init = tuple((jnp.full((tq, 1), NEG, F32), jnp.zeros((tq, AUG), F32)) for _ in range(HP))
        carry = _loop_by_twos(qi, lambda j, c: step(j, c, False), init)
        outs = []
        for hh, (m, acc) in enumerate(step(qi, carry, True)):
            l = _lane_col(acc, AUG_A)
            outs.append(acc / l)
            l_ref[0, hh] = m + jnp.log(l)
        head = lax.broadcasted_iota(jnp.int32, (tq, AUG), 1) < HEAD_DIM
        y = jnp.where(head, outs[0], _other_half(outs[1]))
        y_ref[0] = y
        yb_ref[0] = y.astype(BF16)

    qspec = pl.BlockSpec((1, HP, tq, AUG), lambda b, h, i: (b, h, i, 0))
    kvspec = pl.BlockSpec((1, HP, L, AUG), lambda b, h, i: (b, h, 0, 0))
    lspec = pl.BlockSpec((1, HP, tq, 1), lambda b, h, i: (b, h, i, 0))
    yspec = pl.BlockSpec((1, tq, LANES), lambda b, h, i: (b, i, h))
    out_shape = [jax.ShapeDtypeStruct((B, L, H * HEAD_DIM), F32), jax.ShapeDtypeStruct((B, L, H * HEAD_DIM), BF16),
                 jax.ShapeDtypeStruct((B, H, L, 1), F32)]
    if comm:
        out_shape.append(jax.ShapeDtypeStruct((N_DEV,) + gather.shape, gather.dtype))
    return pl.pallas_call(
        kern, name=name, grid=(B, nh, nq), in_specs=[qspec, kvspec, kvspec] + ([ANY] if comm else []),
        out_specs=[yspec, yspec, lspec] + ([ANY] if comm else []), out_shape=out_shape,
        scratch_shapes=COMM_SCRATCH if comm else [],
        compiler_params=_cparams(("arbitrary",) * 3 if comm else ("parallel", "parallel", "arbitrary")),
    )(qa, ka, va, *([gather] if comm else []))


def _attn_bwd(qa, ka, va, y, dy, lse, name, parts=None):
    B, H, L, _ = qa.shape
    tq = _tile(L, 384)
    nq = L // tq
    nh = H // HP
    comm = parts is not None
    scale = HEAD_DIM ** -0.5

    def kern(*refs):
        if comm:
            (q_ref, k_ref, v_ref, y_ref, dy_ref, l_ref, p_ref, dq_ref, dk_ref, dv_ref, dc_ref, r_ref,
             dk_acc, dv_acc, send_sems, recv_sems, local_sem) = refs
        else:
            q_ref, k_ref, v_ref, y_ref, dy_ref, l_ref, dq_ref, dk_ref, dv_ref, dc_ref, dk_acc, dv_acc = refs
        qi = pl.program_id(2)
        hp = pl.program_id(1)
        lane_row = lax.broadcasted_iota(jnp.int32, (1, LANES), 1)
        onehot = [(lane_row == HP * hp + hh).astype(F32) for hh in range(HP)]

        @pl.when((hp == 0) & (qi == 0))
        def _():
            dc_ref[...] = jnp.zeros_like(dc_ref)

        if comm:
            _ride((pl.program_id(0) * nh + pl.program_id(1)) * nq + qi, B * nh * nq,
                  _exchange_phases(p_ref, r_ref, send_sems, recv_sems, local_sem))

        @pl.when(qi == 0)
        def _():
            dk_acc[...] = jnp.zeros_like(dk_acc)
            dv_acc[...] = jnp.zeros_like(dv_acc)

        lane = lax.broadcasted_iota(jnp.int32, (tq, AUG), 1)
        head = lane < HEAD_DIM
        qbs, dobs = [], []
        for hh in range(HP):
            sel = (lambda t: t) if hh == 0 else _other_half
            qf = q_ref[0, hh].astype(F32)
            dov = jnp.where(head, sel(dy_ref[0]), 0.0)
            dsum = jnp.sum(dov * sel(y_ref[0]), axis=1, keepdims=True)
            dobs.append(_put3(dov, lane, AUG_A, -dsum).astype(BF16))
            c_t = jnp.sum(jnp.where((lane >= AUG_A) & (lane < AUG_A + 3), qf, 0.0), axis=1, keepdims=True)
            qbs.append(_put3(qf, lane, AUG_A, c_t - l_ref[0, hh]).astype(BF16))
        causal = _lower_tri(tq)

        def step(j, dqs, masked):
            rows = pl.ds(pl.multiple_of(j * tq, tq), tq)
            out = []
            for hh in range(HP):
                kj = k_ref[0, hh, rows, :]
                s = _dot_nt(qbs[hh], kj)
                if masked:
                    s = jnp.where(causal, s, NEG)
                p = jnp.exp(s)
                ds = (p * _dot_nt(dobs[hh], v_ref[0, hh, rows, :])).astype(BF16)
                dv_acc[hh, rows, :] += _dot_tn(p.astype(BF16), dobs[hh])
                dk_acc[hh, rows, :] += _dot_tn(ds, qbs[hh])
                out.append(dqs[hh] + _dot(ds, kj))
            return tuple(out)

        dqs = _loop_by_twos(qi, lambda j, c: step(j, c, False), tuple(jnp.zeros((tq, AUG), F32) for _ in range(HP)))
        dqs = step(qi, dqs, True)
        dc_ref[0, pl.ds(pl.multiple_of(qi * tq, tq), tq), :] += (_lane_col(dqs[0], AUG_A) * onehot[0]
                                                                 + _lane_col(dqs[1], AUG_A) * onehot[1])
        dq_ref[0] = (jnp.where(head, dqs[0], _other_half(dqs[1])) * scale).astype(BF16)

        @pl.when(qi == nq - 1)
        def _():
            full = lax.broadcasted_iota(jnp.int32, (L, AUG), 1) < HEAD_DIM
            dk_ref[0] = jnp.where(full, dk_acc[0], _other_half(dk_acc[1])).astype(BF16)
            dv_ref[0] = jnp.where(full, dv_acc[0], _other_half(dv_acc[1])).astype(BF16)
            dc_ref[0] -= _lane_col(dk_acc[0], AUG_B) * onehot[0] + _lane_col(dk_acc[1], AUG_B) * onehot[1]

    qspec = pl.BlockSpec((1, HP, tq, AUG), lambda b, h, i: (b, h, i, 0))
    kvspec = pl.BlockSpec((1, HP, L, AUG), lambda b, h, i: (b, h, 0, 0))
    lspec = pl.BlockSpec((1, HP, tq, 1), lambda b, h, i: (b, h, i, 0))
    tmspec = pl.BlockSpec((1, L, LANES), lambda b, h, i: (b, 0, 0))
    yspec = pl.BlockSpec((1, tq, LANES), lambda b, h, i: (b, i, h))
    yfull = pl.BlockSpec((1, L, LANES), lambda b, h, i: (b, 0, h))
    nat = jax.ShapeDtypeStruct((B, L, H * HEAD_DIM), BF16)
    out_shape = [nat, nat, nat, jax.ShapeDtypeStruct((B, L, LANES), F32)]
    if comm:
        out_shape.append(jax.ShapeDtypeStruct(parts.shape, parts.dtype))
    return pl.pallas_call(
        kern, name=name, grid=(B, nh, nq),
        in_specs=[qspec, kvspec, kvspec, yspec, yspec, lspec] + ([ANY] if comm else []),
        out_specs=[yspec, yfull, yfull, tmspec] + ([ANY] if comm else []), out_shape=out_shape,
        scratch_shapes=[pltpu.VMEM((HP, L, AUG), F32), pltpu.VMEM((HP, L, AUG), F32)] + (COMM_SCRATCH if comm else []),
        compiler_params=_cparams(("parallel", "arbitrary", "arbitrary")),
    )(qa, ka, va, y, dy, lse, *([parts] if comm else []))


PAD = SUBLANES


def _halo_tile(x_ref, i, TR):
    r0 = pl.multiple_of(i * TR, TR)
    before = x_ref[0, pl.ds(pl.multiple_of(jnp.maximum(r0 - PAD, 0), PAD), PAD), :]
    return jnp.concatenate([jnp.where(i > 0, before, 0.0), x_ref[0, pl.ds(r0, TR), :]], axis=0)


def _conv_fwd(x3, x_blk, w, b, n_silu, name):
    B, L, _ = x3.shape
    C = w.shape[1]
    TR = _tile(L, 384, 8)

    def kern(x_ref, w_ref, b_ref, o_ref):
        cb = pl.program_id(1)

        def body(i, carry):
            r0 = pl.multiple_of(i * TR, TR)
            ext = _halo_tile(x_ref, i, TR)
            acc = jnp.zeros((TR, LANES), F32) + b_ref[...]
            for k in range(CONV_K):
                s = CONV_K - 1 - k
                sh = ext if s == 0 else pltpu.roll(ext, s, 0)
                acc = acc + w_ref[k:k + 1, :] * sh[PAD:PAD + TR]
            o_ref[0, pl.ds(r0, TR), :] = jnp.where(cb < n_silu, _silu(acc), acc)
            return carry

        lax.fori_loop(0, L // TR, body, 0)

    return pl.pallas_call(
        kern, name=name, grid=(B, C // LANES),
        in_specs=[pl.BlockSpec((1, L, LANES), lambda b_, c: (b_, 0, x_blk + c)),
                  pl.BlockSpec((CONV_K, LANES), lambda b_, c: (0, c)), pl.BlockSpec((1, LANES), lambda b_, c: (0, c))],
        out_specs=pl.BlockSpec((1, L, LANES), lambda b_, c: (b_, 0, c)),
        out_shape=jax.ShapeDtypeStruct((B, L, C), F32),
        compiler_params=_cparams(("parallel", "parallel")),
    )(x3, w, b)


def _conv_bwd_pre(x3, x_blk, du, w, b, n_silu, name):
    B, L, C = du.shape
    Lp = L + PAD
    TR = _tile(L, 384, 8)

    def kern(x_ref, du_ref, w_ref, b_ref, dp_ref, dw_ref):
        cb = pl.program_id(0)

        @pl.when(pl.program_id(1) == 0)
        def _():
            dw_ref[...] = jnp.zeros_like(dw_ref)

        def body(i, carry):
            r0 = pl.multiple_of(i * TR, TR)
            ext = _halo_tile(x_ref, i, TR)
            taps = []
            acc = jnp.zeros((TR, LANES), F32) + b_ref[...]
            for k in range(CONV_K):
                s = CONV_K - 1 - k
                sh = ext if s == 0 else pltpu.roll(ext, s, 0)
                taps.append(sh[PAD:PAD + TR])
                acc = acc + w_ref[k:k + 1, :] * taps[-1]
            dv = du_ref[0, pl.ds(r0, TR), :]
            dpre = jnp.where(cb < n_silu, dv * _dsilu(acc), dv)
            dp_ref[0, pl.ds(r0, TR), :] = dpre
            return tuple(c + _colsum(dpre * t) for c, t in zip(carry[:CONV_K], taps)) + (carry[CONV_K] + _colsum(dpre),)

        z = jnp.zeros((1, LANES), F32)
        sums = lax.fori_loop(0, L // TR, body, (z,) * (CONV_K + 1))
        dp_ref[0, pl.ds(L, PAD), :] = jnp.zeros((PAD, LANES), F32)
        for k in range(CONV_K + 1):
            dw_ref[k:k + 1, :] += sums[k]

    return pl.pallas_call(
        kern, name=name, grid=(C // LANES, B),
        in_specs=[pl.BlockSpec((1, L, LANES), lambda c, b_: (b_, 0, x_blk + c)),
                  pl.BlockSpec((1, L, LANES), lambda c, b_: (b_, 0, c)),
                  pl.BlockSpec((CONV_K, LANES), lambda c, b_: (0, c)), pl.BlockSpec((1, LANES), lambda c, b_: (0, c))],
        out_specs=[pl.BlockSpec((1, Lp, LANES), lambda c, b_: (b_, 0, c)),
                   pl.BlockSpec((SUBLANES, LANES), lambda c, b_: (0, c))],
        out_shape=[jax.ShapeDtypeStruct((B, Lp, C), F32), jax.ShapeDtypeStruct((SUBLANES, C), F32)],
        compiler_params=_cparams(("parallel", "arbitrary")),
    )(x3, du, w, b)


def _conv_bwd_in(dpp, w, name):
    B, Lp, C = dpp.shape
    L = Lp - PAD
    TR = _tile(L, 384, 16)

    def kern(d_ref, w_ref, o_ref):
        def body(i, carry):
            r0 = pl.multiple_of(i * TR, TR)
            ext = d_ref[0, pl.ds(r0, TR + PAD), :]
            acc = jnp.zeros((TR, LANES), F32)
            for k in range(CONV_K):
                s = CONV_K - 1 - k
                sh = ext if s == 0 else pltpu.roll(ext, TR + PAD - s, 0)
                acc = acc + w_ref[k:k + 1, :] * sh[0:TR]
            o_ref[0, pl.ds(r0, TR), :] = acc.astype(BF16)
            return carry

        lax.fori_loop(0, L // TR, body, 0)

    return pl.pallas_call(
        kern, name=name, grid=(B, C // LANES),
        in_specs=[pl.BlockSpec((1, Lp, LANES), lambda b_, c: (b_, 0, c)),
                  pl.BlockSpec((CONV_K, LANES), lambda b_, c: (0, c))],
        out_specs=pl.BlockSpec((1, L, LANES), lambda b_, c: (b_, 0, c)),
        out_shape=jax.ShapeDtypeStruct((B, L, C), BF16),
        compiler_params=_cparams(("parallel", "parallel")),
    )(dpp, w)


def _dot_nt(a, b):
    return lax.dot_general(a, b, (((1,), (1,)), ((), ())), preferred_element_type=F32)


def _dot_tn(a, b):
    return lax.dot_general(a, b, (((0,), (0,)), ((), ())), preferred_element_type=F32)


def _dot(a, b):
    return jnp.dot(a, b, preferred_element_type=F32)


def _ssd_specs(L, nc, b_blk, c_blk):
    pairs_per_group = HEADS // SSD_GROUPS // HP
    return [
        pl.BlockSpec((1, L, LANES), lambda b, h: (b, 0, h)),
        pl.BlockSpec((1, L, SSD_STATE), lambda b, h: (b, 0, b_blk + h // pairs_per_group)),
        pl.BlockSpec((1, L, SSD_STATE), lambda b, h: (b, 0, c_blk + h // pairs_per_group)),
        pl.BlockSpec((1, L, LANES), lambda b, h: (b, 0, 0)),
        pl.BlockSpec((1, L, LANES), lambda b, h: (b, 0, 0)),
        pl.BlockSpec((1, HP, nc, Q_BLOCK), lambda b, h: (b, HEADS // HP + h, 0, 0)),
        pl.BlockSpec((1, LANES), lambda b, h: (0, 0)),
    ]


def _halves(a, b, shape):
    return jnp.where(lax.broadcasted_iota(jnp.int32, shape, 1) < HEAD_DIM, a, b)


def _half_sums(t):
    first = lax.broadcasted_iota(jnp.int32, t.shape, 1) < HEAD_DIM
    lo = jnp.sum(jnp.where(first, t, 0.0), axis=1, keepdims=True)
    return lo, jnp.sum(t, axis=1, keepdims=True) - lo


def _ssd_chunk(c, S, x_ref, b_ref, c_ref, v_ref, cu_ref, ct_ref, lane0):
    Q = Q_BLOCK
    rows = pl.ds(pl.multiple_of(c * Q, Q), Q)
    x = x_ref[0, rows, :]
    Bb = b_ref[0, rows, :].astype(BF16)
    Cb = c_ref[0, rows, :].astype(BF16)
    vt, ct = v_ref[0, rows, :], cu_ref[0, rows, :]
    tri = _lower_tri(Q)
    A, Lm, e_end_h, eAend_h, dts = [], [], [], [], []
    for hh in range(HP):
        dts.append(_lane_col(vt, lane0 + hh))
        A.append(_lane_col(ct, lane0 + hh))
        Ar = ct_ref[0, hh, pl.ds(c, 1), :]
        Aend = _lane_col(Ar, Q - 1)
        Lm.append(jnp.exp(jnp.where(tri, A[hh] - Ar, NEG)))
        e_end_h.append(jnp.exp(Aend - A[hh]))
        eAend_h.append(jnp.exp(Aend))
    shape = (Q, LANES)
    dt = _halves(dts[0], dts[1], shape)
    eA = _halves(jnp.exp(A[0]), jnp.exp(A[1]), shape)
    e_end = _halves(e_end_h[0], e_end_h[1], shape)
    xdt = x * dt
    CB = _dot_nt(Cb, Bb)
    W = xdt * e_end
    srow = lax.broadcasted_iota(jnp.int32, (HP * HEAD_DIM, 1), 0) < HEAD_DIM
    eAend = jnp.where(srow, eAend_h[0], eAend_h[1])
    S_new = S * eAend + _dot_tn(W.astype(BF16), Bb)
    return dict(rows=rows, x=x, Bb=Bb, Cb=Cb, dt=dt, eA=eA, e_end=e_end, e_end_h=e_end_h, eAend=eAend,
                eAend_h=eAend_h, xdt=xdt, Lm=Lm, CB=CB, W=W, S_new=S_new)


def _ssd_fwd(u, b_blk, c_blk, vals, cums, cums_t, dvec, name):
    B, L, _ = u.shape
    nc = L // Q_BLOCK
    nh = HEADS // HP

    def kern(x_ref, b_ref, c_ref, v_ref, cu_ref, ct_ref, d_ref, y_ref):
        lane0 = HEADS + HP * pl.program_id(1)
        dskip = _halves(_lane_col(d_ref[...], lane0), _lane_col(d_ref[...], lane0 + 1), (1, LANES))
        first = lax.broadcasted_iota(jnp.int32, (Q_BLOCK, LANES), 1) < HEAD_DIM

        def body(c, S):
            q = _ssd_chunk(c, S, x_ref, b_ref, c_ref, v_ref, cu_ref, ct_ref, lane0)
            xb = q["xdt"].astype(BF16)
            yd = jnp.where(first, _dot((q["CB"] * q["Lm"][0]).astype(BF16), xb),
                           _dot((q["CB"] * q["Lm"][1]).astype(BF16), xb))
            z = _dot_nt(q["Cb"], S.astype(BF16))
            y_ref[0, q["rows"], :] = yd + z * q["eA"] + dskip * q["x"]
            return q["S_new"]

        lax.fori_loop(0, nc, body, jnp.zeros((HP * HEAD_DIM, SSD_STATE), F32))

    return pl.pallas_call(
        kern, name=name, grid=(B, nh), in_specs=_ssd_specs(L, nc, b_blk, c_blk),
        out_specs=pl.BlockSpec((1, L, LANES), lambda b, h: (b, 0, h)),
        out_shape=jax.ShapeDtypeStruct((B, L, HEADS * HEAD_DIM), F32),
        compiler_params=_cparams(("parallel", "arbitrary")),
    )(u, u, u, vals, cums, cums_t, dvec)


def _ssd_bwd(u, b_blk, c_blk, vals, cums, cums_t, dvec, dy, name):
    B, L, _ = u.shape
    Q = Q_BLOCK
    nc = L // Q
    N = SSD_STATE
    nh = HEADS // HP
    pairs_per_group = HEADS // SSD_GROUPS // HP
    PP = HP * HEAD_DIM

    def kern(x_ref, b_ref, c_ref, v_ref, cu_ref, ct_ref, d_ref, dy_ref,
             dx_ref, dB_ref, dC_ref, ddt_ref, dAc_ref, dAr_ref, dD_ref, s_all):
        b = pl.program_id(0)
        h = pl.program_id(1)
        lane0 = HEADS + HP * h
        dskip = _halves(_lane_col(d_ref[...], lane0), _lane_col(d_ref[...], lane0 + 1), (1, LANES))
        lane_row = lax.broadcasted_iota(jnp.int32, (1, LANES), 1)
        onehot = [(lane_row == lane0 + hh).astype(F32) for hh in range(HP)]

        @pl.when(h % pairs_per_group == 0)
        def _():
            dB_ref[...] = jnp.zeros_like(dB_ref)
            dC_ref[...] = jnp.zeros_like(dC_ref)

        @pl.when(h == 0)
        def _():
            ddt_ref[...] = jnp.zeros_like(ddt_ref)
            dAc_ref[...] = jnp.zeros_like(dAc_ref)

        @pl.when((b == 0) & (h == 0))
        def _():
            dD_ref[...] = jnp.zeros_like(dD_ref)

        def fwd(c, S):
            s_all[c] = S
            return _ssd_chunk(c, S, x_ref, b_ref, c_ref, v_ref, cu_ref, ct_ref, lane0)["S_new"]

        lax.fori_loop(0, nc, fwd, jnp.zeros((PP, N), F32))
        last_row = lax.broadcasted_iota(jnp.int32, (Q, 1), 0) == Q - 1
        first = lax.broadcasted_iota(jnp.int32, (Q, LANES), 1) < HEAD_DIM
        srow = lax.broadcasted_iota(jnp.int32, (PP, 1), 0) < HEAD_DIM

        def bwd(i, carry):
            dS, dD = carry
            c = nc - 1 - i
            S = s_all[c]
            q = _ssd_chunk(c, S, x_ref, b_ref, c_ref, v_ref, cu_ref, ct_ref, lane0)
            rows, x, Bb, Cb, xdt, Lm, CB = q["rows"], q["x"], q["Bb"], q["Cb"], q["xdt"], q["Lm"], q["CB"]
            dy = dy_ref[0, rows, :]
            dyb = dy.astype(BF16)
            xb = xdt.astype(BF16)
            Sb = S.astype(BF16)
            dD = dD + _colsum(dy * x)
            dyh = [jnp.where(first, dy, 0.0).astype(BF16), jnp.where(first, 0.0, dy).astype(BF16)]
            dM = [_dot_nt(dyh[hh], xb) for hh in range(HP)]
            dxdt = jnp.where(first, _dot_tn((CB * Lm[0]).astype(BF16), dyb), _dot_tn((CB * Lm[1]).astype(BF16), dyb))
            dCBb = (dM[0] * Lm[0] + dM[1] * Lm[1]).astype(BF16)
            dAc, dAr = [], []
            for hh in range(HP):
                G = dM[hh] * CB * Lm[hh]
                dAc.append(jnp.sum(G, axis=1, keepdims=True))
                dAr.append(-jnp.sum(G, axis=0, keepdims=True))
            dC = _dot(dCBb, Bb)
            dBm = _dot_tn(dCBb, Cb)
            z = _dot_nt(Cb, Sb)
            zs = _half_sums(dy * z)
            dzb = (dy * q["eA"]).astype(BF16)
            dC = dC + _dot(dzb, Sb)
            dS_in = _dot_tn(dzb, Cb)
            dSb = dS.astype(BF16)
            dW = _dot_nt(Bb, dSb)
            dBm = dBm + _dot(q["W"].astype(BF16), dSb)
            dxdt = dxdt + dW * q["e_end"]
            des = _half_sums(dW * xdt)
            ss = jnp.sum(dS * S, axis=1, keepdims=True)
            ss_lo = jnp.sum(jnp.where(srow, ss, 0.0), axis=0, keepdims=True)
            ss_h = [ss_lo, jnp.sum(ss, axis=0, keepdims=True) - ss_lo]
            ddts = _half_sums(dxdt * x)
            eA_h = [_lane_col(q["eA"], 0), _lane_col(q["eA"], HEAD_DIM)]
            dAc_tile = jnp.zeros((Q, LANES), F32)
            ddt_tile = jnp.zeros((Q, LANES), F32)
            for hh in range(HP):
                de = des[hh] * q["e_end_h"][hh]
                dAend = ss_h[hh] * q["eAend_h"][hh] + jnp.sum(de, axis=0, keepdims=True)
                col = dAc[hh] + zs[hh] * eA_h[hh] - de + jnp.where(last_row, dAend, 0.0)
                dAc_tile = dAc_tile + col * onehot[hh]
                ddt_tile = ddt_tile + ddts[hh] * onehot[hh]
                dAr_ref[0, hh, pl.ds(c, 1), :] = dAr[hh]
            dx_ref[0, rows, :] = dskip * dy + dxdt * q["dt"]
            dB_ref[0, 0, rows, :] += dBm
            dC_ref[0, 0, rows, :] += dC
            ddt_ref[0, rows, :] += ddt_tile
            dAc_ref[0, rows, :] += dAc_tile
            return dS * q["eAend"] + dS_in, dD

        _, dD = lax.fori_loop(0, nc, bwd, (jnp.zeros((PP, N), F32), jnp.zeros((1, LANES), F32)))
        dlo, dhi = _half_sums(dD)
        dD_ref[...] += dlo * onehot[0] + dhi * onehot[1]

    tm = pl.BlockSpec((1, L, LANES), lambda b, h: (b, 0, 0))
    grp = pl.BlockSpec((1, 1, L, N), lambda b, h: (b, h // pairs_per_group, 0, 0))
    xs = pl.BlockSpec((1, L, LANES), lambda b, h: (b, 0, h))
    return pl.pallas_call(
        kern, name=name, grid=(B, nh), in_specs=_ssd_specs(L, nc, b_blk, c_blk) + [xs],
        out_specs=[xs, grp, grp, tm, tm, pl.BlockSpec((1, HP, nc, Q), lambda b, h: (b, h, 0, 0)),
                   pl.BlockSpec((1, LANES), lambda b, h: (0, 0))],
        out_shape=[jax.ShapeDtypeStruct((B, L, HEADS * HEAD_DIM), F32), jax.ShapeDtypeStruct((B, SSD_GROUPS, L, N), F32),
                   jax.ShapeDtypeStruct((B, SSD_GROUPS, L, N), F32), jax.ShapeDtypeStruct((B, L, LANES), F32),
                   jax.ShapeDtypeStruct((B, L, LANES), F32), jax.ShapeDtypeStruct((B, HEADS, nc, Q), F32),
                   jax.ShapeDtypeStruct((1, LANES), F32)],
        scratch_shapes=[pltpu.VMEM((nc, PP, N), F32)],
        compiler_params=_cparams(("arbitrary", "arbitrary")),
    )(u, u, u, vals, cums, cums_t, dvec, dy)


LRU_TR = 384
LRU_CB = 512


def _lru_gates(xc, ra, ix, p_ref, first):
    r = _sigmoid(ra + p_ref[0:1, :])
    i = _sigmoid(ix + p_ref[1:2, :])
    ls = _log_sigmoid(p_ref[2:3, :])
    log_a = LRU_C * r * ls
    a = jnp.exp(log_a)
    mult0 = jnp.sqrt(_one_minus_exp(2.0 * log_a))
    mult = jnp.where(first, 1.0, mult0)
    return r, i, ls, a, mult0, mult


def _lru_fwd(u, xc_off, ra, ix, proj3, gate_off, pvec, name):
    B, L, D = ra.shape
    TR, CB = _tile(L, LRU_TR, 8), LRU_CB
    nrt = L // TR

    def kern(xc_ref, ra_ref, ix_ref, g_ref, p_ref, y_ref, hs_ref, a_ref, pa_s, pu_s, carry):
        rt = pl.program_id(2)

        @pl.when(rt == 0)
        def _():
            carry[...] = jnp.zeros_like(carry)

        row = lax.broadcasted_iota(jnp.int32, (TR, 1), 0)
        first = (rt == 0) & (row == 0)
        xc = xc_ref[0]
        r, i, ls, a, mult0, mult = _lru_gates(xc, ra_ref[0], ix_ref[0], p_ref, first)
        a_ref[0] = a
        pa, pu = a, mult * (i * xc)
        sub = row % SUBLANES
        for s in (1, 2, 4):
            ok = sub >= s
            pu = jnp.where(ok, pa * pltpu.roll(pu, s, 0) + pu, pu)
            pa = jnp.where(ok, pa * pltpu.roll(pa, s, 0), pa)
        pa_s[...] = pa
        pu_s[...] = pu
        row8 = lax.broadcasted_iota(jnp.int32, (SUBLANES, 1), 0)

        def gbody(g, c):
            r8 = pl.ds(pl.multiple_of(g * SUBLANES, SUBLANES), SUBLANES)
            hg = pa_s[r8, :] * c + pu_s[r8, :]
            hs_ref[0, r8, :] = hg
            return jnp.sum(jnp.where(row8 == SUBLANES - 1, hg, 0.0), axis=0, keepdims=True)

        carry[...] = lax.fori_loop(0, TR // SUBLANES, gbody, carry[...])
        y_ref[0] = (hs_ref[0] * _gelu(g_ref[0])).astype(BF16)

    def win(off):
        assert off % CB == 0
        return pl.BlockSpec((1, TR, CB), functools.partial(lambda b, j, t, o: (b, t, j + o), o=off // CB))

    return pl.pallas_call(
        kern, name=name, grid=(B, D // CB, nrt),
        in_specs=[win(xc_off), win(0), win(0), win(gate_off), pl.BlockSpec((SUBLANES, CB), lambda b, j, t: (0, j))],
        out_specs=[win(0)] * 3,
        out_shape=[jax.ShapeDtypeStruct((B, L, D), BF16), jax.ShapeDtypeStruct((B, L, D), F32),
                   jax.ShapeDtypeStruct((B, L, D), F32)],
        scratch_shapes=[pltpu.VMEM((TR, CB), F32), pltpu.VMEM((TR, CB), F32), pltpu.VMEM((1, CB), F32)],
        compiler_params=_cparams(("parallel", "parallel", "arbitrary")),
    )(u, ra, ix, proj3, pvec)


def _lru_bwd(dy, proj3, gate_off, hs, a, u, xc_off, ra, ix, pvec, name):
    B, L, D = ra.shape
    TR, CB = _tile(L, LRU_TR, 8), LRU_CB
    nrt = L // TR

    def kern(dy_ref, g_ref, hs_ref, hsp_ref, a_ref, an_ref, xc_ref, ra_ref, ix_ref, p_ref,
             dg_ref, dra_ref, dix_ref, dxc_ref, dp_ref, pb_s, pd_s, g_s, carry):
        b = pl.program_id(1)
        rt = pl.program_id(2)
        t = nrt - 1 - rt

        @pl.when((b == 0) & (rt == 0))
        def _():
            dp_ref[...] = jnp.zeros_like(dp_ref)

        @pl.when(rt == 0)
        def _():
            carry[...] = jnp.zeros_like(carry)

        row = lax.broadcasted_iota(jnp.int32, (TR, 1), 0)
        gate, hsv, av, dyv = g_ref[0], hs_ref[0], a_ref[0], dy_ref[0]
        dg_ref[0] = (dyv * hsv * _dgelu(gate)).astype(BF16)
        a_next = jnp.where(t == nrt - 1, 0.0, an_ref[0, 0:1, :])
        pb = jnp.where(row == TR - 1, a_next, pltpu.roll(av, TR - 1, 0))
        pd = dyv * _gelu(gate)
        sub = row % SUBLANES
        for s in (1, 2, 4):
            ok = sub < SUBLANES - s
            pd = jnp.where(ok, pd + pb * pltpu.roll(pd, TR - s, 0), pd)
            pb = jnp.where(ok, pb * pltpu.roll(pb, TR - s, 0), pb)
        pb_s[...] = pb
        pd_s[...] = pd
        row8 = lax.broadcasted_iota(jnp.int32, (SUBLANES, 1), 0)

        def gbody(i, c):
            r8 = pl.ds(pl.multiple_of((TR // SUBLANES - 1 - i) * SUBLANES, SUBLANES), SUBLANES)
            gg = pd_s[r8, :] + pb_s[r8, :] * c
            g_s[r8, :] = gg
            return jnp.sum(jnp.where(row8 == 0, gg, 0.0), axis=0, keepdims=True)

        carry[...] = lax.fori_loop(0, TR // SUBLANES, gbody, carry[...])
        gv = g_s[...]
        h_first = jnp.where(t == 0, 0.0, hsp_ref[0, TR - 1:TR, :])
        hprev = jnp.where(row == 0, h_first, pltpu.roll(hsv, 1, 0))
        first = (t == 0) & (row == 0)
        xc = xc_ref[0]
        r, i, ls, a2, mult0, mult = _lru_gates(xc, ra_ref[0], ix_ref[0], p_ref, first)
        dxc_ref[0] = gv * mult * i
        dlog_a = gv * hprev * av + jnp.where(first, 0.0, gv * i * xc * (-(av * av) / mult0))
        dra = dlog_a * LRU_C * ls * r * (1.0 - r)
        dix = gv * mult * xc * i * (1.0 - i)
        dra_ref[0] = dra.astype(BF16)
        dix_ref[0] = dix.astype(BF16)
        dp_ref[0:1, :] += _colsum(dra)
        dp_ref[1:2, :] += _colsum(dix)
        dp_ref[2:3, :] += _colsum(dlog_a * LRU_C * r) * _sigmoid(-p_ref[2:3, :])

    def win(off, shift=0):
        assert off % CB == 0
        o = off // CB
        return pl.BlockSpec((1, TR, CB), lambda j, b, rt: (b, jnp.clip(nrt - 1 - rt + shift, 0, nrt - 1), j + o))

    return pl.pallas_call(
        kern, name=name, grid=(D // CB, B, nrt),
        in_specs=[win(0), win(gate_off), win(0), win(0, -1), win(0), win(0, 1), win(xc_off), win(0), win(0),
                  pl.BlockSpec((SUBLANES, CB), lambda j, b, rt: (0, j))],
        out_specs=[win(0)] * 4 + [pl.BlockSpec((SUBLANES, CB), lambda j, b, rt: (0, j))],
        out_shape=[jax.ShapeDtypeStruct((B, L, D), BF16)] * 3 + [jax.ShapeDtypeStruct((B, L, D), F32),
                                                                 jax.ShapeDtypeStruct((SUBLANES, D), F32)],
        scratch_shapes=[pltpu.VMEM((TR, CB), F32)] * 3 + [pltpu.VMEM((1, CB), F32)],
        compiler_params=_cparams(("parallel", "arbitrary", "arbitrary")),
    )(dy, proj3, hs, hs, a, a, u, ra, ix, pvec)


def _sum8(parts, name):
    _, R, C = parts.shape
    tr = _tile(R, 1024, ROW_ALIGN if parts.dtype.itemsize == 2 else SUBLANES)

    def kern(p_ref, o_ref):
        acc = p_ref[0].astype(F32)
        for d in range(1, N_DEV):
            acc = acc + p_ref[d].astype(F32)
        o_ref[...] = acc

    return pl.pallas_call(
        kern, name=name, grid=(R // tr,), in_specs=[pl.BlockSpec((N_DEV, tr, C), lambda i: (0, i, 0))],
        out_specs=pl.BlockSpec((tr, C), lambda i: (i, 0)), out_shape=jax.ShapeDtypeStruct((R, C), F32),
        compiler_params=_cparams(("parallel",)),
    )(parts)


def _adamw(w, g, m, v, name):
    shape = w.shape
    C = shape[-1] if w.ndim > 1 else shape[0]
    R = w.size // C
    w2, g2, m2, v2 = (t.reshape(R, C) for t in (w, g, m, v))
    tr = R
    for cand in range(8, min(R, 512) + 1, 8):
        if R % cand == 0:
            tr = cand

    def kern(w_ref, g_ref, m_ref, v_ref, d_ref, nm_ref, nv_ref):
        gv = g_ref[...]
        nm = ADAM_B1 * m_ref[...] + (1.0 - ADAM_B1) * gv
        nv = ADAM_B2 * v_ref[...] + (1.0 - ADAM_B2) * (gv * gv)
        m_hat = nm / (1.0 - ADAM_B1 ** ADAM_STEP)
        v_hat = nv / (1.0 - ADAM_B2 ** ADAM_STEP)
        d_ref[...] = -ADAM_LR * (m_hat / (jnp.sqrt(v_hat) + ADAM_EPS) + ADAM_WD * w_ref[...])
        nm_ref[...] = nm
        nv_ref[...] = nv

    spec = pl.BlockSpec((tr, C), lambda i: (i, 0))
    outs = pl.pallas_call(
        kern, name=name, grid=(R // tr,), in_specs=[spec] * 4, out_specs=[spec] * 3,
        out_shape=[jax.ShapeDtypeStruct((R, C), F32)] * 3, compiler_params=_cparams(("parallel",)),
    )(w2, g2, m2, v2)
    return tuple(o.reshape(shape) for o in outs)


MESH_ID = pl.DeviceIdType.MESH
ANY = pl.BlockSpec(memory_space=pl.ANY)
N_COPIES = N_DEV - 1
COMM_SCRATCH = [pltpu.SemaphoreType.DMA((N_COPIES,)), pltpu.SemaphoreType.DMA((N_COPIES,)), pltpu.SemaphoreType.DMA]


def _my_place():
    return lax.axis_index("x"), lax.axis_index("y"), lax.axis_index("c")


def _gather_phases(x_ref, out_ref, send_sems, recv_sems, local_sem):
    x, y, c = _my_place()
    me, sibling = (x, y, c), (x, y, 1 - c)
    chips = [(1 - x, y), (x, 1 - y), (1 - x, 1 - y)]

    def slab(px, py, pc):
        return out_ref.at[4 * px + 2 * py + pc]

    def copy(k, block, to, src=None):
        return pltpu.make_async_remote_copy(
            src_ref=slab(*block) if src is None else src, dst_ref=slab(*block),
            send_sem=send_sems.at[k], recv_sem=recv_sems.at[k], device_id=to, device_id_type=MESH_ID)

    mine = pltpu.make_async_copy(x_ref, slab(*me), local_sem)
    first = [copy(0, me, sibling, src=x_ref)] + [copy(1 + j, me, (*chip, c), src=x_ref) for j, chip in enumerate(chips)]
    passed = [copy(4 + j, (*chip, c), sibling) for j, chip in enumerate(chips)]

    def start():
        mine.start()
        for cp in first:
            cp.start()

    def forward():
        for j, chip in enumerate(chips):
            copy(1 + j, (*chip, c), me).wait_recv()
            passed[j].start()

    def finish():
        copy(0, sibling, me).wait_recv()
        for j, chip in enumerate(chips):
            copy(4 + j, (*chip, 1 - c), me).wait_recv()
        for cp in first + passed:
            cp.wait_send()
        mine.wait()

    return start, forward, finish


def _exchange_phases(p_ref, out_ref, send_sems, recv_sems, local_sem):
    x, y, c = _my_place()
    my_idx = 4 * x + 2 * y + c
    mine = pltpu.make_async_copy(p_ref.at[my_idx], out_ref.at[my_idx], local_sem)
    copies = []
    for k in range(1, N_DEV):
        px, py, pc = x ^ (k >> 2), y ^ ((k >> 1) & 1), c ^ (k & 1)
        copies.append(pltpu.make_async_remote_copy(
            src_ref=p_ref.at[4 * px + 2 * py + pc], dst_ref=out_ref.at[my_idx],
            send_sem=send_sems.at[k - 1], recv_sem=recv_sems.at[k - 1], device_id=(px, py, pc),
            device_id_type=MESH_ID))

    def start():
        mine.start()
        for cp in copies:
            cp.start()

    def finish():
        for cp in copies:
            cp.wait()
        mine.wait()

    return start, finish


def _ride(lin, total, phases):
    assert total >= 3
    marks = [0, total - 1] if len(phases) == 2 else [0, total // 2, total - 1]
    for mark, phase in zip(marks, phases):
        pl.when(lin == mark)(phase)


def _all_gather(xs, name):
    R, C = xs.shape

    def body(x_ref, out_ref, send_sems, recv_sems, local_sem):
        for phase in _gather_phases(x_ref, out_ref, send_sems, recv_sems, local_sem):
            phase()

    return pl.pallas_call(
        body, name=name, out_shape=jax.ShapeDtypeStruct((N_DEV, R, C), xs.dtype), in_specs=[ANY], out_specs=ANY,
        scratch_shapes=COMM_SCRATCH,
    )(xs)


def _exchange(parts, name):
    def body(p_ref, out_ref, send_sems, recv_sems, local_sem):
        for phase in _exchange_phases(p_ref, out_ref, send_sems, recv_sems, local_sem):
            phase()

    return pl.pallas_call(
        body, name=name, out_shape=jax.ShapeDtypeStruct(parts.shape, parts.dtype), in_specs=[ANY], out_specs=ANY,
        scratch_shapes=COMM_SCRATCH,
    )(parts)


D_XBC_EXTRA = 2 * SSD_GROUPS * SSD_STATE
SMALL_W = LANES
ROW_ALIGN = 16


def _layout(D):
    d_xbc = D + D_XBC_EXTRA
    off = dict(qkv=0, z=3 * D, merge=4 * D, gate=7 * D, conv=8 * D, xr=8 * D + d_xbc, small=9 * D + d_xbc)
    off["n_all"] = off["small"] + SMALL_W
    off["d_xbc"] = d_xbc
    off["conv_c"] = d_xbc + D
    return off


def _w_in_map(D):
    lo = _layout(D)
    widths = [("q", D, 0), ("k", D, D), ("v", D, 2 * D), ("f", HEADS, lo["small"]), ("z", D, lo["z"]),
              ("xbc", lo["d_xbc"], lo["conv"]), ("dt", HEADS, lo["small"] + HEADS), ("xr", D, lo["xr"]),
              ("gate", D, lo["gate"]), ("merge", 3 * D, lo["merge"])]
    out, o = [], 0
    for _, w, mine in widths:
        out.append((o, w, mine))
        o += w
    return out


def _padded(c):
    return -(-c // ROW_ALIGN) * ROW_ALIGN


def _permute_rows(src, pieces, name):
    R, C = src.shape
    n_out = sum(n for _, n in pieces)

    def kern(x_ref, o_ref):
        o = 0
        for start, n in pieces:
            if start is None:
                o_ref[o:o + n, :] = jnp.zeros((n, LANES), src.dtype)
            else:
                o_ref[o:o + n, :] = x_ref[start:start + n, :]
            o += n

    return pl.pallas_call(
        kern, name=name, grid=(C // LANES,), in_specs=[pl.BlockSpec((R, LANES), lambda i: (0, i))],
        out_specs=pl.BlockSpec((n_out, LANES), lambda i: (0, i)), out_shape=jax.ShapeDtypeStruct((n_out, C), src.dtype),
        compiler_params=_cparams(("parallel",)),
    )(src)


def _reorder_rows(wt, D, c, name="reorder_w_in"):
    cp = _padded(c)
    lo = _layout(D)
    pieces = []
    for a, w, mine in sorted(_w_in_map(D), key=lambda t: t[2]):
        b = a + w
        while a < b:
            j = a // c
            e = min(b, (j + 1) * c)
            pieces.append((j * cp + a - j * c, e - a))
            a = e
    pieces.append((None, lo["n_all"] - lo["small"] - 2 * HEADS))
    return _permute_rows(wt, pieces, name)


def _restore_rows(dwt, D, c, name="restore_w_in"):
    cp = _padded(c)
    segs = _w_in_map(D)
    pieces = []
    for j in range(N_DEV):
        a, b = j * c, (j + 1) * c
        for s0, w, mine in segs:
            lo_, hi_ = max(a, s0), min(b, s0 + w)
            if lo_ < hi_:
                pieces.append((mine + lo_ - s0, hi_ - lo_))
        if cp > c:
            pieces.append((None, cp - c))
    return _permute_rows(dwt, pieces, name)


def _block_diag(w):
    H, n, _ = w.shape
    tiled = jnp.tile(w.reshape(H * n, n), (1, H))
    r = lax.broadcasted_iota(jnp.int32, (H * n, H * n), 0) // n
    c = lax.broadcasted_iota(jnp.int32, (H * n, H * n), 1) // n
    return jnp.where(r == c, tiled, jnp.zeros_like(tiled))


def _diag_blocks(m, H):
    n = m.shape[0] // H
    keep = jnp.eye(H, dtype=m.dtype)[:, None, :, None]
    return jnp.sum(m.reshape(H, n, H, n) * keep, axis=2)


def _to_heads(t, B, L):
    return t.reshape(B, L, HEADS, HEAD_DIM).transpose(0, 2, 1, 3)


def _from_heads(t4):
    B, H, L, P = t4.shape
    return t4.transpose(0, 2, 1, 3).reshape(B * L, H * P)


def _rows_to_tm(rows):
    B, H, nc, Q = rows.shape
    return rows.reshape(B, H, nc * Q).transpose(0, 2, 1)


def _ffn_fwd(h, g, wgu_t, wd, tag):
    n = _norm_fwd(h, g, tag + "_norm")
    gate, up, act = _ffn_up_act(n, wgu_t, tag + "_up")
    out = _mm(act, wd, res=h, scale=0.5, name=tag + "_down")
    return out, (h, n, gate, up, act)


def _ffn_bwd(dh, dhb, saved, g, wgu_t, wd, tag):
    h, n, gate, up, act = saved
    T, D = h.shape
    F = wd.shape[0]
    dgate, dup = _ffn_down_dx_act(dhb, wd, gate, up, tag + "_down_dx")
    dwd = _mm(act, dhb, ta=True, scale=0.5, name=tag + "_down_dw")
    dwgu_t = jnp.concatenate([_mm(dgate, n, ta=True, tn=1024, name=tag + "_gate_dw"),
                              _mm(dup, n, ta=True, tn=1024, name=tag + "_up_dw")], axis=0)
    dn = _mm(dgate, wgu_t, dims=(T, D, F), name=tag + "_gate_dx")
    dn = _mm(dup, wgu_t, b_off=(F, 0), dims=(T, D, F), res=dn, name=tag + "_up_dx")
    dh_in, dhb_in, dg = _norm_bwd(h, dn, dh, g, tag + "_norm_bwd")
    return dh_in, dhb_in, dict(norm=dg, gu=dwgu_t, down=dwd)


def _mixer_fwd(h, p, B, L, gather=None):
    T, D = h.shape
    lo = _layout(D)
    n = _norm_fwd(h, p["gm"], "mix_norm")
    proj = _mm(n, p["w_all_t"], tb=True, name="mix_in")
    proj3 = proj.reshape(B, L, lo["n_all"])
    vals, cums = _gate_prep(proj3, lo["small"] // LANES, p["small_bias"], p["avec"], "gate_prep")
    cums_t = cums[..., :2 * HEADS].transpose(0, 2, 1).reshape(B, 2 * HEADS, L // Q_BLOCK, Q_BLOCK)
    qa, ka, va = _attn_pack(proj3, cums, "attn_pack")
    y_a3, y_ab3, lse, *gathered = _attn_fwd(qa, ka, va, "attn_fwd", gather)
    y_a, y_ab = y_a3.reshape(T, D), y_ab3.reshape(T, D)
    u = _conv_fwd(proj3, lo["conv"] // LANES, p["conv_w"], p["conv_b"], lo["d_xbc"] // LANES, "conv_fwd")
    b_blk = D // LANES
    c_blk = b_blk + SSD_GROUPS * SSD_STATE // LANES
    y_s = _ssd_fwd(u, b_blk, c_blk, vals, cums, cums_t, p["dvec"], "ssd_fwd").reshape(T, D)
    yb = _gnorm_fwd(y_s, proj, lo["z"], p["ssd_norm"], "gnorm_fwd")
    u2 = u.reshape(T, lo["conv_c"])
    ra = _mm(u2, p["wa"], a_off=(0, lo["d_xbc"]), dims=(T, D, D), tk=512, name="lru_ra")
    ix = _mm(u2, p["wx"], a_off=(0, lo["d_xbc"]), dims=(T, D, D), tk=512, name="lru_ix")
    yc, hs, a = _lru_fwd(u, lo["d_xbc"], ra.reshape(B, L, D), ix.reshape(B, L, D), proj3, lo["gate"], p["pvec"],
                         "lru_fwd")
    yc = yc.reshape(T, D)
    pa = _mm(y_ab, p["wba"], name="branch_attn")
    pb = _mm(yb, p["wbs"], name="branch_ssd")
    pc = _mm(yc, p["wbl"], name="branch_lru")
    mixed = _merge_fwd(proj, lo["merge"], pa, pb, pc, "merge_fwd")
    out = _mm(mixed, p["wout"], res=h, name="mix_out")
    saved = dict(h=h, n=n, proj=proj, qa=qa, ka=ka, va=va, vals=vals, cums=cums, cums_t=cums_t, lse=lse, y_a=y_a, y_ab=y_ab,
                 u=u, y_s=y_s, yb=yb, ra=ra, ix=ix, yc=yc, hs=hs, a=a, pa=pa, pb=pb, pc=pc, mixed=mixed)
    return out, saved, (gathered[0] if gathered else None)


def _mixer_bwd(dh, dhb, s, p, B, L, parts=None):
    T, D = dh.shape
    lo = _layout(D)
    proj, u = s["proj"], s["u"]
    proj3 = proj.reshape(B, L, lo["n_all"])
    g = {}
    dmixed = _mm(dhb, p["wout"], tb=True, name="mix_out_dx")
    g["wout"] = _mm(s["mixed"], dhb, ta=True, name="mix_out_dw")
    dpa, dpb, dpc, dmerge = _merge_bwd(dmixed, proj, lo["merge"], s["pa"], s["pb"], s["pc"], "merge_bwd")
    dy_a = _mm(dpa, p["wba"], tb=True, name="branch_attn_dx")
    g["wba"] = _mm(s["y_ab"], dpa, ta=True, name="branch_attn_dw")
    dyb = _mm(dpb, p["wbs"], tb=True, name="branch_ssd_dx")
    g["wbs"] = _mm(s["yb"], dpb, ta=True, name="branch_ssd_dw")
    dyc = _mm(dpc, p["wbl"], tb=True, name="branch_lru_dx")
    g["wbl"] = _mm(s["yc"], dpc, ta=True, name="branch_lru_dw")
    dgate, dra, dix, dxc, g["pvec"] = _lru_bwd(dyc.reshape(B, L, D), proj3, lo["gate"], s["hs"], s["a"], u, lo["d_xbc"],
                                               s["ra"].reshape(B, L, D), s["ix"].reshape(B, L, D), p["pvec"], "lru_bwd")
    dra, dix = dra.reshape(T, D), dix.reshape(T, D)
    u2 = u.reshape(T, lo["conv_c"])
    g["wa"] = _mm(u2, dra, ta=True, a_off=(0, lo["d_xbc"]), dims=(D, D, T), tm=512, name="lru_ra_dw")
    g["wx"] = _mm(u2, dix, ta=True, a_off=(0, lo["d_xbc"]), dims=(D, D, T), tm=512, name="lru_ix_dw")
    dxc = _mm(dra, p["wa"], tb=True, res=dxc.reshape(T, D), name="lru_ra_dx")
    dxc = _mm(dix, p["wx"], tb=True, res=dxc, name="lru_ix_dx")
    dy_s, dz, g["ssd_norm"] = _gnorm_bwd(dyb, s["y_s"], proj, lo["z"], p["ssd_norm"], "gnorm_bwd")
    b_blk = D // LANES
    c_blk = b_blk + SSD_GROUPS * SSD_STATE // LANES
    dxs, dBg, dCg, ddt_tm, dAc_tm, dAr, g["dvec"] = _ssd_bwd(u, b_blk, c_blk, s["vals"], s["cums"], s["cums_t"],
                                                             p["dvec"], dy_s.reshape(B, L, D), "ssd_bwd")
    grp = lambda t: t.transpose(0, 2, 1, 3).reshape(B, L, SSD_GROUPS * SSD_STATE)
    du = jnp.concatenate([dxs, grp(dBg), grp(dCg), dxc.reshape(B, L, D)], axis=-1)
    dpp, g["conv_wb"] = _conv_bwd_pre(proj3, lo["conv"] // LANES, du, p["conv_w"], p["conv_b"], lo["d_xbc"] // LANES,
                                      "conv_bwd_pre")
    dconv = _conv_bwd_in(dpp, p["conv_w"], "conv_bwd_in")
    dq3, dk3, dv3, dc_tm, *recv = _attn_bwd(s["qa"], s["ka"], s["va"], s["y_a"].reshape(B, L, D),
                                            dy_a.reshape(B, L, D), s["lse"], "attn_bwd", parts)
    drow_tm = dc_tm + jnp.pad(_rows_to_tm(dAr), ((0, 0), (0, 0), (HEADS, LANES - 2 * HEADS)))
    dsmall, g["small_bias"], g["avec"] = _gate_post(drow_tm, dAc_tm, ddt_tm, proj3, lo["small"] // LANES, s["vals"],
                                                    p["small_bias"], p["avec"], "gate_post")
    dproj = jnp.concatenate([dq3.reshape(T, D), dk3.reshape(T, D), dv3.reshape(T, D), dz, dmerge, dgate.reshape(T, D), dconv.reshape(T, lo["conv_c"]),
                             dsmall.reshape(T, SMALL_W)], axis=1)
    g["w_all_t"] = _mm(dproj, s["n"], ta=True, tn=1024, name="mix_in_dw")
    dn = _mm(dproj, p["w_all_t"], name="mix_in_dx")
    dh_in, dhb_in, g["gm"] = _norm_bwd(s["h"], dn, dh, p["gm"], "mix_norm_bwd")
    return dh_in, dhb_in, g, (recv[0] if recv else None)


def _small_vec(a, b):
    return jnp.concatenate([a, b, jnp.zeros((LANES - 2 * HEADS,), F32)])[None, :]


def _layer_params(w):
    zeros16 = jnp.zeros((HEADS,), F32)
    pvec = jnp.concatenate([w["lru_b_a"][None], w["lru_b_x"][None], w["lru_lambda"][None],
                            jnp.zeros((SUBLANES - 3, w["lru_b_a"].shape[0]), F32)], axis=0)
    return dict(
        g1=w["ffn1_norm"][None], gu1=w["ffn1_w_gate_up"], d1=w["ffn1_w_down"],
        gm=w["mix_norm"][None], w_all_t=w["w_in"],
        small_bias=_small_vec(w["fox_forget_bias"], w["ssd_dt_bias"]),
        avec=_small_vec(zeros16, -jnp.exp(w["ssd_a_log"])), dvec=_small_vec(zeros16, w["ssd_d"]),
        conv_w=jnp.concatenate([w["ssd_conv_w"], w["lru_conv_w"]], axis=1),
        conv_b=jnp.concatenate([w["ssd_conv_b"], w["lru_conv_b"]])[None],
        ssd_norm=w["ssd_norm"][None],
        wa=_block_diag(w["lru_w_a"]).astype(BF16), wx=_block_diag(w["lru_w_x"]).astype(BF16), pvec=pvec,
        wba=w["w_branch_attn"], wbs=w["w_branch_ssd"], wbl=w["w_branch_lru"], wout=w["w_out"],
        g2=w["ffn2_norm"][None], gu2=w["ffn2_w_gate_up"], d2=w["ffn2_w_down"],
    )


def _layer_fwd(h, p, B, L, gather=None):
    h, s1 = _ffn_fwd(h, p["g1"], p["gu1"], p["d1"], "ffn1")
    h, sm, gathered = _mixer_fwd(h, p, B, L, gather)
    h, s2 = _ffn_fwd(h, p["g2"], p["gu2"], p["d2"], "ffn2")
    return h, (s1, sm, s2), gathered


def _layer_bwd(dh, dhb, saved, p, w, B, L, parts=None):
    s1, sm, s2 = saved
    D = dh.shape[1]
    d_xbc = D + D_XBC_EXTRA
    dh, dhb, f2 = _ffn_bwd(dh, dhb, s2, p["g2"], p["gu2"], p["d2"], "ffn2")
    dh, dhb, gm, recv = _mixer_bwd(dh, dhb, sm, p, B, L, parts)
    dh, dhb, f1 = _ffn_bwd(dh, dhb, s1, p["g1"], p["gu1"], p["d1"], "ffn1")
    sb, av = gm["small_bias"][0], gm["avec"][0]
    cw = gm["conv_wb"]
    grads = dict(
        ffn1_norm=f1["norm"][0], ffn1_w_gate_up=f1["gu"], ffn1_w_down=f1["down"],
        mix_norm=gm["gm"][0], w_in=gm["w_all_t"],
        fox_forget_bias=sb[:HEADS], ssd_conv_w=cw[:CONV_K, :d_xbc], ssd_conv_b=cw[CONV_K, :d_xbc],
        ssd_dt_bias=sb[HEADS:2 * HEADS], ssd_a_log=av[HEADS:2 * HEADS] * (-jnp.exp(w["ssd_a_log"])),
        ssd_d=gm["dvec"][0, HEADS:2 * HEADS], ssd_norm=gm["ssd_norm"][0],
        lru_conv_w=cw[:CONV_K, d_xbc:], lru_conv_b=cw[CONV_K, d_xbc:],
        lru_w_a=_diag_blocks(gm["wa"], HEADS), lru_b_a=gm["pvec"][0], lru_w_x=_diag_blocks(gm["wx"], HEADS),
        lru_b_x=gm["pvec"][1], lru_lambda=gm["pvec"][2],
        w_branch_attn=gm["wba"], w_branch_ssd=gm["wbs"], w_branch_lru=gm["wbl"], w_out=gm["wout"],
        ffn2_norm=f2["norm"][0], ffn2_w_gate_up=f2["gu"], ffn2_w_down=f2["down"],
    )
    return dh, dhb, grads, recv


LAYER_NAMES = ["ffn1_norm", "ffn1_w_gate_up", "ffn1_w_down", "mix_norm", "w_in", "fox_forget_bias", "ssd_conv_w",
               "ssd_conv_b", "ssd_dt_bias", "ssd_a_log", "ssd_d", "ssd_norm", "lru_conv_w", "lru_conv_b", "lru_w_a",
               "lru_b_a", "lru_w_x", "lru_b_x", "lru_lambda", "w_branch_attn", "w_branch_ssd", "w_branch_lru", "w_out",
               "ffn2_norm", "ffn2_w_gate_up", "ffn2_w_down"]
WEIGHT_NAMES = ["meta_tokens"] + LAYER_NAMES + ["final_norm"]


def _local_step(x, target, meta, final_norm, depth, layer_weights, pack_next=None, pack_grads=None):
    B, S, D = x.shape
    L = -(-(N_META + S) // Q_BLOCK) * Q_BLOCK
    h = jnp.concatenate([jnp.broadcast_to(meta[None], (B, N_META, D)), x,
                         jnp.zeros((B, L - N_META - S, D), F32)], axis=1).reshape(B * L, D)
    weights, params, saved = [], [], []
    gathered = None
    for l in range(depth):
        w = layer_weights(l, gathered)
        p = _layer_params(w)
        nxt = pack_next(l + 1) if (pack_next is not None and l + 1 < depth) else None
        h, s, gathered = _layer_fwd(h, p, B, L, nxt)
        weights.append(w)
        params.append(p)
        saved.append(s)
    tgt = jnp.pad(target, ((0, 0), (N_META, L - N_META - S), (0, 0))).reshape(B * L, D)
    dh, dhb, loss, dfinal = _loss_bwd(h, tgt, final_norm[None], L, S, "loss")
    grads = [None] * depth
    received, parts = {}, None
    for l in reversed(range(depth)):
        dh, dhb, grads[l], recv = _layer_bwd(dh, dhb, saved[l], params[l], weights[l], B, L, parts)
        if recv is not None:
            received[l + 1] = recv
        parts = pack_grads(grads[l]) if pack_grads is not None else None
    dh3 = dh.reshape(B, L, D)
    return (loss, dh3[:, N_META:N_META + S], jnp.sum(dh3[:, :N_META], axis=0), grads, dfinal[0], received, parts)


BIG_NAMES = ["ffn1_w_gate_up", "ffn1_w_down", "w_in", "w_branch_attn", "w_branch_ssd", "w_branch_lru", "w_out",
             "ffn2_w_gate_up", "ffn2_w_down"]
COL_SHARDED = {"ffn1_w_gate_up", "w_in", "ffn2_w_gate_up"}
SMALL_SHARDED = ["meta_tokens", "ssd_conv_w", "lru_conv_w"]
SMALL_NAMES = [n for n in LAYER_NAMES if n not in BIG_NAMES]


def _shard_rows(name, shape):
    return _padded(shape[1]) if name in COL_SHARDED else shape[0]


def _pack_shards(shards):
    rows = []
    for n in BIG_NAMES:
        s = shards[n]
        if n in COL_SHARDED:
            s = jnp.pad(s.T, ((0, _padded(s.shape[1]) - s.shape[1]), (0, 0)))
        rows.append(s)
    return jnp.concatenate(rows, axis=0)


def _unpack_gathered(gathered, shapes, D):
    out, o = {}, 0
    for n in BIG_NAMES:
        r = _shard_rows(n, shapes[n])
        out[n] = gathered[:, o:o + r].reshape(N_DEV * r, D)
        o += r
    out["w_in"] = _reorder_rows(out["w_in"], D, shapes["w_in"][1])
    return out


def _pack_full_grads(grads, shapes, D):
    slabs = []
    for n in BIG_NAMES:
        g = grads[n]
        if n == "w_in":
            g = _restore_rows(g, D, shapes[n][1])
        slabs.append(g.reshape(N_DEV, _shard_rows(n, shapes[n]), D))
    return jnp.concatenate(slabs, axis=1)


def _unpack_local(rows, shapes):
    out, o = {}, 0
    for n in BIG_NAMES:
        r = _shard_rows(n, shapes[n])
        blk = rows[o:o + r]
        out[n] = blk[:shapes[n][1]].T if n in COL_SHARDED else blk
        o += r
    return out


def _as_rows(flat):
    n = flat.shape[0]
    unit = LANES * SUBLANES
    total = -(-n // unit) * unit
    return jnp.pad(flat, (0, total - n)).reshape(total // LANES, LANES)


def _flatten_list(arrs):
    return _as_rows(jnp.concatenate([a.reshape(-1) for a in arrs]))


def _split_like(rows, shapes):
    flat = rows.reshape(-1)
    out, o = [], 0
    for s in shapes:
        n = math.prod(s)
        out.append(flat[o:o + n].reshape(s))
        o += n
    return out


def _gather_last(rows8, shape):
    lead, c = shape[:-1], shape[-1]
    t = rows8.reshape((N_DEV,) + tuple(lead) + (c,))
    return jnp.moveaxis(t, 0, -2).reshape(tuple(lead) + (N_DEV * c,))


def kernel(x, meta_tokens, ffn1_norm, ffn1_w_gate_up, ffn1_w_down, mix_norm, w_in, fox_forget_bias, ssd_conv_w, ssd_conv_b, ssd_dt_bias, ssd_a_log, ssd_d, ssd_norm, lru_conv_w, lru_conv_b, lru_w_a, lru_b_a, lru_w_x, lru_b_x, lru_lambda, w_branch_attn, w_branch_ssd, w_branch_lru, w_out, ffn2_norm, ffn2_w_gate_up, ffn2_w_down, final_norm, loss_target, m_meta_tokens, m_ffn1_norm, m_ffn1_w_gate_up, m_ffn1_w_down, m_mix_norm, m_w_in, m_fox_forget_bias, m_ssd_conv_w, m_ssd_conv_b, m_ssd_dt_bias, m_ssd_a_log, m_ssd_d, m_ssd_norm, m_lru_conv_w, m_lru_conv_b, m_lru_w_a, m_lru_b_a, m_lru_w_x, m_lru_b_x, m_lru_lambda, m_w_branch_attn, m_w_branch_ssd, m_w_branch_lru, m_w_out, m_ffn2_norm, m_ffn2_w_gate_up, m_ffn2_w_down, m_final_norm, v_meta_tokens, v_ffn1_norm, v_ffn1_w_gate_up, v_ffn1_w_down, v_mix_norm, v_w_in, v_fox_forget_bias, v_ssd_conv_w, v_ssd_conv_b, v_ssd_dt_bias, v_ssd_a_log, v_ssd_d, v_ssd_norm, v_lru_conv_w, v_lru_conv_b, v_lru_w_a, v_lru_b_a, v_lru_w_x, v_lru_b_x, v_lru_lambda, v_w_branch_attn, v_w_branch_ssd, v_w_branch_lru, v_w_out, v_ffn2_norm, v_ffn2_w_gate_up, v_ffn2_w_down, v_final_norm):
    weights = dict(zip(WEIGHT_NAMES, (meta_tokens, ffn1_norm, ffn1_w_gate_up, ffn1_w_down, mix_norm, w_in, fox_forget_bias, ssd_conv_w, ssd_conv_b, ssd_dt_bias, ssd_a_log, ssd_d, ssd_norm, lru_conv_w, lru_conv_b, lru_w_a, lru_b_a, lru_w_x, lru_b_x, lru_lambda, w_branch_attn, w_branch_ssd, w_branch_lru, w_out, ffn2_norm, ffn2_w_gate_up, ffn2_w_down, final_norm,)))
    mom1 = dict(zip(WEIGHT_NAMES, (m_meta_tokens, m_ffn1_norm, m_ffn1_w_gate_up, m_ffn1_w_down, m_mix_norm, m_w_in, m_fox_forget_bias, m_ssd_conv_w, m_ssd_conv_b, m_ssd_dt_bias, m_ssd_a_log, m_ssd_d, m_ssd_norm, m_lru_conv_w, m_lru_conv_b, m_lru_w_a, m_lru_b_a, m_lru_w_x, m_lru_b_x, m_lru_lambda, m_w_branch_attn, m_w_branch_ssd, m_w_branch_lru, m_w_out, m_ffn2_norm, m_ffn2_w_gate_up, m_ffn2_w_down, m_final_norm,)))
    mom2 = dict(zip(WEIGHT_NAMES, (v_meta_tokens, v_ffn1_norm, v_ffn1_w_gate_up, v_ffn1_w_down, v_mix_norm, v_w_in, v_fox_forget_bias, v_ssd_conv_w, v_ssd_conv_b, v_ssd_dt_bias, v_ssd_a_log, v_ssd_d, v_ssd_norm, v_lru_conv_w, v_lru_conv_b, v_lru_w_a, v_lru_b_a, v_lru_w_x, v_lru_b_x, v_lru_lambda, v_w_branch_attn, v_w_branch_ssd, v_w_branch_lru, v_w_out, v_ffn2_norm, v_ffn2_w_gate_up, v_ffn2_w_down, v_final_norm,)))
    depth = ffn1_norm.shape[0]
    D = x.shape[-1]
    my_idx = 4 * lax.axis_index("x") + 2 * lax.axis_index("y") + lax.axis_index("c")

    small_shapes = [weights[n].shape for n in SMALL_SHARDED]
    gathered = _all_gather(_flatten_list([weights[n] for n in SMALL_SHARDED]), "gather_small").reshape(N_DEV, -1)
    small_full, o = {}, 0
    for n, s in zip(SMALL_SHARDED, small_shapes):
        k = math.prod(s)
        small_full[n] = _gather_last(gathered[:, o:o + k], s)
        o += k

    shard_shapes = {n: weights[n].shape[1:] for n in BIG_NAMES}
    pack_next = lambda l: _pack_shards({n: weights[n][l].astype(BF16) for n in BIG_NAMES})

    def layer_weights(l, gathered):
        if gathered is None:
            gathered = _all_gather(pack_next(l), "gather_weights")
        w = _unpack_gathered(gathered, shard_shapes, D)
        for n in SMALL_NAMES:
            w[n] = small_full[n][l] if n in SMALL_SHARDED else weights[n][l]
        return w

    pack_grads = lambda g: _pack_full_grads(g, shard_shapes, D).astype(BF16)
    loss, dx, dmeta, grads, dfinal, received, parts = _local_step(
        x, loss_target, small_full["meta_tokens"], final_norm, depth, layer_weights, pack_next, pack_grads)
    received[0] = _exchange(parts, "exchange_grads")
    loss = lax.psum(loss[0, 0], ("x", "y", "c"))
    summed = {n: [] for n in WEIGHT_NAMES}
    for l in range(depth):
        local = _unpack_local(_sum8(received[l], "sum_grads"), shard_shapes)
        for n in BIG_NAMES:
            summed[n].append(local[n])

    small_list = [dmeta, dfinal] + [grads[l][n] for l in range(depth) for n in SMALL_NAMES]
    total = _sum8(_all_gather(_flatten_list(small_list), "gather_small_grads"), "sum_small_grads")
    parts = _split_like(total, [a.shape for a in small_list])
    full_small = {"meta_tokens": parts[0], "final_norm": parts[1]}
    for i, n in enumerate(SMALL_NAMES):
        full_small[n] = jnp.stack([parts[2 + l * len(SMALL_NAMES) + i] for l in range(depth)])
    grad = {}
    for n in WEIGHT_NAMES:
        if n in BIG_NAMES:
            grad[n] = jnp.stack(summed[n])
        elif n in SMALL_SHARDED:
            c = weights[n].shape[-1]
            grad[n] = lax.dynamic_slice_in_dim(full_small[n], my_idx * c, c, axis=full_small[n].ndim - 1)
        else:
            grad[n] = full_small[n]

    delta, new_m, new_v = {}, {}, {}
    for n in WEIGHT_NAMES:
        delta[n], new_m[n], new_v[n] = _adamw(weights[n], grad[n], mom1[n], mom2[n], "adamw_" + n)
    return (loss, dx, *[grad[n] for n in WEIGHT_NAMES], *[delta[n] for n in WEIGHT_NAMES],
            *[new_m[n] for n in WEIGHT_NAMES], *[new_v[n] for n in WEIGHT_NAMES])
```

```python
import functools
import math

import jax
import jax.numpy as jnp
from jax import lax
from jax.experimental import pallas as pl
from jax.experimental.pallas import tpu as pltpu

F32 = jnp.float32
BF16 = jnp.bfloat16

N_DEV = 8
N_META = 16
Q_BLOCK = 128
NORM_EPS = 1e-6
HEADS = 16
HEAD_DIM = 64
SSD_GROUPS = 2
SSD_STATE = 128
CONV_K = 4
LRU_C = 8.0
ADAM_LR, ADAM_B1, ADAM_B2, ADAM_EPS, ADAM_WD, ADAM_STEP = 0.001, 0.9, 0.999, 1e-08, 0.01, 10

LANES = 128
SUBLANES = 8
VMEM_LIMIT = 56 * 1024 * 1024
NEG = -1e30
MM_TILE = 1408
MM_VMEM = 40 * 1024 * 1024


def _cparams(sem=None):
    return pltpu.CompilerParams(dimension_semantics=sem, vmem_limit_bytes=VMEM_LIMIT)


def _tile(dim, target, mult=LANES):
    if dim <= target:
        return dim
    best = None
    for t in range(mult, target + 1, mult):
        if dim % t == 0:
            best = t
    assert best is not None, (dim, target)
    return best


def _sigmoid(x):
    return 1.0 / (1.0 + jnp.exp(-x))


def _log1p_exp_neg_abs(x):
    e = jnp.exp(-jnp.abs(x))
    u = 1.0 + e
    return jnp.where(u == 1.0, e, jnp.log(u) * (e / jnp.where(u == 1.0, 1.0, u - 1.0)))


def _log_sigmoid(x):
    return jnp.minimum(x, 0.0) - _log1p_exp_neg_abs(x)


def _softplus(x):
    return jnp.maximum(x, 0.0) + _log1p_exp_neg_abs(x)


def _one_minus_exp(y):
    u = jnp.exp(y)
    safe = jnp.where(u == 1.0, 0.5, u)
    return jnp.where(u == 1.0, -y, (1.0 - u) * y / jnp.log(safe))


def _silu(x):
    return x * _sigmoid(x)


def _dsilu(x):
    s = _sigmoid(x)
    return s * (1.0 + x * (1.0 - s))


_GELU_C = math.sqrt(2.0 / math.pi)


def _gelu(x):
    return 0.5 * x * (1.0 + jnp.tanh(_GELU_C * (x + 0.044715 * x * x * x)))


def _dgelu(x):
    t = jnp.tanh(_GELU_C * (x + 0.044715 * x * x * x))
    return 0.5 * (1.0 + t) + 0.5 * x * (1.0 - t * t) * _GELU_C * (1.0 + 3.0 * 0.044715 * x * x)


def _split3_dot(tri, x):
    hi = x.astype(BF16)
    r1 = x - hi.astype(F32)
    mid = r1.astype(BF16)
    lo = (r1 - mid.astype(F32)).astype(BF16)
    t = tri.astype(BF16)
    d = lambda p: jnp.dot(t, p, preferred_element_type=F32)
    return d(hi) + d(mid) + d(lo)


def _lower_tri(n, strict=False):
    r = lax.broadcasted_iota(jnp.int32, (n, n), 0)
    c = lax.broadcasted_iota(jnp.int32, (n, n), 1)
    return (c < r) if strict else (c <= r)


def _mm(a, b, *, ta=False, tb=False, out_dtype=F32, res=None, scale=None, tm=None, tn=None, tk=None,
        a_off=(0, 0), b_off=(0, 0), dims=None, name):
    if dims is None:
        M, K = (a.shape[1], a.shape[0]) if ta else a.shape
        N = b.shape[0] if tb else b.shape[1]
    else:
        M, N, K = dims
    tk = tk or _tile(K, 2816)
    nk_ = K // tk
    pick_m, pick_n = tm is None, tn is None
    tm = tm or _tile(M, MM_TILE)
    tn = tn or _tile(N, MM_TILE)

    def vmem(tm_, tn_):
        a_b = tm_ * tk * a.dtype.itemsize + (tm_ * tk * 2 if a.dtype != BF16 else 0)
        b_b = tn_ * tk * b.dtype.itemsize + (tn_ * tk * 2 if b.dtype != BF16 else 0)
        o_b = tm_ * tn_ * (jnp.dtype(out_dtype).itemsize + (4 if res is not None else 0))
        return 2 * (a_b + b_b + o_b) + (tm_ * tn_ * 4 if nk_ > 1 else 0) + tm_ * tn_ * 4

    while vmem(tm, tn) > MM_VMEM and (pick_m or pick_n):
        if pick_m and (tm >= tn or not pick_n) and tm > LANES:
            tm = _tile(M, tm - LANES)
        elif pick_n and tn > LANES:
            tn = _tile(N, tn - LANES)
        else:
            break
    assert M % tm == 0 and N % tn == 0 and K % tk == 0, (name, M, N, K, tm, tn, tk)
    nk = K // tk
    ca = 0 if ta else 1
    cb = 1 if tb else 0

    def blk(rows, cols, off):
        assert off[0] % rows == 0 and off[1] % cols == 0, (name, off, rows, cols)
        return off[0] // rows, off[1] // cols

    if ta:
        ao = blk(tk, tm, a_off)
        a_spec = pl.BlockSpec((tk, tm), lambda i, j, k: (k + ao[0], i + ao[1]))
    else:
        ao = blk(tm, tk, a_off)
        a_spec = pl.BlockSpec((tm, tk), lambda i, j, k: (i + ao[0], k + ao[1]))
    if tb:
        bo = blk(tn, tk, b_off)
        b_spec = pl.BlockSpec((tn, tk), lambda i, j, k: (j + bo[0], k + bo[1]))
    else:
        bo = blk(tk, tn, b_off)
        b_spec = pl.BlockSpec((tk, tn), lambda i, j, k: (k + bo[0], j + bo[1]))
    o_spec = pl.BlockSpec((tm, tn), lambda i, j, k: (i, j))
    in_specs = [a_spec, b_spec] + ([o_spec] if res is not None else [])
    has_res = res is not None

    def kern(*refs):
        if has_res:
            a_ref, b_ref, r_ref, o_ref = refs[:4]
            scr = refs[4:]
        else:
            a_ref, b_ref, o_ref = refs[:3]
            r_ref = None
            scr = refs[3:]
        p = lax.dot_general(a_ref[...].astype(BF16), b_ref[...].astype(BF16), (((ca,), (cb,)), ((), ())),
                            preferred_element_type=F32)

        def fin(val):
            if scale is not None:
                val = val * scale
            if has_res:
                val = r_ref[...] + val
            o_ref[...] = val.astype(out_dtype)

        if nk == 1:
            fin(p)
        else:
            acc = scr[0]
            k = pl.program_id(2)

            @pl.when(k == 0)
            def _():
                acc[...] = p

            @pl.when(k > 0)
            def _():
                acc[...] += p

            @pl.when(k == nk - 1)
            def _():
                fin(acc[...])

    args = (a, b) + ((res,) if has_res else ())
    return pl.pallas_call(
        kern, name=name, grid=(M // tm, N // tn, nk), in_specs=in_specs, out_specs=o_spec,
        out_shape=jax.ShapeDtypeStruct((M, N), out_dtype),
        scratch_shapes=[pltpu.VMEM((tm, tn), F32)] if nk > 1 else [],
        compiler_params=_cparams(("parallel", "parallel", "arbitrary")),
    )(*args)


def _rows(body, tiled, full, outs, accs, *, tr, name, T):
    assert T % tr == 0
    in_specs = []
    for arr, width, off in tiled:
        assert off % width == 0, (name, off, width)
        in_specs.append(pl.BlockSpec((tr, width), functools.partial(lambda i, o: (i, o), o=off // width)))
    for arr in full:
        in_specs.append(pl.BlockSpec(arr.shape, lambda i: (0, 0)))
    out_specs = [pl.BlockSpec((tr, w), lambda i: (i, 0)) for w, _ in outs]
    out_specs += [pl.BlockSpec(s, lambda i: (0, 0)) for s, _ in accs]
    out_shape = [jax.ShapeDtypeStruct((T, w), d) for w, d in outs] + [jax.ShapeDtypeStruct(s, d) for s, d in accs]
    nt, nf, no = len(tiled), len(full), len(outs)

    def kern(*refs):
        i = pl.program_id(0)
        acc_refs = refs[nt + nf + no:]

        @pl.when(i == 0)
        def _():
            for r in acc_refs:
                r[...] = jnp.zeros(r.shape, r.dtype)

        body(i, refs[:nt], refs[nt:nt + nf], refs[nt + nf:nt + nf + no], acc_refs)

    res = pl.pallas_call(
        kern, name=name, grid=(T // tr,), in_specs=in_specs, out_specs=out_specs, out_shape=out_shape,
        compiler_params=_cparams(("arbitrary",)),
    )(*[t[0] for t in tiled], *full)
    return res


def _colsum(x):
    return jnp.sum(x, axis=0, keepdims=True)


def _norm_fwd(h, g, name):
    T, D = h.shape

    def body(i, t, f, o, a):
        x = t[0][...]
        r = lax.rsqrt(jnp.mean(x * x, axis=-1, keepdims=True) + NORM_EPS)
        o[0][...] = (x * r * f[0][...]).astype(BF16)

    return _rows(body, [(h, D, 0)], [g], [(D, BF16)], [], tr=_tile(T, 768, 8), name=name, T=T)[0]


def _norm_bwd(h, dn, dh, g, name):
    T, D = h.shape

    def body(i, t, f, o, a):
        x, dnv, dhv = t[0][...], t[1][...], t[2][...]
        r = lax.rsqrt(jnp.mean(x * x, axis=-1, keepdims=True) + NORM_EPS)
        xh = x * r
        dng = dnv * f[0][...]
        out = dhv + r * (dng - xh * jnp.mean(dng * xh, axis=-1, keepdims=True))
        o[0][...] = out
        o[1][...] = out.astype(BF16)
        a[0][...] += _colsum(dnv * xh)

    return _rows(body, [(h, D, 0), (dn, D, 0), (dh, D, 0)], [g], [(D, F32), (D, BF16)], [((1, D), F32)],
                 tr=_tile(T, 384, 16), name=name, T=T)


FFN_TM = 768


def _ffn_up_act(n, wgu_t, name):
    T, D = n.shape
    F = wgu_t.shape[0] // 2
    tm, tn = _tile(T, FFN_TM), _tile(F, MM_TILE)
    nj = F // tn

    def kern(n_ref, wg_ref, wu_ref, g_ref, u_ref, a_ref):
        nv = n_ref[...]
        g = _dot_nt(nv, wg_ref[...])
        u = _dot_nt(nv, wu_ref[...])
        g_ref[...] = g
        u_ref[...] = u
        a_ref[...] = (_silu(g) * u).astype(BF16)

    out = pl.BlockSpec((tm, tn), lambda i, j: (i, j))
    return pl.pallas_call(
        kern, name=name, grid=(T // tm, nj),
        in_specs=[pl.BlockSpec((tm, D), lambda i, j: (i, 0)), pl.BlockSpec((tn, D), lambda i, j: (j, 0)),
                  pl.BlockSpec((tn, D), lambda i, j: (nj + j, 0))],
        out_specs=[out] * 3,
        out_shape=[jax.ShapeDtypeStruct((T, F), F32), jax.ShapeDtypeStruct((T, F), F32), jax.ShapeDtypeStruct((T, F), BF16)],
        compiler_params=_cparams(("parallel", "parallel")),
    )(n, wgu_t, wgu_t)


def _ffn_down_dx_act(dhb, wd, g, u, name):
    T, D = dhb.shape
    F = wd.shape[0]
    tm, tn = _tile(T, FFN_TM), _tile(F, MM_TILE)
    nj = F // tn

    def kern(d_ref, w_ref, g_ref, u_ref, o_ref):
        da = _dot_nt(d_ref[...], w_ref[...]) * 0.5
        gv, uv = g_ref[...], u_ref[...]
        dg = (da * uv * _dsilu(gv)).astype(BF16)
        du = (da * _silu(gv)).astype(BF16)
        for jj in range(nj):

            @pl.when(pl.program_id(1) == jj)
            def _():
                o_ref[:, jj * tn:(jj + 1) * tn] = dg
                o_ref[:, F + jj * tn:F + (jj + 1) * tn] = du

    tile = pl.BlockSpec((tm, tn), lambda i, j: (i, j))
    return pl.pallas_call(
        kern, name=name, grid=(T // tm, nj),
        in_specs=[pl.BlockSpec((tm, D), lambda i, j: (i, 0)), pl.BlockSpec((tn, D), lambda i, j: (j, 0)), tile, tile],
        out_specs=pl.BlockSpec((tm, 2 * F), lambda i, j: (i, 0)), out_shape=jax.ShapeDtypeStruct((T, 2 * F), BF16),
        compiler_params=_cparams(("parallel", "arbitrary")),
    )(dhb, wd, g, u)


def _merge_fwd(proj, off, pa, pb, pc, name):
    T, D = pa.shape

    def body(i, t, f, o, a):
        o[0][...] = (_sigmoid(t[0][...]) * t[3][...] + _sigmoid(t[1][...]) * t[4][...]
                     + _sigmoid(t[2][...]) * t[5][...]).astype(BF16)

    tiled = [(proj, D, off), (proj, D, off + D), (proj, D, off + 2 * D), (pa, D, 0), (pb, D, 0), (pc, D, 0)]
    return _rows(body, tiled, [], [(D, BF16)], [], tr=_tile(T, 384, 8), name=name, T=T)[0]


def _merge_bwd(dmixed, proj, off, pa, pb, pc, name):
    T, D = pa.shape

    def body(i, t, f, o, a):
        dm = t[0][...]
        for k in range(3):
            g = _sigmoid(t[1 + k][...])
            o[k][...] = (dm * g).astype(BF16)
            o[3][:, k * D:(k + 1) * D] = (dm * t[4 + k][...] * g * (1.0 - g)).astype(BF16)

    tiled = [(dmixed, D, 0), (proj, D, off), (proj, D, off + D), (proj, D, off + 2 * D), (pa, D, 0), (pb, D, 0),
             (pc, D, 0)]
    return _rows(body, tiled, [], [(D, BF16)] * 3 + [(3 * D, BF16)], [], tr=_tile(T, 384, 8), name=name, T=T)


def _gnorm_fwd(y, proj, zoff, nw, name):
    T, D = y.shape
    gs = D // SSD_GROUPS

    def body(i, t, f, o, a):
        s = t[0][...] * _silu(t[1][...])
        for g in range(SSD_GROUPS):
            sg = s[:, g * gs:(g + 1) * gs]
            r = lax.rsqrt(jnp.mean(sg * sg, axis=-1, keepdims=True) + NORM_EPS)
            o[0][:, g * gs:(g + 1) * gs] = (sg * r * f[0][:, g * gs:(g + 1) * gs]).astype(BF16)

    return _rows(body, [(y, D, 0), (proj, D, zoff)], [nw], [(D, BF16)], [], tr=_tile(T, 384, 8), name=name, T=T)[0]


def _gnorm_bwd(dout, y, proj, zoff, nw, name):
    T, D = y.shape
    gs = D // SSD_GROUPS

    def body(i, t, f, o, a):
        dov, yv, zv = t[0][...], t[1][...], t[2][...]
        sz = _silu(zv)
        s = yv * sz
        dsz = _dsilu(zv)
        for g in range(SSD_GROUPS):
            sl = slice(g * gs, (g + 1) * gs)
            sg = s[:, sl]
            r = lax.rsqrt(jnp.mean(sg * sg, axis=-1, keepdims=True) + NORM_EPS)
            sh = sg * r
            dog = dov[:, sl]
            dng = dog * f[0][:, sl]
            ds = r * (dng - sh * jnp.mean(dng * sh, axis=-1, keepdims=True))
            o[0][:, sl] = ds * sz[:, sl]
            o[1][:, sl] = (ds * yv[:, sl] * dsz[:, sl]).astype(BF16)
            a[0][:, sl] += _colsum(dog * sh)

    return _rows(body, [(dout, D, 0), (y, D, 0), (proj, D, zoff)], [nw], [(D, F32), (D, BF16)], [((1, D), F32)],
                 tr=_tile(T, 384, 8), name=name, T=T)


def _loss_bwd(h, tgt, g, seq_len, n_real, name):
    T, D = h.shape
    tr = _tile(seq_len, 384, 8)
    per_seq = seq_len // tr

    def body(i, t, f, o, a):
        x, tg = t[0][...], t[1][...]
        pos = (i % per_seq) * tr + lax.broadcasted_iota(jnp.int32, (tr, 1), 0)
        valid = (pos >= N_META) & (pos < N_META + n_real)
        r = lax.rsqrt(jnp.mean(x * x, axis=-1, keepdims=True) + NORM_EPS)
        xh = x * r
        e = jnp.where(valid, xh * f[0][...] - tg, 0.0)
        a[0][...] += jnp.zeros((1, LANES), F32) + 0.5 * jnp.sum(jnp.sum(e * e, axis=-1, keepdims=True) / D,
                                                              axis=0, keepdims=True)
        dy = e / D
        dng = dy * f[0][...]
        out = r * (dng - xh * jnp.mean(dng * xh, axis=-1, keepdims=True))
        o[0][...] = out
        o[1][...] = out.astype(BF16)
        a[1][...] += _colsum(dy * xh)

    return _rows(body, [(h, D, 0), (tgt, D, 0)], [g], [(D, F32), (D, BF16)], [((1, LANES), F32), ((1, D), F32)], tr=tr,
                 name=name, T=T)


def _lane_is_attn(shape):
    return lax.broadcasted_iota(jnp.int32, shape, len(shape) - 1) < HEADS


def _gate_prep(proj3, col_blk, bias, avec, name):
    B, L, _ = proj3.shape
    Q = Q_BLOCK
    nc = L // Q

    def kern(x_ref, b_ref, a_ref, v_ref, c_ref, carry):
        c = pl.program_id(1)

        @pl.when(c == 0)
        def _():
            carry[...] = jnp.zeros_like(carry)

        x = x_ref[0] + b_ref[...]
        attn = _lane_is_attn(x.shape)
        v = jnp.where(attn, _log_sigmoid(x), _softplus(x))
        w = jnp.where(attn, v, v * a_ref[...])
        cs = _split3_dot(_lower_tri(Q), w) + jnp.where(attn[:1], carry[...], 0.0)
        v_ref[0] = v
        c_ref[0] = cs
        rows = lax.broadcasted_iota(jnp.int32, (Q, 1), 0)
        carry[...] = jnp.sum(jnp.where(rows == Q - 1, cs, 0.0), axis=0, keepdims=True)

    blk = pl.BlockSpec((1, Q, LANES), lambda b, c: (b, c, 0))
    vec = pl.BlockSpec((1, LANES), lambda b, c: (0, 0))
    return pl.pallas_call(
        kern, name=name, grid=(B, nc),
        in_specs=[pl.BlockSpec((1, Q, LANES), lambda b, c: (b, c, col_blk)), vec, vec],
        out_specs=[blk, blk], out_shape=[jax.ShapeDtypeStruct((B, L, LANES), F32)] * 2,
        scratch_shapes=[pltpu.VMEM((1, LANES), F32)],
        compiler_params=_cparams(("parallel", "arbitrary")),
    )(proj3, bias, avec)


def _gate_post(drow, dcol, ddt, proj3, col_blk, vals, bias, avec, name):
    B, L, _ = proj3.shape
    Q = Q_BLOCK
    nc = L // Q

    def kern(dr_ref, dc_ref, dd_ref, x_ref, v_ref, b_ref, a_ref, o_ref, db_ref, da_ref, carry):
        b = pl.program_id(0)
        c = pl.program_id(1)

        @pl.when((b == 0) & (c == 0))
        def _():
            db_ref[...] = jnp.zeros_like(db_ref)
            da_ref[...] = jnp.zeros_like(da_ref)

        @pl.when(c == 0)
        def _():
            carry[...] = jnp.zeros_like(carry)

        x = x_ref[0] + b_ref[...]
        attn = _lane_is_attn(x.shape)
        dcs = dr_ref[0] + dc_ref[0]
        upper = jnp.logical_not(_lower_tri(Q, strict=True))
        rc = _split3_dot(upper, dcs) + jnp.where(attn[:1], carry[...], 0.0)
        rows = lax.broadcasted_iota(jnp.int32, (Q, 1), 0)
        carry[...] = jnp.sum(jnp.where(rows == 0, rc, 0.0), axis=0, keepdims=True)
        dv = jnp.where(attn, rc, dd_ref[0] + rc * a_ref[...])
        dpre = dv * jnp.where(attn, _sigmoid(-x), _sigmoid(x))
        o_ref[0] = dpre.astype(BF16)
        db_ref[...] += _colsum(dpre)
        da_ref[...] += _colsum(jnp.where(attn, 0.0, rc * v_ref[0]))

    rev = pl.BlockSpec((1, Q, LANES), lambda b, c: (b, nc - 1 - c, 0))
    vec = pl.BlockSpec((1, LANES), lambda b, c: (0, 0))
    return pl.pallas_call(
        kern, name=name, grid=(B, nc),
        in_specs=[rev, rev, rev, pl.BlockSpec((1, Q, LANES), lambda b, c: (b, nc - 1 - c, col_blk)), rev, vec, vec],
        out_specs=[rev, vec, vec],
        out_shape=[jax.ShapeDtypeStruct((B, L, LANES), BF16), jax.ShapeDtypeStruct((1, LANES), F32),
                   jax.ShapeDtypeStruct((1, LANES), F32)],
        scratch_shapes=[pltpu.VMEM((1, LANES), F32)],
        compiler_params=_cparams(("arbitrary", "arbitrary")),
    )(drow, dcol, ddt, proj3, vals, bias, avec)


def _lane_col(tile, lane):
    sel = lax.broadcasted_iota(jnp.int32, tile.shape, 1) == lane
    return jnp.sum(jnp.where(sel, tile, 0.0), axis=1, keepdims=True)


AUG = LANES
AUG_A = HEAD_DIM
AUG_B = HEAD_DIM + 3


def _split3(x):
    hi = x.astype(BF16).astype(F32)
    mid = (x - hi).astype(BF16).astype(F32)
    lo = (x - hi - mid).astype(BF16).astype(F32)
    return hi, mid, lo


def _put3(base, lane, first, x):
    hi, mid, lo = _split3(x)
    return jnp.where(lane == first, hi, jnp.where(lane == first + 1, mid, jnp.where(lane == first + 2, lo, base)))


HP = 2
AH = 2
AW = AH * HEAD_DIM


def _other_half(x):
    return pltpu.roll(x, HEAD_DIM, 1)


def _loop_by_twos(n, step, init):
    carry = lax.fori_loop(0, n // 2, lambda t, c: step(2 * t + 1, step(2 * t, c)), init)
    return lax.cond(n % 2 == 1, lambda c: step(n - 1, c), lambda c: c, carry)


def _attn_pack(proj3, cums, name):
    B, L, _ = proj3.shape
    D = HEADS * HEAD_DIM
    nh = HEADS // HP
    tr = _tile(L, 384)
    scale = HEAD_DIM ** -0.5

    def kern(q_ref, k_ref, v_ref, c_ref, qa_ref, ka_ref, va_ref):
        lane = lax.broadcasted_iota(jnp.int32, (tr, AUG), 1)
        head = lane < HEAD_DIM
        ones_a = jnp.where((lane >= AUG_A) & (lane < AUG_A + 3), 1.0, 0.0)
        ones_b = jnp.where((lane >= AUG_B) & (lane < AUG_B + 3), 1.0, 0.0)
        ct = c_ref[0]
        for hp in range(nh):
            cols = slice(hp * LANES, (hp + 1) * LANES)
            for hh in range(HP):
                h = HP * hp + hh
                c = _lane_col(ct, h)
                sel = (lambda t: t) if hh == 0 else _other_half
                qa_ref[0, h] = jnp.where(head, sel(q_ref[0, :, cols]) * scale, _put3(ones_b, lane, AUG_A, c)).astype(BF16)
                ka_ref[0, h] = jnp.where(head, sel(k_ref[0, :, cols]), _put3(ones_a, lane, AUG_B, -c)).astype(BF16)
                va_ref[0, h] = jnp.where(head, sel(v_ref[0, :, cols]), ones_a).astype(BF16)

    def win(k):
        return pl.BlockSpec((1, tr, D), lambda b, i: (b, i, k))

    out = pl.BlockSpec((1, HEADS, tr, AUG), lambda b, i: (b, 0, i, 0))
    return pl.pallas_call(
        kern, name=name, grid=(B, L // tr),
        in_specs=[win(0), win(1), win(2), pl.BlockSpec((1, tr, LANES), lambda b, i: (b, i, 0))],
        out_specs=[out] * 3, out_shape=[jax.ShapeDtypeStruct((B, HEADS, L, AUG), BF16)] * 3,
        compiler_params=_cparams(("parallel", "parallel")),
    )(proj3, proj3, proj3, cums)


def _attn_fwd(qa, ka, va, name, gather=None):
    B, H, L, _ = qa.shape
    tq = _tile(L, 384)
    nq = L // tq
    nh = H // AH
    comm = gather is not None

    def kern(*refs):
        if comm:
            q_ref, k_ref, v_ref, x_ref, y_ref, yb_ref, l_ref, g_ref, send_sems, recv_sems, local_sem = refs
        else:
            q_ref, k_ref, v_ref, y_ref, yb_ref, l_ref = refs
        qi = pl.program_id(2)
        if comm:
            _ride((pl.program_id(0) * nh + pl.program_id(1)) * nq + qi, B * nh * nq,
                  _gather_phases(x_ref, g_ref, send_sems, recv_sems, local_sem))
        qs = [q_ref[0, hh] for hh in range(AH)]
        causal = _lower_tri(tq)

        def step(j, carry, masked):
            rows = pl.ds(pl.multiple_of(j * tq, tq), tq)
            out = []
            for hh in range(AH):
                m, acc = carry[hh]
                s = _dot_nt(qs[hh], k_ref[0, hh, rows, :])
                if masked:
                    s = jnp.where(causal, s, NEG)
                m_new = jnp.maximum(m, jnp.max(s, axis=1, keepdims=True))
                p = jnp.exp(s - m_new)
                out.append((m_new, jnp.exp(m - m_new) * acc + _dot(p.astype(BF16), v_ref[0, hh, rows, :])))
            return tuple(out)

        init = tuple((jnp.full((tq, 1), NEG, F32), jnp.zeros((tq, AUG), F32)) for _ in range(AH))
        carry = _loop_by_twos(qi, lambda j, c: step(j, c, False), init)
        outs = []
        for hh, (m, acc) in enumerate(step(qi, carry, True)):
            l = _lane_col(acc, AUG_A)
            outs.append(acc / l)
            l_ref[0, hh] = m + jnp.log(l)
        head = lax.broadcasted_iota(jnp.int32, (tq, AUG), 1) < HEAD_DIM
        for pp in range(AH // HP):
            y = jnp.where(head, outs[HP * pp], _other_half(outs[HP * pp + 1]))
            y_ref[0, :, pp * LANES:(pp + 1) * LANES] = y
            yb_ref[0, :, pp * LANES:(pp + 1) * LANES] = y.astype(BF16)

    qspec = pl.BlockSpec((1, AH, tq, AUG), lambda b, h, i: (b, h, i, 0))
    kvspec = pl.BlockSpec((1, AH, L, AUG), lambda b, h, i: (b, h, 0, 0))
    lspec = pl.BlockSpec((1, AH, tq, 1), lambda b, h, i: (b, h, i, 0))
    yspec = pl.BlockSpec((1, tq, AW), lambda b, h, i: (b, i, h))
    out_shape = [jax.ShapeDtypeStruct((B, L, H * HEAD_DIM), F32), jax.ShapeDtypeStruct((B, L, H * HEAD_DIM), BF16),
                 jax.ShapeDtypeStruct((B, H, L, 1), F32)]
    if comm:
        out_shape.append(jax.ShapeDtypeStruct((N_DEV,) + gather.shape, gather.dtype))
    return pl.pallas_call(
        kern, name=name, grid=(B, nh, nq), in_specs=[qspec, kvspec, kvspec] + ([ANY] if comm else []),
        out_specs=[yspec, yspec, lspec] + ([ANY] if comm else []), out_shape=out_shape,
        scratch_shapes=COMM_SCRATCH if comm else [],
        compiler_params=_cparams(("arbitrary",) * 3 if comm else ("parallel", "parallel", "arbitrary")),
    )(qa, ka, va, *([gather] if comm else []))


def _attn_bwd(qa, ka, va, y, dy, lse, name, parts=None):
    B, H, L, _ = qa.shape
    tq = _tile(L, 384)
    nq = L // tq
    nh = H // AH
    comm = parts is not None
    scale = HEAD_DIM ** -0.5

    def kern(*refs):
        if comm:
            (q_ref, k_ref, v_ref, y_ref, dy_ref, l_ref, p_ref, dq_ref, dk_ref, dv_ref, dc_ref, r_ref,
             dk_acc, dv_acc, send_sems, recv_sems, local_sem) = refs
        else:
            q_ref, k_ref, v_ref, y_ref, dy_ref, l_ref, dq_ref, dk_ref, dv_ref, dc_ref, dk_acc, dv_acc = refs
        qi = pl.program_id(2)
        hp = pl.program_id(1)
        lane_row = lax.broadcasted_iota(jnp.int32, (1, LANES), 1)
        onehot = [(lane_row == AH * hp + hh).astype(F32) for hh in range(AH)]

        @pl.when((hp == 0) & (qi == 0))
        def _():
            dc_ref[...] = jnp.zeros_like(dc_ref)

        if comm:
            _ride((pl.program_id(0) * nh + pl.program_id(1)) * nq + qi, B * nh * nq,
                  _exchange_phases(p_ref, r_ref, send_sems, recv_sems, local_sem))

        @pl.when(qi == 0)
        def _():
            dk_acc[...] = jnp.zeros_like(dk_acc)
            dv_acc[...] = jnp.zeros_like(dv_acc)

        lane = lax.broadcasted_iota(jnp.int32, (tq, AUG), 1)
        head = lane < HEAD_DIM
        qbs, dobs = [], []
        for hh in range(AH):
            sel = (lambda t: t) if hh % HP == 0 else _other_half
            cols = slice((hh // HP) * LANES, (hh // HP + 1) * LANES)
            qf = q_ref[0, hh].astype(F32)
            dov = jnp.where(head, sel(dy_ref[0, :, cols]), 0.0)
            dsum = jnp.sum(dov * sel(y_ref[0, :, cols]), axis=1, keepdims=True)
            dobs.append(_put3(dov, lane, AUG_A, -dsum).astype(BF16))
            c_t = jnp.sum(jnp.where((lane >= AUG_A) & (lane < AUG_A + 3), qf, 0.0), axis=1, keepdims=True)
            qbs.append(_put3(qf, lane, AUG_A, c_t - l_ref[0, hh]).astype(BF16))
        causal = _lower_tri(tq)

        def step(j, dqs, masked):
            rows = pl.ds(pl.multiple_of(j * tq, tq), tq)
            out = []
            for hh in range(AH):
                kj = k_ref[0, hh, rows, :]
                s = _dot_nt(qbs[hh], kj)
                if masked:
                    s = jnp.where(causal, s, NEG)
                p = jnp.exp(s)
                ds = (p * _dot_nt(dobs[hh], v_ref[0, hh, rows, :])).astype(BF16)
                dv_acc[hh, rows, :] += _dot_tn(p.astype(BF16), dobs[hh])
                dk_acc[hh, rows, :] += _dot_tn(ds, qbs[hh])
                out.append(dqs[hh] + _dot(ds, kj))
            return tuple(out)

        dqs = _loop_by_twos(qi, lambda j, c: step(j, c, False), tuple(jnp.zeros((tq, AUG), F32) for _ in range(AH)))
        dqs = step(qi, dqs, True)
        dc_ref[0, pl.ds(pl.multiple_of(qi * tq, tq), tq), :] += sum(_lane_col(dqs[hh], AUG_A) * onehot[hh]
                                                                    for hh in range(AH))
        for pp in range(AH // HP):
            dq_ref[0, :, pp * LANES:(pp + 1) * LANES] = (
                jnp.where(head, dqs[HP * pp], _other_half(dqs[HP * pp + 1])) * scale).astype(BF16)

        @pl.when(qi == nq - 1)
        def _():
            full = lax.broadcasted_iota(jnp.int32, (L, AUG), 1) < HEAD_DIM
            for pp in range(AH // HP):
                cols = slice(pp * LANES, (pp + 1) * LANES)
                dk_ref[0, :, cols] = jnp.where(full, dk_acc[HP * pp], _other_half(dk_acc[HP * pp + 1])).astype(BF16)
                dv_ref[0, :, cols] = jnp.where(full, dv_acc[HP * pp], _other_half(dv_acc[HP * pp + 1])).astype(BF16)
            dc_ref[0] -= sum(_lane_col(dk_acc[hh], AUG_B) * onehot[hh] for hh in range(AH))

    qspec = pl.BlockSpec((1, AH, tq, AUG), lambda b, h, i: (b, h, i, 0))
    kvspec = pl.BlockSpec((1, AH, L, AUG), lambda b, h, i: (b, h, 0, 0))
    lspec = pl.BlockSpec((1, AH, tq, 1), lambda b, h, i: (b, h, i, 0))
    tmspec = pl.BlockSpec((1, L, LANES), lambda b, h, i: (b, 0, 0))
    yspec = pl.BlockSpec((1, tq, AW), lambda b, h, i: (b, i, h))
    yfull = pl.BlockSpec((1, L, AW), lambda b, h, i: (b, 0, h))
    nat = jax.ShapeDtypeStruct((B, L, H * HEAD_DIM), BF16)
    out_shape = [nat, nat, nat, jax.ShapeDtypeStruct((B, L, LANES), F32)]
    if comm:
        out_shape.append(jax.ShapeDtypeStruct(parts.shape, parts.dtype))
    return pl.pallas_call(
        kern, name=name, grid=(B, nh, nq),
        in_specs=[qspec, kvspec, kvspec, yspec, yspec, lspec] + ([ANY] if comm else []),
        out_specs=[yspec, yfull, yfull, tmspec] + ([ANY] if comm else []), out_shape=out_shape,
        scratch_shapes=[pltpu.VMEM((AH, L, AUG), F32), pltpu.VMEM((AH, L, AUG), F32)] + (COMM_SCRATCH if comm else []),
        compiler_params=_cparams(("parallel", "arbitrary", "arbitrary")),
    )(qa, ka, va, y, dy, lse, *([parts] if comm else []))


PAD = SUBLANES


def _halo_tile(x_ref, i, TR):
    r0 = pl.multiple_of(i * TR, TR)
    before = x_ref[0, pl.ds(pl.multiple_of(jnp.maximum(r0 - PAD, 0), PAD), PAD), :]
    return jnp.concatenate([jnp.where(i > 0, before, 0.0), x_ref[0, pl.ds(r0, TR), :]], axis=0)


def _conv_fwd(x3, x_blk, w, b, n_silu, name):
    B, L, _ = x3.shape
    C = w.shape[1]
    TR = _tile(L, 384, 8)

    def kern(x_ref, w_ref, b_ref, o_ref):
        cb = pl.program_id(1)

        def body(i, carry):
            r0 = pl.multiple_of(i * TR, TR)
            ext = _halo_tile(x_ref, i, TR)
            acc = jnp.zeros((TR, LANES), F32) + b_ref[...]
            for k in range(CONV_K):
                s = CONV_K - 1 - k
                sh = ext if s == 0 else pltpu.roll(ext, s, 0)
                acc = acc + w_ref[k:k + 1, :] * sh[PAD:PAD + TR]
            o_ref[0, pl.ds(r0, TR), :] = jnp.where(cb < n_silu, _silu(acc), acc)
            return carry

        lax.fori_loop(0, L // TR, body, 0)

    return pl.pallas_call(
        kern, name=name, grid=(B, C // LANES),
        in_specs=[pl.BlockSpec((1, L, LANES), lambda b_, c: (b_, 0, x_blk + c)),
                  pl.BlockSpec((CONV_K, LANES), lambda b_, c: (0, c)), pl.BlockSpec((1, LANES), lambda b_, c: (0, c))],
        out_specs=pl.BlockSpec((1, L, LANES), lambda b_, c: (b_, 0, c)),
        out_shape=jax.ShapeDtypeStruct((B, L, C), F32),
        compiler_params=_cparams(("parallel", "parallel")),
    )(x3, w, b)


def _conv_bwd(x3, x_blk, du, w, b, n_silu, name):
    B, L, C = du.shape
    TR = _tile(L, 384, 16)

    def kern(x_ref, du_ref, w_ref, b_ref, dx_ref, dw_ref, dp_s):
        cb = pl.program_id(0)

        @pl.when(pl.program_id(1) == 0)
        def _():
            dw_ref[...] = jnp.zeros_like(dw_ref)

        def pre(i, carry):
            r0 = pl.multiple_of(i * TR, TR)
            ext = _halo_tile(x_ref, i, TR)
            taps = []
            acc = jnp.zeros((TR, LANES), F32) + b_ref[...]
            for k in range(CONV_K):
                s = CONV_K - 1 - k
                sh = ext if s == 0 else pltpu.roll(ext, s, 0)
                taps.append(sh[PAD:PAD + TR])
                acc = acc + w_ref[k:k + 1, :] * taps[-1]
            dv = du_ref[0, pl.ds(r0, TR), :]
            dpre = jnp.where(cb < n_silu, dv * _dsilu(acc), dv)
            dp_s[pl.ds(r0, TR), :] = dpre
            return tuple(c + _colsum(dpre * t) for c, t in zip(carry[:CONV_K], taps)) + (carry[CONV_K] + _colsum(dpre),)

        z = jnp.zeros((1, LANES), F32)
        sums = lax.fori_loop(0, L // TR, pre, (z,) * (CONV_K + 1))
        dp_s[pl.ds(L, PAD), :] = jnp.zeros((PAD, LANES), F32)
        for k in range(CONV_K + 1):
            dw_ref[k:k + 1, :] += sums[k]

        def back(i, carry):
            r0 = pl.multiple_of(i * TR, TR)
            ext = dp_s[pl.ds(r0, TR + PAD), :]
            acc = jnp.zeros((TR, LANES), F32)
            for k in range(CONV_K):
                s = CONV_K - 1 - k
                sh = ext if s == 0 else pltpu.roll(ext, TR + PAD - s, 0)
                acc = acc + w_ref[k:k + 1, :] * sh[0:TR]
            dx_ref[0, pl.ds(r0, TR), :] = acc.astype(BF16)
            return carry

        lax.fori_loop(0, L // TR, back, 0)

    return pl.pallas_call(
        kern, name=name, grid=(C // LANES, B),
        in_specs=[pl.BlockSpec((1, L, LANES), lambda c, b_: (b_, 0, x_blk + c)),
                  pl.BlockSpec((1, L, LANES), lambda c, b_: (b_, 0, c)),
                  pl.BlockSpec((CONV_K, LANES), lambda c, b_: (0, c)), pl.BlockSpec((1, LANES), lambda c, b_: (0, c))],
        out_specs=[pl.BlockSpec((1, L, LANES), lambda c, b_: (b_, 0, c)),
                   pl.BlockSpec((SUBLANES, LANES), lambda c, b_: (0, c))],
        out_shape=[jax.ShapeDtypeStruct((B, L, C), BF16), jax.ShapeDtypeStruct((SUBLANES, C), F32)],
        scratch_shapes=[pltpu.VMEM((L + PAD, LANES), F32)],
        compiler_params=_cparams(("parallel", "arbitrary")),
    )(x3, du, w, b)


def _dot_nt(a, b):
    return lax.dot_general(a, b, (((1,), (1,)), ((), ())), preferred_element_type=F32)


def _dot_tn(a, b):
    return lax.dot_general(a, b, (((0,), (0,)), ((), ())), preferred_element_type=F32)


def _dot(a, b):
    return jnp.dot(a, b, preferred_element_type=F32)


def _ssd_specs(L, nc, b_blk, c_blk):
    pairs_per_group = HEADS // SSD_GROUPS // HP
    return [
        pl.BlockSpec((1, L, LANES), lambda b, h: (b, 0, h)),
        pl.BlockSpec((1, L, SSD_STATE), lambda b, h: (b, 0, b_blk + h // pairs_per_group)),
        pl.BlockSpec((1, L, SSD_STATE), lambda b, h: (b, 0, c_blk + h // pairs_per_group)),
        pl.BlockSpec((1, L, LANES), lambda b, h: (b, 0, 0)),
        pl.BlockSpec((1, L, LANES), lambda b, h: (b, 0, 0)),
        pl.BlockSpec((1, HP, nc, Q_BLOCK), lambda b, h: (b, HEADS // HP + h, 0, 0)),
        pl.BlockSpec((1, LANES), lambda b, h: (0, 0)),
    ]


def _halves(a, b, shape):
    return jnp.where(lax.broadcasted_iota(jnp.int32, shape, 1) < HEAD_DIM, a, b)


def _half_sums(t):
    first = lax.broadcasted_iota(jnp.int32, t.shape, 1) < HEAD_DIM
    lo = jnp.sum(jnp.where(first, t, 0.0), axis=1, keepdims=True)
    return lo, jnp.sum(t, axis=1, keepdims=True) - lo


def _ssd_chunk(c, S, x_ref, b_ref, c_ref, v_ref, cu_ref, ct_ref, lane0):
    Q = Q_BLOCK
    rows = pl.ds(pl.multiple_of(c * Q, Q), Q)
    x = x_ref[0, rows, :]
    Bb = b_ref[0, rows, :].astype(BF16)
    Cb = c_ref[0, rows, :].astype(BF16)
    vt, ct = v_ref[0, rows, :], cu_ref[0, rows, :]
    tri = _lower_tri(Q)
    A, Lm, e_end_h, eAend_h, dts = [], [], [], [], []
    for hh in range(HP):
        dts.append(_lane_col(vt, lane0 + hh))
        A.append(_lane_col(ct, lane0 + hh))
        Ar = ct_ref[0, hh, pl.ds(c, 1), :]
        Aend = _lane_col(Ar, Q - 1)
        Lm.append(jnp.exp(jnp.where(tri, A[hh] - Ar, NEG)))
        e_end_h.append(jnp.exp(Aend - A[hh]))
        eAend_h.append(jnp.exp(Aend))
    shape = (Q, LANES)
    dt = _halves(dts[0], dts[1], shape)
    eA = _halves(jnp.exp(A[0]), jnp.exp(A[1]), shape)
    e_end = _halves(e_end_h[0], e_end_h[1], shape)
    xdt = x * dt
    CB = _dot_nt(Cb, Bb)
    W = xdt * e_end
    srow = lax.broadcasted_iota(jnp.int32, (HP * HEAD_DIM, 1), 0) < HEAD_DIM
    eAend = jnp.where(srow, eAend_h[0], eAend_h[1])
    S_new = S * eAend + _dot_tn(W.astype(BF16), Bb)
    return dict(rows=rows, x=x, Bb=Bb, Cb=Cb, dt=dt, eA=eA, e_end=e_end, e_end_h=e_end_h, eAend=eAend,
                eAend_h=eAend_h, xdt=xdt, Lm=Lm, CB=CB, W=W, S_new=S_new)


def _ssd_fwd(u, b_blk, c_blk, vals, cums, cums_t, dvec, name):
    B, L, _ = u.shape
    nc = L // Q_BLOCK
    nh = HEADS // HP
    PP = HP * HEAD_DIM

    def kern(x_ref, b_ref, c_ref, v_ref, cu_ref, ct_ref, d_ref, y_ref, st_ref):
        lane0 = HEADS + HP * pl.program_id(1)
        dskip = _halves(_lane_col(d_ref[...], lane0), _lane_col(d_ref[...], lane0 + 1), (1, LANES))
        first = lax.broadcasted_iota(jnp.int32, (Q_BLOCK, LANES), 1) < HEAD_DIM

        def body(c, S):
            st_ref[0, 0, c] = S
            q = _ssd_chunk(c, S, x_ref, b_ref, c_ref, v_ref, cu_ref, ct_ref, lane0)
            xb = q["xdt"].astype(BF16)
            yd = jnp.where(first, _dot((q["CB"] * q["Lm"][0]).astype(BF16), xb),
                           _dot((q["CB"] * q["Lm"][1]).astype(BF16), xb))
            z = _dot_nt(q["Cb"], S.astype(BF16))
            y_ref[0, q["rows"], :] = yd + z * q["eA"] + dskip * q["x"]
            return q["S_new"]

        lax.fori_loop(0, nc, body, jnp.zeros((HP * HEAD_DIM, SSD_STATE), F32))

    return pl.pallas_call(
        kern, name=name, grid=(B, nh), in_specs=_ssd_specs(L, nc, b_blk, c_blk),
        out_specs=[pl.BlockSpec((1, L, LANES), lambda b, h: (b, 0, h)),
                   pl.BlockSpec((1, 1, nc, PP, SSD_STATE), lambda b, h: (b, h, 0, 0, 0))],
        out_shape=[jax.ShapeDtypeStruct((B, L, HEADS * HEAD_DIM), F32),
                   jax.ShapeDtypeStruct((B, nh, nc, PP, SSD_STATE), F32)],
        compiler_params=_cparams(("parallel", "arbitrary")),
    )(u, u, u, vals, cums, cums_t, dvec)


def _ssd_bwd(u, b_blk, c_blk, vals, cums, cums_t, dvec, dy, states, name):
    B, L, _ = u.shape
    Q = Q_BLOCK
    nc = L // Q
    N = SSD_STATE
    nh = HEADS // HP
    pairs_per_group = HEADS // SSD_GROUPS // HP
    PP = HP * HEAD_DIM

    def kern(x_ref, b_ref, c_ref, v_ref, cu_ref, ct_ref, d_ref, dy_ref, st_ref,
             dx_ref, dB_ref, dC_ref, ddt_ref, dAc_ref, dAr_ref, dD_ref):
        b = pl.program_id(0)
        h = pl.program_id(1)
        lane0 = HEADS + HP * h
        dskip = _halves(_lane_col(d_ref[...], lane0), _lane_col(d_ref[...], lane0 + 1), (1, LANES))
        lane_row = lax.broadcasted_iota(jnp.int32, (1, LANES), 1)
        onehot = [(lane_row == lane0 + hh).astype(F32) for hh in range(HP)]

        @pl.when(h % pairs_per_group == 0)
        def _():
            dB_ref[...] = jnp.zeros_like(dB_ref)
            dC_ref[...] = jnp.zeros_like(dC_ref)

        @pl.when(h == 0)
        def _():
            ddt_ref[...] = jnp.zeros_like(ddt_ref)
            dAc_ref[...] = jnp.zeros_like(dAc_ref)

        @pl.when((b == 0) & (h == 0))
        def _():
            dD_ref[...] = jnp.zeros_like(dD_ref)

        last_row = lax.broadcasted_iota(jnp.int32, (Q, 1), 0) == Q - 1
        first = lax.broadcasted_iota(jnp.int32, (Q, LANES), 1) < HEAD_DIM
        srow = lax.broadcasted_iota(jnp.int32, (PP, 1), 0) < HEAD_DIM

        def bwd(i, carry):
            dS, dD = carry
            c = nc - 1 - i
            S = st_ref[0, 0, c]
            q = _ssd_chunk(c, S, x_ref, b_ref, c_ref, v_ref, cu_ref, ct_ref, lane0)
            rows, x, Bb, Cb, xdt, Lm, CB = q["rows"], q["x"], q["Bb"], q["Cb"], q["xdt"], q["Lm"], q["CB"]
            dy = dy_ref[0, rows, :]
            dyb = dy.astype(BF16)
            xb = xdt.astype(BF16)
            Sb = S.astype(BF16)
            dD = dD + _colsum(dy * x)
            dyh = [jnp.where(first, dy, 0.0).astype(BF16), jnp.where(first, 0.0, dy).astype(BF16)]
            dM = [_dot_nt(dyh[hh], xb) for hh in range(HP)]
            dxdt = jnp.where(first, _dot_tn((CB * Lm[0]).astype(BF16), dyb), _dot_tn((CB * Lm[1]).astype(BF16), dyb))
            dCBb = (dM[0] * Lm[0] + dM[1] * Lm[1]).astype(BF16)
            dAc, dAr = [], []
            for hh in range(HP):
                G = dM[hh] * CB * Lm[hh]
                dAc.append(jnp.sum(G, axis=1, keepdims=True))
                dAr.append(-jnp.sum(G, axis=0, keepdims=True))
            dC = _dot(dCBb, Bb)
            dBm = _dot_tn(dCBb, Cb)
            z = _dot_nt(Cb, Sb)
            zs = _half_sums(dy * z)
            dzb = (dy * q["eA"]).astype(BF16)
            dC = dC + _dot(dzb, Sb)
            dS_in = _dot_tn(dzb, Cb)
            dSb = dS.astype(BF16)
            dW = _dot_nt(Bb, dSb)
            dBm = dBm + _dot(q["W"].astype(BF16), dSb)
            dxdt = dxdt + dW * q["e_end"]
            des = _half_sums(dW * xdt)
            ss = jnp.sum(dS * S, axis=1, keepdims=True)
            ss_lo = jnp.sum(jnp.where(srow, ss, 0.0), axis=0, keepdims=True)
            ss_h = [ss_lo, jnp.sum(ss, axis=0, keepdims=True) - ss_lo]
            ddts = _half_sums(dxdt * x)
            eA_h = [_lane_col(q["eA"], 0), _lane_col(q["eA"], HEAD_DIM)]
            dAc_tile = jnp.zeros((Q, LANES), F32)
            ddt_tile = jnp.zeros((Q, LANES), F32)
            for hh in range(HP):
                de = des[hh] * q["e_end_h"][hh]
                dAend = ss_h[hh] * q["eAend_h"][hh] + jnp.sum(de, axis=0, keepdims=True)
                col = dAc[hh] + zs[hh] * eA_h[hh] - de + jnp.where(last_row, dAend, 0.0)
                dAc_tile = dAc_tile + col * onehot[hh]
                ddt_tile = ddt_tile + ddts[hh] * onehot[hh]
                dAr_ref[0, hh, pl.ds(c, 1), :] = dAr[hh]
            dx_ref[0, rows, :] = dskip * dy + dxdt * q["dt"]
            dB_ref[0, 0, rows, :] += dBm
            dC_ref[0, 0, rows, :] += dC
            ddt_ref[0, rows, :] += ddt_tile
            dAc_ref[0, rows, :] += dAc_tile
            return dS * q["eAend"] + dS_in, dD

        _, dD = lax.fori_loop(0, nc, bwd, (jnp.zeros((PP, N), F32), jnp.zeros((1, LANES), F32)))
        dlo, dhi = _half_sums(dD)
        dD_ref[...] += dlo * onehot[0] + dhi * onehot[1]

    tm = pl.BlockSpec((1, L, LANES), lambda b, h: (b, 0, 0))
    grp = pl.BlockSpec((1, 1, L, N), lambda b, h: (b, h // pairs_per_group, 0, 0))
    xs = pl.BlockSpec((1, L, LANES), lambda b, h: (b, 0, h))
    return pl.pallas_call(
        kern, name=name, grid=(B, nh),
        in_specs=_ssd_specs(L, nc, b_blk, c_blk) + [xs, pl.BlockSpec((1, 1, nc, PP, N), lambda b, h: (b, h, 0, 0, 0))],
        out_specs=[xs, grp, grp, tm, tm, pl.BlockSpec((1, HP, nc, Q), lambda b, h: (b, h, 0, 0)),
                   pl.BlockSpec((1, LANES), lambda b, h: (0, 0))],
        out_shape=[jax.ShapeDtypeStruct((B, L, HEADS * HEAD_DIM), F32), jax.ShapeDtypeStruct((B, SSD_GROUPS, L, N), F32),
                   jax.ShapeDtypeStruct((B, SSD_GROUPS, L, N), F32), jax.ShapeDtypeStruct((B, L, LANES), F32),
                   jax.ShapeDtypeStruct((B, L, LANES), F32), jax.ShapeDtypeStruct((B, HEADS, nc, Q), F32),
                   jax.ShapeDtypeStruct((1, LANES), F32)],
        compiler_params=_cparams(("arbitrary", "arbitrary")),
    )(u, u, u, vals, cums, cums_t, dvec, dy, states)


LRU_TR = 384
LRU_CB = 512


def _lru_gates(xc, ra, ix, p_ref, first):
    r = _sigmoid(ra + p_ref[0:1, :])
    i = _sigmoid(ix + p_ref[1:2, :])
    ls = _log_sigmoid(p_ref[2:3, :])
    log_a = LRU_C * r * ls
    a = jnp.exp(log_a)
    mult0 = jnp.sqrt(_one_minus_exp(2.0 * log_a))
    mult = jnp.where(first, 1.0, mult0)
    return r, i, ls, a, mult0, mult


def _lru_fwd(u, xc_off, ra, ix, proj3, gate_off, pvec, name):
    B, L, D = ra.shape
    TR, CB = _tile(L, LRU_TR, 8), LRU_CB
    nrt = L // TR

    def kern(xc_ref, ra_ref, ix_ref, g_ref, p_ref, y_ref, hs_ref, a_ref, pa_s, pu_s, carry):
        rt = pl.program_id(2)

        @pl.when(rt == 0)
        def _():
            carry[...] = jnp.zeros_like(carry)

        row = lax.broadcasted_iota(jnp.int32, (TR, 1), 0)
        first = (rt == 0) & (row == 0)
        xc = xc_ref[0]
        r, i, ls, a, mult0, mult = _lru_gates(xc, ra_ref[0], ix_ref[0], p_ref, first)
        a_ref[0] = a
        pa, pu = a, mult * (i * xc)
        sub = row % SUBLANES
        for s in (1, 2, 4):
            ok = sub >= s
            pu = jnp.where(ok, pa * pltpu.roll(pu, s, 0) + pu, pu)
            pa = jnp.where(ok, pa * pltpu.roll(pa, s, 0), pa)
        pa_s[...] = pa
        pu_s[...] = pu
        row8 = lax.broadcasted_iota(jnp.int32, (SUBLANES, 1), 0)

        def gbody(g, c):
            r8 = pl.ds(pl.multiple_of(g * SUBLANES, SUBLANES), SUBLANES)
            hg = pa_s[r8, :] * c + pu_s[r8, :]
            hs_ref[0, r8, :] = hg
            return jnp.sum(jnp.where(row8 == SUBLANES - 1, hg, 0.0), axis=0, keepdims=True)

        carry[...] = lax.fori_loop(0, TR // SUBLANES, gbody, carry[...])
        y_ref[0] = (hs_ref[0] * _gelu(g_ref[0])).astype(BF16)

    def win(off):
        assert off % CB == 0
        return pl.BlockSpec((1, TR, CB), functools.partial(lambda b, j, t, o: (b, t, j + o), o=off // CB))

    return pl.pallas_call(
        kern, name=name, grid=(B, D // CB, nrt),
        in_specs=[win(xc_off), win(0), win(0), win(gate_off), pl.BlockSpec((SUBLANES, CB), lambda b, j, t: (0, j))],
        out_specs=[win(0)] * 3,
        out_shape=[jax.ShapeDtypeStruct((B, L, D), BF16), jax.ShapeDtypeStruct((B, L, D), F32),
                   jax.ShapeDtypeStruct((B, L, D), F32)],
        scratch_shapes=[pltpu.VMEM((TR, CB), F32), pltpu.VMEM((TR, CB), F32), pltpu.VMEM((1, CB), F32)],
        compiler_params=_cparams(("parallel", "parallel", "arbitrary")),
    )(u, ra, ix, proj3, pvec)


def _lru_bwd(dy, proj3, gate_off, hs, a, u, xc_off, ra, ix, pvec, name):
    B, L, D = ra.shape
    TR, CB = _tile(L, LRU_TR, 8), LRU_CB
    nrt = L // TR

    def kern(dy_ref, g_ref, hs_ref, hsp_ref, a_ref, an_ref, xc_ref, ra_ref, ix_ref, p_ref,
             dg_ref, dra_ref, dix_ref, dxc_ref, dp_ref, pb_s, pd_s, g_s, carry):
        b = pl.program_id(1)
        rt = pl.program_id(2)
        t = nrt - 1 - rt

        @pl.when((b == 0) & (rt == 0))
        def _():
            dp_ref[...] = jnp.zeros_like(dp_ref)

        @pl.when(rt == 0)
        def _():
            carry[...] = jnp.zeros_like(carry)

        row = lax.broadcasted_iota(jnp.int32, (TR, 1), 0)
        gate, hsv, av, dyv = g_ref[0], hs_ref[0], a_ref[0], dy_ref[0]
        dg_ref[0] = (dyv * hsv * _dgelu(gate)).astype(BF16)
        a_next = jnp.where(t == nrt - 1, 0.0, an_ref[0, 0:1, :])
        pb = jnp.where(row == TR - 1, a_next, pltpu.roll(av, TR - 1, 0))
        pd = dyv * _gelu(gate)
        sub = row % SUBLANES
        for s in (1, 2, 4):
            ok = sub < SUBLANES - s
            pd = jnp.where(ok, pd + pb * pltpu.roll(pd, TR - s, 0), pd)
            pb = jnp.where(ok, pb * pltpu.roll(pb, TR - s, 0), pb)
        pb_s[...] = pb
        pd_s[...] = pd
        row8 = lax.broadcasted_iota(jnp.int32, (SUBLANES, 1), 0)

        def gbody(i, c):
            r8 = pl.ds(pl.multiple_of((TR // SUBLANES - 1 - i) * SUBLANES, SUBLANES), SUBLANES)
            gg = pd_s[r8, :] + pb_s[r8, :] * c
            g_s[r8, :] = gg
            return jnp.sum(jnp.where(row8 == 0, gg, 0.0), axis=0, keepdims=True)

        carry[...] = lax.fori_loop(0, TR // SUBLANES, gbody, carry[...])
        gv = g_s[...]
        h_first = jnp.where(t == 0, 0.0, hsp_ref[0, TR - 1:TR, :])
        hprev = jnp.where(row == 0, h_first, pltpu.roll(hsv, 1, 0))
        first = (t == 0) & (row == 0)
        xc = xc_ref[0]
        r, i, ls, a2, mult0, mult = _lru_gates(xc, ra_ref[0], ix_ref[0], p_ref, first)
        dxc_ref[0] = gv * mult * i
        dlog_a = gv * hprev * av + jnp.where(first, 0.0, gv * i * xc * (-(av * av) / mult0))
        dra = dlog_a * LRU_C * ls * r * (1.0 - r)
        dix = gv * mult * xc * i * (1.0 - i)
        dra_ref[0] = dra.astype(BF16)
        dix_ref[0] = dix.astype(BF16)
        dp_ref[0:1, :] += _colsum(dra)
        dp_ref[1:2, :] += _colsum(dix)
        dp_ref[2:3, :] += _colsum(dlog_a * LRU_C * r) * _sigmoid(-p_ref[2:3, :])

    def win(off, shift=0):
        assert off % CB == 0
        o = off // CB
        return pl.BlockSpec((1, TR, CB), lambda j, b, rt: (b, jnp.clip(nrt - 1 - rt + shift, 0, nrt - 1), j + o))

    return pl.pallas_call(
        kern, name=name, grid=(D // CB, B, nrt),
        in_specs=[win(0), win(gate_off), win(0), win(0, -1), win(0), win(0, 1), win(xc_off), win(0), win(0),
                  pl.BlockSpec((SUBLANES, CB), lambda j, b, rt: (0, j))],
        out_specs=[win(0)] * 4 + [pl.BlockSpec((SUBLANES, CB), lambda j, b, rt: (0, j))],
        out_shape=[jax.ShapeDtypeStruct((B, L, D), BF16)] * 3 + [jax.ShapeDtypeStruct((B, L, D), F32),
                                                                 jax.ShapeDtypeStruct((SUBLANES, D), F32)],
        scratch_shapes=[pltpu.VMEM((TR, CB), F32)] * 3 + [pltpu.VMEM((1, CB), F32)],
        compiler_params=_cparams(("parallel", "arbitrary", "arbitrary")),
    )(dy, proj3, hs, hs, a, a, u, ra, ix, pvec)


def _sum8(parts, name):
    _, R, C = parts.shape
    tr = _tile(R, 1024, ROW_ALIGN if parts.dtype.itemsize == 2 else SUBLANES)

    def kern(p_ref, o_ref):
        acc = p_ref[0].astype(F32)
        for d in range(1, N_DEV):
            acc = acc + p_ref[d].astype(F32)
        o_ref[...] = acc

    return pl.pallas_call(
        kern, name=name, grid=(R // tr,), in_specs=[pl.BlockSpec((N_DEV, tr, C), lambda i: (0, i, 0))],
        out_specs=pl.BlockSpec((tr, C), lambda i: (i, 0)), out_shape=jax.ShapeDtypeStruct((R, C), F32),
        compiler_params=_cparams(("parallel",)),
    )(parts)


def _adamw(w, g, m, v, name):
    shape = w.shape
    C = shape[-1] if w.ndim > 1 else shape[0]
    R = w.size // C
    w2, g2, m2, v2 = (t.reshape(R, C) for t in (w, g, m, v))
    tr = R
    for cand in range(8, min(R, 512) + 1, 8):
        if R % cand == 0:
            tr = cand

    def kern(w_ref, g_ref, m_ref, v_ref, d_ref, nm_ref, nv_ref):
        gv = g_ref[...]
        nm = ADAM_B1 * m_ref[...] + (1.0 - ADAM_B1) * gv
        nv = ADAM_B2 * v_ref[...] + (1.0 - ADAM_B2) * (gv * gv)
        m_hat = nm / (1.0 - ADAM_B1 ** ADAM_STEP)
        v_hat = nv / (1.0 - ADAM_B2 ** ADAM_STEP)
        d_ref[...] = -ADAM_LR * (m_hat / (jnp.sqrt(v_hat) + ADAM_EPS) + ADAM_WD * w_ref[...])
        nm_ref[...] = nm
        nv_ref[...] = nv

    spec = pl.BlockSpec((tr, C), lambda i: (i, 0))
    outs = pl.pallas_call(
        kern, name=name, grid=(R // tr,), in_specs=[spec] * 4, out_specs=[spec] * 3,
        out_shape=[jax.ShapeDtypeStruct((R, C), F32)] * 3, compiler_params=_cparams(("parallel",)),
    )(w2, g2, m2, v2)
    return tuple(o.reshape(shape) for o in outs)


MESH_ID = pl.DeviceIdType.MESH
ANY = pl.BlockSpec(memory_space=pl.ANY)
N_COPIES = N_DEV - 1
COMM_SCRATCH = [pltpu.SemaphoreType.DMA((N_COPIES,)), pltpu.SemaphoreType.DMA((N_COPIES,)), pltpu.SemaphoreType.DMA]


def _my_place():
    return lax.axis_index("x"), lax.axis_index("y"), lax.axis_index("c")


def _gather_phases(x_ref, out_ref, send_sems, recv_sems, local_sem):
    x, y, c = _my_place()
    me, sibling = (x, y, c), (x, y, 1 - c)
    chips = [(1 - x, y), (x, 1 - y), (1 - x, 1 - y)]

    def slab(px, py, pc):
        return out_ref.at[4 * px + 2 * py + pc]

    def copy(k, block, to, src=None):
        return pltpu.make_async_remote_copy(
            src_ref=slab(*block) if src is None else src, dst_ref=slab(*block),
            send_sem=send_sems.at[k], recv_sem=recv_sems.at[k], device_id=to, device_id_type=MESH_ID)

    mine = pltpu.make_async_copy(x_ref, slab(*me), local_sem)
    first = [copy(0, me, sibling, src=x_ref)] + [copy(1 + j, me, (*chip, c), src=x_ref) for j, chip in enumerate(chips)]
    passed = [copy(4 + j, (*chip, c), sibling) for j, chip in enumerate(chips)]

    def start():
        mine.start()
        for cp in first:
            cp.start()

    def forward():
        for j, chip in enumerate(chips):
            copy(1 + j, (*chip, c), me).wait_recv()
            passed[j].start()

    def finish():
        copy(0, sibling, me).wait_recv()
        for j, chip in enumerate(chips):
            copy(4 + j, (*chip, 1 - c), me).wait_recv()
        for cp in first + passed:
            cp.wait_send()
        mine.wait()

    return start, forward, finish


def _exchange_phases(p_ref, out_ref, send_sems, recv_sems, local_sem):
    x, y, c = _my_place()
    my_idx = 4 * x + 2 * y + c
    mine = pltpu.make_async_copy(p_ref.at[my_idx], out_ref.at[my_idx], local_sem)
    copies = []
    for k in range(1, N_DEV):
        px, py, pc = x ^ (k >> 2), y ^ ((k >> 1) & 1), c ^ (k & 1)
        copies.append(pltpu.make_async_remote_copy(
            src_ref=p_ref.at[4 * px + 2 * py + pc], dst_ref=out_ref.at[my_idx],
            send_sem=send_sems.at[k - 1], recv_sem=recv_sems.at[k - 1], device_id=(px, py, pc),
            device_id_type=MESH_ID))

    def start():
        mine.start()
        for cp in copies:
            cp.start()

    def finish():
        for cp in copies:
            cp.wait()
        mine.wait()

    return start, finish


def _ride(lin, total, phases):
    assert total >= 3
    marks = [0, total - 1] if len(phases) == 2 else [0, total // 2, total - 1]
    for mark, phase in zip(marks, phases):
        pl.when(lin == mark)(phase)


def _all_gather(xs, name):
    R, C = xs.shape

    def body(x_ref, out_ref, send_sems, recv_sems, local_sem):
        for phase in _gather_phases(x_ref, out_ref, send_sems, recv_sems, local_sem):
            phase()

    return pl.pallas_call(
        body, name=name, out_shape=jax.ShapeDtypeStruct((N_DEV, R, C), xs.dtype), in_specs=[ANY], out_specs=ANY,
        scratch_shapes=COMM_SCRATCH,
    )(xs)


def _exchange(parts, name):
    def body(p_ref, out_ref, send_sems, recv_sems, local_sem):
        for phase in _exchange_phases(p_ref, out_ref, send_sems, recv_sems, local_sem):
            phase()

    return pl.pallas_call(
        body, name=name, out_shape=jax.ShapeDtypeStruct(parts.shape, parts.dtype), in_specs=[ANY], out_specs=ANY,
        scratch_shapes=COMM_SCRATCH,
    )(parts)


D_XBC_EXTRA = 2 * SSD_GROUPS * SSD_STATE
SMALL_W = LANES
ROW_ALIGN = 16


def _layout(D):
    d_xbc = D + D_XBC_EXTRA
    off = dict(qkv=0, z=3 * D, merge=4 * D, gate=7 * D, conv=8 * D, xr=8 * D + d_xbc, small=9 * D + d_xbc)
    off["n_all"] = off["small"] + SMALL_W
    off["d_xbc"] = d_xbc
    off["conv_c"] = d_xbc + D
    return off


def _w_in_map(D):
    lo = _layout(D)
    widths = [("q", D, 0), ("k", D, D), ("v", D, 2 * D), ("f", HEADS, lo["small"]), ("z", D, lo["z"]),
              ("xbc", lo["d_xbc"], lo["conv"]), ("dt", HEADS, lo["small"] + HEADS), ("xr", D, lo["xr"]),
              ("gate", D, lo["gate"]), ("merge", 3 * D, lo["merge"])]
    out, o = [], 0
    for _, w, mine in widths:
        out.append((o, w, mine))
        o += w
    return out


def _padded(c):
    return -(-c // ROW_ALIGN) * ROW_ALIGN


def _permute_rows(src, pieces, name):
    R, C = src.shape
    n_out = sum(n for _, n in pieces)

    def kern(x_ref, o_ref):
        o = 0
        for start, n in pieces:
            if start is None:
                o_ref[o:o + n, :] = jnp.zeros((n, LANES), src.dtype)
            else:
                o_ref[o:o + n, :] = x_ref[start:start + n, :]
            o += n

    return pl.pallas_call(
        kern, name=name, grid=(C // LANES,), in_specs=[pl.BlockSpec((R, LANES), lambda i: (0, i))],
        out_specs=pl.BlockSpec((n_out, LANES), lambda i: (0, i)), out_shape=jax.ShapeDtypeStruct((n_out, C), src.dtype),
        compiler_params=_cparams(("parallel",)),
    )(src)


def _reorder_rows(wt, D, c, name="reorder_w_in"):
    cp = _padded(c)
    lo = _layout(D)
    pieces = []
    for a, w, mine in sorted(_w_in_map(D), key=lambda t: t[2]):
        b = a + w
        while a < b:
            j = a // c
            e = min(b, (j + 1) * c)
            pieces.append((j * cp + a - j * c, e - a))
            a = e
    pieces.append((None, lo["n_all"] - lo["small"] - 2 * HEADS))
    return _permute_rows(wt, pieces, name)


def _restore_rows(dwt, D, c, name="restore_w_in"):
    cp = _padded(c)
    segs = _w_in_map(D)
    pieces = []
    for j in range(N_DEV):
        a, b = j * c, (j + 1) * c
        for s0, w, mine in segs:
            lo_, hi_ = max(a, s0), min(b, s0 + w)
            if lo_ < hi_:
                pieces.append((mine + lo_ - s0, hi_ - lo_))
        if cp > c:
            pieces.append((None, cp - c))
    return _permute_rows(dwt, pieces, name)


def _block_diag(w):
    H, n, _ = w.shape
    tiled = jnp.tile(w.reshape(H * n, n), (1, H))
    r = lax.broadcasted_iota(jnp.int32, (H * n, H * n), 0) // n
    c = lax.broadcasted_iota(jnp.int32, (H * n, H * n), 1) // n
    return jnp.where(r == c, tiled, jnp.zeros_like(tiled))


def _diag_blocks(m, H):
    n = m.shape[0] // H
    keep = jnp.eye(H, dtype=m.dtype)[:, None, :, None]
    return jnp.sum(m.reshape(H, n, H, n) * keep, axis=2)


def _to_heads(t, B, L):
    return t.reshape(B, L, HEADS, HEAD_DIM).transpose(0, 2, 1, 3)


def _from_heads(t4):
    B, H, L, P = t4.shape
    return t4.transpose(0, 2, 1, 3).reshape(B * L, H * P)


def _rows_to_tm(rows):
    B, H, nc, Q = rows.shape
    return rows.reshape(B, H, nc * Q).transpose(0, 2, 1)


def _ffn_fwd(h, g, wgu_t, wd, tag):
    n = _norm_fwd(h, g, tag + "_norm")
    gate, up, act = _ffn_up_act(n, wgu_t, tag + "_up")
    out = _mm(act, wd, res=h, scale=0.5, name=tag + "_down")
    return out, (h, n, gate, up, act)


def _ffn_bwd(dh, dhb, saved, g, wgu_t, wd, tag):
    h, n, gate, up, act = saved
    dgu = _ffn_down_dx_act(dhb, wd, gate, up, tag + "_down_dx")
    dwd = _mm(act, dhb, ta=True, scale=0.5, name=tag + "_down_dw")
    dwgu_t = _mm(dgu, n, ta=True, tn=1024, name=tag + "_up_dw")
    dn = _mm(dgu, wgu_t, name=tag + "_up_dx")
    dh_in, dhb_in, dg = _norm_bwd(h, dn, dh, g, tag + "_norm_bwd")
    return dh_in, dhb_in, dict(norm=dg, gu=dwgu_t, down=dwd)


def _mixer_fwd(h, p, B, L, gather=None):
    T, D = h.shape
    lo = _layout(D)
    n = _norm_fwd(h, p["gm"], "mix_norm")
    proj = _mm(n, p["w_all_t"], tb=True, name="mix_in")
    proj3 = proj.reshape(B, L, lo["n_all"])
    vals, cums = _gate_prep(proj3, lo["small"] // LANES, p["small_bias"], p["avec"], "gate_prep")
    cums_t = cums[..., :2 * HEADS].transpose(0, 2, 1).reshape(B, 2 * HEADS, L // Q_BLOCK, Q_BLOCK)
    qa, ka, va = _attn_pack(proj3, cums, "attn_pack")
    y_a3, y_ab3, lse, *gathered = _attn_fwd(qa, ka, va, "attn_fwd", gather)
    y_a, y_ab = y_a3.reshape(T, D), y_ab3.reshape(T, D)
    u = _conv_fwd(proj3, lo["conv"] // LANES, p["conv_w"], p["conv_b"], lo["d_xbc"] // LANES, "conv_fwd")
    b_blk = D // LANES
    c_blk = b_blk + SSD_GROUPS * SSD_STATE // LANES
    y_s3, ssd_states = _ssd_fwd(u, b_blk, c_blk, vals, cums, cums_t, p["dvec"], "ssd_fwd")
    y_s = y_s3.reshape(T, D)
    yb = _gnorm_fwd(y_s, proj, lo["z"], p["ssd_norm"], "gnorm_fwd")
    u2 = u.reshape(T, lo["conv_c"])
    ra = _mm(u2, p["wa"], a_off=(0, lo["d_xbc"]), dims=(T, D, D), tk=512, name="lru_ra")
    ix = _mm(u2, p["wx"], a_off=(0, lo["d_xbc"]), dims=(T, D, D), tk=512, name="lru_ix")
    yc, hs, a = _lru_fwd(u, lo["d_xbc"], ra.reshape(B, L, D), ix.reshape(B, L, D), proj3, lo["gate"], p["pvec"],
                         "lru_fwd")
    yc = yc.reshape(T, D)
    pa = _mm(y_ab, p["wba"], name="branch_attn")
    pb = _mm(yb, p["wbs"], name="branch_ssd")
    pc = _mm(yc, p["wbl"], name="branch_lru")
    mixed = _merge_fwd(proj, lo["merge"], pa, pb, pc, "merge_fwd")
    out = _mm(mixed, p["wout"], res=h, name="mix_out")
    saved = dict(h=h, n=n, proj=proj, qa=qa, ka=ka, va=va, vals=vals, cums=cums, cums_t=cums_t, lse=lse, y_a=y_a, y_ab=y_ab,
                 u=u, y_s=y_s, ssd_states=ssd_states, yb=yb, ra=ra, ix=ix, yc=yc, hs=hs, a=a, pa=pa, pb=pb, pc=pc, mixed=mixed)
    return out, saved, (gathered[0] if gathered else None)


def _mixer_bwd(dh, dhb, s, p, B, L, parts=None):
    T, D = dh.shape
    lo = _layout(D)
    proj, u = s["proj"], s["u"]
    proj3 = proj.reshape(B, L, lo["n_all"])
    g = {}
    dmixed = _mm(dhb, p["wout"], tb=True, name="mix_out_dx")
    g["wout"] = _mm(s["mixed"], dhb, ta=True, name="mix_out_dw")
    dpa, dpb, dpc, dmerge = _merge_bwd(dmixed, proj, lo["merge"], s["pa"], s["pb"], s["pc"], "merge_bwd")
    dy_a = _mm(dpa, p["wba"], tb=True, name="branch_attn_dx")
    g["wba"] = _mm(s["y_ab"], dpa, ta=True, name="branch_attn_dw")
    dyb = _mm(dpb, p["wbs"], tb=True, name="branch_ssd_dx")
    g["wbs"] = _mm(s["yb"], dpb, ta=True, name="branch_ssd_dw")
    dyc = _mm(dpc, p["wbl"], tb=True, name="branch_lru_dx")
    g["wbl"] = _mm(s["yc"], dpc, ta=True, name="branch_lru_dw")
    dgate, dra, dix, dxc, g["pvec"] = _lru_bwd(dyc.reshape(B, L, D), proj3, lo["gate"], s["hs"], s["a"], u, lo["d_xbc"],
                                               s["ra"].reshape(B, L, D), s["ix"].reshape(B, L, D), p["pvec"], "lru_bwd")
    dra, dix = dra.reshape(T, D), dix.reshape(T, D)
    u2 = u.reshape(T, lo["conv_c"])
    g["wa"] = _mm(u2, dra, ta=True, a_off=(0, lo["d_xbc"]), dims=(D, D, T), tm=512, name="lru_ra_dw")
    g["wx"] = _mm(u2, dix, ta=True, a_off=(0, lo["d_xbc"]), dims=(D, D, T), tm=512, name="lru_ix_dw")
    dxc = _mm(dra, p["wa"], tb=True, res=dxc.reshape(T, D), name="lru_ra_dx")
    dxc = _mm(dix, p["wx"], tb=True, res=dxc, name="lru_ix_dx")
    dy_s, dz, g["ssd_norm"] = _gnorm_bwd(dyb, s["y_s"], proj, lo["z"], p["ssd_norm"], "gnorm_bwd")
    b_blk = D // LANES
    c_blk = b_blk + SSD_GROUPS * SSD_STATE // LANES
    dxs, dBg, dCg, ddt_tm, dAc_tm, dAr, g["dvec"] = _ssd_bwd(u, b_blk, c_blk, s["vals"], s["cums"], s["cums_t"],
                                                             p["dvec"], dy_s.reshape(B, L, D), s["ssd_states"], "ssd_bwd")
    grp = lambda t: t.transpose(0, 2, 1, 3).reshape(B, L, SSD_GROUPS * SSD_STATE)
    du = jnp.concatenate([dxs, grp(dBg), grp(dCg), dxc.reshape(B, L, D)], axis=-1)
    dconv, g["conv_wb"] = _conv_bwd(proj3, lo["conv"] // LANES, du, p["conv_w"], p["conv_b"], lo["d_xbc"] // LANES,
                                    "conv_bwd")
    dq3, dk3, dv3, dc_tm, *recv = _attn_bwd(s["qa"], s["ka"], s["va"], s["y_a"].reshape(B, L, D),
                                            dy_a.reshape(B, L, D), s["lse"], "attn_bwd", parts)
    drow_tm = dc_tm + jnp.pad(_rows_to_tm(dAr), ((0, 0), (0, 0), (HEADS, LANES - 2 * HEADS)))
    dsmall, g["small_bias"], g["avec"] = _gate_post(drow_tm, dAc_tm, ddt_tm, proj3, lo["small"] // LANES, s["vals"],
                                                    p["small_bias"], p["avec"], "gate_post")
    dproj = jnp.concatenate([dq3.reshape(T, D), dk3.reshape(T, D), dv3.reshape(T, D), dz, dmerge, dgate.reshape(T, D), dconv.reshape(T, lo["conv_c"]),
                             dsmall.reshape(T, SMALL_W)], axis=1)
    g["w_all_t"] = _mm(dproj, s["n"], ta=True, tn=1024, name="mix_in_dw")
    dn = _mm(dproj, p["w_all_t"], name="mix_in_dx")
    dh_in, dhb_in, g["gm"] = _norm_bwd(s["h"], dn, dh, p["gm"], "mix_norm_bwd")
    return dh_in, dhb_in, g, (recv[0] if recv else None)


def _small_vec(a, b):
    return jnp.concatenate([a, b, jnp.zeros((LANES - 2 * HEADS,), F32)])[None, :]


def _layer_params(w):
    zeros16 = jnp.zeros((HEADS,), F32)
    pvec = jnp.concatenate([w["lru_b_a"][None], w["lru_b_x"][None], w["lru_lambda"][None],
                            jnp.zeros((SUBLANES - 3, w["lru_b_a"].shape[0]), F32)], axis=0)
    return dict(
        g1=w["ffn1_norm"][None], gu1=w["ffn1_w_gate_up"], d1=w["ffn1_w_down"],
        gm=w["mix_norm"][None], w_all_t=w["w_in"],
        small_bias=_small_vec(w["fox_forget_bias"], w["ssd_dt_bias"]),
        avec=_small_vec(zeros16, -jnp.exp(w["ssd_a_log"])), dvec=_small_vec(zeros16, w["ssd_d"]),
        conv_w=jnp.concatenate([w["ssd_conv_w"], w["lru_conv_w"]], axis=1),
        conv_b=jnp.concatenate([w["ssd_conv_b"], w["lru_conv_b"]])[None],
        ssd_norm=w["ssd_norm"][None],
        wa=_block_diag(w["lru_w_a"]).astype(BF16), wx=_block_diag(w["lru_w_x"]).astype(BF16), pvec=pvec,
        wba=w["w_branch_attn"], wbs=w["w_branch_ssd"], wbl=w["w_branch_lru"], wout=w["w_out"],
        g2=w["ffn2_norm"][None], gu2=w["ffn2_w_gate_up"], d2=w["ffn2_w_down"],
    )


def _layer_fwd(h, p, B, L, gather=None):
    h, s1 = _ffn_fwd(h, p["g1"], p["gu1"], p["d1"], "ffn1")
    h, sm, gathered = _mixer_fwd(h, p, B, L, gather)
    h, s2 = _ffn_fwd(h, p["g2"], p["gu2"], p["d2"], "ffn2")
    return h, (s1, sm, s2), gathered


def _layer_bwd(dh, dhb, saved, p, w, B, L, parts=None):
    s1, sm, s2 = saved
    D = dh.shape[1]
    d_xbc = D + D_XBC_EXTRA
    dh, dhb, f2 = _ffn_bwd(dh, dhb, s2, p["g2"], p["gu2"], p["d2"], "ffn2")
    dh, dhb, gm, recv = _mixer_bwd(dh, dhb, sm, p, B, L, parts)
    dh, dhb, f1 = _ffn_bwd(dh, dhb, s1, p["g1"], p["gu1"], p["d1"], "ffn1")
    sb, av = gm["small_bias"][0], gm["avec"][0]
    cw = gm["conv_wb"]
    grads = dict(
        ffn1_norm=f1["norm"][0], ffn1_w_gate_up=f1["gu"], ffn1_w_down=f1["down"],
        mix_norm=gm["gm"][0], w_in=gm["w_all_t"],
        fox_forget_bias=sb[:HEADS], ssd_conv_w=cw[:CONV_K, :d_xbc], ssd_conv_b=cw[CONV_K, :d_xbc],
        ssd_dt_bias=sb[HEADS:2 * HEADS], ssd_a_log=av[HEADS:2 * HEADS] * (-jnp.exp(w["ssd_a_log"])),
        ssd_d=gm["dvec"][0, HEADS:2 * HEADS], ssd_norm=gm["ssd_norm"][0],
        lru_conv_w=cw[:CONV_K, d_xbc:], lru_conv_b=cw[CONV_K, d_xbc:],
        lru_w_a=_diag_blocks(gm["wa"], HEADS), lru_b_a=gm["pvec"][0], lru_w_x=_diag_blocks(gm["wx"], HEADS),
        lru_b_x=gm["pvec"][1], lru_lambda=gm["pvec"][2],
        w_branch_attn=gm["wba"], w_branch_ssd=gm["wbs"], w_branch_lru=gm["wbl"], w_out=gm["wout"],
        ffn2_norm=f2["norm"][0], ffn2_w_gate_up=f2["gu"], ffn2_w_down=f2["down"],
    )
    return dh, dhb, grads, recv


LAYER_NAMES = ["ffn1_norm", "ffn1_w_gate_up", "ffn1_w_down", "mix_norm", "w_in", "fox_forget_bias", "ssd_conv_w",
               "ssd_conv_b", "ssd_dt_bias", "ssd_a_log", "ssd_d", "ssd_norm", "lru_conv_w", "lru_conv_b", "lru_w_a",
               "lru_b_a", "lru_w_x", "lru_b_x", "lru_lambda", "w_branch_attn", "w_branch_ssd", "w_branch_lru", "w_out",
               "ffn2_norm", "ffn2_w_gate_up", "ffn2_w_down"]
WEIGHT_NAMES = ["meta_tokens"] + LAYER_NAMES + ["final_norm"]


def _local_step(x, target, meta, final_norm, depth, layer_weights, pack_next=None, pack_grads=None):
    B, S, D = x.shape
    L = -(-(N_META + S) // Q_BLOCK) * Q_BLOCK
    h = jnp.concatenate([jnp.broadcast_to(meta[None], (B, N_META, D)), x,
                         jnp.zeros((B, L - N_META - S, D), F32)], axis=1).reshape(B * L, D)
    weights, params, saved = [], [], []
    gathered = None
    for l in range(depth):
        w = layer_weights(l, gathered)
        p = _layer_params(w)
        nxt = pack_next(l + 1) if (pack_next is not None and l + 1 < depth) else None
        h, s, gathered = _layer_fwd(h, p, B, L, nxt)
        weights.append(w)
        params.append(p)
        saved.append(s)
    tgt = jnp.pad(target, ((0, 0), (N_META, L - N_META - S), (0, 0))).reshape(B * L, D)
    dh, dhb, loss, dfinal = _loss_bwd(h, tgt, final_norm[None], L, S, "loss")
    grads = [None] * depth
    received, parts = {}, None
    for l in reversed(range(depth)):
        dh, dhb, grads[l], recv = _layer_bwd(dh, dhb, saved[l], params[l], weights[l], B, L, parts)
        if recv is not None:
            received[l + 1] = recv
        parts = pack_grads(grads[l]) if pack_grads is not None else None
    dh3 = dh.reshape(B, L, D)
    return (loss, dh3[:, N_META:N_META + S], jnp.sum(dh3[:, :N_META], axis=0), grads, dfinal[0], received, parts)


BIG_NAMES = ["ffn1_w_gate_up", "ffn1_w_down", "w_in", "w_branch_attn", "w_branch_ssd", "w_branch_lru", "w_out",
             "ffn2_w_gate_up", "ffn2_w_down"]
COL_SHARDED = {"ffn1_w_gate_up", "w_in", "ffn2_w_gate_up"}
SMALL_SHARDED = ["meta_tokens", "ssd_conv_w", "lru_conv_w"]
SMALL_NAMES = [n for n in LAYER_NAMES if n not in BIG_NAMES]


def _shard_rows(name, shape):
    return _padded(shape[1]) if name in COL_SHARDED else shape[0]


def _pack_shards(shards):
    rows = []
    for n in BIG_NAMES:
        s = shards[n]
        if n in COL_SHARDED:
            s = jnp.pad(s.T, ((0, _padded(s.shape[1]) - s.shape[1]), (0, 0)))
        rows.append(s)
    return jnp.concatenate(rows, axis=0)


def _unpack_gathered(gathered, shapes, D):
    out, o = {}, 0
    for n in BIG_NAMES:
        r = _shard_rows(n, shapes[n])
        out[n] = gathered[:, o:o + r].reshape(N_DEV * r, D)
        o += r
    out["w_in"] = _reorder_rows(out["w_in"], D, shapes["w_in"][1])
    return out


def _pack_full_grads(grads, shapes, D):
    slabs = []
    for n in BIG_NAMES:
        g = grads[n]
        if n == "w_in":
            g = _restore_rows(g, D, shapes[n][1])
        slabs.append(g.reshape(N_DEV, _shard_rows(n, shapes[n]), D))
    return jnp.concatenate(slabs, axis=1)


def _unpack_local(rows, shapes):
    out, o = {}, 0
    for n in BIG_NAMES:
        r = _shard_rows(n, shapes[n])
        blk = rows[o:o + r]
        out[n] = blk[:shapes[n][1]].T if n in COL_SHARDED else blk
        o += r
    return out


def _as_rows(flat):
    n = flat.shape[0]
    unit = LANES * SUBLANES
    total = -(-n // unit) * unit
    return jnp.pad(flat, (0, total - n)).reshape(total // LANES, LANES)


def _flatten_list(arrs):
    return _as_rows(jnp.concatenate([a.reshape(-1) for a in arrs]))


def _split_like(rows, shapes):
    flat = rows.reshape(-1)
    out, o = [], 0
    for s in shapes:
        n = math.prod(s)
        out.append(flat[o:o + n].reshape(s))
        o += n
    return out


def _gather_last(rows8, shape):
    lead, c = shape[:-1], shape[-1]
    t = rows8.reshape((N_DEV,) + tuple(lead) + (c,))
    return jnp.moveaxis(t, 0, -2).reshape(tuple(lead) + (N_DEV * c,))


def kernel(x, meta_tokens, ffn1_norm, ffn1_w_gate_up, ffn1_w_down, mix_norm, w_in, fox_forget_bias, ssd_conv_w, ssd_conv_b, ssd_dt_bias, ssd_a_log, ssd_d, ssd_norm, lru_conv_w, lru_conv_b, lru_w_a, lru_b_a, lru_w_x, lru_b_x, lru_lambda, w_branch_attn, w_branch_ssd, w_branch_lru, w_out, ffn2_norm, ffn2_w_gate_up, ffn2_w_down, final_norm, loss_target, m_meta_tokens, m_ffn1_norm, m_ffn1_w_gate_up, m_ffn1_w_down, m_mix_norm, m_w_in, m_fox_forget_bias, m_ssd_conv_w, m_ssd_conv_b, m_ssd_dt_bias, m_ssd_a_log, m_ssd_d, m_ssd_norm, m_lru_conv_w, m_lru_conv_b, m_lru_w_a, m_lru_b_a, m_lru_w_x, m_lru_b_x, m_lru_lambda, m_w_branch_attn, m_w_branch_ssd, m_w_branch_lru, m_w_out, m_ffn2_norm, m_ffn2_w_gate_up, m_ffn2_w_down, m_final_norm, v_meta_tokens, v_ffn1_norm, v_ffn1_w_gate_up, v_ffn1_w_down, v_mix_norm, v_w_in, v_fox_forget_bias, v_ssd_conv_w, v_ssd_conv_b, v_ssd_dt_bias, v_ssd_a_log, v_ssd_d, v_ssd_norm, v_lru_conv_w, v_lru_conv_b, v_lru_w_a, v_lru_b_a, v_lru_w_x, v_lru_b_x, v_lru_lambda, v_w_branch_attn, v_w_branch_ssd, v_w_branch_lru, v_w_out, v_ffn2_norm, v_ffn2_w_gate_up, v_ffn2_w_down, v_final_norm):
    weights = dict(zip(WEIGHT_NAMES, (meta_tokens, ffn1_norm, ffn1_w_gate_up, ffn1_w_down, mix_norm, w_in, fox_forget_bias, ssd_conv_w, ssd_conv_b, ssd_dt_bias, ssd_a_log, ssd_d, ssd_norm, lru_conv_w, lru_conv_b, lru_w_a, lru_b_a, lru_w_x, lru_b_x, lru_lambda, w_branch_attn, w_branch_ssd, w_branch_lru, w_out, ffn2_norm, ffn2_w_gate_up, ffn2_w_down, final_norm,)))
    mom1 = dict(zip(WEIGHT_NAMES, (m_meta_tokens, m_ffn1_norm, m_ffn1_w_gate_up, m_ffn1_w_down, m_mix_norm, m_w_in, m_fox_forget_bias, m_ssd_conv_w, m_ssd_conv_b, m_ssd_dt_bias, m_ssd_a_log, m_ssd_d, m_ssd_norm, m_lru_conv_w, m_lru_conv_b, m_lru_w_a, m_lru_b_a, m_lru_w_x, m_lru_b_x, m_lru_lambda, m_w_branch_attn, m_w_branch_ssd, m_w_branch_lru, m_w_out, m_ffn2_norm, m_ffn2_w_gate_up, m_ffn2_w_down, m_final_norm,)))
    mom2 = dict(zip(WEIGHT_NAMES, (v_meta_tokens, v_ffn1_norm, v_ffn1_w_gate_up, v_ffn1_w_down, v_mix_norm, v_w_in, v_fox_forget_bias, v_ssd_conv_w, v_ssd_conv_b, v_ssd_dt_bias, v_ssd_a_log, v_ssd_d, v_ssd_norm, v_lru_conv_w, v_lru_conv_b, v_lru_w_a, v_lru_b_a, v_lru_w_x, v_lru_b_x, v_lru_lambda, v_w_branch_attn, v_w_branch_ssd, v_w_branch_lru, v_w_out, v_ffn2_norm, v_ffn2_w_gate_up, v_ffn2_w_down, v_final_norm,)))
    depth = ffn1_norm.shape[0]
    D = x.shape[-1]
    my_idx = 4 * lax.axis_index("x") + 2 * lax.axis_index("y") + lax.axis_index("c")

    small_shapes = [weights[n].shape for n in SMALL_SHARDED]
    gathered = _all_gather(_flatten_list([weights[n] for n in SMALL_SHARDED]), "gather_small").reshape(N_DEV, -1)
    small_full, o = {}, 0
    for n, s in zip(SMALL_SHARDED, small_shapes):
        k = math.prod(s)
        small_full[n] = _gather_last(gathered[:, o:o + k], s)
        o += k

    shard_shapes = {n: weights[n].shape[1:] for n in BIG_NAMES}
    pack_next = lambda l: _pack_shards({n: weights[n][l].astype(BF16) for n in BIG_NAMES})

    def layer_weights(l, gathered):
        if gathered is None:
            gathered = _all_gather(pack_next(l), "gather_weights")
        w = _unpack_gathered(gathered, shard_shapes, D)
        for n in SMALL_NAMES:
            w[n] = small_full[n][l] if n in SMALL_SHARDED else weights[n][l]
        return w

    pack_grads = lambda g: _pack_full_grads(g, shard_shapes, D).astype(BF16)
    loss, dx, dmeta, grads, dfinal, received, parts = _local_step(
        x, loss_target, small_full["meta_tokens"], final_norm, depth, layer_weights, pack_next, pack_grads)
    received[0] = _exchange(parts, "exchange_grads")
    loss = lax.psum(loss[0, 0], ("x", "y", "c"))
    summed = {n: [] for n in WEIGHT_NAMES}
    for l in range(depth):
        local = _unpack_local(_sum8(received[l], "sum_grads"), shard_shapes)
        for n in BIG_NAMES:
            summed[n].append(local[n])

    small_list = [dmeta, dfinal] + [grads[l][n] for l in range(depth) for n in SMALL_NAMES]
    total = _sum8(_all_gather(_flatten_list(small_list), "gather_small_grads"), "sum_small_grads")
    parts = _split_like(total, [a.shape for a in small_list])
    full_small = {"meta_tokens": parts[0], "final_norm": parts[1]}
    for i, n in enumerate(SMALL_NAMES):
        full_small[n] = jnp.stack([parts[2 + l * len(SMALL_NAMES) + i] for l in range(depth)])
    grad = {}
    for n in WEIGHT_NAMES:
        if n in BIG_NAMES:
            grad[n] = jnp.stack(summed[n])
        elif n in SMALL_SHARDED:
            c = weights[n].shape[-1]
            grad[n] = lax.dynamic_slice_in_dim(full_small[n], my_idx * c, c, axis=full_small[n].ndim - 1)
        else:
            grad[n] = full_small[n]

    delta, new_m, new_v = {}, {}, {}
    for n in WEIGHT_NAMES:
        delta[n], new_m[n], new_v[n] = _adamw(weights[n], grad[n], mom1[n], mom2[n], "adamw_" + n)
    return (loss, dx, *[grad[n] for n in WEIGHT_NAMES], *[delta[n] for n in WEIGHT_NAMES],
            *[new_m[n] for n in WEIGHT_NAMES], *[new_v[n] for n in WEIGHT_NAMES])
```

```python
import functools
import math

import jax
import jax.numpy as jnp
from jax import lax
from jax.experimental import pallas as pl
from jax.experimental.pallas import tpu as pltpu

F32 = jnp.float32
BF16 = jnp.bfloat16

N_DEV = 8
N_META = 16
Q_BLOCK = 128
NORM_EPS = 1e-6
HEADS = 16
HEAD_DIM = 64
SSD_GROUPS = 2
SSD_STATE = 128
CONV_K = 4
LRU_C = 8.0
ADAM_LR, ADAM_B1, ADAM_B2, ADAM_EPS, ADAM_WD, ADAM_STEP = 0.001, 0.9, 0.999, 1e-08, 0.01, 10

LANES = 128
SUBLANES = 8
VMEM_LIMIT = 56 * 1024 * 1024
NEG = -1e30
MM_TILE = 1408
MM_VMEM = 40 * 1024 * 1024


def _cparams(sem=None):
    return pltpu.CompilerParams(dimension_semantics=sem, vmem_limit_bytes=VMEM_LIMIT)


def _tile(dim, target, mult=LANES):
    if dim <= target:
        return dim
    best = None
    for t in range(mult, target + 1, mult):
        if dim % t == 0:
            best = t
    assert best is not None, (dim, target)
    return best


def _sigmoid(x):
    return 1.0 / (1.0 + jnp.exp(-x))


def _log1p_exp_neg_abs(x):
    e = jnp.exp(-jnp.abs(x))
    u = 1.0 + e
    return jnp.where(u == 1.0, e, jnp.log(u) * (e / jnp.where(u == 1.0, 1.0, u - 1.0)))


def _log_sigmoid(x):
    return jnp.minimum(x, 0.0) - _log1p_exp_neg_abs(x)


def _softplus(x):
    return jnp.maximum(x, 0.0) + _log1p_exp_neg_abs(x)


def _one_minus_exp(y):
    u = jnp.exp(y)
    safe = jnp.where(u == 1.0, 0.5, u)
    return jnp.where(u == 1.0, -y, (1.0 - u) * y / jnp.log(safe))


def _silu(x):
    return x * _sigmoid(x)


def _dsilu(x):
    s = _sigmoid(x)
    return s * (1.0 + x * (1.0 - s))


_GELU_C = math.sqrt(2.0 / math.pi)


def _gelu(x):
    return 0.5 * x * (1.0 + jnp.tanh(_GELU_C * (x + 0.044715 * x * x * x)))


def _dgelu(x):
    t = jnp.tanh(_GELU_C * (x + 0.044715 * x * x * x))
    return 0.5 * (1.0 + t) + 0.5 * x * (1.0 - t * t) * _GELU_C * (1.0 + 3.0 * 0.044715 * x * x)


def _split3_dot(tri, x):
    hi = x.astype(BF16)
    r1 = x - hi.astype(F32)
    mid = r1.astype(BF16)
    lo = (r1 - mid.astype(F32)).astype(BF16)
    t = tri.astype(BF16)
    d = lambda p: jnp.dot(t, p, preferred_element_type=F32)
    return d(hi) + d(mid) + d(lo)


def _lower_tri(n, strict=False):
    r = lax.broadcasted_iota(jnp.int32, (n, n), 0)
    c = lax.broadcasted_iota(jnp.int32, (n, n), 1)
    return (c < r) if strict else (c <= r)


def _mm(a, b, *, ta=False, tb=False, out_dtype=F32, res=None, scale=None, tm=None, tn=None, tk=None,
        a_off=(0, 0), b_off=(0, 0), dims=None, name):
    if dims is None:
        M, K = (a.shape[1], a.shape[0]) if ta else a.shape
        N = b.shape[0] if tb else b.shape[1]
    else:
        M, N, K = dims
    tk = tk or _tile(K, 2816)
    nk_ = K // tk
    pick_m, pick_n = tm is None, tn is None
    tm = tm or _tile(M, MM_TILE)
    tn = tn or _tile(N, MM_TILE)

    def vmem(tm_, tn_):
        a_b = tm_ * tk * a.dtype.itemsize + (tm_ * tk * 2 if a.dtype != BF16 else 0)
        b_b = tn_ * tk * b.dtype.itemsize + (tn_ * tk * 2 if b.dtype != BF16 else 0)
        o_b = tm_ * tn_ * (jnp.dtype(out_dtype).itemsize + (4 if res is not None else 0))
        return 2 * (a_b + b_b + o_b) + (tm_ * tn_ * 4 if nk_ > 1 else 0) + tm_ * tn_ * 4

    while vmem(tm, tn) > MM_VMEM and (pick_m or pick_n):
        if pick_m and (tm >= tn or not pick_n) and tm > LANES:
            tm = _tile(M, tm - LANES)
        elif pick_n and tn > LANES:
            tn = _tile(N, tn - LANES)
        else:
            break
    assert M % tm == 0 and N % tn == 0 and K % tk == 0, (name, M, N, K, tm, tn, tk)
    nk = K // tk
    ca = 0 if ta else 1
    cb = 1 if tb else 0

    def blk(rows, cols, off):
        assert off[0] % rows == 0 and off[1] % cols == 0, (name, off, rows, cols)
        return off[0] // rows, off[1] // cols

    if ta:
        ao = blk(tk, tm, a_off)
        a_spec = pl.BlockSpec((tk, tm), lambda i, j, k: (k + ao[0], i + ao[1]))
    else:
        ao = blk(tm, tk, a_off)
        a_spec = pl.BlockSpec((tm, tk), lambda i, j, k: (i + ao[0], k + ao[1]))
    if tb:
        bo = blk(tn, tk, b_off)
        b_spec = pl.BlockSpec((tn, tk), lambda i, j, k: (j + bo[0], k + bo[1]))
    else:
        bo = blk(tk, tn, b_off)
        b_spec = pl.BlockSpec((tk, tn), lambda i, j, k: (k + bo[0], j + bo[1]))
    o_spec = pl.BlockSpec((tm, tn), lambda i, j, k: (i, j))
    in_specs = [a_spec, b_spec] + ([o_spec] if res is not None else [])
    has_res = res is not None

    def kern(*refs):
        if has_res:
            a_ref, b_ref, r_ref, o_ref = refs[:4]
            scr = refs[4:]
        else:
            a_ref, b_ref, o_ref = refs[:3]
            r_ref = None
            scr = refs[3:]
        p = lax.dot_general(a_ref[...].astype(BF16), b_ref[...].astype(BF16), (((ca,), (cb,)), ((), ())),
                            preferred_element_type=F32)

        def fin(val):
            if scale is not None:
                val = val * scale
            if has_res:
                val = r_ref[...] + val
            o_ref[...] = val.astype(out_dtype)

        if nk == 1:
            fin(p)
        else:
            acc = scr[0]
            k = pl.program_id(2)

            @pl.when(k == 0)
            def _():
                acc[...] = p

            @pl.when(k > 0)
            def _():
                acc[...] += p

            @pl.when(k == nk - 1)
            def _():
                fin(acc[...])

    args = (a, b) + ((res,) if has_res else ())
    return pl.pallas_call(
        kern, name=name, grid=(M // tm, N // tn, nk), in_specs=in_specs, out_specs=o_spec,
        out_shape=jax.ShapeDtypeStruct((M, N), out_dtype),
        scratch_shapes=[pltpu.VMEM((tm, tn), F32)] if nk > 1 else [],
        compiler_params=_cparams(("parallel", "parallel", "arbitrary")),
    )(*args)


def _rows(body, tiled, full, outs, accs, *, tr, name, T):
    assert T % tr == 0
    in_specs = []
    for arr, width, off in tiled:
        assert off % width == 0, (name, off, width)
        in_specs.append(pl.BlockSpec((tr, width), functools.partial(lambda i, o: (i, o), o=off // width)))
    for arr in full:
        in_specs.append(pl.BlockSpec(arr.shape, lambda i: (0, 0)))
    out_specs = [pl.BlockSpec((tr, w), lambda i: (i, 0)) for w, _ in outs]
    out_specs += [pl.BlockSpec(s, lambda i: (0, 0)) for s, _ in accs]
    out_shape = [jax.ShapeDtypeStruct((T, w), d) for w, d in outs] + [jax.ShapeDtypeStruct(s, d) for s, d in accs]
    nt, nf, no = len(tiled), len(full), len(outs)

    def kern(*refs):
        i = pl.program_id(0)
        acc_refs = refs[nt + nf + no:]

        @pl.when(i == 0)
        def _():
            for r in acc_refs:
                r[...] = jnp.zeros(r.shape, r.dtype)

        body(i, refs[:nt], refs[nt:nt + nf], refs[nt + nf:nt + nf + no], acc_refs)

    res = pl.pallas_call(
        kern, name=name, grid=(T // tr,), in_specs=in_specs, out_specs=out_specs, out_shape=out_shape,
        compiler_params=_cparams(("arbitrary",)),
    )(*[t[0] for t in tiled], *full)
    return res


def _colsum(x):
    return jnp.sum(x, axis=0, keepdims=True)


def _norm_fwd(h, g, name):
    T, D = h.shape

    def body(i, t, f, o, a):
        x = t[0][...]
        r = lax.rsqrt(jnp.mean(x * x, axis=-1, keepdims=True) + NORM_EPS)
        o[0][...] = (x * r * f[0][...]).astype(BF16)

    return _rows(body, [(h, D, 0)], [g], [(D, BF16)], [], tr=_tile(T, 768, 8), name=name, T=T)[0]


def _norm_bwd(h, dn, dh, g, name):
    T, D = h.shape

    def body(i, t, f, o, a):
        x, dnv, dhv = t[0][...], t[1][...], t[2][...]
        r = lax.rsqrt(jnp.mean(x * x, axis=-1, keepdims=True) + NORM_EPS)
        xh = x * r
        dng = dnv * f[0][...]
        out = dhv + r * (dng - xh * jnp.mean(dng * xh, axis=-1, keepdims=True))
        o[0][...] = out
        o[1][...] = out.astype(BF16)
        a[0][...] += _colsum(dnv * xh)

    return _rows(body, [(h, D, 0), (dn, D, 0), (dh, D, 0)], [g], [(D, F32), (D, BF16)], [((1, D), F32)],
                 tr=_tile(T, 384, 16), name=name, T=T)


FFN_TM = 768


def _ffn_up_act(n, wgu_t, name):
    T, D = n.shape
    F = wgu_t.shape[0] // 2
    tm, tn = _tile(T, FFN_TM), _tile(F, MM_TILE)
    nj = F // tn

    def kern(n_ref, wg_ref, wu_ref, g_ref, u_ref, a_ref):
        nv = n_ref[...]
        g = _dot_nt(nv, wg_ref[...])
        u = _dot_nt(nv, wu_ref[...])
        g_ref[...] = g
        u_ref[...] = u
        a_ref[...] = (_silu(g) * u).astype(BF16)

    out = pl.BlockSpec((tm, tn), lambda i, j: (i, j))
    return pl.pallas_call(
        kern, name=name, grid=(T // tm, nj),
        in_specs=[pl.BlockSpec((tm, D), lambda i, j: (i, 0)), pl.BlockSpec((tn, D), lambda i, j: (j, 0)),
                  pl.BlockSpec((tn, D), lambda i, j: (nj + j, 0))],
        out_specs=[out] * 3,
        out_shape=[jax.ShapeDtypeStruct((T, F), F32), jax.ShapeDtypeStruct((T, F), F32), jax.ShapeDtypeStruct((T, F), BF16)],
        compiler_params=_cparams(("parallel", "parallel")),
    )(n, wgu_t, wgu_t)


def _ffn_down_dx_act(dhb, wd, g, u, name):
    T, D = dhb.shape
    F = wd.shape[0]
    tm, tn = _tile(T, FFN_TM), _tile(F, MM_TILE)
    nj = F // tn

    def kern(d_ref, w_ref, g_ref, u_ref, o_ref):
        da = _dot_nt(d_ref[...], w_ref[...]) * 0.5
        gv, uv = g_ref[...], u_ref[...]
        dg = (da * uv * _dsilu(gv)).astype(BF16)
        du = (da * _silu(gv)).astype(BF16)
        for jj in range(nj):

            @pl.when(pl.program_id(1) == jj)
            def _():
                o_ref[:, jj * tn:(jj + 1) * tn] = dg
                o_ref[:, F + jj * tn:F + (jj + 1) * tn] = du

    tile = pl.BlockSpec((tm, tn), lambda i, j: (i, j))
    return pl.pallas_call(
        kern, name=name, grid=(T // tm, nj),
        in_specs=[pl.BlockSpec((tm, D), lambda i, j: (i, 0)), pl.BlockSpec((tn, D), lambda i, j: (j, 0)), tile, tile],
        out_specs=pl.BlockSpec((tm, 2 * F), lambda i, j: (i, 0)), out_shape=jax.ShapeDtypeStruct((T, 2 * F), BF16),
        compiler_params=_cparams(("parallel", "arbitrary")),
    )(dhb, wd, g, u)


def _merge_fwd(proj, off, pa, pb, pc, name):
    T, D = pa.shape

    def body(i, t, f, o, a):
        o[0][...] = (_sigmoid(t[0][...]) * t[3][...] + _sigmoid(t[1][...]) * t[4][...]
                     + _sigmoid(t[2][...]) * t[5][...]).astype(BF16)

    tiled = [(proj, D, off), (proj, D, off + D), (proj, D, off + 2 * D), (pa, D, 0), (pb, D, 0), (pc, D, 0)]
    return _rows(body, tiled, [], [(D, BF16)], [], tr=_tile(T, 384, 8), name=name, T=T)[0]


def _merge_bwd(dmixed, proj, off, pa, pb, pc, name):
    T, D = pa.shape

    def body(i, t, f, o, a):
        dm = t[0][...]
        for k in range(3):
            g = _sigmoid(t[1 + k][...])
            o[k][...] = (dm * g).astype(BF16)
            o[3][:, k * D:(k + 1) * D] = (dm * t[4 + k][...] * g * (1.0 - g)).astype(BF16)

    tiled = [(dmixed, D, 0), (proj, D, off), (proj, D, off + D), (proj, D, off + 2 * D), (pa, D, 0), (pb, D, 0),
             (pc, D, 0)]
    return _rows(body, tiled, [], [(D, BF16)] * 3 + [(3 * D, BF16)], [], tr=_tile(T, 384, 8), name=name, T=T)


def _gnorm_fwd(y, proj, zoff, nw, name):
    T, D = y.shape
    gs = D // SSD_GROUPS

    def body(i, t, f, o, a):
        s = t[0][...] * _silu(t[1][...])
        for g in range(SSD_GROUPS):
            sg = s[:, g * gs:(g + 1) * gs]
            r = lax.rsqrt(jnp.mean(sg * sg, axis=-1, keepdims=True) + NORM_EPS)
            o[0][:, g * gs:(g + 1) * gs] = (sg * r * f[0][:, g * gs:(g + 1) * gs]).astype(BF16)

    return _rows(body, [(y, D, 0), (proj, D, zoff)], [nw], [(D, BF16)], [], tr=_tile(T, 384, 8), name=name, T=T)[0]


def _gnorm_bwd(dout, y, proj, zoff, nw, name):
    T, D = y.shape
    gs = D // SSD_GROUPS

    def body(i, t, f, o, a):
        dov, yv, zv = t[0][...], t[1][...], t[2][...]
        sz = _silu(zv)
        s = yv * sz
        dsz = _dsilu(zv)
        for g in range(SSD_GROUPS):
            sl = slice(g * gs, (g + 1) * gs)
            sg = s[:, sl]
            r = lax.rsqrt(jnp.mean(sg * sg, axis=-1, keepdims=True) + NORM_EPS)
            sh = sg * r
            dog = dov[:, sl]
            dng = dog * f[0][:, sl]
            ds = r * (dng - sh * jnp.mean(dng * sh, axis=-1, keepdims=True))
            o[0][:, sl] = ds * sz[:, sl]
            o[1][:, sl] = (ds * yv[:, sl] * dsz[:, sl]).astype(BF16)
            a[0][:, sl] += _colsum(dog * sh)

    return _rows(body, [(dout, D, 0), (y, D, 0), (proj, D, zoff)], [nw], [(D, F32), (D, BF16)], [((1, D), F32)],
                 tr=_tile(T, 384, 8), name=name, T=T)


def _loss_bwd(h, tgt, g, seq_len, n_real, name):
    T, D = h.shape
    tr = _tile(seq_len, 384, 8)
    per_seq = seq_len // tr

    def body(i, t, f, o, a):
        x, tg = t[0][...], t[1][...]
        pos = (i % per_seq) * tr + lax.broadcasted_iota(jnp.int32, (tr, 1), 0)
        valid = (pos >= N_META) & (pos < N_META + n_real)
        r = lax.rsqrt(jnp.mean(x * x, axis=-1, keepdims=True) + NORM_EPS)
        xh = x * r
        e = jnp.where(valid, xh * f[0][...] - tg, 0.0)
        a[0][...] += jnp.zeros((1, LANES), F32) + 0.5 * jnp.sum(jnp.sum(e * e, axis=-1, keepdims=True) / D,
                                                              axis=0, keepdims=True)
        dy = e / D
        dng = dy * f[0][...]
        out = r * (dng - xh * jnp.mean(dng * xh, axis=-1, keepdims=True))
        o[0][...] = out
        o[1][...] = out.astype(BF16)
        a[1][...] += _colsum(dy * xh)

    return _rows(body, [(h, D, 0), (tgt, D, 0)], [g], [(D, F32), (D, BF16)], [((1, LANES), F32), ((1, D), F32)], tr=tr,
                 name=name, T=T)


def _lane_is_attn(shape):
    return lax.broadcasted_iota(jnp.int32, shape, len(shape) - 1) < HEADS


def _gate_prep(proj3, col_blk, bias, avec, name):
    B, L, _ = proj3.shape
    Q = Q_BLOCK
    nc = L // Q

    def kern(x_ref, b_ref, a_ref, v_ref, c_ref, carry):
        c = pl.program_id(1)

        @pl.when(c == 0)
        def _():
            carry[...] = jnp.zeros_like(carry)

        x = x_ref[0] + b_ref[...]
        attn = _lane_is_attn(x.shape)
        v = jnp.where(attn, _log_sigmoid(x), _softplus(x))
        w = jnp.where(attn, v, v * a_ref[...])
        cs = _split3_dot(_lower_tri(Q), w) + jnp.where(attn[:1], carry[...], 0.0)
        v_ref[0] = v
        c_ref[0] = cs
        rows = lax.broadcasted_iota(jnp.int32, (Q, 1), 0)
        carry[...] = jnp.sum(jnp.where(rows == Q - 1, cs, 0.0), axis=0, keepdims=True)

    blk = pl.BlockSpec((1, Q, LANES), lambda b, c: (b, c, 0))
    vec = pl.BlockSpec((1, LANES), lambda b, c: (0, 0))
    return pl.pallas_call(
        kern, name=name, grid=(B, nc),
        in_specs=[pl.BlockSpec((1, Q, LANES), lambda b, c: (b, c, col_blk)), vec, vec],
        out_specs=[blk, blk], out_shape=[jax.ShapeDtypeStruct((B, L, LANES), F32)] * 2,
        scratch_shapes=[pltpu.VMEM((1, LANES), F32)],
        compiler_params=_cparams(("parallel", "arbitrary")),
    )(proj3, bias, avec)


def _gate_post(drow, dcol, ddt, proj3, col_blk, vals, bias, avec, name):
    B, L, _ = proj3.shape
    Q = Q_BLOCK
    nc = L // Q

    def kern(dr_ref, dc_ref, dd_ref, x_ref, v_ref, b_ref, a_ref, o_ref, db_ref, da_ref, carry):
        b = pl.program_id(0)
        c = pl.program_id(1)

        @pl.when((b == 0) & (c == 0))
        def _():
            db_ref[...] = jnp.zeros_like(db_ref)
            da_ref[...] = jnp.zeros_like(da_ref)

        @pl.when(c == 0)
        def _():
            carry[...] = jnp.zeros_like(carry)

        x = x_ref[0] + b_ref[...]
        attn = _lane_is_attn(x.shape)
        dcs = dr_ref[0] + dc_ref[0]
        upper = jnp.logical_not(_lower_tri(Q, strict=True))
        rc = _split3_dot(upper, dcs) + jnp.where(attn[:1], carry[...], 0.0)
        rows = lax.broadcasted_iota(jnp.int32, (Q, 1), 0)
        carry[...] = jnp.sum(jnp.where(rows == 0, rc, 0.0), axis=0, keepdims=True)
        dv = jnp.where(attn, rc, dd_ref[0] + rc * a_ref[...])
        dpre = dv * jnp.where(attn, _sigmoid(-x), _sigmoid(x))
        o_ref[0] = dpre.astype(BF16)
        db_ref[...] += _colsum(dpre)
        da_ref[...] += _colsum(jnp.where(attn, 0.0, rc * v_ref[0]))

    rev = pl.BlockSpec((1, Q, LANES), lambda b, c: (b, nc - 1 - c, 0))
    vec = pl.BlockSpec((1, LANES), lambda b, c: (0, 0))
    return pl.pallas_call(
        kern, name=name, grid=(B, nc),
        in_specs=[rev, rev, rev, pl.BlockSpec((1, Q, LANES), lambda b, c: (b, nc - 1 - c, col_blk)), rev, vec, vec],
        out_specs=[rev, vec, vec],
        out_shape=[jax.ShapeDtypeStruct((B, L, LANES), BF16), jax.ShapeDtypeStruct((1, LANES), F32),
                   jax.ShapeDtypeStruct((1, LANES), F32)],
        scratch_shapes=[pltpu.VMEM((1, LANES), F32)],
        compiler_params=_cparams(("arbitrary", "arbitrary")),
    )(drow, dcol, ddt, proj3, vals, bias, avec)


def _lane_col(tile, lane):
    sel = lax.broadcasted_iota(jnp.int32, tile.shape, 1) == lane
    return jnp.sum(jnp.where(sel, tile, 0.0), axis=1, keepdims=True)


AUG = LANES
AUG_A = HEAD_DIM
AUG_B = HEAD_DIM + 3


def _split3(x):
    hi = x.astype(BF16).astype(F32)
    mid = (x - hi).astype(BF16).astype(F32)
    lo = (x - hi - mid).astype(BF16).astype(F32)
    return hi, mid, lo


def _put3(base, lane, first, x):
    hi, mid, lo = _split3(x)
    return jnp.where(lane == first, hi, jnp.where(lane == first + 1, mid, jnp.where(lane == first + 2, lo, base)))


HP = 2
AH = 2
AW = AH * HEAD_DIM


def _other_half(x):
    return pltpu.roll(x, HEAD_DIM, 1)


def _loop_by_twos(n, step, init):
    carry = lax.fori_loop(0, n // 4, lambda t, c: step(4 * t + 3, step(4 * t + 2, step(4 * t + 1, step(4 * t, c)))), init)
    carry = lax.cond(n % 4 >= 2, lambda c: step(n // 4 * 4 + 1, step(n // 4 * 4, c)), lambda c: c, carry)
    return lax.cond(n % 2 == 1, lambda c: step(n - 1, c), lambda c: c, carry)


def _attn_pack(proj3, cums, name):
    B, L, _ = proj3.shape
    D = HEADS * HEAD_DIM
    nh = HEADS // HP
    tr = _tile(L, 384)
    scale = HEAD_DIM ** -0.5

    def kern(q_ref, k_ref, v_ref, c_ref, qa_ref, ka_ref, va_ref):
        lane = lax.broadcasted_iota(jnp.int32, (tr, AUG), 1)
        head = lane < HEAD_DIM
        ones_a = jnp.where((lane >= AUG_A) & (lane < AUG_A + 3), 1.0, 0.0)
        ones_b = jnp.where((lane >= AUG_B) & (lane < AUG_B + 3), 1.0, 0.0)
        ct = c_ref[0]
        for hp in range(nh):
            cols = slice(hp * LANES, (hp + 1) * LANES)
            for hh in range(HP):
                h = HP * hp + hh
                c = _lane_col(ct, h)
                sel = (lambda t: t) if hh == 0 else _other_half
                qa_ref[0, h] = jnp.where(head, sel(q_ref[0, :, cols]) * scale, _put3(ones_b, lane, AUG_A, c)).astype(BF16)
                ka_ref[0, h] = jnp.where(head, sel(k_ref[0, :, cols]), _put3(ones_a, lane, AUG_B, -c)).astype(BF16)
                va_ref[0, h] = jnp.where(head, sel(v_ref[0, :, cols]), ones_a).astype(BF16)

    def win(k):
        return pl.BlockSpec((1, tr, D), lambda b, i: (b, i, k))

    out = pl.BlockSpec((1, HEADS, tr, AUG), lambda b, i: (b, 0, i, 0))
    return pl.pallas_call(
        kern, name=name, grid=(B, L // tr),
        in_specs=[win(0), win(1), win(2), pl.BlockSpec((1, tr, LANES), lambda b, i: (b, i, 0))],
        out_specs=[out] * 3, out_shape=[jax.ShapeDtypeStruct((B, HEADS, L, AUG), BF16)] * 3,
        compiler_params=_cparams(("parallel", "parallel")),
    )(proj3, proj3, proj3, cums)


def _attn_fwd(qa, ka, va, name, gather=None):
    B, H, L, _ = qa.shape
    tq = _tile(L, 384)
    nq = L // tq
    nh = H // AH
    comm = gather is not None

    def kern(*refs):
        if comm:
            q_ref, k_ref, v_ref, x_ref, y_ref, yb_ref, l_ref, g_ref, send_sems, recv_sems, local_sem = refs
        else:
            q_ref, k_ref, v_ref, y_ref, yb_ref, l_ref = refs
        qi = pl.program_id(2)
        if comm:
            _ride((pl.program_id(0) * nh + pl.program_id(1)) * nq + qi, B * nh * nq,
                  _gather_phases(x_ref, g_ref, send_sems, recv_sems, local_sem))
        qs = [q_ref[0, hh] for hh in range(AH)]
        causal = _lower_tri(tq)

        def step(j, carry, masked):
            rows = pl.ds(pl.multiple_of(j * tq, tq), tq)
            out = []
            for hh in range(AH):
                m, acc = carry[hh]
                s = _dot_nt(qs[hh], k_ref[0, hh, rows, :])
                if masked:
                    s = jnp.where(causal, s, NEG)
                m_new = jnp.maximum(m, jnp.max(s, axis=1, keepdims=True))
                p = jnp.exp(s - m_new)
                out.append((m_new, jnp.exp(m - m_new) * acc + _dot(p.astype(BF16), v_ref[0, hh, rows, :])))
            return tuple(out)

        init = tuple((jnp.full((tq, 1), NEG, F32), jnp.zeros((tq, AUG), F32)) for _ in range(AH))
        carry = _loop_by_twos(qi, lambda j, c: step(j, c, False), init)
        outs = []
        for hh, (m, acc) in enumerate(step(qi, carry, True)):
            l = _lane_col(acc, AUG_A)
            outs.append(acc / l)
            l_ref[0, hh] = m + jnp.log(l)
        head = lax.broadcasted_iota(jnp.int32, (tq, AUG), 1) < HEAD_DIM
        for pp in range(AH // HP):
            y = jnp.where(head, outs[HP * pp], _other_half(outs[HP * pp + 1]))
            y_ref[0, :, pp * LANES:(pp + 1) * LANES] = y
            yb_ref[0, :, pp * LANES:(pp + 1) * LANES] = y.astype(BF16)

    qspec = pl.BlockSpec((1, AH, tq, AUG), lambda b, h, i: (b, h, i, 0))
    kvspec = pl.BlockSpec((1, AH, L, AUG), lambda b, h, i: (b, h, 0, 0))
    lspec = pl.BlockSpec((1, AH, tq, 1), lambda b, h, i: (b, h, i, 0))
    yspec = pl.BlockSpec((1, tq, AW), lambda b, h, i: (b, i, h))
    out_shape = [jax.ShapeDtypeStruct((B, L, H * HEAD_DIM), F32), jax.ShapeDtypeStruct((B, L, H * HEAD_DIM), BF16),
                 jax.ShapeDtypeStruct((B, H, L, 1), F32)]
    if comm:
        out_shape.append(jax.ShapeDtypeStruct((N_DEV,) + gather.shape, gather.dtype))
    return pl.pallas_call(
        kern, name=name, grid=(B, nh, nq), in_specs=[qspec, kvspec, kvspec] + ([ANY] if comm else []),
        out_specs=[yspec, yspec, lspec] + ([ANY] if comm else []), out_shape=out_shape,
        scratch_shapes=COMM_SCRATCH if comm else [],
        compiler_params=_cparams(("arbitrary",) * 3 if comm else ("parallel", "parallel", "arbitrary")),
    )(qa, ka, va, *([gather] if comm else []))


def _attn_bwd(qa, ka, va, y, dy, lse, name, parts=None):
    B, H, L, _ = qa.shape
    tq = _tile(L, 384)
    nq = L // tq
    nh = H // AH
    comm = parts is not None
    scale = HEAD_DIM ** -0.5

    def kern(*refs):
        if comm:
            (q_ref, k_ref, v_ref, y_ref, dy_ref, l_ref, p_ref, dq_ref, dk_ref, dv_ref, dc_ref, r_ref,
             dk_acc, dv_acc, send_sems, recv_sems, local_sem) = refs
        else:
            q_ref, k_ref, v_ref, y_ref, dy_ref, l_ref, dq_ref, dk_ref, dv_ref, dc_ref, dk_acc, dv_acc = refs
        qi = pl.program_id(2)
        hp = pl.program_id(1)
        lane_row = lax.broadcasted_iota(jnp.int32, (1, LANES), 1)
        onehot = [(lane_row == AH * hp + hh).astype(F32) for hh in range(AH)]

        @pl.when((hp == 0) & (qi == 0))
        def _():
            dc_ref[...] = jnp.zeros_like(dc_ref)

        if comm:
            _ride((pl.program_id(0) * nh + pl.program_id(1)) * nq + qi, B * nh * nq,
                  _exchange_phases(p_ref, r_ref, send_sems, recv_sems, local_sem))

        @pl.when(qi == 0)
        def _():
            dk_acc[...] = jnp.zeros_like(dk_acc)
            dv_acc[...] = jnp.zeros_like(dv_acc)

        lane = lax.broadcasted_iota(jnp.int32, (tq, AUG), 1)
        head = lane < HEAD_DIM
        qbs, dobs = [], []
        for hh in range(AH):
            sel = (lambda t: t) if hh % HP == 0 else _other_half
            cols = slice((hh // HP) * LANES, (hh // HP + 1) * LANES)
            qf = q_ref[0, hh].astype(F32)
            dov = jnp.where(head, sel(dy_ref[0, :, cols]), 0.0)
            dsum = jnp.sum(dov * sel(y_ref[0, :, cols]), axis=1, keepdims=True)
            dobs.append(_put3(dov, lane, AUG_A, -dsum).astype(BF16))
            c_t = jnp.sum(jnp.where((lane >= AUG_A) & (lane < AUG_A + 3), qf, 0.0), axis=1, keepdims=True)
            qbs.append(_put3(qf, lane, AUG_A, c_t - l_ref[0, hh]).astype(BF16))
        causal = _lower_tri(tq)

        def step(j, dqs, masked):
            rows = pl.ds(pl.multiple_of(j * tq, tq), tq)
            out = []
            for hh in range(AH):
                kj = k_ref[0, hh, rows, :]
                s = _dot_nt(qbs[hh], kj)
                if masked:
                    s = jnp.where(causal, s, NEG)
                p = jnp.exp(s)
                ds = (p * _dot_nt(dobs[hh], v_ref[0, hh, rows, :])).astype(BF16)
                dv_acc[hh, rows, :] += _dot_tn(p.astype(BF16), dobs[hh])
                dk_acc[hh, rows, :] += _dot_tn(ds, qbs[hh])
                out.append(dqs[hh] + _dot(ds, kj))
            return tuple(out)

        dqs = _loop_by_twos(qi, lambda j, c: step(j, c, False), tuple(jnp.zeros((tq, AUG), F32) for _ in range(AH)))
        dqs = step(qi, dqs, True)
        dc_ref[0, pl.ds(pl.multiple_of(qi * tq, tq), tq), :] += sum(_lane_col(dqs[hh], AUG_A) * onehot[hh]
                                                                    for hh in range(AH))
        for pp in range(AH // HP):
            dq_ref[0, :, pp * LANES:(pp + 1) * LANES] = (
                jnp.where(head, dqs[HP * pp], _other_half(dqs[HP * pp + 1])) * scale).astype(BF16)

        @pl.when(qi == nq - 1)
        def _():
            full = lax.broadcasted_iota(jnp.int32, (L, AUG), 1) < HEAD_DIM
            for pp in range(AH // HP):
                cols = slice(pp * LANES, (pp + 1) * LANES)
                dk_ref[0, :, cols] = jnp.where(full, dk_acc[HP * pp], _other_half(dk_acc[HP * pp + 1])).astype(BF16)
                dv_ref[0, :, cols] = jnp.where(full, dv_acc[HP * pp], _other_half(dv_acc[HP * pp + 1])).astype(BF16)
            dc_ref[0] -= sum(_lane_col(dk_acc[hh], AUG_B) * onehot[hh] for hh in range(AH))

    qspec = pl.BlockSpec((1, AH, tq, AUG), lambda b, h, i: (b, h, i, 0))
    kvspec = pl.BlockSpec((1, AH, L, AUG), lambda b, h, i: (b, h, 0, 0))
    lspec = pl.BlockSpec((1, AH, tq, 1), lambda b, h, i: (b, h, i, 0))
    tmspec = pl.BlockSpec((1, L, LANES), lambda b, h, i: (b, 0, 0))
    yspec = pl.BlockSpec((1, tq, AW), lambda b, h, i: (b, i, h))
    yfull = pl.BlockSpec((1, L, AW), lambda b, h, i: (b, 0, h))
    nat = jax.ShapeDtypeStruct((B, L, H * HEAD_DIM), BF16)
    out_shape = [nat, nat, nat, jax.ShapeDtypeStruct((B, L, LANES), F32)]
    if comm:
        out_shape.append(jax.ShapeDtypeStruct(parts.shape, parts.dtype))
    return pl.pallas_call(
        kern, name=name, grid=(B, nh, nq),
        in_specs=[qspec, kvspec, kvspec, yspec, yspec, lspec] + ([ANY] if comm else []),
        out_specs=[yspec, yfull, yfull, tmspec] + ([ANY] if comm else []), out_shape=out_shape,
        scratch_shapes=[pltpu.VMEM((AH, L, AUG), F32), pltpu.VMEM((AH, L, AUG), F32)] + (COMM_SCRATCH if comm else []),
        compiler_params=_cparams(("parallel", "arbitrary", "arbitrary")),
    )(qa, ka, va, y, dy, lse, *([parts] if comm else []))


PAD = SUBLANES


def _halo_tile(x_ref, i, TR):
    r0 = pl.multiple_of(i * TR, TR)
    before = x_ref[0, pl.ds(pl.multiple_of(jnp.maximum(r0 - PAD, 0), PAD), PAD), :]
    return jnp.concatenate([jnp.where(i > 0, before, 0.0), x_ref[0, pl.ds(r0, TR), :]], axis=0)


def _conv_fwd(x3, x_blk, w, b, n_silu, name):
    B, L, _ = x3.shape
    C = w.shape[1]
    TR = _tile(L, 384, 8)

    def kern(x_ref, w_ref, b_ref, o_ref):
        cb = pl.program_id(1)

        def body(i, carry):
            r0 = pl.multiple_of(i * TR, TR)
            ext = _halo_tile(x_ref, i, TR)
            acc = jnp.zeros((TR, LANES), F32) + b_ref[...]
            for k in range(CONV_K):
                s = CONV_K - 1 - k
                sh = ext if s == 0 else pltpu.roll(ext, s, 0)
                acc = acc + w_ref[k:k + 1, :] * sh[PAD:PAD + TR]
            o_ref[0, pl.ds(r0, TR), :] = jnp.where(cb < n_silu, _silu(acc), acc)
            return carry

        lax.fori_loop(0, L // TR, body, 0)

    return pl.pallas_call(
        kern, name=name, grid=(B, C // LANES),
        in_specs=[pl.BlockSpec((1, L, LANES), lambda b_, c: (b_, 0, x_blk + c)),
                  pl.BlockSpec((CONV_K, LANES), lambda b_, c: (0, c)), pl.BlockSpec((1, LANES), lambda b_, c: (0, c))],
        out_specs=pl.BlockSpec((1, L, LANES), lambda b_, c: (b_, 0, c)),
        out_shape=jax.ShapeDtypeStruct((B, L, C), F32),
        compiler_params=_cparams(("parallel", "parallel")),
    )(x3, w, b)


def _conv_bwd(x3, x_blk, du, w, b, n_silu, name):
    B, L, C = du.shape
    TR = _tile(L, 384, 16)

    def kern(x_ref, du_ref, w_ref, b_ref, dx_ref, dw_ref, dp_s):
        cb = pl.program_id(0)

        @pl.when(pl.program_id(1) == 0)
        def _():
            dw_ref[...] = jnp.zeros_like(dw_ref)

        def pre(i, carry):
            r0 = pl.multiple_of(i * TR, TR)
            ext = _halo_tile(x_ref, i, TR)
            taps = []
            acc = jnp.zeros((TR, LANES), F32) + b_ref[...]
            for k in range(CONV_K):
                s = CONV_K - 1 - k
                sh = ext if s == 0 else pltpu.roll(ext, s, 0)
                taps.append(sh[PAD:PAD + TR])
                acc = acc + w_ref[k:k + 1, :] * taps[-1]
            dv = du_ref[0, pl.ds(r0, TR), :]
            dpre = jnp.where(cb < n_silu, dv * _dsilu(acc), dv)
            dp_s[pl.ds(r0, TR), :] = dpre
            return tuple(c + _colsum(dpre * t) for c, t in zip(carry[:CONV_K], taps)) + (carry[CONV_K] + _colsum(dpre),)

        z = jnp.zeros((1, LANES), F32)
        sums = lax.fori_loop(0, L // TR, pre, (z,) * (CONV_K + 1))
        dp_s[pl.ds(L, PAD), :] = jnp.zeros((PAD, LANES), F32)
        for k in range(CONV_K + 1):
            dw_ref[k:k + 1, :] += sums[k]

        def back(i, carry):
            r0 = pl.multiple_of(i * TR, TR)
            ext = dp_s[pl.ds(r0, TR + PAD), :]
            acc = jnp.zeros((TR, LANES), F32)
            for k in range(CONV_K):
                s = CONV_K - 1 - k
                sh = ext if s == 0 else pltpu.roll(ext, TR + PAD - s, 0)
                acc = acc + w_ref[k:k + 1, :] * sh[0:TR]
            dx_ref[0, pl.ds(r0, TR), :] = acc.astype(BF16)
            return carry

        lax.fori_loop(0, L // TR, back, 0)

    return pl.pallas_call(
        kern, name=name, grid=(C // LANES, B),
        in_specs=[pl.BlockSpec((1, L, LANES), lambda c, b_: (b_, 0, x_blk + c)),
                  pl.BlockSpec((1, L, LANES), lambda c, b_: (b_, 0, c)),
                  pl.BlockSpec((CONV_K, LANES), lambda c, b_: (0, c)), pl.BlockSpec((1, LANES), lambda c, b_: (0, c))],
        out_specs=[pl.BlockSpec((1, L, LANES), lambda c, b_: (b_, 0, c)),
                   pl.BlockSpec((SUBLANES, LANES), lambda c, b_: (0, c))],
        out_shape=[jax.ShapeDtypeStruct((B, L, C), BF16), jax.ShapeDtypeStruct((SUBLANES, C), F32)],
        scratch_shapes=[pltpu.VMEM((L + PAD, LANES), F32)],
        compiler_params=_cparams(("parallel", "arbitrary")),
    )(x3, du, w, b)


def _dot_nt(a, b):
    return lax.dot_general(a, b, (((1,), (1,)), ((), ())), preferred_element_type=F32)


def _dot_tn(a, b):
    return lax.dot_general(a, b, (((0,), (0,)), ((), ())), preferred_element_type=F32)


def _dot(a, b):
    return jnp.dot(a, b, preferred_element_type=F32)


def _ssd_specs(L, nc, b_blk, c_blk):
    pairs_per_group = HEADS // SSD_GROUPS // HP
    return [
        pl.BlockSpec((1, L, LANES), lambda b, h: (b, 0, h)),
        pl.BlockSpec((1, L, SSD_STATE), lambda b, h: (b, 0, b_blk + h // pairs_per_group)),
        pl.BlockSpec((1, L, SSD_STATE), lambda b, h: (b, 0, c_blk + h // pairs_per_group)),
        pl.BlockSpec((1, L, LANES), lambda b, h: (b, 0, 0)),
        pl.BlockSpec((1, L, LANES), lambda b, h: (b, 0, 0)),
        pl.BlockSpec((1, HP, nc, Q_BLOCK), lambda b, h: (b, HEADS // HP + h, 0, 0)),
        pl.BlockSpec((1, LANES), lambda b, h: (0, 0)),
    ]


def _halves(a, b, shape):
    return jnp.where(lax.broadcasted_iota(jnp.int32, shape, 1) < HEAD_DIM, a, b)


def _half_sums(t):
    first = lax.broadcasted_iota(jnp.int32, t.shape, 1) < HEAD_DIM
    lo = jnp.sum(jnp.where(first, t, 0.0), axis=1, keepdims=True)
    return lo, jnp.sum(t, axis=1, keepdims=True) - lo


def _ssd_chunk(c, S, x_ref, b_ref, c_ref, v_ref, cu_ref, ct_ref, lane0):
    Q = Q_BLOCK
    rows = pl.ds(pl.multiple_of(c * Q, Q), Q)
    x = x_ref[0, rows, :]
    Bb = b_ref[0, rows, :].astype(BF16)
    Cb = c_ref[0, rows, :].astype(BF16)
    vt, ct = v_ref[0, rows, :], cu_ref[0, rows, :]
    tri = _lower_tri(Q)
    A, Lm, e_end_h, eAend_h, dts = [], [], [], [], []
    for hh in range(HP):
        dts.append(_lane_col(vt, lane0 + hh))
        A.append(_lane_col(ct, lane0 + hh))
        Ar = ct_ref[0, hh, pl.ds(c, 1), :]
        Aend = _lane_col(Ar, Q - 1)
        Lm.append(jnp.exp(jnp.where(tri, A[hh] - Ar, NEG)))
        e_end_h.append(jnp.exp(Aend - A[hh]))
        eAend_h.append(jnp.exp(Aend))
    shape = (Q, LANES)
    dt = _halves(dts[0], dts[1], shape)
    eA = _halves(jnp.exp(A[0]), jnp.exp(A[1]), shape)
    e_end = _halves(e_end_h[0], e_end_h[1], shape)
    xdt = x * dt
    CB = _dot_nt(Cb, Bb)
    W = xdt * e_end
    srow = lax.broadcasted_iota(jnp.int32, (HP * HEAD_DIM, 1), 0) < HEAD_DIM
    eAend = jnp.where(srow, eAend_h[0], eAend_h[1])
    S_new = S * eAend + _dot_tn(W.astype(BF16), Bb)
    return dict(rows=rows, x=x, Bb=Bb, Cb=Cb, dt=dt, eA=eA, e_end=e_end, e_end_h=e_end_h, eAend=eAend,
                eAend_h=eAend_h, xdt=xdt, Lm=Lm, CB=CB, W=W, S_new=S_new)


def _ssd_fwd(u, b_blk, c_blk, vals, cums, cums_t, dvec, name):
    B, L, _ = u.shape
    nc = L // Q_BLOCK
    nh = HEADS // HP
    PP = HP * HEAD_DIM

    def kern(x_ref, b_ref, c_ref, v_ref, cu_ref, ct_ref, d_ref, y_ref, st_ref):
        lane0 = HEADS + HP * pl.program_id(1)
        dskip = _halves(_lane_col(d_ref[...], lane0), _lane_col(d_ref[...], lane0 + 1), (1, LANES))
        first = lax.broadcasted_iota(jnp.int32, (Q_BLOCK, LANES), 1) < HEAD_DIM

        def body(c, S):
            st_ref[0, 0, c] = S
            q = _ssd_chunk(c, S, x_ref, b_ref, c_ref, v_ref, cu_ref, ct_ref, lane0)
            xb = q["xdt"].astype(BF16)
            yd = jnp.where(first, _dot((q["CB"] * q["Lm"][0]).astype(BF16), xb),
                           _dot((q["CB"] * q["Lm"][1]).astype(BF16), xb))
            z = _dot_nt(q["Cb"], S.astype(BF16))
            y_ref[0, q["rows"], :] = yd + z * q["eA"] + dskip * q["x"]
            return q["S_new"]

        lax.fori_loop(0, nc, body, jnp.zeros((HP * HEAD_DIM, SSD_STATE), F32))

    return pl.pallas_call(
        kern, name=name, grid=(B, nh), in_specs=_ssd_specs(L, nc, b_blk, c_blk),
        out_specs=[pl.BlockSpec((1, L, LANES), lambda b, h: (b, 0, h)),
                   pl.BlockSpec((1, 1, nc, PP, SSD_STATE), lambda b, h: (b, h, 0, 0, 0))],
        out_shape=[jax.ShapeDtypeStruct((B, L, HEADS * HEAD_DIM), F32),
                   jax.ShapeDtypeStruct((B, nh, nc, PP, SSD_STATE), F32)],
        compiler_params=_cparams(("parallel", "arbitrary")),
    )(u, u, u, vals, cums, cums_t, dvec)


def _ssd_bwd(u, b_blk, c_blk, vals, cums, cums_t, dvec, dy, states, name):
    B, L, _ = u.shape
    Q = Q_BLOCK
    nc = L // Q
    N = SSD_STATE
    nh = HEADS // HP
    pairs_per_group = HEADS // SSD_GROUPS // HP
    PP = HP * HEAD_DIM

    def kern(x_ref, b_ref, c_ref, v_ref, cu_ref, ct_ref, d_ref, dy_ref, st_ref,
             dx_ref, dB_ref, dC_ref, ddt_ref, dAc_ref, dAr_ref, dD_ref):
        b = pl.program_id(0)
        h = pl.program_id(1)
        lane0 = HEADS + HP * h
        dskip = _halves(_lane_col(d_ref[...], lane0), _lane_col(d_ref[...], lane0 + 1), (1, LANES))
        lane_row = lax.broadcasted_iota(jnp.int32, (1, LANES), 1)
        onehot = [(lane_row == lane0 + hh).astype(F32) for hh in range(HP)]

        @pl.when(h % pairs_per_group == 0)
        def _():
            dB_ref[...] = jnp.zeros_like(dB_ref)
            dC_ref[...] = jnp.zeros_like(dC_ref)

        @pl.when(h == 0)
        def _():
            ddt_ref[...] = jnp.zeros_like(ddt_ref)
            dAc_ref[...] = jnp.zeros_like(dAc_ref)

        @pl.when((b == 0) & (h == 0))
        def _():
            dD_ref[...] = jnp.zeros_like(dD_ref)

        last_row = lax.broadcasted_iota(jnp.int32, (Q, 1), 0) == Q - 1
        first = lax.broadcasted_iota(jnp.int32, (Q, LANES), 1) < HEAD_DIM
        srow = lax.broadcasted_iota(jnp.int32, (PP, 1), 0) < HEAD_DIM

        def bwd(i, carry):
            dS, dD = carry
            c = nc - 1 - i
            S = st_ref[0, 0, c]
            q = _ssd_chunk(c, S, x_ref, b_ref, c_ref, v_ref, cu_ref, ct_ref, lane0)
            rows, x, Bb, Cb, xdt, Lm, CB = q["rows"], q["x"], q["Bb"], q["Cb"], q["xdt"], q["Lm"], q["CB"]
            dy = dy_ref[0, rows, :]
            dyb = dy.astype(BF16)
            xb = xdt.astype(BF16)
            Sb = S.astype(BF16)
            dD = dD + _colsum(dy * x)
            dyh = [jnp.where(first, dy, 0.0).astype(BF16), jnp.where(first, 0.0, dy).astype(BF16)]
            dM = [_dot_nt(dyh[hh], xb) for hh in range(HP)]
            dxdt = jnp.where(first, _dot_tn((CB * Lm[0]).astype(BF16), dyb), _dot_tn((CB * Lm[1]).astype(BF16), dyb))
            dCBb = (dM[0] * Lm[0] + dM[1] * Lm[1]).astype(BF16)
            dAc, dAr = [], []
            for hh in range(HP):
                G = dM[hh] * CB * Lm[hh]
                dAc.append(jnp.sum(G, axis=1, keepdims=True))
                dAr.append(-jnp.sum(G, axis=0, keepdims=True))
            dC = _dot(dCBb, Bb)
            dBm = _dot_tn(dCBb, Cb)
            z = _dot_nt(Cb, Sb)
            zs = _half_sums(dy * z)
            dzb = (dy * q["eA"]).astype(BF16)
            dC = dC + _dot(dzb, Sb)
            dS_in = _dot_tn(dzb, Cb)
            dSb = dS.astype(BF16)
            dW = _dot_nt(Bb, dSb)
            dBm = dBm + _dot(q["W"].astype(BF16), dSb)
            dxdt = dxdt + dW * q["e_end"]
            des = _half_sums(dW * xdt)
            ss = jnp.sum(dS * S, axis=1, keepdims=True)
            ss_lo = jnp.sum(jnp.where(srow, ss, 0.0), axis=0, keepdims=True)
            ss_h = [ss_lo, jnp.sum(ss, axis=0, keepdims=True) - ss_lo]
            ddts = _half_sums(dxdt * x)
            eA_h = [_lane_col(q["eA"], 0), _lane_col(q["eA"], HEAD_DIM)]
            dAc_tile = jnp.zeros((Q, LANES), F32)
            ddt_tile = jnp.zeros((Q, LANES), F32)
            for hh in range(HP):
                de = des[hh] * q["e_end_h"][hh]
                dAend = ss_h[hh] * q["eAend_h"][hh] + jnp.sum(de, axis=0, keepdims=True)
                col = dAc[hh] + zs[hh] * eA_h[hh] - de + jnp.where(last_row, dAend, 0.0)
                dAc_tile = dAc_tile + col * onehot[hh]
                ddt_tile = ddt_tile + ddts[hh] * onehot[hh]
                dAr_ref[0, hh, pl.ds(c, 1), :] = dAr[hh]
            dx_ref[0, rows, :] = dskip * dy + dxdt * q["dt"]
            dB_ref[0, 0, rows, :] += dBm
            dC_ref[0, 0, rows, :] += dC
            ddt_ref[0, rows, :] += ddt_tile
            dAc_ref[0, rows, :] += dAc_tile
            return dS * q["eAend"] + dS_in, dD

        _, dD = lax.fori_loop(0, nc, bwd, (jnp.zeros((PP, N), F32), jnp.zeros((1, LANES), F32)))
        dlo, dhi = _half_sums(dD)
        dD_ref[...] += dlo * onehot[0] + dhi * onehot[1]

    tm = pl.BlockSpec((1, L, LANES), lambda b, h: (b, 0, 0))
    grp = pl.BlockSpec((1, 1, L, N), lambda b, h: (b, h // pairs_per_group, 0, 0))
    xs = pl.BlockSpec((1, L, LANES), lambda b, h: (b, 0, h))
    return pl.pallas_call(
        kern, name=name, grid=(B, nh),
        in_specs=_ssd_specs(L, nc, b_blk, c_blk) + [xs, pl.BlockSpec((1, 1, nc, PP, N), lambda b, h: (b, h, 0, 0, 0))],
        out_specs=[xs, grp, grp, tm, tm, pl.BlockSpec((1, HP, nc, Q), lambda b, h: (b, h, 0, 0)),
                   pl.BlockSpec((1, LANES), lambda b, h: (0, 0))],
        out_shape=[jax.ShapeDtypeStruct((B, L, HEADS * HEAD_DIM), F32), jax.ShapeDtypeStruct((B, SSD_GROUPS, L, N), F32),
                   jax.ShapeDtypeStruct((B, SSD_GROUPS, L, N), F32), jax.ShapeDtypeStruct((B, L, LANES), F32),
                   jax.ShapeDtypeStruct((B, L, LANES), F32), jax.ShapeDtypeStruct((B, HEADS, nc, Q), F32),
                   jax.ShapeDtypeStruct((1, LANES), F32)],
        compiler_params=_cparams(("arbitrary", "arbitrary")),
    )(u, u, u, vals, cums, cums_t, dvec, dy, states)


LRU_TR = 384
LRU_CB = 512


def _lru_gates(xc, ra, ix, p_ref, first):
    r = _sigmoid(ra + p_ref[0:1, :])
    i = _sigmoid(ix + p_ref[1:2, :])
    ls = _log_sigmoid(p_ref[2:3, :])
    log_a = LRU_C * r * ls
    a = jnp.exp(log_a)
    mult0 = jnp.sqrt(_one_minus_exp(2.0 * log_a))
    mult = jnp.where(first, 1.0, mult0)
    return r, i, ls, a, mult0, mult


def _lru_fwd(u, xc_off, ra, ix, proj3, gate_off, pvec, name):
    B, L, D = ra.shape
    TR, CB = _tile(L, LRU_TR, 8), LRU_CB
    nrt = L // TR

    def kern(xc_ref, ra_ref, ix_ref, g_ref, p_ref, y_ref, hs_ref, a_ref, pa_s, pu_s, carry):
        rt = pl.program_id(2)

        @pl.when(rt == 0)
        def _():
            carry[...] = jnp.zeros_like(carry)

        row = lax.broadcasted_iota(jnp.int32, (TR, 1), 0)
        first = (rt == 0) & (row == 0)
        xc = xc_ref[0]
        r, i, ls, a, mult0, mult = _lru_gates(xc, ra_ref[0], ix_ref[0], p_ref, first)
        a_ref[0] = a
        pa, pu = a, mult * (i * xc)
        sub = row % SUBLANES
        for s in (1, 2, 4):
            ok = sub >= s
            pu = jnp.where(ok, pa * pltpu.roll(pu, s, 0) + pu, pu)
            pa = jnp.where(ok, pa * pltpu.roll(pa, s, 0), pa)
        pa_s[...] = pa
        pu_s[...] = pu
        row8 = lax.broadcasted_iota(jnp.int32, (SUBLANES, 1), 0)

        def gbody(g, c):
            r8 = pl.ds(pl.multiple_of(g * SUBLANES, SUBLANES), SUBLANES)
            hg = pa_s[r8, :] * c + pu_s[r8, :]
            hs_ref[0, r8, :] = hg
            return jnp.sum(jnp.where(row8 == SUBLANES - 1, hg, 0.0), axis=0, keepdims=True)

        carry[...] = lax.fori_loop(0, TR // SUBLANES, gbody, carry[...])
        y_ref[0] = (hs_ref[0] * _gelu(g_ref[0])).astype(BF16)

    def win(off):
        assert off % CB == 0
        return pl.BlockSpec((1, TR, CB), functools.partial(lambda b, j, t, o: (b, t, j + o), o=off // CB))

    return pl.pallas_call(
        kern, name=name, grid=(B, D // CB, nrt),
        in_specs=[win(xc_off), win(0), win(0), win(gate_off), pl.BlockSpec((SUBLANES, CB), lambda b, j, t: (0, j))],
        out_specs=[win(0)] * 3,
        out_shape=[jax.ShapeDtypeStruct((B, L, D), BF16), jax.ShapeDtypeStruct((B, L, D), F32),
                   jax.ShapeDtypeStruct((B, L, D), F32)],
        scratch_shapes=[pltpu.VMEM((TR, CB), F32), pltpu.VMEM((TR, CB), F32), pltpu.VMEM((1, CB), F32)],
        compiler_params=_cparams(("parallel", "parallel", "arbitrary")),
    )(u, ra, ix, proj3, pvec)


def _lru_bwd(dy, proj3, gate_off, hs, a, u, xc_off, ra, ix, pvec, name):
    B, L, D = ra.shape
    TR, CB = _tile(L, LRU_TR, 8), LRU_CB
    nrt = L // TR

    def kern(dy_ref, g_ref, hs_ref, hsp_ref, a_ref, an_ref, xc_ref, ra_ref, ix_ref, p_ref,
             dg_ref, dra_ref, dix_ref, dxc_ref, dp_ref, pb_s, pd_s, g_s, carry):
        b = pl.program_id(1)
        rt = pl.program_id(2)
        t = nrt - 1 - rt

        @pl.when((b == 0) & (rt == 0))
        def _():
            dp_ref[...] = jnp.zeros_like(dp_ref)

        @pl.when(rt == 0)
        def _():
            carry[...] = jnp.zeros_like(carry)

        row = lax.broadcasted_iota(jnp.int32, (TR, 1), 0)
        gate, hsv, av, dyv = g_ref[0], hs_ref[0], a_ref[0], dy_ref[0]
        dg_ref[0] = (dyv * hsv * _dgelu(gate)).astype(BF16)
        a_next = jnp.where(t == nrt - 1, 0.0, an_ref[0, 0:1, :])
        pb = jnp.where(row == TR - 1, a_next, pltpu.roll(av, TR - 1, 0))
        pd = dyv * _gelu(gate)
        sub = row % SUBLANES
        for s in (1, 2, 4):
            ok = sub < SUBLANES - s
            pd = jnp.where(ok, pd + pb * pltpu.roll(pd, TR - s, 0), pd)
            pb = jnp.where(ok, pb * pltpu.roll(pb, TR - s, 0), pb)
        pb_s[...] = pb
        pd_s[...] = pd
        row8 = lax.broadcasted_iota(jnp.int32, (SUBLANES, 1), 0)

        def gbody(i, c):
            r8 = pl.ds(pl.multiple_of((TR // SUBLANES - 1 - i) * SUBLANES, SUBLANES), SUBLANES)
            gg = pd_s[r8, :] + pb_s[r8, :] * c
            g_s[r8, :] = gg
            return jnp.sum(jnp.where(row8 == 0, gg, 0.0), axis=0, keepdims=True)

        carry[...] = lax.fori_loop(0, TR // SUBLANES, gbody, carry[...])
        gv = g_s[...]
        h_first = jnp.where(t == 0, 0.0, hsp_ref[0, SUBLANES - 1:SUBLANES, :])
        hprev = jnp.where(row == 0, h_first, pltpu.roll(hsv, 1, 0))
        first = (t == 0) & (row == 0)
        xc = xc_ref[0]
        r, i, ls, a2, mult0, mult = _lru_gates(xc, ra_ref[0], ix_ref[0], p_ref, first)
        dxc_ref[0] = gv * mult * i
        dlog_a = gv * hprev * av + jnp.where(first, 0.0, gv * i * xc * (-(av * av) / mult0))
        dra = dlog_a * LRU_C * ls * r * (1.0 - r)
        dix = gv * mult * xc * i * (1.0 - i)
        dra_ref[0] = dra.astype(BF16)
        dix_ref[0] = dix.astype(BF16)
        dp_ref[0:1, :] += _colsum(dra)
        dp_ref[1:2, :] += _colsum(dix)
        dp_ref[2:3, :] += _colsum(dlog_a * LRU_C * r) * _sigmoid(-p_ref[2:3, :])

    def win(off, shift=0):
        assert off % CB == 0
        o = off // CB
        return pl.BlockSpec((1, TR, CB), lambda j, b, rt: (b, jnp.clip(nrt - 1 - rt + shift, 0, nrt - 1), j + o))

    per_tile = TR // SUBLANES
    before = pl.BlockSpec((1, SUBLANES, CB), lambda j, b, rt: (b, jnp.maximum((nrt - 1 - rt) * per_tile - 1, 0), j))
    behind = pl.BlockSpec((1, SUBLANES, CB), lambda j, b, rt: (b, jnp.minimum((nrt - rt) * per_tile, nrt * per_tile - 1), j))

    return pl.pallas_call(
        kern, name=name, grid=(D // CB, B, nrt),
        in_specs=[win(0), win(gate_off), win(0), before, win(0), behind, win(xc_off), win(0), win(0),
                  pl.BlockSpec((SUBLANES, CB), lambda j, b, rt: (0, j))],
        out_specs=[win(0)] * 4 + [pl.BlockSpec((SUBLANES, CB), lambda j, b, rt: (0, j))],
        out_shape=[jax.ShapeDtypeStruct((B, L, D), BF16)] * 3 + [jax.ShapeDtypeStruct((B, L, D), F32),
                                                                 jax.ShapeDtypeStruct((SUBLANES, D), F32)],
        scratch_shapes=[pltpu.VMEM((TR, CB), F32)] * 3 + [pltpu.VMEM((1, CB), F32)],
        compiler_params=_cparams(("parallel", "arbitrary", "arbitrary")),
    )(dy, proj3, hs, hs, a, a, u, ra, ix, pvec)


def _sum8(parts, name):
    _, R, C = parts.shape
    tr = _tile(R, 1024, ROW_ALIGN if parts.dtype.itemsize == 2 else SUBLANES)

    def kern(p_ref, o_ref):
        acc = p_ref[0].astype(F32)
        for d in range(1, N_DEV):
            acc = acc + p_ref[d].astype(F32)
        o_ref[...] = acc

    return pl.pallas_call(
        kern, name=name, grid=(R // tr,), in_specs=[pl.BlockSpec((N_DEV, tr, C), lambda i: (0, i, 0))],
        out_specs=pl.BlockSpec((tr, C), lambda i: (i, 0)), out_shape=jax.ShapeDtypeStruct((R, C), F32),
        compiler_params=_cparams(("parallel",)),
    )(parts)


def _adamw(w, g, m, v, name):
    shape = w.shape
    C = shape[-1] if w.ndim > 1 else shape[0]
    R = w.size // C
    w2, g2, m2, v2 = (t.reshape(R, C) for t in (w, g, m, v))
    tr = R
    for cand in range(8, min(R, 512) + 1, 8):
        if R % cand == 0:
            tr = cand

    def kern(w_ref, g_ref, m_ref, v_ref, d_ref, nm_ref, nv_ref):
        gv = g_ref[...]
        nm = ADAM_B1 * m_ref[...] + (1.0 - ADAM_B1) * gv
        nv = ADAM_B2 * v_ref[...] + (1.0 - ADAM_B2) * (gv * gv)
        m_hat = nm / (1.0 - ADAM_B1 ** ADAM_STEP)
        v_hat = nv / (1.0 - ADAM_B2 ** ADAM_STEP)
        d_ref[...] = -ADAM_LR * (m_hat / (jnp.sqrt(v_hat) + ADAM_EPS) + ADAM_WD * w_ref[...])
        nm_ref[...] = nm
        nv_ref[...] = nv

    spec = pl.BlockSpec((tr, C), lambda i: (i, 0))
    outs = pl.pallas_call(
        kern, name=name, grid=(R // tr,), in_specs=[spec] * 4, out_specs=[spec] * 3,
        out_shape=[jax.ShapeDtypeStruct((R, C), F32)] * 3, compiler_params=_cparams(("parallel",)),
    )(w2, g2, m2, v2)
    return tuple(o.reshape(shape) for o in outs)


MESH_ID = pl.DeviceIdType.MESH
ANY = pl.BlockSpec(memory_space=pl.ANY)
N_COPIES = N_DEV - 1
COMM_SCRATCH = [pltpu.SemaphoreType.DMA((N_COPIES,)), pltpu.SemaphoreType.DMA((N_COPIES,)), pltpu.SemaphoreType.DMA]


def _my_place():
    return lax.axis_index("x"), lax.axis_index("y"), lax.axis_index("c")


def _gather_phases(x_ref, out_ref, send_sems, recv_sems, local_sem):
    x, y, c = _my_place()
    me, sibling = (x, y, c), (x, y, 1 - c)
    chips = [(1 - x, y), (x, 1 - y), (1 - x, 1 - y)]

    def slab(px, py, pc):
        return out_ref.at[4 * px + 2 * py + pc]

    def copy(k, block, to, src=None):
        return pltpu.make_async_remote_copy(
            src_ref=slab(*block) if src is None else src, dst_ref=slab(*block),
            send_sem=send_sems.at[k], recv_sem=recv_sems.at[k], device_id=to, device_id_type=MESH_ID)

    mine = pltpu.make_async_copy(x_ref, slab(*me), local_sem)
    first = [copy(0, me, sibling, src=x_ref)] + [copy(1 + j, me, (*chip, c), src=x_ref) for j, chip in enumerate(chips)]
    passed = [copy(4 + j, (*chip, c), sibling) for j, chip in enumerate(chips)]

    def start():
        mine.start()
        for cp in first:
            cp.start()

    def forward():
        for j, chip in enumerate(chips):
            copy(1 + j, (*chip, c), me).wait_recv()
            passed[j].start()

    def finish():
        copy(0, sibling, me).wait_recv()
        for j, chip in enumerate(chips):
            copy(4 + j, (*chip, 1 - c), me).wait_recv()
        for cp in first + passed:
            cp.wait_send()
        mine.wait()

    return start, forward, finish


def _exchange_phases(p_ref, out_ref, send_sems, recv_sems, local_sem):
    x, y, c = _my_place()
    my_idx = 4 * x + 2 * y + c
    mine = pltpu.make_async_copy(p_ref.at[my_idx], out_ref.at[my_idx], local_sem)
    copies = []
    for k in range(1, N_DEV):
        px, py, pc = x ^ (k >> 2), y ^ ((k >> 1) & 1), c ^ (k & 1)
        copies.append(pltpu.make_async_remote_copy(
            src_ref=p_ref.at[4 * px + 2 * py + pc], dst_ref=out_ref.at[my_idx],
            send_sem=send_sems.at[k - 1], recv_sem=recv_sems.at[k - 1], device_id=(px, py, pc),
            device_id_type=MESH_ID))

    def start():
        mine.start()
        for cp in copies:
            cp.start()

    def finish():
        for cp in copies:
            cp.wait()
        mine.wait()

    return start, finish


def _ride(lin, total, phases):
    assert total >= 3
    marks = [0, total - 1] if len(phases) == 2 else [0, total // 2, total - 1]
    for mark, phase in zip(marks, phases):
        pl.when(lin == mark)(phase)


def _all_gather(xs, name):
    R, C = xs.shape

    def body(x_ref, out_ref, send_sems, recv_sems, local_sem):
        for phase in _gather_phases(x_ref, out_ref, send_sems, recv_sems, local_sem):
            phase()

    return pl.pallas_call(
        body, name=name, out_shape=jax.ShapeDtypeStruct((N_DEV, R, C), xs.dtype), in_specs=[ANY], out_specs=ANY,
        scratch_shapes=COMM_SCRATCH,
    )(xs)


def _exchange(parts, name):
    def body(p_ref, out_ref, send_sems, recv_sems, local_sem):
        for phase in _exchange_phases(p_ref, out_ref, send_sems, recv_sems, local_sem):
            phase()

    return pl.pallas_call(
        body, name=name, out_shape=jax.ShapeDtypeStruct(parts.shape, parts.dtype), in_specs=[ANY], out_specs=ANY,
        scratch_shapes=COMM_SCRATCH,
    )(parts)


D_XBC_EXTRA = 2 * SSD_GROUPS * SSD_STATE
SMALL_W = LANES
ROW_ALIGN = 16


def _layout(D):
    d_xbc = D + D_XBC_EXTRA
    off = dict(qkv=0, z=3 * D, merge=4 * D, gate=7 * D, conv=8 * D, xr=8 * D + d_xbc, small=9 * D + d_xbc)
    off["n_all"] = off["small"] + SMALL_W
    off["d_xbc"] = d_xbc
    off["conv_c"] = d_xbc + D
    return off


def _w_in_map(D):
    lo = _layout(D)
    widths = [("q", D, 0), ("k", D, D), ("v", D, 2 * D), ("f", HEADS, lo["small"]), ("z", D, lo["z"]),
              ("xbc", lo["d_xbc"], lo["conv"]), ("dt", HEADS, lo["small"] + HEADS), ("xr", D, lo["xr"]),
              ("gate", D, lo["gate"]), ("merge", 3 * D, lo["merge"])]
    out, o = [], 0
    for _, w, mine in widths:
        out.append((o, w, mine))
        o += w
    return out


def _padded(c):
    return -(-c // ROW_ALIGN) * ROW_ALIGN


def _permute_rows(src, pieces, name):
    R, C = src.shape
    n_out = sum(n for _, n in pieces)

    def kern(x_ref, o_ref):
        o = 0
        for start, n in pieces:
            if start is None:
                o_ref[o:o + n, :] = jnp.zeros((n, LANES), src.dtype)
            else:
                o_ref[o:o + n, :] = x_ref[start:start + n, :]
            o += n

    return pl.pallas_call(
        kern, name=name, grid=(C // LANES,), in_specs=[pl.BlockSpec((R, LANES), lambda i: (0, i))],
        out_specs=pl.BlockSpec((n_out, LANES), lambda i: (0, i)), out_shape=jax.ShapeDtypeStruct((n_out, C), src.dtype),
        compiler_params=_cparams(("parallel",)),
    )(src)


def _reorder_rows(wt, D, c, name="reorder_w_in"):
    cp = _padded(c)
    lo = _layout(D)
    pieces = []
    for a, w, mine in sorted(_w_in_map(D), key=lambda t: t[2]):
        b = a + w
        while a < b:
            j = a // c
            e = min(b, (j + 1) * c)
            pieces.append((j * cp + a - j * c, e - a))
            a = e
    pieces.append((None, lo["n_all"] - lo["small"] - 2 * HEADS))
    return _permute_rows(wt, pieces, name)


def _restore_rows(dwt, D, c, name="restore_w_in"):
    cp = _padded(c)
    segs = _w_in_map(D)
    pieces = []
    for j in range(N_DEV):
        a, b = j * c, (j + 1) * c
        for s0, w, mine in segs:
            lo_, hi_ = max(a, s0), min(b, s0 + w)
            if lo_ < hi_:
                pieces.append((mine + lo_ - s0, hi_ - lo_))
        if cp > c:
            pieces.append((None, cp - c))
    return _permute_rows(dwt, pieces, name)


def _block_diag(w):
    H, n, _ = w.shape
    tiled = jnp.tile(w.reshape(H * n, n), (1, H))
    r = lax.broadcasted_iota(jnp.int32, (H * n, H * n), 0) // n
    c = lax.broadcasted_iota(jnp.int32, (H * n, H * n), 1) // n
    return jnp.where(r == c, tiled, jnp.zeros_like(tiled))


def _diag_blocks(m, H):
    n = m.shape[0] // H
    keep = jnp.eye(H, dtype=m.dtype)[:, None, :, None]
    return jnp.sum(m.reshape(H, n, H, n) * keep, axis=2)


def _to_heads(t, B, L):
    return t.reshape(B, L, HEADS, HEAD_DIM).transpose(0, 2, 1, 3)


def _from_heads(t4):
    B, H, L, P = t4.shape
    return t4.transpose(0, 2, 1, 3).reshape(B * L, H * P)


def _rows_to_tm(rows):
    B, H, nc, Q = rows.shape
    return rows.reshape(B, H, nc * Q).transpose(0, 2, 1)


def _ffn_fwd(h, g, wgu_t, wd, tag):
    n = _norm_fwd(h, g, tag + "_norm")
    gate, up, act = _ffn_up_act(n, wgu_t, tag + "_up")
    out = _mm(act, wd, res=h, scale=0.5, name=tag + "_down")
    return out, (h, n, gate, up, act)


def _ffn_bwd(dh, dhb, saved, g, wgu_t, wd, tag):
    h, n, gate, up, act = saved
    dgu = _ffn_down_dx_act(dhb, wd, gate, up, tag + "_down_dx")
    dwd = _mm(act, dhb, ta=True, scale=0.5, name=tag + "_down_dw")
    dwgu_t = _mm(dgu, n, ta=True, tn=1024, name=tag + "_up_dw")
    dn = _mm(dgu, wgu_t, name=tag + "_up_dx")
    dh_in, dhb_in, dg = _norm_bwd(h, dn, dh, g, tag + "_norm_bwd")
    return dh_in, dhb_in, dict(norm=dg, gu=dwgu_t, down=dwd)


def _mixer_fwd(h, p, B, L, gather=None):
    T, D = h.shape
    lo = _layout(D)
    n = _norm_fwd(h, p["gm"], "mix_norm")
    proj = _mm(n, p["w_all_t"], tb=True, name="mix_in")
    proj3 = proj.reshape(B, L, lo["n_all"])
    vals, cums = _gate_prep(proj3, lo["small"] // LANES, p["small_bias"], p["avec"], "gate_prep")
    cums_t = cums[..., :2 * HEADS].transpose(0, 2, 1).reshape(B, 2 * HEADS, L // Q_BLOCK, Q_BLOCK)
    qa, ka, va = _attn_pack(proj3, cums, "attn_pack")
    y_a3, y_ab3, lse, *gathered = _attn_fwd(qa, ka, va, "attn_fwd", gather)
    y_a, y_ab = y_a3.reshape(T, D), y_ab3.reshape(T, D)
    u = _conv_fwd(proj3, lo["conv"] // LANES, p["conv_w"], p["conv_b"], lo["d_xbc"] // LANES, "conv_fwd")
    b_blk = D // LANES
    c_blk = b_blk + SSD_GROUPS * SSD_STATE // LANES
    y_s3, ssd_states = _ssd_fwd(u, b_blk, c_blk, vals, cums, cums_t, p["dvec"], "ssd_fwd")
    y_s = y_s3.reshape(T, D)
    yb = _gnorm_fwd(y_s, proj, lo["z"], p["ssd_norm"], "gnorm_fwd")
    u2 = u.reshape(T, lo["conv_c"])
    ra = _mm(u2, p["wa"], a_off=(0, lo["d_xbc"]), dims=(T, D, D), tk=512, name="lru_ra")
    ix = _mm(u2, p["wx"], a_off=(0, lo["d_xbc"]), dims=(T, D, D), tk=512, name="lru_ix")
    yc, hs, a = _lru_fwd(u, lo["d_xbc"], ra.reshape(B, L, D), ix.reshape(B, L, D), proj3, lo["gate"], p["pvec"],
                         "lru_fwd")
    yc = yc.reshape(T, D)
    pa = _mm(y_ab, p["wba"], name="branch_attn")
    pb = _mm(yb, p["wbs"], name="branch_ssd")
    pc = _mm(yc, p["wbl"], name="branch_lru")
    mixed = _merge_fwd(proj, lo["merge"], pa, pb, pc, "merge_fwd")
    out = _mm(mixed, p["wout"], res=h, name="mix_out")
    saved = dict(h=h, n=n, proj=proj, qa=qa, ka=ka, va=va, vals=vals, cums=cums, cums_t=cums_t, lse=lse, y_a=y_a, y_ab=y_ab,
                 u=u, y_s=y_s, ssd_states=ssd_states, yb=yb, ra=ra, ix=ix, yc=yc, hs=hs, a=a, pa=pa, pb=pb, pc=pc, mixed=mixed)
    return out, saved, (gathered[0] if gathered else None)


def _mixer_bwd(dh, dhb, s, p, B, L, parts=None):
    T, D = dh.shape
    lo = _layout(D)
    proj, u = s["proj"], s["u"]
    proj3 = proj.reshape(B, L, lo["n_all"])
    g = {}
    dmixed = _mm(dhb, p["wout"], tb=True, name="mix_out_dx")
    g["wout"] = _mm(s["mixed"], dhb, ta=True, name="mix_out_dw")
    dpa, dpb, dpc, dmerge = _merge_bwd(dmixed, proj, lo["merge"], s["pa"], s["pb"], s["pc"], "merge_bwd")
    dy_a = _mm(dpa, p["wba"], tb=True, name="branch_attn_dx")
    g["wba"] = _mm(s["y_ab"], dpa, ta=True, name="branch_attn_dw")
    dyb = _mm(dpb, p["wbs"], tb=True, name="branch_ssd_dx")
    g["wbs"] = _mm(s["yb"], dpb, ta=True, name="branch_ssd_dw")
    dyc = _mm(dpc, p["wbl"], tb=True, name="branch_lru_dx")
    g["wbl"] = _mm(s["yc"], dpc, ta=True, name="branch_lru_dw")
    dgate, dra, dix, dxc, g["pvec"] = _lru_bwd(dyc.reshape(B, L, D), proj3, lo["gate"], s["hs"], s["a"], u, lo["d_xbc"],
                                               s["ra"].reshape(B, L, D), s["ix"].reshape(B, L, D), p["pvec"], "lru_bwd")
    dra, dix = dra.reshape(T, D), dix.reshape(T, D)
    u2 = u.reshape(T, lo["conv_c"])
    g["wa"] = _mm(u2, dra, ta=True, a_off=(0, lo["d_xbc"]), dims=(D, D, T), tm=512, name="lru_ra_dw")
    g["wx"] = _mm(u2, dix, ta=True, a_off=(0, lo["d_xbc"]), dims=(D, D, T), tm=512, name="lru_ix_dw")
    dxc = _mm(dra, p["wa"], tb=True, res=dxc.reshape(T, D), name="lru_ra_dx")
    dxc = _mm(dix, p["wx"], tb=True, res=dxc, name="lru_ix_dx")
    dy_s, dz, g["ssd_norm"] = _gnorm_bwd(dyb, s["y_s"], proj, lo["z"], p["ssd_norm"], "gnorm_bwd")
    b_blk = D // LANES
    c_blk = b_blk + SSD_GROUPS * SSD_STATE // LANES
    dxs, dBg, dCg, ddt_tm, dAc_tm, dAr, g["dvec"] = _ssd_bwd(u, b_blk, c_blk, s["vals"], s["cums"], s["cums_t"],
                                                             p["dvec"], dy_s.reshape(B, L, D), s["ssd_states"], "ssd_bwd")
    grp = lambda t: t.transpose(0, 2, 1, 3).reshape(B, L, SSD_GROUPS * SSD_STATE)
    du = jnp.concatenate([dxs, grp(dBg), grp(dCg), dxc.reshape(B, L, D)], axis=-1)
    dconv, g["conv_wb"] = _conv_bwd(proj3, lo["conv"] // LANES, du, p["conv_w"], p["conv_b"], lo["d_xbc"] // LANES,
                                    "conv_bwd")
    dq3, dk3, dv3, dc_tm, *recv = _attn_bwd(s["qa"], s["ka"], s["va"], s["y_a"].reshape(B, L, D),
                                            dy_a.reshape(B, L, D), s["lse"], "attn_bwd", parts)
    drow_tm = dc_tm + jnp.pad(_rows_to_tm(dAr), ((0, 0), (0, 0), (HEADS, LANES - 2 * HEADS)))
    dsmall, g["small_bias"], g["avec"] = _gate_post(drow_tm, dAc_tm, ddt_tm, proj3, lo["small"] // LANES, s["vals"],
                                                    p["small_bias"], p["avec"], "gate_post")
    dproj = jnp.concatenate([dq3.reshape(T, D), dk3.reshape(T, D), dv3.reshape(T, D), dz, dmerge, dgate.reshape(T, D), dconv.reshape(T, lo["conv_c"]),
                             dsmall.reshape(T, SMALL_W)], axis=1)
    g["w_all_t"] = _mm(dproj, s["n"], ta=True, tn=1024, name="mix_in_dw")
    dn = _mm(dproj, p["w_all_t"], name="mix_in_dx")
    dh_in, dhb_in, g["gm"] = _norm_bwd(s["h"], dn, dh, p["gm"], "mix_norm_bwd")
    return dh_in, dhb_in, g, (recv[0] if recv else None)


def _small_vec(a, b):
    return jnp.concatenate([a, b, jnp.zeros((LANES - 2 * HEADS,), F32)])[None, :]


def _layer_params(w):
    zeros16 = jnp.zeros((HEADS,), F32)
    pvec = jnp.concatenate([w["lru_b_a"][None], w["lru_b_x"][None], w["lru_lambda"][None],
                            jnp.zeros((SUBLANES - 3, w["lru_b_a"].shape[0]), F32)], axis=0)
    return dict(
        g1=w["ffn1_norm"][None], gu1=w["ffn1_w_gate_up"], d1=w["ffn1_w_down"],
        gm=w["mix_norm"][None], w_all_t=w["w_in"],
        small_bias=_small_vec(w["fox_forget_bias"], w["ssd_dt_bias"]),
        avec=_small_vec(zeros16, -jnp.exp(w["ssd_a_log"])), dvec=_small_vec(zeros16, w["ssd_d"]),
        conv_w=jnp.concatenate([w["ssd_conv_w"], w["lru_conv_w"]], axis=1),
        conv_b=jnp.concatenate([w["ssd_conv_b"], w["lru_conv_b"]])[None],
        ssd_norm=w["ssd_norm"][None],
        wa=_block_diag(w["lru_w_a"]).astype(BF16), wx=_block_diag(w["lru_w_x"]).astype(BF16), pvec=pvec,
        wba=w["w_branch_attn"], wbs=w["w_branch_ssd"], wbl=w["w_branch_lru"], wout=w["w_out"],
        g2=w["ffn2_norm"][None], gu2=w["ffn2_w_gate_up"], d2=w["ffn2_w_down"],
    )


def _layer_fwd(h, p, B, L, gather=None):
    h, s1 = _ffn_fwd(h, p["g1"], p["gu1"], p["d1"], "ffn1")
    h, sm, gathered = _mixer_fwd(h, p, B, L, gather)
    h, s2 = _ffn_fwd(h, p["g2"], p["gu2"], p["d2"], "ffn2")
    return h, (s1, sm, s2), gathered


def _layer_bwd(dh, dhb, saved, p, w, B, L, parts=None):
    s1, sm, s2 = saved
    D = dh.shape[1]
    d_xbc = D + D_XBC_EXTRA
    dh, dhb, f2 = _ffn_bwd(dh, dhb, s2, p["g2"], p["gu2"], p["d2"], "ffn2")
    dh, dhb, gm, recv = _mixer_bwd(dh, dhb, sm, p, B, L, parts)
    dh, dhb, f1 = _ffn_bwd(dh, dhb, s1, p["g1"], p["gu1"], p["d1"], "ffn1")
    sb, av = gm["small_bias"][0], gm["avec"][0]
    cw = gm["conv_wb"]
    grads = dict(
        ffn1_norm=f1["norm"][0], ffn1_w_gate_up=f1["gu"], ffn1_w_down=f1["down"],
        mix_norm=gm["gm"][0], w_in=gm["w_all_t"],
        fox_forget_bias=sb[:HEADS], ssd_conv_w=cw[:CONV_K, :d_xbc], ssd_conv_b=cw[CONV_K, :d_xbc],
        ssd_dt_bias=sb[HEADS:2 * HEADS], ssd_a_log=av[HEADS:2 * HEADS] * (-jnp.exp(w["ssd_a_log"])),
        ssd_d=gm["dvec"][0, HEADS:2 * HEADS], ssd_norm=gm["ssd_norm"][0],
        lru_conv_w=cw[:CONV_K, d_xbc:], lru_conv_b=cw[CONV_K, d_xbc:],
        lru_w_a=_diag_blocks(gm["wa"], HEADS), lru_b_a=gm["pvec"][0], lru_w_x=_diag_blocks(gm["wx"], HEADS),
        lru_b_x=gm["pvec"][1], lru_lambda=gm["pvec"][2],
        w_branch_attn=gm["wba"], w_branch_ssd=gm["wbs"], w_branch_lru=gm["wbl"], w_out=gm["wout"],
        ffn2_norm=f2["norm"][0], ffn2_w_gate_up=f2["gu"], ffn2_w_down=f2["down"],
    )
    return dh, dhb, grads, recv


LAYER_NAMES = ["ffn1_norm", "ffn1_w_gate_up", "ffn1_w_down", "mix_norm", "w_in", "fox_forget_bias", "ssd_conv_w",
               "ssd_conv_b", "ssd_dt_bias", "ssd_a_log", "ssd_d", "ssd_norm", "lru_conv_w", "lru_conv_b", "lru_w_a",
               "lru_b_a", "lru_w_x", "lru_b_x", "lru_lambda", "w_branch_attn", "w_branch_ssd", "w_branch_lru", "w_out",
               "ffn2_norm", "ffn2_w_gate_up", "ffn2_w_down"]
WEIGHT_NAMES = ["meta_tokens"] + LAYER_NAMES + ["final_norm"]


def _local_step(x, target, meta, final_norm, depth, layer_weights, pack_next=None, pack_grads=None):
    B, S, D = x.shape
    L = -(-(N_META + S) // Q_BLOCK) * Q_BLOCK
    h = jnp.concatenate([jnp.broadcast_to(meta[None], (B, N_META, D)), x,
                         jnp.zeros((B, L - N_META - S, D), F32)], axis=1).reshape(B * L, D)
    weights, params, saved = [], [], []
    gathered = None
    for l in range(depth):
        w = layer_weights(l, gathered)
        p = _layer_params(w)
        nxt = pack_next(l + 1) if (pack_next is not None and l + 1 < depth) else None
        h, s, gathered = _layer_fwd(h, p, B, L, nxt)
        weights.append(w)
        params.append(p)
        saved.append(s)
    tgt = jnp.pad(target, ((0, 0), (N_META, L - N_META - S), (0, 0))).reshape(B * L, D)
    dh, dhb, loss, dfinal = _loss_bwd(h, tgt, final_norm[None], L, S, "loss")
    grads = [None] * depth
    received, parts = {}, None
    for l in reversed(range(depth)):
        dh, dhb, grads[l], recv = _layer_bwd(dh, dhb, saved[l], params[l], weights[l], B, L, parts)
        if recv is not None:
            received[l + 1] = recv
        parts = pack_grads(grads[l]) if pack_grads is not None else None
    dh3 = dh.reshape(B, L, D)
    return (loss, dh3[:, N_META:N_META + S], jnp.sum(dh3[:, :N_META], axis=0), grads, dfinal[0], received, parts)


BIG_NAMES = ["ffn1_w_gate_up", "ffn1_w_down", "w_in", "w_branch_attn", "w_branch_ssd", "w_branch_lru", "w_out",
             "ffn2_w_gate_up", "ffn2_w_down"]
COL_SHARDED = {"ffn1_w_gate_up", "w_in", "ffn2_w_gate_up"}
SMALL_SHARDED = ["meta_tokens", "ssd_conv_w", "lru_conv_w"]
SMALL_NAMES = [n for n in LAYER_NAMES if n not in BIG_NAMES]


def _shard_rows(name, shape):
    return _padded(shape[1]) if name in COL_SHARDED else shape[0]


def _pack_shards(shards):
    rows = []
    for n in BIG_NAMES:
        s = shards[n]
        if n in COL_SHARDED:
            s = jnp.pad(s.T, ((0, _padded(s.shape[1]) - s.shape[1]), (0, 0)))
        rows.append(s)
    return jnp.concatenate(rows, axis=0)


def _unpack_gathered(gathered, shapes, D):
    out, o = {}, 0
    for n in BIG_NAMES:
        r = _shard_rows(n, shapes[n])
        out[n] = gathered[:, o:o + r].reshape(N_DEV * r, D)
        o += r
    out["w_in"] = _reorder_rows(out["w_in"], D, shapes["w_in"][1])
    return out


def _pack_full_grads(grads, shapes, D):
    slabs = []
    for n in BIG_NAMES:
        g = grads[n]
        if n == "w_in":
            g = _restore_rows(g, D, shapes[n][1])
        slabs.append(g.reshape(N_DEV, _shard_rows(n, shapes[n]), D))
    return jnp.concatenate(slabs, axis=1)


def _unpack_local(rows, shapes):
    out, o = {}, 0
    for n in BIG_NAMES:
        r = _shard_rows(n, shapes[n])
        blk = rows[o:o + r]
        out[n] = blk[:shapes[n][1]].T if n in COL_SHARDED else blk
        o += r
    return out


def _as_rows(flat):
    n = flat.shape[0]
    unit = LANES * SUBLANES
    total = -(-n // unit) * unit
    return jnp.pad(flat, (0, total - n)).reshape(total // LANES, LANES)


def _flatten_list(arrs):
    return _as_rows(jnp.concatenate([a.reshape(-1) for a in arrs]))


def _split_like(rows, shapes):
    flat = rows.reshape(-1)
    out, o = [], 0
    for s in shapes:
        n = math.prod(s)
        out.append(flat[o:o + n].reshape(s))
        o += n
    return out


def _gather_last(rows8, shape):
    lead, c = shape[:-1], shape[-1]
    t = rows8.reshape((N_DEV,) + tuple(lead) + (c,))
    return jnp.moveaxis(t, 0, -2).reshape(tuple(lead) + (N_DEV * c,))


def kernel(x, meta_tokens, ffn1_norm, ffn1_w_gate_up, ffn1_w_down, mix_norm, w_in, fox_forget_bias, ssd_conv_w, ssd_conv_b, ssd_dt_bias, ssd_a_log, ssd_d, ssd_norm, lru_conv_w, lru_conv_b, lru_w_a, lru_b_a, lru_w_x, lru_b_x, lru_lambda, w_branch_attn, w_branch_ssd, w_branch_lru, w_out, ffn2_norm, ffn2_w_gate_up, ffn2_w_down, final_norm, loss_target, m_meta_tokens, m_ffn1_norm, m_ffn1_w_gate_up, m_ffn1_w_down, m_mix_norm, m_w_in, m_fox_forget_bias, m_ssd_conv_w, m_ssd_conv_b, m_ssd_dt_bias, m_ssd_a_log, m_ssd_d, m_ssd_norm, m_lru_conv_w, m_lru_conv_b, m_lru_w_a, m_lru_b_a, m_lru_w_x, m_lru_b_x, m_lru_lambda, m_w_branch_attn, m_w_branch_ssd, m_w_branch_lru, m_w_out, m_ffn2_norm, m_ffn2_w_gate_up, m_ffn2_w_down, m_final_norm, v_meta_tokens, v_ffn1_norm, v_ffn1_w_gate_up, v_ffn1_w_down, v_mix_norm, v_w_in, v_fox_forget_bias, v_ssd_conv_w, v_ssd_conv_b, v_ssd_dt_bias, v_ssd_a_log, v_ssd_d, v_ssd_norm, v_lru_conv_w, v_lru_conv_b, v_lru_w_a, v_lru_b_a, v_lru_w_x, v_lru_b_x, v_lru_lambda, v_w_branch_attn, v_w_branch_ssd, v_w_branch_lru, v_w_out, v_ffn2_norm, v_ffn2_w_gate_up, v_ffn2_w_down, v_final_norm):
    weights = dict(zip(WEIGHT_NAMES, (meta_tokens, ffn1_norm, ffn1_w_gate_up, ffn1_w_down, mix_norm, w_in, fox_forget_bias, ssd_conv_w, ssd_conv_b, ssd_dt_bias, ssd_a_log, ssd_d, ssd_norm, lru_conv_w, lru_conv_b, lru_w_a, lru_b_a, lru_w_x, lru_b_x, lru_lambda, w_branch_attn, w_branch_ssd, w_branch_lru, w_out, ffn2_norm, ffn2_w_gate_up, ffn2_w_down, final_norm,)))
    mom1 = dict(zip(WEIGHT_NAMES, (m_meta_tokens, m_ffn1_norm, m_ffn1_w_gate_up, m_ffn1_w_down, m_mix_norm, m_w_in, m_fox_forget_bias, m_ssd_conv_w, m_ssd_conv_b, m_ssd_dt_bias, m_ssd_a_log, m_ssd_d, m_ssd_norm, m_lru_conv_w, m_lru_conv_b, m_lru_w_a, m_lru_b_a, m_lru_w_x, m_lru_b_x, m_lru_lambda, m_w_branch_attn, m_w_branch_ssd, m_w_branch_lru, m_w_out, m_ffn2_norm, m_ffn2_w_gate_up, m_ffn2_w_down, m_final_norm,)))
    mom2 = dict(zip(WEIGHT_NAMES, (v_meta_tokens, v_ffn1_norm, v_ffn1_w_gate_up, v_ffn1_w_down, v_mix_norm, v_w_in, v_fox_forget_bias, v_ssd_conv_w, v_ssd_conv_b, v_ssd_dt_bias, v_ssd_a_log, v_ssd_d, v_ssd_norm, v_lru_conv_w, v_lru_conv_b, v_lru_w_a, v_lru_b_a, v_lru_w_x, v_lru_b_x, v_lru_lambda, v_w_branch_attn, v_w_branch_ssd, v_w_branch_lru, v_w_out, v_ffn2_norm, v_ffn2_w_gate_up, v_ffn2_w_down, v_final_norm,)))
    depth = ffn1_norm.shape[0]
    D = x.shape[-1]
    my_idx = 4 * lax.axis_index("x") + 2 * lax.axis_index("y") + lax.axis_index("c")

    small_shapes = [weights[n].shape for n in SMALL_SHARDED]
    gathered = _all_gather(_flatten_list([weights[n] for n in SMALL_SHARDED]), "gather_small").reshape(N_DEV, -1)
    small_full, o = {}, 0
    for n, s in zip(SMALL_SHARDED, small_shapes):
        k = math.prod(s)
        small_full[n] = _gather_last(gathered[:, o:o + k], s)
        o += k

    shard_shapes = {n: weights[n].shape[1:] for n in BIG_NAMES}
    pack_next = lambda l: _pack_shards({n: weights[n][l].astype(BF16) for n in BIG_NAMES})

    def layer_weights(l, gathered):
        if gathered is None:
            gathered = _all_gather(pack_next(l), "gather_weights")
        w = _unpack_gathered(gathered, shard_shapes, D)
        for n in SMALL_NAMES:
            w[n] = small_full[n][l] if n in SMALL_SHARDED else weights[n][l]
        return w

    pack_grads = lambda g: _pack_full_grads(g, shard_shapes, D).astype(BF16)
    loss, dx, dmeta, grads, dfinal, received, parts = _local_step(
        x, loss_target, small_full["meta_tokens"], final_norm, depth, layer_weights, pack_next, pack_grads)
    received[0] = _exchange(parts, "exchange_grads")
    loss = lax.psum(loss[0, 0], ("x", "y", "c"))
    summed = {n: [] for n in WEIGHT_NAMES}
    for l in range(depth):
        local = _unpack_local(_sum8(received[l], "sum_grads"), shard_shapes)
        for n in BIG_NAMES:
            summed[n].append(local[n])

    small_list = [dmeta, dfinal] + [grads[l][n] for l in range(depth) for n in SMALL_NAMES]
    total = _sum8(_all_gather(_flatten_list(small_list), "gather_small_grads"), "sum_small_grads")
    parts = _split_like(total, [a.shape for a in small_list])
    full_small = {"meta_tokens": parts[0], "final_norm": parts[1]}
    for i, n in enumerate(SMALL_NAMES):
        full_small[n] = jnp.stack([parts[2 + l * len(SMALL_NAMES) + i] for l in range(depth)])
    grad = {}
    for n in WEIGHT_NAMES:
        if n in BIG_NAMES:
            grad[n] = jnp.stack(summed[n])
        elif n in SMALL_SHARDED:
            c = weights[n].shape[-1]
            grad[n] = lax.dynamic_slice_in_dim(full_small[n], my_idx * c, c, axis=full_small[n].ndim - 1)
        else:
            grad[n] = full_small[n]

    delta, new_m, new_v = {}, {}, {}
    for n in WEIGHT_NAMES:
        delta[n], new_m[n], new_v[n] = _adamw(weights[n], grad[n], mom1[n], mom2[n], "adamw_" + n)
    return (loss, dx, *[grad[n] for n in WEIGHT_NAMES], *[delta[n] for n in WEIGHT_NAMES],
            *[new_m[n] for n in WEIGHT_NAMES], *[new_v[n] for n in WEIGHT_NAMES])
```

```python
import functools
import math

import jax
import jax.numpy as jnp
from jax import lax
from jax.experimental import pallas as pl
from jax.experimental.pallas import tpu as pltpu

F32 = jnp.float32
BF16 = jnp.bfloat16

N_DEV = 8
N_META = 16
Q_BLOCK = 128
NORM_EPS = 1e-6
HEADS = 16
HEAD_DIM = 64
SSD_GROUPS = 2
SSD_STATE = 128
CONV_K = 4
LRU_C = 8.0
ADAM_LR, ADAM_B1, ADAM_B2, ADAM_EPS, ADAM_WD, ADAM_STEP = 0.001, 0.9, 0.999, 1e-08, 0.01, 10

LANES = 128
SUBLANES = 8
VMEM_LIMIT = 56 * 1024 * 1024
NEG = -1e30
MM_TILE = 1408
MM_VMEM = 40 * 1024 * 1024


def _cparams(sem=None):
    return pltpu.CompilerParams(dimension_semantics=sem, vmem_limit_bytes=VMEM_LIMIT)


def _tile(dim, target, mult=LANES):
    if dim <= target:
        return dim
    best = None
    for t in range(mult, target + 1, mult):
        if dim % t == 0:
            best = t
    assert best is not None, (dim, target)
    return best


def _sigmoid(x):
    return 1.0 / (1.0 + jnp.exp(-x))


def _log1p_exp_neg_abs(x):
    e = jnp.exp(-jnp.abs(x))
    u = 1.0 + e
    return jnp.where(u == 1.0, e, jnp.log(u) * (e / jnp.where(u == 1.0, 1.0, u - 1.0)))


def _log_sigmoid(x):
    return jnp.minimum(x, 0.0) - _log1p_exp_neg_abs(x)


def _softplus(x):
    return jnp.maximum(x, 0.0) + _log1p_exp_neg_abs(x)


def _one_minus_exp(y):
    u = jnp.exp(y)
    safe = jnp.where(u == 1.0, 0.5, u)
    return jnp.where(u == 1.0, -y, (1.0 - u) * y / jnp.log(safe))


def _silu(x):
    return x * _sigmoid(x)


def _dsilu(x):
    s = _sigmoid(x)
    return s * (1.0 + x * (1.0 - s))


_GELU_C = math.sqrt(2.0 / math.pi)


def _gelu(x):
    return 0.5 * x * (1.0 + jnp.tanh(_GELU_C * (x + 0.044715 * x * x * x)))


def _dgelu(x):
    t = jnp.tanh(_GELU_C * (x + 0.044715 * x * x * x))
    return 0.5 * (1.0 + t) + 0.5 * x * (1.0 - t * t) * _GELU_C * (1.0 + 3.0 * 0.044715 * x * x)


def _split3_dot(tri, x):
    hi = x.astype(BF16)
    r1 = x - hi.astype(F32)
    mid = r1.astype(BF16)
    lo = (r1 - mid.astype(F32)).astype(BF16)
    t = tri.astype(BF16)
    d = lambda p: jnp.dot(t, p, preferred_element_type=F32)
    return d(hi) + d(mid) + d(lo)


def _lower_tri(n, strict=False):
    r = lax.broadcasted_iota(jnp.int32, (n, n), 0)
    c = lax.broadcasted_iota(jnp.int32, (n, n), 1)
    return (c < r) if strict else (c <= r)


def _mm(a, b, *, ta=False, tb=False, out_dtype=F32, res=None, scale=None, tm=None, tn=None, tk=None,
        a_off=(0, 0), b_off=(0, 0), dims=None, name):
    if dims is None:
        M, K = (a.shape[1], a.shape[0]) if ta else a.shape
        N = b.shape[0] if tb else b.shape[1]
    else:
        M, N, K = dims
    tk = tk or _tile(K, 2816)
    nk_ = K // tk
    pick_m, pick_n = tm is None, tn is None
    tm = tm or _tile(M, MM_TILE)
    tn = tn or _tile(N, MM_TILE)

    def vmem(tm_, tn_):
        a_b = tm_ * tk * a.dtype.itemsize + (tm_ * tk * 2 if a.dtype != BF16 else 0)
        b_b = tn_ * tk * b.dtype.itemsize + (tn_ * tk * 2 if b.dtype != BF16 else 0)
        o_b = tm_ * tn_ * (jnp.dtype(out_dtype).itemsize + (4 if res is not None else 0))
        return 2 * (a_b + b_b + o_b) + (tm_ * tn_ * 4 if nk_ > 1 else 0) + tm_ * tn_ * 4

    while vmem(tm, tn) > MM_VMEM and (pick_m or pick_n):
        if pick_m and (tm >= tn or not pick_n) and tm > LANES:
            tm = _tile(M, tm - LANES)
        elif pick_n and tn > LANES:
            tn = _tile(N, tn - LANES)
        else:
            break
    assert M % tm == 0 and N % tn == 0 and K % tk == 0, (name, M, N, K, tm, tn, tk)
    nk = K // tk
    ca = 0 if ta else 1
    cb = 1 if tb else 0

    def blk(rows, cols, off):
        assert off[0] % rows == 0 and off[1] % cols == 0, (name, off, rows, cols)
        return off[0] // rows, off[1] // cols

    if ta:
        ao = blk(tk, tm, a_off)
        a_spec = pl.BlockSpec((tk, tm), lambda i, j, k: (k + ao[0], i + ao[1]))
    else:
        ao = blk(tm, tk, a_off)
        a_spec = pl.BlockSpec((tm, tk), lambda i, j, k: (i + ao[0], k + ao[1]))
    if tb:
        bo = blk(tn, tk, b_off)
        b_spec = pl.BlockSpec((tn, tk), lambda i, j, k: (j + bo[0], k + bo[1]))
    else:
        bo = blk(tk, tn, b_off)
        b_spec = pl.BlockSpec((tk, tn), lambda i, j, k: (k + bo[0], j + bo[1]))
    o_spec = pl.BlockSpec((tm, tn), lambda i, j, k: (i, j))
    in_specs = [a_spec, b_spec] + ([o_spec] if res is not None else [])
    has_res = res is not None

    def kern(*refs):
        if has_res:
            a_ref, b_ref, r_ref, o_ref = refs[:4]
            scr = refs[4:]
        else:
            a_ref, b_ref, o_ref = refs[:3]
            r_ref = None
            scr = refs[3:]
        p = lax.dot_general(a_ref[...].astype(BF16), b_ref[...].astype(BF16), (((ca,), (cb,)), ((), ())),
                            preferred_element_type=F32)

        def fin(val):
            if scale is not None:
                val = val * scale
            if has_res:
                val = r_ref[...] + val
            o_ref[...] = val.astype(out_dtype)

        if nk == 1:
            fin(p)
        else:
            acc = scr[0]
            k = pl.program_id(2)

            @pl.when(k == 0)
            def _():
                acc[...] = p

            @pl.when(k > 0)
            def _():
                acc[...] += p

            @pl.when(k == nk - 1)
            def _():
                fin(acc[...])

    args = (a, b) + ((res,) if has_res else ())
    return pl.pallas_call(
        kern, name=name, grid=(M // tm, N // tn, nk), in_specs=in_specs, out_specs=o_spec,
        out_shape=jax.ShapeDtypeStruct((M, N), out_dtype),
        scratch_shapes=[pltpu.VMEM((tm, tn), F32)] if nk > 1 else [],
        compiler_params=_cparams(("parallel", "parallel", "arbitrary")),
    )(*args)


def _rows(body, tiled, full, outs, accs, *, tr, name, T):
    assert T % tr == 0
    in_specs = []
    for arr, width, off in tiled:
        assert off % width == 0, (name, off, width)
        in_specs.append(pl.BlockSpec((tr, width), functools.partial(lambda i, o: (i, o), o=off // width)))
    for arr in full:
        in_specs.append(pl.BlockSpec(arr.shape, lambda i: (0, 0)))
    out_specs = [pl.BlockSpec((tr, w), lambda i: (i, 0)) for w, _ in outs]
    out_specs += [pl.BlockSpec(s, lambda i: (0, 0)) for s, _ in accs]
    out_shape = [jax.ShapeDtypeStruct((T, w), d) for w, d in outs] + [jax.ShapeDtypeStruct(s, d) for s, d in accs]
    nt, nf, no = len(tiled), len(full), len(outs)

    def kern(*refs):
        i = pl.program_id(0)
        acc_refs = refs[nt + nf + no:]

        @pl.when(i == 0)
        def _():
            for r in acc_refs:
                r[...] = jnp.zeros(r.shape, r.dtype)

        body(i, refs[:nt], refs[nt:nt + nf], refs[nt + nf:nt + nf + no], acc_refs)

    res = pl.pallas_call(
        kern, name=name, grid=(T // tr,), in_specs=in_specs, out_specs=out_specs, out_shape=out_shape,
        compiler_params=_cparams(("arbitrary",)),
    )(*[t[0] for t in tiled], *full)
    return res


def _colsum(x):
    return jnp.sum(x, axis=0, keepdims=True)


def _norm_fwd(h, g, name):
    T, D = h.shape

    def body(i, t, f, o, a):
        x = t[0][...]
        r = lax.rsqrt(jnp.mean(x * x, axis=-1, keepdims=True) + NORM_EPS)
        o[0][...] = (x * r * f[0][...]).astype(BF16)

    return _rows(body, [(h, D, 0)], [g], [(D, BF16)], [], tr=_tile(T, 768, 8), name=name, T=T)[0]


def _norm_bwd(h, dn, dh, g, name):
    T, D = h.shape

    def body(i, t, f, o, a):
        x, dnv, dhv = t[0][...], t[1][...], t[2][...]
        r = lax.rsqrt(jnp.mean(x * x, axis=-1, keepdims=True) + NORM_EPS)
        xh = x * r
        dng = dnv * f[0][...]
        out = dhv + r * (dng - xh * jnp.mean(dng * xh, axis=-1, keepdims=True))
        o[0][...] = out
        o[1][...] = out.astype(BF16)
        a[0][...] += _colsum(dnv * xh)

    return _rows(body, [(h, D, 0), (dn, D, 0), (dh, D, 0)], [g], [(D, F32), (D, BF16)], [((1, D), F32)],
                 tr=_tile(T, 384, 16), name=name, T=T)


FFN_TM = 768


def _ffn_up_act(n, wgu_t, name):
    T, D = n.shape
    F = wgu_t.shape[0] // 2
    tm, tn = _tile(T, FFN_TM), _tile(F, MM_TILE)
    nj = F // tn

    def kern(n_ref, wg_ref, wu_ref, g_ref, u_ref, a_ref):
        nv = n_ref[...]
        g = _dot_nt(nv, wg_ref[...])
        u = _dot_nt(nv, wu_ref[...])
        g_ref[...] = g
        u_ref[...] = u
        a_ref[...] = (_silu(g) * u).astype(BF16)

    out = pl.BlockSpec((tm, tn), lambda i, j: (i, j))
    return pl.pallas_call(
        kern, name=name, grid=(T // tm, nj),
        in_specs=[pl.BlockSpec((tm, D), lambda i, j: (i, 0)), pl.BlockSpec((tn, D), lambda i, j: (j, 0)),
                  pl.BlockSpec((tn, D), lambda i, j: (nj + j, 0))],
        out_specs=[out] * 3,
        out_shape=[jax.ShapeDtypeStruct((T, F), F32), jax.ShapeDtypeStruct((T, F), F32), jax.ShapeDtypeStruct((T, F), BF16)],
        compiler_params=_cparams(("parallel", "parallel")),
    )(n, wgu_t, wgu_t)


def _ffn_down_dx_act(dhb, wd, g, u, name):
    T, D = dhb.shape
    F = wd.shape[0]
    tm, tn = _tile(T, FFN_TM), _tile(F, MM_TILE)
    nj = F // tn

    def kern(d_ref, w_ref, g_ref, u_ref, o_ref):
        da = _dot_nt(d_ref[...], w_ref[...]) * 0.5
        gv, uv = g_ref[...], u_ref[...]
        dg = (da * uv * _dsilu(gv)).astype(BF16)
        du = (da * _silu(gv)).astype(BF16)
        for jj in range(nj):

            @pl.when(pl.program_id(1) == jj)
            def _():
                o_ref[:, jj * tn:(jj + 1) * tn] = dg
                o_ref[:, F + jj * tn:F + (jj + 1) * tn] = du

    tile = pl.BlockSpec((tm, tn), lambda i, j: (i, j))
    return pl.pallas_call(
        kern, name=name, grid=(T // tm, nj),
        in_specs=[pl.BlockSpec((tm, D), lambda i, j: (i, 0)), pl.BlockSpec((tn, D), lambda i, j: (j, 0)), tile, tile],
        out_specs=pl.BlockSpec((tm, 2 * F), lambda i, j: (i, 0)), out_shape=jax.ShapeDtypeStruct((T, 2 * F), BF16),
        compiler_params=_cparams(("parallel", "arbitrary")),
    )(dhb, wd, g, u)


def _merge_fwd(proj, off, pa, pb, pc, name):
    T, D = pa.shape

    def body(i, t, f, o, a):
        o[0][...] = (_sigmoid(t[0][...]) * t[3][...] + _sigmoid(t[1][...]) * t[4][...]
                     + _sigmoid(t[2][...]) * t[5][...]).astype(BF16)

    tiled = [(proj, D, off), (proj, D, off + D), (proj, D, off + 2 * D), (pa, D, 0), (pb, D, 0), (pc, D, 0)]
    return _rows(body, tiled, [], [(D, BF16)], [], tr=_tile(T, 384, 8), name=name, T=T)[0]


def _merge_bwd(dmixed, proj, off, pa, pb, pc, name):
    T, D = pa.shape

    def body(i, t, f, o, a):
        dm = t[0][...]
        for k in range(3):
            g = _sigmoid(t[1 + k][...])
            o[k][...] = (dm * g).astype(BF16)
            o[3][:, k * D:(k + 1) * D] = (dm * t[4 + k][...] * g * (1.0 - g)).astype(BF16)

    tiled = [(dmixed, D, 0), (proj, D, off), (proj, D, off + D), (proj, D, off + 2 * D), (pa, D, 0), (pb, D, 0),
             (pc, D, 0)]
    return _rows(body, tiled, [], [(D, BF16)] * 3 + [(3 * D, BF16)], [], tr=_tile(T, 384, 8), name=name, T=T)


def _gnorm_fwd(y, proj, zoff, nw, name):
    T, D = y.shape
    gs = D // SSD_GROUPS

    def body(i, t, f, o, a):
        s = t[0][...] * _silu(t[1][...])
        for g in range(SSD_GROUPS):
            sg = s[:, g * gs:(g + 1) * gs]
            r = lax.rsqrt(jnp.mean(sg * sg, axis=-1, keepdims=True) + NORM_EPS)
            o[0][:, g * gs:(g + 1) * gs] = (sg * r * f[0][:, g * gs:(g + 1) * gs]).astype(BF16)

    return _rows(body, [(y, D, 0), (proj, D, zoff)], [nw], [(D, BF16)], [], tr=_tile(T, 384, 8), name=name, T=T)[0]


def _gnorm_bwd(dout, y, proj, zoff, nw, name):
    T, D = y.shape
    gs = D // SSD_GROUPS

    def body(i, t, f, o, a):
        dov, yv, zv = t[0][...], t[1][...], t[2][...]
        sz = _silu(zv)
        s = yv * sz
        dsz = _dsilu(zv)
        for g in range(SSD_GROUPS):
            sl = slice(g * gs, (g + 1) * gs)
            sg = s[:, sl]
            r = lax.rsqrt(jnp.mean(sg * sg, axis=-1, keepdims=True) + NORM_EPS)
            sh = sg * r
            dog = dov[:, sl]
            dng = dog * f[0][:, sl]
            ds = r * (dng - sh * jnp.mean(dng * sh, axis=-1, keepdims=True))
            o[0][:, sl] = ds * sz[:, sl]
            o[1][:, sl] = (ds * yv[:, sl] * dsz[:, sl]).astype(BF16)
            a[0][:, sl] += _colsum(dog * sh)

    return _rows(body, [(dout, D, 0), (y, D, 0), (proj, D, zoff)], [nw], [(D, F32), (D, BF16)], [((1, D), F32)],
                 tr=_tile(T, 384, 8), name=name, T=T)


def _loss_bwd(h, tgt, g, seq_len, n_real, name):
    T, D = h.shape
    tr = _tile(seq_len, 384, 8)
    per_seq = seq_len // tr

    def body(i, t, f, o, a):
        x, tg = t[0][...], t[1][...]
        pos = (i % per_seq) * tr + lax.broadcasted_iota(jnp.int32, (tr, 1), 0)
        valid = (pos >= N_META) & (pos < N_META + n_real)
        r = lax.rsqrt(jnp.mean(x * x, axis=-1, keepdims=True) + NORM_EPS)
        xh = x * r
        e = jnp.where(valid, xh * f[0][...] - tg, 0.0)
        a[0][...] += jnp.zeros((1, LANES), F32) + 0.5 * jnp.sum(jnp.sum(e * e, axis=-1, keepdims=True) / D,
                                                              axis=0, keepdims=True)
        dy = e / D
        dng = dy * f[0][...]
        out = r * (dng - xh * jnp.mean(dng * xh, axis=-1, keepdims=True))
        o[0][...] = out
        o[1][...] = out.astype(BF16)
        a[1][...] += _colsum(dy * xh)

    return _rows(body, [(h, D, 0), (tgt, D, 0)], [g], [(D, F32), (D, BF16)], [((1, LANES), F32), ((1, D), F32)], tr=tr,
                 name=name, T=T)


def _lane_is_attn(shape):
    return lax.broadcasted_iota(jnp.int32, shape, len(shape) - 1) < HEADS


def _gate_prep(proj3, col_blk, bias, avec, name):
    B, L, _ = proj3.shape
    Q = Q_BLOCK
    nc = L // Q

    def kern(x_ref, b_ref, a_ref, v_ref, c_ref, carry):
        c = pl.program_id(1)

        @pl.when(c == 0)
        def _():
            carry[...] = jnp.zeros_like(carry)

        x = x_ref[0] + b_ref[...]
        attn = _lane_is_attn(x.shape)
        v = jnp.where(attn, _log_sigmoid(x), _softplus(x))
        w = jnp.where(attn, v, v * a_ref[...])
        cs = _split3_dot(_lower_tri(Q), w) + jnp.where(attn[:1], carry[...], 0.0)
        v_ref[0] = v
        c_ref[0] = cs
        rows = lax.broadcasted_iota(jnp.int32, (Q, 1), 0)
        carry[...] = jnp.sum(jnp.where(rows == Q - 1, cs, 0.0), axis=0, keepdims=True)

    blk = pl.BlockSpec((1, Q, LANES), lambda b, c: (b, c, 0))
    vec = pl.BlockSpec((1, LANES), lambda b, c: (0, 0))
    return pl.pallas_call(
        kern, name=name, grid=(B, nc),
        in_specs=[pl.BlockSpec((1, Q, LANES), lambda b, c: (b, c, col_blk)), vec, vec],
        out_specs=[blk, blk], out_shape=[jax.ShapeDtypeStruct((B, L, LANES), F32)] * 2,
        scratch_shapes=[pltpu.VMEM((1, LANES), F32)],
        compiler_params=_cparams(("parallel", "arbitrary")),
    )(proj3, bias, avec)


def _gate_post(drow, dcol, ddt, proj3, col_blk, vals, bias, avec, name):
    B, L, _ = proj3.shape
    Q = Q_BLOCK
    nc = L // Q

    def kern(dr_ref, dc_ref, dd_ref, x_ref, v_ref, b_ref, a_ref, o_ref, db_ref, da_ref, carry):
        b = pl.program_id(0)
        c = pl.program_id(1)

        @pl.when((b == 0) & (c == 0))
        def _():
            db_ref[...] = jnp.zeros_like(db_ref)
            da_ref[...] = jnp.zeros_like(da_ref)

        @pl.when(c == 0)
        def _():
            carry[...] = jnp.zeros_like(carry)

        x = x_ref[0] + b_ref[...]
        attn = _lane_is_attn(x.shape)
        dcs = dr_ref[0] + dc_ref[0]
        upper = jnp.logical_not(_lower_tri(Q, strict=True))
        rc = _split3_dot(upper, dcs) + jnp.where(attn[:1], carry[...], 0.0)
        rows = lax.broadcasted_iota(jnp.int32, (Q, 1), 0)
        carry[...] = jnp.sum(jnp.where(rows == 0, rc, 0.0), axis=0, keepdims=True)
        dv = jnp.where(attn, rc, dd_ref[0] + rc * a_ref[...])
        dpre = dv * jnp.where(attn, _sigmoid(-x), _sigmoid(x))
        o_ref[0] = dpre.astype(BF16)
        db_ref[...] += _colsum(dpre)
        da_ref[...] += _colsum(jnp.where(attn, 0.0, rc * v_ref[0]))

    rev = pl.BlockSpec((1, Q, LANES), lambda b, c: (b, nc - 1 - c, 0))
    vec = pl.BlockSpec((1, LANES), lambda b, c: (0, 0))
    return pl.pallas_call(
        kern, name=name, grid=(B, nc),
        in_specs=[rev, rev, rev, pl.BlockSpec((1, Q, LANES), lambda b, c: (b, nc - 1 - c, col_blk)), rev, vec, vec],
        out_specs=[rev, vec, vec],
        out_shape=[jax.ShapeDtypeStruct((B, L, LANES), BF16), jax.ShapeDtypeStruct((1, LANES), F32),
                   jax.ShapeDtypeStruct((1, LANES), F32)],
        scratch_shapes=[pltpu.VMEM((1, LANES), F32)],
        compiler_params=_cparams(("arbitrary", "arbitrary")),
    )(drow, dcol, ddt, proj3, vals, bias, avec)


def _lane_col(tile, lane):
    sel = lax.broadcasted_iota(jnp.int32, tile.shape, 1) == lane
    return jnp.sum(jnp.where(sel, tile, 0.0), axis=1, keepdims=True)


AUG = LANES
AUG_A = HEAD_DIM
AUG_B = HEAD_DIM + 3


def _split3(x):
    hi = x.astype(BF16).astype(F32)
    mid = (x - hi).astype(BF16).astype(F32)
    lo = (x - hi - mid).astype(BF16).astype(F32)
    return hi, mid, lo


def _put3(base, lane, first, x):
    hi, mid, lo = _split3(x)
    return jnp.where(lane == first, hi, jnp.where(lane == first + 1, mid, jnp.where(lane == first + 2, lo, base)))


HP = 2
AH = 2
AW = AH * HEAD_DIM


def _other_half(x):
    return pltpu.roll(x, HEAD_DIM, 1)


def _loop_by_twos(n, step, init):
    carry = lax.fori_loop(0, n // 4, lambda t, c: step(4 * t + 3, step(4 * t + 2, step(4 * t + 1, step(4 * t, c)))), init)
    carry = lax.cond(n % 4 >= 2, lambda c: step(n // 4 * 4 + 1, step(n // 4 * 4, c)), lambda c: c, carry)
    return lax.cond(n % 2 == 1, lambda c: step(n - 1, c), lambda c: c, carry)


def _attn_pack(proj3, cums, name):
    B, L, _ = proj3.shape
    D = HEADS * HEAD_DIM
    nh = HEADS // HP
    tr = _tile(L, 384)
    scale = HEAD_DIM ** -0.5

    def kern(q_ref, k_ref, v_ref, c_ref, qa_ref, ka_ref, va_ref):
        lane = lax.broadcasted_iota(jnp.int32, (tr, AUG), 1)
        head = lane < HEAD_DIM
        ones_a = jnp.where((lane >= AUG_A) & (lane < AUG_A + 3), 1.0, 0.0)
        ones_b = jnp.where((lane >= AUG_B) & (lane < AUG_B + 3), 1.0, 0.0)
        ct = c_ref[0]
        for hp in range(nh):
            cols = slice(hp * LANES, (hp + 1) * LANES)
            for hh in range(HP):
                h = HP * hp + hh
                c = _lane_col(ct, h)
                sel = (lambda t: t) if hh == 0 else _other_half
                qa_ref[0, h] = jnp.where(head, sel(q_ref[0, :, cols]) * scale, _put3(ones_b, lane, AUG_A, c)).astype(BF16)
                ka_ref[0, h] = jnp.where(head, sel(k_ref[0, :, cols]), _put3(ones_a, lane, AUG_B, -c)).astype(BF16)
                va_ref[0, h] = jnp.where(head, sel(v_ref[0, :, cols]), ones_a).astype(BF16)

    def win(k):
        return pl.BlockSpec((1, tr, D), lambda b, i: (b, i, k))

    out = pl.BlockSpec((1, HEADS, tr, AUG), lambda b, i: (b, 0, i, 0))
    return pl.pallas_call(
        kern, name=name, grid=(B, L // tr),
        in_specs=[win(0), win(1), win(2), pl.BlockSpec((1, tr, LANES), lambda b, i: (b, i, 0))],
        out_specs=[out] * 3, out_shape=[jax.ShapeDtypeStruct((B, HEADS, L, AUG), BF16)] * 3,
        compiler_params=_cparams(("parallel", "parallel")),
    )(proj3, proj3, proj3, cums)


def _attn_fwd(qa, ka, va, name, gather=None):
    B, H, L, _ = qa.shape
    tq = _tile(L, 384)
    nq = L // tq
    nh = H // AH
    comm = gather is not None

    def kern(*refs):
        if comm:
            q_ref, k_ref, v_ref, x_ref, y_ref, yb_ref, l_ref, g_ref, send_sems, recv_sems, local_sem = refs
        else:
            q_ref, k_ref, v_ref, y_ref, yb_ref, l_ref = refs
        qi = pl.program_id(2)
        if comm:
            _ride((pl.program_id(0) * nh + pl.program_id(1)) * nq + qi, B * nh * nq,
                  _gather_phases(x_ref, g_ref, send_sems, recv_sems, local_sem))
        qs = [q_ref[0, hh] for hh in range(AH)]
        causal = _lower_tri(tq)

        def step(j, carry, masked):
            rows = pl.ds(pl.multiple_of(j * tq, tq), tq)
            out = []
            for hh in range(AH):
                m, acc = carry[hh]
                s = _dot_nt(qs[hh], k_ref[0, hh, rows, :])
                if masked:
                    s = jnp.where(causal, s, NEG)
                m_new = jnp.maximum(m, jnp.max(s, axis=1, keepdims=True))
                p = jnp.exp(s - m_new)
                out.append((m_new, jnp.exp(m - m_new) * acc + _dot(p.astype(BF16), v_ref[0, hh, rows, :])))
            return tuple(out)

        init = tuple((jnp.full((tq, 1), NEG, F32), jnp.zeros((tq, AUG), F32)) for _ in range(AH))
        carry = _loop_by_twos(qi, lambda j, c: step(j, c, False), init)
        outs = []
        for hh, (m, acc) in enumerate(step(qi, carry, True)):
            l = _lane_col(acc, AUG_A)
            outs.append(acc / l)
            l_ref[0, hh] = m + jnp.log(l)
        head = lax.broadcasted_iota(jnp.int32, (tq, AUG), 1) < HEAD_DIM
        for pp in range(AH // HP):
            y = jnp.where(head, outs[HP * pp], _other_half(outs[HP * pp + 1]))
            y_ref[0, :, pp * LANES:(pp + 1) * LANES] = y
            yb_ref[0, :, pp * LANES:(pp + 1) * LANES] = y.astype(BF16)

    qspec = pl.BlockSpec((1, AH, tq, AUG), lambda b, h, i: (b, h, i, 0))
    kvspec = pl.BlockSpec((1, AH, L, AUG), lambda b, h, i: (b, h, 0, 0))
    lspec = pl.BlockSpec((1, AH, tq, 1), lambda b, h, i: (b, h, i, 0))
    yspec = pl.BlockSpec((1, tq, AW), lambda b, h, i: (b, i, h))
    out_shape = [jax.ShapeDtypeStruct((B, L, H * HEAD_DIM), F32), jax.ShapeDtypeStruct((B, L, H * HEAD_DIM), BF16),
                 jax.ShapeDtypeStruct((B, H, L, 1), F32)]
    if comm:
        out_shape.append(jax.ShapeDtypeStruct((N_DEV,) + gather.shape, gather.dtype))
    return pl.pallas_call(
        kern, name=name, grid=(B, nh, nq), in_specs=[qspec, kvspec, kvspec] + ([ANY] if comm else []),
        out_specs=[yspec, yspec, lspec] + ([ANY] if comm else []), out_shape=out_shape,
        scratch_shapes=COMM_SCRATCH if comm else [],
        compiler_params=_cparams(("arbitrary",) * 3 if comm else ("parallel", "parallel", "arbitrary")),
    )(qa, ka, va, *([gather] if comm else []))


def _attn_bwd(qa, ka, va, y, dy, lse, name, parts=None):
    B, H, L, _ = qa.shape
    tq = _tile(L, 384)
    nq = L // tq
    nh = H // AH
    comm = parts is not None
    scale = HEAD_DIM ** -0.5

    def kern(*refs):
        if comm:
            (q_ref, k_ref, v_ref, y_ref, dy_ref, l_ref, p_ref, dq_ref, dk_ref, dv_ref, dc_ref, r_ref,
             dk_acc, dv_acc, send_sems, recv_sems, local_sem) = refs
        else:
            q_ref, k_ref, v_ref, y_ref, dy_ref, l_ref, dq_ref, dk_ref, dv_ref, dc_ref, dk_acc, dv_acc = refs
        qi = pl.program_id(2)
        hp = pl.program_id(1)
        lane_row = lax.broadcasted_iota(jnp.int32, (1, LANES), 1)
        onehot = [(lane_row == AH * hp + hh).astype(F32) for hh in range(AH)]

        @pl.when((hp == 0) & (qi == 0))
        def _():
            dc_ref[...] = jnp.zeros_like(dc_ref)

        if comm:
            _ride((pl.program_id(0) * nh + pl.program_id(1)) * nq + qi, B * nh * nq,
                  _exchange_phases(p_ref, r_ref, send_sems, recv_sems, local_sem))

        @pl.when(qi == 0)
        def _():
            dk_acc[...] = jnp.zeros_like(dk_acc)
            dv_acc[...] = jnp.zeros_like(dv_acc)

        lane = lax.broadcasted_iota(jnp.int32, (tq, AUG), 1)
        head = lane < HEAD_DIM
        qbs, dobs = [], []
        for hh in range(AH):
            sel = (lambda t: t) if hh % HP == 0 else _other_half
            cols = slice((hh // HP) * LANES, (hh // HP + 1) * LANES)
            qf = q_ref[0, hh].astype(F32)
            dov = jnp.where(head, sel(dy_ref[0, :, cols]), 0.0)
            dsum = jnp.sum(dov * sel(y_ref[0, :, cols]), axis=1, keepdims=True)
            dobs.append(_put3(dov, lane, AUG_A, -dsum).astype(BF16))
            c_t = jnp.sum(jnp.where((lane >= AUG_A) & (lane < AUG_A + 3), qf, 0.0), axis=1, keepdims=True)
            qbs.append(_put3(qf, lane, AUG_A, c_t - l_ref[0, hh]).astype(BF16))
        causal = _lower_tri(tq)

        def step(j, dqs, masked):
            rows = pl.ds(pl.multiple_of(j * tq, tq), tq)
            out = []
            for hh in range(AH):
                kj = k_ref[0, hh, rows, :]
                s = _dot_nt(qbs[hh], kj)
                if masked:
                    s = jnp.where(causal, s, NEG)
                p = jnp.exp(s)
                ds = (p * _dot_nt(dobs[hh], v_ref[0, hh, rows, :])).astype(BF16)
                dv_acc[hh, rows, :] += _dot_tn(p.astype(BF16), dobs[hh])
                dk_acc[hh, rows, :] += _dot_tn(ds, qbs[hh])
                out.append(dqs[hh] + _dot(ds, kj))
            return tuple(out)

        dqs = _loop_by_twos(qi, lambda j, c: step(j, c, False), tuple(jnp.zeros((tq, AUG), F32) for _ in range(AH)))
        dqs = step(qi, dqs, True)
        dc_ref[0, pl.ds(pl.multiple_of(qi * tq, tq), tq), :] += sum(_lane_col(dqs[hh], AUG_A) * onehot[hh]
                                                                    for hh in range(AH))
        for pp in range(AH // HP):
            dq_ref[0, :, pp * LANES:(pp + 1) * LANES] = (
                jnp.where(head, dqs[HP * pp], _other_half(dqs[HP * pp + 1])) * scale).astype(BF16)

        @pl.when(qi == nq - 1)
        def _():
            full = lax.broadcasted_iota(jnp.int32, (L, AUG), 1) < HEAD_DIM
            for pp in range(AH // HP):
                cols = slice(pp * LANES, (pp + 1) * LANES)
                dk_ref[0, :, cols] = jnp.where(full, dk_acc[HP * pp], _other_half(dk_acc[HP * pp + 1])).astype(BF16)
                dv_ref[0, :, cols] = jnp.where(full, dv_acc[HP * pp], _other_half(dv_acc[HP * pp + 1])).astype(BF16)
            dc_ref[0] -= sum(_lane_col(dk_acc[hh], AUG_B) * onehot[hh] for hh in range(AH))

    qspec = pl.BlockSpec((1, AH, tq, AUG), lambda b, h, i: (b, h, i, 0))
    kvspec = pl.BlockSpec((1, AH, L, AUG), lambda b, h, i: (b, h, 0, 0))
    lspec = pl.BlockSpec((1, AH, tq, 1), lambda b, h, i: (b, h, i, 0))
    tmspec = pl.BlockSpec((1, L, LANES), lambda b, h, i: (b, 0, 0))
    yspec = pl.BlockSpec((1, tq, AW), lambda b, h, i: (b, i, h))
    yfull = pl.BlockSpec((1, L, AW), lambda b, h, i: (b, 0, h))
    nat = jax.ShapeDtypeStruct((B, L, H * HEAD_DIM), BF16)
    out_shape = [nat, nat, nat, jax.ShapeDtypeStruct((B, L, LANES), F32)]
    if comm:
        out_shape.append(jax.ShapeDtypeStruct(parts.shape, parts.dtype))
    return pl.pallas_call(
        kern, name=name, grid=(B, nh, nq),
        in_specs=[qspec, kvspec, kvspec, yspec, yspec, lspec] + ([ANY] if comm else []),
        out_specs=[yspec, yfull, yfull, tmspec] + ([ANY] if comm else []), out_shape=out_shape,
        scratch_shapes=[pltpu.VMEM((AH, L, AUG), F32), pltpu.VMEM((AH, L, AUG), F32)] + (COMM_SCRATCH if comm else []),
        compiler_params=_cparams(("parallel", "arbitrary", "arbitrary")),
    )(qa, ka, va, y, dy, lse, *([parts] if comm else []))


PAD = SUBLANES


def _halo_tile(x_ref, i, TR):
    r0 = pl.multiple_of(i * TR, TR)
    before = x_ref[0, pl.ds(pl.multiple_of(jnp.maximum(r0 - PAD, 0), PAD), PAD), :]
    return jnp.concatenate([jnp.where(i > 0, before, 0.0), x_ref[0, pl.ds(r0, TR), :]], axis=0)


def _conv_fwd(x3, x_blk, w, b, n_silu, name):
    B, L, _ = x3.shape
    C = w.shape[1]
    TR = _tile(L, 384, 8)

    def kern(x_ref, w_ref, b_ref, o_ref):
        cb = pl.program_id(1)

        def body(i, carry):
            r0 = pl.multiple_of(i * TR, TR)
            ext = _halo_tile(x_ref, i, TR)
            acc = jnp.zeros((TR, LANES), F32) + b_ref[...]
            for k in range(CONV_K):
                s = CONV_K - 1 - k
                sh = ext if s == 0 else pltpu.roll(ext, s, 0)
                acc = acc + w_ref[k:k + 1, :] * sh[PAD:PAD + TR]
            o_ref[0, pl.ds(r0, TR), :] = jnp.where(cb < n_silu, _silu(acc), acc)
            return carry

        lax.fori_loop(0, L // TR, body, 0)

    return pl.pallas_call(
        kern, name=name, grid=(B, C // LANES),
        in_specs=[pl.BlockSpec((1, L, LANES), lambda b_, c: (b_, 0, x_blk + c)),
                  pl.BlockSpec((CONV_K, LANES), lambda b_, c: (0, c)), pl.BlockSpec((1, LANES), lambda b_, c: (0, c))],
        out_specs=pl.BlockSpec((1, L, LANES), lambda b_, c: (b_, 0, c)),
        out_shape=jax.ShapeDtypeStruct((B, L, C), F32),
        compiler_params=_cparams(("parallel", "parallel")),
    )(x3, w, b)


def _conv_bwd(x3, x_blk, dxs, dBg, dCg, dxc, w, b, n_silu, name):
    B, L, D = dxs.shape
    G = dBg.shape[1]
    C = w.shape[1]
    nx = D // LANES
    TR = _tile(L, 384, 16)

    def kern(x_ref, s_ref, bg_ref, cg_ref, xc_ref, w_ref, b_ref, dx_ref, dw_ref, dp_s):
        cb = pl.program_id(1)

        def pre(i, carry):
            r0 = pl.multiple_of(i * TR, TR)
            ext = _halo_tile(x_ref, i, TR)
            taps = []
            acc = jnp.zeros((TR, LANES), F32) + b_ref[...]
            for k in range(CONV_K):
                s = CONV_K - 1 - k
                sh = ext if s == 0 else pltpu.roll(ext, s, 0)
                taps.append(sh[PAD:PAD + TR])
                acc = acc + w_ref[k:k + 1, :] * taps[-1]
            rows = pl.ds(r0, TR)
            dv = jnp.where(cb < nx, s_ref[0, rows, :],
                           jnp.where(cb < nx + G, bg_ref[0, 0, rows, :],
                                     jnp.where(cb < nx + 2 * G, cg_ref[0, 0, rows, :], xc_ref[0, rows, :])))
            dpre = jnp.where(cb < n_silu, dv * _dsilu(acc), dv)
            dp_s[rows, :] = dpre
            return tuple(c + _colsum(dpre * t) for c, t in zip(carry[:CONV_K], taps)) + (carry[CONV_K] + _colsum(dpre),)

        z = jnp.zeros((1, LANES), F32)
        sums = lax.fori_loop(0, L // TR, pre, (z,) * (CONV_K + 1))
        dp_s[pl.ds(L, PAD), :] = jnp.zeros((PAD, LANES), F32)
        dw_ref[0] = jnp.zeros((SUBLANES, LANES), F32)
        for k in range(CONV_K + 1):
            dw_ref[0, k:k + 1, :] = sums[k]

        def back(i, carry):
            r0 = pl.multiple_of(i * TR, TR)
            ext = dp_s[pl.ds(r0, TR + PAD), :]
            acc = jnp.zeros((TR, LANES), F32)
            for k in range(CONV_K):
                s = CONV_K - 1 - k
                sh = ext if s == 0 else pltpu.roll(ext, TR + PAD - s, 0)
                acc = acc + w_ref[k:k + 1, :] * sh[0:TR]
            dx_ref[0, pl.ds(r0, TR), :] = acc.astype(BF16)
            return carry

        lax.fori_loop(0, L // TR, back, 0)

    seq = pl.BlockSpec((1, L, LANES), lambda b_, c: (b_, 0, jnp.minimum(c, nx - 1)))
    grp = lambda first: pl.BlockSpec((1, 1, L, LANES), lambda b_, c: (b_, jnp.clip(c - first, 0, G - 1), 0, 0))
    tail = pl.BlockSpec((1, L, LANES), lambda b_, c: (b_, 0, jnp.clip(c - nx - 2 * G, 0, nx - 1)))
    return pl.pallas_call(
        kern, name=name, grid=(B, C // LANES),
        in_specs=[pl.BlockSpec((1, L, LANES), lambda b_, c: (b_, 0, x_blk + c)), seq, grp(nx), grp(nx + G), tail,
                  pl.BlockSpec((CONV_K, LANES), lambda b_, c: (0, c)), pl.BlockSpec((1, LANES), lambda b_, c: (0, c))],
        out_specs=[pl.BlockSpec((1, L, LANES), lambda b_, c: (b_, 0, c)),
                   pl.BlockSpec((1, SUBLANES, LANES), lambda b_, c: (b_, 0, c))],
        out_shape=[jax.ShapeDtypeStruct((B, L, C), BF16), jax.ShapeDtypeStruct((B, SUBLANES, C), F32)],
        scratch_shapes=[pltpu.VMEM((L + PAD, LANES), F32)],
        compiler_params=_cparams(("parallel", "parallel")),
    )(x3, dxs, dBg, dCg, dxc, w, b)


def _dot_nt(a, b):
    return lax.dot_general(a, b, (((1,), (1,)), ((), ())), preferred_element_type=F32)


def _dot_tn(a, b):
    return lax.dot_general(a, b, (((0,), (0,)), ((), ())), preferred_element_type=F32)


def _dot(a, b):
    return jnp.dot(a, b, preferred_element_type=F32)


def _ssd_specs(L, nc, b_blk, c_blk):
    pairs_per_group = HEADS // SSD_GROUPS // HP
    return [
        pl.BlockSpec((1, L, LANES), lambda b, h: (b, 0, h)),
        pl.BlockSpec((1, L, SSD_STATE), lambda b, h: (b, 0, b_blk + h // pairs_per_group)),
        pl.BlockSpec((1, L, SSD_STATE), lambda b, h: (b, 0, c_blk + h // pairs_per_group)),
        pl.BlockSpec((1, L, LANES), lambda b, h: (b, 0, 0)),
        pl.BlockSpec((1, L, LANES), lambda b, h: (b, 0, 0)),
        pl.BlockSpec((1, HP, nc, Q_BLOCK), lambda b, h: (b, HEADS // HP + h, 0, 0)),
        pl.BlockSpec((1, LANES), lambda b, h: (0, 0)),
    ]


def _halves(a, b, shape):
    return jnp.where(lax.broadcasted_iota(jnp.int32, shape, 1) < HEAD_DIM, a, b)


def _half_sums(t):
    first = lax.broadcasted_iota(jnp.int32, t.shape, 1) < HEAD_DIM
    lo = jnp.sum(jnp.where(first, t, 0.0), axis=1, keepdims=True)
    return lo, jnp.sum(t, axis=1, keepdims=True) - lo


def _ssd_chunk(c, S, x_ref, b_ref, c_ref, v_ref, cu_ref, ct_ref, lane0):
    Q = Q_BLOCK
    rows = pl.ds(pl.multiple_of(c * Q, Q), Q)
    x = x_ref[0, rows, :]
    Bb = b_ref[0, rows, :].astype(BF16)
    Cb = c_ref[0, rows, :].astype(BF16)
    vt, ct = v_ref[0, rows, :], cu_ref[0, rows, :]
    tri = _lower_tri(Q)
    A, Lm, e_end_h, eAend_h, dts = [], [], [], [], []
    for hh in range(HP):
        dts.append(_lane_col(vt, lane0 + hh))
        A.append(_lane_col(ct, lane0 + hh))
        Ar = ct_ref[0, hh, pl.ds(c, 1), :]
        Aend = _lane_col(Ar, Q - 1)
        Lm.append(jnp.exp(jnp.where(tri, A[hh] - Ar, NEG)))
        e_end_h.append(jnp.exp(Aend - A[hh]))
        eAend_h.append(jnp.exp(Aend))
    shape = (Q, LANES)
    dt = _halves(dts[0], dts[1], shape)
    eA = _halves(jnp.exp(A[0]), jnp.exp(A[1]), shape)
    e_end = _halves(e_end_h[0], e_end_h[1], shape)
    xdt = x * dt
    CB = _dot_nt(Cb, Bb)
    W = xdt * e_end
    srow = lax.broadcasted_iota(jnp.int32, (HP * HEAD_DIM, 1), 0) < HEAD_DIM
    eAend = jnp.where(srow, eAend_h[0], eAend_h[1])
    S_new = S * eAend + _dot_tn(W.astype(BF16), Bb)
    return dict(rows=rows, x=x, Bb=Bb, Cb=Cb, dt=dt, eA=eA, e_end=e_end, e_end_h=e_end_h, eAend=eAend,
                eAend_h=eAend_h, xdt=xdt, Lm=Lm, CB=CB, W=W, S_new=S_new)


def _ssd_fwd(u, b_blk, c_blk, vals, cums, cums_t, dvec, name):
    B, L, _ = u.shape
    nc = L // Q_BLOCK
    nh = HEADS // HP
    PP = HP * HEAD_DIM

    def kern(x_ref, b_ref, c_ref, v_ref, cu_ref, ct_ref, d_ref, y_ref, st_ref):
        lane0 = HEADS + HP * pl.program_id(1)
        dskip = _halves(_lane_col(d_ref[...], lane0), _lane_col(d_ref[...], lane0 + 1), (1, LANES))
        first = lax.broadcasted_iota(jnp.int32, (Q_BLOCK, LANES), 1) < HEAD_DIM

        def body(c, S):
            st_ref[0, 0, c] = S
            q = _ssd_chunk(c, S, x_ref, b_ref, c_ref, v_ref, cu_ref, ct_ref, lane0)
            xb = q["xdt"].astype(BF16)
            yd = jnp.where(first, _dot((q["CB"] * q["Lm"][0]).astype(BF16), xb),
                           _dot((q["CB"] * q["Lm"][1]).astype(BF16), xb))
            z = _dot_nt(q["Cb"], S.astype(BF16))
            y_ref[0, q["rows"], :] = yd + z * q["eA"] + dskip * q["x"]
            return q["S_new"]

        lax.fori_loop(0, nc, body, jnp.zeros((HP * HEAD_DIM, SSD_STATE), F32))

    return pl.pallas_call(
        kern, name=name, grid=(B, nh), in_specs=_ssd_specs(L, nc, b_blk, c_blk),
        out_specs=[pl.BlockSpec((1, L, LANES), lambda b, h: (b, 0, h)),
                   pl.BlockSpec((1, 1, nc, PP, SSD_STATE), lambda b, h: (b, h, 0, 0, 0))],
        out_shape=[jax.ShapeDtypeStruct((B, L, HEADS * HEAD_DIM), F32),
                   jax.ShapeDtypeStruct((B, nh, nc, PP, SSD_STATE), F32)],
        compiler_params=_cparams(("parallel", "arbitrary")),
    )(u, u, u, vals, cums, cums_t, dvec)


def _ssd_bwd(u, b_blk, c_blk, vals, cums, cums_t, dvec, dy, states, name):
    B, L, _ = u.shape
    Q = Q_BLOCK
    nc = L // Q
    N = SSD_STATE
    nh = HEADS // HP
    pairs_per_group = HEADS // SSD_GROUPS // HP
    PP = HP * HEAD_DIM

    def kern(x_ref, b_ref, c_ref, v_ref, cu_ref, ct_ref, d_ref, dy_ref, st_ref,
             dx_ref, dB_ref, dC_ref, ddt_ref, dAc_ref, dAr_ref, dD_ref):
        b = pl.program_id(0)
        h = pl.program_id(1)
        lane0 = HEADS + HP * h
        dskip = _halves(_lane_col(d_ref[...], lane0), _lane_col(d_ref[...], lane0 + 1), (1, LANES))
        lane_row = lax.broadcasted_iota(jnp.int32, (1, LANES), 1)
        onehot = [(lane_row == lane0 + hh).astype(F32) for hh in range(HP)]

        @pl.when(h % pairs_per_group == 0)
        def _():
            dB_ref[...] = jnp.zeros_like(dB_ref)
            dC_ref[...] = jnp.zeros_like(dC_ref)

        @pl.when(h == 0)
        def _():
            ddt_ref[...] = jnp.zeros_like(ddt_ref)
            dAc_ref[...] = jnp.zeros_like(dAc_ref)

        @pl.when((b == 0) & (h == 0))
        def _():
            dD_ref[...] = jnp.zeros_like(dD_ref)

        last_row = lax.broadcasted_iota(jnp.int32, (Q, 1), 0) == Q - 1
        first = lax.broadcasted_iota(jnp.int32, (Q, LANES), 1) < HEAD_DIM
        srow = lax.broadcasted_iota(jnp.int32, (PP, 1), 0) < HEAD_DIM

        def bwd(i, carry):
            dS, dD = carry
            c = nc - 1 - i
            S = st_ref[0, 0, c]
            q = _ssd_chunk(c, S, x_ref, b_ref, c_ref, v_ref, cu_ref, ct_ref, lane0)
            rows, x, Bb, Cb, xdt, Lm, CB = q["rows"], q["x"], q["Bb"], q["Cb"], q["xdt"], q["Lm"], q["CB"]
            dy = dy_ref[0, rows, :]
            dyb = dy.astype(BF16)
            xb = xdt.astype(BF16)
            Sb = S.astype(BF16)
            dD = dD + _colsum(dy * x)
            dyh = [jnp.where(first, dy, 0.0).astype(BF16), jnp.where(first, 0.0, dy).astype(BF16)]
            dM = [_dot_nt(dyh[hh], xb) for hh in range(HP)]
            dxdt = jnp.where(first, _dot_tn((CB * Lm[0]).astype(BF16), dyb), _dot_tn((CB * Lm[1]).astype(BF16), dyb))
            dCBb = (dM[0] * Lm[0] + dM[1] * Lm[1]).astype(BF16)
            dAc, dAr = [], []
            for hh in range(HP):
                G = dM[hh] * CB * Lm[hh]
                dAc.append(jnp.sum(G, axis=1, keepdims=True))
                dAr.append(-jnp.sum(G, axis=0, keepdims=True))
            dC = _dot(dCBb, Bb)
            dBm = _dot_tn(dCBb, Cb)
            z = _dot_nt(Cb, Sb)
            zs = _half_sums(dy * z)
            dzb = (dy * q["eA"]).astype(BF16)
            dC = dC + _dot(dzb, Sb)
            dS_in = _dot_tn(dzb, Cb)
            dSb = dS.astype(BF16)
            dW = _dot_nt(Bb, dSb)
            dBm = dBm + _dot(q["W"].astype(BF16), dSb)
            dxdt = dxdt + dW * q["e_end"]
            des = _half_sums(dW * xdt)
            ss = jnp.sum(dS * S, axis=1, keepdims=True)
            ss_lo = jnp.sum(jnp.where(srow, ss, 0.0), axis=0, keepdims=True)
            ss_h = [ss_lo, jnp.sum(ss, axis=0, keepdims=True) - ss_lo]
            ddts = _half_sums(dxdt * x)
            eA_h = [_lane_col(q["eA"], 0), _lane_col(q["eA"], HEAD_DIM)]
            dAc_tile = jnp.zeros((Q, LANES), F32)
            ddt_tile = jnp.zeros((Q, LANES), F32)
            for hh in range(HP):
                de = des[hh] * q["e_end_h"][hh]
                dAend = ss_h[hh] * q["eAend_h"][hh] + jnp.sum(de, axis=0, keepdims=True)
                col = dAc[hh] + zs[hh] * eA_h[hh] - de + jnp.where(last_row, dAend, 0.0)
                dAc_tile = dAc_tile + col * onehot[hh]
                ddt_tile = ddt_tile + ddts[hh] * onehot[hh]
                dAr_ref[0, hh, pl.ds(c, 1), :] = dAr[hh]
            dx_ref[0, rows, :] = dskip * dy + dxdt * q["dt"]
            dB_ref[0, 0, rows, :] += dBm
            dC_ref[0, 0, rows, :] += dC
            ddt_ref[0, rows, :] += ddt_tile
            dAc_ref[0, rows, :] += dAc_tile
            return dS * q["eAend"] + dS_in, dD

        _, dD = lax.fori_loop(0, nc, bwd, (jnp.zeros((PP, N), F32), jnp.zeros((1, LANES), F32)))
        dlo, dhi = _half_sums(dD)
        dD_ref[...] += dlo * onehot[0] + dhi * onehot[1]

    tm = pl.BlockSpec((1, L, LANES), lambda b, h: (b, 0, 0))
    grp = pl.BlockSpec((1, 1, L, N), lambda b, h: (b, h // pairs_per_group, 0, 0))
    xs = pl.BlockSpec((1, L, LANES), lambda b, h: (b, 0, h))
    return pl.pallas_call(
        kern, name=name, grid=(B, nh),
        in_specs=_ssd_specs(L, nc, b_blk, c_blk) + [xs, pl.BlockSpec((1, 1, nc, PP, N), lambda b, h: (b, h, 0, 0, 0))],
        out_specs=[xs, grp, grp, tm, tm, pl.BlockSpec((1, HP, nc, Q), lambda b, h: (b, h, 0, 0)),
                   pl.BlockSpec((1, LANES), lambda b, h: (0, 0))],
        out_shape=[jax.ShapeDtypeStruct((B, L, HEADS * HEAD_DIM), F32), jax.ShapeDtypeStruct((B, SSD_GROUPS, L, N), F32),
                   jax.ShapeDtypeStruct((B, SSD_GROUPS, L, N), F32), jax.ShapeDtypeStruct((B, L, LANES), F32),
                   jax.ShapeDtypeStruct((B, L, LANES), F32), jax.ShapeDtypeStruct((B, HEADS, nc, Q), F32),
                   jax.ShapeDtypeStruct((1, LANES), F32)],
        compiler_params=_cparams(("arbitrary", "arbitrary")),
    )(u, u, u, vals, cums, cums_t, dvec, dy, states)


LRU_TR = 384
LRU_CB = 512


def _lru_gates(xc, ra, ix, p_ref, first):
    r = _sigmoid(ra + p_ref[0:1, :])
    i = _sigmoid(ix + p_ref[1:2, :])
    ls = _log_sigmoid(p_ref[2:3, :])
    log_a = LRU_C * r * ls
    a = jnp.exp(log_a)
    mult0 = jnp.sqrt(_one_minus_exp(2.0 * log_a))
    mult = jnp.where(first, 1.0, mult0)
    return r, i, ls, a, mult0, mult


def _lru_fwd(u, xc_off, ra, ix, proj3, gate_off, pvec, name):
    B, L, D = ra.shape
    TR, CB = _tile(L, LRU_TR, 8), LRU_CB
    nrt = L // TR

    def kern(xc_ref, ra_ref, ix_ref, g_ref, p_ref, y_ref, hs_ref, a_ref, pa_s, pu_s, carry):
        rt = pl.program_id(2)

        @pl.when(rt == 0)
        def _():
            carry[...] = jnp.zeros_like(carry)

        row = lax.broadcasted_iota(jnp.int32, (TR, 1), 0)
        first = (rt == 0) & (row == 0)
        xc = xc_ref[0]
        r, i, ls, a, mult0, mult = _lru_gates(xc, ra_ref[0], ix_ref[0], p_ref, first)
        a_ref[0] = a
        pa, pu = a, mult * (i * xc)
        sub = row % SUBLANES
        for s in (1, 2, 4):
            ok = sub >= s
            pu = jnp.where(ok, pa * pltpu.roll(pu, s, 0) + pu, pu)
            pa = jnp.where(ok, pa * pltpu.roll(pa, s, 0), pa)
        pa_s[...] = pa
        pu_s[...] = pu
        row8 = lax.broadcasted_iota(jnp.int32, (SUBLANES, 1), 0)

        def gbody(g, c):
            r8 = pl.ds(pl.multiple_of(g * SUBLANES, SUBLANES), SUBLANES)
            hg = pa_s[r8, :] * c + pu_s[r8, :]
            hs_ref[0, r8, :] = hg
            return jnp.sum(jnp.where(row8 == SUBLANES - 1, hg, 0.0), axis=0, keepdims=True)

        carry[...] = lax.fori_loop(0, TR // SUBLANES, gbody, carry[...])
        y_ref[0] = (hs_ref[0] * _gelu(g_ref[0])).astype(BF16)

    def win(off):
        assert off % CB == 0
        return pl.BlockSpec((1, TR, CB), functools.partial(lambda b, j, t, o: (b, t, j + o), o=off // CB))

    return pl.pallas_call(
        kern, name=name, grid=(B, D // CB, nrt),
        in_specs=[win(xc_off), win(0), win(0), win(gate_off), pl.BlockSpec((SUBLANES, CB), lambda b, j, t: (0, j))],
        out_specs=[win(0)] * 3,
        out_shape=[jax.ShapeDtypeStruct((B, L, D), BF16), jax.ShapeDtypeStruct((B, L, D), F32),
                   jax.ShapeDtypeStruct((B, L, D), F32)],
        scratch_shapes=[pltpu.VMEM((TR, CB), F32), pltpu.VMEM((TR, CB), F32), pltpu.VMEM((1, CB), F32)],
        compiler_params=_cparams(("parallel", "parallel", "arbitrary")),
    )(u, ra, ix, proj3, pvec)


def _lru_bwd(dy, proj3, gate_off, hs, a, u, xc_off, ra, ix, pvec, name):
    B, L, D = ra.shape
    TR, CB = _tile(L, LRU_TR, 8), LRU_CB
    nrt = L // TR

    def kern(dy_ref, g_ref, hs_ref, hsp_ref, a_ref, an_ref, xc_ref, ra_ref, ix_ref, p_ref,
             dg_ref, dra_ref, dix_ref, dxc_ref, dp_ref, pb_s, pd_s, g_s, carry):
        b = pl.program_id(1)
        rt = pl.program_id(2)
        t = nrt - 1 - rt

        @pl.when((b == 0) & (rt == 0))
        def _():
            dp_ref[...] = jnp.zeros_like(dp_ref)

        @pl.when(rt == 0)
        def _():
            carry[...] = jnp.zeros_like(carry)

        row = lax.broadcasted_iota(jnp.int32, (TR, 1), 0)
        gate, hsv, av, dyv = g_ref[0], hs_ref[0], a_ref[0], dy_ref[0]
        dg_ref[0] = (dyv * hsv * _dgelu(gate)).astype(BF16)
        a_next = jnp.where(t == nrt - 1, 0.0, an_ref[0, 0:1, :])
        pb = jnp.where(row == TR - 1, a_next, pltpu.roll(av, TR - 1, 0))
        pd = dyv * _gelu(gate)
        sub = row % SUBLANES
        for s in (1, 2, 4):
            ok = sub < SUBLANES - s
            pd = jnp.where(ok, pd + pb * pltpu.roll(pd, TR - s, 0), pd)
            pb = jnp.where(ok, pb * pltpu.roll(pb, TR - s, 0), pb)
        pb_s[...] = pb
        pd_s[...] = pd
        row8 = lax.broadcasted_iota(jnp.int32, (SUBLANES, 1), 0)

        def gbody(i, c):
            r8 = pl.ds(pl.multiple_of((TR // SUBLANES - 1 - i) * SUBLANES, SUBLANES), SUBLANES)
            gg = pd_s[r8, :] + pb_s[r8, :] * c
            g_s[r8, :] = gg
            return jnp.sum(jnp.where(row8 == 0, gg, 0.0), axis=0, keepdims=True)

        carry[...] = lax.fori_loop(0, TR // SUBLANES, gbody, carry[...])
        gv = g_s[...]
        h_first = jnp.where(t == 0, 0.0, hsp_ref[0, SUBLANES - 1:SUBLANES, :])
        hprev = jnp.where(row == 0, h_first, pltpu.roll(hsv, 1, 0))
        first = (t == 0) & (row == 0)
        xc = xc_ref[0]
        r, i, ls, a2, mult0, mult = _lru_gates(xc, ra_ref[0], ix_ref[0], p_ref, first)
        dxc_ref[0] = gv * mult * i
        dlog_a = gv * hprev * av + jnp.where(first, 0.0, gv * i * xc * (-(av * av) / mult0))
        dra = dlog_a * LRU_C * ls * r * (1.0 - r)
        dix = gv * mult * xc * i * (1.0 - i)
        dra_ref[0] = dra.astype(BF16)
        dix_ref[0] = dix.astype(BF16)
        dp_ref[0:1, :] += _colsum(dra)
        dp_ref[1:2, :] += _colsum(dix)
        dp_ref[2:3, :] += _colsum(dlog_a * LRU_C * r) * _sigmoid(-p_ref[2:3, :])

    def win(off, shift=0):
        assert off % CB == 0
        o = off // CB
        return pl.BlockSpec((1, TR, CB), lambda j, b, rt: (b, jnp.clip(nrt - 1 - rt + shift, 0, nrt - 1), j + o))

    per_tile = TR // SUBLANES
    before = pl.BlockSpec((1, SUBLANES, CB), lambda j, b, rt: (b, jnp.maximum((nrt - 1 - rt) * per_tile - 1, 0), j))
    behind = pl.BlockSpec((1, SUBLANES, CB), lambda j, b, rt: (b, jnp.minimum((nrt - rt) * per_tile, nrt * per_tile - 1), j))

    return pl.pallas_call(
        kern, name=name, grid=(D // CB, B, nrt),
        in_specs=[win(0), win(gate_off), win(0), before, win(0), behind, win(xc_off), win(0), win(0),
                  pl.BlockSpec((SUBLANES, CB), lambda j, b, rt: (0, j))],
        out_specs=[win(0)] * 4 + [pl.BlockSpec((SUBLANES, CB), lambda j, b, rt: (0, j))],
        out_shape=[jax.ShapeDtypeStruct((B, L, D), BF16)] * 3 + [jax.ShapeDtypeStruct((B, L, D), F32),
                                                                 jax.ShapeDtypeStruct((SUBLANES, D), F32)],
        scratch_shapes=[pltpu.VMEM((TR, CB), F32)] * 3 + [pltpu.VMEM((1, CB), F32)],
        compiler_params=_cparams(("parallel", "arbitrary", "arbitrary")),
    )(dy, proj3, hs, hs, a, a, u, ra, ix, pvec)


def _sum8(parts, name):
    _, R, C = parts.shape
    tr = _tile(R, 1024, ROW_ALIGN if parts.dtype.itemsize == 2 else SUBLANES)

    def kern(p_ref, o_ref):
        acc = p_ref[0].astype(F32)
        for d in range(1, N_DEV):
            acc = acc + p_ref[d].astype(F32)
        o_ref[...] = acc

    return pl.pallas_call(
        kern, name=name, grid=(R // tr,), in_specs=[pl.BlockSpec((N_DEV, tr, C), lambda i: (0, i, 0))],
        out_specs=pl.BlockSpec((tr, C), lambda i: (i, 0)), out_shape=jax.ShapeDtypeStruct((R, C), F32),
        compiler_params=_cparams(("parallel",)),
    )(parts)


def _adamw(w, g, m, v, name):
    shape = w.shape
    C = shape[-1] if w.ndim > 1 else shape[0]
    R = w.size // C
    w2, g2, m2, v2 = (t.reshape(R, C) for t in (w, g, m, v))
    tr = R
    for cand in range(8, min(R, 512) + 1, 8):
        if R % cand == 0:
            tr = cand

    def kern(w_ref, g_ref, m_ref, v_ref, d_ref, nm_ref, nv_ref):
        gv = g_ref[...]
        nm = ADAM_B1 * m_ref[...] + (1.0 - ADAM_B1) * gv
        nv = ADAM_B2 * v_ref[...] + (1.0 - ADAM_B2) * (gv * gv)
        m_hat = nm / (1.0 - ADAM_B1 ** ADAM_STEP)
        v_hat = nv / (1.0 - ADAM_B2 ** ADAM_STEP)
        d_ref[...] = -ADAM_LR * (m_hat / (jnp.sqrt(v_hat) + ADAM_EPS) + ADAM_WD * w_ref[...])
        nm_ref[...] = nm
        nv_ref[...] = nv

    spec = pl.BlockSpec((tr, C), lambda i: (i, 0))
    outs = pl.pallas_call(
        kern, name=name, grid=(R // tr,), in_specs=[spec] * 4, out_specs=[spec] * 3,
        out_shape=[jax.ShapeDtypeStruct((R, C), F32)] * 3, compiler_params=_cparams(("parallel",)),
    )(w2, g2, m2, v2)
    return tuple(o.reshape(shape) for o in outs)


MESH_ID = pl.DeviceIdType.MESH
ANY = pl.BlockSpec(memory_space=pl.ANY)
N_COPIES = N_DEV - 1
COMM_SCRATCH = [pltpu.SemaphoreType.DMA((N_COPIES,)), pltpu.SemaphoreType.DMA((N_COPIES,)), pltpu.SemaphoreType.DMA]


def _my_place():
    return lax.axis_index("x"), lax.axis_index("y"), lax.axis_index("c")


def _gather_phases(x_ref, out_ref, send_sems, recv_sems, local_sem):
    x, y, c = _my_place()
    me, sibling = (x, y, c), (x, y, 1 - c)
    chips = [(1 - x, y), (x, 1 - y), (1 - x, 1 - y)]

    def slab(px, py, pc):
        return out_ref.at[4 * px + 2 * py + pc]

    def copy(k, block, to, src=None):
        return pltpu.make_async_remote_copy(
            src_ref=slab(*block) if src is None else src, dst_ref=slab(*block),
            send_sem=send_sems.at[k], recv_sem=recv_sems.at[k], device_id=to, device_id_type=MESH_ID)

    mine = pltpu.make_async_copy(x_ref, slab(*me), local_sem)
    first = [copy(0, me, sibling, src=x_ref)] + [copy(1 + j, me, (*chip, c), src=x_ref) for j, chip in enumerate(chips)]
    passed = [copy(4 + j, (*chip, c), sibling) for j, chip in enumerate(chips)]

    def start():
        mine.start()
        for cp in first:
            cp.start()

    def forward():
        for j, chip in enumerate(chips):
            copy(1 + j, (*chip, c), me).wait_recv()
            passed[j].start()

    def finish():
        copy(0, sibling, me).wait_recv()
        for j, chip in enumerate(chips):
            copy(4 + j, (*chip, 1 - c), me).wait_recv()
        for cp in first + passed:
            cp.wait_send()
        mine.wait()

    return start, forward, finish


def _exchange_phases(p_ref, out_ref, send_sems, recv_sems, local_sem):
    x, y, c = _my_place()
    my_idx = 4 * x + 2 * y + c
    mine = pltpu.make_async_copy(p_ref.at[my_idx], out_ref.at[my_idx], local_sem)
    copies = []
    for k in range(1, N_DEV):
        px, py, pc = x ^ (k >> 2), y ^ ((k >> 1) & 1), c ^ (k & 1)
        copies.append(pltpu.make_async_remote_copy(
            src_ref=p_ref.at[4 * px + 2 * py + pc], dst_ref=out_ref.at[my_idx],
            send_sem=send_sems.at[k - 1], recv_sem=recv_sems.at[k - 1], device_id=(px, py, pc),
            device_id_type=MESH_ID))

    def start():
        mine.start()
        for cp in copies:
            cp.start()

    def finish():
        for cp in copies:
            cp.wait()
        mine.wait()

    return start, finish


def _ride(lin, total, phases):
    assert total >= 3
    marks = [0, total - 1] if len(phases) == 2 else [0, total // 2, total - 1]
    for mark, phase in zip(marks, phases):
        pl.when(lin == mark)(phase)


def _all_gather(xs, name):
    R, C = xs.shape

    def body(x_ref, out_ref, send_sems, recv_sems, local_sem):
        for phase in _gather_phases(x_ref, out_ref, send_sems, recv_sems, local_sem):
            phase()

    return pl.pallas_call(
        body, name=name, out_shape=jax.ShapeDtypeStruct((N_DEV, R, C), xs.dtype), in_specs=[ANY], out_specs=ANY,
        scratch_shapes=COMM_SCRATCH,
    )(xs)


def _exchange(parts, name):
    def body(p_ref, out_ref, send_sems, recv_sems, local_sem):
        for phase in _exchange_phases(p_ref, out_ref, send_sems, recv_sems, local_sem):
            phase()

    return pl.pallas_call(
        body, name=name, out_shape=jax.ShapeDtypeStruct(parts.shape, parts.dtype), in_specs=[ANY], out_specs=ANY,
        scratch_shapes=COMM_SCRATCH,
    )(parts)


D_XBC_EXTRA = 2 * SSD_GROUPS * SSD_STATE
SMALL_W = LANES
ROW_ALIGN = 16


def _layout(D):
    d_xbc = D + D_XBC_EXTRA
    off = dict(qkv=0, z=3 * D, merge=4 * D, gate=7 * D, conv=8 * D, xr=8 * D + d_xbc, small=9 * D + d_xbc)
    off["n_all"] = off["small"] + SMALL_W
    off["d_xbc"] = d_xbc
    off["conv_c"] = d_xbc + D
    return off


def _w_in_map(D):
    lo = _layout(D)
    widths = [("q", D, 0), ("k", D, D), ("v", D, 2 * D), ("f", HEADS, lo["small"]), ("z", D, lo["z"]),
              ("xbc", lo["d_xbc"], lo["conv"]), ("dt", HEADS, lo["small"] + HEADS), ("xr", D, lo["xr"]),
              ("gate", D, lo["gate"]), ("merge", 3 * D, lo["merge"])]
    out, o = [], 0
    for _, w, mine in widths:
        out.append((o, w, mine))
        o += w
    return out


def _padded(c):
    return -(-c // ROW_ALIGN) * ROW_ALIGN


def _permute_rows(src, pieces, name):
    R, C = src.shape
    n_out = sum(n for _, n in pieces)

    def kern(x_ref, o_ref):
        o = 0
        for start, n in pieces:
            if start is None:
                o_ref[o:o + n, :] = jnp.zeros((n, LANES), src.dtype)
            else:
                o_ref[o:o + n, :] = x_ref[start:start + n, :]
            o += n

    return pl.pallas_call(
        kern, name=name, grid=(C // LANES,), in_specs=[pl.BlockSpec((R, LANES), lambda i: (0, i))],
        out_specs=pl.BlockSpec((n_out, LANES), lambda i: (0, i)), out_shape=jax.ShapeDtypeStruct((n_out, C), src.dtype),
        compiler_params=_cparams(("parallel",)),
    )(src)


def _reorder_rows(wt, D, c, name="reorder_w_in"):
    cp = _padded(c)
    lo = _layout(D)
    pieces = []
    for a, w, mine in sorted(_w_in_map(D), key=lambda t: t[2]):
        b = a + w
        while a < b:
            j = a // c
            e = min(b, (j + 1) * c)
            pieces.append((j * cp + a - j * c, e - a))
            a = e
    pieces.append((None, lo["n_all"] - lo["small"] - 2 * HEADS))
    return _permute_rows(wt, pieces, name)


def _restore_rows(dwt, D, c, name="restore_w_in"):
    cp = _padded(c)
    segs = _w_in_map(D)
    pieces = []
    for j in range(N_DEV):
        a, b = j * c, (j + 1) * c
        for s0, w, mine in segs:
            lo_, hi_ = max(a, s0), min(b, s0 + w)
            if lo_ < hi_:
                pieces.append((mine + lo_ - s0, hi_ - lo_))
        if cp > c:
            pieces.append((None, cp - c))
    return _permute_rows(dwt, pieces, name)


def _block_diag(w):
    H, n, _ = w.shape
    tiled = jnp.tile(w.reshape(H * n, n), (1, H))
    r = lax.broadcasted_iota(jnp.int32, (H * n, H * n), 0) // n
    c = lax.broadcasted_iota(jnp.int32, (H * n, H * n), 1) // n
    return jnp.where(r == c, tiled, jnp.zeros_like(tiled))


def _diag_blocks(m, H):
    n = m.shape[0] // H
    keep = jnp.eye(H, dtype=m.dtype)[:, None, :, None]
    return jnp.sum(m.reshape(H, n, H, n) * keep, axis=2)


def _to_heads(t, B, L):
    return t.reshape(B, L, HEADS, HEAD_DIM).transpose(0, 2, 1, 3)


def _from_heads(t4):
    B, H, L, P = t4.shape
    return t4.transpose(0, 2, 1, 3).reshape(B * L, H * P)


def _rows_to_tm(rows):
    B, H, nc, Q = rows.shape
    return rows.reshape(B, H, nc * Q).transpose(0, 2, 1)


def _ffn_fwd(h, g, wgu_t, wd, tag):
    n = _norm_fwd(h, g, tag + "_norm")
    gate, up, act = _ffn_up_act(n, wgu_t, tag + "_up")
    out = _mm(act, wd, res=h, scale=0.5, name=tag + "_down")
    return out, (h, n, gate, up, act)


def _ffn_bwd(dh, dhb, saved, g, wgu_t, wd, tag):
    h, n, gate, up, act = saved
    dgu = _ffn_down_dx_act(dhb, wd, gate, up, tag + "_down_dx")
    dwd = _mm(act, dhb, ta=True, scale=0.5, name=tag + "_down_dw")
    dwgu_t = _mm(dgu, n, ta=True, tn=1024, name=tag + "_up_dw")
    dn = _mm(dgu, wgu_t, name=tag + "_up_dx")
    dh_in, dhb_in, dg = _norm_bwd(h, dn, dh, g, tag + "_norm_bwd")
    return dh_in, dhb_in, dict(norm=dg, gu=dwgu_t, down=dwd)


def _mixer_fwd(h, p, B, L, gather=None):
    T, D = h.shape
    lo = _layout(D)
    n = _norm_fwd(h, p["gm"], "mix_norm")
    proj = _mm(n, p["w_all_t"], tb=True, name="mix_in")
    proj3 = proj.reshape(B, L, lo["n_all"])
    vals, cums = _gate_prep(proj3, lo["small"] // LANES, p["small_bias"], p["avec"], "gate_prep")
    cums_t = cums[..., :2 * HEADS].transpose(0, 2, 1).reshape(B, 2 * HEADS, L // Q_BLOCK, Q_BLOCK)
    qa, ka, va = _attn_pack(proj3, cums, "attn_pack")
    y_a3, y_ab3, lse, *gathered = _attn_fwd(qa, ka, va, "attn_fwd", gather)
    y_a, y_ab = y_a3.reshape(T, D), y_ab3.reshape(T, D)
    u = _conv_fwd(proj3, lo["conv"] // LANES, p["conv_w"], p["conv_b"], lo["d_xbc"] // LANES, "conv_fwd")
    b_blk = D // LANES
    c_blk = b_blk + SSD_GROUPS * SSD_STATE // LANES
    y_s3, ssd_states = _ssd_fwd(u, b_blk, c_blk, vals, cums, cums_t, p["dvec"], "ssd_fwd")
    y_s = y_s3.reshape(T, D)
    yb = _gnorm_fwd(y_s, proj, lo["z"], p["ssd_norm"], "gnorm_fwd")
    u2 = u.reshape(T, lo["conv_c"])
    ra = _mm(u2, p["wa"], a_off=(0, lo["d_xbc"]), dims=(T, D, D), tk=512, name="lru_ra")
    ix = _mm(u2, p["wx"], a_off=(0, lo["d_xbc"]), dims=(T, D, D), tk=512, name="lru_ix")
    yc, hs, a = _lru_fwd(u, lo["d_xbc"], ra.reshape(B, L, D), ix.reshape(B, L, D), proj3, lo["gate"], p["pvec"],
                         "lru_fwd")
    yc = yc.reshape(T, D)
    pa = _mm(y_ab, p["wba"], name="branch_attn")
    pb = _mm(yb, p["wbs"], name="branch_ssd")
    pc = _mm(yc, p["wbl"], name="branch_lru")
    mixed = _merge_fwd(proj, lo["merge"], pa, pb, pc, "merge_fwd")
    out = _mm(mixed, p["wout"], res=h, name="mix_out")
    saved = dict(h=h, n=n, proj=proj, qa=qa, ka=ka, va=va, vals=vals, cums=cums, cums_t=cums_t, lse=lse, y_a=y_a, y_ab=y_ab,
                 u=u, y_s=y_s, ssd_states=ssd_states, yb=yb, ra=ra, ix=ix, yc=yc, hs=hs, a=a, pa=pa, pb=pb, pc=pc, mixed=mixed)
    return out, saved, (gathered[0] if gathered else None)


def _mixer_bwd(dh, dhb, s, p, B, L, parts=None):
    T, D = dh.shape
    lo = _layout(D)
    proj, u = s["proj"], s["u"]
    proj3 = proj.reshape(B, L, lo["n_all"])
    g = {}
    dmixed = _mm(dhb, p["wout"], tb=True, name="mix_out_dx")
    g["wout"] = _mm(s["mixed"], dhb, ta=True, name="mix_out_dw")
    dpa, dpb, dpc, dmerge = _merge_bwd(dmixed, proj, lo["merge"], s["pa"], s["pb"], s["pc"], "merge_bwd")
    dy_a = _mm(dpa, p["wba"], tb=True, name="branch_attn_dx")
    g["wba"] = _mm(s["y_ab"], dpa, ta=True, name="branch_attn_dw")
    dyb = _mm(dpb, p["wbs"], tb=True, name="branch_ssd_dx")
    g["wbs"] = _mm(s["yb"], dpb, ta=True, name="branch_ssd_dw")
    dyc = _mm(dpc, p["wbl"], tb=True, name="branch_lru_dx")
    g["wbl"] = _mm(s["yc"], dpc, ta=True, name="branch_lru_dw")
    dgate, dra, dix, dxc, g["pvec"] = _lru_bwd(dyc.reshape(B, L, D), proj3, lo["gate"], s["hs"], s["a"], u, lo["d_xbc"],
                                               s["ra"].reshape(B, L, D), s["ix"].reshape(B, L, D), p["pvec"], "lru_bwd")
    dra, dix = dra.reshape(T, D), dix.reshape(T, D)
    u2 = u.reshape(T, lo["conv_c"])
    g["wa"] = _mm(u2, dra, ta=True, a_off=(0, lo["d_xbc"]), dims=(D, D, T), tm=512, name="lru_ra_dw")
    g["wx"] = _mm(u2, dix, ta=True, a_off=(0, lo["d_xbc"]), dims=(D, D, T), tm=512, name="lru_ix_dw")
    dxc = _mm(dra, p["wa"], tb=True, res=dxc.reshape(T, D), name="lru_ra_dx")
    dxc = _mm(dix, p["wx"], tb=True, res=dxc, name="lru_ix_dx")
    dy_s, dz, g["ssd_norm"] = _gnorm_bwd(dyb, s["y_s"], proj, lo["z"], p["ssd_norm"], "gnorm_bwd")
    b_blk = D // LANES
    c_blk = b_blk + SSD_GROUPS * SSD_STATE // LANES
    dxs, dBg, dCg, ddt_tm, dAc_tm, dAr, g["dvec"] = _ssd_bwd(u, b_blk, c_blk, s["vals"], s["cums"], s["cums_t"],
                                                             p["dvec"], dy_s.reshape(B, L, D), s["ssd_states"], "ssd_bwd")
    dconv, conv_wb = _conv_bwd(proj3, lo["conv"] // LANES, dxs, dBg, dCg, dxc.reshape(B, L, D), p["conv_w"], p["conv_b"],
                               lo["d_xbc"] // LANES, "conv_bwd")
    g["conv_wb"] = jnp.sum(conv_wb, axis=0)
    dq3, dk3, dv3, dc_tm, *recv = _attn_bwd(s["qa"], s["ka"], s["va"], s["y_a"].reshape(B, L, D),
                                            dy_a.reshape(B, L, D), s["lse"], "attn_bwd", parts)
    drow_tm = dc_tm + jnp.pad(_rows_to_tm(dAr), ((0, 0), (0, 0), (HEADS, LANES - 2 * HEADS)))
    dsmall, g["small_bias"], g["avec"] = _gate_post(drow_tm, dAc_tm, ddt_tm, proj3, lo["small"] // LANES, s["vals"],
                                                    p["small_bias"], p["avec"], "gate_post")
    dproj = jnp.concatenate([dq3.reshape(T, D), dk3.reshape(T, D), dv3.reshape(T, D), dz, dmerge, dgate.reshape(T, D), dconv.reshape(T, lo["conv_c"]),
                             dsmall.reshape(T, SMALL_W)], axis=1)
    g["w_all_t"] = _mm(dproj, s["n"], ta=True, tn=1024, name="mix_in_dw")
    dn = _mm(dproj, p["w_all_t"], name="mix_in_dx")
    dh_in, dhb_in, g["gm"] = _norm_bwd(s["h"], dn, dh, p["gm"], "mix_norm_bwd")
    return dh_in, dhb_in, g, (recv[0] if recv else None)


def _small_vec(a, b):
    return jnp.concatenate([a, b, jnp.zeros((LANES - 2 * HEADS,), F32)])[None, :]


def _layer_params(w):
    zeros16 = jnp.zeros((HEADS,), F32)
    pvec = jnp.concatenate([w["lru_b_a"][None], w["lru_b_x"][None], w["lru_lambda"][None],
                            jnp.zeros((SUBLANES - 3, w["lru_b_a"].shape[0]), F32)], axis=0)
    return dict(
        g1=w["ffn1_norm"][None], gu1=w["ffn1_w_gate_up"], d1=w["ffn1_w_down"],
        gm=w["mix_norm"][None], w_all_t=w["w_in"],
        small_bias=_small_vec(w["fox_forget_bias"], w["ssd_dt_bias"]),
        avec=_small_vec(zeros16, -jnp.exp(w["ssd_a_log"])), dvec=_small_vec(zeros16, w["ssd_d"]),
        conv_w=jnp.concatenate([w["ssd_conv_w"], w["lru_conv_w"]], axis=1),
        conv_b=jnp.concatenate([w["ssd_conv_b"], w["lru_conv_b"]])[None],
        ssd_norm=w["ssd_norm"][None],
        wa=_block_diag(w["lru_w_a"]).astype(BF16), wx=_block_diag(w["lru_w_x"]).astype(BF16), pvec=pvec,
        wba=w["w_branch_attn"], wbs=w["w_branch_ssd"], wbl=w["w_branch_lru"], wout=w["w_out"],
        g2=w["ffn2_norm"][None], gu2=w["ffn2_w_gate_up"], d2=w["ffn2_w_down"],
    )


def _layer_fwd(h, p, B, L, gather=None):
    h, s1 = _ffn_fwd(h, p["g1"], p["gu1"], p["d1"], "ffn1")
    h, sm, gathered = _mixer_fwd(h, p, B, L, gather)
    h, s2 = _ffn_fwd(h, p["g2"], p["gu2"], p["d2"], "ffn2")
    return h, (s1, sm, s2), gathered


def _layer_bwd(dh, dhb, saved, p, w, B, L, parts=None):
    s1, sm, s2 = saved
    D = dh.shape[1]
    d_xbc = D + D_XBC_EXTRA
    dh, dhb, f2 = _ffn_bwd(dh, dhb, s2, p["g2"], p["gu2"], p["d2"], "ffn2")
    dh, dhb, gm, recv = _mixer_bwd(dh, dhb, sm, p, B, L, parts)
    dh, dhb, f1 = _ffn_bwd(dh, dhb, s1, p["g1"], p["gu1"], p["d1"], "ffn1")
    sb, av = gm["small_bias"][0], gm["avec"][0]
    cw = gm["conv_wb"]
    grads = dict(
        ffn1_norm=f1["norm"][0], ffn1_w_gate_up=f1["gu"], ffn1_w_down=f1["down"],
        mix_norm=gm["gm"][0], w_in=gm["w_all_t"],
        fox_forget_bias=sb[:HEADS], ssd_conv_w=cw[:CONV_K, :d_xbc], ssd_conv_b=cw[CONV_K, :d_xbc],
        ssd_dt_bias=sb[HEADS:2 * HEADS], ssd_a_log=av[HEADS:2 * HEADS] * (-jnp.exp(w["ssd_a_log"])),
        ssd_d=gm["dvec"][0, HEADS:2 * HEADS], ssd_norm=gm["ssd_norm"][0],
        lru_conv_w=cw[:CONV_K, d_xbc:], lru_conv_b=cw[CONV_K, d_xbc:],
        lru_w_a=_diag_blocks(gm["wa"], HEADS), lru_b_a=gm["pvec"][0], lru_w_x=_diag_blocks(gm["wx"], HEADS),
        lru_b_x=gm["pvec"][1], lru_lambda=gm["pvec"][2],
        w_branch_attn=gm["wba"], w_branch_ssd=gm["wbs"], w_branch_lru=gm["wbl"], w_out=gm["wout"],
        ffn2_norm=f2["norm"][0], ffn2_w_gate_up=f2["gu"], ffn2_w_down=f2["down"],
    )
    return dh, dhb, grads, recv


LAYER_NAMES = ["ffn1_norm", "ffn1_w_gate_up", "ffn1_w_down", "mix_norm", "w_in", "fox_forget_bias", "ssd_conv_w",
               "ssd_conv_b", "ssd_dt_bias", "ssd_a_log", "ssd_d", "ssd_norm", "lru_conv_w", "lru_conv_b", "lru_w_a",
               "lru_b_a", "lru_w_x", "lru_b_x", "lru_lambda", "w_branch_attn", "w_branch_ssd", "w_branch_lru", "w_out",
               "ffn2_norm", "ffn2_w_gate_up", "ffn2_w_down"]
WEIGHT_NAMES = ["meta_tokens"] + LAYER_NAMES + ["final_norm"]


def _local_step(x, target, meta, final_norm, depth, layer_weights, pack_next=None, pack_grads=None):
    B, S, D = x.shape
    L = -(-(N_META + S) // Q_BLOCK) * Q_BLOCK
    h = jnp.concatenate([jnp.broadcast_to(meta[None], (B, N_META, D)), x,
                         jnp.zeros((B, L - N_META - S, D), F32)], axis=1).reshape(B * L, D)
    weights, params, saved = [], [], []
    gathered = None
    for l in range(depth):
        w = layer_weights(l, gathered)
        p = _layer_params(w)
        nxt = pack_next(l + 1) if (pack_next is not None and l + 1 < depth) else None
        h, s, gathered = _layer_fwd(h, p, B, L, nxt)
        weights.append(w)
        params.append(p)
        saved.append(s)
    tgt = jnp.pad(target, ((0, 0), (N_META, L - N_META - S), (0, 0))).reshape(B * L, D)
    dh, dhb, loss, dfinal = _loss_bwd(h, tgt, final_norm[None], L, S, "loss")
    grads = [None] * depth
    received, parts = {}, None
    for l in reversed(range(depth)):
        dh, dhb, grads[l], recv = _layer_bwd(dh, dhb, saved[l], params[l], weights[l], B, L, parts)
        if recv is not None:
            received[l + 1] = recv
        parts = pack_grads(grads[l]) if pack_grads is not None else None
    dh3 = dh.reshape(B, L, D)
    return (loss, dh3[:, N_META:N_META + S], jnp.sum(dh3[:, :N_META], axis=0), grads, dfinal[0], received, parts)


BIG_NAMES = ["ffn1_w_gate_up", "ffn1_w_down", "w_in", "w_branch_attn", "w_branch_ssd", "w_branch_lru", "w_out",
             "ffn2_w_gate_up", "ffn2_w_down"]
COL_SHARDED = {"ffn1_w_gate_up", "w_in", "ffn2_w_gate_up"}
SMALL_SHARDED = ["meta_tokens", "ssd_conv_w", "lru_conv_w"]
SMALL_NAMES = [n for n in LAYER_NAMES if n not in BIG_NAMES]


def _shard_rows(name, shape):
    return _padded(shape[1]) if name in COL_SHARDED else shape[0]


def _pack_shards(shards):
    rows = []
    for n in BIG_NAMES:
        s = shards[n]
        if n in COL_SHARDED:
            s = jnp.pad(s.T, ((0, _padded(s.shape[1]) - s.shape[1]), (0, 0)))
        rows.append(s)
    return jnp.concatenate(rows, axis=0)


def _unpack_gathered(gathered, shapes, D):
    out, o = {}, 0
    for n in BIG_NAMES:
        r = _shard_rows(n, shapes[n])
        out[n] = gathered[:, o:o + r].reshape(N_DEV * r, D)
        o += r
    out["w_in"] = _reorder_rows(out["w_in"], D, shapes["w_in"][1])
    return out


def _pack_full_grads(grads, shapes, D):
    slabs = []
    for n in BIG_NAMES:
        g = grads[n]
        if n == "w_in":
            g = _restore_rows(g, D, shapes[n][1])
        slabs.append(g.reshape(N_DEV, _shard_rows(n, shapes[n]), D))
    return jnp.concatenate(slabs, axis=1)


def _unpack_local(rows, shapes):
    out, o = {}, 0
    for n in BIG_NAMES:
        r = _shard_rows(n, shapes[n])
        blk = rows[o:o + r]
        out[n] = blk[:shapes[n][1]].T if n in COL_SHARDED else blk
        o += r
    return out


def _as_rows(flat):
    n = flat.shape[0]
    unit = LANES * SUBLANES
    total = -(-n // unit) * unit
    return jnp.pad(flat, (0, total - n)).reshape(total // LANES, LANES)


def _flatten_list(arrs):
    return _as_rows(jnp.concatenate([a.reshape(-1) for a in arrs]))


def _split_like(rows, shapes):
    flat = rows.reshape(-1)
    out, o = [], 0
    for s in shapes:
        n = math.prod(s)
        out.append(flat[o:o + n].reshape(s))
        o += n
    return out


def _gather_last(rows8, shape):
    lead, c = shape[:-1], shape[-1]
    t = rows8.reshape((N_DEV,) + tuple(lead) + (c,))
    return jnp.moveaxis(t, 0, -2).reshape(tuple(lead) + (N_DEV * c,))


def kernel(x, meta_tokens, ffn1_norm, ffn1_w_gate_up, ffn1_w_down, mix_norm, w_in, fox_forget_bias, ssd_conv_w, ssd_conv_b, ssd_dt_bias, ssd_a_log, ssd_d, ssd_norm, lru_conv_w, lru_conv_b, lru_w_a, lru_b_a, lru_w_x, lru_b_x, lru_lambda, w_branch_attn, w_branch_ssd, w_branch_lru, w_out, ffn2_norm, ffn2_w_gate_up, ffn2_w_down, final_norm, loss_target, m_meta_tokens, m_ffn1_norm, m_ffn1_w_gate_up, m_ffn1_w_down, m_mix_norm, m_w_in, m_fox_forget_bias, m_ssd_conv_w, m_ssd_conv_b, m_ssd_dt_bias, m_ssd_a_log, m_ssd_d, m_ssd_norm, m_lru_conv_w, m_lru_conv_b, m_lru_w_a, m_lru_b_a, m_lru_w_x, m_lru_b_x, m_lru_lambda, m_w_branch_attn, m_w_branch_ssd, m_w_branch_lru, m_w_out, m_ffn2_norm, m_ffn2_w_gate_up, m_ffn2_w_down, m_final_norm, v_meta_tokens, v_ffn1_norm, v_ffn1_w_gate_up, v_ffn1_w_down, v_mix_norm, v_w_in, v_fox_forget_bias, v_ssd_conv_w, v_ssd_conv_b, v_ssd_dt_bias, v_ssd_a_log, v_ssd_d, v_ssd_norm, v_lru_conv_w, v_lru_conv_b, v_lru_w_a, v_lru_b_a, v_lru_w_x, v_lru_b_x, v_lru_lambda, v_w_branch_attn, v_w_branch_ssd, v_w_branch_lru, v_w_out, v_ffn2_norm, v_ffn2_w_gate_up, v_ffn2_w_down, v_final_norm):
    weights = dict(zip(WEIGHT_NAMES, (meta_tokens, ffn1_norm, ffn1_w_gate_up, ffn1_w_down, mix_norm, w_in, fox_forget_bias, ssd_conv_w, ssd_conv_b, ssd_dt_bias, ssd_a_log, ssd_d, ssd_norm, lru_conv_w, lru_conv_b, lru_w_a, lru_b_a, lru_w_x, lru_b_x, lru_lambda, w_branch_attn, w_branch_ssd, w_branch_lru, w_out, ffn2_norm, ffn2_w_gate_up, ffn2_w_down, final_norm,)))
    mom1 = dict(zip(WEIGHT_NAMES, (m_meta_tokens, m_ffn1_norm, m_ffn1_w_gate_up, m_ffn1_w_down, m_mix_norm, m_w_in, m_fox_forget_bias, m_ssd_conv_w, m_ssd_conv_b, m_ssd_dt_bias, m_ssd_a_log, m_ssd_d, m_ssd_norm, m_lru_conv_w, m_lru_conv_b, m_lru_w_a, m_lru_b_a, m_lru_w_x, m_lru_b_x, m_lru_lambda, m_w_branch_attn, m_w_branch_ssd, m_w_branch_lru, m_w_out, m_ffn2_norm, m_ffn2_w_gate_up, m_ffn2_w_down, m_final_norm,)))
    mom2 = dict(zip(WEIGHT_NAMES, (v_meta_tokens, v_ffn1_norm, v_ffn1_w_gate_up, v_ffn1_w_down, v_mix_norm, v_w_in, v_fox_forget_bias, v_ssd_conv_w, v_ssd_conv_b, v_ssd_dt_bias, v_ssd_a_log, v_ssd_d, v_ssd_norm, v_lru_conv_w, v_lru_conv_b, v_lru_w_a, v_lru_b_a, v_lru_w_x, v_lru_b_x, v_lru_lambda, v_w_branch_attn, v_w_branch_ssd, v_w_branch_lru, v_w_out, v_ffn2_norm, v_ffn2_w_gate_up, v_ffn2_w_down, v_final_norm,)))
    depth = ffn1_norm.shape[0]
    D = x.shape[-1]
    my_idx = 4 * lax.axis_index("x") + 2 * lax.axis_index("y") + lax.axis_index("c")

    small_shapes = [weights[n].shape for n in SMALL_SHARDED]
    gathered = _all_gather(_flatten_list([weights[n] for n in SMALL_SHARDED]), "gather_small").reshape(N_DEV, -1)
    small_full, o = {}, 0
    for n, s in zip(SMALL_SHARDED, small_shapes):
        k = math.prod(s)
        small_full[n] = _gather_last(gathered[:, o:o + k], s)
        o += k

    shard_shapes = {n: weights[n].shape[1:] for n in BIG_NAMES}
    pack_next = lambda l: _pack_shards({n: weights[n][l].astype(BF16) for n in BIG_NAMES})

    def layer_weights(l, gathered):
        if gathered is None:
            gathered = _all_gather(pack_next(l), "gather_weights")
        w = _unpack_gathered(gathered, shard_shapes, D)
        for n in SMALL_NAMES:
            w[n] = small_full[n][l] if n in SMALL_SHARDED else weights[n][l]
        return w

    pack_grads = lambda g: _pack_full_grads(g, shard_shapes, D).astype(BF16)
    loss, dx, dmeta, grads, dfinal, received, parts = _local_step(
        x, loss_target, small_full["meta_tokens"], final_norm, depth, layer_weights, pack_next, pack_grads)
    received[0] = _exchange(parts, "exchange_grads")
    loss = lax.psum(loss[0, 0], ("x", "y", "c"))
    summed = {n: [] for n in WEIGHT_NAMES}
    for l in range(depth):
        local = _unpack_local(_sum8(received[l], "sum_grads"), shard_shapes)
        for n in BIG_NAMES:
            summed[n].append(local[n])

    small_list = [dmeta, dfinal] + [grads[l][n] for l in range(depth) for n in SMALL_NAMES]
    total = _sum8(_all_gather(_flatten_list(small_list), "gather_small_grads"), "sum_small_grads")
    parts = _split_like(total, [a.shape for a in small_list])
    full_small = {"meta_tokens": parts[0], "final_norm": parts[1]}
    for i, n in enumerate(SMALL_NAMES):
        full_small[n] = jnp.stack([parts[2 + l * len(SMALL_NAMES) + i] for l in range(depth)])
    grad = {}
    for n in WEIGHT_NAMES:
        if n in BIG_NAMES:
            grad[n] = jnp.stack(summed[n])
        elif n in SMALL_SHARDED:
            c = weights[n].shape[-1]
            grad[n] = lax.dynamic_slice_in_dim(full_small[n], my_idx * c, c, axis=full_small[n].ndim - 1)
        else:
            grad[n] = full_small[n]

    delta, new_m, new_v = {}, {}, {}
    for n in WEIGHT_NAMES:
        delta[n], new_m[n], new_v[n] = _adamw(weights[n], grad[n], mom1[n], mom2[n], "adamw_" + n)
    return (loss, dx, *[grad[n] for n in WEIGHT_NAMES], *[delta[n] for n in WEIGHT_NAMES],
            *[new_m[n] for n in WEIGHT_NAMES], *[new_v[n] for n in WEIGHT_NAMES])
```

```python
import functools
import math

import jax
import jax.numpy as jnp
from jax import lax
from jax.experimental import pallas as pl
from jax.experimental.pallas import tpu as pltpu

F32 = jnp.float32
BF16 = jnp.bfloat16

N_DEV = 8
N_META = 16
Q_BLOCK = 128
NORM_EPS = 1e-6
HEADS = 16
HEAD_DIM = 64
SSD_GROUPS = 2
SSD_STATE = 128
CONV_K = 4
LRU_C = 8.0
ADAM_LR, ADAM_B1, ADAM_B2, ADAM_EPS, ADAM_WD, ADAM_STEP = 0.001, 0.9, 0.999, 1e-08, 0.01, 10

LANES = 128
SUBLANES = 8
VMEM_LIMIT = 56 * 1024 * 1024
NEG = -1e30
MM_TILE = 1408
MM_VMEM = 40 * 1024 * 1024


def _cparams(sem=None):
    return pltpu.CompilerParams(dimension_semantics=sem, vmem_limit_bytes=VMEM_LIMIT)


def _tile(dim, target, mult=LANES):
    if dim <= target:
        return dim
    best = None
    for t in range(mult, target + 1, mult):
        if dim % t == 0:
            best = t
    assert best is not None, (dim, target)
    return best


def _sigmoid(x):
    return 1.0 / (1.0 + jnp.exp(-x))


def _log1p_exp_neg_abs(x):
    e = jnp.exp(-jnp.abs(x))
    u = 1.0 + e
    return jnp.where(u == 1.0, e, jnp.log(u) * (e / jnp.where(u == 1.0, 1.0, u - 1.0)))


def _log_sigmoid(x):
    return jnp.minimum(x, 0.0) - _log1p_exp_neg_abs(x)


def _softplus(x):
    return jnp.maximum(x, 0.0) + _log1p_exp_neg_abs(x)


def _one_minus_exp(y):
    u = jnp.exp(y)
    safe = jnp.where(u == 1.0, 0.5, u)
    return jnp.where(u == 1.0, -y, (1.0 - u) * y / jnp.log(safe))


def _silu(x):
    return x * _sigmoid(x)


def _dsilu(x):
    s = _sigmoid(x)
    return s * (1.0 + x * (1.0 - s))


_GELU_C = math.sqrt(2.0 / math.pi)


def _gelu(x):
    return 0.5 * x * (1.0 + jnp.tanh(_GELU_C * (x + 0.044715 * x * x * x)))


def _dgelu(x):
    t = jnp.tanh(_GELU_C * (x + 0.044715 * x * x * x))
    return 0.5 * (1.0 + t) + 0.5 * x * (1.0 - t * t) * _GELU_C * (1.0 + 3.0 * 0.044715 * x * x)


def _split3_dot(tri, x):
    hi = x.astype(BF16)
    r1 = x - hi.astype(F32)
    mid = r1.astype(BF16)
    lo = (r1 - mid.astype(F32)).astype(BF16)
    t = tri.astype(BF16)
    d = lambda p: jnp.dot(t, p, preferred_element_type=F32)
    return d(hi) + d(mid) + d(lo)


def _lower_tri(n, strict=False):
    r = lax.broadcasted_iota(jnp.int32, (n, n), 0)
    c = lax.broadcasted_iota(jnp.int32, (n, n), 1)
    return (c < r) if strict else (c <= r)


def _mm(a, b, *, ta=False, tb=False, out_dtype=F32, res=None, scale=None, tm=None, tn=None, tk=None,
        a_off=(0, 0), b_off=(0, 0), dims=None, name):
    if dims is None:
        M, K = (a.shape[1], a.shape[0]) if ta else a.shape
        N = b.shape[0] if tb else b.shape[1]
    else:
        M, N, K = dims
    tk = tk or _tile(K, 2816)
    nk_ = K // tk
    pick_m, pick_n = tm is None, tn is None
    tm = tm or _tile(M, MM_TILE)
    tn = tn or _tile(N, MM_TILE)

    def vmem(tm_, tn_):
        a_b = tm_ * tk * a.dtype.itemsize + (tm_ * tk * 2 if a.dtype != BF16 else 0)
        b_b = tn_ * tk * b.dtype.itemsize + (tn_ * tk * 2 if b.dtype != BF16 else 0)
        o_b = tm_ * tn_ * (jnp.dtype(out_dtype).itemsize + (4 if res is not None else 0))
        return 2 * (a_b + b_b + o_b) + (tm_ * tn_ * 4 if nk_ > 1 else 0) + tm_ * tn_ * 4

    while vmem(tm, tn) > MM_VMEM and (pick_m or pick_n):
        if pick_m and (tm >= tn or not pick_n) and tm > LANES:
            tm = _tile(M, tm - LANES)
        elif pick_n and tn > LANES:
            tn = _tile(N, tn - LANES)
        else:
            break
    assert M % tm == 0 and N % tn == 0 and K % tk == 0, (name, M, N, K, tm, tn, tk)
    nk = K // tk
    ca = 0 if ta else 1
    cb = 1 if tb else 0

    def blk(rows, cols, off):
        assert off[0] % rows == 0 and off[1] % cols == 0, (name, off, rows, cols)
        return off[0] // rows, off[1] // cols

    if ta:
        ao = blk(tk, tm, a_off)
        a_spec = pl.BlockSpec((tk, tm), lambda i, j, k: (k + ao[0], i + ao[1]))
    else:
        ao = blk(tm, tk, a_off)
        a_spec = pl.BlockSpec((tm, tk), lambda i, j, k: (i + ao[0], k + ao[1]))
    if tb:
        bo = blk(tn, tk, b_off)
        b_spec = pl.BlockSpec((tn, tk), lambda i, j, k: (j + bo[0], k + bo[1]))
    else:
        bo = blk(tk, tn, b_off)
        b_spec = pl.BlockSpec((tk, tn), lambda i, j, k: (k + bo[0], j + bo[1]))
    o_spec = pl.BlockSpec((tm, tn), lambda i, j, k: (i, j))
    in_specs = [a_spec, b_spec] + ([o_spec] if res is not None else [])
    has_res = res is not None

    def kern(*refs):
        if has_res:
            a_ref, b_ref, r_ref, o_ref = refs[:4]
            scr = refs[4:]
        else:
            a_ref, b_ref, o_ref = refs[:3]
            r_ref = None
            scr = refs[3:]
        p = lax.dot_general(a_ref[...].astype(BF16), b_ref[...].astype(BF16), (((ca,), (cb,)), ((), ())),
                            preferred_element_type=F32)

        def fin(val):
            if scale is not None:
                val = val * scale
            if has_res:
                val = r_ref[...] + val
            o_ref[...] = val.astype(out_dtype)

        if nk == 1:
            fin(p)
        else:
            acc = scr[0]
            k = pl.program_id(2)

            @pl.when(k == 0)
            def _():
                acc[...] = p

            @pl.when(k > 0)
            def _():
                acc[...] += p

            @pl.when(k == nk - 1)
            def _():
                fin(acc[...])

    args = (a, b) + ((res,) if has_res else ())
    return pl.pallas_call(
        kern, name=name, grid=(M // tm, N // tn, nk), in_specs=in_specs, out_specs=o_spec,
        out_shape=jax.ShapeDtypeStruct((M, N), out_dtype),
        scratch_shapes=[pltpu.VMEM((tm, tn), F32)] if nk > 1 else [],
        compiler_params=_cparams(("parallel", "parallel", "arbitrary")),
    )(*args)


def _rows(body, tiled, full, outs, accs, *, tr, name, T):
    assert T % tr == 0
    in_specs = []
    for arr, width, off in tiled:
        assert off % width == 0, (name, off, width)
        in_specs.append(pl.BlockSpec((tr, width), functools.partial(lambda i, o: (i, o), o=off // width)))
    for arr in full:
        in_specs.append(pl.BlockSpec(arr.shape, lambda i: (0, 0)))
    out_specs = [pl.BlockSpec((tr, w), lambda i: (i, 0)) for w, _ in outs]
    out_specs += [pl.BlockSpec(s, lambda i: (0, 0)) for s, _ in accs]
    out_shape = [jax.ShapeDtypeStruct((T, w), d) for w, d in outs] + [jax.ShapeDtypeStruct(s, d) for s, d in accs]
    nt, nf, no = len(tiled), len(full), len(outs)

    def kern(*refs):
        i = pl.program_id(0)
        acc_refs = refs[nt + nf + no:]

        @pl.when(i == 0)
        def _():
            for r in acc_refs:
                r[...] = jnp.zeros(r.shape, r.dtype)

        body(i, refs[:nt], refs[nt:nt + nf], refs[nt + nf:nt + nf + no], acc_refs)

    res = pl.pallas_call(
        kern, name=name, grid=(T // tr,), in_specs=in_specs, out_specs=out_specs, out_shape=out_shape,
        compiler_params=_cparams(("arbitrary",)),
    )(*[t[0] for t in tiled], *full)
    return res


def _colsum(x):
    return jnp.sum(x, axis=0, keepdims=True)


def _norm_fwd(h, g, name):
    T, D = h.shape

    def body(i, t, f, o, a):
        x = t[0][...]
        r = lax.rsqrt(jnp.mean(x * x, axis=-1, keepdims=True) + NORM_EPS)
        o[0][...] = (x * r * f[0][...]).astype(BF16)

    return _rows(body, [(h, D, 0)], [g], [(D, BF16)], [], tr=_tile(T, 768, 8), name=name, T=T)[0]


def _norm_bwd(h, dn, dh, g, name):
    T, D = h.shape

    def body(i, t, f, o, a):
        x, dnv, dhv = t[0][...], t[1][...], t[2][...]
        r = lax.rsqrt(jnp.mean(x * x, axis=-1, keepdims=True) + NORM_EPS)
        xh = x * r
        dng = dnv * f[0][...]
        out = dhv + r * (dng - xh * jnp.mean(dng * xh, axis=-1, keepdims=True))
        o[0][...] = out
        o[1][...] = out.astype(BF16)
        a[0][...] += _colsum(dnv * xh)

    return _rows(body, [(h, D, 0), (dn, D, 0), (dh, D, 0)], [g], [(D, F32), (D, BF16)], [((1, D), F32)],
                 tr=_tile(T, 384, 16), name=name, T=T)


FFN_TM = 768


def _ffn_up_act(n, wgu_t, name):
    T, D = n.shape
    F = wgu_t.shape[0] // 2
    tm, tn = _tile(T, FFN_TM), _tile(F, MM_TILE)
    nj = F // tn

    def kern(n_ref, wg_ref, wu_ref, g_ref, u_ref, a_ref):
        nv = n_ref[...]
        g = _dot_nt(nv, wg_ref[...])
        u = _dot_nt(nv, wu_ref[...])
        g_ref[...] = g
        u_ref[...] = u
        a_ref[...] = (_silu(g) * u).astype(BF16)

    out = pl.BlockSpec((tm, tn), lambda i, j: (i, j))
    return pl.pallas_call(
        kern, name=name, grid=(T // tm, nj),
        in_specs=[pl.BlockSpec((tm, D), lambda i, j: (i, 0)), pl.BlockSpec((tn, D), lambda i, j: (j, 0)),
                  pl.BlockSpec((tn, D), lambda i, j: (nj + j, 0))],
        out_specs=[out] * 3,
        out_shape=[jax.ShapeDtypeStruct((T, F), F32), jax.ShapeDtypeStruct((T, F), F32), jax.ShapeDtypeStruct((T, F), BF16)],
        compiler_params=_cparams(("parallel", "parallel")),
    )(n, wgu_t, wgu_t)


def _ffn_down_dx_act(dhb, wd, g, u, name):
    T, D = dhb.shape
    F = wd.shape[0]
    tm, tn = _tile(T, FFN_TM), _tile(F, MM_TILE)
    nj = F // tn

    def kern(d_ref, w_ref, g_ref, u_ref, o_ref):
        da = _dot_nt(d_ref[...], w_ref[...]) * 0.5
        gv, uv = g_ref[...], u_ref[...]
        dg = (da * uv * _dsilu(gv)).astype(BF16)
        du = (da * _silu(gv)).astype(BF16)
        for jj in range(nj):

            @pl.when(pl.program_id(1) == jj)
            def _():
                o_ref[:, jj * tn:(jj + 1) * tn] = dg
                o_ref[:, F + jj * tn:F + (jj + 1) * tn] = du

    tile = pl.BlockSpec((tm, tn), lambda i, j: (i, j))
    return pl.pallas_call(
        kern, name=name, grid=(T // tm, nj),
        in_specs=[pl.BlockSpec((tm, D), lambda i, j: (i, 0)), pl.BlockSpec((tn, D), lambda i, j: (j, 0)), tile, tile],
        out_specs=pl.BlockSpec((tm, 2 * F), lambda i, j: (i, 0)), out_shape=jax.ShapeDtypeStruct((T, 2 * F), BF16),
        compiler_params=_cparams(("parallel", "arbitrary")),
    )(dhb, wd, g, u)


def _merge_fwd(proj, off, pa, pb, pc, name):
    T, D = pa.shape

    def body(i, t, f, o, a):
        o[0][...] = (_sigmoid(t[0][...]) * t[3][...] + _sigmoid(t[1][...]) * t[4][...]
                     + _sigmoid(t[2][...]) * t[5][...]).astype(BF16)

    tiled = [(proj, D, off), (proj, D, off + D), (proj, D, off + 2 * D), (pa, D, 0), (pb, D, 0), (pc, D, 0)]
    return _rows(body, tiled, [], [(D, BF16)], [], tr=_tile(T, 384, 8), name=name, T=T)[0]


def _merge_bwd(dmixed, proj, off, pa, pb, pc, name):
    T, D = pa.shape

    def body(i, t, f, o, a):
        dm = t[0][...]
        for k in range(3):
            g = _sigmoid(t[1 + k][...])
            o[k][...] = (dm * g).astype(BF16)
            o[3][:, k * D:(k + 1) * D] = (dm * t[4 + k][...] * g * (1.0 - g)).astype(BF16)

    tiled = [(dmixed, D, 0), (proj, D, off), (proj, D, off + D), (proj, D, off + 2 * D), (pa, D, 0), (pb, D, 0),
             (pc, D, 0)]
    return _rows(body, tiled, [], [(D, BF16)] * 3 + [(3 * D, BF16)], [], tr=_tile(T, 384, 8), name=name, T=T)


def _gnorm_fwd(y, proj, zoff, nw, name):
    T, D = y.shape
    gs = D // SSD_GROUPS

    def body(i, t, f, o, a):
        s = t[0][...] * _silu(t[1][...])
        for g in range(SSD_GROUPS):
            sg = s[:, g * gs:(g + 1) * gs]
            r = lax.rsqrt(jnp.mean(sg * sg, axis=-1, keepdims=True) + NORM_EPS)
            o[0][:, g * gs:(g + 1) * gs] = (sg * r * f[0][:, g * gs:(g + 1) * gs]).astype(BF16)

    return _rows(body, [(y, D, 0), (proj, D, zoff)], [nw], [(D, BF16)], [], tr=_tile(T, 384, 8), name=name, T=T)[0]


def _gnorm_bwd(dout, y, proj, zoff, nw, name):
    T, D = y.shape
    gs = D // SSD_GROUPS

    def body(i, t, f, o, a):
        dov, yv, zv = t[0][...], t[1][...], t[2][...]
        sz = _silu(zv)
        s = yv * sz
        dsz = _dsilu(zv)
        for g in range(SSD_GROUPS):
            sl = slice(g * gs, (g + 1) * gs)
            sg = s[:, sl]
            r = lax.rsqrt(jnp.mean(sg * sg, axis=-1, keepdims=True) + NORM_EPS)
            sh = sg * r
            dog = dov[:, sl]
            dng = dog * f[0][:, sl]
            ds = r * (dng - sh * jnp.mean(dng * sh, axis=-1, keepdims=True))
            o[0][:, sl] = ds * sz[:, sl]
            o[1][:, sl] = (ds * yv[:, sl] * dsz[:, sl]).astype(BF16)
            a[0][:, sl] += _colsum(dog * sh)

    return _rows(body, [(dout, D, 0), (y, D, 0), (proj, D, zoff)], [nw], [(D, F32), (D, BF16)], [((1, D), F32)],
                 tr=_tile(T, 384, 8), name=name, T=T)


def _loss_bwd(h, tgt, g, seq_len, n_real, name):
    T, D = h.shape
    tr = _tile(seq_len, 384, 8)
    per_seq = seq_len // tr

    def body(i, t, f, o, a):
        x, tg = t[0][...], t[1][...]
        pos = (i % per_seq) * tr + lax.broadcasted_iota(jnp.int32, (tr, 1), 0)
        valid = (pos >= N_META) & (pos < N_META + n_real)
        r = lax.rsqrt(jnp.mean(x * x, axis=-1, keepdims=True) + NORM_EPS)
        xh = x * r
        e = jnp.where(valid, xh * f[0][...] - tg, 0.0)
        a[0][...] += jnp.zeros((1, LANES), F32) + 0.5 * jnp.sum(jnp.sum(e * e, axis=-1, keepdims=True) / D,
                                                              axis=0, keepdims=True)
        dy = e / D
        dng = dy * f[0][...]
        out = r * (dng - xh * jnp.mean(dng * xh, axis=-1, keepdims=True))
        o[0][...] = out
        o[1][...] = out.astype(BF16)
        a[1][...] += _colsum(dy * xh)

    return _rows(body, [(h, D, 0), (tgt, D, 0)], [g], [(D, F32), (D, BF16)], [((1, LANES), F32), ((1, D), F32)], tr=tr,
                 name=name, T=T)


def _lane_is_attn(shape):
    return lax.broadcasted_iota(jnp.int32, shape, len(shape) - 1) < HEADS


def _gate_prep(proj3, col_blk, bias, avec, name):
    B, L, _ = proj3.shape
    Q = Q_BLOCK
    nc = L // Q

    def kern(x_ref, b_ref, a_ref, v_ref, c_ref, carry):
        c = pl.program_id(1)

        @pl.when(c == 0)
        def _():
            carry[...] = jnp.zeros_like(carry)

        x = x_ref[0] + b_ref[...]
        attn = _lane_is_attn(x.shape)
        v = jnp.where(attn, _log_sigmoid(x), _softplus(x))
        w = jnp.where(attn, v, v * a_ref[...])
        cs = _split3_dot(_lower_tri(Q), w) + jnp.where(attn[:1], carry[...], 0.0)
        v_ref[0] = v
        c_ref[0] = cs
        rows = lax.broadcasted_iota(jnp.int32, (Q, 1), 0)
        carry[...] = jnp.sum(jnp.where(rows == Q - 1, cs, 0.0), axis=0, keepdims=True)

    blk = pl.BlockSpec((1, Q, LANES), lambda b, c: (b, c, 0))
    vec = pl.BlockSpec((1, LANES), lambda b, c: (0, 0))
    return pl.pallas_call(
        kern, name=name, grid=(B, nc),
        in_specs=[pl.BlockSpec((1, Q, LANES), lambda b, c: (b, c, col_blk)), vec, vec],
        out_specs=[blk, blk], out_shape=[jax.ShapeDtypeStruct((B, L, LANES), F32)] * 2,
        scratch_shapes=[pltpu.VMEM((1, LANES), F32)],
        compiler_params=_cparams(("parallel", "arbitrary")),
    )(proj3, bias, avec)


def _gate_post(drow, dcol, ddt, proj3, col_blk, vals, bias, avec, name):
    B, L, _ = proj3.shape
    Q = Q_BLOCK
    nc = L // Q

    def kern(dr_ref, dc_ref, dd_ref, x_ref, v_ref, b_ref, a_ref, o_ref, db_ref, da_ref, carry):
        b = pl.program_id(0)
        c = pl.program_id(1)

        @pl.when((b == 0) & (c == 0))
        def _():
            db_ref[...] = jnp.zeros_like(db_ref)
            da_ref[...] = jnp.zeros_like(da_ref)

        @pl.when(c == 0)
        def _():
            carry[...] = jnp.zeros_like(carry)

        x = x_ref[0] + b_ref[...]
        attn = _lane_is_attn(x.shape)
        dcs = dr_ref[0] + dc_ref[0]
        upper = jnp.logical_not(_lower_tri(Q, strict=True))
        rc = _split3_dot(upper, dcs) + jnp.where(attn[:1], carry[...], 0.0)
        rows = lax.broadcasted_iota(jnp.int32, (Q, 1), 0)
        carry[...] = jnp.sum(jnp.where(rows == 0, rc, 0.0), axis=0, keepdims=True)
        dv = jnp.where(attn, rc, dd_ref[0] + rc * a_ref[...])
        dpre = dv * jnp.where(attn, _sigmoid(-x), _sigmoid(x))
        o_ref[0] = dpre.astype(BF16)
        db_ref[...] += _colsum(dpre)
        da_ref[...] += _colsum(jnp.where(attn, 0.0, rc * v_ref[0]))

    rev = pl.BlockSpec((1, Q, LANES), lambda b, c: (b, nc - 1 - c, 0))
    vec = pl.BlockSpec((1, LANES), lambda b, c: (0, 0))
    return pl.pallas_call(
        kern, name=name, grid=(B, nc),
        in_specs=[rev, rev, rev, pl.BlockSpec((1, Q, LANES), lambda b, c: (b, nc - 1 - c, col_blk)), rev, vec, vec],
        out_specs=[rev, vec, vec],
        out_shape=[jax.ShapeDtypeStruct((B, L, LANES), BF16), jax.ShapeDtypeStruct((1, LANES), F32),
                   jax.ShapeDtypeStruct((1, LANES), F32)],
        scratch_shapes=[pltpu.VMEM((1, LANES), F32)],
        compiler_params=_cparams(("arbitrary", "arbitrary")),
    )(drow, dcol, ddt, proj3, vals, bias, avec)


def _lane_col(tile, lane):
    sel = lax.broadcasted_iota(jnp.int32, tile.shape, 1) == lane
    return jnp.sum(jnp.where(sel, tile, 0.0), axis=1, keepdims=True)


AUG = LANES
AUG_A = HEAD_DIM
AUG_B = HEAD_DIM + 3


def _split3(x):
    hi = x.astype(BF16).astype(F32)
    mid = (x - hi).astype(BF16).astype(F32)
    lo = (x - hi - mid).astype(BF16).astype(F32)
    return hi, mid, lo


def _put3(base, lane, first, x):
    hi, mid, lo = _split3(x)
    return jnp.where(lane == first, hi, jnp.where(lane == first + 1, mid, jnp.where(lane == first + 2, lo, base)))


HP = 2
AH = 2
AW = AH * HEAD_DIM


def _other_half(x):
    return pltpu.roll(x, HEAD_DIM, 1)


def _loop_by_twos(n, step, init):
    carry = lax.fori_loop(0, n // 4, lambda t, c: step(4 * t + 3, step(4 * t + 2, step(4 * t + 1, step(4 * t, c)))), init)
    carry = lax.cond(n % 4 >= 2, lambda c: step(n // 4 * 4 + 1, step(n // 4 * 4, c)), lambda c: c, carry)
    return lax.cond(n % 2 == 1, lambda c: step(n - 1, c), lambda c: c, carry)


def _attn_pack(proj3, cums, name):
    B, L, _ = proj3.shape
    D = HEADS * HEAD_DIM
    nh = HEADS // HP
    tr = _tile(L, 384)
    scale = HEAD_DIM ** -0.5

    def kern(q_ref, k_ref, v_ref, c_ref, qa_ref, ka_ref, va_ref):
        lane = lax.broadcasted_iota(jnp.int32, (tr, AUG), 1)
        head = lane < HEAD_DIM
        ones_a = jnp.where((lane >= AUG_A) & (lane < AUG_A + 3), 1.0, 0.0)
        ones_b = jnp.where((lane >= AUG_B) & (lane < AUG_B + 3), 1.0, 0.0)
        ct = c_ref[0]
        for hp in range(nh):
            cols = slice(hp * LANES, (hp + 1) * LANES)
            for hh in range(HP):
                h = HP * hp + hh
                c = _lane_col(ct, h)
                sel = (lambda t: t) if hh == 0 else _other_half
                qa_ref[0, h] = jnp.where(head, sel(q_ref[0, :, cols]) * scale, _put3(ones_b, lane, AUG_A, c)).astype(BF16)
                ka_ref[0, h] = jnp.where(head, sel(k_ref[0, :, cols]), _put3(ones_a, lane, AUG_B, -c)).astype(BF16)
                va_ref[0, h] = jnp.where(head, sel(v_ref[0, :, cols]), ones_a).astype(BF16)

    def win(k):
        return pl.BlockSpec((1, tr, D), lambda b, i: (b, i, k))

    out = pl.BlockSpec((1, HEADS, tr, AUG), lambda b, i: (b, 0, i, 0))
    return pl.pallas_call(
        kern, name=name, grid=(B, L // tr),
        in_specs=[win(0), win(1), win(2), pl.BlockSpec((1, tr, LANES), lambda b, i: (b, i, 0))],
        out_specs=[out] * 3, out_shape=[jax.ShapeDtypeStruct((B, HEADS, L, AUG), BF16)] * 3,
        compiler_params=_cparams(("parallel", "parallel")),
    )(proj3, proj3, proj3, cums)


def _attn_fwd(qa, ka, va, name, gather=None):
    B, H, L, _ = qa.shape
    tq = _tile(L, 384)
    nq = L // tq
    nh = H // AH
    comm = gather is not None

    def kern(*refs):
        if comm:
            q_ref, k_ref, v_ref, x_ref, y_ref, yb_ref, l_ref, g_ref, send_sems, recv_sems, local_sem = refs
        else:
            q_ref, k_ref, v_ref, y_ref, yb_ref, l_ref = refs
        qi = pl.program_id(2)
        if comm:
            _ride((pl.program_id(0) * nh + pl.program_id(1)) * nq + qi, B * nh * nq,
                  _gather_phases(x_ref, g_ref, send_sems, recv_sems, local_sem))
        qs = [q_ref[0, hh] for hh in range(AH)]
        causal = _lower_tri(tq)

        def step(j, carry, masked):
            rows = pl.ds(pl.multiple_of(j * tq, tq), tq)
            out = []
            for hh in range(AH):
                m, acc = carry[hh]
                s = _dot_nt(qs[hh], k_ref[0, hh, rows, :])
                if masked:
                    s = jnp.where(causal, s, NEG)
                m_new = jnp.maximum(m, jnp.max(s, axis=1, keepdims=True))
                p = jnp.exp(s - m_new)
                out.append((m_new, jnp.exp(m - m_new) * acc + _dot(p.astype(BF16), v_ref[0, hh, rows, :])))
            return tuple(out)

        init = tuple((jnp.full((tq, 1), NEG, F32), jnp.zeros((tq, AUG), F32)) for _ in range(AH))
        carry = _loop_by_twos(qi, lambda j, c: step(j, c, False), init)
        outs = []
        for hh, (m, acc) in enumerate(step(qi, carry, True)):
            l = _lane_col(acc, AUG_A)
            outs.append(acc / l)
            l_ref[0, hh] = m + jnp.log(l)
        head = lax.broadcasted_iota(jnp.int32, (tq, AUG), 1) < HEAD_DIM
        for pp in range(AH // HP):
            y = jnp.where(head, outs[HP * pp], _other_half(outs[HP * pp + 1]))
            y_ref[0, :, pp * LANES:(pp + 1) * LANES] = y
            yb_ref[0, :, pp * LANES:(pp + 1) * LANES] = y.astype(BF16)

    qspec = pl.BlockSpec((1, AH, tq, AUG), lambda b, h, i: (b, h, i, 0))
    kvspec = pl.BlockSpec((1, AH, L, AUG), lambda b, h, i: (b, h, 0, 0))
    lspec = pl.BlockSpec((1, AH, tq, 1), lambda b, h, i: (b, h, i, 0))
    yspec = pl.BlockSpec((1, tq, AW), lambda b, h, i: (b, i, h))
    out_shape = [jax.ShapeDtypeStruct((B, L, H * HEAD_DIM), F32), jax.ShapeDtypeStruct((B, L, H * HEAD_DIM), BF16),
                 jax.ShapeDtypeStruct((B, H, L, 1), F32)]
    if comm:
        out_shape.append(jax.ShapeDtypeStruct((N_DEV,) + gather.shape, gather.dtype))
    return pl.pallas_call(
        kern, name=name, grid=(B, nh, nq), in_specs=[qspec, kvspec, kvspec] + ([ANY] if comm else []),
        out_specs=[yspec, yspec, lspec] + ([ANY] if comm else []), out_shape=out_shape,
        scratch_shapes=COMM_SCRATCH if comm else [],
        compiler_params=_cparams(("arbitrary",) * 3 if comm else ("parallel", "parallel", "arbitrary")),
    )(qa, ka, va, *([gather] if comm else []))


def _attn_bwd(qa, ka, va, y, dy, lse, name, parts=None):
    B, H, L, _ = qa.shape
    tq = _tile(L, 384)
    nq = L // tq
    nh = H // AH
    comm = parts is not None
    scale = HEAD_DIM ** -0.5

    def kern(*refs):
        if comm:
            (q_ref, k_ref, v_ref, y_ref, dy_ref, l_ref, p_ref, dq_ref, dk_ref, dv_ref, dc_ref, r_ref,
             dk_acc, dv_acc, send_sems, recv_sems, local_sem) = refs
        else:
            q_ref, k_ref, v_ref, y_ref, dy_ref, l_ref, dq_ref, dk_ref, dv_ref, dc_ref, dk_acc, dv_acc = refs
        qi = pl.program_id(2)
        hp = pl.program_id(1)
        lane_row = lax.broadcasted_iota(jnp.int32, (1, LANES), 1)
        onehot = [(lane_row == AH * hp + hh).astype(F32) for hh in range(AH)]

        @pl.when((hp == 0) & (qi == 0))
        def _():
            dc_ref[...] = jnp.zeros_like(dc_ref)

        if comm:
            _ride((pl.program_id(0) * nh + pl.program_id(1)) * nq + qi, B * nh * nq,
                  _exchange_phases(p_ref, r_ref, send_sems, recv_sems, local_sem))

        @pl.when(qi == 0)
        def _():
            dk_acc[...] = jnp.zeros_like(dk_acc)
            dv_acc[...] = jnp.zeros_like(dv_acc)

        lane = lax.broadcasted_iota(jnp.int32, (tq, AUG), 1)
        head = lane < HEAD_DIM
        qbs, dobs = [], []
        for hh in range(AH):
            sel = (lambda t: t) if hh % HP == 0 else _other_half
            cols = slice((hh // HP) * LANES, (hh // HP + 1) * LANES)
            qf = q_ref[0, hh].astype(F32)
            dov = jnp.where(head, sel(dy_ref[0, :, cols]), 0.0)
            dsum = jnp.sum(dov * sel(y_ref[0, :, cols]), axis=1, keepdims=True)
            dobs.append(_put3(dov, lane, AUG_A, -dsum).astype(BF16))
            c_t = jnp.sum(jnp.where((lane >= AUG_A) & (lane < AUG_A + 3), qf, 0.0), axis=1, keepdims=True)
            qbs.append(_put3(qf, lane, AUG_A, c_t - l_ref[0, hh]).astype(BF16))
        causal = _lower_tri(tq)

        def step(j, dqs, masked):
            rows = pl.ds(pl.multiple_of(j * tq, tq), tq)
            out = []
            for hh in range(AH):
                kj = k_ref[0, hh, rows, :]
                s = _dot_nt(qbs[hh], kj)
                if masked:
                    s = jnp.where(causal, s, NEG)
                p = jnp.exp(s)
                ds = (p * _dot_nt(dobs[hh], v_ref[0, hh, rows, :])).astype(BF16)
                dv_acc[hh, rows, :] += _dot_tn(p.astype(BF16), dobs[hh])
                dk_acc[hh, rows, :] += _dot_tn(ds, qbs[hh])
                out.append(dqs[hh] + _dot(ds, kj))
            return tuple(out)

        dqs = _loop_by_twos(qi, lambda j, c: step(j, c, False), tuple(jnp.zeros((tq, AUG), F32) for _ in range(AH)))
        dqs = step(qi, dqs, True)
        dc_ref[0, pl.ds(pl.multiple_of(qi * tq, tq), tq), :] += sum(_lane_col(dqs[hh], AUG_A) * onehot[hh]
                                                                    for hh in range(AH))
        for pp in range(AH // HP):
            dq_ref[0, :, pp * LANES:(pp + 1) * LANES] = (
                jnp.where(head, dqs[HP * pp], _other_half(dqs[HP * pp + 1])) * scale).astype(BF16)

        @pl.when(qi == nq - 1)
        def _():
            full = lax.broadcasted_iota(jnp.int32, (L, AUG), 1) < HEAD_DIM
            for pp in range(AH // HP):
                cols = slice(pp * LANES, (pp + 1) * LANES)
                dk_ref[0, :, cols] = jnp.where(full, dk_acc[HP * pp], _other_half(dk_acc[HP * pp + 1])).astype(BF16)
                dv_ref[0, :, cols] = jnp.where(full, dv_acc[HP * pp], _other_half(dv_acc[HP * pp + 1])).astype(BF16)
            dc_ref[0] -= sum(_lane_col(dk_acc[hh], AUG_B) * onehot[hh] for hh in range(AH))

    qspec = pl.BlockSpec((1, AH, tq, AUG), lambda b, h, i: (b, h, i, 0))
    kvspec = pl.BlockSpec((1, AH, L, AUG), lambda b, h, i: (b, h, 0, 0))
    lspec = pl.BlockSpec((1, AH, tq, 1), lambda b, h, i: (b, h, i, 0))
    tmspec = pl.BlockSpec((1, L, LANES), lambda b, h, i: (b, 0, 0))
    yspec = pl.BlockSpec((1, tq, AW), lambda b, h, i: (b, i, h))
    yfull = pl.BlockSpec((1, L, AW), lambda b, h, i: (b, 0, h))
    nat = jax.ShapeDtypeStruct((B, L, H * HEAD_DIM), BF16)
    out_shape = [nat, nat, nat, jax.ShapeDtypeStruct((B, L, LANES), F32)]
    if comm:
        out_shape.append(jax.ShapeDtypeStruct(parts.shape, parts.dtype))
    return pl.pallas_call(
        kern, name=name, grid=(B, nh, nq),
        in_specs=[qspec, kvspec, kvspec, yspec, yspec, lspec] + ([ANY] if comm else []),
        out_specs=[yspec, yfull, yfull, tmspec] + ([ANY] if comm else []), out_shape=out_shape,
        scratch_shapes=[pltpu.VMEM((AH, L, AUG), F32), pltpu.VMEM((AH, L, AUG), F32)] + (COMM_SCRATCH if comm else []),
        compiler_params=_cparams(("parallel", "arbitrary", "arbitrary")),
    )(qa, ka, va, y, dy, lse, *([parts] if comm else []))


PAD = SUBLANES


def _halo_tile(x_ref, i, TR):
    r0 = pl.multiple_of(i * TR, TR)
    before = x_ref[0, pl.ds(pl.multiple_of(jnp.maximum(r0 - PAD, 0), PAD), PAD), :]
    return jnp.concatenate([jnp.where(i > 0, before, 0.0), x_ref[0, pl.ds(r0, TR), :]], axis=0)


def _conv_fwd(x3, x_blk, w, b, n_silu, name):
    B, L, _ = x3.shape
    C = w.shape[1]
    TR = _tile(L, 384, 8)

    def kern(x_ref, w_ref, b_ref, o_ref):
        cb = pl.program_id(1)

        def body(i, carry):
            r0 = pl.multiple_of(i * TR, TR)
            ext = _halo_tile(x_ref, i, TR)
            acc = jnp.zeros((TR, LANES), F32) + b_ref[...]
            for k in range(CONV_K):
                s = CONV_K - 1 - k
                sh = ext if s == 0 else pltpu.roll(ext, s, 0)
                acc = acc + w_ref[k:k + 1, :] * sh[PAD:PAD + TR]
            o_ref[0, pl.ds(r0, TR), :] = jnp.where(cb < n_silu, _silu(acc), acc)
            return carry

        lax.fori_loop(0, L // TR, body, 0)

    return pl.pallas_call(
        kern, name=name, grid=(B, C // LANES),
        in_specs=[pl.BlockSpec((1, L, LANES), lambda b_, c: (b_, 0, x_blk + c)),
                  pl.BlockSpec((CONV_K, LANES), lambda b_, c: (0, c)), pl.BlockSpec((1, LANES), lambda b_, c: (0, c))],
        out_specs=pl.BlockSpec((1, L, LANES), lambda b_, c: (b_, 0, c)),
        out_shape=jax.ShapeDtypeStruct((B, L, C), F32),
        compiler_params=_cparams(("parallel", "parallel")),
    )(x3, w, b)


def _conv_bwd(x3, x_blk, dxs, dBg, dCg, dxc, w, b, n_silu, name):
    B, L, D = dxs.shape
    G = dBg.shape[1]
    C = w.shape[1]
    nx = D // LANES
    TR = _tile(L, 384, 16)

    def kern(x_ref, s_ref, bg_ref, cg_ref, xc_ref, w_ref, b_ref, dx_ref, dw_ref, dp_s):
        cb = pl.program_id(1)

        def pre(i, carry):
            r0 = pl.multiple_of(i * TR, TR)
            ext = _halo_tile(x_ref, i, TR)
            taps = []
            acc = jnp.zeros((TR, LANES), F32) + b_ref[...]
            for k in range(CONV_K):
                s = CONV_K - 1 - k
                sh = ext if s == 0 else pltpu.roll(ext, s, 0)
                taps.append(sh[PAD:PAD + TR])
                acc = acc + w_ref[k:k + 1, :] * taps[-1]
            rows = pl.ds(r0, TR)
            dv = jnp.where(cb < nx, s_ref[0, rows, :],
                           jnp.where(cb < nx + G, bg_ref[0, 0, rows, :],
                                     jnp.where(cb < nx + 2 * G, cg_ref[0, 0, rows, :], xc_ref[0, rows, :])))
            dpre = jnp.where(cb < n_silu, dv * _dsilu(acc), dv)
            dp_s[rows, :] = dpre
            return tuple(c + _colsum(dpre * t) for c, t in zip(carry[:CONV_K], taps)) + (carry[CONV_K] + _colsum(dpre),)

        z = jnp.zeros((1, LANES), F32)
        sums = lax.fori_loop(0, L // TR, pre, (z,) * (CONV_K + 1))
        dp_s[pl.ds(L, PAD), :] = jnp.zeros((PAD, LANES), F32)
        dw_ref[0] = jnp.zeros((SUBLANES, LANES), F32)
        for k in range(CONV_K + 1):
            dw_ref[0, k:k + 1, :] = sums[k]

        def back(i, carry):
            r0 = pl.multiple_of(i * TR, TR)
            ext = dp_s[pl.ds(r0, TR + PAD), :]
            acc = jnp.zeros((TR, LANES), F32)
            for k in range(CONV_K):
                s = CONV_K - 1 - k
                sh = ext if s == 0 else pltpu.roll(ext, TR + PAD - s, 0)
                acc = acc + w_ref[k:k + 1, :] * sh[0:TR]
            dx_ref[0, pl.ds(r0, TR), :] = acc.astype(BF16)
            return carry

        lax.fori_loop(0, L // TR, back, 0)

    seq = pl.BlockSpec((1, L, LANES), lambda b_, c: (b_, 0, jnp.minimum(c, nx - 1)))
    grp = lambda first: pl.BlockSpec((1, 1, L, LANES), lambda b_, c: (b_, jnp.clip(c - first, 0, G - 1), 0, 0))
    tail = pl.BlockSpec((1, L, LANES), lambda b_, c: (b_, 0, jnp.clip(c - nx - 2 * G, 0, nx - 1)))
    return pl.pallas_call(
        kern, name=name, grid=(B, C // LANES),
        in_specs=[pl.BlockSpec((1, L, LANES), lambda b_, c: (b_, 0, x_blk + c)), seq, grp(nx), grp(nx + G), tail,
                  pl.BlockSpec((CONV_K, LANES), lambda b_, c: (0, c)), pl.BlockSpec((1, LANES), lambda b_, c: (0, c))],
        out_specs=[pl.BlockSpec((1, L, LANES), lambda b_, c: (b_, 0, c)),
                   pl.BlockSpec((1, SUBLANES, LANES), lambda b_, c: (b_, 0, c))],
        out_shape=[jax.ShapeDtypeStruct((B, L, C), BF16), jax.ShapeDtypeStruct((B, SUBLANES, C), F32)],
        scratch_shapes=[pltpu.VMEM((L + PAD, LANES), F32)],
        compiler_params=_cparams(("parallel", "parallel")),
    )(x3, dxs, dBg, dCg, dxc, w, b)


def _dot_nt(a, b):
    return lax.dot_general(a, b, (((1,), (1,)), ((), ())), preferred_element_type=F32)


def _dot_tn(a, b):
    return lax.dot_general(a, b, (((0,), (0,)), ((), ())), preferred_element_type=F32)


def _dot(a, b):
    return jnp.dot(a, b, preferred_element_type=F32)


def _ssd_specs(L, nc, b_blk, c_blk):
    pairs_per_group = HEADS // SSD_GROUPS // HP
    return [
        pl.BlockSpec((1, L, LANES), lambda b, h: (b, 0, h)),
        pl.BlockSpec((1, L, SSD_STATE), lambda b, h: (b, 0, b_blk + h // pairs_per_group)),
        pl.BlockSpec((1, L, SSD_STATE), lambda b, h: (b, 0, c_blk + h // pairs_per_group)),
        pl.BlockSpec((1, L, LANES), lambda b, h: (b, 0, 0)),
        pl.BlockSpec((1, L, LANES), lambda b, h: (b, 0, 0)),
        pl.BlockSpec((1, HP, nc, Q_BLOCK), lambda b, h: (b, HEADS // HP + h, 0, 0)),
        pl.BlockSpec((1, LANES), lambda b, h: (0, 0)),
    ]


SSD_UNROLL = 3


def _chunk_loop(nc, body, init):
    def trip(t, carry):
        for k in range(SSD_UNROLL):
            carry = body(SSD_UNROLL * t + k, carry)
        return carry

    carry = lax.fori_loop(0, nc // SSD_UNROLL, trip, init)
    for c in range(nc // SSD_UNROLL * SSD_UNROLL, nc):
        carry = body(jnp.int32(c), carry)
    return carry


def _halves(a, b, shape):
    return jnp.where(lax.broadcasted_iota(jnp.int32, shape, 1) < HEAD_DIM, a, b)


def _half_sums(t):
    first = lax.broadcasted_iota(jnp.int32, t.shape, 1) < HEAD_DIM
    lo = jnp.sum(jnp.where(first, t, 0.0), axis=1, keepdims=True)
    return lo, jnp.sum(t, axis=1, keepdims=True) - lo


def _ssd_chunk(c, S, x_ref, b_ref, c_ref, v_ref, cu_ref, ct_ref, lane0):
    Q = Q_BLOCK
    rows = pl.ds(pl.multiple_of(c * Q, Q), Q)
    x = x_ref[0, rows, :]
    Bb = b_ref[0, rows, :].astype(BF16)
    Cb = c_ref[0, rows, :].astype(BF16)
    vt, ct = v_ref[0, rows, :], cu_ref[0, rows, :]
    tri = _lower_tri(Q)
    A, Lm, e_end_h, eAend_h, dts = [], [], [], [], []
    for hh in range(HP):
        dts.append(_lane_col(vt, lane0 + hh))
        A.append(_lane_col(ct, lane0 + hh))
        Ar = ct_ref[0, hh, pl.ds(c, 1), :]
        Aend = _lane_col(Ar, Q - 1)
        Lm.append(jnp.exp(jnp.where(tri, A[hh] - Ar, NEG)))
        e_end_h.append(jnp.exp(Aend - A[hh]))
        eAend_h.append(jnp.exp(Aend))
    shape = (Q, LANES)
    dt = _halves(dts[0], dts[1], shape)
    eA = _halves(jnp.exp(A[0]), jnp.exp(A[1]), shape)
    e_end = _halves(e_end_h[0], e_end_h[1], shape)
    xdt = x * dt
    CB = _dot_nt(Cb, Bb)
    W = xdt * e_end
    srow = lax.broadcasted_iota(jnp.int32, (HP * HEAD_DIM, 1), 0) < HEAD_DIM
    eAend = jnp.where(srow, eAend_h[0], eAend_h[1])
    S_new = S * eAend + _dot_tn(W.astype(BF16), Bb)
    return dict(rows=rows, x=x, Bb=Bb, Cb=Cb, dt=dt, eA=eA, e_end=e_end, e_end_h=e_end_h, eAend=eAend,
                eAend_h=eAend_h, xdt=xdt, Lm=Lm, CB=CB, W=W, S_new=S_new)


def _ssd_fwd(u, b_blk, c_blk, vals, cums, cums_t, dvec, name):
    B, L, _ = u.shape
    nc = L // Q_BLOCK
    nh = HEADS // HP
    PP = HP * HEAD_DIM

    def kern(x_ref, b_ref, c_ref, v_ref, cu_ref, ct_ref, d_ref, y_ref, st_ref):
        lane0 = HEADS + HP * pl.program_id(1)
        dskip = _halves(_lane_col(d_ref[...], lane0), _lane_col(d_ref[...], lane0 + 1), (1, LANES))
        first = lax.broadcasted_iota(jnp.int32, (Q_BLOCK, LANES), 1) < HEAD_DIM

        def body(c, S):
            st_ref[0, 0, c] = S
            q = _ssd_chunk(c, S, x_ref, b_ref, c_ref, v_ref, cu_ref, ct_ref, lane0)
            xb = q["xdt"].astype(BF16)
            yd = jnp.where(first, _dot((q["CB"] * q["Lm"][0]).astype(BF16), xb),
                           _dot((q["CB"] * q["Lm"][1]).astype(BF16), xb))
            z = _dot_nt(q["Cb"], S.astype(BF16))
            y_ref[0, q["rows"], :] = yd + z * q["eA"] + dskip * q["x"]
            return q["S_new"]

        _chunk_loop(nc, body, jnp.zeros((HP * HEAD_DIM, SSD_STATE), F32))

    return pl.pallas_call(
        kern, name=name, grid=(B, nh), in_specs=_ssd_specs(L, nc, b_blk, c_blk),
        out_specs=[pl.BlockSpec((1, L, LANES), lambda b, h: (b, 0, h)),
                   pl.BlockSpec((1, 1, nc, PP, SSD_STATE), lambda b, h: (b, h, 0, 0, 0))],
        out_shape=[jax.ShapeDtypeStruct((B, L, HEADS * HEAD_DIM), F32),
                   jax.ShapeDtypeStruct((B, nh, nc, PP, SSD_STATE), F32)],
        compiler_params=_cparams(("parallel", "arbitrary")),
    )(u, u, u, vals, cums, cums_t, dvec)


def _ssd_bwd(u, b_blk, c_blk, vals, cums, cums_t, dvec, dy, states, name):
    B, L, _ = u.shape
    Q = Q_BLOCK
    nc = L // Q
    N = SSD_STATE
    nh = HEADS // HP
    pairs_per_group = HEADS // SSD_GROUPS // HP
    PP = HP * HEAD_DIM

    def kern(x_ref, b_ref, c_ref, v_ref, cu_ref, ct_ref, d_ref, dy_ref, st_ref,
             dx_ref, dB_ref, dC_ref, ddt_ref, dAc_ref, dAr_ref, dD_ref):
        b = pl.program_id(0)
        h = pl.program_id(1)
        lane0 = HEADS + HP * h
        dskip = _halves(_lane_col(d_ref[...], lane0), _lane_col(d_ref[...], lane0 + 1), (1, LANES))
        lane_row = lax.broadcasted_iota(jnp.int32, (1, LANES), 1)
        onehot = [(lane_row == lane0 + hh).astype(F32) for hh in range(HP)]

        @pl.when(h % pairs_per_group == 0)
        def _():
            dB_ref[...] = jnp.zeros_like(dB_ref)
            dC_ref[...] = jnp.zeros_like(dC_ref)

        @pl.when(h == 0)
        def _():
            ddt_ref[...] = jnp.zeros_like(ddt_ref)
            dAc_ref[...] = jnp.zeros_like(dAc_ref)

        @pl.when((b == 0) & (h == 0))
        def _():
            dD_ref[...] = jnp.zeros_like(dD_ref)

        last_row = lax.broadcasted_iota(jnp.int32, (Q, 1), 0) == Q - 1
        first = lax.broadcasted_iota(jnp.int32, (Q, LANES), 1) < HEAD_DIM
        srow = lax.broadcasted_iota(jnp.int32, (PP, 1), 0) < HEAD_DIM

        def bwd(i, carry):
            dS, dD = carry
            c = nc - 1 - i
            S = st_ref[0, 0, c]
            q = _ssd_chunk(c, S, x_ref, b_ref, c_ref, v_ref, cu_ref, ct_ref, lane0)
            rows, x, Bb, Cb, xdt, Lm, CB = q["rows"], q["x"], q["Bb"], q["Cb"], q["xdt"], q["Lm"], q["CB"]
            dy = dy_ref[0, rows, :]
            dyb = dy.astype(BF16)
            xb = xdt.astype(BF16)
            Sb = S.astype(BF16)
            dD = dD + _colsum(dy * x)
            dyh = [jnp.where(first, dy, 0.0).astype(BF16), jnp.where(first, 0.0, dy).astype(BF16)]
            dM = [_dot_nt(dyh[hh], xb) for hh in range(HP)]
            dxdt = jnp.where(first, _dot_tn((CB * Lm[0]).astype(BF16), dyb), _dot_tn((CB * Lm[1]).astype(BF16), dyb))
            dCBb = (dM[0] * Lm[0] + dM[1] * Lm[1]).astype(BF16)
            dAc, dAr = [], []
            for hh in range(HP):
                G = dM[hh] * CB * Lm[hh]
                dAc.append(jnp.sum(G, axis=1, keepdims=True))
                dAr.append(-jnp.sum(G, axis=0, keepdims=True))
            dC = _dot(dCBb, Bb)
            dBm = _dot_tn(dCBb, Cb)
            z = _dot_nt(Cb, Sb)
            zs = _half_sums(dy * z)
            dzb = (dy * q["eA"]).astype(BF16)
            dC = dC + _dot(dzb, Sb)
            dS_in = _dot_tn(dzb, Cb)
            dSb = dS.astype(BF16)
            dW = _dot_nt(Bb, dSb)
            dBm = dBm + _dot(q["W"].astype(BF16), dSb)
            dxdt = dxdt + dW * q["e_end"]
            des = _half_sums(dW * xdt)
            ss = jnp.sum(dS * S, axis=1, keepdims=True)
            ss_lo = jnp.sum(jnp.where(srow, ss, 0.0), axis=0, keepdims=True)
            ss_h = [ss_lo, jnp.sum(ss, axis=0, keepdims=True) - ss_lo]
            ddts = _half_sums(dxdt * x)
            eA_h = [_lane_col(q["eA"], 0), _lane_col(q["eA"], HEAD_DIM)]
            dAc_tile = jnp.zeros((Q, LANES), F32)
            ddt_tile = jnp.zeros((Q, LANES), F32)
            for hh in range(HP):
                de = des[hh] * q["e_end_h"][hh]
                dAend = ss_h[hh] * q["eAend_h"][hh] + jnp.sum(de, axis=0, keepdims=True)
                col = dAc[hh] + zs[hh] * eA_h[hh] - de + jnp.where(last_row, dAend, 0.0)
                dAc_tile = dAc_tile + col * onehot[hh]
                ddt_tile = ddt_tile + ddts[hh] * onehot[hh]
                dAr_ref[0, hh, pl.ds(c, 1), :] = dAr[hh]
            dx_ref[0, rows, :] = dskip * dy + dxdt * q["dt"]
            dB_ref[0, 0, rows, :] += dBm
            dC_ref[0, 0, rows, :] += dC
            ddt_ref[0, rows, :] += ddt_tile
            dAc_ref[0, rows, :] += dAc_tile
            return dS * q["eAend"] + dS_in, dD

        _, dD = _chunk_loop(nc, bwd, (jnp.zeros((PP, N), F32), jnp.zeros((1, LANES), F32)))
        dlo, dhi = _half_sums(dD)
        dD_ref[...] += dlo * onehot[0] + dhi * onehot[1]

    tm = pl.BlockSpec((1, L, LANES), lambda b, h: (b, 0, 0))
    grp = pl.BlockSpec((1, 1, L, N), lambda b, h: (b, h // pairs_per_group, 0, 0))
    xs = pl.BlockSpec((1, L, LANES), lambda b, h: (b, 0, h))
    return pl.pallas_call(
        kern, name=name, grid=(B, nh),
        in_specs=_ssd_specs(L, nc, b_blk, c_blk) + [xs, pl.BlockSpec((1, 1, nc, PP, N), lambda b, h: (b, h, 0, 0, 0))],
        out_specs=[xs, grp, grp, tm, tm, pl.BlockSpec((1, HP, nc, Q), lambda b, h: (b, h, 0, 0)),
                   pl.BlockSpec((1, LANES), lambda b, h: (0, 0))],
        out_shape=[jax.ShapeDtypeStruct((B, L, HEADS * HEAD_DIM), F32), jax.ShapeDtypeStruct((B, SSD_GROUPS, L, N), F32),
                   jax.ShapeDtypeStruct((B, SSD_GROUPS, L, N), F32), jax.ShapeDtypeStruct((B, L, LANES), F32),
                   jax.ShapeDtypeStruct((B, L, LANES), F32), jax.ShapeDtypeStruct((B, HEADS, nc, Q), F32),
                   jax.ShapeDtypeStruct((1, LANES), F32)],
        compiler_params=_cparams(("arbitrary", "arbitrary")),
    )(u, u, u, vals, cums, cums_t, dvec, dy, states)


LRU_TR = 384
LRU_CB = 512


def _lru_gates(xc, ra, ix, p_ref, first):
    r = _sigmoid(ra + p_ref[0:1, :])
    i = _sigmoid(ix + p_ref[1:2, :])
    ls = _log_sigmoid(p_ref[2:3, :])
    log_a = LRU_C * r * ls
    a = jnp.exp(log_a)
    mult0 = jnp.sqrt(_one_minus_exp(2.0 * log_a))
    mult = jnp.where(first, 1.0, mult0)
    return r, i, ls, a, mult0, mult


def _lru_fwd(u, xc_off, ra, ix, proj3, gate_off, pvec, name):
    B, L, D = ra.shape
    TR, CB = _tile(L, LRU_TR, 8), LRU_CB
    nrt = L // TR

    def kern(xc_ref, ra_ref, ix_ref, g_ref, p_ref, y_ref, hs_ref, a_ref, pa_s, pu_s, carry):
        rt = pl.program_id(2)

        @pl.when(rt == 0)
        def _():
            carry[...] = jnp.zeros_like(carry)

        row = lax.broadcasted_iota(jnp.int32, (TR, 1), 0)
        first = (rt == 0) & (row == 0)
        xc = xc_ref[0]
        r, i, ls, a, mult0, mult = _lru_gates(xc, ra_ref[0], ix_ref[0], p_ref, first)
        a_ref[0] = a
        pa, pu = a, mult * (i * xc)
        sub = row % SUBLANES
        for s in (1, 2, 4):
            ok = sub >= s
            pu = jnp.where(ok, pa * pltpu.roll(pu, s, 0) + pu, pu)
            pa = jnp.where(ok, pa * pltpu.roll(pa, s, 0), pa)
        pa_s[...] = pa
        pu_s[...] = pu
        row8 = lax.broadcasted_iota(jnp.int32, (SUBLANES, 1), 0)

        def gbody(g, c):
            r8 = pl.ds(pl.multiple_of(g * SUBLANES, SUBLANES), SUBLANES)
            hg = pa_s[r8, :] * c + pu_s[r8, :]
            hs_ref[0, r8, :] = hg
            return jnp.sum(jnp.where(row8 == SUBLANES - 1, hg, 0.0), axis=0, keepdims=True)

        carry[...] = lax.fori_loop(0, TR // SUBLANES, gbody, carry[...])
        y_ref[0] = (hs_ref[0] * _gelu(g_ref[0])).astype(BF16)

    def win(off):
        assert off % CB == 0
        return pl.BlockSpec((1, TR, CB), functools.partial(lambda b, j, t, o: (b, t, j + o), o=off // CB))

    return pl.pallas_call(
        kern, name=name, grid=(B, D // CB, nrt),
        in_specs=[win(xc_off), win(0), win(0), win(gate_off), pl.BlockSpec((SUBLANES, CB), lambda b, j, t: (0, j))],
        out_specs=[win(0)] * 3,
        out_shape=[jax.ShapeDtypeStruct((B, L, D), BF16), jax.ShapeDtypeStruct((B, L, D), F32),
                   jax.ShapeDtypeStruct((B, L, D), F32)],
        scratch_shapes=[pltpu.VMEM((TR, CB), F32), pltpu.VMEM((TR, CB), F32), pltpu.VMEM((1, CB), F32)],
        compiler_params=_cparams(("parallel", "parallel", "arbitrary")),
    )(u, ra, ix, proj3, pvec)


def _lru_bwd(dy, proj3, gate_off, hs, a, u, xc_off, ra, ix, pvec, name):
    B, L, D = ra.shape
    TR, CB = _tile(L, LRU_TR, 8), LRU_CB
    nrt = L // TR

    def kern(dy_ref, g_ref, hs_ref, hsp_ref, a_ref, an_ref, xc_ref, ra_ref, ix_ref, p_ref,
             dg_ref, dra_ref, dix_ref, dxc_ref, dp_ref, pb_s, pd_s, g_s, carry):
        b = pl.program_id(1)
        rt = pl.program_id(2)
        t = nrt - 1 - rt

        @pl.when((b == 0) & (rt == 0))
        def _():
            dp_ref[...] = jnp.zeros_like(dp_ref)

        @pl.when(rt == 0)
        def _():
            carry[...] = jnp.zeros_like(carry)

        row = lax.broadcasted_iota(jnp.int32, (TR, 1), 0)
        gate, hsv, av, dyv = g_ref[0], hs_ref[0], a_ref[0], dy_ref[0]
        dg_ref[0] = (dyv * hsv * _dgelu(gate)).astype(BF16)
        a_next = jnp.where(t == nrt - 1, 0.0, an_ref[0, 0:1, :])
        pb = jnp.where(row == TR - 1, a_next, pltpu.roll(av, TR - 1, 0))
        pd = dyv * _gelu(gate)
        sub = row % SUBLANES
        for s in (1, 2, 4):
            ok = sub < SUBLANES - s
            pd = jnp.where(ok, pd + pb * pltpu.roll(pd, TR - s, 0), pd)
            pb = jnp.where(ok, pb * pltpu.roll(pb, TR - s, 0), pb)
        pb_s[...] = pb
        pd_s[...] = pd
        row8 = lax.broadcasted_iota(jnp.int32, (SUBLANES, 1), 0)

        def gbody(i, c):
            r8 = pl.ds(pl.multiple_of((TR // SUBLANES - 1 - i) * SUBLANES, SUBLANES), SUBLANES)
            gg = pd_s[r8, :] + pb_s[r8, :] * c
            g_s[r8, :] = gg
            return jnp.sum(jnp.where(row8 == 0, gg, 0.0), axis=0, keepdims=True)

        carry[...] = lax.fori_loop(0, TR // SUBLANES, gbody, carry[...])
        gv = g_s[...]
        h_first = jnp.where(t == 0, 0.0, hsp_ref[0, SUBLANES - 1:SUBLANES, :])
        hprev = jnp.where(row == 0, h_first, pltpu.roll(hsv, 1, 0))
        first = (t == 0) & (row == 0)
        xc = xc_ref[0]
        r, i, ls, a2, mult0, mult = _lru_gates(xc, ra_ref[0], ix_ref[0], p_ref, first)
        dxc_ref[0] = gv * mult * i
        dlog_a = gv * hprev * av + jnp.where(first, 0.0, gv * i * xc * (-(av * av) / mult0))
        dra = dlog_a * LRU_C * ls * r * (1.0 - r)
        dix = gv * mult * xc * i * (1.0 - i)
        dra_ref[0] = dra.astype(BF16)
        dix_ref[0] = dix.astype(BF16)
        dp_ref[0:1, :] += _colsum(dra)
        dp_ref[1:2, :] += _colsum(dix)
        dp_ref[2:3, :] += _colsum(dlog_a * LRU_C * r) * _sigmoid(-p_ref[2:3, :])

    def win(off, shift=0):
        assert off % CB == 0
        o = off // CB
        return pl.BlockSpec((1, TR, CB), lambda j, b, rt: (b, jnp.clip(nrt - 1 - rt + shift, 0, nrt - 1), j + o))

    per_tile = TR // SUBLANES
    before = pl.BlockSpec((1, SUBLANES, CB), lambda j, b, rt: (b, jnp.maximum((nrt - 1 - rt) * per_tile - 1, 0), j))
    behind = pl.BlockSpec((1, SUBLANES, CB), lambda j, b, rt: (b, jnp.minimum((nrt - rt) * per_tile, nrt * per_tile - 1), j))

    return pl.pallas_call(
        kern, name=name, grid=(D // CB, B, nrt),
        in_specs=[win(0), win(gate_off), win(0), before, win(0), behind, win(xc_off), win(0), win(0),
                  pl.BlockSpec((SUBLANES, CB), lambda j, b, rt: (0, j))],
        out_specs=[win(0)] * 4 + [pl.BlockSpec((SUBLANES, CB), lambda j, b, rt: (0, j))],
        out_shape=[jax.ShapeDtypeStruct((B, L, D), BF16)] * 3 + [jax.ShapeDtypeStruct((B, L, D), F32),
                                                                 jax.ShapeDtypeStruct((SUBLANES, D), F32)],
        scratch_shapes=[pltpu.VMEM((TR, CB), F32)] * 3 + [pltpu.VMEM((1, CB), F32)],
        compiler_params=_cparams(("parallel", "arbitrary", "arbitrary")),
    )(dy, proj3, hs, hs, a, a, u, ra, ix, pvec)


def _sum8(parts, name):
    _, R, C = parts.shape
    tr = _tile(R, 1024, ROW_ALIGN if parts.dtype.itemsize == 2 else SUBLANES)

    def kern(p_ref, o_ref):
        acc = p_ref[0].astype(F32)
        for d in range(1, N_DEV):
            acc = acc + p_ref[d].astype(F32)
        o_ref[...] = acc

    return pl.pallas_call(
        kern, name=name, grid=(R // tr,), in_specs=[pl.BlockSpec((N_DEV, tr, C), lambda i: (0, i, 0))],
        out_specs=pl.BlockSpec((tr, C), lambda i: (i, 0)), out_shape=jax.ShapeDtypeStruct((R, C), F32),
        compiler_params=_cparams(("parallel",)),
    )(parts)


def _adamw(w, g, m, v, name):
    shape = w.shape
    C = shape[-1] if w.ndim > 1 else shape[0]
    R = w.size // C
    w2, g2, m2, v2 = (t.reshape(R, C) for t in (w, g, m, v))
    tr = R
    for cand in range(8, min(R, 512) + 1, 8):
        if R % cand == 0:
            tr = cand

    def kern(w_ref, g_ref, m_ref, v_ref, d_ref, nm_ref, nv_ref):
        gv = g_ref[...]
        nm = ADAM_B1 * m_ref[...] + (1.0 - ADAM_B1) * gv
        nv = ADAM_B2 * v_ref[...] + (1.0 - ADAM_B2) * (gv * gv)
        m_hat = nm / (1.0 - ADAM_B1 ** ADAM_STEP)
        v_hat = nv / (1.0 - ADAM_B2 ** ADAM_STEP)
        d_ref[...] = -ADAM_LR * (m_hat / (jnp.sqrt(v_hat) + ADAM_EPS) + ADAM_WD * w_ref[...])
        nm_ref[...] = nm
        nv_ref[...] = nv

    spec = pl.BlockSpec((tr, C), lambda i: (i, 0))
    outs = pl.pallas_call(
        kern, name=name, grid=(R // tr,), in_specs=[spec] * 4, out_specs=[spec] * 3,
        out_shape=[jax.ShapeDtypeStruct((R, C), F32)] * 3, compiler_params=_cparams(("parallel",)),
    )(w2, g2, m2, v2)
    return tuple(o.reshape(shape) for o in outs)


MESH_ID = pl.DeviceIdType.MESH
ANY = pl.BlockSpec(memory_space=pl.ANY)
N_COPIES = N_DEV - 1
COMM_SCRATCH = [pltpu.SemaphoreType.DMA((N_COPIES,)), pltpu.SemaphoreType.DMA((N_COPIES,)), pltpu.SemaphoreType.DMA]


def _my_place():
    return lax.axis_index("x"), lax.axis_index("y"), lax.axis_index("c")


def _gather_phases(x_ref, out_ref, send_sems, recv_sems, local_sem):
    x, y, c = _my_place()
    me, sibling = (x, y, c), (x, y, 1 - c)
    chips = [(1 - x, y), (x, 1 - y), (1 - x, 1 - y)]

    def slab(px, py, pc):
        return out_ref.at[4 * px + 2 * py + pc]

    def copy(k, block, to, src=None):
        return pltpu.make_async_remote_copy(
            src_ref=slab(*block) if src is None else src, dst_ref=slab(*block),
            send_sem=send_sems.at[k], recv_sem=recv_sems.at[k], device_id=to, device_id_type=MESH_ID)

    mine = pltpu.make_async_copy(x_ref, slab(*me), local_sem)
    first = [copy(0, me, sibling, src=x_ref)] + [copy(1 + j, me, (*chip, c), src=x_ref) for j, chip in enumerate(chips)]
    passed = [copy(4 + j, (*chip, c), sibling) for j, chip in enumerate(chips)]

    def start():
        mine.start()
        for cp in first:
            cp.start()

    def forward():
        for j, chip in enumerate(chips):
            copy(1 + j, (*chip, c), me).wait_recv()
            passed[j].start()

    def finish():
        copy(0, sibling, me).wait_recv()
        for j, chip in enumerate(chips):
            copy(4 + j, (*chip, 1 - c), me).wait_recv()
        for cp in first + passed:
            cp.wait_send()
        mine.wait()

    return start, forward, finish


def _exchange_phases(p_ref, out_ref, send_sems, recv_sems, local_sem):
    x, y, c = _my_place()
    my_idx = 4 * x + 2 * y + c
    mine = pltpu.make_async_copy(p_ref.at[my_idx], out_ref.at[my_idx], local_sem)
    copies = []
    for k in range(1, N_DEV):
        px, py, pc = x ^ (k >> 2), y ^ ((k >> 1) & 1), c ^ (k & 1)
        copies.append(pltpu.make_async_remote_copy(
            src_ref=p_ref.at[4 * px + 2 * py + pc], dst_ref=out_ref.at[my_idx],
            send_sem=send_sems.at[k - 1], recv_sem=recv_sems.at[k - 1], device_id=(px, py, pc),
            device_id_type=MESH_ID))

    def start():
        mine.start()
        for cp in copies:
            cp.start()

    def finish():
        for cp in copies:
            cp.wait()
        mine.wait()

    return start, finish


def _ride(lin, total, phases):
    assert total >= 3
    marks = [0, total - 1] if len(phases) == 2 else [0, total // 2, total - 1]
    for mark, phase in zip(marks, phases):
        pl.when(lin == mark)(phase)


def _all_gather(xs, name):
    R, C = xs.shape

    def body(x_ref, out_ref, send_sems, recv_sems, local_sem):
        for phase in _gather_phases(x_ref, out_ref, send_sems, recv_sems, local_sem):
            phase()

    return pl.pallas_call(
        body, name=name, out_shape=jax.ShapeDtypeStruct((N_DEV, R, C), xs.dtype), in_specs=[ANY], out_specs=ANY,
        scratch_shapes=COMM_SCRATCH,
    )(xs)


def _exchange(parts, name):
    def body(p_ref, out_ref, send_sems, recv_sems, local_sem):
        for phase in _exchange_phases(p_ref, out_ref, send_sems, recv_sems, local_sem):
            phase()

    return pl.pallas_call(
        body, name=name, out_shape=jax.ShapeDtypeStruct(parts.shape, parts.dtype), in_specs=[ANY], out_specs=ANY,
        scratch_shapes=COMM_SCRATCH,
    )(parts)


D_XBC_EXTRA = 2 * SSD_GROUPS * SSD_STATE
SMALL_W = LANES
ROW_ALIGN = 16


def _layout(D):
    d_xbc = D + D_XBC_EXTRA
    off = dict(qkv=0, z=3 * D, merge=4 * D, gate=7 * D, conv=8 * D, xr=8 * D + d_xbc, small=9 * D + d_xbc)
    off["n_all"] = off["small"] + SMALL_W
    off["d_xbc"] = d_xbc
    off["conv_c"] = d_xbc + D
    return off


def _w_in_map(D):
    lo = _layout(D)
    widths = [("q", D, 0), ("k", D, D), ("v", D, 2 * D), ("f", HEADS, lo["small"]), ("z", D, lo["z"]),
              ("xbc", lo["d_xbc"], lo["conv"]), ("dt", HEADS, lo["small"] + HEADS), ("xr", D, lo["xr"]),
              ("gate", D, lo["gate"]), ("merge", 3 * D, lo["merge"])]
    out, o = [], 0
    for _, w, mine in widths:
        out.append((o, w, mine))
        o += w
    return out


def _padded(c):
    return -(-c // ROW_ALIGN) * ROW_ALIGN


def _permute_rows(src, pieces, name):
    R, C = src.shape
    n_out = sum(n for _, n in pieces)

    def kern(x_ref, o_ref):
        o = 0
        for start, n in pieces:
            if start is None:
                o_ref[o:o + n, :] = jnp.zeros((n, LANES), src.dtype)
            else:
                o_ref[o:o + n, :] = x_ref[start:start + n, :]
            o += n

    return pl.pallas_call(
        kern, name=name, grid=(C // LANES,), in_specs=[pl.BlockSpec((R, LANES), lambda i: (0, i))],
        out_specs=pl.BlockSpec((n_out, LANES), lambda i: (0, i)), out_shape=jax.ShapeDtypeStruct((n_out, C), src.dtype),
        compiler_params=_cparams(("parallel",)),
    )(src)


def _reorder_rows(wt, D, c, name="reorder_w_in"):
    cp = _padded(c)
    lo = _layout(D)
    pieces = []
    for a, w, mine in sorted(_w_in_map(D), key=lambda t: t[2]):
        b = a + w
        while a < b:
            j = a // c
            e = min(b, (j + 1) * c)
            pieces.append((j * cp + a - j * c, e - a))
            a = e
    pieces.append((None, lo["n_all"] - lo["small"] - 2 * HEADS))
    return _permute_rows(wt, pieces, name)


def _restore_rows(dwt, D, c, name="restore_w_in"):
    cp = _padded(c)
    segs = _w_in_map(D)
    pieces = []
    for j in range(N_DEV):
        a, b = j * c, (j + 1) * c
        for s0, w, mine in segs:
            lo_, hi_ = max(a, s0), min(b, s0 + w)
            if lo_ < hi_:
                pieces.append((mine + lo_ - s0, hi_ - lo_))
        if cp > c:
            pieces.append((None, cp - c))
    return _permute_rows(dwt, pieces, name)


def _block_diag(w):
    H, n, _ = w.shape
    tiled = jnp.tile(w.reshape(H * n, n), (1, H))
    r = lax.broadcasted_iota(jnp.int32, (H * n, H * n), 0) // n
    c = lax.broadcasted_iota(jnp.int32, (H * n, H * n), 1) // n
    return jnp.where(r == c, tiled, jnp.zeros_like(tiled))


def _diag_blocks(m, H):
    n = m.shape[0] // H
    keep = jnp.eye(H, dtype=m.dtype)[:, None, :, None]
    return jnp.sum(m.reshape(H, n, H, n) * keep, axis=2)


def _to_heads(t, B, L):
    return t.reshape(B, L, HEADS, HEAD_DIM).transpose(0, 2, 1, 3)


def _from_heads(t4):
    B, H, L, P = t4.shape
    return t4.transpose(0, 2, 1, 3).reshape(B * L, H * P)


def _rows_to_tm(rows):
    B, H, nc, Q = rows.shape
    return rows.reshape(B, H, nc * Q).transpose(0, 2, 1)


def _ffn_fwd(h, g, wgu_t, wd, tag):
    n = _norm_fwd(h, g, tag + "_norm")
    gate, up, act = _ffn_up_act(n, wgu_t, tag + "_up")
    out = _mm(act, wd, res=h, scale=0.5, name=tag + "_down")
    return out, (h, n, gate, up, act)


def _ffn_bwd(dh, dhb, saved, g, wgu_t, wd, tag):
    h, n, gate, up, act = saved
    dgu = _ffn_down_dx_act(dhb, wd, gate, up, tag + "_down_dx")
    dwd = _mm(act, dhb, ta=True, scale=0.5, name=tag + "_down_dw")
    dwgu_t = _mm(dgu, n, ta=True, tn=1024, name=tag + "_up_dw")
    dn = _mm(dgu, wgu_t, name=tag + "_up_dx")
    dh_in, dhb_in, dg = _norm_bwd(h, dn, dh, g, tag + "_norm_bwd")
    return dh_in, dhb_in, dict(norm=dg, gu=dwgu_t, down=dwd)


def _mixer_fwd(h, p, B, L, gather=None):
    T, D = h.shape
    lo = _layout(D)
    n = _norm_fwd(h, p["gm"], "mix_norm")
    proj = _mm(n, p["w_all_t"], tb=True, name="mix_in")
    proj3 = proj.reshape(B, L, lo["n_all"])
    vals, cums = _gate_prep(proj3, lo["small"] // LANES, p["small_bias"], p["avec"], "gate_prep")
    cums_t = cums[..., :2 * HEADS].transpose(0, 2, 1).reshape(B, 2 * HEADS, L // Q_BLOCK, Q_BLOCK)
    qa, ka, va = _attn_pack(proj3, cums, "attn_pack")
    y_a3, y_ab3, lse, *gathered = _attn_fwd(qa, ka, va, "attn_fwd", gather)
    y_a, y_ab = y_a3.reshape(T, D), y_ab3.reshape(T, D)
    u = _conv_fwd(proj3, lo["conv"] // LANES, p["conv_w"], p["conv_b"], lo["d_xbc"] // LANES, "conv_fwd")
    b_blk = D // LANES
    c_blk = b_blk + SSD_GROUPS * SSD_STATE // LANES
    y_s3, ssd_states = _ssd_fwd(u, b_blk, c_blk, vals, cums, cums_t, p["dvec"], "ssd_fwd")
    y_s = y_s3.reshape(T, D)
    yb = _gnorm_fwd(y_s, proj, lo["z"], p["ssd_norm"], "gnorm_fwd")
    u2 = u.reshape(T, lo["conv_c"])
    ra = _mm(u2, p["wa"], a_off=(0, lo["d_xbc"]), dims=(T, D, D), tk=512, name="lru_ra")
    ix = _mm(u2, p["wx"], a_off=(0, lo["d_xbc"]), dims=(T, D, D), tk=512, name="lru_ix")
    yc, hs, a = _lru_fwd(u, lo["d_xbc"], ra.reshape(B, L, D), ix.reshape(B, L, D), proj3, lo["gate"], p["pvec"],
                         "lru_fwd")
    yc = yc.reshape(T, D)
    pa = _mm(y_ab, p["wba"], name="branch_attn")
    pb = _mm(yb, p["wbs"], name="branch_ssd")
    pc = _mm(yc, p["wbl"], name="branch_lru")
    mixed = _merge_fwd(proj, lo["merge"], pa, pb, pc, "merge_fwd")
    out = _mm(mixed, p["wout"], res=h, name="mix_out")
    saved = dict(h=h, n=n, proj=proj, qa=qa, ka=ka, va=va, vals=vals, cums=cums, cums_t=cums_t, lse=lse, y_a=y_a, y_ab=y_ab,
                 u=u, y_s=y_s, ssd_states=ssd_states, yb=yb, ra=ra, ix=ix, yc=yc, hs=hs, a=a, pa=pa, pb=pb, pc=pc, mixed=mixed)
    return out, saved, (gathered[0] if gathered else None)


def _mixer_bwd(dh, dhb, s, p, B, L, parts=None):
    T, D = dh.shape
    lo = _layout(D)
    proj, u = s["proj"], s["u"]
    proj3 = proj.reshape(B, L, lo["n_all"])
    g = {}
    dmixed = _mm(dhb, p["wout"], tb=True, name="mix_out_dx")
    g["wout"] = _mm(s["mixed"], dhb, ta=True, name="mix_out_dw")
    dpa, dpb, dpc, dmerge = _merge_bwd(dmixed, proj, lo["merge"], s["pa"], s["pb"], s["pc"], "merge_bwd")
    dy_a = _mm(dpa, p["wba"], tb=True, name="branch_attn_dx")
    g["wba"] = _mm(s["y_ab"], dpa, ta=True, name="branch_attn_dw")
    dyb = _mm(dpb, p["wbs"], tb=True, name="branch_ssd_dx")
    g["wbs"] = _mm(s["yb"], dpb, ta=True, name="branch_ssd_dw")
    dyc = _mm(dpc, p["wbl"], tb=True, name="branch_lru_dx")
    g["wbl"] = _mm(s["yc"], dpc, ta=True, name="branch_lru_dw")
    dgate, dra, dix, dxc, g["pvec"] = _lru_bwd(dyc.reshape(B, L, D), proj3, lo["gate"], s["hs"], s["a"], u, lo["d_xbc"],
                                               s["ra"].reshape(B, L, D), s["ix"].reshape(B, L, D), p["pvec"], "lru_bwd")
    dra, dix = dra.reshape(T, D), dix.reshape(T, D)
    u2 = u.reshape(T, lo["conv_c"])
    g["wa"] = _mm(u2, dra, ta=True, a_off=(0, lo["d_xbc"]), dims=(D, D, T), tm=512, name="lru_ra_dw")
    g["wx"] = _mm(u2, dix, ta=True, a_off=(0, lo["d_xbc"]), dims=(D, D, T), tm=512, name="lru_ix_dw")
    dxc = _mm(dra, p["wa"], tb=True, res=dxc.reshape(T, D), name="lru_ra_dx")
    dxc = _mm(dix, p["wx"], tb=True, res=dxc, name="lru_ix_dx")
    dy_s, dz, g["ssd_norm"] = _gnorm_bwd(dyb, s["y_s"], proj, lo["z"], p["ssd_norm"], "gnorm_bwd")
    b_blk = D // LANES
    c_blk = b_blk + SSD_GROUPS * SSD_STATE // LANES
    dxs, dBg, dCg, ddt_tm, dAc_tm, dAr, g["dvec"] = _ssd_bwd(u, b_blk, c_blk, s["vals"], s["cums"], s["cums_t"],
                                                             p["dvec"], dy_s.reshape(B, L, D), s["ssd_states"], "ssd_bwd")
    dconv, conv_wb = _conv_bwd(proj3, lo["conv"] // LANES, dxs, dBg, dCg, dxc.reshape(B, L, D), p["conv_w"], p["conv_b"],
                               lo["d_xbc"] // LANES, "conv_bwd")
    g["conv_wb"] = jnp.sum(conv_wb, axis=0)
    dq3, dk3, dv3, dc_tm, *recv = _attn_bwd(s["qa"], s["ka"], s["va"], s["y_a"].reshape(B, L, D),
                                            dy_a.reshape(B, L, D), s["lse"], "attn_bwd", parts)
    drow_tm = dc_tm + jnp.pad(_rows_to_tm(dAr), ((0, 0), (0, 0), (HEADS, LANES - 2 * HEADS)))
    dsmall, g["small_bias"], g["avec"] = _gate_post(drow_tm, dAc_tm, ddt_tm, proj3, lo["small"] // LANES, s["vals"],
                                                    p["small_bias"], p["avec"], "gate_post")
    dproj = jnp.concatenate([dq3.reshape(T, D), dk3.reshape(T, D), dv3.reshape(T, D), dz, dmerge, dgate.reshape(T, D), dconv.reshape(T, lo["conv_c"]),
                             dsmall.reshape(T, SMALL_W)], axis=1)
    g["w_all_t"] = _mm(dproj, s["n"], ta=True, tn=1024, name="mix_in_dw")
    dn = _mm(dproj, p["w_all_t"], name="mix_in_dx")
    dh_in, dhb_in, g["gm"] = _norm_bwd(s["h"], dn, dh, p["gm"], "mix_norm_bwd")
    return dh_in, dhb_in, g, (recv[0] if recv else None)


def _small_vec(a, b):
    return jnp.concatenate([a, b, jnp.zeros((LANES - 2 * HEADS,), F32)])[None, :]


def _layer_params(w):
    zeros16 = jnp.zeros((HEADS,), F32)
    pvec = jnp.concatenate([w["lru_b_a"][None], w["lru_b_x"][None], w["lru_lambda"][None],
                            jnp.zeros((SUBLANES - 3, w["lru_b_a"].shape[0]), F32)], axis=0)
    return dict(
        g1=w["ffn1_norm"][None], gu1=w["ffn1_w_gate_up"], d1=w["ffn1_w_down"],
        gm=w["mix_norm"][None], w_all_t=w["w_in"],
        small_bias=_small_vec(w["fox_forget_bias"], w["ssd_dt_bias"]),
        avec=_small_vec(zeros16, -jnp.exp(w["ssd_a_log"])), dvec=_small_vec(zeros16, w["ssd_d"]),
        conv_w=jnp.concatenate([w["ssd_conv_w"], w["lru_conv_w"]], axis=1),
        conv_b=jnp.concatenate([w["ssd_conv_b"], w["lru_conv_b"]])[None],
        ssd_norm=w["ssd_norm"][None],
        wa=_block_diag(w["lru_w_a"]).astype(BF16), wx=_block_diag(w["lru_w_x"]).astype(BF16), pvec=pvec,
        wba=w["w_branch_attn"], wbs=w["w_branch_ssd"], wbl=w["w_branch_lru"], wout=w["w_out"],
        g2=w["ffn2_norm"][None], gu2=w["ffn2_w_gate_up"], d2=w["ffn2_w_down"],
    )


def _layer_fwd(h, p, B, L, gather=None):
    h, s1 = _ffn_fwd(h, p["g1"], p["gu1"], p["d1"], "ffn1")
    h, sm, gathered = _mixer_fwd(h, p, B, L, gather)
    h, s2 = _ffn_fwd(h, p["g2"], p["gu2"], p["d2"], "ffn2")
    return h, (s1, sm, s2), gathered


def _layer_bwd(dh, dhb, saved, p, w, B, L, parts=None):
    s1, sm, s2 = saved
    D = dh.shape[1]
    d_xbc = D + D_XBC_EXTRA
    dh, dhb, f2 = _ffn_bwd(dh, dhb, s2, p["g2"], p["gu2"], p["d2"], "ffn2")
    dh, dhb, gm, recv = _mixer_bwd(dh, dhb, sm, p, B, L, parts)
    dh, dhb, f1 = _ffn_bwd(dh, dhb, s1, p["g1"], p["gu1"], p["d1"], "ffn1")
    sb, av = gm["small_bias"][0], gm["avec"][0]
    cw = gm["conv_wb"]
    grads = dict(
        ffn1_norm=f1["norm"][0], ffn1_w_gate_up=f1["gu"], ffn1_w_down=f1["down"],
        mix_norm=gm["gm"][0], w_in=gm["w_all_t"],
        fox_forget_bias=sb[:HEADS], ssd_conv_w=cw[:CONV_K, :d_xbc], ssd_conv_b=cw[CONV_K, :d_xbc],
        ssd_dt_bias=sb[HEADS:2 * HEADS], ssd_a_log=av[HEADS:2 * HEADS] * (-jnp.exp(w["ssd_a_log"])),
        ssd_d=gm["dvec"][0, HEADS:2 * HEADS], ssd_norm=gm["ssd_norm"][0],
        lru_conv_w=cw[:CONV_K, d_xbc:], lru_conv_b=cw[CONV_K, d_xbc:],
        lru_w_a=_diag_blocks(gm["wa"], HEADS), lru_b_a=gm["pvec"][0], lru_w_x=_diag_blocks(gm["wx"], HEADS),
        lru_b_x=gm["pvec"][1], lru_lambda=gm["pvec"][2],
        w_branch_attn=gm["wba"], w_branch_ssd=gm["wbs"], w_branch_lru=gm["wbl"], w_out=gm["wout"],
        ffn2_norm=f2["norm"][0], ffn2_w_gate_up=f2["gu"], ffn2_w_down=f2["down"],
    )
    return dh, dhb, grads, recv


LAYER_NAMES = ["ffn1_norm", "ffn1_w_gate_up", "ffn1_w_down", "mix_norm", "w_in", "fox_forget_bias", "ssd_conv_w",
               "ssd_conv_b", "ssd_dt_bias", "ssd_a_log", "ssd_d", "ssd_norm", "lru_conv_w", "lru_conv_b", "lru_w_a",
               "lru_b_a", "lru_w_x", "lru_b_x", "lru_lambda", "w_branch_attn", "w_branch_ssd", "w_branch_lru", "w_out",
               "ffn2_norm", "ffn2_w_gate_up", "ffn2_w_down"]
WEIGHT_NAMES = ["meta_tokens"] + LAYER_NAMES + ["final_norm"]


def _local_step(x, target, meta, final_norm, depth, layer_weights, pack_next=None, pack_grads=None):
    B, S, D = x.shape
    L = -(-(N_META + S) // Q_BLOCK) * Q_BLOCK
    h = jnp.concatenate([jnp.broadcast_to(meta[None], (B, N_META, D)), x,
                         jnp.zeros((B, L - N_META - S, D), F32)], axis=1).reshape(B * L, D)
    weights, params, saved = [], [], []
    gathered = None
    for l in range(depth):
        w = layer_weights(l, gathered)
        p = _layer_params(w)
        nxt = pack_next(l + 1) if (pack_next is not None and l + 1 < depth) else None
        h, s, gathered = _layer_fwd(h, p, B, L, nxt)
        weights.append(w)
        params.append(p)
        saved.append(s)
    tgt = jnp.pad(target, ((0, 0), (N_META, L - N_META - S), (0, 0))).reshape(B * L, D)
    dh, dhb, loss, dfinal = _loss_bwd(h, tgt, final_norm[None], L, S, "loss")
    grads = [None] * depth
    received, parts = {}, None
    for l in reversed(range(depth)):
        dh, dhb, grads[l], recv = _layer_bwd(dh, dhb, saved[l], params[l], weights[l], B, L, parts)
        if recv is not None:
            received[l + 1] = recv
        parts = pack_grads(grads[l]) if pack_grads is not None else None
    dh3 = dh.reshape(B, L, D)
    return (loss, dh3[:, N_META:N_META + S], jnp.sum(dh3[:, :N_META], axis=0), grads, dfinal[0], received, parts)


BIG_NAMES = ["ffn1_w_gate_up", "ffn1_w_down", "w_in", "w_branch_attn", "w_branch_ssd", "w_branch_lru", "w_out",
             "ffn2_w_gate_up", "ffn2_w_down"]
COL_SHARDED = {"ffn1_w_gate_up", "w_in", "ffn2_w_gate_up"}
SMALL_SHARDED = ["meta_tokens", "ssd_conv_w", "lru_conv_w"]
SMALL_NAMES = [n for n in LAYER_NAMES if n not in BIG_NAMES]


def _shard_rows(name, shape):
    return _padded(shape[1]) if name in COL_SHARDED else shape[0]


def _pack_shards(shards):
    rows = []
    for n in BIG_NAMES:
        s = shards[n]
        if n in COL_SHARDED:
            s = jnp.pad(s.T, ((0, _padded(s.shape[1]) - s.shape[1]), (0, 0)))
        rows.append(s)
    return jnp.concatenate(rows, axis=0)


def _unpack_gathered(gathered, shapes, D):
    out, o = {}, 0
    for n in BIG_NAMES:
        r = _shard_rows(n, shapes[n])
        out[n] = gathered[:, o:o + r].reshape(N_DEV * r, D)
        o += r
    out["w_in"] = _reorder_rows(out["w_in"], D, shapes["w_in"][1])
    return out


def _pack_full_grads(grads, shapes, D):
    slabs = []
    for n in BIG_NAMES:
        g = grads[n]
        if n == "w_in":
            g = _restore_rows(g, D, shapes[n][1])
        slabs.append(g.reshape(N_DEV, _shard_rows(n, shapes[n]), D))
    return jnp.concatenate(slabs, axis=1)


def _unpack_local(rows, shapes):
    out, o = {}, 0
    for n in BIG_NAMES:
        r = _shard_rows(n, shapes[n])
        blk = rows[o:o + r]
        out[n] = blk[:shapes[n][1]].T if n in COL_SHARDED else blk
        o += r
    return out


def _as_rows(flat):
    n = flat.shape[0]
    unit = LANES * SUBLANES
    total = -(-n // unit) * unit
    return jnp.pad(flat, (0, total - n)).reshape(total // LANES, LANES)


def _flatten_list(arrs):
    return _as_rows(jnp.concatenate([a.reshape(-1) for a in arrs]))


def _split_like(rows, shapes):
    flat = rows.reshape(-1)
    out, o = [], 0
    for s in shapes:
        n = math.prod(s)
        out.append(flat[o:o + n].reshape(s))
        o += n
    return out


def _gather_last(rows8, shape):
    lead, c = shape[:-1], shape[-1]
    t = rows8.reshape((N_DEV,) + tuple(lead) + (c,))
    return jnp.moveaxis(t, 0, -2).reshape(tuple(lead) + (N_DEV * c,))


def kernel(x, meta_tokens, ffn1_norm, ffn1_w_gate_up, ffn1_w_down, mix_norm, w_in, fox_forget_bias, ssd_conv_w, ssd_conv_b, ssd_dt_bias, ssd_a_log, ssd_d, ssd_norm, lru_conv_w, lru_conv_b, lru_w_a, lru_b_a, lru_w_x, lru_b_x, lru_lambda, w_branch_attn, w_branch_ssd, w_branch_lru, w_out, ffn2_norm, ffn2_w_gate_up, ffn2_w_down, final_norm, loss_target, m_meta_tokens, m_ffn1_norm, m_ffn1_w_gate_up, m_ffn1_w_down, m_mix_norm, m_w_in, m_fox_forget_bias, m_ssd_conv_w, m_ssd_conv_b, m_ssd_dt_bias, m_ssd_a_log, m_ssd_d, m_ssd_norm, m_lru_conv_w, m_lru_conv_b, m_lru_w_a, m_lru_b_a, m_lru_w_x, m_lru_b_x, m_lru_lambda, m_w_branch_attn, m_w_branch_ssd, m_w_branch_lru, m_w_out, m_ffn2_norm, m_ffn2_w_gate_up, m_ffn2_w_down, m_final_norm, v_meta_tokens, v_ffn1_norm, v_ffn1_w_gate_up, v_ffn1_w_down, v_mix_norm, v_w_in, v_fox_forget_bias, v_ssd_conv_w, v_ssd_conv_b, v_ssd_dt_bias, v_ssd_a_log, v_ssd_d, v_ssd_norm, v_lru_conv_w, v_lru_conv_b, v_lru_w_a, v_lru_b_a, v_lru_w_x, v_lru_b_x, v_lru_lambda, v_w_branch_attn, v_w_branch_ssd, v_w_branch_lru, v_w_out, v_ffn2_norm, v_ffn2_w_gate_up, v_ffn2_w_down, v_final_norm):
    weights = dict(zip(WEIGHT_NAMES, (meta_tokens, ffn1_norm, ffn1_w_gate_up, ffn1_w_down, mix_norm, w_in, fox_forget_bias, ssd_conv_w, ssd_conv_b, ssd_dt_bias, ssd_a_log, ssd_d, ssd_norm, lru_conv_w, lru_conv_b, lru_w_a, lru_b_a, lru_w_x, lru_b_x, lru_lambda, w_branch_attn, w_branch_ssd, w_branch_lru, w_out, ffn2_norm, ffn2_w_gate_up, ffn2_w_down, final_norm,)))
    mom1 = dict(zip(WEIGHT_NAMES, (m_meta_tokens, m_ffn1_norm, m_ffn1_w_gate_up, m_ffn1_w_down, m_mix_norm, m_w_in, m_fox_forget_bias, m_ssd_conv_w, m_ssd_conv_b, m_ssd_dt_bias, m_ssd_a_log, m_ssd_d, m_ssd_norm, m_lru_conv_w, m_lru_conv_b, m_lru_w_a, m_lru_b_a, m_lru_w_x, m_lru_b_x, m_lru_lambda, m_w_branch_attn, m_w_branch_ssd, m_w_branch_lru, m_w_out, m_ffn2_norm, m_ffn2_w_gate_up, m_ffn2_w_down, m_final_norm,)))
    mom2 = dict(zip(WEIGHT_NAMES, (v_meta_tokens, v_ffn1_norm, v_ffn1_w_gate_up, v_ffn1_w_down, v_mix_norm, v_w_in, v_fox_forget_bias, v_ssd_conv_w, v_ssd_conv_b, v_ssd_dt_bias, v_ssd_a_log, v_ssd_d, v_ssd_norm, v_lru_conv_w, v_lru_conv_b, v_lru_w_a, v_lru_b_a, v_lru_w_x, v_lru_b_x, v_lru_lambda, v_w_branch_attn, v_w_branch_ssd, v_w_branch_lru, v_w_out, v_ffn2_norm, v_ffn2_w_gate_up, v_ffn2_w_down, v_final_norm,)))
    depth = ffn1_norm.shape[0]
    D = x.shape[-1]
    my_idx = 4 * lax.axis_index("x") + 2 * lax.axis_index("y") + lax.axis_index("c")

    small_shapes = [weights[n].shape for n in SMALL_SHARDED]
    gathered = _all_gather(_flatten_list([weights[n] for n in SMALL_SHARDED]), "gather_small").reshape(N_DEV, -1)
    small_full, o = {}, 0
    for n, s in zip(SMALL_SHARDED, small_shapes):
        k = math.prod(s)
        small_full[n] = _gather_last(gathered[:, o:o + k], s)
        o += k

    shard_shapes = {n: weights[n].shape[1:] for n in BIG_NAMES}
    pack_next = lambda l: _pack_shards({n: weights[n][l].astype(BF16) for n in BIG_NAMES})

    def layer_weights(l, gathered):
        if gathered is None:
            gathered = _all_gather(pack_next(l), "gather_weights")
        w = _unpack_gathered(gathered, shard_shapes, D)
        for n in SMALL_NAMES:
            w[n] = small_full[n][l] if n in SMALL_SHARDED else weights[n][l]
        return w

    pack_grads = lambda g: _pack_full_grads(g, shard_shapes, D).astype(BF16)
    loss, dx, dmeta, grads, dfinal, received, parts = _local_step(
        x, loss_target, small_full["meta_tokens"], final_norm, depth, layer_weights, pack_next, pack_grads)
    received[0] = _exchange(parts, "exchange_grads")
    loss = lax.psum(loss[0, 0], ("x", "y", "c"))
    summed = {n: [] for n in WEIGHT_NAMES}
    for l in range(depth):
        local = _unpack_local(_sum8(received[l], "sum_grads"), shard_shapes)
        for n in BIG_NAMES:
            summed[n].append(local[n])

    small_list = [dmeta, dfinal] + [grads[l][n] for l in range(depth) for n in SMALL_NAMES]
    total = _sum8(_all_gather(_flatten_list(small_list), "gather_small_grads"), "sum_small_grads")
    parts = _split_like(total, [a.shape for a in small_list])
    full_small = {"meta_tokens": parts[0], "final_norm": parts[1]}
    for i, n in enumerate(SMALL_NAMES):
        full_small[n] = jnp.stack([parts[2 + l * len(SMALL_NAMES) + i] for l in range(depth)])
    grad = {}
    for n in WEIGHT_NAMES:
        if n in BIG_NAMES:
            grad[n] = jnp.stack(summed[n])
        elif n in SMALL_SHARDED:
            c = weights[n].shape[-1]
            grad[n] = lax.dynamic_slice_in_dim(full_small[n], my_idx * c, c, axis=full_small[n].ndim - 1)
        else:
            grad[n] = full_small[n]

    delta, new_m, new_v = {}, {}, {}
    for n in WEIGHT_NAMES:
        delta[n], new_m[n], new_v[n] = _adamw(weights[n], grad[n], mom1[n], mom2[n], "adamw_" + n)
    return (loss, dx, *[grad[n] for n in WEIGHT_NAMES], *[delta[n] for n in WEIGHT_NAMES],
            *[new_m[n] for n in WEIGHT_NAMES], *[new_v[n] for n in WEIGHT_NAMES])
```
